```python
import jax, jax.numpy as jnp
from jax import lax
import numpy as np

D_MODEL = 1024
BATCH = 8
SEQ = 8192
DEPTH = 4

A_WIDTH = D_MODEL // 2
A_GROUPS = 8
A_GROUP_DIM = A_WIDTH // A_GROUPS
CHUNK = 128
B_WIDTH = D_MODEL // 2
CONV_WIDTH = 31
AB_IN = 2 * A_WIDTH + 2 * B_WIDTH
AB_OUT = A_WIDTH + B_WIDTH
HEAD_DIM = 64
C_HEADS = D_MODEL // HEAD_DIM
C_WIDTH = C_HEADS * HEAD_DIM
DILATED_PATTERNS = ((128, 1), (512, 4), (2048, 16))
ROT_DIM = HEAD_DIM // 4
ROPE_THETA = 500000.0
D_FF = 4 * D_MODEL
EPS = 1e-6
NEG = -1e30
N_EVEN = (DEPTH + 1) // 2
N_ODD = DEPTH // 2

kernel_name = "hybrid_gmlp_conv_dilated_attn_encoder"


def rmsnorm(t, g):
    tf = t.astype(jnp.float32)
    y = tf * lax.rsqrt(jnp.mean(tf * tf, axis=-1, keepdims=True) + EPS)
    return (y * g.astype(jnp.float32)).astype(t.dtype)


def layernorm(t, g, b):
    tf = t.astype(jnp.float32)
    mu = jnp.mean(tf, axis=-1, keepdims=True)
    var = jnp.mean(jnp.square(tf - mu), axis=-1, keepdims=True)
    y = (tf - mu) * lax.rsqrt(var + EPS)
    return (y * g.astype(jnp.float32) + b.astype(jnp.float32)).astype(t.dtype)


def rope_tables(seq):
    pos = jnp.arange(seq, dtype=jnp.float32)
    inv_freq = ROPE_THETA ** (-jnp.arange(0, ROT_DIM, 2, dtype=jnp.float32) / ROT_DIM)
    ang = pos[:, None] * inv_freq[None, :]
    return jnp.cos(ang), jnp.sin(ang)


def partial_rope(t, cos, sin):
    half = ROT_DIM // 2
    t1 = t[..., :half].astype(jnp.float32)
    t2 = t[..., half:ROT_DIM].astype(jnp.float32)
    c, s = cos[:, None, :], sin[:, None, :]
    rot = jnp.concatenate([t1 * c - t2 * s, t2 * c + t1 * s], axis=-1).astype(t.dtype)
    return jnp.concatenate([rot, t[..., ROT_DIM:]], axis=-1)


def dilated_band_attention(q, k, v, window, dilation):
    B, S, H, hd = q.shape
    half = window // (2 * dilation)
    blk = half
    L = S // dilation
    nb = -(-L // blk)
    Lp = nb * blk

    def to_strided(t):
        return t.reshape(B, L, dilation, H, hd).transpose(0, 2, 3, 1, 4)

    qs, ks, vs = to_strided(q), to_strided(k), to_strided(v)
    qb = jnp.pad(qs, ((0, 0),) * 3 + ((0, Lp - L), (0, 0))).reshape(B, dilation, H, nb, blk, hd)
    pad_kv = ((0, 0),) * 3 + ((blk, Lp - L + blk), (0, 0))
    ks, vs = jnp.pad(ks, pad_kv), jnp.pad(vs, pad_kv)

    def key_blocks(t):
        return jnp.concatenate(
            [t[..., o * blk:o * blk + Lp, :].reshape(B, dilation, H, nb, blk, hd) for o in range(3)],
            axis=-2)

    kb, vb = key_blocks(ks), key_blocks(vs)
    qi = jnp.arange(nb)[:, None, None] * blk + jnp.arange(blk)[None, :, None]
    kj = jnp.arange(nb)[:, None, None] * blk + jnp.arange(3 * blk)[None, None, :] - blk
    mask = (jnp.abs(kj - qi) <= half) & (kj >= 0) & (kj < L)

    s = jnp.einsum('brhnqd,brhnkd->brhnqk', qb.astype(jnp.float32), kb.astype(jnp.float32)) * (hd ** -0.5)
    s = jnp.where(mask, s, NEG)
    m = jnp.max(s, axis=-1, keepdims=True)
    p = jnp.exp(s - m)
    denom = jnp.sum(p, axis=-1, keepdims=True)
    o = jnp.einsum('brhnqk,brhnkd->brhnqd', p, vb.astype(jnp.float32)) / denom
    lse = (m + jnp.log(denom))[..., 0]
    o = o.reshape(B, dilation, H, Lp, hd)[..., :L, :].transpose(0, 3, 1, 2, 4).reshape(B, S, H, hd)
    lse = lse.reshape(B, dilation, H, Lp)[..., :L].transpose(0, 3, 1, 2).reshape(B, S, H)
    return o, lse


def mixer_ab(h, w_in, sp_w, sp_b, v_g, v_b, conv_w, conv_b, cn_g, cn_b, w_out):
    B, S, _ = h.shape
    z = h @ w_in
    za = jax.nn.gelu(z[..., :2 * A_WIDTH])
    u, v = za[..., :A_WIDTH], za[..., A_WIDTH:]
    v = layernorm(v, v_g, v_b)
    vc = v.reshape(B, S // CHUNK, CHUNK, A_GROUPS, A_GROUP_DIM)
    sv = jnp.einsum('gpq,bnqgc->bnpgc', sp_w, vc) + sp_b.T[:, :, None]
    ya = u * sv.reshape(B, S, A_WIDTH)
    zb = z[..., 2 * A_WIDTH:]
    g = zb[..., :B_WIDTH] * jax.nn.sigmoid(zb[..., B_WIDTH:])
    g = lax.conv_general_dilated(
        g, conv_w[:, None, :].astype(g.dtype), window_strides=(1,),
        padding=((CONV_WIDTH // 2, CONV_WIDTH // 2),),
        dimension_numbers=('NWC', 'WIO', 'NWC'), feature_group_count=B_WIDTH) + conv_b
    yb = jax.nn.silu(layernorm(g, cn_g, cn_b))
    return jnp.concatenate([ya, yb], axis=-1) @ w_out


def mixer_c(h, w_qkv, q_g, k_g, w_out, cos, sin):
    B, S, _ = h.shape
    qkv = (h @ w_qkv).reshape(B, S, 3, C_HEADS, HEAD_DIM)
    q = partial_rope(rmsnorm(qkv[:, :, 0], q_g), cos, sin)
    k = partial_rope(rmsnorm(qkv[:, :, 1], k_g), cos, sin)
    v = qkv[:, :, 2]
    outs, lses = [], []
    for window, dilation in DILATED_PATTERNS:
        o, l = dilated_band_attention(q, k, v, window, dilation)
        outs.append(o)
        lses.append(l)
    wts = jax.nn.softmax(jnp.stack(lses), axis=0)
    o = jnp.einsum('pbsh,pbshd->bshd', wts, jnp.stack(outs))
    return o.reshape(B, S, C_WIDTH).astype(h.dtype) @ w_out


def squared_relu_mlp(h, w1, w2):
    return jnp.square(jax.nn.relu(h @ w1)) @ w2


def _fwd_setup_inputs(seed: int = 0) -> dict:
    key = jax.random.key(seed)
    ks = jax.random.split(key, 20)

    def nrm(k, shape, scale):
        return jax.random.normal(k, shape, jnp.float32) * scale

    res = (2 * DEPTH) ** -0.5
    return {
        "x": nrm(ks[0], (BATCH, SEQ, D_MODEL), 1.0),
        "mix_norm_g": 1.0 + nrm(ks[1], (DEPTH, D_MODEL), 0.02),
        "mlp_norm_g": 1.0 + nrm(ks[2], (DEPTH, D_MODEL), 0.02),
        "mlp_w1": nrm(ks[3], (DEPTH, D_MODEL, D_FF), D_MODEL ** -0.5),
        "mlp_w2": nrm(ks[4], (DEPTH, D_FF, D_MODEL), D_FF ** -0.5 * res),
        "ab_w_in": nrm(ks[5], (N_EVEN, D_MODEL, AB_IN), D_MODEL ** -0.5),
        "a_spatial_w": nrm(ks[6], (N_EVEN, A_GROUPS, CHUNK, CHUNK), 0.5 * CHUNK ** -0.5),
        "a_spatial_b": 1.0 + nrm(ks[7], (N_EVEN, A_GROUPS, CHUNK), 0.02),
        "a_vnorm_g": 1.0 + nrm(ks[8], (N_EVEN, A_WIDTH), 0.02),
        "a_vnorm_b": nrm(ks[9], (N_EVEN, A_WIDTH), 0.02),
        "b_conv_w": nrm(ks[10], (N_EVEN, CONV_WIDTH, B_WIDTH), CONV_WIDTH ** -0.5),
        "b_conv_b": nrm(ks[11], (N_EVEN, B_WIDTH), 0.02),
        "b_norm_g": 1.0 + nrm(ks[12], (N_EVEN, B_WIDTH), 0.02),
        "b_norm_b": nrm(ks[13], (N_EVEN, B_WIDTH), 0.02),
        "ab_w_out": nrm(ks[14], (N_EVEN, AB_OUT, D_MODEL), AB_OUT ** -0.5 * res),
        "c_w_qkv": nrm(ks[15], (N_ODD, D_MODEL, 3 * C_WIDTH), D_MODEL ** -0.5),
        "c_q_norm_g": 1.0 + nrm(ks[16], (N_ODD, HEAD_DIM), 0.02),
        "c_k_norm_g": 1.0 + nrm(ks[17], (N_ODD, HEAD_DIM), 0.02),
        "c_w_out": nrm(ks[18], (N_ODD, C_WIDTH, D_MODEL), C_WIDTH ** -0.5 * res),
    }


def _fwd_reference(x, mix_norm_g, mlp_norm_g, mlp_w1, mlp_w2, ab_w_in, a_spatial_w, a_spatial_b,
              a_vnorm_g, a_vnorm_b, b_conv_w, b_conv_b, b_norm_g, b_norm_b, ab_w_out,
              c_w_qkv, c_q_norm_g, c_k_norm_g, c_w_out):
    cos, sin = rope_tables(x.shape[1])
    for layer in range(DEPTH):
        i = layer // 2
        h = rmsnorm(x, mix_norm_g[layer])
        if layer % 2 == 0:
            x = x + mixer_ab(h, ab_w_in[i], a_spatial_w[i], a_spatial_b[i], a_vnorm_g[i], a_vnorm_b[i],
                             b_conv_w[i], b_conv_b[i], b_norm_g[i], b_norm_b[i], ab_w_out[i])
        else:
            x = x + mixer_c(h, c_w_qkv[i], c_q_norm_g[i], c_k_norm_g[i], c_w_out[i], cos, sin)
        h = rmsnorm(x, mlp_norm_g[layer])
        x = x + squared_relu_mlp(h, mlp_w1[layer], mlp_w2[layer])
    return x


import jax as _jax
import jax.numpy as _jnp

TWIN_FORMAT = 'train_step'
FWD_PARAMS = ['x', 'mix_norm_g', 'mlp_norm_g', 'mlp_w1', 'mlp_w2', 'ab_w_in', 'a_spatial_w', 'a_spatial_b', 'a_vnorm_g', 'a_vnorm_b', 'b_conv_w', 'b_conv_b', 'b_norm_g', 'b_norm_b', 'ab_w_out', 'c_w_qkv', 'c_q_norm_g', 'c_k_norm_g', 'c_w_out']
TWIN_WEIGHTS = ['mix_norm_g', 'mlp_norm_g', 'mlp_w1', 'mlp_w2', 'ab_w_in', 'a_spatial_w', 'a_spatial_b', 'a_vnorm_g', 'a_vnorm_b', 'b_conv_w', 'b_conv_b', 'b_norm_g', 'b_norm_b', 'ab_w_out', 'c_w_qkv', 'c_q_norm_g', 'c_k_norm_g', 'c_w_out']
TWIN_DIFF_INPUT = 'x'
TWIN_INPUTS = ['x', 'mix_norm_g', 'mlp_norm_g', 'mlp_w1', 'mlp_w2', 'ab_w_in', 'a_spatial_w', 'a_spatial_b', 'a_vnorm_g', 'a_vnorm_b', 'b_conv_w', 'b_conv_b', 'b_norm_g', 'b_norm_b', 'ab_w_out', 'c_w_qkv', 'c_q_norm_g', 'c_k_norm_g', 'c_w_out', 'loss_target', 'm_mix_norm_g', 'm_mlp_norm_g', 'm_mlp_w1', 'm_mlp_w2', 'm_ab_w_in', 'm_a_spatial_w', 'm_a_spatial_b', 'm_a_vnorm_g', 'm_a_vnorm_b', 'm_b_conv_w', 'm_b_conv_b', 'm_b_norm_g', 'm_b_norm_b', 'm_ab_w_out', 'm_c_w_qkv', 'm_c_q_norm_g', 'm_c_k_norm_g', 'm_c_w_out', 'v_mix_norm_g', 'v_mlp_norm_g', 'v_mlp_w1', 'v_mlp_w2', 'v_ab_w_in', 'v_a_spatial_w', 'v_a_spatial_b', 'v_a_vnorm_g', 'v_a_vnorm_b', 'v_b_conv_w', 'v_b_conv_b', 'v_b_norm_g', 'v_b_norm_b', 'v_ab_w_out', 'v_c_w_qkv', 'v_c_q_norm_g', 'v_c_k_norm_g', 'v_c_w_out']
TWIN_OUTPUTS = ['loss', 'grad_x', 'grad_mix_norm_g', 'grad_mlp_norm_g', 'grad_mlp_w1', 'grad_mlp_w2', 'grad_ab_w_in', 'grad_a_spatial_w', 'grad_a_spatial_b', 'grad_a_vnorm_g', 'grad_a_vnorm_b', 'grad_b_conv_w', 'grad_b_conv_b', 'grad_b_norm_g', 'grad_b_norm_b', 'grad_ab_w_out', 'grad_c_w_qkv', 'grad_c_q_norm_g', 'grad_c_k_norm_g', 'grad_c_w_out', 'delta_mix_norm_g', 'delta_mlp_norm_g', 'delta_mlp_w1', 'delta_mlp_w2', 'delta_ab_w_in', 'delta_a_spatial_w', 'delta_a_spatial_b', 'delta_a_vnorm_g', 'delta_a_vnorm_b', 'delta_b_conv_w', 'delta_b_conv_b', 'delta_b_norm_g', 'delta_b_norm_b', 'delta_ab_w_out', 'delta_c_w_qkv', 'delta_c_q_norm_g', 'delta_c_k_norm_g', 'delta_c_w_out', 'new_m_mix_norm_g', 'new_m_mlp_norm_g', 'new_m_mlp_w1', 'new_m_mlp_w2', 'new_m_ab_w_in', 'new_m_a_spatial_w', 'new_m_a_spatial_b', 'new_m_a_vnorm_g', 'new_m_a_vnorm_b', 'new_m_b_conv_w', 'new_m_b_conv_b', 'new_m_b_norm_g', 'new_m_b_norm_b', 'new_m_ab_w_out', 'new_m_c_w_qkv', 'new_m_c_q_norm_g', 'new_m_c_k_norm_g', 'new_m_c_w_out', 'new_v_mix_norm_g', 'new_v_mlp_norm_g', 'new_v_mlp_w1', 'new_v_mlp_w2', 'new_v_ab_w_in', 'new_v_a_spatial_w', 'new_v_a_spatial_b', 'new_v_a_vnorm_g', 'new_v_a_vnorm_b', 'new_v_b_conv_w', 'new_v_b_conv_b', 'new_v_b_norm_g', 'new_v_b_norm_b', 'new_v_ab_w_out', 'new_v_c_w_qkv', 'new_v_c_q_norm_g', 'new_v_c_k_norm_g', 'new_v_c_w_out']
TWIN_LEAF_KINDS = {'loss': 'loss', 'grad_x': 'grad_x', 'grad_mix_norm_g': 'grad_w', 'grad_mlp_norm_g': 'grad_w', 'grad_mlp_w1': 'grad_w', 'grad_mlp_w2': 'grad_w', 'grad_ab_w_in': 'grad_w', 'grad_a_spatial_w': 'grad_w', 'grad_a_spatial_b': 'grad_w', 'grad_a_vnorm_g': 'grad_w', 'grad_a_vnorm_b': 'grad_w', 'grad_b_conv_w': 'grad_w', 'grad_b_conv_b': 'grad_w', 'grad_b_norm_g': 'grad_w', 'grad_b_norm_b': 'grad_w', 'grad_ab_w_out': 'grad_w', 'grad_c_w_qkv': 'grad_w', 'grad_c_q_norm_g': 'grad_w', 'grad_c_k_norm_g': 'grad_w', 'grad_c_w_out': 'grad_w', 'delta_mix_norm_g': 'delta_w', 'delta_mlp_norm_g': 'delta_w', 'delta_mlp_w1': 'delta_w', 'delta_mlp_w2': 'delta_w', 'delta_ab_w_in': 'delta_w', 'delta_a_spatial_w': 'delta_w', 'delta_a_spatial_b': 'delta_w', 'delta_a_vnorm_g': 'delta_w', 'delta_a_vnorm_b': 'delta_w', 'delta_b_conv_w': 'delta_w', 'delta_b_conv_b': 'delta_w', 'delta_b_norm_g': 'delta_w', 'delta_b_norm_b': 'delta_w', 'delta_ab_w_out': 'delta_w', 'delta_c_w_qkv': 'delta_w', 'delta_c_q_norm_g': 'delta_w', 'delta_c_k_norm_g': 'delta_w', 'delta_c_w_out': 'delta_w', 'new_m_mix_norm_g': 'new_m', 'new_m_mlp_norm_g': 'new_m', 'new_m_mlp_w1': 'new_m', 'new_m_mlp_w2': 'new_m', 'new_m_ab_w_in': 'new_m', 'new_m_a_spatial_w': 'new_m', 'new_m_a_spatial_b': 'new_m', 'new_m_a_vnorm_g': 'new_m', 'new_m_a_vnorm_b': 'new_m', 'new_m_b_conv_w': 'new_m', 'new_m_b_conv_b': 'new_m', 'new_m_b_norm_g': 'new_m', 'new_m_b_norm_b': 'new_m', 'new_m_ab_w_out': 'new_m', 'new_m_c_w_qkv': 'new_m', 'new_m_c_q_norm_g': 'new_m', 'new_m_c_k_norm_g': 'new_m', 'new_m_c_w_out': 'new_m', 'new_v_mix_norm_g': 'new_v', 'new_v_mlp_norm_g': 'new_v', 'new_v_mlp_w1': 'new_v', 'new_v_mlp_w2': 'new_v', 'new_v_ab_w_in': 'new_v', 'new_v_a_spatial_w': 'new_v', 'new_v_a_spatial_b': 'new_v', 'new_v_a_vnorm_g': 'new_v', 'new_v_a_vnorm_b': 'new_v', 'new_v_b_conv_w': 'new_v', 'new_v_b_conv_b': 'new_v', 'new_v_b_norm_g': 'new_v', 'new_v_b_norm_b': 'new_v', 'new_v_ab_w_out': 'new_v', 'new_v_c_w_qkv': 'new_v', 'new_v_c_q_norm_g': 'new_v', 'new_v_c_k_norm_g': 'new_v', 'new_v_c_w_out': 'new_v'}


def _forward(args):
    return _fwd_reference(*[args[k] for k in FWD_PARAMS])


def _output_shape():
    def fwd():
        inp = _fwd_setup_inputs(0)
        return _fwd_reference(*[inp[k] for k in FWD_PARAMS])
    out = _jax.eval_shape(fwd)
    return out.shape, out.dtype

N_MICROBATCH = 1
ADAM_LR = 0.001
ADAM_B1 = 0.9
ADAM_B2 = 0.999
ADAM_EPS = 1e-08
ADAM_WD = 0.01
ADAM_STEP = 10
PER_EXAMPLE_BATCH_AXIS = {'x': 0, 'loss_target': 0}
SHARED_INPUTS = []
_WEIGHT_DTYPES = {'mix_norm_g': _jnp.float32, 'mlp_norm_g': _jnp.float32, 'mlp_w1': _jnp.float32, 'mlp_w2': _jnp.float32, 'ab_w_in': _jnp.float32, 'a_spatial_w': _jnp.float32, 'a_spatial_b': _jnp.float32, 'a_vnorm_g': _jnp.float32, 'a_vnorm_b': _jnp.float32, 'b_conv_w': _jnp.float32, 'b_conv_b': _jnp.float32, 'b_norm_g': _jnp.float32, 'b_norm_b': _jnp.float32, 'ab_w_out': _jnp.float32, 'c_w_qkv': _jnp.float32, 'c_q_norm_g': _jnp.float32, 'c_k_norm_g': _jnp.float32, 'c_w_out': _jnp.float32}
MOMENT_SCALE = {'mix_norm_g': 2.438649e+00, 'mlp_norm_g': 2.392703e+01, 'mlp_w1': 1.079081e+00, 'mlp_w2': 1.522588e+01, 'ab_w_in': 6.666252e-01, 'a_spatial_w': 1.079845e-01, 'a_spatial_b': 1.707652e+00, 'a_vnorm_g': 8.530445e-01, 'a_vnorm_b': 1.597024e-01, 'b_conv_w': 1.040664e+00, 'b_conv_b': 1.084662e+01, 'b_norm_g': 5.721555e+00, 'b_norm_b': 6.333708e+00, 'ab_w_out': 8.234594e+00, 'c_w_qkv': 1.294629e+00, 'c_q_norm_g': 7.439055e-01, 'c_k_norm_g': 7.463131e-01, 'c_w_out': 6.385250e+00}


def _to_microbatches(a, axis):
    t = _jnp.moveaxis(a, axis, 0)
    t = t.reshape((N_MICROBATCH, t.shape[0] // N_MICROBATCH) + t.shape[1:])
    return _jnp.moveaxis(t, 1, axis + 1)


def setup_inputs(seed: int = 0) -> dict:
    inp = _fwd_setup_inputs(seed)
    key = _jax.random.fold_in(_jax.random.key(seed), 7919)
    shape, _ = _output_shape()
    out = dict(inp)
    out["loss_target"] = _jax.random.normal(_jax.random.fold_in(key, 0), shape, _jnp.float32)
    for i, name in enumerate(TWIN_WEIGHTS):
        w = inp[name].astype(_jnp.float32)
        if MOMENT_SCALE is None:
            s = _jnp.sqrt(_jnp.mean(_jnp.square(w)) + 1e-30)
        else:
            s = MOMENT_SCALE[name]
        km, kv = _jax.random.split(_jax.random.fold_in(key, i + 1))
        out[name] = w
        out["m_" + name] = s * _jax.random.normal(km, w.shape, _jnp.float32)
        out["v_" + name] = (s * s) * _jax.random.uniform(kv, w.shape, _jnp.float32, 0.5, 1.5)
    if N_MICROBATCH > 1:
        for name, axis in PER_EXAMPLE_BATCH_AXIS.items():
            out[name] = _to_microbatches(out[name], axis)
    return {'x': out['x'], 'mix_norm_g': out['mix_norm_g'], 'mlp_norm_g': out['mlp_norm_g'], 'mlp_w1': out['mlp_w1'], 'mlp_w2': out['mlp_w2'], 'ab_w_in': out['ab_w_in'], 'a_spatial_w': out['a_spatial_w'], 'a_spatial_b': out['a_spatial_b'], 'a_vnorm_g': out['a_vnorm_g'], 'a_vnorm_b': out['a_vnorm_b'], 'b_conv_w': out['b_conv_w'], 'b_conv_b': out['b_conv_b'], 'b_norm_g': out['b_norm_g'], 'b_norm_b': out['b_norm_b'], 'ab_w_out': out['ab_w_out'], 'c_w_qkv': out['c_w_qkv'], 'c_q_norm_g': out['c_q_norm_g'], 'c_k_norm_g': out['c_k_norm_g'], 'c_w_out': out['c_w_out'], 'loss_target': out['loss_target'], 'm_mix_norm_g': out['m_mix_norm_g'], 'm_mlp_norm_g': out['m_mlp_norm_g'], 'm_mlp_w1': out['m_mlp_w1'], 'm_mlp_w2': out['m_mlp_w2'], 'm_ab_w_in': out['m_ab_w_in'], 'm_a_spatial_w': out['m_a_spatial_w'], 'm_a_spatial_b': out['m_a_spatial_b'], 'm_a_vnorm_g': out['m_a_vnorm_g'], 'm_a_vnorm_b': out['m_a_vnorm_b'], 'm_b_conv_w': out['m_b_conv_w'], 'm_b_conv_b': out['m_b_conv_b'], 'm_b_norm_g': out['m_b_norm_g'], 'm_b_norm_b': out['m_b_norm_b'], 'm_ab_w_out': out['m_ab_w_out'], 'm_c_w_qkv': out['m_c_w_qkv'], 'm_c_q_norm_g': out['m_c_q_norm_g'], 'm_c_k_norm_g': out['m_c_k_norm_g'], 'm_c_w_out': out['m_c_w_out'], 'v_mix_norm_g': out['v_mix_norm_g'], 'v_mlp_norm_g': out['v_mlp_norm_g'], 'v_mlp_w1': out['v_mlp_w1'], 'v_mlp_w2': out['v_mlp_w2'], 'v_ab_w_in': out['v_ab_w_in'], 'v_a_spatial_w': out['v_a_spatial_w'], 'v_a_spatial_b': out['v_a_spatial_b'], 'v_a_vnorm_g': out['v_a_vnorm_g'], 'v_a_vnorm_b': out['v_a_vnorm_b'], 'v_b_conv_w': out['v_b_conv_w'], 'v_b_conv_b': out['v_b_conv_b'], 'v_b_norm_g': out['v_b_norm_g'], 'v_b_norm_b': out['v_b_norm_b'], 'v_ab_w_out': out['v_ab_w_out'], 'v_c_w_qkv': out['v_c_w_qkv'], 'v_c_q_norm_g': out['v_c_q_norm_g'], 'v_c_k_norm_g': out['v_c_k_norm_g'], 'v_c_w_out': out['v_c_w_out']}


def _loss(weights, diff, rest, loss_target):
    with _jax.named_scope("forward"):
        args = {**rest, TWIN_DIFF_INPUT: diff, **{k: w.astype(_WEIGHT_DTYPES[k]) for k, w in weights.items()}}
        y = _forward(args)
    with _jax.named_scope("loss_head"):
        err = _jnp.square(y.astype(_jnp.float32) - loss_target)
        return 0.5 * _jnp.sum(_jnp.mean(err, axis=-1)) if err.ndim else 0.5 * err


def _adamw(w, g, m, v):
    m = ADAM_B1 * m + (1.0 - ADAM_B1) * g
    v = ADAM_B2 * v + (1.0 - ADAM_B2) * _jnp.square(g)
    m_hat = m / (1.0 - ADAM_B1 ** ADAM_STEP)
    v_hat = v / (1.0 - ADAM_B2 ** ADAM_STEP)
    delta = -ADAM_LR * (m_hat / (_jnp.sqrt(v_hat) + ADAM_EPS) + ADAM_WD * w)
    return delta, m, v


def reference(x, mix_norm_g, mlp_norm_g, mlp_w1, mlp_w2, ab_w_in, a_spatial_w, a_spatial_b, a_vnorm_g, a_vnorm_b, b_conv_w, b_conv_b, b_norm_g, b_norm_b, ab_w_out, c_w_qkv, c_q_norm_g, c_k_norm_g, c_w_out, loss_target, m_mix_norm_g, m_mlp_norm_g, m_mlp_w1, m_mlp_w2, m_ab_w_in, m_a_spatial_w, m_a_spatial_b, m_a_vnorm_g, m_a_vnorm_b, m_b_conv_w, m_b_conv_b, m_b_norm_g, m_b_norm_b, m_ab_w_out, m_c_w_qkv, m_c_q_norm_g, m_c_k_norm_g, m_c_w_out, v_mix_norm_g, v_mlp_norm_g, v_mlp_w1, v_mlp_w2, v_ab_w_in, v_a_spatial_w, v_a_spatial_b, v_a_vnorm_g, v_a_vnorm_b, v_b_conv_w, v_b_conv_b, v_b_norm_g, v_b_norm_b, v_ab_w_out, v_c_w_qkv, v_c_q_norm_g, v_c_k_norm_g, v_c_w_out):
    given = dict(x=x, mix_norm_g=mix_norm_g, mlp_norm_g=mlp_norm_g, mlp_w1=mlp_w1, mlp_w2=mlp_w2, ab_w_in=ab_w_in, a_spatial_w=a_spatial_w, a_spatial_b=a_spatial_b, a_vnorm_g=a_vnorm_g, a_vnorm_b=a_vnorm_b, b_conv_w=b_conv_w, b_conv_b=b_conv_b, b_norm_g=b_norm_g, b_norm_b=b_norm_b, ab_w_out=ab_w_out, c_w_qkv=c_w_qkv, c_q_norm_g=c_q_norm_g, c_k_norm_g=c_k_norm_g, c_w_out=c_w_out, loss_target=loss_target, m_mix_norm_g=m_mix_norm_g, m_mlp_norm_g=m_mlp_norm_g, m_mlp_w1=m_mlp_w1, m_mlp_w2=m_mlp_w2, m_ab_w_in=m_ab_w_in, m_a_spatial_w=m_a_spatial_w, m_a_spatial_b=m_a_spatial_b, m_a_vnorm_g=m_a_vnorm_g, m_a_vnorm_b=m_a_vnorm_b, m_b_conv_w=m_b_conv_w, m_b_conv_b=m_b_conv_b, m_b_norm_g=m_b_norm_g, m_b_norm_b=m_b_norm_b, m_ab_w_out=m_ab_w_out, m_c_w_qkv=m_c_w_qkv, m_c_q_norm_g=m_c_q_norm_g, m_c_k_norm_g=m_c_k_norm_g, m_c_w_out=m_c_w_out, v_mix_norm_g=v_mix_norm_g, v_mlp_norm_g=v_mlp_norm_g, v_mlp_w1=v_mlp_w1, v_mlp_w2=v_mlp_w2, v_ab_w_in=v_ab_w_in, v_a_spatial_w=v_a_spatial_w, v_a_spatial_b=v_a_spatial_b, v_a_vnorm_g=v_a_vnorm_g, v_a_vnorm_b=v_a_vnorm_b, v_b_conv_w=v_b_conv_w, v_b_conv_b=v_b_conv_b, v_b_norm_g=v_b_norm_g, v_b_norm_b=v_b_norm_b, v_ab_w_out=v_ab_w_out, v_c_w_qkv=v_c_w_qkv, v_c_q_norm_g=v_c_q_norm_g, v_c_k_norm_g=v_c_k_norm_g, v_c_w_out=v_c_w_out)
    weights = {n: given[n] for n in TWIN_WEIGHTS}
    shared = {n: given[n] for n in SHARED_INPUTS}
    per_example = {n: given[n] for n in ['x']}
    grad_fn = _jax.value_and_grad(_loss, argnums=(0, 1))

    def one_microbatch(ex, loss_target):
        ex = dict(ex)
        diff = ex.pop(TWIN_DIFF_INPUT)
        return grad_fn(weights, diff, {**shared, **ex}, loss_target)

    if N_MICROBATCH == 1:
        loss, (grad_w, grad_x) = one_microbatch(per_example, given["loss_target"])
    else:
        def body(carry, xs):
            loss_sum, grad_sum = carry
            l_k, (gw_k, gx_k) = one_microbatch(xs[0], xs[1])
            with _jax.named_scope("update"):
                return (loss_sum + l_k, _jax.tree.map(_jnp.add, grad_sum, gw_k)), gx_k

        init = (_jnp.zeros((), _jnp.float32), _jax.tree.map(_jnp.zeros_like, weights))
        (loss, grad_w), grad_x = _jax.lax.scan(body, init, (per_example, given["loss_target"]))
    with _jax.named_scope("update"):
        delta_w, new_m, new_v = {}, {}, {}
        for n in TWIN_WEIGHTS:
            delta_w[n], new_m[n], new_v[n] = _adamw(weights[n], grad_w[n], given["m_" + n], given["v_" + n])
    return (loss, grad_x, *[grad_w[n] for n in TWIN_WEIGHTS], *[delta_w[n] for n in TWIN_WEIGHTS],
            *[new_m[n] for n in TWIN_WEIGHTS], *[new_v[n] for n in TWIN_WEIGHTS])
```

```python
import functools
import math

import jax
import jax.numpy as jnp
from jax import lax
from jax.experimental import pallas as pl
from jax.experimental.pallas import tpu as pltpu

F32, BF16 = jnp.float32, jnp.bfloat16
N_DEV = 8
EPS = 1e-6
NEG = -1e30
LANES = 128
HEAD_DIM = 64
CHUNK = 128
CONV_W = 31
CONV_HALO = 16
BAND = 64
DILATIONS = (1, 4, 16)
ROT_DIM = 16
ROPE_THETA = 500000.0
VMEM_LIMIT = 56 * 1024 * 1024
ADAM_LR, ADAM_B1, ADAM_B2, ADAM_EPS, ADAM_WD, ADAM_STEP = 0.001, 0.9, 0.999, 1e-08, 0.01, 10
MESH = pl.DeviceIdType.MESH


def _params(*sem):
    return pltpu.CompilerParams(dimension_semantics=sem, vmem_limit_bytes=VMEM_LIMIT)


def _dot(a, b):
    return jnp.dot(a, b, preferred_element_type=F32)


def _dot_nt(a, b):
    return lax.dot_general(a, b, (((1,), (1,)), ((), ())), preferred_element_type=F32)


def _dot_tn(a, b):
    return lax.dot_general(a, b, (((0,), (0,)), ((), ())), preferred_element_type=F32)


def _rms_r(x):
    return lax.rsqrt(jnp.mean(x * x, axis=-1, keepdims=True) + EPS)


def _sigmoid(x):
    return 1.0 / (1.0 + jnp.exp(-x))


_GK = math.sqrt(2.0 / math.pi)


def _gelu(x):
    return 0.5 * x * (1.0 + jnp.tanh(_GK * (x + 0.044715 * x * x * x)))


def _gelu_grad(x):
    t = jnp.tanh(_GK * (x + 0.044715 * x * x * x))
    return 0.5 * (1.0 + t) + 0.5 * x * (1.0 - t * t) * (_GK * (1.0 + 3.0 * 0.044715 * x * x))


def _seg_sum(x, bd):
    hi = x.astype(BF16)
    r1 = x - hi.astype(F32)
    mid = r1.astype(BF16)
    lo = (r1 - mid.astype(F32)).astype(BF16)
    return _dot(hi, bd) + _dot(mid, bd) + _dot(lo, bd)


def _block_diag(n):
    i = lax.broadcasted_iota(jnp.int32, (n, n), 0) // HEAD_DIM
    j = lax.broadcasted_iota(jnp.int32, (n, n), 1) // HEAD_DIM
    return jnp.where(i == j, 1.0, 0.0).astype(BF16)


def _tile(s, cap):
    t = min(s, cap)
    assert s % t == 0
    return t


def norm_matmul(x, g, wg, layer, name):
    s, d = x.shape
    ns = wg.shape[-1]
    tm = _tile(s, 1024)

    def body(x_ref, g_ref, w_ref, z_ref, h_ref):
        @pl.when(pl.program_id(1) == 0)
        def _():
            xv = x_ref[...]
            h_ref[...] = (xv * _rms_r(xv) * g_ref[...]).astype(BF16)
        z_ref[...] = _dot(h_ref[...], w_ref[...])

    return pl.pallas_call(
        body, name=name, grid=(s // tm, N_DEV),
        in_specs=[pl.BlockSpec((tm, d), lambda i, j: (i, 0)),
                  pl.BlockSpec((1, d), lambda i, j: (0, 0)),
                  pl.BlockSpec((None, None, d, ns), lambda i, j: (layer, j, 0, 0))],
        out_specs=[pl.BlockSpec((tm, ns), lambda i, j: (i, j)),
                   pl.BlockSpec((tm, d), lambda i, j: (i, 0))],
        out_shape=[jax.ShapeDtypeStruct((s, N_DEV * ns), F32), jax.ShapeDtypeStruct((s, d), BF16)],
        compiler_params=_params("parallel", "arbitrary"),
    )(x, g, wg)


def mlp_forward(x, g, w1g, w2g, layer, name):
    s, d = x.shape
    fs = w1g.shape[-1]
    tm = _tile(s, 1024)

    def body(x_ref, g_ref, w1_ref, w2_ref, xo_ref, a_ref, h_ref):
        @pl.when(pl.program_id(1) == 0)
        def _():
            xv = x_ref[...]
            h_ref[...] = (xv * _rms_r(xv) * g_ref[...]).astype(BF16)
            xo_ref[...] = xv
        a = _dot(h_ref[...], w1_ref[...])
        a_ref[...] = a.astype(BF16)
        r = jnp.maximum(a, 0.0)
        xo_ref[...] += _dot((r * r).astype(BF16), w2_ref[...])

    return pl.pallas_call(
        body, name=name, grid=(s // tm, N_DEV),
        in_specs=[pl.BlockSpec((tm, d), lambda i, j: (i, 0)),
                  pl.BlockSpec((1, d), lambda i, j: (0, 0)),
                  pl.BlockSpec((None, None, d, fs), lambda i, j: (layer, j, 0, 0)),
                  pl.BlockSpec((None, None, fs, d), lambda i, j: (layer, j, 0, 0))],
        out_specs=[pl.BlockSpec((tm, d), lambda i, j: (i, 0)),
                   pl.BlockSpec((tm, fs), lambda i, j: (i, j)),
                   pl.BlockSpec((tm, d), lambda i, j: (i, 0))],
        out_shape=[jax.ShapeDtypeStruct((s, d), F32), jax.ShapeDtypeStruct((s, N_DEV * fs), BF16),
                   jax.ShapeDtypeStruct((s, d), BF16)],
        compiler_params=_params("parallel", "arbitrary"),
    )(x, g, w1g, w2g)


def _norm_backward(dh, xv, g, dres):
    r = _rms_r(xv)
    xh = xv * r
    t = dh * g
    dx = dres + r * (t - xh * jnp.mean(t * xh, axis=-1, keepdims=True))
    return dx, jnp.sum(dh * xh, axis=0, keepdims=True)


def mlp_backward(dy, a, x, g, w1g, w2g, layer, name):
    s, d = x.shape
    fs = w1g.shape[-1]
    tm = _tile(s, 1024)

    def body(dy_ref, a_ref, x_ref, g_ref, w1_ref, w2_ref, dx_ref, da_ref, dg_ref, dyb_ref, dh_ref):
        i, j = pl.program_id(0), pl.program_id(1)

        @pl.when(j == 0)
        def _():
            dyb_ref[...] = dy_ref[...].astype(BF16)
            dh_ref[...] = jnp.zeros_like(dh_ref)

        dr = _dot_nt(dyb_ref[...], w2_ref[...])
        da = (dr * (2.0 * jnp.maximum(a_ref[...].astype(F32), 0.0))).astype(BF16)
        da_ref[...] = da
        dh_ref[...] += _dot_nt(da, w1_ref[...])

        @pl.when(j == N_DEV - 1)
        def _():
            dx, dgp = _norm_backward(dh_ref[...], x_ref[...], g_ref[...], dy_ref[...])
            dx_ref[...] = dx

            @pl.when(i == 0)
            def _():
                dg_ref[...] = dgp

            @pl.when(i > 0)
            def _():
                dg_ref[...] += dgp

    return pl.pallas_call(
        body, name=name, grid=(s // tm, N_DEV),
        in_specs=[pl.BlockSpec((tm, d), lambda i, j: (i, 0)),
                  pl.BlockSpec((tm, fs), lambda i, j: (i, j)),
                  pl.BlockSpec((tm, d), lambda i, j: (i, 0)),
                  pl.BlockSpec((1, d), lambda i, j: (0, 0)),
                  pl.BlockSpec((None, None, d, fs), lambda i, j: (layer, j, 0, 0)),
                  pl.BlockSpec((None, None, fs, d), lambda i, j: (layer, j, 0, 0))],
        out_specs=[pl.BlockSpec((tm, d), lambda i, j: (i, 0)),
                   pl.BlockSpec((tm, fs), lambda i, j: (i, j)),
                   pl.BlockSpec((1, d), lambda i, j: (0, 0))],
        out_shape=[jax.ShapeDtypeStruct((s, d), F32), jax.ShapeDtypeStruct((s, N_DEV * fs), BF16),
                   jax.ShapeDtypeStruct((1, d), F32)],
        scratch_shapes=[pltpu.VMEM((tm, d), BF16), pltpu.VMEM((tm, d), F32)],
        compiler_params=_params("arbitrary", "arbitrary"),
    )(dy, a, x, g, w1g, w2g)


def matmul_tn(a, b, shard, name, relu2=False):
    s, m = a.shape
    n = b.shape[1]
    ts = _tile(s, 1024)
    if shard == "cols":
        bm, bn = m, n // N_DEV
        a_map, b_map = (lambda j, k: (k, 0)), (lambda j, k: (k, j))
    else:
        bm, bn = m // N_DEV, n
        a_map, b_map = (lambda j, k: (k, j)), (lambda j, k: (k, 0))

    def body(a_ref, b_ref, o_ref):
        k = pl.program_id(1)
        av = a_ref[...]
        if relu2:
            af = jnp.maximum(av.astype(F32), 0.0)
            av = af * af
        p = _dot_tn(av.astype(BF16), b_ref[...].astype(BF16))

        @pl.when(k == 0)
        def _():
            o_ref[...] = p

        @pl.when(k > 0)
        def _():
            o_ref[...] += p

    return pl.pallas_call(
        body, name=name, grid=(N_DEV, s // ts),
        in_specs=[pl.BlockSpec((ts, bm), a_map), pl.BlockSpec((ts, bn), b_map)],
        out_specs=pl.BlockSpec((None, bm, bn), lambda j, k: (j, 0, 0)),
        out_shape=jax.ShapeDtypeStruct((N_DEV, bm, bn), F32),
        compiler_params=_params("parallel", "arbitrary"),
    )(a, b)


def matmul_residual(x, y, wg, layer, name):
    s, n = x.shape
    k = y.shape[1]
    tm = _tile(s, 1024)

    def body(x_ref, y_ref, w_ref, o_ref):
        o_ref[...] = x_ref[...] + _dot(y_ref[...], w_ref[...])

    return pl.pallas_call(
        body, name=name, grid=(s // tm,),
        in_specs=[pl.BlockSpec((tm, n), lambda i: (i, 0)),
                  pl.BlockSpec((tm, k), lambda i: (i, 0)),
                  pl.BlockSpec((None, k, n), lambda i: (layer, 0, 0))],
        out_specs=pl.BlockSpec((tm, n), lambda i: (i, 0)),
        out_shape=jax.ShapeDtypeStruct((s, n), F32),
        compiler_params=_params("parallel"),
    )(x, y, wg)


def matmul_nt(dy, wg, layer, name):
    s, n = dy.shape
    k = wg.shape[1]
    tm = _tile(s, 1024)

    def body(dy_ref, w_ref, o_ref):
        o_ref[...] = _dot_nt(dy_ref[...].astype(BF16), w_ref[...])

    return pl.pallas_call(
        body, name=name, grid=(s // tm,),
        in_specs=[pl.BlockSpec((tm, n), lambda i: (i, 0)),
                  pl.BlockSpec((None, k, n), lambda i: (layer, 0, 0))],
        out_specs=pl.BlockSpec((tm, k), lambda i: (i, 0)),
        out_shape=jax.ShapeDtypeStruct((s, k), F32),
        compiler_params=_params("parallel"),
    )(dy, wg)


def matmul_nt_norm_backward(dz, wg, layer, x, g, dres, name):
    s, d = x.shape
    ns = wg.shape[-1]
    tm = _tile(s, 1024)

    def body(dz_ref, w_ref, x_ref, g_ref, dres_ref, dx_ref, dg_ref, dh_ref):
        i, j = pl.program_id(0), pl.program_id(1)
        p = _dot_nt(dz_ref[...], w_ref[...])

        @pl.when(j == 0)
        def _():
            dh_ref[...] = p

        @pl.when(j > 0)
        def _():
            dh_ref[...] += p

        @pl.when(j == N_DEV - 1)
        def _():
            dx, dgp = _norm_backward(dh_ref[...], x_ref[...], g_ref[...], dres_ref[...])
            dx_ref[...] = dx

            @pl.when(i == 0)
            def _():
                dg_ref[...] = dgp

            @pl.when(i > 0)
            def _():
                dg_ref[...] += dgp

    return pl.pallas_call(
        body, name=name, grid=(s // tm, N_DEV),
        in_specs=[pl.BlockSpec((tm, ns), lambda i, j: (i, j)),
                  pl.BlockSpec((None, None, d, ns), lambda i, j: (layer, j, 0, 0)),
                  pl.BlockSpec((tm, d), lambda i, j: (i, 0)),
                  pl.BlockSpec((1, d), lambda i, j: (0, 0)),
                  pl.BlockSpec((tm, d), lambda i, j: (i, 0))],
        out_specs=[pl.BlockSpec((tm, d), lambda i, j: (i, 0)),
                   pl.BlockSpec((1, d), lambda i, j: (0, 0))],
        out_shape=[jax.ShapeDtypeStruct((s, d), F32), jax.ShapeDtypeStruct((1, d), F32)],
        scratch_shapes=[pltpu.VMEM((tm, d), F32)],
        compiler_params=_params("arbitrary", "arbitrary"),
    )(dz, wg, x, g, dres)


def loss_and_grad(y, target, name):
    s, d = y.shape
    tm = _tile(s, 1024)

    def body(y_ref, t_ref, dy_ref, l_ref):
        e = y_ref[...] - t_ref[...]
        dy_ref[...] = e / d
        part = jnp.sum(e * e, axis=0, keepdims=True) * (0.5 / d)

        @pl.when(pl.program_id(0) == 0)
        def _():
            l_ref[...] = part

        @pl.when(pl.program_id(0) > 0)
        def _():
            l_ref[...] += part

    return pl.pallas_call(
        body, name=name, grid=(s // tm,),
        in_specs=[pl.BlockSpec((tm, d), lambda i: (i, 0)), pl.BlockSpec((tm, d), lambda i: (i, 0))],
        out_specs=[pl.BlockSpec((tm, d), lambda i: (i, 0)), pl.BlockSpec((1, d), lambda i: (0, 0))],
        out_shape=[jax.ShapeDtypeStruct((s, d), F32), jax.ShapeDtypeStruct((1, d), F32)],
        compiler_params=_params("arbitrary"),
    )(y, target)


def _layernorm(x, g, b):
    mu = jnp.mean(x, axis=-1, keepdims=True)
    xc = x - mu
    rstd = lax.rsqrt(jnp.mean(xc * xc, axis=-1, keepdims=True) + EPS)
    xn = xc * rstd
    return xn * g + b, xn, rstd


def _layernorm_backward(dy, xn, rstd, g):
    dxn = dy * g
    return rstd * (dxn - jnp.mean(dxn, axis=-1, keepdims=True) - xn * jnp.mean(dxn * xn, axis=-1, keepdims=True))


def _group_halves(x_ref, jp, nch):
    blk = jnp.concatenate([x_ref[c * CHUNK:(c + 1) * CHUNK, jp * LANES:(jp + 1) * LANES] for c in range(nch)], axis=1)
    low = (lax.broadcasted_iota(jnp.int32, blk.shape, 1) % LANES) < HEAD_DIM
    return jnp.where(low, blk, 0.0).astype(BF16), jnp.where(low, 0.0, blk).astype(BF16)


def _spatial_apply(src_ref, w_ref, dst_ref, nch, bias_ref=None):
    for jp in range(4):
        lo, hi = _group_halves(src_ref, jp, nch)
        r = _dot(w_ref[2 * jp], lo) + _dot(w_ref[2 * jp + 1], hi)
        for c in range(nch):
            v = r[:, c * LANES:(c + 1) * LANES]
            if bias_ref is not None:
                v = v + bias_ref[:, jp * LANES:(jp + 1) * LANES]
            dst_ref[c * CHUNK:(c + 1) * CHUNK, jp * LANES:(jp + 1) * LANES] = v


def _glu(zb):
    w = zb.shape[1] // 2
    return zb[:, :w] * _sigmoid(zb[:, w:])


def _fill_padded(pad_ref, prev, cur, nxt, i, nt, tm):
    pad_ref[0:CONV_HALO, :] = jnp.where(i > 0, prev, 0.0)
    pad_ref[CONV_HALO:CONV_HALO + tm, :] = cur
    pad_ref[CONV_HALO + tm:2 * CONV_HALO + tm, :] = jnp.where(i < nt - 1, nxt, 0.0)


def _halo_specs(tm, s, width, col):
    hb, nhb = tm // CONV_HALO, s // CONV_HALO
    return [pl.BlockSpec((tm, width), lambda i: (i, col)),
            pl.BlockSpec((CONV_HALO, width), lambda i: (jnp.maximum(i * hb - 1, 0), col)),
            pl.BlockSpec((CONV_HALO, width), lambda i: (jnp.minimum((i + 1) * hb, nhb - 1), col))]


def _const_spec(shape):
    nd = len(shape)
    return pl.BlockSpec(shape, lambda i: (0,) * nd)


def ab_mid_forward(z, sp, name):
    s = z.shape[0]
    aw = z.shape[1] // 4
    tm = _tile(s, 512)
    nch, nt = tm // CHUNK, s // tm

    def body(zu_ref, zv_ref, zb_ref, zp_ref, zn_ref, w_ref, bias_ref, vg_ref, vb_ref, cw_ref, cb_ref, cg_ref, cbn_ref,
             y_ref, gc_ref, vl_ref, sv_ref, pad_ref):
        i = pl.program_id(0)
        vl_ref[...] = _layernorm(_gelu(zv_ref[...]), vg_ref[...], vb_ref[...])[0]
        _spatial_apply(vl_ref, w_ref, sv_ref, nch, bias_ref)
        y_ref[:, :aw] = (_gelu(zu_ref[...]) * sv_ref[...]).astype(BF16)

        _fill_padded(pad_ref, _glu(zp_ref[...]), _glu(zb_ref[...]), _glu(zn_ref[...]), i, nt, tm)
        for rb in range(tm // CHUNK):
            for lb in range(aw // LANES):
                cols = pl.ds(lb * LANES, LANES)
                acc = jnp.broadcast_to(cb_ref[:, cols], (CHUNK, LANES))
                for k in range(CONV_W):
                    acc = acc + cw_ref[k:k + 1, cols] * pad_ref[pl.ds(rb * CHUNK + CONV_HALO - CONV_W // 2 + k, CHUNK), cols]
                gc_ref[rb * CHUNK:(rb + 1) * CHUNK, cols] = acc
        yl = _layernorm(gc_ref[...], cg_ref[...], cbn_ref[...])[0]
        y_ref[:, aw:] = (yl * _sigmoid(yl)).astype(BF16)

    return pl.pallas_call(
        body, name=name, grid=(nt,),
        in_specs=[pl.BlockSpec((tm, aw), lambda i: (i, 0)), pl.BlockSpec((tm, aw), lambda i: (i, 1))]
        + _halo_specs(tm, s, 2 * aw, 1)
        + [_const_spec(sp["w"].shape), _const_spec(sp["bias"].shape)]
        + [_const_spec((1, aw))] * 2 + [_const_spec(sp["cw"].shape)] + [_const_spec((1, aw))] * 3,
        out_specs=[pl.BlockSpec((tm, 2 * aw), lambda i: (i, 0)), pl.BlockSpec((tm, aw), lambda i: (i, 0))],
        out_shape=[jax.ShapeDtypeStruct((s, 2 * aw), BF16), jax.ShapeDtypeStruct((s, aw), F32)],
        scratch_shapes=[pltpu.VMEM((tm, aw), F32), pltpu.VMEM((tm, aw), F32), pltpu.VMEM((tm + 2 * CONV_HALO, aw), F32)],
        compiler_params=_params("parallel"),
    )(z, z, z, z, z, sp["w"], sp["bias"], sp["vg"], sp["vb"], sp["cw"], sp["cb"], sp["cg"], sp["cbn"])


def _accumulate(ref, val, first):
    @pl.when(first)
    def _():
        ref[...] = val

    @pl.when(jnp.logical_not(first))
    def _():
        ref[...] += val


def ab_mid_backward(dy, z, gc, sp, name):
    s = z.shape[0]
    aw = z.shape[1] // 4
    tm = _tile(s, 512)
    nch, nt = tm // CHUNK, s // tm

    def body(dya_ref, dyb_ref, zu_ref, zv_ref, gc_ref, w_ref, wt_ref, bias_ref, vg_ref, vb_ref, cg_ref, cbn_ref,
             dz_ref, dgc_ref, dw_ref, dsb_ref, dvg_ref, dvb_ref, dcg_ref, dcbn_ref, dcb_ref,
             vl_ref, sv_ref, dsv_ref, dvl_ref):
        first = pl.program_id(0) == 0
        zu, zv = zu_ref[...], zv_ref[...]
        u = _gelu(zu)
        vl, vn, vrstd = _layernorm(_gelu(zv), vg_ref[...], vb_ref[...])
        vl_ref[...] = vl
        _spatial_apply(vl_ref, w_ref, sv_ref, nch, bias_ref)
        dya = dya_ref[...]
        dz_ref[:, :aw] = (dya * sv_ref[...] * _gelu_grad(zu)).astype(BF16)
        dsv = dya * u
        dsv_ref[...] = dsv
        _spatial_apply(dsv_ref, wt_ref, dvl_ref, nch)

        for jp in range(4):
            dlo, dhi = _group_halves(dsv_ref, jp, nch)
            vlo, vhi = _group_halves(vl_ref, jp, nch)
            vall = vlo + vhi
            _accumulate(dw_ref.at[2 * jp], _dot_nt(dlo, vall), first)
            _accumulate(dw_ref.at[2 * jp + 1], _dot_nt(dhi, vall), first)
        rows = dsv[0:CHUNK]
        for c in range(1, nch):
            rows = rows + dsv[c * CHUNK:(c + 1) * CHUNK]
        grp = lax.broadcasted_iota(jnp.int32, (8, aw), 0) == lax.broadcasted_iota(jnp.int32, (8, aw), 1) // HEAD_DIM
        e = jnp.where(grp, 1.0, 0.0).astype(BF16)
        hi = rows.astype(BF16)
        r1 = rows - hi.astype(F32)
        mid = r1.astype(BF16)
        lo = (r1 - mid.astype(F32)).astype(BF16)
        _accumulate(dsb_ref, _dot_nt(e, hi) + _dot_nt(e, mid) + _dot_nt(e, lo), first)

        dvl = dvl_ref[...]
        _accumulate(dvg_ref, jnp.sum(dvl * vn, axis=0, keepdims=True), first)
        _accumulate(dvb_ref, jnp.sum(dvl, axis=0, keepdims=True), first)
        dz_ref[:, aw:] = (_layernorm_backward(dvl, vn, vrstd, vg_ref[...]) * _gelu_grad(zv)).astype(BF16)

        yl, yn, yrstd = _layernorm(gc_ref[...], cg_ref[...], cbn_ref[...])
        sg = _sigmoid(yl)
        dyl = dyb_ref[...] * (sg + yl * sg * (1.0 - sg))
        _accumulate(dcg_ref, jnp.sum(dyl * yn, axis=0, keepdims=True), first)
        _accumulate(dcbn_ref, jnp.sum(dyl, axis=0, keepdims=True), first)
        dgc = _layernorm_backward(dyl, yn, yrstd, cg_ref[...])
        dgc_ref[...] = dgc
        _accumulate(dcb_ref, jnp.sum(dgc, axis=0, keepdims=True), first)

    vec = jax.ShapeDtypeStruct((1, aw), F32)
    return pl.pallas_call(
        body, name=name, grid=(nt,),
        in_specs=[pl.BlockSpec((tm, aw), lambda i: (i, 0)), pl.BlockSpec((tm, aw), lambda i: (i, 1)),
                  pl.BlockSpec((tm, aw), lambda i: (i, 0)), pl.BlockSpec((tm, aw), lambda i: (i, 1)),
                  pl.BlockSpec((tm, aw), lambda i: (i, 0)),
                  _const_spec(sp["w"].shape), _const_spec(sp["w"].shape), _const_spec(sp["bias"].shape)]
        + [_const_spec((1, aw))] * 4,
        out_specs=[pl.BlockSpec((tm, 2 * aw), lambda i: (i, 0)), pl.BlockSpec((tm, aw), lambda i: (i, 0)),
                   _const_spec(sp["w"].shape), _const_spec((8, CHUNK))] + [_const_spec((1, aw))] * 5,
        out_shape=[jax.ShapeDtypeStruct((s, 4 * aw), BF16), jax.ShapeDtypeStruct((s, aw), F32),
                   jax.ShapeDtypeStruct(sp["w"].shape, F32), jax.ShapeDtypeStruct((8, CHUNK), F32)] + [vec] * 5,
        scratch_shapes=[pltpu.VMEM((tm, aw), F32)] * 4,
        compiler_params=_params("arbitrary"),
    )(dy, dy, z, z, gc, sp["w"], sp["wt"], sp["bias"], sp["vg"], sp["vb"], sp["cg"], sp["cbn"])


def conv_backward(dgc, z, dz_in, sp, name):
    s = z.shape[0]
    aw = z.shape[1] // 4
    tm = _tile(s, 512)
    nt = s // tm
    off = CONV_HALO - CONV_W // 2

    def body(d_ref, dp_ref, dn_ref, zb_ref, zp_ref, zn_ref, cw_ref, dzin_ref, dz_ref, dcw_ref, padd_ref, padg_ref, dgg_ref):
        i = pl.program_id(0)
        _fill_padded(padd_ref, dp_ref[...], d_ref[...], dn_ref[...], i, nt, tm)
        _fill_padded(padg_ref, _glu(zp_ref[...]), _glu(zb_ref[...]), _glu(zn_ref[...]), i, nt, tm)

        @pl.when(i == 0)
        def _():
            dcw_ref[...] = jnp.zeros_like(dcw_ref)

        for lb in range(aw // LANES):
            cols = pl.ds(lb * LANES, LANES)
            for rb in range(tm // CHUNK):
                acc = jnp.zeros((CHUNK, LANES), F32)
                for k in range(CONV_W):
                    acc = acc + cw_ref[k:k + 1, cols] * padd_ref[pl.ds(rb * CHUNK + CONV_HALO + CONV_W // 2 - k, CHUNK), cols]
                dgg_ref[rb * CHUNK:(rb + 1) * CHUNK, cols] = acc
            for k in range(CONV_W):
                part = jnp.zeros((1, LANES), F32)
                for rb in range(tm // CHUNK):
                    prod = d_ref[rb * CHUNK:(rb + 1) * CHUNK, cols] * padg_ref[pl.ds(rb * CHUNK + off + k, CHUNK), cols]
                    part = part + jnp.sum(prod, axis=0, keepdims=True)
                dcw_ref[k:k + 1, cols] += part

        zb = zb_ref[...]
        val, sg = zb[:, :aw], _sigmoid(zb[:, aw:])
        dgg = dgg_ref[...]
        dz_ref[:, :aw] = (dgg * sg).astype(BF16)
        dz_ref[:, aw:] = (dgg * val * sg * (1.0 - sg)).astype(BF16)

    return pl.pallas_call(
        body, name=name, grid=(nt,),
        in_specs=_halo_specs(tm, s, aw, 0) + _halo_specs(tm, s, 2 * aw, 1)
        + [_const_spec(sp["cw"].shape), pl.BlockSpec(memory_space=pl.ANY)],
        out_specs=[pl.BlockSpec((tm, 2 * aw), lambda i: (i, 1)), _const_spec(sp["cw"].shape)],
        out_shape=[jax.ShapeDtypeStruct((s, 4 * aw), BF16), jax.ShapeDtypeStruct(sp["cw"].shape, F32)],
        scratch_shapes=[pltpu.VMEM((tm + 2 * CONV_HALO, aw), F32)] * 2 + [pltpu.VMEM((tm, aw), F32)],
        input_output_aliases={7: 0},
        compiler_params=_params("arbitrary"),
    )(dgc, dgc, dgc, z, z, z, sp["cw"], dz_in)


def rope_tables(s):
    pos = jnp.arange(s, dtype=F32)
    inv_freq = ROPE_THETA ** (-jnp.arange(0, ROT_DIM, 2, dtype=F32) / ROT_DIM)
    ang = pos[:, None] * inv_freq[None, :]
    cos, sin = jnp.cos(ang), jnp.sin(ang)
    half = ROT_DIM // 2
    rest = HEAD_DIM - ROT_DIM
    one, zero, zrest = jnp.ones((s, rest), F32), jnp.zeros((s, half), F32), jnp.zeros((s, rest), F32)
    c = jnp.concatenate([cos, cos, one], axis=1)
    s1 = jnp.concatenate([-sin, zero, zrest], axis=1)
    s2 = jnp.concatenate([zero, sin, zrest], axis=1)
    return tuple(jnp.tile(t, (1, LANES // HEAD_DIM)) for t in (c, s1, s2))


def qk_prep_forward(qkv, tabs, gains, name):
    s, w3 = qkv.shape
    w = w3 // 3
    tm = _tile(s, 512)

    def body(x_ref, c_ref, s1_ref, s2_ref, g_ref, o_ref):
        part = pl.program_id(1)

        @pl.when(part < 2)
        def _():
            bd = _block_diag(LANES)
            for b in range(w // LANES):
                cols = pl.ds(b * LANES, LANES)
                t = x_ref[:, cols]
                r = lax.rsqrt(_seg_sum(t * t, bd) * (1.0 / HEAD_DIM) + EPS)
                y = t * r * g_ref[...]
                o_ref[:, cols] = (y * c_ref[...] + pltpu.roll(y, LANES - ROT_DIM // 2, 1) * s1_ref[...]
                                  + pltpu.roll(y, ROT_DIM // 2, 1) * s2_ref[...]).astype(BF16)

        @pl.when(part == 2)
        def _():
            o_ref[...] = x_ref[...].astype(BF16)

    tab = pl.BlockSpec((tm, LANES), lambda i, p: (i, 0))
    return pl.pallas_call(
        body, name=name, grid=(s // tm, 3),
        in_specs=[pl.BlockSpec((tm, w), lambda i, p: (i, p)), tab, tab, tab,
                  pl.BlockSpec((None, 1, LANES), lambda i, p: (p, 0, 0))],
        out_specs=pl.BlockSpec((tm, w), lambda i, p: (i, p)),
        out_shape=jax.ShapeDtypeStruct((s, w3), BF16),
        compiler_params=_params("parallel", "arbitrary"),
    )(qkv, *tabs, gains)


def qk_prep_backward(dqs, dks, dvs, qkv, tabs, gains, name):
    s, w3 = qkv.shape
    w = w3 // 3
    tm = _tile(s, 512)

    def body(*refs):
        grads = (refs[0:3], refs[3:6], refs[6:9])
        x_ref, c_ref, s1_ref, s2_ref, g_ref, o_ref, dg_ref = refs[9:]
        part, first = pl.program_id(0), pl.program_id(1) == 0

        def normed(ds):
            bd = _block_diag(LANES)
            acc = jnp.zeros((1, LANES), F32)
            for b in range(w // LANES):
                cols = pl.ds(b * LANES, LANES)
                dout = ds[0][:, cols] + ds[1][:, cols] + ds[2][:, cols]
                dy = (dout * c_ref[...] + pltpu.roll(dout * s1_ref[...], ROT_DIM // 2, 1)
                      + pltpu.roll(dout * s2_ref[...], LANES - ROT_DIM // 2, 1))
                t = x_ref[:, cols]
                r = lax.rsqrt(_seg_sum(t * t, bd) * (1.0 / HEAD_DIM) + EPS)
                xh = t * r
                acc = acc + jnp.sum(dy * xh, axis=0, keepdims=True)
                tt = dy * g_ref[...]
                o_ref[:, cols] = (r * (tt - xh * (_seg_sum(tt * xh, bd) * (1.0 / HEAD_DIM)))).astype(BF16)
            _accumulate(dg_ref, acc, first)

        for p in range(2):
            pl.when(part == p)(functools.partial(normed, grads[p]))

        @pl.when(part == 2)
        def _():
            o_ref[...] = (grads[2][0][...] + grads[2][1][...] + grads[2][2][...]).astype(BF16)
            _accumulate(dg_ref, jnp.zeros((1, LANES), F32), first)

    def gspec(p):
        return pl.BlockSpec((tm, w), lambda q, i: (jnp.where(q == p, i, 0), 0))

    tab = pl.BlockSpec((tm, LANES), lambda q, i: (i, 0))
    return pl.pallas_call(
        body, name=name, grid=(3, s // tm),
        in_specs=[gspec(0)] * 3 + [gspec(1)] * 3 + [gspec(2)] * 3
        + [pl.BlockSpec((tm, w), lambda q, i: (i, q)), tab, tab, tab,
           pl.BlockSpec((None, 1, LANES), lambda q, i: (q, 0, 0))],
        out_specs=[pl.BlockSpec((tm, w), lambda q, i: (i, q)), pl.BlockSpec((None, 1, LANES), lambda q, i: (q, 0, 0))],
        out_shape=[jax.ShapeDtypeStruct((s, w3), BF16), jax.ShapeDtypeStruct((3, 1, LANES), F32)],
        compiler_params=_params("arbitrary", "arbitrary"),
    )(*dqs, *dks, *dvs, qkv, *tabs, gains)


def _window_specs(tq, l, col_fn):
    hb, nhb = tq // BAND, l // BAND
    return [pl.BlockSpec((tq, LANES), lambda c, i: (i, col_fn(c))),
            pl.BlockSpec((BAND, LANES), lambda c, i: (jnp.maximum(i * hb - 1, 0), col_fn(c))),
            pl.BlockSpec((BAND, LANES), lambda c, i: (jnp.minimum((i + 1) * hb, nhb - 1), col_fn(c)))]


def _window(cur_ref, prev_ref, next_ref):
    return jnp.concatenate([prev_ref[...], cur_ref[...], next_ref[...]], axis=0)


def _band_mask(shape, centre_axis, first_row, length):
    ctr = lax.broadcasted_iota(jnp.int32, shape, centre_axis)
    win = lax.broadcasted_iota(jnp.int32, shape, 1 - centre_axis)
    row = first_row + win
    return (jnp.abs(win - BAND - ctr) <= BAND) & (row >= 0) & (row < length)


def _col_q(d):
    return lambda c: (c // 8) * 24 + c % 8


def _col_k(d):
    return lambda c: (c // 8) * 24 + 8 + c % 8


def _col_v(d):
    return lambda c: (c // 8) * 24 + 16 + c % 8


def band_attention_forward(qkvn, d, name):
    s, w3 = qkvn.shape
    w = w3 // 3
    l = s // d
    tq = _tile(l, 512)
    scale = HEAD_DIM ** -0.5
    xv = qkvn.reshape(l, d * w3)

    def body(q_ref, k_ref, kp_ref, kn_ref, v_ref, vp_ref, vn_ref, o_ref, lse_ref):
        i = pl.program_id(1)
        kw, vw = _window(k_ref, kp_ref, kn_ref), _window(v_ref, vp_ref, vn_ref)
        head0 = lax.broadcasted_iota(jnp.int32, (CHUNK, LANES), 1) < HEAD_DIM
        for b in range(tq // CHUNK):
            rows = pl.ds(b * CHUNK, CHUNK)
            mask = _band_mask((CHUNK, 2 * CHUNK), 0, i * tq + b * CHUNK - BAND, l)
            qb = q_ref[rows, :]
            kb, vb = kw[b * CHUNK:(b + 2) * CHUNK], vw[b * CHUNK:(b + 2) * CHUNK]
            outs, lses = [], []
            for hm in (head0, jnp.logical_not(head0)):
                sc = jnp.where(mask, _dot_nt(jnp.where(hm, qb, jnp.zeros_like(qb)), kb) * scale, NEG)
                m = jnp.max(sc, axis=1, keepdims=True)
                p = jnp.exp(sc - m)
                den = jnp.sum(p, axis=1, keepdims=True)
                outs.append(_dot(p.astype(BF16), vb) / den)
                lses.append(jnp.broadcast_to(m + jnp.log(den), (CHUNK, LANES)))
            o_ref[rows, :] = jnp.where(head0, outs[0], outs[1]).astype(BF16)
            lse_ref[rows, :] = jnp.where(head0, lses[0], lses[1])

    ospec = pl.BlockSpec((tq, LANES), lambda c, i: (i, c))
    o, lse = pl.pallas_call(
        body, name=name, grid=(d * w // LANES, l // tq),
        in_specs=[pl.BlockSpec((tq, LANES), lambda c, i: (i, _col_q(d)(c)))]
        + _window_specs(tq, l, _col_k(d)) + _window_specs(tq, l, _col_v(d)),
        out_specs=[ospec, ospec],
        out_shape=[jax.ShapeDtypeStruct((l, d * w), BF16), jax.ShapeDtypeStruct((l, d * w), F32)],
        compiler_params=_params("parallel", "parallel"),
    )(xv, xv, xv, xv, xv, xv, xv)
    return o.reshape(s, w), lse.reshape(s, w)


def attention_merge(os_, lses, name):
    s, w = os_[0].shape
    tm = _tile(s, 512)

    def body(o0, o1, o2, l0, l1, l2, o_ref, lse_ref):
        la, lb, lc = l0[...], l1[...], l2[...]
        m = jnp.maximum(jnp.maximum(la, lb), lc)
        wa, wb, wc = jnp.exp(la - m), jnp.exp(lb - m), jnp.exp(lc - m)
        den = wa + wb + wc
        o = (wa * o0[...].astype(F32) + wb * o1[...].astype(F32) + wc * o2[...].astype(F32)) / den
        o_ref[...] = o.astype(BF16)
        lse_ref[...] = m + jnp.log(den)

    spec = pl.BlockSpec((tm, w), lambda i: (i, 0))
    return pl.pallas_call(
        body, name=name, grid=(s // tm,), in_specs=[spec] * 6, out_specs=[spec, spec],
        out_shape=[jax.ShapeDtypeStruct((s, w), BF16), jax.ShapeDtypeStruct((s, w), F32)],
        compiler_params=_params("parallel"),
    )(*os_, *lses)


def attention_delta(do, o, name):
    s, w = do.shape
    tm = _tile(s, 512)

    def body(do_ref, o_ref, dl_ref, dob_ref):
        bd = _block_diag(LANES)
        for b in range(w // LANES):
            cols = pl.ds(b * LANES, LANES)
            dv = do_ref[:, cols]
            dl_ref[:, cols] = _seg_sum(dv * o_ref[:, cols].astype(F32), bd)
            dob_ref[:, cols] = dv.astype(BF16)

    spec = pl.BlockSpec((tm, w), lambda i: (i, 0))
    return pl.pallas_call(
        body, name=name, grid=(s // tm,), in_specs=[spec, spec], out_specs=[spec, spec],
        out_shape=[jax.ShapeDtypeStruct((s, w), F32), jax.ShapeDtypeStruct((s, w), BF16)],
        compiler_params=_params("parallel"),
    )(do, o)


def band_attention_dq(qkvn, dob, lse, delta, d, name):
    s, w3 = qkvn.shape
    w = w3 // 3
    l = s // d
    tq = _tile(l, 512)
    scale = HEAD_DIM ** -0.5
    xv = qkvn.reshape(l, d * w3)

    def body(q_ref, k_ref, kp_ref, kn_ref, v_ref, vp_ref, vn_ref, do_ref, lse_ref, dl_ref, dq_ref):
        i = pl.program_id(1)
        kw, vw = _window(k_ref, kp_ref, kn_ref), _window(v_ref, vp_ref, vn_ref)
        head0 = lax.broadcasted_iota(jnp.int32, (CHUNK, LANES), 1) < HEAD_DIM
        for b in range(tq // CHUNK):
            rows = pl.ds(b * CHUNK, CHUNK)
            mask = _band_mask((CHUNK, 2 * CHUNK), 0, i * tq + b * CHUNK - BAND, l)
            qb, dob_ = q_ref[rows, :], do_ref[rows, :]
            kb, vb = kw[b * CHUNK:(b + 2) * CHUNK], vw[b * CHUNK:(b + 2) * CHUNK]
            outs = []
            for h, hm in enumerate((head0, jnp.logical_not(head0))):
                col = pl.ds(h * HEAD_DIM, 1)
                sc = jnp.where(mask, _dot_nt(jnp.where(hm, qb, jnp.zeros_like(qb)), kb) * scale, NEG)
                p = jnp.exp(sc - lse_ref[rows, col])
                dp = _dot_nt(jnp.where(hm, dob_, jnp.zeros_like(dob_)), vb)
                ds = p * (dp - dl_ref[rows, col]) * scale
                outs.append(_dot(ds.astype(BF16), kb))
            dq_ref[rows, :] = jnp.where(head0, outs[0], outs[1])

    ospec = pl.BlockSpec((tq, LANES), lambda c, i: (i, c))
    dq = pl.pallas_call(
        body, name=name, grid=(d * w // LANES, l // tq),
        in_specs=[pl.BlockSpec((tq, LANES), lambda c, i: (i, _col_q(d)(c)))]
        + _window_specs(tq, l, _col_k(d)) + _window_specs(tq, l, _col_v(d)) + [ospec, ospec, ospec],
        out_specs=ospec,
        out_shape=jax.ShapeDtypeStruct((l, d * w), F32),
        compiler_params=_params("parallel", "parallel"),
    )(xv, xv, xv, xv, xv, xv, xv, dob.reshape(l, d * w), lse.reshape(l, d * w), delta.reshape(l, d * w))
    return dq.reshape(s, w)


def band_attention_dkv(qkvn, dob, lse, delta, d, name):
    s, w3 = qkvn.shape
    w = w3 // 3
    l = s // d
    tq = _tile(l, 512)
    scale = HEAD_DIM ** -0.5
    xv = qkvn.reshape(l, d * w3)

    def body(k_ref, v_ref, q_ref, qp_ref, qn_ref, do_ref, dop_ref, don_ref, lse_ref, lsep_ref, lsen_ref,
             dl_ref, dlp_ref, dln_ref, dk_ref, dv_ref):
        i = pl.program_id(1)
        qw, dow = _window(q_ref, qp_ref, qn_ref), _window(do_ref, dop_ref, don_ref)
        lsew, dlw = _window(lse_ref, lsep_ref, lsen_ref), _window(dl_ref, dlp_ref, dln_ref)
        head0 = lax.broadcasted_iota(jnp.int32, (2 * CHUNK, LANES), 1) < HEAD_DIM
        for b in range(tq // CHUNK):
            rows = pl.ds(b * CHUNK, CHUNK)
            mask = _band_mask((2 * CHUNK, CHUNK), 1, i * tq + b * CHUNK - BAND, l)
            kb, vb = k_ref[rows, :], v_ref[rows, :]
            win = slice(b * CHUNK, (b + 2) * CHUNK)
            qb, dob_, lseb, dlb = qw[win], dow[win], lsew[win], dlw[win]
            dk = jnp.zeros((CHUNK, LANES), F32)
            dv = jnp.zeros((CHUNK, LANES), F32)
            for h, hm in enumerate((head0, jnp.logical_not(head0))):
                col = slice(h * HEAD_DIM, h * HEAD_DIM + 1)
                qm = jnp.where(hm, qb, jnp.zeros_like(qb))
                dom = jnp.where(hm, dob_, jnp.zeros_like(dob_))
                sc = jnp.where(mask, _dot_nt(qm, kb) * scale, NEG)
                p = jnp.exp(sc - lseb[:, col])
                ds = p * (_dot_nt(dom, vb) - dlb[:, col]) * scale
                dv = dv + _dot_tn(p.astype(BF16), dom)
                dk = dk + _dot_tn(ds.astype(BF16), qm)
            dk_ref[rows, :] = dk
            dv_ref[rows, :] = dv

    ident = lambda c: c
    ospec = pl.BlockSpec((tq, LANES), lambda c, i: (i, c))
    dk, dv = pl.pallas_call(
        body, name=name, grid=(d * w // LANES, l // tq),
        in_specs=[pl.BlockSpec((tq, LANES), lambda c, i: (i, _col_k(d)(c))),
                  pl.BlockSpec((tq, LANES), lambda c, i: (i, _col_v(d)(c)))]
        + _window_specs(tq, l, _col_q(d)) + _window_specs(tq, l, ident) * 3,
        out_specs=[ospec, ospec],
        out_shape=[jax.ShapeDtypeStruct((l, d * w), F32)] * 2,
        compiler_params=_params("parallel", "parallel"),
    )(xv, xv, xv, xv, xv, *[dob.reshape(l, d * w)] * 3, *[lse.reshape(l, d * w)] * 3, *[delta.reshape(l, d * w)] * 3)
    return dk.reshape(s, w), dv.reshape(s, w)


def _my_index():
    return 4 * lax.axis_index("x") + 2 * lax.axis_index("y") + lax.axis_index("c")


def _device(i):
    return (i // 4, (i // 2) % 2, i % 2)


_HBM = pl.BlockSpec(memory_space=pl.ANY)


def all_gather(shard, name):
    lead, rest = shard.shape[0], shard.shape[1:]

    def body(x_ref, o_ref, send_sems, recv_sems, local_sem):
        me = _my_index()

        def copy(k, block, to):
            return pltpu.make_async_remote_copy(
                src_ref=x_ref, dst_ref=o_ref.at[:, block], send_sem=send_sems.at[k], recv_sem=recv_sems.at[k],
                device_id=_device(to), device_id_type=MESH)

        mine = pltpu.make_async_copy(x_ref, o_ref.at[:, me], local_sem)
        mine.start()
        sends = [copy(k, me, (me + k + 1) % N_DEV) for k in range(N_DEV - 1)]
        for cp in sends:
            cp.start()
        for k in range(N_DEV - 1):
            copy(k, (me + N_DEV - k - 1) % N_DEV, me).wait_recv()
        for cp in sends:
            cp.wait_send()
        mine.wait()

    return pl.pallas_call(
        body, name=name, in_specs=[_HBM], out_specs=_HBM,
        out_shape=jax.ShapeDtypeStruct((lead, N_DEV) + rest, shard.dtype),
        scratch_shapes=[pltpu.SemaphoreType.DMA((N_DEV - 1,)), pltpu.SemaphoreType.DMA((N_DEV - 1,)),
                        pltpu.SemaphoreType.DMA(())],
    )(shard)


def exchange(parts, name):
    n = len(parts)
    shape = parts[0].shape[1:]

    def body(*refs):
        g_refs, o_ref, send_sems, recv_sems, local_sems = refs[:n], refs[n], refs[n + 1], refs[n + 2], refs[n + 3]
        me = _my_index()
        local, sends = [], []
        for t in range(n):
            cp = pltpu.make_async_copy(g_refs[t].at[me], o_ref.at[me, t], local_sems.at[t])
            cp.start()
            local.append(cp)
            for k in range(N_DEV - 1):
                to = (me + k + 1) % N_DEV
                cp = pltpu.make_async_remote_copy(
                    src_ref=g_refs[t].at[to], dst_ref=o_ref.at[me, t], send_sem=send_sems.at[t, k],
                    recv_sem=recv_sems.at[t, k], device_id=_device(to), device_id_type=MESH)
                cp.start()
                sends.append(cp)
        for t in range(n):
            for k in range(N_DEV - 1):
                src = (me + N_DEV - k - 1) % N_DEV
                pltpu.make_async_remote_copy(
                    src_ref=g_refs[t].at[me], dst_ref=o_ref.at[src, t], send_sem=send_sems.at[t, k],
                    recv_sem=recv_sems.at[t, k], device_id=_device(src), device_id_type=MESH).wait_recv()
        for cp in sends:
            cp.wait_send()
        for cp in local:
            cp.wait()

    return pl.pallas_call(
        body, name=name, in_specs=[_HBM] * n, out_specs=_HBM,
        out_shape=jax.ShapeDtypeStruct((N_DEV, n) + shape, parts[0].dtype),
        scratch_shapes=[pltpu.SemaphoreType.DMA((n, N_DEV - 1)), pltpu.SemaphoreType.DMA((n, N_DEV - 1)),
                        pltpu.SemaphoreType.DMA((n,))],
    )(*parts)


def adamw_update(recv, w, m, v, name):
    r, c = w.shape
    tr = 256 if (r > 256 and r % 256 == 0) else r
    c1 = 1.0 - ADAM_B1 ** ADAM_STEP
    c2 = 1.0 - ADAM_B2 ** ADAM_STEP

    def body(g_ref, w_ref, m_ref, v_ref, go_ref, d_ref, mo_ref, vo_ref):
        g = g_ref[0]
        for j in range(1, N_DEV):
            g = g + g_ref[j]
        mn = ADAM_B1 * m_ref[...] + (1.0 - ADAM_B1) * g
        vn = ADAM_B2 * v_ref[...] + (1.0 - ADAM_B2) * (g * g)
        go_ref[...] = g
        mo_ref[...] = mn
        vo_ref[...] = vn
        d_ref[...] = -ADAM_LR * ((mn / c1) / (jnp.sqrt(vn / c2) + ADAM_EPS) + ADAM_WD * w_ref[...])

    spec = pl.BlockSpec((tr, c), lambda i: (i, 0))
    return pl.pallas_call(
        body, name=name, grid=(r // tr,),
        in_specs=[pl.BlockSpec((N_DEV, tr, c), lambda i: (0, i, 0)), spec, spec, spec],
        out_specs=[spec] * 4, out_shape=[jax.ShapeDtypeStruct((r, c), F32)] * 4,
        compiler_params=_params("parallel"),
    )(recv, w, m, v)


BIG = ("mlp_w1", "mlp_w2", "ab_w_in", "ab_w_out", "c_w_qkv", "c_w_out")
SMALL = ("mix_norm_g", "mlp_norm_g", "a_spatial_w", "a_spatial_b", "a_vnorm_g", "a_vnorm_b", "b_conv_b", "b_norm_g",
         "b_norm_b", "c_q_norm_g", "c_k_norm_g")
WEIGHTS = ("mix_norm_g", "mlp_norm_g", "mlp_w1", "mlp_w2", "ab_w_in", "a_spatial_w", "a_spatial_b", "a_vnorm_g",
           "a_vnorm_b", "b_conv_w", "b_conv_b", "b_norm_g", "b_norm_b", "ab_w_out", "c_w_qkv", "c_q_norm_g",
           "c_k_norm_g", "c_w_out")


def _mixer_params(p, conv_full, i):
    aw = p["a_vnorm_g"].shape[1]
    row = lambda t: t[i][None, :]
    return dict(
        w=p["a_spatial_w"][i].astype(BF16), wt=jnp.swapaxes(p["a_spatial_w"][i], 1, 2).astype(BF16),
        bias=jnp.repeat(p["a_spatial_b"][i].T, aw // p["a_spatial_b"].shape[1], axis=1),
        vg=row(p["a_vnorm_g"]), vb=row(p["a_vnorm_b"]), cw=jnp.pad(conv_full[i], ((0, 1), (0, 0))),
        cb=row(p["b_conv_b"]), cg=row(p["b_norm_g"]), cbn=row(p["b_norm_b"]))


def _head_gains(p, i):
    rep = LANES // HEAD_DIM
    return jnp.stack([jnp.tile(p["c_q_norm_g"][i], rep), jnp.tile(p["c_k_norm_g"][i], rep),
                      jnp.ones((LANES,), F32)])[:, None, :]


def forward_backward(x, target, p, wg, conv_full):
    s, d = x.shape
    depth = p["mix_norm_g"].shape[0]
    tabs = rope_tables(s)
    saved = []
    for l in range(depth):
        i = l // 2
        mix_g, mlp_g = p["mix_norm_g"][l][None, :], p["mlp_norm_g"][l][None, :]
        st = dict(x_in=x)
        if l % 2 == 0:
            sp = _mixer_params(p, conv_full, i)
            z, h = norm_matmul(x, mix_g, wg["ab_w_in"], i, f"ab_in_{l}")
            ycat, gc = ab_mid_forward(z, sp, f"ab_mid_{l}")
            x = matmul_residual(x, ycat, wg["ab_w_out"], i, f"ab_out_{l}")
            st.update(z=z, h=h, y=ycat, gc=gc, sp=sp)
        else:
            gains = _head_gains(p, i)
            qkv, h = norm_matmul(x, mix_g, wg["c_w_qkv"], i, f"c_qkv_{l}")
            qkvn = qk_prep_forward(qkv, tabs, gains, f"c_prep_{l}")
            outs = [band_attention_forward(qkvn, dil, f"c_attn_{l}_d{dil}") for dil in DILATIONS]
            o, lse = attention_merge([t[0] for t in outs], [t[1] for t in outs], f"c_merge_{l}")
            x = matmul_residual(x, o, wg["c_w_out"], i, f"c_out_{l}")
            st.update(qkv=qkv, h=h, qkvn=qkvn, y=o, lse=lse, gains=gains)
        st["x_mid"] = x
        x, a, h2 = mlp_forward(x, mlp_g, wg["mlp_w1"], wg["mlp_w2"], l, f"mlp_{l}")
        st.update(a=a, h2=h2)
        saved.append(st)

    dy, loss_part = loss_and_grad(x, target, "loss")

    big = {n: [None] * p_len for n, p_len in (("mlp_w1", depth), ("mlp_w2", depth), ("ab_w_in", depth // 2 + depth % 2),
                                              ("ab_w_out", depth // 2 + depth % 2), ("c_w_qkv", depth // 2),
                                              ("c_w_out", depth // 2))}
    small = {n: [None] * p[n].shape[0] for n in SMALL}
    conv_grads = [None] * p["b_conv_b"].shape[0]
    for l in reversed(range(depth)):
        i, st = l // 2, saved[l]
        mix_g, mlp_g = p["mix_norm_g"][l][None, :], p["mlp_norm_g"][l][None, :]
        dxm, da, dg = mlp_backward(dy, st["a"], st["x_mid"], mlp_g, wg["mlp_w1"], wg["mlp_w2"], l, f"mlp_bwd_{l}")
        small["mlp_norm_g"][l] = dg[0]
        big["mlp_w1"][l] = matmul_tn(st["h2"], da, "cols", f"mlp_dw1_{l}")
        big["mlp_w2"][l] = matmul_tn(st["a"], dy, "rows", f"mlp_dw2_{l}", relu2=True)
        if l % 2 == 0:
            sp = st["sp"]
            dycat = matmul_nt(dxm, wg["ab_w_out"], i, f"ab_out_bwd_{l}")
            big["ab_w_out"][i] = matmul_tn(st["y"], dxm, "rows", f"ab_dwout_{l}")
            dz, dgc, dw, dsb, dvg, dvb, dcg, dcbn, dcb = ab_mid_backward(dycat, st["z"], st["gc"], sp, f"ab_mid_bwd_{l}")
            dz, dcw = conv_backward(dgc, st["z"], dz, sp, f"ab_conv_bwd_{l}")
            small["a_spatial_w"][i], small["a_spatial_b"][i] = dw, dsb
            small["a_vnorm_g"][i], small["a_vnorm_b"][i] = dvg[0], dvb[0]
            small["b_norm_g"][i], small["b_norm_b"][i], small["b_conv_b"][i] = dcg[0], dcbn[0], dcb[0]
            conv_grads[i] = dcw[:CONV_W]
            dy, dg = matmul_nt_norm_backward(dz, wg["ab_w_in"], i, st["x_in"], mix_g, dxm, f"ab_in_bwd_{l}")
            big["ab_w_in"][i] = matmul_tn(st["h"], dz, "cols", f"ab_dwin_{l}")
        else:
            do = matmul_nt(dxm, wg["c_w_out"], i, f"c_out_bwd_{l}")
            big["c_w_out"][i] = matmul_tn(st["y"], dxm, "rows", f"c_dwout_{l}")
            delta, dob = attention_delta(do, st["y"], f"c_delta_{l}")
            dqs, dks, dvs = [], [], []
            for dil in DILATIONS:
                dqs.append(band_attention_dq(st["qkvn"], dob, st["lse"], delta, dil, f"c_attn_dq_{l}_d{dil}"))
                dk, dv = band_attention_dkv(st["qkvn"], dob, st["lse"], delta, dil, f"c_attn_dkv_{l}_d{dil}")
                dks.append(dk)
                dvs.append(dv)
            dqkv, dgn = qk_prep_backward(dqs, dks, dvs, st["qkv"], tabs, st["gains"], f"c_prep_bwd_{l}")
            small["c_q_norm_g"][i] = dgn[0, 0, :HEAD_DIM] + dgn[0, 0, HEAD_DIM:]
            small["c_k_norm_g"][i] = dgn[1, 0, :HEAD_DIM] + dgn[1, 0, HEAD_DIM:]
            dy, dg = matmul_nt_norm_backward(dqkv, wg["c_w_qkv"], i, st["x_in"], mix_g, dxm, f"c_qkv_bwd_{l}")
            big["c_w_qkv"][i] = matmul_tn(st["h"], dqkv, "cols", f"c_dwqkv_{l}")
        small["mix_norm_g"][l] = dg[0]
    small = {n: jnp.stack(v) for n, v in small.items()}
    return loss_part, dy, big, small, jnp.stack(conv_grads)


def _pack(d):
    flat = jnp.concatenate([d[n].reshape(-1) for n in SMALL])
    rows = -(-flat.shape[0] // (8 * LANES)) * 8
    return jnp.pad(flat, (0, rows * LANES - flat.shape[0])).reshape(rows, LANES)


def _unpack(packed, like):
    flat, out, pos = packed.reshape(-1), {}, 0
    for n in SMALL:
        size = math.prod(like[n].shape)
        out[n] = flat[pos:pos + size].reshape(like[n].shape)
        pos += size
    return out


def kernel(x, mix_norm_g, mlp_norm_g, mlp_w1, mlp_w2, ab_w_in, a_spatial_w, a_spatial_b, a_vnorm_g, a_vnorm_b, b_conv_w, b_conv_b, b_norm_g, b_norm_b, ab_w_out, c_w_qkv, c_q_norm_g, c_k_norm_g, c_w_out, loss_target, m_mix_norm_g, m_mlp_norm_g, m_mlp_w1, m_mlp_w2, m_ab_w_in, m_a_spatial_w, m_a_spatial_b, m_a_vnorm_g, m_a_vnorm_b, m_b_conv_w, m_b_conv_b, m_b_norm_g, m_b_norm_b, m_ab_w_out, m_c_w_qkv, m_c_q_norm_g, m_c_k_norm_g, m_c_w_out, v_mix_norm_g, v_mlp_norm_g, v_mlp_w1, v_mlp_w2, v_ab_w_in, v_a_spatial_w, v_a_spatial_b, v_a_vnorm_g, v_a_vnorm_b, v_b_conv_w, v_b_conv_b, v_b_norm_g, v_b_norm_b, v_ab_w_out, v_c_w_qkv, v_c_q_norm_g, v_c_k_norm_g, v_c_w_out):
    args = dict(locals())
    w = {n: args[n] for n in WEIGHTS}
    m = {n: args["m_" + n] for n in WEIGHTS}
    v = {n: args["v_" + n] for n in WEIGHTS}

    wg = {n: all_gather(w[n].astype(BF16), "gather_" + n) for n in BIG}
    for n in ("ab_w_out", "c_w_out"):
        t = wg[n]
        wg[n] = t.reshape(t.shape[0], t.shape[1] * t.shape[2], t.shape[3])
    conv = all_gather(w["b_conv_w"], "gather_b_conv_w")
    conv_full = jnp.swapaxes(conv, 1, 2).reshape(conv.shape[0], conv.shape[2], N_DEV * conv.shape[3])

    loss_part, dx, big, small, conv_grad = forward_backward(x[0], loss_target[0], w, wg, conv_full)
    loss = lax.psum(jnp.sum(loss_part), ("x", "y", "c"))

    grads, deltas, new_m, new_v = {}, {}, {}, {}

    def update(n, recv):
        shape = w[n].shape
        flat = lambda t: t.reshape(-1, shape[-1])
        outs = adamw_update(recv.reshape((N_DEV, -1, shape[-1])), flat(w[n]), flat(m[n]), flat(v[n]), "adamw_" + n)
        grads[n], deltas[n], new_m[n], new_v[n] = (t.reshape(shape) for t in outs)

    for n in BIG:
        update(n, exchange(big[n], "exchange_" + n))
    nl, kw, cw = conv_grad.shape
    conv_parts = jnp.transpose(conv_grad.reshape(nl, kw, N_DEV, cw // N_DEV), (2, 0, 1, 3))
    update("b_conv_w", exchange([conv_parts], "exchange_b_conv_w"))

    packed = all_gather(_pack(small)[None], "gather_small_grads")[0]
    outs = adamw_update(packed, _pack(w), _pack(m), _pack(v), "adamw_small")
    for dst, t in zip((grads, deltas, new_m, new_v), outs):
        dst.update(_unpack(t, w))

    return (loss, dx[None], *[grads[n] for n in WEIGHTS], *[deltas[n] for n in WEIGHTS],
            *[new_m[n] for n in WEIGHTS], *[new_v[n] for n in WEIGHTS])
```

```python
import functools
import math

import jax
import jax.numpy as jnp
from jax import lax
from jax.experimental import pallas as pl
from jax.experimental.pallas import tpu as pltpu

F32, BF16 = jnp.float32, jnp.bfloat16
N_DEV = 8
EPS = 1e-6
NEG = -1e30
LANES = 128
HEAD_DIM = 64
CHUNK = 128
CONV_W = 31
CONV_HALO = 16
BAND = 64
DILATIONS = (1, 4, 16)
ROT_DIM = 16
ROPE_THETA = 500000.0
VMEM_LIMIT = 56 * 1024 * 1024
ADAM_LR, ADAM_B1, ADAM_B2, ADAM_EPS, ADAM_WD, ADAM_STEP = 0.001, 0.9, 0.999, 1e-08, 0.01, 10
MESH = pl.DeviceIdType.MESH


def _params(*sem):
    return pltpu.CompilerParams(dimension_semantics=sem, vmem_limit_bytes=VMEM_LIMIT)


def _dot(a, b):
    return jnp.dot(a, b, preferred_element_type=F32)


def _dot_nt(a, b):
    return lax.dot_general(a, b, (((1,), (1,)), ((), ())), preferred_element_type=F32)


def _dot_tn(a, b):
    return lax.dot_general(a, b, (((0,), (0,)), ((), ())), preferred_element_type=F32)


def _rms_r(x):
    return lax.rsqrt(jnp.mean(x * x, axis=-1, keepdims=True) + EPS)


def _sigmoid(x):
    return 1.0 / (1.0 + jnp.exp(-x))


_GK = math.sqrt(2.0 / math.pi)


def _gelu(x):
    return 0.5 * x * (1.0 + jnp.tanh(_GK * (x + 0.044715 * x * x * x)))


def _gelu_grad(x):
    t = jnp.tanh(_GK * (x + 0.044715 * x * x * x))
    return 0.5 * (1.0 + t) + 0.5 * x * (1.0 - t * t) * (_GK * (1.0 + 3.0 * 0.044715 * x * x))


def _seg_sum(x, bd):
    hi = x.astype(BF16)
    r1 = x - hi.astype(F32)
    mid = r1.astype(BF16)
    lo = (r1 - mid.astype(F32)).astype(BF16)
    return _dot(hi, bd) + _dot(mid, bd) + _dot(lo, bd)


def _block_diag(n):
    i = lax.broadcasted_iota(jnp.int32, (n, n), 0) // HEAD_DIM
    j = lax.broadcasted_iota(jnp.int32, (n, n), 1) // HEAD_DIM
    return jnp.where(i == j, 1.0, 0.0).astype(BF16)


def _tile(s, cap):
    t = min(s, cap)
    assert s % t == 0
    return t


def _my_index():
    return 4 * lax.axis_index("x") + 2 * lax.axis_index("y") + lax.axis_index("c")


def _device(i):
    return (i // 4, (i // 2) % 2, i % 2)


_HBM = pl.BlockSpec(memory_space=pl.ANY)


class PeerCopies:
    def __init__(self, operands, scatter):
        self.inputs = list(operands)
        self.scatter = scatter
        self.out_shape = [jax.ShapeDtypeStruct(t.shape if scatter else (N_DEV,) + t.shape, t.dtype) for t in operands]
        n = len(operands)
        self.scratch = [pltpu.SemaphoreType.DMA((n, N_DEV - 1)), pltpu.SemaphoreType.DMA((n, N_DEV - 1)),
                        pltpu.SemaphoreType.DMA((n,))]

    def _copies(self, in_refs, out_refs, sems):
        send_sems, recv_sems, local_sems = sems
        me = _my_index()
        local, sends, recvs = [], [], []
        for t, (src, dst) in enumerate(zip(in_refs, out_refs)):
            block = (lambda j: src.at[j]) if self.scatter else (lambda j: src)
            local.append(pltpu.make_async_copy(block(me), dst.at[me], local_sems.at[t]))
            for k in range(N_DEV - 1):
                to, frm = (me + k + 1) % N_DEV, (me + N_DEV - k - 1) % N_DEV
                sends.append(pltpu.make_async_remote_copy(
                    src_ref=block(to), dst_ref=dst.at[me], send_sem=send_sems.at[t, k], recv_sem=recv_sems.at[t, k],
                    device_id=_device(to), device_id_type=MESH))
                recvs.append(pltpu.make_async_remote_copy(
                    src_ref=block(me), dst_ref=dst.at[frm], send_sem=send_sems.at[t, k], recv_sem=recv_sems.at[t, k],
                    device_id=_device(frm), device_id_type=MESH))
        return local, sends, recvs

    def start(self, in_refs, out_refs, sems):
        local, sends, _ = self._copies(in_refs, out_refs, sems)
        for cp in local + sends:
            cp.start()

    def finish(self, in_refs, out_refs, sems):
        local, sends, recvs = self._copies(in_refs, out_refs, sems)
        for cp in recvs:
            cp.wait_recv()
        for cp in sends:
            cp.wait_send()
        for cp in local:
            cp.wait()


def _pallas(body, *, name, args, in_specs, out_specs, out_shape, grid=(), scratch=(), sem=(), comm=None, aliases=None):
    n_in, n_out, n_scr = len(args), len(out_shape), len(scratch)
    if comm is None:
        outs = pl.pallas_call(
            body, name=name, grid=grid, in_specs=in_specs, out_specs=out_specs, out_shape=out_shape,
            scratch_shapes=list(scratch), input_output_aliases=aliases or {}, compiler_params=_params(*sem))(*args)
        return list(outs), []
    ci, co = len(comm.inputs), len(comm.out_shape)

    def hosted(*refs):
        ins, cins = refs[:n_in], refs[n_in:n_in + ci]
        outs, couts = refs[n_in + ci:n_in + ci + n_out], refs[n_in + ci + n_out:n_in + ci + n_out + co]
        rest = refs[n_in + ci + n_out + co:]
        scr, sems = rest[:n_scr], rest[n_scr:]
        if not grid:
            comm.start(cins, couts, sems)
            comm.finish(cins, couts, sems)
            return
        first = last = None
        for axis, size in enumerate(grid):
            f, l = pl.program_id(axis) == 0, pl.program_id(axis) == size - 1
            first, last = (f, l) if first is None else (first & f, last & l)
        pl.when(first)(lambda: comm.start(cins, couts, sems))
        body(*ins, *outs, *scr)
        pl.when(last)(lambda: comm.finish(cins, couts, sems))

    outs = pl.pallas_call(
        hosted, name=name, grid=grid, in_specs=list(in_specs) + [_HBM] * ci, out_specs=list(out_specs) + [_HBM] * co,
        out_shape=list(out_shape) + comm.out_shape, scratch_shapes=list(scratch) + comm.scratch,
        input_output_aliases=aliases or {}, compiler_params=_params(*["arbitrary"] * len(grid)))(*args, *comm.inputs)
    return list(outs[:n_out]), list(outs[n_out:])


def run_copies(comm, name):
    return _pallas(None, name=name, args=[], in_specs=[], out_specs=[], out_shape=[], comm=comm)[1]


def norm_matmul(x, g, wg, name, comm=None):
    s, d = x.shape
    ns = wg.shape[-1]
    tm = _tile(s, 1024)

    def body(x_ref, g_ref, w_ref, z_ref, h_ref):
        @pl.when(pl.program_id(1) == 0)
        def _():
            xv = x_ref[...]
            h_ref[...] = (xv * _rms_r(xv) * g_ref[...]).astype(BF16)
        z_ref[...] = _dot(h_ref[...], w_ref[...])

    return _pallas(
        body, name=name, grid=(s // tm, N_DEV), args=[x, g, wg], comm=comm,
        in_specs=[pl.BlockSpec((tm, d), lambda i, j: (i, 0)),
                  pl.BlockSpec((1, d), lambda i, j: (0, 0)),
                  pl.BlockSpec((None, d, ns), lambda i, j: (j, 0, 0))],
        out_specs=[pl.BlockSpec((tm, ns), lambda i, j: (i, j)),
                   pl.BlockSpec((tm, d), lambda i, j: (i, 0))],
        out_shape=[jax.ShapeDtypeStruct((s, N_DEV * ns), F32), jax.ShapeDtypeStruct((s, d), BF16)],
        sem=("parallel", "arbitrary"))


def mlp_forward(x, g, w1g, w2g, name, comm=None):
    s, d = x.shape
    fs = w1g.shape[-1]
    tm = _tile(s, 1024)

    def body(x_ref, g_ref, w1_ref, w2_ref, xo_ref, a_ref, h_ref):
        @pl.when(pl.program_id(1) == 0)
        def _():
            xv = x_ref[...]
            h_ref[...] = (xv * _rms_r(xv) * g_ref[...]).astype(BF16)
            xo_ref[...] = xv
        a = _dot(h_ref[...], w1_ref[...])
        a_ref[...] = a.astype(BF16)
        r = jnp.maximum(a, 0.0)
        xo_ref[...] += _dot((r * r).astype(BF16), w2_ref[...])

    return _pallas(
        body, name=name, grid=(s // tm, N_DEV), args=[x, g, w1g, w2g], comm=comm,
        in_specs=[pl.BlockSpec((tm, d), lambda i, j: (i, 0)),
                  pl.BlockSpec((1, d), lambda i, j: (0, 0)),
                  pl.BlockSpec((None, d, fs), lambda i, j: (j, 0, 0)),
                  pl.BlockSpec((None, fs, d), lambda i, j: (j, 0, 0))],
        out_specs=[pl.BlockSpec((tm, d), lambda i, j: (i, 0)),
                   pl.BlockSpec((tm, fs), lambda i, j: (i, j)),
                   pl.BlockSpec((tm, d), lambda i, j: (i, 0))],
        out_shape=[jax.ShapeDtypeStruct((s, d), F32), jax.ShapeDtypeStruct((s, N_DEV * fs), BF16),
                   jax.ShapeDtypeStruct((s, d), BF16)],
        sem=("parallel", "arbitrary"))


def _norm_backward(dh, xv, g, dres):
    r = _rms_r(xv)
    xh = xv * r
    t = dh * g
    dx = dres + r * (t - xh * jnp.mean(t * xh, axis=-1, keepdims=True))
    return dx, jnp.sum(dh * xh, axis=0, keepdims=True)


def mlp_backward(dy, a, x, g, w1g, w2g, name, comm=None):
    s, d = x.shape
    fs = w1g.shape[-1]
    tm = _tile(s, 1024)

    def body(dy_ref, a_ref, x_ref, g_ref, w1_ref, w2_ref, dx_ref, da_ref, dg_ref, dyb_ref, dh_ref):
        i, j = pl.program_id(0), pl.program_id(1)

        @pl.when(j == 0)
        def _():
            dyb_ref[...] = dy_ref[...].astype(BF16)
            dh_ref[...] = jnp.zeros_like(dh_ref)

        dr = _dot_nt(dyb_ref[...], w2_ref[...])
        da = (dr * (2.0 * jnp.maximum(a_ref[...].astype(F32), 0.0))).astype(BF16)
        da_ref[...] = da
        dh_ref[...] += _dot_nt(da, w1_ref[...])

        @pl.when(j == N_DEV - 1)
        def _():
            dx, dgp = _norm_backward(dh_ref[...], x_ref[...], g_ref[...], dy_ref[...])
            dx_ref[...] = dx

            @pl.when(i == 0)
            def _():
                dg_ref[...] = dgp

            @pl.when(i > 0)
            def _():
                dg_ref[...] += dgp

    return _pallas(
        body, name=name, grid=(s // tm, N_DEV), args=[dy, a, x, g, w1g, w2g], comm=comm,
        in_specs=[pl.BlockSpec((tm, d), lambda i, j: (i, 0)),
                  pl.BlockSpec((tm, fs), lambda i, j: (i, j)),
                  pl.BlockSpec((tm, d), lambda i, j: (i, 0)),
                  pl.BlockSpec((1, d), lambda i, j: (0, 0)),
                  pl.BlockSpec((None, d, fs), lambda i, j: (j, 0, 0)),
                  pl.BlockSpec((None, fs, d), lambda i, j: (j, 0, 0))],
        out_specs=[pl.BlockSpec((tm, d), lambda i, j: (i, 0)),
                   pl.BlockSpec((tm, fs), lambda i, j: (i, j)),
                   pl.BlockSpec((1, d), lambda i, j: (0, 0))],
        out_shape=[jax.ShapeDtypeStruct((s, d), F32), jax.ShapeDtypeStruct((s, N_DEV * fs), BF16),
                   jax.ShapeDtypeStruct((1, d), F32)],
        scratch=[pltpu.VMEM((tm, d), BF16), pltpu.VMEM((tm, d), F32)],
        sem=("arbitrary", "arbitrary"))


def matmul_tn(a, b, shard, name, relu2=False, comm=None):
    s, m = a.shape
    n = b.shape[1]
    ts = _tile(s, 1024)
    if shard == "cols":
        bm, bn = m, n // N_DEV
        a_map, b_map = (lambda j, k: (k, 0)), (lambda j, k: (k, j))
    else:
        bm, bn = m // N_DEV, n
        a_map, b_map = (lambda j, k: (k, j)), (lambda j, k: (k, 0))

    def body(a_ref, b_ref, o_ref, acc_ref):
        k = pl.program_id(1)
        av = a_ref[...]
        if relu2:
            af = jnp.maximum(av.astype(F32), 0.0)
            av = af * af
        p = _dot_tn(av.astype(BF16), b_ref[...].astype(BF16))

        @pl.when(k == 0)
        def _():
            acc_ref[...] = p

        @pl.when(k > 0)
        def _():
            acc_ref[...] += p

        @pl.when(k == s // ts - 1)
        def _():
            o_ref[...] = acc_ref[...].astype(BF16)

    outs, couts = _pallas(
        body, name=name, grid=(N_DEV, s // ts), args=[a, b], comm=comm,
        in_specs=[pl.BlockSpec((ts, bm), a_map), pl.BlockSpec((ts, bn), b_map)],
        out_specs=[pl.BlockSpec((None, bm, bn), lambda j, k: (j, 0, 0))],
        out_shape=[jax.ShapeDtypeStruct((N_DEV, bm, bn), BF16)],
        scratch=[pltpu.VMEM((bm, bn), F32)],
        sem=("parallel", "arbitrary"))
    return outs[0], couts


def matmul_residual(x, y, w, name):
    s, n = x.shape
    k = y.shape[1]
    tm = _tile(s, 1024)

    def body(x_ref, y_ref, w_ref, o_ref):
        o_ref[...] = x_ref[...] + _dot(y_ref[...], w_ref[...])

    return pl.pallas_call(
        body, name=name, grid=(s // tm,),
        in_specs=[pl.BlockSpec((tm, n), lambda i: (i, 0)),
                  pl.BlockSpec((tm, k), lambda i: (i, 0)),
                  pl.BlockSpec((k, n), lambda i: (0, 0))],
        out_specs=pl.BlockSpec((tm, n), lambda i: (i, 0)),
        out_shape=jax.ShapeDtypeStruct((s, n), F32),
        compiler_params=_params("parallel"),
    )(x, y, w)


def matmul_nt(dy, wg, name):
    s, n = dy.shape
    k = wg.shape[0]
    tm = _tile(s, 1024)

    def body(dy_ref, w_ref, o_ref):
        o_ref[...] = _dot_nt(dy_ref[...].astype(BF16), w_ref[...])

    return pl.pallas_call(
        body, name=name, grid=(s // tm,),
        in_specs=[pl.BlockSpec((tm, n), lambda i: (i, 0)),
                  pl.BlockSpec((k, n), lambda i: (0, 0))],
        out_specs=pl.BlockSpec((tm, k), lambda i: (i, 0)),
        out_shape=jax.ShapeDtypeStruct((s, k), F32),
        compiler_params=_params("parallel"),
    )(dy, wg)


def matmul_nt_norm_backward(dz, wg, x, g, dres, name):
    s, d = x.shape
    ns = wg.shape[-1]
    tm = _tile(s, 1024)

    def body(dz_ref, w_ref, x_ref, g_ref, dres_ref, dx_ref, dg_ref, dh_ref):
        i, j = pl.program_id(0), pl.program_id(1)
        p = _dot_nt(dz_ref[...], w_ref[...])

        @pl.when(j == 0)
        def _():
            dh_ref[...] = p

        @pl.when(j > 0)
        def _():
            dh_ref[...] += p

        @pl.when(j == N_DEV - 1)
        def _():
            dx, dgp = _norm_backward(dh_ref[...], x_ref[...], g_ref[...], dres_ref[...])
            dx_ref[...] = dx

            @pl.when(i == 0)
            def _():
                dg_ref[...] = dgp

            @pl.when(i > 0)
            def _():
                dg_ref[...] += dgp

    return pl.pallas_call(
        body, name=name, grid=(s // tm, N_DEV),
        in_specs=[pl.BlockSpec((tm, ns), lambda i, j: (i, j)),
                  pl.BlockSpec((None, d, ns), lambda i, j: (j, 0, 0)),
                  pl.BlockSpec((tm, d), lambda i, j: (i, 0)),
                  pl.BlockSpec((1, d), lambda i, j: (0, 0)),
                  pl.BlockSpec((tm, d), lambda i, j: (i, 0))],
        out_specs=[pl.BlockSpec((tm, d), lambda i, j: (i, 0)),
                   pl.BlockSpec((1, d), lambda i, j: (0, 0))],
        out_shape=[jax.ShapeDtypeStruct((s, d), F32), jax.ShapeDtypeStruct((1, d), F32)],
        scratch_shapes=[pltpu.VMEM((tm, d), F32)],
        compiler_params=_params("arbitrary", "arbitrary"),
    )(dz, wg, x, g, dres)


def loss_and_grad(y, target, name):
    s, d = y.shape
    tm = _tile(s, 1024)

    def body(y_ref, t_ref, dy_ref, l_ref):
        e = y_ref[...] - t_ref[...]
        dy_ref[...] = e / d
        part = jnp.sum(e * e, axis=0, keepdims=True) * (0.5 / d)

        @pl.when(pl.program_id(0) == 0)
        def _():
            l_ref[...] = part

        @pl.when(pl.program_id(0) > 0)
        def _():
            l_ref[...] += part

    return pl.pallas_call(
        body, name=name, grid=(s // tm,),
        in_specs=[pl.BlockSpec((tm, d), lambda i: (i, 0)), pl.BlockSpec((tm, d), lambda i: (i, 0))],
        out_specs=[pl.BlockSpec((tm, d), lambda i: (i, 0)), pl.BlockSpec((1, d), lambda i: (0, 0))],
        out_shape=[jax.ShapeDtypeStruct((s, d), F32), jax.ShapeDtypeStruct((1, d), F32)],
        compiler_params=_params("arbitrary"),
    )(y, target)


def _layernorm(x, g, b):
    mu = jnp.mean(x, axis=-1, keepdims=True)
    xc = x - mu
    rstd = lax.rsqrt(jnp.mean(xc * xc, axis=-1, keepdims=True) + EPS)
    xn = xc * rstd
    return xn * g + b, xn, rstd


def _layernorm_backward(dy, xn, rstd, g):
    dxn = dy * g
    return rstd * (dxn - jnp.mean(dxn, axis=-1, keepdims=True) - xn * jnp.mean(dxn * xn, axis=-1, keepdims=True))


def _group_halves(x_ref, jp, nch):
    blk = jnp.concatenate([x_ref[c * CHUNK:(c + 1) * CHUNK, jp * LANES:(jp + 1) * LANES] for c in range(nch)], axis=1)
    low = (lax.broadcasted_iota(jnp.int32, blk.shape, 1) % LANES) < HEAD_DIM
    return jnp.where(low, blk, 0.0).astype(BF16), jnp.where(low, 0.0, blk).astype(BF16)


def _spatial_apply(src_ref, w_ref, dst_ref, nch, bias_ref=None):
    for jp in range(4):
        lo, hi = _group_halves(src_ref, jp, nch)
        r = _dot(w_ref[2 * jp], lo) + _dot(w_ref[2 * jp + 1], hi)
        for c in range(nch):
            v = r[:, c * LANES:(c + 1) * LANES]
            if bias_ref is not None:
                v = v + bias_ref[:, jp * LANES:(jp + 1) * LANES]
            dst_ref[c * CHUNK:(c + 1) * CHUNK, jp * LANES:(jp + 1) * LANES] = v


def _glu(zb):
    w = zb.shape[1] // 2
    return zb[:, :w] * _sigmoid(zb[:, w:])


def _fill_padded(pad_ref, prev, cur, nxt, i, nt, tm):
    pad_ref[0:CONV_HALO, :] = jnp.where(i > 0, prev, 0.0)
    pad_ref[CONV_HALO:CONV_HALO + tm, :] = cur
    pad_ref[CONV_HALO + tm:2 * CONV_HALO + tm, :] = jnp.where(i < nt - 1, nxt, 0.0)


def _halo_specs(tm, s, width, col):
    hb, nhb = tm // CONV_HALO, s // CONV_HALO
    return [pl.BlockSpec((tm, width), lambda i: (i, col)),
            pl.BlockSpec((CONV_HALO, width), lambda i: (jnp.maximum(i * hb - 1, 0), col)),
            pl.BlockSpec((CONV_HALO, width), lambda i: (jnp.minimum((i + 1) * hb, nhb - 1), col))]


def _const_spec(shape):
    nd = len(shape)
    return pl.BlockSpec(shape, lambda i: (0,) * nd)


def ab_mid_forward(z, sp, name, comm=None):
    s = z.shape[0]
    aw = z.shape[1] // 4
    tm = _tile(s, 512)
    nch, nt = tm // CHUNK, s // tm

    def body(zu_ref, zv_ref, zb_ref, zp_ref, zn_ref, w_ref, bias_ref, vg_ref, vb_ref, cw_ref, cb_ref, cg_ref, cbn_ref,
             y_ref, gc_ref, vl_ref, sv_ref, pad_ref):
        i = pl.program_id(0)
        vl_ref[...] = _layernorm(_gelu(zv_ref[...]), vg_ref[...], vb_ref[...])[0]
        _spatial_apply(vl_ref, w_ref, sv_ref, nch, bias_ref)
        y_ref[:, :aw] = (_gelu(zu_ref[...]) * sv_ref[...]).astype(BF16)

        _fill_padded(pad_ref, _glu(zp_ref[...]), _glu(zb_ref[...]), _glu(zn_ref[...]), i, nt, tm)
        for rb in range(tm // CHUNK):
            for lb in range(aw // LANES):
                cols = pl.ds(lb * LANES, LANES)
                acc = jnp.broadcast_to(cb_ref[:, cols], (CHUNK, LANES))
                for k in range(CONV_W):
                    acc = acc + cw_ref[k:k + 1, cols] * pad_ref[pl.ds(rb * CHUNK + CONV_HALO - CONV_W // 2 + k, CHUNK), cols]
                gc_ref[rb * CHUNK:(rb + 1) * CHUNK, cols] = acc
        yl = _layernorm(gc_ref[...], cg_ref[...], cbn_ref[...])[0]
        y_ref[:, aw:] = (yl * _sigmoid(yl)).astype(BF16)

    return _pallas(
        body, name=name, grid=(nt,), comm=comm,
        args=[z, z, z, z, z, sp["w"], sp["bias"], sp["vg"], sp["vb"], sp["cw"], sp["cb"], sp["cg"], sp["cbn"]],
        in_specs=[pl.BlockSpec((tm, aw), lambda i: (i, 0)), pl.BlockSpec((tm, aw), lambda i: (i, 1))]
        + _halo_specs(tm, s, 2 * aw, 1)
        + [_const_spec(sp["w"].shape), _const_spec(sp["bias"].shape)]
        + [_const_spec((1, aw))] * 2 + [_const_spec(sp["cw"].shape)] + [_const_spec((1, aw))] * 3,
        out_specs=[pl.BlockSpec((tm, 2 * aw), lambda i: (i, 0)), pl.BlockSpec((tm, aw), lambda i: (i, 0))],
        out_shape=[jax.ShapeDtypeStruct((s, 2 * aw), BF16), jax.ShapeDtypeStruct((s, aw), F32)],
        scratch=[pltpu.VMEM((tm, aw), F32), pltpu.VMEM((tm, aw), F32), pltpu.VMEM((tm + 2 * CONV_HALO, aw), F32)],
        sem=("parallel",))


def _accumulate(ref, val, first):
    @pl.when(first)
    def _():
        ref[...] = val

    @pl.when(jnp.logical_not(first))
    def _():
        ref[...] += val


def ab_mid_backward(dy, z, gc, sp, name):
    s = z.shape[0]
    aw = z.shape[1] // 4
    tm = _tile(s, 512)
    nch, nt = tm // CHUNK, s // tm

    def body(dya_ref, dyb_ref, zu_ref, zv_ref, gc_ref, w_ref, wt_ref, bias_ref, vg_ref, vb_ref, cg_ref, cbn_ref,
             dz_ref, dgc_ref, dw_ref, dsb_ref, dvg_ref, dvb_ref, dcg_ref, dcbn_ref, dcb_ref,
             vl_ref, sv_ref, dsv_ref, dvl_ref):
        first = pl.program_id(0) == 0
        zu, zv = zu_ref[...], zv_ref[...]
        u = _gelu(zu)
        vl, vn, vrstd = _layernorm(_gelu(zv), vg_ref[...], vb_ref[...])
        vl_ref[...] = vl
        _spatial_apply(vl_ref, w_ref, sv_ref, nch, bias_ref)
        dya = dya_ref[...]
        dz_ref[:, :aw] = (dya * sv_ref[...] * _gelu_grad(zu)).astype(BF16)
        dsv = dya * u
        dsv_ref[...] = dsv
        _spatial_apply(dsv_ref, wt_ref, dvl_ref, nch)

        for jp in range(4):
            dlo, dhi = _group_halves(dsv_ref, jp, nch)
            vlo, vhi = _group_halves(vl_ref, jp, nch)
            vall = vlo + vhi
            _accumulate(dw_ref.at[2 * jp], _dot_nt(dlo, vall), first)
            _accumulate(dw_ref.at[2 * jp + 1], _dot_nt(dhi, vall), first)
        rows = dsv[0:CHUNK]
        for c in range(1, nch):
            rows = rows + dsv[c * CHUNK:(c + 1) * CHUNK]
        grp = lax.broadcasted_iota(jnp.int32, (8, aw), 0) == lax.broadcasted_iota(jnp.int32, (8, aw), 1) // HEAD_DIM
        e = jnp.where(grp, 1.0, 0.0).astype(BF16)
        hi = rows.astype(BF16)
        r1 = rows - hi.astype(F32)
        mid = r1.astype(BF16)
        lo = (r1 - mid.astype(F32)).astype(BF16)
        _accumulate(dsb_ref, _dot_nt(e, hi) + _dot_nt(e, mid) + _dot_nt(e, lo), first)

        dvl = dvl_ref[...]
        _accumulate(dvg_ref, jnp.sum(dvl * vn, axis=0, keepdims=True), first)
        _accumulate(dvb_ref, jnp.sum(dvl, axis=0, keepdims=True), first)
        dz_ref[:, aw:] = (_layernorm_backward(dvl, vn, vrstd, vg_ref[...]) * _gelu_grad(zv)).astype(BF16)

        yl, yn, yrstd = _layernorm(gc_ref[...], cg_ref[...], cbn_ref[...])
        sg = _sigmoid(yl)
        dyl = dyb_ref[...] * (sg + yl * sg * (1.0 - sg))
        _accumulate(dcg_ref, jnp.sum(dyl * yn, axis=0, keepdims=True), first)
        _accumulate(dcbn_ref, jnp.sum(dyl, axis=0, keepdims=True), first)
        dgc = _layernorm_backward(dyl, yn, yrstd, cg_ref[...])
        dgc_ref[...] = dgc
        _accumulate(dcb_ref, jnp.sum(dgc, axis=0, keepdims=True), first)

    vec = jax.ShapeDtypeStruct((1, aw), F32)
    return pl.pallas_call(
        body, name=name, grid=(nt,),
        in_specs=[pl.BlockSpec((tm, aw), lambda i: (i, 0)), pl.BlockSpec((tm, aw), lambda i: (i, 1)),
                  pl.BlockSpec((tm, aw), lambda i: (i, 0)), pl.BlockSpec((tm, aw), lambda i: (i, 1)),
                  pl.BlockSpec((tm, aw), lambda i: (i, 0)),
                  _const_spec(sp["w"].shape), _const_spec(sp["w"].shape), _const_spec(sp["bias"].shape)]
        + [_const_spec((1, aw))] * 4,
        out_specs=[pl.BlockSpec((tm, 2 * aw), lambda i: (i, 0)), pl.BlockSpec((tm, aw), lambda i: (i, 0)),
                   _const_spec(sp["w"].shape), _const_spec((8, CHUNK))] + [_const_spec((1, aw))] * 5,
        out_shape=[jax.ShapeDtypeStruct((s, 4 * aw), BF16), jax.ShapeDtypeStruct((s, aw), F32),
                   jax.ShapeDtypeStruct(sp["w"].shape, F32), jax.ShapeDtypeStruct((8, CHUNK), F32)] + [vec] * 5,
        scratch_shapes=[pltpu.VMEM((tm, aw), F32)] * 4,
        compiler_params=_params("arbitrary"),
    )(dy, dy, z, z, gc, sp["w"], sp["wt"], sp["bias"], sp["vg"], sp["vb"], sp["cg"], sp["cbn"])


def conv_backward(dgc, z, dz_in, sp, name):
    s = z.shape[0]
    aw = z.shape[1] // 4
    tm = _tile(s, 512)
    nt = s // tm
    off = CONV_HALO - CONV_W // 2

    def body(d_ref, dp_ref, dn_ref, zb_ref, zp_ref, zn_ref, cw_ref, dzin_ref, dz_ref, dcw_ref, padd_ref, padg_ref, dgg_ref):
        i = pl.program_id(0)
        _fill_padded(padd_ref, dp_ref[...], d_ref[...], dn_ref[...], i, nt, tm)
        _fill_padded(padg_ref, _glu(zp_ref[...]), _glu(zb_ref[...]), _glu(zn_ref[...]), i, nt, tm)

        @pl.when(i == 0)
        def _():
            dcw_ref[...] = jnp.zeros_like(dcw_ref)

        for lb in range(aw // LANES):
            cols = pl.ds(lb * LANES, LANES)
            for rb in range(tm // CHUNK):
                acc = jnp.zeros((CHUNK, LANES), F32)
                for k in range(CONV_W):
                    acc = acc + cw_ref[k:k + 1, cols] * padd_ref[pl.ds(rb * CHUNK + CONV_HALO + CONV_W // 2 - k, CHUNK), cols]
                dgg_ref[rb * CHUNK:(rb + 1) * CHUNK, cols] = acc
            for k in range(CONV_W):
                part = jnp.zeros((1, LANES), F32)
                for rb in range(tm // CHUNK):
                    prod = d_ref[rb * CHUNK:(rb + 1) * CHUNK, cols] * padg_ref[pl.ds(rb * CHUNK + off + k, CHUNK), cols]
                    part = part + jnp.sum(prod, axis=0, keepdims=True)
                dcw_ref[k:k + 1, cols] += part

        zb = zb_ref[...]
        val, sg = zb[:, :aw], _sigmoid(zb[:, aw:])
        dgg = dgg_ref[...]
        dz_ref[:, :aw] = (dgg * sg).astype(BF16)
        dz_ref[:, aw:] = (dgg * val * sg * (1.0 - sg)).astype(BF16)

    return pl.pallas_call(
        body, name=name, grid=(nt,),
        in_specs=_halo_specs(tm, s, aw, 0) + _halo_specs(tm, s, 2 * aw, 1)
        + [_const_spec(sp["cw"].shape), pl.BlockSpec(memory_space=pl.ANY)],
        out_specs=[pl.BlockSpec((tm, 2 * aw), lambda i: (i, 1)), _const_spec(sp["cw"].shape)],
        out_shape=[jax.ShapeDtypeStruct((s, 4 * aw), BF16), jax.ShapeDtypeStruct(sp["cw"].shape, F32)],
        scratch_shapes=[pltpu.VMEM((tm + 2 * CONV_HALO, aw), F32)] * 2 + [pltpu.VMEM((tm, aw), F32)],
        input_output_aliases={7: 0},
        compiler_params=_params("arbitrary"),
    )(dgc, dgc, dgc, z, z, z, sp["cw"], dz_in)


def rope_tables(s):
    pos = jnp.arange(s, dtype=F32)
    inv_freq = ROPE_THETA ** (-jnp.arange(0, ROT_DIM, 2, dtype=F32) / ROT_DIM)
    ang = pos[:, None] * inv_freq[None, :]
    cos, sin = jnp.cos(ang), jnp.sin(ang)
    half = ROT_DIM // 2
    rest = HEAD_DIM - ROT_DIM
    one, zero, zrest = jnp.ones((s, rest), F32), jnp.zeros((s, half), F32), jnp.zeros((s, rest), F32)
    c = jnp.concatenate([cos, cos, one], axis=1)
    s1 = jnp.concatenate([-sin, zero, zrest], axis=1)
    s2 = jnp.concatenate([zero, sin, zrest], axis=1)
    return tuple(jnp.tile(t, (1, LANES // HEAD_DIM)) for t in (c, s1, s2))


def qk_prep_forward(qkv, tabs, gains, name, comm=None):
    s, w3 = qkv.shape
    w = w3 // 3
    tm = _tile(s, 512)

    def body(x_ref, c_ref, s1_ref, s2_ref, g_ref, o_ref):
        bd = _block_diag(LANES)
        for b in range(w // LANES):
            cols = pl.ds(b * LANES, LANES)
            t = x_ref[:, cols]
            r = lax.rsqrt(_seg_sum(t * t, bd) * (1.0 / HEAD_DIM) + EPS)
            y = t * r * g_ref[...]
            o_ref[:, cols] = (y * c_ref[...] + pltpu.roll(y, LANES - ROT_DIM // 2, 1) * s1_ref[...]
                              + pltpu.roll(y, ROT_DIM // 2, 1) * s2_ref[...])

    tab = pl.BlockSpec((tm, LANES), lambda i, p: (i, 0))
    outs, couts = _pallas(
        body, name=name, grid=(s // tm, 2), args=[qkv, *tabs, gains], comm=comm,
        in_specs=[pl.BlockSpec((tm, w), lambda i, p: (i, p)), tab, tab, tab,
                  pl.BlockSpec((None, 1, LANES), lambda i, p: (p, 0, 0))],
        out_specs=[pl.BlockSpec((tm, w), lambda i, p: (i, p))],
        out_shape=[jax.ShapeDtypeStruct((s, 2 * w), F32)],
        sem=("parallel", "arbitrary"))
    return outs[0], couts


def qk_prep_backward(dq, dk, dv, qkv, tabs, gains, name):
    s, w3 = qkv.shape
    w = w3 // 3
    tm = _tile(s, 512)

    def body(*refs):
        grads = ((refs[0],), (refs[1],), (refs[2],))
        x_ref, c_ref, s1_ref, s2_ref, g_ref, o_ref, dg_ref = refs[3:]
        part, first = pl.program_id(0), pl.program_id(1) == 0

        def normed(ds):
            bd = _block_diag(LANES)
            acc = jnp.zeros((1, LANES), F32)
            for b in range(w // LANES):
                cols = pl.ds(b * LANES, LANES)
                dout = ds[0][:, cols]
                dy = (dout * c_ref[...] + pltpu.roll(dout * s1_ref[...], ROT_DIM // 2, 1)
                      + pltpu.roll(dout * s2_ref[...], LANES - ROT_DIM // 2, 1))
                t = x_ref[:, cols]
                r = lax.rsqrt(_seg_sum(t * t, bd) * (1.0 / HEAD_DIM) + EPS)
                xh = t * r
                acc = acc + jnp.sum(dy * xh, axis=0, keepdims=True)
                tt = dy * g_ref[...]
                o_ref[:, cols] = (r * (tt - xh * (_seg_sum(tt * xh, bd) * (1.0 / HEAD_DIM)))).astype(BF16)
            _accumulate(dg_ref, acc, first)

        for p in range(2):
            pl.when(part == p)(functools.partial(normed, grads[p]))

        @pl.when(part == 2)
        def _():
            o_ref[...] = grads[2][0][...].astype(BF16)
            _accumulate(dg_ref, jnp.zeros((1, LANES), F32), first)

    def gspec(p):
        return pl.BlockSpec((tm, w), lambda q, i: (jnp.where(q == p, i, 0), 0))

    tab = pl.BlockSpec((tm, LANES), lambda q, i: (i, 0))
    return pl.pallas_call(
        body, name=name, grid=(3, s // tm),
        in_specs=[gspec(0), gspec(1), gspec(2)]
        + [pl.BlockSpec((tm, w), lambda q, i: (i, q)), tab, tab, tab,
           pl.BlockSpec((None, 1, LANES), lambda q, i: (q, 0, 0))],
        out_specs=[pl.BlockSpec((tm, w), lambda q, i: (i, q)), pl.BlockSpec((None, 1, LANES), lambda q, i: (q, 0, 0))],
        out_shape=[jax.ShapeDtypeStruct((s, w3), BF16), jax.ShapeDtypeStruct((3, 1, LANES), F32)],
        compiler_params=_params("arbitrary", "arbitrary"),
    )(dq, dk, dv, qkv, *tabs, gains)


def _window_specs(tq, l, col_fn):
    hb, nhb = tq // BAND, l // BAND
    return [pl.BlockSpec((tq, LANES), lambda c, i: (i, col_fn(c))),
            pl.BlockSpec((BAND, LANES), lambda c, i: (jnp.maximum(i * hb - 1, 0), col_fn(c))),
            pl.BlockSpec((BAND, LANES), lambda c, i: (jnp.minimum((i + 1) * hb, nhb - 1), col_fn(c)))]


def _window(cur_ref, prev_ref, next_ref):
    return jnp.concatenate([prev_ref[...], cur_ref[...], next_ref[...]], axis=0)


def _band_mask(shape, centre_axis, first_row, length):
    ctr = lax.broadcasted_iota(jnp.int32, shape, centre_axis)
    win = lax.broadcasted_iota(jnp.int32, shape, 1 - centre_axis)
    row = first_row + win
    return (jnp.abs(win - BAND - ctr) <= BAND) & (row >= 0) & (row < length)


def _col_q(d):
    return lambda c: (c // 8) * 24 + c % 8


def _col_k(d):
    return lambda c: (c // 8) * 24 + 8 + c % 8


def _col_v(d):
    return lambda c: (c // 8) * 24 + 16 + c % 8


def band_attention_forward(qkvn, d, name):
    s, w3 = qkvn.shape
    w = w3 // 3
    l = s // d
    tq = _tile(l, 512)
    scale = HEAD_DIM ** -0.5
    xv = qkvn.reshape(l, d * w3)

    def body(q_ref, k_ref, kp_ref, kn_ref, v_ref, vp_ref, vn_ref, o_ref, lse_ref):
        i = pl.program_id(1)
        kw, vw = _window(k_ref, kp_ref, kn_ref), _window(v_ref, vp_ref, vn_ref)
        head0 = lax.broadcasted_iota(jnp.int32, (CHUNK, LANES), 1) < HEAD_DIM
        for b in range(tq // CHUNK):
            rows = pl.ds(b * CHUNK, CHUNK)
            mask = _band_mask((CHUNK, 2 * CHUNK), 0, i * tq + b * CHUNK - BAND, l)
            qb = q_ref[rows, :]
            kb, vb = kw[b * CHUNK:(b + 2) * CHUNK], vw[b * CHUNK:(b + 2) * CHUNK]
            outs, lses = [], []
            for hm in (head0, jnp.logical_not(head0)):
                sc = jnp.where(mask, _dot_nt(jnp.where(hm, qb, jnp.zeros_like(qb)), kb) * scale, NEG)
                m = jnp.max(sc, axis=1, keepdims=True)
                p = jnp.exp(sc - m)
                den = jnp.sum(p, axis=1, keepdims=True)
                outs.append(_dot(p.astype(BF16), vb) / den)
                lses.append(jnp.broadcast_to(m + jnp.log(den), (CHUNK, LANES)))
            o_ref[rows, :] = jnp.where(head0, outs[0], outs[1]).astype(BF16)
            lse_ref[rows, :] = jnp.where(head0, lses[0], lses[1])

    ospec = pl.BlockSpec((tq, LANES), lambda c, i: (i, c))
    o, lse = pl.pallas_call(
        body, name=name, grid=(d * w // LANES, l // tq),
        in_specs=[pl.BlockSpec((tq, LANES), lambda c, i: (i, _col_q(d)(c)))]
        + _window_specs(tq, l, _col_k(d)) + _window_specs(tq, l, _col_v(d)),
        out_specs=[ospec, ospec],
        out_shape=[jax.ShapeDtypeStruct((l, d * w), BF16), jax.ShapeDtypeStruct((l, d * w), F32)],
        compiler_params=_params("parallel", "parallel"),
    )(xv, xv, xv, xv, xv, xv, xv)
    return o.reshape(s, w), lse.reshape(s, w)


def attention_merge(os_, lses, name):
    s, w = os_[0].shape
    tm = _tile(s, 512)

    def body(o0, o1, o2, l0, l1, l2, o_ref, lse_ref):
        la, lb, lc = l0[...], l1[...], l2[...]
        m = jnp.maximum(jnp.maximum(la, lb), lc)
        wa, wb, wc = jnp.exp(la - m), jnp.exp(lb - m), jnp.exp(lc - m)
        den = wa + wb + wc
        o = (wa * o0[...].astype(F32) + wb * o1[...].astype(F32) + wc * o2[...].astype(F32)) / den
        o_ref[...] = o.astype(BF16)
        lse_ref[...] = m + jnp.log(den)

    spec = pl.BlockSpec((tm, w), lambda i: (i, 0))
    return pl.pallas_call(
        body, name=name, grid=(s // tm,), in_specs=[spec] * 6, out_specs=[spec, spec],
        out_shape=[jax.ShapeDtypeStruct((s, w), BF16), jax.ShapeDtypeStruct((s, w), F32)],
        compiler_params=_params("parallel"),
    )(*os_, *lses)


def attention_delta(do, o, name):
    s, w = do.shape
    tm = _tile(s, 512)

    def body(do_ref, o_ref, dl_ref, dob_ref):
        bd = _block_diag(LANES)
        for b in range(w // LANES):
            cols = pl.ds(b * LANES, LANES)
            dv = do_ref[:, cols]
            dl_ref[:, cols] = _seg_sum(dv * o_ref[:, cols].astype(F32), bd)
            dob_ref[:, cols] = dv.astype(BF16)

    spec = pl.BlockSpec((tm, w), lambda i: (i, 0))
    return pl.pallas_call(
        body, name=name, grid=(s // tm,), in_specs=[spec, spec], out_specs=[spec, spec],
        out_shape=[jax.ShapeDtypeStruct((s, w), F32), jax.ShapeDtypeStruct((s, w), BF16)],
        compiler_params=_params("parallel"),
    )(do, o)


def band_attention_dq(qkvn, dob, lse, delta, d, name):
    s, w3 = qkvn.shape
    w = w3 // 3
    l = s // d
    tq = _tile(l, 512)
    scale = HEAD_DIM ** -0.5
    xv = qkvn.reshape(l, d * w3)

    def body(q_ref, k_ref, kp_ref, kn_ref, v_ref, vp_ref, vn_ref, do_ref, lse_ref, dl_ref, dq_ref):
        i = pl.program_id(1)
        kw, vw = _window(k_ref, kp_ref, kn_ref), _window(v_ref, vp_ref, vn_ref)
        head0 = lax.broadcasted_iota(jnp.int32, (CHUNK, LANES), 1) < HEAD_DIM
        for b in range(tq // CHUNK):
            rows = pl.ds(b * CHUNK, CHUNK)
            mask = _band_mask((CHUNK, 2 * CHUNK), 0, i * tq + b * CHUNK - BAND, l)
            qb, dob_ = q_ref[rows, :], do_ref[rows, :]
            kb, vb = kw[b * CHUNK:(b + 2) * CHUNK], vw[b * CHUNK:(b + 2) * CHUNK]
            outs = []
            for h, hm in enumerate((head0, jnp.logical_not(head0))):
                col = pl.ds(h * HEAD_DIM, 1)
                sc = jnp.where(mask, _dot_nt(jnp.where(hm, qb, jnp.zeros_like(qb)), kb) * scale, NEG)
                p = jnp.exp(sc - lse_ref[rows, col])
                dp = _dot_nt(jnp.where(hm, dob_, jnp.zeros_like(dob_)), vb)
                ds = p * (dp - dl_ref[rows, col]) * scale
                outs.append(_dot(ds.astype(BF16), kb))
            dq_ref[rows, :] = jnp.where(head0, outs[0], outs[1])

    ospec = pl.BlockSpec((tq, LANES), lambda c, i: (i, c))
    dq = pl.pallas_call(
        body, name=name, grid=(d * w // LANES, l // tq),
        in_specs=[pl.BlockSpec((tq, LANES), lambda c, i: (i, _col_q(d)(c)))]
        + _window_specs(tq, l, _col_k(d)) + _window_specs(tq, l, _col_v(d)) + [ospec, ospec, ospec],
        out_specs=ospec,
        out_shape=jax.ShapeDtypeStruct((l, d * w), F32),
        compiler_params=_params("parallel", "parallel"),
    )(xv, xv, xv, xv, xv, xv, xv, dob.reshape(l, d * w), lse.reshape(l, d * w), delta.reshape(l, d * w))
    return dq.reshape(s, w)


def band_attention_dkv(qkvn, dob, lse, delta, d, name):
    s, w3 = qkvn.shape
    w = w3 // 3
    l = s // d
    tq = _tile(l, 512)
    scale = HEAD_DIM ** -0.5
    xv = qkvn.reshape(l, d * w3)

    def body(k_ref, v_ref, q_ref, qp_ref, qn_ref, do_ref, dop_ref, don_ref, lse_ref, lsep_ref, lsen_ref,
             dl_ref, dlp_ref, dln_ref, dk_ref, dv_ref):
        i = pl.program_id(1)
        qw, dow = _window(q_ref, qp_ref, qn_ref), _window(do_ref, dop_ref, don_ref)
        lsew, dlw = _window(lse_ref, lsep_ref, lsen_ref), _window(dl_ref, dlp_ref, dln_ref)
        head0 = lax.broadcasted_iota(jnp.int32, (2 * CHUNK, LANES), 1) < HEAD_DIM
        for b in range(tq // CHUNK):
            rows = pl.ds(b * CHUNK, CHUNK)
            mask = _band_mask((2 * CHUNK, CHUNK), 1, i * tq + b * CHUNK - BAND, l)
            kb, vb = k_ref[rows, :], v_ref[rows, :]
            win = slice(b * CHUNK, (b + 2) * CHUNK)
            qb, dob_, lseb, dlb = qw[win], dow[win], lsew[win], dlw[win]
            dk = jnp.zeros((CHUNK, LANES), F32)
            dv = jnp.zeros((CHUNK, LANES), F32)
            for h, hm in enumerate((head0, jnp.logical_not(head0))):
                col = slice(h * HEAD_DIM, h * HEAD_DIM + 1)
                qm = jnp.where(hm, qb, jnp.zeros_like(qb))
                dom = jnp.where(hm, dob_, jnp.zeros_like(dob_))
                sc = jnp.where(mask, _dot_nt(qm, kb) * scale, NEG)
                p = jnp.exp(sc - lseb[:, col])
                ds = p * (_dot_nt(dom, vb) - dlb[:, col]) * scale
                dv = dv + _dot_tn(p.astype(BF16), dom)
                dk = dk + _dot_tn(ds.astype(BF16), qm)
            dk_ref[rows, :] = dk
            dv_ref[rows, :] = dv

    ident = lambda c: c
    ospec = pl.BlockSpec((tq, LANES), lambda c, i: (i, c))
    dk, dv = pl.pallas_call(
        body, name=name, grid=(d * w // LANES, l // tq),
        in_specs=[pl.BlockSpec((tq, LANES), lambda c, i: (i, _col_k(d)(c))),
                  pl.BlockSpec((tq, LANES), lambda c, i: (i, _col_v(d)(c)))]
        + _window_specs(tq, l, _col_q(d)) + _window_specs(tq, l, ident) * 3,
        out_specs=[ospec, ospec],
        out_shape=[jax.ShapeDtypeStruct((l, d * w), F32)] * 2,
        compiler_params=_params("parallel", "parallel"),
    )(xv, xv, xv, xv, xv, *[dob.reshape(l, d * w)] * 3, *[lse.reshape(l, d * w)] * 3, *[delta.reshape(l, d * w)] * 3)
    return dk.reshape(s, w), dv.reshape(s, w)


ATT_TILE = 2048
ATT_HALO = BAND * max(DILATIONS)
ROWS_PER_COPY = 256


def _att_specs(s, t, col_fn, halo=True):
    hb, nhb = t // ATT_HALO, s // ATT_HALO
    specs = [pl.BlockSpec((t, LANES), lambda hp, i: (i, col_fn(hp)))]
    if halo:
        specs += [pl.BlockSpec((ATT_HALO, LANES), lambda hp, i: (jnp.maximum(i * hb - 1, 0), col_fn(hp))),
                  pl.BlockSpec((ATT_HALO, LANES), lambda hp, i: (jnp.minimum((i + 1) * hb, nhb - 1), col_fn(hp)))]
    return specs


def _gather_rows(dst_ref, dst_row, src_ref, start, count, stride):
    for c in range(0, count, ROWS_PER_COPY):
        m = min(ROWS_PER_COPY, count - c)
        dst_ref[dst_row + c:dst_row + c + m, :] = src_ref[pl.ds(start + c * stride, m, stride=stride), :].astype(dst_ref.dtype)


def _stage(dst_ref, cur_ref, d, t):
    n = t // d
    for r in range(d):
        _gather_rows(dst_ref, r * n, cur_ref, r, n, d)


def _stage_window(dst_ref, refs, d, t):
    cur_ref, prev_ref, next_ref = refs
    n = t // d
    nw = n + 2 * BAND
    for r in range(d):
        _gather_rows(dst_ref, r * nw, prev_ref, ATT_HALO - BAND * d + r, BAND, d)
        _gather_rows(dst_ref, r * nw + BAND, cur_ref, r, n, d)
        _gather_rows(dst_ref, r * nw + BAND + n, next_ref, r, BAND, d)


def _scatter_rows(dst_ref, src_ref, d, t, combine):
    n = t // d
    for r in range(d):
        for c in range(0, n, ROWS_PER_COPY):
            m = min(ROWS_PER_COPY, n - c)
            idx = pl.ds(r + c * d, m, stride=d)
            combine(idx, slice(r * n + c, r * n + c + m))


def _unit_rows(u, n):
    upr = n // CHUNK
    r = u // upr
    b = u - r * upr
    return pl.multiple_of(u * CHUNK, CHUNK), pl.multiple_of((u + r) * CHUNK, CHUNK), b * CHUNK - BAND


def _two_heads(x):
    head0 = lax.broadcasted_iota(jnp.int32, x.shape, 1) < HEAD_DIM
    zero = jnp.zeros_like(x)
    return jnp.concatenate([jnp.where(head0, x, zero), jnp.where(head0, zero, x)], axis=0)


def _head_columns(x):
    return jnp.concatenate([x[:, 0:1], x[:, HEAD_DIM:HEAD_DIM + 1]], axis=0)


def _merge_heads(x2):
    rows = x2.shape[0] // 2
    head0 = lax.broadcasted_iota(jnp.int32, (rows, LANES), 1) < HEAD_DIM
    return jnp.where(head0, jnp.broadcast_to(x2[:rows], (rows, LANES)), jnp.broadcast_to(x2[rows:], (rows, LANES)))


def _col(part):
    return lambda hp: part * 8 + hp


def attention_forward(qk, qkv, name):
    s, w2 = qk.shape
    w = w2 // 2
    t = _tile(s, ATT_TILE)
    scale = HEAD_DIM ** -0.5

    def body(q_ref, k_ref, kp_ref, kn_ref, v_ref, vp_ref, vn_ref, o_ref, lse_ref, qs_ref, ks_ref, vs_ref, os_ref, ls_ref, or_ref):
        i = pl.program_id(1)
        for pi, d in enumerate(DILATIONS):
            n = t // d
            _stage(qs_ref, q_ref, d, t)
            _stage_window(ks_ref, (k_ref, kp_ref, kn_ref), d, t)
            _stage_window(vs_ref, (v_ref, vp_ref, vn_ref), d, t)

            def unit(u, carry, n=n, d=d):
                qrow, wrow, first = _unit_rows(u, n)
                mask = _band_mask((CHUNK, 2 * CHUNK), 0, i * n + first, s // d)
                mask2 = jnp.concatenate([mask, mask], axis=0)
                kb, vb = ks_ref[pl.ds(wrow, 2 * CHUNK), :], vs_ref[pl.ds(wrow, 2 * CHUNK), :]
                sc = jnp.where(mask2, _dot_nt(_two_heads(qs_ref[pl.ds(qrow, CHUNK), :]), kb) * scale, NEG)
                m = jnp.max(sc, axis=1, keepdims=True)
                p = jnp.exp(sc - m)
                den = jnp.sum(p, axis=1, keepdims=True)
                os_ref[pl.ds(qrow, CHUNK), :] = _merge_heads(_dot(p.astype(BF16), vb) / den)
                ls_ref[pl.ds(qrow, CHUNK), :] = _merge_heads(m + jnp.log(den))
                return carry

            lax.fori_loop(0, t // CHUNK, unit, 0)

            if pi == 0:
                def assign(idx, rows):
                    or_ref[idx, :] = os_ref[rows, :]
                    lse_ref[idx, :] = ls_ref[rows, :]
                _scatter_rows(None, None, d, t, assign)
            else:
                def merge(idx, rows):
                    la, lb = lse_ref[idx, :], ls_ref[rows, :]
                    mx = jnp.maximum(la, lb)
                    wa, wb = jnp.exp(la - mx), jnp.exp(lb - mx)
                    den = wa + wb
                    or_ref[idx, :] = (wa * or_ref[idx, :] + wb * os_ref[rows, :]) / den
                    lse_ref[idx, :] = mx + jnp.log(den)
                _scatter_rows(None, None, d, t, merge)
        o_ref[...] = or_ref[...].astype(BF16)

    ospec = pl.BlockSpec((t, LANES), lambda hp, i: (i, hp))
    win_rows = t + 2 * ATT_HALO
    o, lse = pl.pallas_call(
        body, name=name, grid=(w // LANES, s // t),
        in_specs=_att_specs(s, t, _col(0), halo=False) + _att_specs(s, t, _col(1)) + _att_specs(s, t, _col(2)),
        out_specs=[ospec, ospec],
        out_shape=[jax.ShapeDtypeStruct((s, w), BF16), jax.ShapeDtypeStruct((s, w), F32)],
        scratch_shapes=[pltpu.VMEM((t, LANES), BF16), pltpu.VMEM((win_rows, LANES), BF16), pltpu.VMEM((win_rows, LANES), BF16),
                        pltpu.VMEM((t, LANES), F32), pltpu.VMEM((t, LANES), F32), pltpu.VMEM((t, LANES), F32)],
        compiler_params=_params("parallel", "parallel"),
    )(qk, qk, qk, qk, qkv, qkv, qkv)
    return o, lse


def attention_delta(do, o, name):
    s, w = do.shape
    tm = _tile(s, 512)

    def body(do_ref, o_ref, dl_ref):
        bd = _block_diag(LANES)
        for b in range(w // LANES):
            cols = pl.ds(b * LANES, LANES)
            dl_ref[:, cols] = _seg_sum(do_ref[:, cols] * o_ref[:, cols].astype(F32), bd)

    spec = pl.BlockSpec((tm, w), lambda i: (i, 0))
    return pl.pallas_call(
        body, name=name, grid=(s // tm,), in_specs=[spec, spec], out_specs=spec,
        out_shape=jax.ShapeDtypeStruct((s, w), F32), compiler_params=_params("parallel"),
    )(do, o)


def attention_dq(qk, qkv, do, lse, delta, name):
    s, w2 = qk.shape
    w = w2 // 2
    t = _tile(s, ATT_TILE)
    scale = HEAD_DIM ** -0.5

    def body(q_ref, k_ref, kp_ref, kn_ref, v_ref, vp_ref, vn_ref, do_ref, lse_ref, dl_ref, dq_ref,
             qs_ref, ks_ref, vs_ref, dos_ref, ls_ref, dls_ref, dqs_ref):
        i = pl.program_id(1)
        for pi, d in enumerate(DILATIONS):
            n = t // d
            _stage(qs_ref, q_ref, d, t)
            _stage(dos_ref, do_ref, d, t)
            _stage(ls_ref, lse_ref, d, t)
            _stage(dls_ref, dl_ref, d, t)
            _stage_window(ks_ref, (k_ref, kp_ref, kn_ref), d, t)
            _stage_window(vs_ref, (v_ref, vp_ref, vn_ref), d, t)

            def unit(u, carry, n=n, d=d):
                qrow, wrow, first = _unit_rows(u, n)
                rows = pl.ds(qrow, CHUNK)
                mask = _band_mask((CHUNK, 2 * CHUNK), 0, i * n + first, s // d)
                mask2 = jnp.concatenate([mask, mask], axis=0)
                kb, vb = ks_ref[pl.ds(wrow, 2 * CHUNK), :], vs_ref[pl.ds(wrow, 2 * CHUNK), :]
                sc = jnp.where(mask2, _dot_nt(_two_heads(qs_ref[rows, :]), kb) * scale, NEG)
                p = jnp.exp(sc - _head_columns(ls_ref[rows, :]))
                dp = _dot_nt(_two_heads(dos_ref[rows, :]), vb)
                ds = p * (dp - _head_columns(dls_ref[rows, :])) * scale
                dqs_ref[rows, :] = _merge_heads(_dot(ds.astype(BF16), kb))
                return carry

            lax.fori_loop(0, t // CHUNK, unit, 0)

            def add(idx, rows, pi=pi):
                dq_ref[idx, :] = dqs_ref[rows, :] if pi == 0 else dq_ref[idx, :] + dqs_ref[rows, :]
            _scatter_rows(None, None, d, t, add)

    ospec = pl.BlockSpec((t, LANES), lambda hp, i: (i, hp))
    win_rows = t + 2 * ATT_HALO
    return pl.pallas_call(
        body, name=name, grid=(w // LANES, s // t),
        in_specs=_att_specs(s, t, _col(0), halo=False) + _att_specs(s, t, _col(1)) + _att_specs(s, t, _col(2))
        + [ospec, ospec, ospec],
        out_specs=ospec,
        out_shape=jax.ShapeDtypeStruct((s, w), F32),
        scratch_shapes=[pltpu.VMEM((t, LANES), BF16), pltpu.VMEM((win_rows, LANES), BF16), pltpu.VMEM((win_rows, LANES), BF16),
                        pltpu.VMEM((t, LANES), BF16), pltpu.VMEM((t, LANES), F32), pltpu.VMEM((t, LANES), F32),
                        pltpu.VMEM((t, LANES), F32)],
        compiler_params=_params("parallel", "parallel"),
    )(qk, qk, qk, qk, qkv, qkv, qkv, do, lse, delta)


def attention_dkv(qk, qkv, do, lse, delta, name):
    s, w2 = qk.shape
    w = w2 // 2
    t = _tile(s, ATT_TILE)
    scale = HEAD_DIM ** -0.5

    def body(k_ref, v_ref, q_ref, qp_ref, qn_ref, do_ref, dop_ref, don_ref, lse_ref, lsep_ref, lsen_ref,
             dl_ref, dlp_ref, dln_ref, dk_ref, dv_ref, ks_ref, vs_ref, qs_ref, dos_ref, ls_ref, dls_ref, dks_ref, dvs_ref):
        i = pl.program_id(1)
        for pi, d in enumerate(DILATIONS):
            n = t // d
            _stage(ks_ref, k_ref, d, t)
            _stage(vs_ref, v_ref, d, t)
            _stage_window(qs_ref, (q_ref, qp_ref, qn_ref), d, t)
            _stage_window(dos_ref, (do_ref, dop_ref, don_ref), d, t)
            _stage_window(ls_ref, (lse_ref, lsep_ref, lsen_ref), d, t)
            _stage_window(dls_ref, (dl_ref, dlp_ref, dln_ref), d, t)

            def unit(u, carry, n=n, d=d):
                krow, wrow, first = _unit_rows(u, n)
                rows, win = pl.ds(krow, CHUNK), pl.ds(wrow, 2 * CHUNK)
                mask = _band_mask((2 * CHUNK, CHUNK), 1, i * n + first, s // d)
                mask2 = jnp.concatenate([mask, mask], axis=0)
                q2, do2 = _two_heads(qs_ref[win, :]), _two_heads(dos_ref[win, :])
                sc = jnp.where(mask2, _dot_nt(q2, ks_ref[rows, :]) * scale, NEG)
                p = jnp.exp(sc - _head_columns(ls_ref[win, :]))
                ds = p * (_dot_nt(do2, vs_ref[rows, :]) - _head_columns(dls_ref[win, :])) * scale
                dvs_ref[rows, :] = _dot_tn(p.astype(BF16), do2)
                dks_ref[rows, :] = _dot_tn(ds.astype(BF16), q2)
                return carry

            lax.fori_loop(0, t // CHUNK, unit, 0)

            def add(idx, rows, pi=pi):
                dk_ref[idx, :] = dks_ref[rows, :] if pi == 0 else dk_ref[idx, :] + dks_ref[rows, :]
                dv_ref[idx, :] = dvs_ref[rows, :] if pi == 0 else dv_ref[idx, :] + dvs_ref[rows, :]
            _scatter_rows(None, None, d, t, add)

    ident = lambda hp: hp
    ospec = pl.BlockSpec((t, LANES), lambda hp, i: (i, hp))
    win_rows = t + 2 * ATT_HALO
    return pl.pallas_call(
        body, name=name, grid=(w // LANES, s // t),
        in_specs=_att_specs(s, t, _col(1), halo=False) + _att_specs(s, t, _col(2), halo=False)
        + _att_specs(s, t, _col(0)) + _att_specs(s, t, ident) * 3,
        out_specs=[ospec, ospec],
        out_shape=[jax.ShapeDtypeStruct((s, w), F32)] * 2,
        scratch_shapes=[pltpu.VMEM((t, LANES), BF16), pltpu.VMEM((t, LANES), BF16),
                        pltpu.VMEM((win_rows, LANES), BF16), pltpu.VMEM((win_rows, LANES), BF16),
                        pltpu.VMEM((win_rows, LANES), F32), pltpu.VMEM((win_rows, LANES), F32),
                        pltpu.VMEM((t, LANES), F32), pltpu.VMEM((t, LANES), F32)],
        compiler_params=_params("parallel", "parallel"),
    )(qk, qkv, qk, qk, qk, do, do, do, lse, lse, lse, delta, delta, delta)


def adamw_update(recvs, w, m, v, name):
    nl = len(recvs)
    r, c = recvs[0].shape[1:]
    tr = 256 if (r > 256 and r % 256 == 0) else r
    nt = r // tr
    c1 = 1.0 - ADAM_B1 ** ADAM_STEP
    c2 = 1.0 - ADAM_B2 ** ADAM_STEP

    def body(*refs):
        g_refs = refs[:nl]
        w_ref, m_ref, v_ref, go_ref, d_ref, mo_ref, vo_ref = refs[nl:]

        def update(g_ref):
            g = g_ref[0].astype(F32)
            for j in range(1, N_DEV):
                g = g + g_ref[j].astype(F32)
            mn = ADAM_B1 * m_ref[...] + (1.0 - ADAM_B1) * g
            vn = ADAM_B2 * v_ref[...] + (1.0 - ADAM_B2) * (g * g)
            go_ref[...] = g
            mo_ref[...] = mn
            vo_ref[...] = vn
            d_ref[...] = -ADAM_LR * ((mn / c1) / (jnp.sqrt(vn / c2) + ADAM_EPS) + ADAM_WD * w_ref[...])

        for layer in range(nl):
            pl.when(pl.program_id(0) == layer)(functools.partial(update, g_refs[layer]))

    def gspec(layer):
        return pl.BlockSpec((N_DEV, tr, c), lambda l, i: (0, jnp.where(l == layer, i, 0), 0))

    spec = pl.BlockSpec((tr, c), lambda l, i: (l * nt + i, 0))
    return pl.pallas_call(
        body, name=name, grid=(nl, nt),
        in_specs=[gspec(layer) for layer in range(nl)] + [spec, spec, spec],
        out_specs=[spec] * 4, out_shape=[jax.ShapeDtypeStruct((nl * r, c), F32)] * 4,
        compiler_params=_params("arbitrary", "arbitrary"),
    )(*recvs, w, m, v)


BIG = ("mlp_w1", "mlp_w2", "ab_w_in", "ab_w_out", "c_w_qkv", "c_w_out")
SMALL = ("mix_norm_g", "mlp_norm_g", "a_spatial_w", "a_spatial_b", "a_vnorm_g", "a_vnorm_b", "b_conv_b", "b_norm_g",
         "b_norm_b", "c_q_norm_g", "c_k_norm_g")
WEIGHTS = ("mix_norm_g", "mlp_norm_g", "mlp_w1", "mlp_w2", "ab_w_in", "a_spatial_w", "a_spatial_b", "a_vnorm_g",
           "a_vnorm_b", "b_conv_w", "b_conv_b", "b_norm_g", "b_norm_b", "ab_w_out", "c_w_qkv", "c_q_norm_g",
           "c_k_norm_g", "c_w_out")


def _mixer_params(p, conv_full, i):
    aw = p["a_vnorm_g"].shape[1]
    row = lambda t: t[i][None, :]
    return dict(
        w=p["a_spatial_w"][i].astype(BF16), wt=jnp.swapaxes(p["a_spatial_w"][i], 1, 2).astype(BF16),
        bias=jnp.repeat(p["a_spatial_b"][i].T, aw // p["a_spatial_b"].shape[1], axis=1),
        vg=row(p["a_vnorm_g"]), vb=row(p["a_vnorm_b"]), cw=jnp.pad(conv_full[i], ((0, 1), (0, 0))),
        cb=row(p["b_conv_b"]), cg=row(p["b_norm_g"]), cbn=row(p["b_norm_b"]))


def _head_gains(p, i):
    rep = LANES // HEAD_DIM
    return jnp.stack([jnp.tile(p["c_q_norm_g"][i], rep), jnp.tile(p["c_k_norm_g"][i], rep),
                      jnp.ones((LANES,), F32)])[:, None, :]


def _unused_forward_backward(x, target, p, wg, conv_full):
    s, d = x.shape
    depth = p["mix_norm_g"].shape[0]
    tabs = rope_tables(s)
    saved = []
    for l in range(depth):
        i = l // 2
        mix_g, mlp_g = p["mix_norm_g"][l][None, :], p["mlp_norm_g"][l][None, :]
        st = dict(x_in=x)
        if l % 2 == 0:
            sp = _mixer_params(p, conv_full, i)
            z, h = norm_matmul(x, mix_g, wg["ab_w_in"], i, f"ab_in_{l}")
            ycat, gc = ab_mid_forward(z, sp, f"ab_mid_{l}")
            x = matmul_residual(x, ycat, wg["ab_w_out"], i, f"ab_out_{l}")
            st.update(z=z, h=h, y=ycat, gc=gc, sp=sp)
        else:
            gains = _head_gains(p, i)
            qkv, h = norm_matmul(x, mix_g, wg["c_w_qkv"], i, f"c_qkv_{l}")
            qkvn = qk_prep_forward(qkv, tabs, gains, f"c_prep_{l}")
            outs = [band_attention_forward(qkvn, dil, f"c_attn_{l}_d{dil}") for dil in DILATIONS]
            o, lse = attention_merge([t[0] for t in outs], [t[1] for t in outs], f"c_merge_{l}")
            x = matmul_residual(x, o, wg["c_w_out"], i, f"c_out_{l}")
            st.update(qkv=qkv, h=h, qkvn=qkvn, y=o, lse=lse, gains=gains)
        st["x_mid"] = x
        x, a, h2 = mlp_forward(x, mlp_g, wg["mlp_w1"], wg["mlp_w2"], l, f"mlp_{l}")
        st.update(a=a, h2=h2)
        saved.append(st)

    dy, loss_part = loss_and_grad(x, target, "loss")

    big = {n: [None] * p_len for n, p_len in (("mlp_w1", depth), ("mlp_w2", depth), ("ab_w_in", depth // 2 + depth % 2),
                                              ("ab_w_out", depth // 2 + depth % 2), ("c_w_qkv", depth // 2),
                                              ("c_w_out", depth // 2))}
    small = {n: [None] * p[n].shape[0] for n in SMALL}
    conv_grads = [None] * p["b_conv_b"].shape[0]
    for l in reversed(range(depth)):
        i, st = l // 2, saved[l]
        mix_g, mlp_g = p["mix_norm_g"][l][None, :], p["mlp_norm_g"][l][None, :]
        dxm, da, dg = mlp_backward(dy, st["a"], st["x_mid"], mlp_g, wg["mlp_w1"], wg["mlp_w2"], l, f"mlp_bwd_{l}")
        small["mlp_norm_g"][l] = dg[0]
        big["mlp_w1"][l] = matmul_tn(st["h2"], da, "cols", f"mlp_dw1_{l}")
        big["mlp_w2"][l] = matmul_tn(st["a"], dy, "rows", f"mlp_dw2_{l}", relu2=True)
        if l % 2 == 0:
            sp = st["sp"]
            dycat = matmul_nt(dxm, wg["ab_w_out"], i, f"ab_out_bwd_{l}")
            big["ab_w_out"][i] = matmul_tn(st["y"], dxm, "rows", f"ab_dwout_{l}")
            dz, dgc, dw, dsb, dvg, dvb, dcg, dcbn, dcb = ab_mid_backward(dycat, st["z"], st["gc"], sp, f"ab_mid_bwd_{l}")
            dz, dcw = conv_backward(dgc, st["z"], dz, sp, f"ab_conv_bwd_{l}")
            small["a_spatial_w"][i], small["a_spatial_b"][i] = dw, dsb
            small["a_vnorm_g"][i], small["a_vnorm_b"][i] = dvg[0], dvb[0]
            small["b_norm_g"][i], small["b_norm_b"][i], small["b_conv_b"][i] = dcg[0], dcbn[0], dcb[0]
            conv_grads[i] = dcw[:CONV_W]
            dy, dg = matmul_nt_norm_backward(dz, wg["ab_w_in"], i, st["x_in"], mix_g, dxm, f"ab_in_bwd_{l}")
            big["ab_w_in"][i] = matmul_tn(st["h"], dz, "cols", f"ab_dwin_{l}")
        else:
            do = matmul_nt(dxm, wg["c_w_out"], i, f"c_out_bwd_{l}")
            big["c_w_out"][i] = matmul_tn(st["y"], dxm, "rows", f"c_dwout_{l}")
            delta, dob = attention_delta(do, st["y"], f"c_delta_{l}")
            dqs, dks, dvs = [], [], []
            for dil in DILATIONS:
                dqs.append(band_attention_dq(st["qkvn"], dob, st["lse"], delta, dil, f"c_attn_dq_{l}_d{dil}"))
                dk, dv = band_attention_dkv(st["qkvn"], dob, st["lse"], delta, dil, f"c_attn_dkv_{l}_d{dil}")
                dks.append(dk)
                dvs.append(dv)
            dqkv, dgn = qk_prep_backward(dqs, dks, dvs, st["qkv"], tabs, st["gains"], f"c_prep_bwd_{l}")
            small["c_q_norm_g"][i] = dgn[0, 0, :HEAD_DIM] + dgn[0, 0, HEAD_DIM:]
            small["c_k_norm_g"][i] = dgn[1, 0, :HEAD_DIM] + dgn[1, 0, HEAD_DIM:]
            dy, dg = matmul_nt_norm_backward(dqkv, wg["c_w_qkv"], i, st["x_in"], mix_g, dxm, f"c_qkv_bwd_{l}")
            big["c_w_qkv"][i] = matmul_tn(st["h"], dqkv, "cols", f"c_dwqkv_{l}")
        small["mix_norm_g"][l] = dg[0]
    small = {n: jnp.stack(v) for n, v in small.items()}
    return loss_part, dy, big, small, jnp.stack(conv_grads)


def _pack(d):
    flat = jnp.concatenate([d[n].reshape(-1) for n in SMALL])
    rows = -(-flat.shape[0] // (8 * LANES)) * 8
    return jnp.pad(flat, (0, rows * LANES - flat.shape[0])).reshape(rows, LANES)


def _unpack(packed, like):
    flat, out, pos = packed.reshape(-1), {}, 0
    for n in SMALL:
        size = math.prod(like[n].shape)
        out[n] = flat[pos:pos + size].reshape(like[n].shape)
        pos += size
    return out


def _unused_kernel(x, mix_norm_g, mlp_norm_g, mlp_w1, mlp_w2, ab_w_in, a_spatial_w, a_spatial_b, a_vnorm_g, a_vnorm_b, b_conv_w, b_conv_b, b_norm_g, b_norm_b, ab_w_out, c_w_qkv, c_q_norm_g, c_k_norm_g, c_w_out, loss_target, m_mix_norm_g, m_mlp_norm_g, m_mlp_w1, m_mlp_w2, m_ab_w_in, m_a_spatial_w, m_a_spatial_b, m_a_vnorm_g, m_a_vnorm_b, m_b_conv_w, m_b_conv_b, m_b_norm_g, m_b_norm_b, m_ab_w_out, m_c_w_qkv, m_c_q_norm_g, m_c_k_norm_g, m_c_w_out, v_mix_norm_g, v_mlp_norm_g, v_mlp_w1, v_mlp_w2, v_ab_w_in, v_a_spatial_w, v_a_spatial_b, v_a_vnorm_g, v_a_vnorm_b, v_b_conv_w, v_b_conv_b, v_b_norm_g, v_b_norm_b, v_ab_w_out, v_c_w_qkv, v_c_q_norm_g, v_c_k_norm_g, v_c_w_out):
    args = dict(locals())
    w = {n: args[n] for n in WEIGHTS}
    m = {n: args["m_" + n] for n in WEIGHTS}
    v = {n: args["v_" + n] for n in WEIGHTS}

    wg = {n: all_gather(w[n].astype(BF16), "gather_" + n) for n in BIG}
    for n in ("ab_w_out", "c_w_out"):
        t = wg[n]
        wg[n] = t.reshape(t.shape[0], t.shape[1] * t.shape[2], t.shape[3])
    conv = all_gather(w["b_conv_w"], "gather_b_conv_w")
    conv_full = jnp.swapaxes(conv, 1, 2).reshape(conv.shape[0], conv.shape[2], N_DEV * conv.shape[3])

    loss_part, dx, big, small, conv_grad = forward_backward(x[0], loss_target[0], w, wg, conv_full)
    loss = lax.psum(jnp.sum(loss_part), ("x", "y", "c"))

    grads, deltas, new_m, new_v = {}, {}, {}, {}

    def update(n, recv):
        shape = w[n].shape
        flat = lambda t: t.reshape(-1, shape[-1])
        outs = adamw_update(recv.reshape((N_DEV, -1, shape[-1])), flat(w[n]), flat(m[n]), flat(v[n]), "adamw_" + n)
        grads[n], deltas[n], new_m[n], new_v[n] = (t.reshape(shape) for t in outs)

    for n in BIG:
        update(n, exchange(big[n], "exchange_" + n))
    nl, kw, cw = conv_grad.shape
    conv_parts = jnp.transpose(conv_grad.reshape(nl, kw, N_DEV, cw // N_DEV), (2, 0, 1, 3))
    update("b_conv_w", exchange([conv_parts], "exchange_b_conv_w"))

    packed = all_gather(_pack(small)[None], "gather_small_grads")[0]
    outs = adamw_update(packed, _pack(w), _pack(m), _pack(v), "adamw_small")
    for dst, t in zip((grads, deltas, new_m, new_v), outs):
        dst.update(_unpack(t, w))

    return (loss, dx[None], *[grads[n] for n in WEIGHTS], *[deltas[n] for n in WEIGHTS],
            *[new_m[n] for n in WEIGHTS], *[new_v[n] for n in WEIGHTS])


class Traffic:
    def __init__(self, shards, full=()):
        self.shards, self.w, self.queue, self.parts = shards, dict(full), [], {}

    def run(self, fn, *args, gather=(), send=False, **kw):
        comm, keys = None, []
        if self.shards is None:
            if send:
                self.parts.update(self.queue)
                self.queue = []
        elif gather:
            keys = [k for k in gather if k not in self.w]
            comm = PeerCopies([self.shards[k] for k in keys], scatter=False)
        elif send and self.queue:
            keys = [k for k, _ in self.queue]
            comm = PeerCopies([t for _, t in self.queue], scatter=True)
            self.queue = []
        outs, couts = fn(*args, comm=comm, **kw)
        (self.w if gather else self.parts).update(zip(keys, couts))
        return outs

    def flush(self, name, extra=()):
        self.queue += list(extra)
        self.run(lambda comm: ([], run_copies(comm, name) if comm is not None else []), send=True)


def _square(t):
    return t.reshape(t.shape[0] * t.shape[1], t.shape[2])


def forward_backward(x, target, p, tr, conv_full):
    s, d = x.shape
    depth = p["mix_norm_g"].shape[0]
    tabs = rope_tables(s)
    saved = []
    for l in range(depth):
        i = l // 2
        mix_g, mlp_g = p["mix_norm_g"][l][None, :], p["mlp_norm_g"][l][None, :]
        st = dict(x_in=x)
        nxt = () if l + 1 == depth else ((("c_w_qkv", i), ("c_w_out", i)) if l % 2 == 0 else
                                        (("ab_w_in", i + 1), ("ab_w_out", i + 1)))
        if l % 2 == 0:
            sp = _mixer_params(p, conv_full, i)
            z, h = tr.run(norm_matmul, x, mix_g, tr.w["ab_w_in", i], f"ab_in_{l}", gather=[("mlp_w1", l)])
            ycat, gc = tr.run(ab_mid_forward, z, sp, f"ab_mid_{l}", gather=[("mlp_w2", l)])
            x = matmul_residual(x, ycat, _square(tr.w["ab_w_out", i]), f"ab_out_{l}")
            st.update(z=z, h=h, y=ycat, gc=gc, sp=sp)
        else:
            gains = _head_gains(p, i)
            qkv, h = tr.run(norm_matmul, x, mix_g, tr.w["c_w_qkv", i], f"c_qkv_{l}", gather=[("mlp_w1", l)])
            qk = tr.run(lambda *a, comm: (lambda o, c: ([o], c))(*qk_prep_forward(*a, comm=comm)),
                        qkv, tabs, gains, f"c_prep_{l}", gather=[("mlp_w2", l)])[0]
            o, lse = attention_forward(qk, qkv, f"c_attn_{l}")
            x = matmul_residual(x, o, _square(tr.w["c_w_out", i]), f"c_out_{l}")
            st.update(qkv=qkv, h=h, qk=qk, y=o, lse=lse, gains=gains)
        st["x_mid"] = x
        x, a, h2 = tr.run(mlp_forward, x, mlp_g, tr.w["mlp_w1", l], tr.w["mlp_w2", l], f"mlp_{l}", gather=nxt)
        st.update(a=a, h2=h2)
        saved.append(st)

    dy, loss_part = loss_and_grad(x, target, "loss")

    def tn(*a, comm, **kw):
        out, couts = matmul_tn(*a, comm=comm, **kw)
        return [out], couts

    small = {n: [None] * p[n].shape[0] for n in SMALL}
    conv_grads = [None] * p["b_conv_b"].shape[0]
    for l in reversed(range(depth)):
        i, st = l // 2, saved[l]
        mix_g, mlp_g = p["mix_norm_g"][l][None, :], p["mlp_norm_g"][l][None, :]
        w1, w2 = tr.w["mlp_w1", l], tr.w["mlp_w2", l]
        dxm, da, dg = tr.run(mlp_backward, dy, st["a"], st["x_mid"], mlp_g, w1, w2, f"mlp_bwd_{l}", send=True)
        small["mlp_norm_g"][l] = dg[0]
        tr.queue.append((("mlp_w1", l), matmul_tn(st["h2"], da, "cols", f"mlp_dw1_{l}")[0]))
        dw2 = tr.run(tn, st["a"], dy, "rows", f"mlp_dw2_{l}", relu2=True, send=True)[0]
        tr.queue.append((("mlp_w2", l), dw2))
        if l % 2 == 0:
            sp = st["sp"]
            wout = _square(tr.w["ab_w_out", i])
            dycat = matmul_nt(dxm, wout, f"ab_out_bwd_{l}")
            dwout = tr.run(tn, st["y"], dxm, "rows", f"ab_dwout_{l}", send=True)[0]
            dz, dgc, dw, dsb, dvg, dvb, dcg, dcbn, dcb = ab_mid_backward(dycat, st["z"], st["gc"], sp, f"ab_mid_bwd_{l}")
            dz, dcw = conv_backward(dgc, st["z"], dz, sp, f"ab_conv_bwd_{l}")
            small["a_spatial_w"][i], small["a_spatial_b"][i] = dw, dsb
            small["a_vnorm_g"][i], small["a_vnorm_b"][i] = dvg[0], dvb[0]
            small["b_norm_g"][i], small["b_norm_b"][i], small["b_conv_b"][i] = dcg[0], dcbn[0], dcb[0]
            conv_grads[i] = dcw[:CONV_W]
            dy, dg = matmul_nt_norm_backward(dz, tr.w["ab_w_in", i], st["x_in"], mix_g, dxm, f"ab_in_bwd_{l}")
            tr.queue += [(("ab_w_out", i), dwout), (("ab_w_in", i), matmul_tn(st["h"], dz, "cols", f"ab_dwin_{l}")[0])]
        else:
            wout = _square(tr.w["c_w_out", i])
            do = matmul_nt(dxm, wout, f"c_out_bwd_{l}")
            dwout = tr.run(tn, st["y"], dxm, "rows", f"c_dwout_{l}", send=True)[0]
            delta = attention_delta(do, st["y"], f"c_delta_{l}")
            dq = attention_dq(st["qk"], st["qkv"], do, st["lse"], delta, f"c_attn_dq_{l}")
            dk, dv = attention_dkv(st["qk"], st["qkv"], do, st["lse"], delta, f"c_attn_dkv_{l}")
            dqkv, dgn = qk_prep_backward(dq, dk, dv, st["qkv"], tabs, st["gains"], f"c_prep_bwd_{l}")
            small["c_q_norm_g"][i] = dgn[0, 0, :HEAD_DIM] + dgn[0, 0, HEAD_DIM:]
            small["c_k_norm_g"][i] = dgn[1, 0, :HEAD_DIM] + dgn[1, 0, HEAD_DIM:]
            dy, dg = matmul_nt_norm_backward(dqkv, tr.w["c_w_qkv", i], st["x_in"], mix_g, dxm, f"c_qkv_bwd_{l}")
            tr.queue += [(("c_w_out", i), dwout), (("c_w_qkv", i), matmul_tn(st["h"], dqkv, "cols", f"c_dwqkv_{l}")[0])]
        small["mix_norm_g"][l] = dg[0]
    small = {n: jnp.stack(v) for n, v in small.items()}
    return loss_part, dy, small, jnp.stack(conv_grads)


def kernel(x, mix_norm_g, mlp_norm_g, mlp_w1, mlp_w2, ab_w_in, a_spatial_w, a_spatial_b, a_vnorm_g, a_vnorm_b, b_conv_w, b_conv_b, b_norm_g, b_norm_b, ab_w_out, c_w_qkv, c_q_norm_g, c_k_norm_g, c_w_out, loss_target, m_mix_norm_g, m_mlp_norm_g, m_mlp_w1, m_mlp_w2, m_ab_w_in, m_a_spatial_w, m_a_spatial_b, m_a_vnorm_g, m_a_vnorm_b, m_b_conv_w, m_b_conv_b, m_b_norm_g, m_b_norm_b, m_ab_w_out, m_c_w_qkv, m_c_q_norm_g, m_c_k_norm_g, m_c_w_out, v_mix_norm_g, v_mlp_norm_g, v_mlp_w1, v_mlp_w2, v_ab_w_in, v_a_spatial_w, v_a_spatial_b, v_a_vnorm_g, v_a_vnorm_b, v_b_conv_w, v_b_conv_b, v_b_norm_g, v_b_norm_b, v_ab_w_out, v_c_w_qkv, v_c_q_norm_g, v_c_k_norm_g, v_c_w_out):
    args = dict(locals())
    w = {n: args[n] for n in WEIGHTS}
    m = {n: args["m_" + n] for n in WEIGHTS}
    v = {n: args["v_" + n] for n in WEIGHTS}

    shards = {(n, l): w[n][l].astype(BF16) for n in BIG for l in range(w[n].shape[0])}
    shards["b_conv_w", 0] = w["b_conv_w"]
    tr = Traffic(shards)
    first = [("ab_w_in", 0), ("ab_w_out", 0), ("b_conv_w", 0)]
    tr.run(lambda comm: ([], run_copies(comm, "gather_first")), gather=first)
    conv = tr.w["b_conv_w", 0]
    conv_full = jnp.transpose(conv, (1, 2, 0, 3)).reshape(conv.shape[1], conv.shape[2], -1)

    loss_part, dx, small, conv_grad = forward_backward(x[0], loss_target[0], w, tr, conv_full)
    loss = lax.psum(jnp.sum(loss_part), ("x", "y", "c"))
    nl, kw, cw = conv_grad.shape
    conv_parts = jnp.transpose(conv_grad.reshape(nl, kw, N_DEV, cw // N_DEV), (2, 0, 1, 3))
    tr.flush("exchange_last", [(("b_conv_w", 0), conv_parts)])
    packed = run_copies(PeerCopies([_pack(small)], scatter=False), "gather_small_grads")[0]

    grads, deltas, new_m, new_v = {}, {}, {}, {}

    def update(n, recvs):
        shape = w[n].shape
        flat = lambda t: t.reshape(-1, shape[-1])
        recvs = [t.reshape(N_DEV, -1, shape[-1]) for t in recvs]
        outs = adamw_update(recvs, flat(w[n]), flat(m[n]), flat(v[n]), "adamw_" + n)
        grads[n], deltas[n], new_m[n], new_v[n] = (t.reshape(shape) for t in outs)

    for n in BIG:
        update(n, [tr.parts[n, l] for l in range(w[n].shape[0])])
    update("b_conv_w", [tr.parts["b_conv_w", 0]])
    outs = adamw_update([packed], _pack(w), _pack(m), _pack(v), "adamw_small")
    for dst, t in zip((grads, deltas, new_m, new_v), outs):
        dst.update(_unpack(t, w))

    return (loss, dx[None], *[grads[n] for n in WEIGHTS], *[deltas[n] for n in WEIGHTS],
            *[new_m[n] for n in WEIGHTS], *[new_v[n] for n in WEIGHTS])
```

```python
import functools
import math

import jax
import jax.numpy as jnp
from jax import lax
from jax.experimental import pallas as pl
from jax.experimental.pallas import tpu as pltpu

F32, BF16 = jnp.float32, jnp.bfloat16
N_DEV = 8
EPS = 1e-6
NEG = -1e30
LANES = 128
HEAD_DIM = 64
CHUNK = 128
CONV_W = 31
CONV_HALO = 16
BAND = 64
DILATIONS = (1, 4, 16)
ROT_DIM = 16
ROPE_THETA = 500000.0
VMEM_LIMIT = 56 * 1024 * 1024
ADAM_LR, ADAM_B1, ADAM_B2, ADAM_EPS, ADAM_WD, ADAM_STEP = 0.001, 0.9, 0.999, 1e-08, 0.01, 10
MESH = pl.DeviceIdType.MESH


def _params(*sem):
    return pltpu.CompilerParams(dimension_semantics=sem, vmem_limit_bytes=VMEM_LIMIT)


def _dot(a, b):
    return jnp.dot(a, b, preferred_element_type=F32)


def _dot_nt(a, b):
    return lax.dot_general(a, b, (((1,), (1,)), ((), ())), preferred_element_type=F32)


def _dot_tn(a, b):
    return lax.dot_general(a, b, (((0,), (0,)), ((), ())), preferred_element_type=F32)


def _rms_r(x):
    return lax.rsqrt(jnp.mean(x * x, axis=-1, keepdims=True) + EPS)


def _sigmoid(x):
    return 1.0 / (1.0 + jnp.exp(-x))


_GK = math.sqrt(2.0 / math.pi)


def _gelu(x):
    return 0.5 * x * (1.0 + jnp.tanh(_GK * (x + 0.044715 * x * x * x)))


def _gelu_grad(x):
    t = jnp.tanh(_GK * (x + 0.044715 * x * x * x))
    return 0.5 * (1.0 + t) + 0.5 * x * (1.0 - t * t) * (_GK * (1.0 + 3.0 * 0.044715 * x * x))


def _seg_sum(x, bd):
    hi = x.astype(BF16)
    r1 = x - hi.astype(F32)
    mid = r1.astype(BF16)
    lo = (r1 - mid.astype(F32)).astype(BF16)
    return _dot(hi, bd) + _dot(mid, bd) + _dot(lo, bd)


def _block_diag(n):
    i = lax.broadcasted_iota(jnp.int32, (n, n), 0) // HEAD_DIM
    j = lax.broadcasted_iota(jnp.int32, (n, n), 1) // HEAD_DIM
    return jnp.where(i == j, 1.0, 0.0).astype(BF16)


def _tile(s, cap):
    t = min(s, cap)
    assert s % t == 0
    return t


def _my_index():
    return 4 * lax.axis_index("x") + 2 * lax.axis_index("y") + lax.axis_index("c")


def _device(i):
    return (i // 4, (i // 2) % 2, i % 2)


_HBM = pl.BlockSpec(memory_space=pl.ANY)


class PeerCopies:
    def __init__(self, operands, scatter):
        self.inputs = list(operands)
        self.scatter = list(scatter) if isinstance(scatter, (list, tuple)) else [scatter] * len(self.inputs)
        self.out_shape = [jax.ShapeDtypeStruct(t.shape if sc else (N_DEV,) + t.shape, t.dtype)
                          for t, sc in zip(self.inputs, self.scatter)]
        n = len(operands)
        self.scratch = [pltpu.SemaphoreType.DMA((n, N_DEV - 1)), pltpu.SemaphoreType.DMA((n, N_DEV - 1)),
                        pltpu.SemaphoreType.DMA((n,))]

    def _copies(self, in_refs, out_refs, sems, arrivals):
        send_sems, recv_sems, local_sems = sems
        me = _my_index()
        local, sends, recvs = [], [], []
        for t, (src, dst) in enumerate(zip(in_refs, out_refs)):
            block = (lambda j: src.at[j]) if self.scatter[t] else (lambda j: src)
            local.append(pltpu.make_async_copy(block(me), dst.at[me], local_sems.at[t]))
            for k in range(N_DEV - 1):
                to, frm = (me + k + 1) % N_DEV, (me + N_DEV - k - 1) % N_DEV
                sends.append(pltpu.make_async_remote_copy(
                    src_ref=block(to), dst_ref=dst.at[me], send_sem=send_sems.at[t, k], recv_sem=recv_sems.at[t, k],
                    device_id=_device(to), device_id_type=MESH))
                if arrivals:
                    recvs.append(pltpu.make_async_remote_copy(
                        src_ref=block(me), dst_ref=dst.at[frm], send_sem=send_sems.at[t, k], recv_sem=recv_sems.at[t, k],
                        device_id=_device(frm), device_id_type=MESH))
        return local, sends, recvs

    def start(self, in_refs, out_refs, sems):
        local, sends, _ = self._copies(in_refs, out_refs, sems, False)
        for cp in local + sends:
            cp.start()

    def finish(self, in_refs, out_refs, sems):
        local, sends, recvs = self._copies(in_refs, out_refs, sems, True)
        for cp in recvs:
            cp.wait_recv()
        for cp in sends:
            cp.wait_send()
        for cp in local:
            cp.wait()


def _pallas(body, *, name, args, in_specs, out_specs, out_shape, grid=(), scratch=(), sem=(), comm=None, aliases=None):
    n_in, n_out, n_scr = len(args), len(out_shape), len(scratch)
    if comm is None:
        outs = pl.pallas_call(
            body, name=name, grid=grid, in_specs=in_specs, out_specs=out_specs, out_shape=out_shape,
            scratch_shapes=list(scratch), input_output_aliases=aliases or {}, compiler_params=_params(*sem))(*args)
        return list(outs), []
    ci, co = len(comm.inputs), len(comm.out_shape)

    def hosted(*refs):
        ins, cins = refs[:n_in], refs[n_in:n_in + ci]
        outs, couts = refs[n_in + ci:n_in + ci + n_out], refs[n_in + ci + n_out:n_in + ci + n_out + co]
        rest = refs[n_in + ci + n_out + co:]
        scr, sems = rest[:n_scr], rest[n_scr:]
        if not grid:
            comm.start(cins, couts, sems)
            comm.finish(cins, couts, sems)
            return
        first = last = None
        for axis, size in enumerate(grid):
            f, l = pl.program_id(axis) == 0, pl.program_id(axis) == size - 1
            first, last = (f, l) if first is None else (first & f, last & l)
        pl.when(first)(lambda: comm.start(cins, couts, sems))
        body(*ins, *outs, *scr)
        pl.when(last)(lambda: comm.finish(cins, couts, sems))

    outs = pl.pallas_call(
        hosted, name=name, grid=grid, in_specs=list(in_specs) + [_HBM] * ci, out_specs=list(out_specs) + [_HBM] * co,
        out_shape=list(out_shape) + comm.out_shape, scratch_shapes=list(scratch) + comm.scratch,
        input_output_aliases=aliases or {}, compiler_params=_params(*["arbitrary"] * len(grid)))(*args, *comm.inputs)
    return list(outs[:n_out]), list(outs[n_out:])


def run_copies(comm, name):
    return _pallas(None, name=name, args=[], in_specs=[], out_specs=[], out_shape=[], comm=comm)[1]


def norm_matmul(x, g, wg, name, comm=None):
    s, d = x.shape
    ns = wg.shape[-1]
    tm = _tile(s, 1024)

    def body(x_ref, g_ref, w_ref, z_ref, h_ref):
        @pl.when(pl.program_id(1) == 0)
        def _():
            xv = x_ref[...]
            h_ref[...] = (xv * _rms_r(xv) * g_ref[...]).astype(BF16)
        z_ref[...] = _dot(h_ref[...], w_ref[...])

    return _pallas(
        body, name=name, grid=(s // tm, N_DEV), args=[x, g, wg], comm=comm,
        in_specs=[pl.BlockSpec((tm, d), lambda i, j: (i, 0)),
                  pl.BlockSpec((1, d), lambda i, j: (0, 0)),
                  pl.BlockSpec((None, d, ns), lambda i, j: (j, 0, 0))],
        out_specs=[pl.BlockSpec((tm, ns), lambda i, j: (i, j)),
                   pl.BlockSpec((tm, d), lambda i, j: (i, 0))],
        out_shape=[jax.ShapeDtypeStruct((s, N_DEV * ns), F32), jax.ShapeDtypeStruct((s, d), BF16)],
        sem=("parallel", "arbitrary"))


def mlp_forward(x, g, w1g, w2g, name, comm=None):
    s, d = x.shape
    fs = w1g.shape[-1]
    tm = _tile(s, 1024)

    def body(x_ref, g_ref, w1_ref, w2_ref, xo_ref, a_ref, h_ref):
        @pl.when(pl.program_id(1) == 0)
        def _():
            xv = x_ref[...]
            h_ref[...] = (xv * _rms_r(xv) * g_ref[...]).astype(BF16)
            xo_ref[...] = xv
        a = _dot(h_ref[...], w1_ref[...])
        a_ref[...] = a.astype(BF16)
        r = jnp.maximum(a, 0.0)
        xo_ref[...] += _dot((r * r).astype(BF16), w2_ref[...])

    return _pallas(
        body, name=name, grid=(s // tm, N_DEV), args=[x, g, w1g, w2g], comm=comm,
        in_specs=[pl.BlockSpec((tm, d), lambda i, j: (i, 0)),
                  pl.BlockSpec((1, d), lambda i, j: (0, 0)),
                  pl.BlockSpec((None, d, fs), lambda i, j: (j, 0, 0)),
                  pl.BlockSpec((None, fs, d), lambda i, j: (j, 0, 0))],
        out_specs=[pl.BlockSpec((tm, d), lambda i, j: (i, 0)),
                   pl.BlockSpec((tm, fs), lambda i, j: (i, j)),
                   pl.BlockSpec((tm, d), lambda i, j: (i, 0))],
        out_shape=[jax.ShapeDtypeStruct((s, d), F32), jax.ShapeDtypeStruct((s, N_DEV * fs), BF16),
                   jax.ShapeDtypeStruct((s, d), BF16)],
        sem=("parallel", "arbitrary"))


def _norm_backward(dh, xv, g, dres):
    r = _rms_r(xv)
    xh = xv * r
    t = dh * g
    dx = dres + r * (t - xh * jnp.mean(t * xh, axis=-1, keepdims=True))
    return dx, jnp.sum(dh * xh, axis=0, keepdims=True)


def mlp_backward(dy, a, x, g, w1g, w2g, name, comm=None):
    s, d = x.shape
    fs = w1g.shape[-1]
    tm = _tile(s, 1024)

    def body(dy_ref, a_ref, x_ref, g_ref, w1_ref, w2_ref, dx_ref, da_ref, dg_ref, dyb_ref, dh_ref):
        i, j = pl.program_id(0), pl.program_id(1)

        @pl.when(j == 0)
        def _():
            dyb_ref[...] = dy_ref[...].astype(BF16)
            dh_ref[...] = jnp.zeros_like(dh_ref)

        dr = _dot_nt(dyb_ref[...], w2_ref[...])
        da = (dr * (2.0 * jnp.maximum(a_ref[...].astype(F32), 0.0))).astype(BF16)
        da_ref[...] = da
        dh_ref[...] += _dot_nt(da, w1_ref[...])

        @pl.when(j == N_DEV - 1)
        def _():
            dx, dgp = _norm_backward(dh_ref[...], x_ref[...], g_ref[...], dy_ref[...])
            dx_ref[...] = dx

            @pl.when(i == 0)
            def _():
                dg_ref[...] = dgp

            @pl.when(i > 0)
            def _():
                dg_ref[...] += dgp

    return _pallas(
        body, name=name, grid=(s // tm, N_DEV), args=[dy, a, x, g, w1g, w2g], comm=comm,
        in_specs=[pl.BlockSpec((tm, d), lambda i, j: (i, 0)),
                  pl.BlockSpec((tm, fs), lambda i, j: (i, j)),
                  pl.BlockSpec((tm, d), lambda i, j: (i, 0)),
                  pl.BlockSpec((1, d), lambda i, j: (0, 0)),
                  pl.BlockSpec((None, d, fs), lambda i, j: (j, 0, 0)),
                  pl.BlockSpec((None, fs, d), lambda i, j: (j, 0, 0))],
        out_specs=[pl.BlockSpec((tm, d), lambda i, j: (i, 0)),
                   pl.BlockSpec((tm, fs), lambda i, j: (i, j)),
                   pl.BlockSpec((1, d), lambda i, j: (0, 0))],
        out_shape=[jax.ShapeDtypeStruct((s, d), F32), jax.ShapeDtypeStruct((s, N_DEV * fs), BF16),
                   jax.ShapeDtypeStruct((1, d), F32)],
        scratch=[pltpu.VMEM((tm, d), BF16), pltpu.VMEM((tm, d), F32)],
        sem=("arbitrary", "arbitrary"))


def matmul_tn(a, b, shard, name, relu2=False, comm=None):
    s, m = a.shape
    n = b.shape[1]
    ts = _tile(s, 1024)
    if shard == "cols":
        bm, bn = m, n // N_DEV
        a_map, b_map = (lambda j, k: (k, 0)), (lambda j, k: (k, j))
    else:
        bm, bn = m // N_DEV, n
        a_map, b_map = (lambda j, k: (k, j)), (lambda j, k: (k, 0))

    def body(a_ref, b_ref, o_ref, acc_ref):
        k = pl.program_id(1)
        av = a_ref[...]
        if relu2:
            af = jnp.maximum(av.astype(F32), 0.0)
            av = af * af
        p = _dot_tn(av.astype(BF16), b_ref[...].astype(BF16))

        @pl.when(k == 0)
        def _():
            acc_ref[...] = p

        @pl.when(k > 0)
        def _():
            acc_ref[...] += p

        @pl.when(k == s // ts - 1)
        def _():
            o_ref[...] = acc_ref[...].astype(BF16)

    outs, couts = _pallas(
        body, name=name, grid=(N_DEV, s // ts), args=[a, b], comm=comm,
        in_specs=[pl.BlockSpec((ts, bm), a_map), pl.BlockSpec((ts, bn), b_map)],
        out_specs=[pl.BlockSpec((None, bm, bn), lambda j, k: (j, 0, 0))],
        out_shape=[jax.ShapeDtypeStruct((N_DEV, bm, bn), BF16)],
        scratch=[pltpu.VMEM((bm, bn), F32)],
        sem=("parallel", "arbitrary"))
    return outs[0], couts


def matmul_residual(x, y, w, name):
    s, n = x.shape
    k = y.shape[1]
    tm = _tile(s, 1024)

    def body(x_ref, y_ref, w_ref, o_ref):
        o_ref[...] = x_ref[...] + _dot(y_ref[...], w_ref[...])

    return pl.pallas_call(
        body, name=name, grid=(s // tm,),
        in_specs=[pl.BlockSpec((tm, n), lambda i: (i, 0)),
                  pl.BlockSpec((tm, k), lambda i: (i, 0)),
                  pl.BlockSpec((k, n), lambda i: (0, 0))],
        out_specs=pl.BlockSpec((tm, n), lambda i: (i, 0)),
        out_shape=jax.ShapeDtypeStruct((s, n), F32),
        compiler_params=_params("parallel"),
    )(x, y, w)


def matmul_nt(dy, wg, name):
    s, n = dy.shape
    k = wg.shape[0]
    tm = _tile(s, 1024)

    def body(dy_ref, w_ref, o_ref):
        o_ref[...] = _dot_nt(dy_ref[...].astype(BF16), w_ref[...])

    return pl.pallas_call(
        body, name=name, grid=(s // tm,),
        in_specs=[pl.BlockSpec((tm, n), lambda i: (i, 0)),
                  pl.BlockSpec((k, n), lambda i: (0, 0))],
        out_specs=pl.BlockSpec((tm, k), lambda i: (i, 0)),
        out_shape=jax.ShapeDtypeStruct((s, k), F32),
        compiler_params=_params("parallel"),
    )(dy, wg)


def matmul_nt_norm_backward(dz, wg, x, g, dres, name):
    s, d = x.shape
    ns = wg.shape[-1]
    tm = _tile(s, 1024)

    def body(dz_ref, w_ref, x_ref, g_ref, dres_ref, dx_ref, dg_ref, dh_ref):
        i, j = pl.program_id(0), pl.program_id(1)
        p = _dot_nt(dz_ref[...], w_ref[...])

        @pl.when(j == 0)
        def _():
            dh_ref[...] = p

        @pl.when(j > 0)
        def _():
            dh_ref[...] += p

        @pl.when(j == N_DEV - 1)
        def _():
            dx, dgp = _norm_backward(dh_ref[...], x_ref[...], g_ref[...], dres_ref[...])
            dx_ref[...] = dx

            @pl.when(i == 0)
            def _():
                dg_ref[...] = dgp

            @pl.when(i > 0)
            def _():
                dg_ref[...] += dgp

    return pl.pallas_call(
        body, name=name, grid=(s // tm, N_DEV),
        in_specs=[pl.BlockSpec((tm, ns), lambda i, j: (i, j)),
                  pl.BlockSpec((None, d, ns), lambda i, j: (j, 0, 0)),
                  pl.BlockSpec((tm, d), lambda i, j: (i, 0)),
                  pl.BlockSpec((1, d), lambda i, j: (0, 0)),
                  pl.BlockSpec((tm, d), lambda i, j: (i, 0))],
        out_specs=[pl.BlockSpec((tm, d), lambda i, j: (i, 0)),
                   pl.BlockSpec((1, d), lambda i, j: (0, 0))],
        out_shape=[jax.ShapeDtypeStruct((s, d), F32), jax.ShapeDtypeStruct((1, d), F32)],
        scratch_shapes=[pltpu.VMEM((tm, d), F32)],
        compiler_params=_params("arbitrary", "arbitrary"),
    )(dz, wg, x, g, dres)


def loss_and_grad(y, target, name):
    s, d = y.shape
    tm = _tile(s, 1024)

    def body(y_ref, t_ref, dy_ref, l_ref):
        e = y_ref[...] - t_ref[...]
        dy_ref[...] = e / d
        part = jnp.sum(e * e, axis=0, keepdims=True) * (0.5 / d)

        @pl.when(pl.program_id(0) == 0)
        def _():
            l_ref[...] = part

        @pl.when(pl.program_id(0) > 0)
        def _():
            l_ref[...] += part

    return pl.pallas_call(
        body, name=name, grid=(s // tm,),
        in_specs=[pl.BlockSpec((tm, d), lambda i: (i, 0)), pl.BlockSpec((tm, d), lambda i: (i, 0))],
        out_specs=[pl.BlockSpec((tm, d), lambda i: (i, 0)), pl.BlockSpec((1, d), lambda i: (0, 0))],
        out_shape=[jax.ShapeDtypeStruct((s, d), F32), jax.ShapeDtypeStruct((1, d), F32)],
        compiler_params=_params("arbitrary"),
    )(y, target)


def _layernorm(x, g, b):
    mu = jnp.mean(x, axis=-1, keepdims=True)
    xc = x - mu
    rstd = lax.rsqrt(jnp.mean(xc * xc, axis=-1, keepdims=True) + EPS)
    xn = xc * rstd
    return xn * g + b, xn, rstd


def _layernorm_backward(dy, xn, rstd, g):
    dxn = dy * g
    return rstd * (dxn - jnp.mean(dxn, axis=-1, keepdims=True) - xn * jnp.mean(dxn * xn, axis=-1, keepdims=True))


def _group_halves(x_ref, jp, nch):
    blk = jnp.concatenate([x_ref[c * CHUNK:(c + 1) * CHUNK, jp * LANES:(jp + 1) * LANES] for c in range(nch)], axis=1)
    low = (lax.broadcasted_iota(jnp.int32, blk.shape, 1) % LANES) < HEAD_DIM
    return jnp.where(low, blk, 0.0).astype(BF16), jnp.where(low, 0.0, blk).astype(BF16)


def _spatial_apply(src_ref, w_ref, dst_ref, nch, bias_ref=None):
    for jp in range(4):
        lo, hi = _group_halves(src_ref, jp, nch)
        r = _dot(w_ref[2 * jp], lo) + _dot(w_ref[2 * jp + 1], hi)
        for c in range(nch):
            v = r[:, c * LANES:(c + 1) * LANES]
            if bias_ref is not None:
                v = v + bias_ref[:, jp * LANES:(jp + 1) * LANES]
            dst_ref[c * CHUNK:(c + 1) * CHUNK, jp * LANES:(jp + 1) * LANES] = v


def _glu(zb):
    w = zb.shape[1] // 2
    return zb[:, :w] * _sigmoid(zb[:, w:])


def _fill_padded(pad_ref, prev, cur, nxt, i, nt, tm):
    pad_ref[0:CONV_HALO, :] = jnp.where(i > 0, prev, 0.0)
    pad_ref[CONV_HALO:CONV_HALO + tm, :] = cur
    pad_ref[CONV_HALO + tm:2 * CONV_HALO + tm, :] = jnp.where(i < nt - 1, nxt, 0.0)


def _halo_specs(tm, s, width, col):
    hb, nhb = tm // CONV_HALO, s // CONV_HALO
    return [pl.BlockSpec((tm, width), lambda i: (i, col)),
            pl.BlockSpec((CONV_HALO, width), lambda i: (jnp.maximum(i * hb - 1, 0), col)),
            pl.BlockSpec((CONV_HALO, width), lambda i: (jnp.minimum((i + 1) * hb, nhb - 1), col))]


def _const_spec(shape):
    nd = len(shape)
    return pl.BlockSpec(shape, lambda i: (0,) * nd)


def ab_mid_forward(z, sp, name, comm=None):
    s = z.shape[0]
    aw = z.shape[1] // 4
    tm = _tile(s, 512)
    nch, nt = tm // CHUNK, s // tm

    def body(zu_ref, zv_ref, zb_ref, zp_ref, zn_ref, w_ref, bias_ref, vg_ref, vb_ref, cw_ref, cb_ref, cg_ref, cbn_ref,
             y_ref, gc_ref, vl_ref, sv_ref, pad_ref):
        i = pl.program_id(0)
        vl_ref[...] = _layernorm(_gelu(zv_ref[...]), vg_ref[...], vb_ref[...])[0]
        _spatial_apply(vl_ref, w_ref, sv_ref, nch, bias_ref)
        y_ref[:, :aw] = (_gelu(zu_ref[...]) * sv_ref[...]).astype(BF16)

        _fill_padded(pad_ref, _glu(zp_ref[...]), _glu(zb_ref[...]), _glu(zn_ref[...]), i, nt, tm)
        for rb in range(tm // CHUNK):
            for lb in range(aw // LANES):
                cols = pl.ds(lb * LANES, LANES)
                acc = jnp.broadcast_to(cb_ref[:, cols], (CHUNK, LANES))
                for k in range(CONV_W):
                    acc = acc + cw_ref[k:k + 1, cols] * pad_ref[pl.ds(rb * CHUNK + CONV_HALO - CONV_W // 2 + k, CHUNK), cols]
                gc_ref[rb * CHUNK:(rb + 1) * CHUNK, cols] = acc
        yl = _layernorm(gc_ref[...], cg_ref[...], cbn_ref[...])[0]
        y_ref[:, aw:] = (yl * _sigmoid(yl)).astype(BF16)

    return _pallas(
        body, name=name, grid=(nt,), comm=comm,
        args=[z, z, z, z, z, sp["w"], sp["bias"], sp["vg"], sp["vb"], sp["cw"], sp["cb"], sp["cg"], sp["cbn"]],
        in_specs=[pl.BlockSpec((tm, aw), lambda i: (i, 0)), pl.BlockSpec((tm, aw), lambda i: (i, 1))]
        + _halo_specs(tm, s, 2 * aw, 1)
        + [_const_spec(sp["w"].shape), _const_spec(sp["bias"].shape)]
        + [_const_spec((1, aw))] * 2 + [_const_spec(sp["cw"].shape)] + [_const_spec((1, aw))] * 3,
        out_specs=[pl.BlockSpec((tm, 2 * aw), lambda i: (i, 0)), pl.BlockSpec((tm, aw), lambda i: (i, 0))],
        out_shape=[jax.ShapeDtypeStruct((s, 2 * aw), BF16), jax.ShapeDtypeStruct((s, aw), F32)],
        scratch=[pltpu.VMEM((tm, aw), F32), pltpu.VMEM((tm, aw), F32), pltpu.VMEM((tm + 2 * CONV_HALO, aw), F32)],
        sem=("parallel",))


def _accumulate(ref, val, first):
    @pl.when(first)
    def _():
        ref[...] = val

    @pl.when(jnp.logical_not(first))
    def _():
        ref[...] += val


def ab_mid_backward(dy, z, gc, sp, name):
    s = z.shape[0]
    aw = z.shape[1] // 4
    tm = _tile(s, 512)
    nch, nt = tm // CHUNK, s // tm

    def body(dya_ref, dyb_ref, zu_ref, zv_ref, gc_ref, w_ref, wt_ref, bias_ref, vg_ref, vb_ref, cg_ref, cbn_ref,
             dz_ref, dgc_ref, dw_ref, dsb_ref, dvg_ref, dvb_ref, dcg_ref, dcbn_ref, dcb_ref,
             vl_ref, sv_ref, dsv_ref, dvl_ref):
        first = pl.program_id(0) == 0
        zu, zv = zu_ref[...], zv_ref[...]
        u = _gelu(zu)
        vl, vn, vrstd = _layernorm(_gelu(zv), vg_ref[...], vb_ref[...])
        vl_ref[...] = vl
        _spatial_apply(vl_ref, w_ref, sv_ref, nch, bias_ref)
        dya = dya_ref[...]
        dz_ref[:, :aw] = (dya * sv_ref[...] * _gelu_grad(zu)).astype(BF16)
        dsv = dya * u
        dsv_ref[...] = dsv
        _spatial_apply(dsv_ref, wt_ref, dvl_ref, nch)

        for jp in range(4):
            dlo, dhi = _group_halves(dsv_ref, jp, nch)
            vlo, vhi = _group_halves(vl_ref, jp, nch)
            vall = vlo + vhi
            _accumulate(dw_ref.at[2 * jp], _dot_nt(dlo, vall), first)
            _accumulate(dw_ref.at[2 * jp + 1], _dot_nt(dhi, vall), first)
        rows = dsv[0:CHUNK]
        for c in range(1, nch):
            rows = rows + dsv[c * CHUNK:(c + 1) * CHUNK]
        grp = lax.broadcasted_iota(jnp.int32, (8, aw), 0) == lax.broadcasted_iota(jnp.int32, (8, aw), 1) // HEAD_DIM
        e = jnp.where(grp, 1.0, 0.0).astype(BF16)
        hi = rows.astype(BF16)
        r1 = rows - hi.astype(F32)
        mid = r1.astype(BF16)
        lo = (r1 - mid.astype(F32)).astype(BF16)
        _accumulate(dsb_ref, _dot_nt(e, hi) + _dot_nt(e, mid) + _dot_nt(e, lo), first)

        dvl = dvl_ref[...]
        _accumulate(dvg_ref, jnp.sum(dvl * vn, axis=0, keepdims=True), first)
        _accumulate(dvb_ref, jnp.sum(dvl, axis=0, keepdims=True), first)
        dz_ref[:, aw:] = (_layernorm_backward(dvl, vn, vrstd, vg_ref[...]) * _gelu_grad(zv)).astype(BF16)

        yl, yn, yrstd = _layernorm(gc_ref[...], cg_ref[...], cbn_ref[...])
        sg = _sigmoid(yl)
        dyl = dyb_ref[...] * (sg + yl * sg * (1.0 - sg))
        _accumulate(dcg_ref, jnp.sum(dyl * yn, axis=0, keepdims=True), first)
        _accumulate(dcbn_ref, jnp.sum(dyl, axis=0, keepdims=True), first)
        dgc = _layernorm_backward(dyl, yn, yrstd, cg_ref[...])
        dgc_ref[...] = dgc
        _accumulate(dcb_ref, jnp.sum(dgc, axis=0, keepdims=True), first)

    vec = jax.ShapeDtypeStruct((1, aw), F32)
    return pl.pallas_call(
        body, name=name, grid=(nt,),
        in_specs=[pl.BlockSpec((tm, aw), lambda i: (i, 0)), pl.BlockSpec((tm, aw), lambda i: (i, 1)),
                  pl.BlockSpec((tm, aw), lambda i: (i, 0)), pl.BlockSpec((tm, aw), lambda i: (i, 1)),
                  pl.BlockSpec((tm, aw), lambda i: (i, 0)),
                  _const_spec(sp["w"].shape), _const_spec(sp["w"].shape), _const_spec(sp["bias"].shape)]
        + [_const_spec((1, aw))] * 4,
        out_specs=[pl.BlockSpec((tm, 2 * aw), lambda i: (i, 0)), pl.BlockSpec((tm, aw), lambda i: (i, 0)),
                   _const_spec(sp["w"].shape), _const_spec((8, CHUNK))] + [_const_spec((1, aw))] * 5,
        out_shape=[jax.ShapeDtypeStruct((s, 4 * aw), BF16), jax.ShapeDtypeStruct((s, aw), F32),
                   jax.ShapeDtypeStruct(sp["w"].shape, F32), jax.ShapeDtypeStruct((8, CHUNK), F32)] + [vec] * 5,
        scratch_shapes=[pltpu.VMEM((tm, aw), F32)] * 4,
        compiler_params=_params("arbitrary"),
    )(dy, dy, z, z, gc, sp["w"], sp["wt"], sp["bias"], sp["vg"], sp["vb"], sp["cg"], sp["cbn"])


def conv_backward(dgc, z, dz_in, sp, name):
    s = z.shape[0]
    aw = z.shape[1] // 4
    tm = _tile(s, 512)
    nt = s // tm
    off = CONV_HALO - CONV_W // 2

    def body(d_ref, dp_ref, dn_ref, zb_ref, zp_ref, zn_ref, cw_ref, dzin_ref, dz_ref, dcw_ref, padd_ref, padg_ref, dgg_ref):
        i = pl.program_id(0)
        _fill_padded(padd_ref, dp_ref[...], d_ref[...], dn_ref[...], i, nt, tm)
        _fill_padded(padg_ref, _glu(zp_ref[...]), _glu(zb_ref[...]), _glu(zn_ref[...]), i, nt, tm)

        @pl.when(i == 0)
        def _():
            dcw_ref[...] = jnp.zeros_like(dcw_ref)

        for lb in range(aw // LANES):
            cols = pl.ds(lb * LANES, LANES)
            for rb in range(tm // CHUNK):
                acc = jnp.zeros((CHUNK, LANES), F32)
                for k in range(CONV_W):
                    acc = acc + cw_ref[k:k + 1, cols] * padd_ref[pl.ds(rb * CHUNK + CONV_HALO + CONV_W // 2 - k, CHUNK), cols]
                dgg_ref[rb * CHUNK:(rb + 1) * CHUNK, cols] = acc
            for k in range(CONV_W):
                part = jnp.zeros((1, LANES), F32)
                for rb in range(tm // CHUNK):
                    prod = d_ref[rb * CHUNK:(rb + 1) * CHUNK, cols] * padg_ref[pl.ds(rb * CHUNK + off + k, CHUNK), cols]
                    part = part + jnp.sum(prod, axis=0, keepdims=True)
                dcw_ref[k:k + 1, cols] += part

        zb = zb_ref[...]
        val, sg = zb[:, :aw], _sigmoid(zb[:, aw:])
        dgg = dgg_ref[...]
        dz_ref[:, :aw] = (dgg * sg).astype(BF16)
        dz_ref[:, aw:] = (dgg * val * sg * (1.0 - sg)).astype(BF16)

    return pl.pallas_call(
        body, name=name, grid=(nt,),
        in_specs=_halo_specs(tm, s, aw, 0) + _halo_specs(tm, s, 2 * aw, 1)
        + [_const_spec(sp["cw"].shape), pl.BlockSpec(memory_space=pl.ANY)],
        out_specs=[pl.BlockSpec((tm, 2 * aw), lambda i: (i, 1)), _const_spec(sp["cw"].shape)],
        out_shape=[jax.ShapeDtypeStruct((s, 4 * aw), BF16), jax.ShapeDtypeStruct(sp["cw"].shape, F32)],
        scratch_shapes=[pltpu.VMEM((tm + 2 * CONV_HALO, aw), F32)] * 2 + [pltpu.VMEM((tm, aw), F32)],
        input_output_aliases={7: 0},
        compiler_params=_params("arbitrary"),
    )(dgc, dgc, dgc, z, z, z, sp["cw"], dz_in)


def rope_tables(s):
    pos = jnp.arange(s, dtype=F32)
    inv_freq = ROPE_THETA ** (-jnp.arange(0, ROT_DIM, 2, dtype=F32) / ROT_DIM)
    ang = pos[:, None] * inv_freq[None, :]
    cos, sin = jnp.cos(ang), jnp.sin(ang)
    half = ROT_DIM // 2
    rest = HEAD_DIM - ROT_DIM
    one, zero, zrest = jnp.ones((s, rest), F32), jnp.zeros((s, half), F32), jnp.zeros((s, rest), F32)
    c = jnp.concatenate([cos, cos, one], axis=1)
    s1 = jnp.concatenate([-sin, zero, zrest], axis=1)
    s2 = jnp.concatenate([zero, sin, zrest], axis=1)
    return tuple(jnp.tile(t, (1, LANES // HEAD_DIM)) for t in (c, s1, s2))


def qk_prep_forward(qkv, tabs, gains, name, comm=None):
    s, w3 = qkv.shape
    w = w3 // 3
    tm = _tile(s, 512)

    def body(x_ref, c_ref, s1_ref, s2_ref, g_ref, o_ref):
        bd = _block_diag(LANES)
        for b in range(w // LANES):
            cols = pl.ds(b * LANES, LANES)
            t = x_ref[:, cols]
            r = lax.rsqrt(_seg_sum(t * t, bd) * (1.0 / HEAD_DIM) + EPS)
            y = t * r * g_ref[...]
            o_ref[:, cols] = (y * c_ref[...] + pltpu.roll(y, LANES - ROT_DIM // 2, 1) * s1_ref[...]
                              + pltpu.roll(y, ROT_DIM // 2, 1) * s2_ref[...])

    tab = pl.BlockSpec((tm, LANES), lambda i, p: (i, 0))
    outs, couts = _pallas(
        body, name=name, grid=(s // tm, 2), args=[qkv, *tabs, gains], comm=comm,
        in_specs=[pl.BlockSpec((tm, w), lambda i, p: (i, p)), tab, tab, tab,
                  pl.BlockSpec((None, 1, LANES), lambda i, p: (p, 0, 0))],
        out_specs=[pl.BlockSpec((tm, w), lambda i, p: (i, p))],
        out_shape=[jax.ShapeDtypeStruct((s, 2 * w), F32)],
        sem=("parallel", "arbitrary"))
    return outs[0], couts


def qk_prep_backward(dq, dk, dv, qkv, tabs, gains, name):
    s, w3 = qkv.shape
    w = w3 // 3
    tm = _tile(s, 512)

    def body(*refs):
        grads = ((refs[0],), (refs[1],), (refs[2],))
        x_ref, c_ref, s1_ref, s2_ref, g_ref, o_ref, dg_ref = refs[3:]
        part, first = pl.program_id(0), pl.program_id(1) == 0

        def normed(ds):
            bd = _block_diag(LANES)
            acc = jnp.zeros((1, LANES), F32)
            for b in range(w // LANES):
                cols = pl.ds(b * LANES, LANES)
                dout = ds[0][:, cols]
                dy = (dout * c_ref[...] + pltpu.roll(dout * s1_ref[...], ROT_DIM // 2, 1)
                      + pltpu.roll(dout * s2_ref[...], LANES - ROT_DIM // 2, 1))
                t = x_ref[:, cols]
                r = lax.rsqrt(_seg_sum(t * t, bd) * (1.0 / HEAD_DIM) + EPS)
                xh = t * r
                acc = acc + jnp.sum(dy * xh, axis=0, keepdims=True)
                tt = dy * g_ref[...]
                o_ref[:, cols] = (r * (tt - xh * (_seg_sum(tt * xh, bd) * (1.0 / HEAD_DIM)))).astype(BF16)
            _accumulate(dg_ref, acc, first)

        for p in range(2):
            pl.when(part == p)(functools.partial(normed, grads[p]))

        @pl.when(part == 2)
        def _():
            o_ref[...] = grads[2][0][...].astype(BF16)
            _accumulate(dg_ref, jnp.zeros((1, LANES), F32), first)

    def gspec(p):
        return pl.BlockSpec((tm, w), lambda q, i: (jnp.where(q == p, i, 0), 0))

    tab = pl.BlockSpec((tm, LANES), lambda q, i: (i, 0))
    return pl.pallas_call(
        body, name=name, grid=(3, s // tm),
        in_specs=[gspec(0), gspec(1), gspec(2)]
        + [pl.BlockSpec((tm, w), lambda q, i: (i, q)), tab, tab, tab,
           pl.BlockSpec((None, 1, LANES), lambda q, i: (q, 0, 0))],
        out_specs=[pl.BlockSpec((tm, w), lambda q, i: (i, q)), pl.BlockSpec((None, 1, LANES), lambda q, i: (q, 0, 0))],
        out_shape=[jax.ShapeDtypeStruct((s, w3), BF16), jax.ShapeDtypeStruct((3, 1, LANES), F32)],
        compiler_params=_params("arbitrary", "arbitrary"),
    )(dq, dk, dv, qkv, *tabs, gains)


def _window_specs(tq, l, col_fn):
    hb, nhb = tq // BAND, l // BAND
    return [pl.BlockSpec((tq, LANES), lambda c, i: (i, col_fn(c))),
            pl.BlockSpec((BAND, LANES), lambda c, i: (jnp.maximum(i * hb - 1, 0), col_fn(c))),
            pl.BlockSpec((BAND, LANES), lambda c, i: (jnp.minimum((i + 1) * hb, nhb - 1), col_fn(c)))]


def _window(cur_ref, prev_ref, next_ref):
    return jnp.concatenate([prev_ref[...], cur_ref[...], next_ref[...]], axis=0)


def _band_mask(shape, centre_axis, first_row, length):
    ctr = lax.broadcasted_iota(jnp.int32, shape, centre_axis)
    win = lax.broadcasted_iota(jnp.int32, shape, 1 - centre_axis)
    row = first_row + win
    return (jnp.abs(win - BAND - ctr) <= BAND) & (row >= 0) & (row < length)


def _col_q(d):
    return lambda c: (c // 8) * 24 + c % 8


def _col_k(d):
    return lambda c: (c // 8) * 24 + 8 + c % 8


def _col_v(d):
    return lambda c: (c // 8) * 24 + 16 + c % 8


def band_attention_forward(qkvn, d, name):
    s, w3 = qkvn.shape
    w = w3 // 3
    l = s // d
    tq = _tile(l, 512)
    scale = HEAD_DIM ** -0.5
    xv = qkvn.reshape(l, d * w3)

    def body(q_ref, k_ref, kp_ref, kn_ref, v_ref, vp_ref, vn_ref, o_ref, lse_ref):
        i = pl.program_id(1)
        kw, vw = _window(k_ref, kp_ref, kn_ref), _window(v_ref, vp_ref, vn_ref)
        head0 = lax.broadcasted_iota(jnp.int32, (CHUNK, LANES), 1) < HEAD_DIM
        for b in range(tq // CHUNK):
            rows = pl.ds(b * CHUNK, CHUNK)
            mask = _band_mask((CHUNK, 2 * CHUNK), 0, i * tq + b * CHUNK - BAND, l)
            qb = q_ref[rows, :]
            kb, vb = kw[b * CHUNK:(b + 2) * CHUNK], vw[b * CHUNK:(b + 2) * CHUNK]
            outs, lses = [], []
            for hm in (head0, jnp.logical_not(head0)):
                sc = jnp.where(mask, _dot_nt(jnp.where(hm, qb, jnp.zeros_like(qb)), kb) * scale, NEG)
                m = jnp.max(sc, axis=1, keepdims=True)
                p = jnp.exp(sc - m)
                den = jnp.sum(p, axis=1, keepdims=True)
                outs.append(_dot(p.astype(BF16), vb) / den)
                lses.append(jnp.broadcast_to(m + jnp.log(den), (CHUNK, LANES)))
            o_ref[rows, :] = jnp.where(head0, outs[0], outs[1]).astype(BF16)
            lse_ref[rows, :] = jnp.where(head0, lses[0], lses[1])

    ospec = pl.BlockSpec((tq, LANES), lambda c, i: (i, c))
    o, lse = pl.pallas_call(
        body, name=name, grid=(d * w // LANES, l // tq),
        in_specs=[pl.BlockSpec((tq, LANES), lambda c, i: (i, _col_q(d)(c)))]
        + _window_specs(tq, l, _col_k(d)) + _window_specs(tq, l, _col_v(d)),
        out_specs=[ospec, ospec],
        out_shape=[jax.ShapeDtypeStruct((l, d * w), BF16), jax.ShapeDtypeStruct((l, d * w), F32)],
        compiler_params=_params("parallel", "parallel"),
    )(xv, xv, xv, xv, xv, xv, xv)
    return o.reshape(s, w), lse.reshape(s, w)


def attention_merge(os_, lses, name):
    s, w = os_[0].shape
    tm = _tile(s, 512)

    def body(o0, o1, o2, l0, l1, l2, o_ref, lse_ref):
        la, lb, lc = l0[...], l1[...], l2[...]
        m = jnp.maximum(jnp.maximum(la, lb), lc)
        wa, wb, wc = jnp.exp(la - m), jnp.exp(lb - m), jnp.exp(lc - m)
        den = wa + wb + wc
        o = (wa * o0[...].astype(F32) + wb * o1[...].astype(F32) + wc * o2[...].astype(F32)) / den
        o_ref[...] = o.astype(BF16)
        lse_ref[...] = m + jnp.log(den)

    spec = pl.BlockSpec((tm, w), lambda i: (i, 0))
    return pl.pallas_call(
        body, name=name, grid=(s // tm,), in_specs=[spec] * 6, out_specs=[spec, spec],
        out_shape=[jax.ShapeDtypeStruct((s, w), BF16), jax.ShapeDtypeStruct((s, w), F32)],
        compiler_params=_params("parallel"),
    )(*os_, *lses)


def attention_delta(do, o, name):
    s, w = do.shape
    tm = _tile(s, 512)

    def body(do_ref, o_ref, dl_ref, dob_ref):
        bd = _block_diag(LANES)
        for b in range(w // LANES):
            cols = pl.ds(b * LANES, LANES)
            dv = do_ref[:, cols]
            dl_ref[:, cols] = _seg_sum(dv * o_ref[:, cols].astype(F32), bd)
            dob_ref[:, cols] = dv.astype(BF16)

    spec = pl.BlockSpec((tm, w), lambda i: (i, 0))
    return pl.pallas_call(
        body, name=name, grid=(s // tm,), in_specs=[spec, spec], out_specs=[spec, spec],
        out_shape=[jax.ShapeDtypeStruct((s, w), F32), jax.ShapeDtypeStruct((s, w), BF16)],
        compiler_params=_params("parallel"),
    )(do, o)


def band_attention_dq(qkvn, dob, lse, delta, d, name):
    s, w3 = qkvn.shape
    w = w3 // 3
    l = s // d
    tq = _tile(l, 512)
    scale = HEAD_DIM ** -0.5
    xv = qkvn.reshape(l, d * w3)

    def body(q_ref, k_ref, kp_ref, kn_ref, v_ref, vp_ref, vn_ref, do_ref, lse_ref, dl_ref, dq_ref):
        i = pl.program_id(1)
        kw, vw = _window(k_ref, kp_ref, kn_ref), _window(v_ref, vp_ref, vn_ref)
        head0 = lax.broadcasted_iota(jnp.int32, (CHUNK, LANES), 1) < HEAD_DIM
        for b in range(tq // CHUNK):
            rows = pl.ds(b * CHUNK, CHUNK)
            mask = _band_mask((CHUNK, 2 * CHUNK), 0, i * tq + b * CHUNK - BAND, l)
            qb, dob_ = q_ref[rows, :], do_ref[rows, :]
            kb, vb = kw[b * CHUNK:(b + 2) * CHUNK], vw[b * CHUNK:(b + 2) * CHUNK]
            outs = []
            for h, hm in enumerate((head0, jnp.logical_not(head0))):
                col = pl.ds(h * HEAD_DIM, 1)
                sc = jnp.where(mask, _dot_nt(jnp.where(hm, qb, jnp.zeros_like(qb)), kb) * scale, NEG)
                p = jnp.exp(sc - lse_ref[rows, col])
                dp = _dot_nt(jnp.where(hm, dob_, jnp.zeros_like(dob_)), vb)
                ds = p * (dp - dl_ref[rows, col]) * scale
                outs.append(_dot(ds.astype(BF16), kb))
            dq_ref[rows, :] = jnp.where(head0, outs[0], outs[1])

    ospec = pl.BlockSpec((tq, LANES), lambda c, i: (i, c))
    dq = pl.pallas_call(
        body, name=name, grid=(d * w // LANES, l // tq),
        in_specs=[pl.BlockSpec((tq, LANES), lambda c, i: (i, _col_q(d)(c)))]
        + _window_specs(tq, l, _col_k(d)) + _window_specs(tq, l, _col_v(d)) + [ospec, ospec, ospec],
        out_specs=ospec,
        out_shape=jax.ShapeDtypeStruct((l, d * w), F32),
        compiler_params=_params("parallel", "parallel"),
    )(xv, xv, xv, xv, xv, xv, xv, dob.reshape(l, d * w), lse.reshape(l, d * w), delta.reshape(l, d * w))
    return dq.reshape(s, w)


def band_attention_dkv(qkvn, dob, lse, delta, d, name):
    s, w3 = qkvn.shape
    w = w3 // 3
    l = s // d
    tq = _tile(l, 512)
    scale = HEAD_DIM ** -0.5
    xv = qkvn.reshape(l, d * w3)

    def body(k_ref, v_ref, q_ref, qp_ref, qn_ref, do_ref, dop_ref, don_ref, lse_ref, lsep_ref, lsen_ref,
             dl_ref, dlp_ref, dln_ref, dk_ref, dv_ref):
        i = pl.program_id(1)
        qw, dow = _window(q_ref, qp_ref, qn_ref), _window(do_ref, dop_ref, don_ref)
        lsew, dlw = _window(lse_ref, lsep_ref, lsen_ref), _window(dl_ref, dlp_ref, dln_ref)
        head0 = lax.broadcasted_iota(jnp.int32, (2 * CHUNK, LANES), 1) < HEAD_DIM
        for b in range(tq // CHUNK):
            rows = pl.ds(b * CHUNK, CHUNK)
            mask = _band_mask((2 * CHUNK, CHUNK), 1, i * tq + b * CHUNK - BAND, l)
            kb, vb = k_ref[rows, :], v_ref[rows, :]
            win = slice(b * CHUNK, (b + 2) * CHUNK)
            qb, dob_, lseb, dlb = qw[win], dow[win], lsew[win], dlw[win]
            dk = jnp.zeros((CHUNK, LANES), F32)
            dv = jnp.zeros((CHUNK, LANES), F32)
            for h, hm in enumerate((head0, jnp.logical_not(head0))):
                col = slice(h * HEAD_DIM, h * HEAD_DIM + 1)
                qm = jnp.where(hm, qb, jnp.zeros_like(qb))
                dom = jnp.where(hm, dob_, jnp.zeros_like(dob_))
                sc = jnp.where(mask, _dot_nt(qm, kb) * scale, NEG)
                p = jnp.exp(sc - lseb[:, col])
                ds = p * (_dot_nt(dom, vb) - dlb[:, col]) * scale
                dv = dv + _dot_tn(p.astype(BF16), dom)
                dk = dk + _dot_tn(ds.astype(BF16), qm)
            dk_ref[rows, :] = dk
            dv_ref[rows, :] = dv

    ident = lambda c: c
    ospec = pl.BlockSpec((tq, LANES), lambda c, i: (i, c))
    dk, dv = pl.pallas_call(
        body, name=name, grid=(d * w // LANES, l // tq),
        in_specs=[pl.BlockSpec((tq, LANES), lambda c, i: (i, _col_k(d)(c))),
                  pl.BlockSpec((tq, LANES), lambda c, i: (i, _col_v(d)(c)))]
        + _window_specs(tq, l, _col_q(d)) + _window_specs(tq, l, ident) * 3,
        out_specs=[ospec, ospec],
        out_shape=[jax.ShapeDtypeStruct((l, d * w), F32)] * 2,
        compiler_params=_params("parallel", "parallel"),
    )(xv, xv, xv, xv, xv, *[dob.reshape(l, d * w)] * 3, *[lse.reshape(l, d * w)] * 3, *[delta.reshape(l, d * w)] * 3)
    return dk.reshape(s, w), dv.reshape(s, w)


ATT_TILE = 2048
ATT_HALO = BAND * max(DILATIONS)
ROWS_PER_COPY = 256


def _att_specs(s, t, col_fn, halo=True):
    hb, nhb = t // ATT_HALO, s // ATT_HALO
    specs = [pl.BlockSpec((t, LANES), lambda hp, i: (i, col_fn(hp)))]
    if halo:
        specs += [pl.BlockSpec((ATT_HALO, LANES), lambda hp, i: (jnp.maximum(i * hb - 1, 0), col_fn(hp))),
                  pl.BlockSpec((ATT_HALO, LANES), lambda hp, i: (jnp.minimum((i + 1) * hb, nhb - 1), col_fn(hp)))]
    return specs


def _gather_rows(dst_ref, dst_row, src_ref, start, count, stride, scale=None):
    for c in range(0, count, ROWS_PER_COPY):
        m = min(ROWS_PER_COPY, count - c)
        v = src_ref[pl.ds(start + c * stride, m, stride=stride), :]
        if scale is not None:
            v = v * scale
        dst_ref[dst_row + c:dst_row + c + m, :] = v.astype(dst_ref.dtype)


def _stage(dst_ref, cur_ref, d, t, scale=None):
    n = t // d
    for r in range(d):
        _gather_rows(dst_ref, r * n, cur_ref, r, n, d, scale)


def _stage_window(dst_ref, refs, d, t, scale=None, edges=None):
    cur_ref, prev_ref, next_ref = refs
    n = t // d
    nw = n + 2 * BAND
    for r in range(d):
        _gather_rows(dst_ref, r * nw, prev_ref, ATT_HALO - BAND * d + r, BAND, d, scale)
        _gather_rows(dst_ref, r * nw + BAND, cur_ref, r, n, d, scale)
        _gather_rows(dst_ref, r * nw + BAND + n, next_ref, r, BAND, d, scale)
    if edges is not None:
        first, last, value = edges
        fill = jnp.full((BAND, LANES), value, dst_ref.dtype)

        @pl.when(first)
        def _():
            for r in range(d):
                dst_ref[r * nw:r * nw + BAND, :] = fill

        @pl.when(last)
        def _():
            for r in range(d):
                dst_ref[r * nw + BAND + n:(r + 1) * nw, :] = fill


def _band_bias(rows, cols, centre_axis):
    shape = (rows // 2, cols)
    ctr = lax.broadcasted_iota(jnp.int32, shape, centre_axis)
    win = lax.broadcasted_iota(jnp.int32, shape, 1 - centre_axis)
    bias = jnp.where(jnp.abs(win - BAND - ctr) <= BAND, 0.0, NEG).astype(F32)
    return jnp.concatenate([bias, bias], axis=0)


def _window_bias(first_row, length):
    row = first_row + lax.broadcasted_iota(jnp.int32, (1, 2 * CHUNK), 1)
    return jnp.where((row >= 0) & (row < length), 0.0, NEG).astype(F32)


UNITS_PER_TRIP = 4


def _scatter_rows(dst_ref, src_ref, d, t, combine):
    n = t // d
    for r in range(d):
        for c in range(0, n, ROWS_PER_COPY):
            m = min(ROWS_PER_COPY, n - c)
            idx = pl.ds(r + c * d, m, stride=d)
            combine(idx, slice(r * n + c, r * n + c + m))


def _unit_rows(u, n):
    upr = n // CHUNK
    r = u // upr
    b = u - r * upr
    return pl.multiple_of(u * CHUNK, CHUNK), pl.multiple_of((u + r) * CHUNK, CHUNK), b * CHUNK - BAND


def _two_heads(x):
    head0 = lax.broadcasted_iota(jnp.int32, x.shape, 1) < HEAD_DIM
    zero = jnp.zeros_like(x)
    return jnp.concatenate([jnp.where(head0, x, zero), jnp.where(head0, zero, x)], axis=0)


def _head_columns(x):
    return jnp.concatenate([x[:, 0:1], x[:, HEAD_DIM:HEAD_DIM + 1]], axis=0)


def _merge_heads(x2):
    rows = x2.shape[0] // 2
    head0 = lax.broadcasted_iota(jnp.int32, (rows, LANES), 1) < HEAD_DIM
    return jnp.where(head0, jnp.broadcast_to(x2[:rows], (rows, LANES)), jnp.broadcast_to(x2[rows:], (rows, LANES)))


def _col(part):
    return lambda hp: part * 8 + hp


def attention_forward(qk, qkv, name):
    s, w2 = qk.shape
    w = w2 // 2
    t = _tile(s, ATT_TILE)
    scale = HEAD_DIM ** -0.5

    def body(q_ref, k_ref, kp_ref, kn_ref, v_ref, vp_ref, vn_ref, o_ref, lse_ref, qs_ref, ks_ref, vs_ref, os_ref, ls_ref, or_ref):
        i = pl.program_id(1)
        band = _band_bias(2 * CHUNK, 2 * CHUNK, 0)
        for pi, d in enumerate(DILATIONS):
            n = t // d
            _stage(qs_ref, q_ref, d, t, scale)
            _stage_window(ks_ref, (k_ref, kp_ref, kn_ref), d, t)
            _stage_window(vs_ref, (v_ref, vp_ref, vn_ref), d, t)

            def unit(u, carry, n=n, d=d):
                qrow, wrow, first = _unit_rows(u, n)
                kb, vb = ks_ref[pl.ds(wrow, 2 * CHUNK), :], vs_ref[pl.ds(wrow, 2 * CHUNK), :]
                sc = _dot_nt(_two_heads(qs_ref[pl.ds(qrow, CHUNK), :]), kb) + band + _window_bias(i * n + first, s // d)
                m = jnp.max(sc, axis=1, keepdims=True)
                p = jnp.exp(sc - m)
                den = jnp.sum(p, axis=1, keepdims=True)
                os_ref[pl.ds(qrow, CHUNK), :] = _merge_heads(_dot(p.astype(BF16), vb) / den)
                ls_ref[pl.ds(qrow, CHUNK), :] = _merge_heads(m + jnp.log(den))
                return carry

            lax.fori_loop(0, t // CHUNK, unit, 0, unroll=UNITS_PER_TRIP)

            if pi == 0:
                def assign(idx, rows):
                    or_ref[idx, :] = os_ref[rows, :]
                    lse_ref[idx, :] = ls_ref[rows, :]
                _scatter_rows(None, None, d, t, assign)
            else:
                def merge(idx, rows):
                    la, lb = lse_ref[idx, :], ls_ref[rows, :]
                    mx = jnp.maximum(la, lb)
                    wa, wb = jnp.exp(la - mx), jnp.exp(lb - mx)
                    den = wa + wb
                    or_ref[idx, :] = (wa * or_ref[idx, :] + wb * os_ref[rows, :]) / den
                    lse_ref[idx, :] = mx + jnp.log(den)
                _scatter_rows(None, None, d, t, merge)
        o_ref[...] = or_ref[...].astype(BF16)

    ospec = pl.BlockSpec((t, LANES), lambda hp, i: (i, hp))
    win_rows = t + 2 * ATT_HALO
    o, lse = pl.pallas_call(
        body, name=name, grid=(w // LANES, s // t),
        in_specs=_att_specs(s, t, _col(0), halo=False) + _att_specs(s, t, _col(1)) + _att_specs(s, t, _col(2)),
        out_specs=[ospec, ospec],
        out_shape=[jax.ShapeDtypeStruct((s, w), BF16), jax.ShapeDtypeStruct((s, w), F32)],
        scratch_shapes=[pltpu.VMEM((t, LANES), BF16), pltpu.VMEM((win_rows, LANES), BF16), pltpu.VMEM((win_rows, LANES), BF16),
                        pltpu.VMEM((t, LANES), F32), pltpu.VMEM((t, LANES), F32), pltpu.VMEM((t, LANES), F32)],
        compiler_params=_params("parallel", "parallel"),
    )(qk, qk, qk, qk, qkv, qkv, qkv)
    return o, lse


def attention_delta(do, o, lse, name):
    s, w = do.shape
    tm = _tile(s, 512)

    def body(do_ref, o_ref, lse_ref, dl_ref, st_ref):
        bd = _block_diag(LANES)
        lane = lax.broadcasted_iota(jnp.int32, (tm, LANES), 1)
        for b in range(w // LANES):
            cols = pl.ds(b * LANES, LANES)
            dl = _seg_sum(do_ref[:, cols] * o_ref[:, cols].astype(F32), bd)
            dl_ref[:, cols] = dl
            ls = lse_ref[:, cols]
            st_ref[:, cols] = jnp.where(lane == 0, ls, jnp.where(
                lane == 1, pltpu.roll(ls, HEAD_DIM - 1, 1), jnp.where(
                    lane == 2, pltpu.roll(dl, 2, 1), pltpu.roll(dl, HEAD_DIM + 3, 1))))

    spec = pl.BlockSpec((tm, w), lambda i: (i, 0))
    return pl.pallas_call(
        body, name=name, grid=(s // tm,), in_specs=[spec, spec, spec], out_specs=[spec, spec],
        out_shape=[jax.ShapeDtypeStruct((s, w), F32)] * 2, compiler_params=_params("parallel"),
    )(do, o, lse)


def attention_dq(qk, qkv, do, lse, delta, name):
    s, w2 = qk.shape
    w = w2 // 2
    t = _tile(s, ATT_TILE)
    scale = HEAD_DIM ** -0.5

    def body(q_ref, k_ref, kp_ref, kn_ref, v_ref, vp_ref, vn_ref, do_ref, lse_ref, dl_ref, dq_ref,
             qs_ref, ks_ref, vs_ref, dos_ref, ls_ref, dls_ref, dqs_ref):
        i = pl.program_id(1)
        band = _band_bias(2 * CHUNK, 2 * CHUNK, 0)
        for pi, d in enumerate(DILATIONS):
            n = t // d
            _stage(qs_ref, q_ref, d, t, scale)
            _stage(dos_ref, do_ref, d, t)
            _stage(ls_ref, lse_ref, d, t)
            _stage(dls_ref, dl_ref, d, t)
            _stage_window(ks_ref, (k_ref, kp_ref, kn_ref), d, t)
            _stage_window(vs_ref, (v_ref, vp_ref, vn_ref), d, t)

            def unit(u, carry, n=n, d=d):
                qrow, wrow, first = _unit_rows(u, n)
                rows = pl.ds(qrow, CHUNK)
                kb, vb = ks_ref[pl.ds(wrow, 2 * CHUNK), :], vs_ref[pl.ds(wrow, 2 * CHUNK), :]
                sc = _dot_nt(_two_heads(qs_ref[rows, :]), kb) + band + _window_bias(i * n + first, s // d)
                p = jnp.exp(sc - _head_columns(ls_ref[rows, :]))
                dp = _dot_nt(_two_heads(dos_ref[rows, :]), vb)
                ds = p * (dp - _head_columns(dls_ref[rows, :]))
                dqs_ref[rows, :] = _merge_heads(_dot(ds.astype(BF16), kb)) * scale
                return carry

            lax.fori_loop(0, t // CHUNK, unit, 0, unroll=UNITS_PER_TRIP)

            def add(idx, rows, pi=pi):
                dq_ref[idx, :] = dqs_ref[rows, :] if pi == 0 else dq_ref[idx, :] + dqs_ref[rows, :]
            _scatter_rows(None, None, d, t, add)

    ospec = pl.BlockSpec((t, LANES), lambda hp, i: (i, hp))
    win_rows = t + 2 * ATT_HALO
    return pl.pallas_call(
        body, name=name, grid=(w // LANES, s // t),
        in_specs=_att_specs(s, t, _col(0), halo=False) + _att_specs(s, t, _col(1)) + _att_specs(s, t, _col(2))
        + [ospec, ospec, ospec],
        out_specs=ospec,
        out_shape=jax.ShapeDtypeStruct((s, w), F32),
        scratch_shapes=[pltpu.VMEM((t, LANES), BF16), pltpu.VMEM((win_rows, LANES), BF16), pltpu.VMEM((win_rows, LANES), BF16),
                        pltpu.VMEM((t, LANES), BF16), pltpu.VMEM((t, LANES), F32), pltpu.VMEM((t, LANES), F32),
                        pltpu.VMEM((t, LANES), F32)],
        compiler_params=_params("parallel", "parallel"),
    )(qk, qk, qk, qk, qkv, qkv, qkv, do, lse, delta)


def attention_dkv(qk, qkv, do, stats, name):
    s, w2 = qk.shape
    w = w2 // 2
    t = _tile(s, ATT_TILE)
    scale = HEAD_DIM ** -0.5

    def body(k_ref, v_ref, q_ref, qp_ref, qn_ref, do_ref, dop_ref, don_ref, st_ref, stp_ref, stn_ref,
             dk_ref, dv_ref, ks_ref, vs_ref, qs_ref, dos_ref, sts_ref, dks_ref, dvs_ref):
        i = pl.program_id(1)
        key = lax.broadcasted_iota(jnp.int32, (CHUNK, 2 * CHUNK), 0)
        win_ = lax.broadcasted_iota(jnp.int32, (CHUNK, 2 * CHUNK), 1)
        half = jnp.where(jnp.abs(win_ - BAND - key) <= BAND, 0.0, NEG).astype(F32)
        band = jnp.concatenate([half, half], axis=1)
        edges = (i == 0, i == s // t - 1, -NEG)
        for pi, d in enumerate(DILATIONS):
            n = t // d
            _stage(ks_ref, k_ref, d, t)
            _stage(vs_ref, v_ref, d, t)
            _stage_window(qs_ref, (q_ref, qp_ref, qn_ref), d, t, scale)
            _stage_window(dos_ref, (do_ref, dop_ref, don_ref), d, t)
            _stage_window(sts_ref, (st_ref, stp_ref, stn_ref), d, t, edges=edges)

            def unit(u, carry, n=n, d=d):
                krow, wrow, _ = _unit_rows(u, n)
                rows, win = pl.ds(krow, CHUNK), pl.ds(wrow, 2 * CHUNK)
                q2, do2 = _two_heads(qs_ref[win, :]), _two_heads(dos_ref[win, :])
                st = jnp.transpose(sts_ref[win, :])
                lse2 = jnp.concatenate([st[0:1, :], st[1:2, :]], axis=1)
                dl2 = jnp.concatenate([st[2:3, :], st[3:4, :]], axis=1)
                p = jnp.exp(_dot_nt(ks_ref[rows, :], q2) + band - lse2)
                ds = p * (_dot_nt(vs_ref[rows, :], do2) - dl2)
                dvs_ref[rows, :] = _dot(p.astype(BF16), do2)
                dks_ref[rows, :] = _dot(ds.astype(BF16), q2)
                return carry

            lax.fori_loop(0, t // CHUNK, unit, 0, unroll=UNITS_PER_TRIP)

            def add(idx, rows, pi=pi):
                dk_ref[idx, :] = dks_ref[rows, :] if pi == 0 else dk_ref[idx, :] + dks_ref[rows, :]
                dv_ref[idx, :] = dvs_ref[rows, :] if pi == 0 else dv_ref[idx, :] + dvs_ref[rows, :]
            _scatter_rows(None, None, d, t, add)

    ident = lambda hp: hp
    ospec = pl.BlockSpec((t, LANES), lambda hp, i: (i, hp))
    win_rows = t + 2 * ATT_HALO
    return pl.pallas_call(
        body, name=name, grid=(w // LANES, s // t),
        in_specs=_att_specs(s, t, _col(1), halo=False) + _att_specs(s, t, _col(2), halo=False)
        + _att_specs(s, t, _col(0)) + _att_specs(s, t, ident) * 2,
        out_specs=[ospec, ospec],
        out_shape=[jax.ShapeDtypeStruct((s, w), F32)] * 2,
        scratch_shapes=[pltpu.VMEM((t, LANES), BF16), pltpu.VMEM((t, LANES), BF16),
                        pltpu.VMEM((win_rows, LANES), BF16), pltpu.VMEM((win_rows, LANES), BF16),
                        pltpu.VMEM((win_rows, LANES), F32),
                        pltpu.VMEM((t, LANES), F32), pltpu.VMEM((t, LANES), F32)],
        compiler_params=_params("parallel", "parallel"),
    )(qk, qkv, qk, qk, qk, do, do, do, stats, stats, stats)


def adamw_update(recvs, w, m, v, name):
    nl = len(recvs)
    r, c = recvs[0].shape[1:]
    tr = 256 if (r > 256 and r % 256 == 0) else r
    nt = r // tr
    c1 = 1.0 - ADAM_B1 ** ADAM_STEP
    c2 = 1.0 - ADAM_B2 ** ADAM_STEP

    def body(*refs):
        g_refs = refs[:nl]
        w_ref, m_ref, v_ref, go_ref, d_ref, mo_ref, vo_ref = refs[nl:]

        def update(g_ref):
            g = g_ref[0].astype(F32)
            for j in range(1, N_DEV):
                g = g + g_ref[j].astype(F32)
            mn = ADAM_B1 * m_ref[...] + (1.0 - ADAM_B1) * g
            vn = ADAM_B2 * v_ref[...] + (1.0 - ADAM_B2) * (g * g)
            go_ref[...] = g
            mo_ref[...] = mn
            vo_ref[...] = vn
            d_ref[...] = -ADAM_LR * ((mn / c1) / (jnp.sqrt(vn / c2) + ADAM_EPS) + ADAM_WD * w_ref[...])

        for layer in range(nl):
            pl.when(pl.program_id(0) == layer)(functools.partial(update, g_refs[layer]))

    def gspec(layer):
        return pl.BlockSpec((N_DEV, tr, c), lambda l, i: (0, jnp.where(l == layer, i, 0), 0))

    spec = pl.BlockSpec((tr, c), lambda l, i: (l * nt + i, 0))
    return pl.pallas_call(
        body, name=name, grid=(nl, nt),
        in_specs=[gspec(layer) for layer in range(nl)] + [spec, spec, spec],
        out_specs=[spec] * 4, out_shape=[jax.ShapeDtypeStruct((nl * r, c), F32)] * 4,
        compiler_params=_params("arbitrary", "arbitrary"),
    )(*recvs, w, m, v)


BIG = ("mlp_w1", "mlp_w2", "ab_w_in", "ab_w_out", "c_w_qkv", "c_w_out")
SMALL = ("mix_norm_g", "mlp_norm_g", "a_spatial_w", "a_spatial_b", "a_vnorm_g", "a_vnorm_b", "b_conv_b", "b_norm_g",
         "b_norm_b", "c_q_norm_g", "c_k_norm_g")
WEIGHTS = ("mix_norm_g", "mlp_norm_g", "mlp_w1", "mlp_w2", "ab_w_in", "a_spatial_w", "a_spatial_b", "a_vnorm_g",
           "a_vnorm_b", "b_conv_w", "b_conv_b", "b_norm_g", "b_norm_b", "ab_w_out", "c_w_qkv", "c_q_norm_g",
           "c_k_norm_g", "c_w_out")


def _mixer_params(p, conv_full, i):
    aw = p["a_vnorm_g"].shape[1]
    row = lambda t: t[i][None, :]
    return dict(
        w=p["a_spatial_w"][i].astype(BF16), wt=jnp.swapaxes(p["a_spatial_w"][i], 1, 2).astype(BF16),
        bias=jnp.repeat(p["a_spatial_b"][i].T, aw // p["a_spatial_b"].shape[1], axis=1),
        vg=row(p["a_vnorm_g"]), vb=row(p["a_vnorm_b"]), cw=jnp.pad(conv_full[i], ((0, 1), (0, 0))),
        cb=row(p["b_conv_b"]), cg=row(p["b_norm_g"]), cbn=row(p["b_norm_b"]))


def _head_gains(p, i):
    rep = LANES // HEAD_DIM
    return jnp.stack([jnp.tile(p["c_q_norm_g"][i], rep), jnp.tile(p["c_k_norm_g"][i], rep),
                      jnp.ones((LANES,), F32)])[:, None, :]


def _unused_forward_backward(x, target, p, wg, conv_full):
    s, d = x.shape
    depth = p["mix_norm_g"].shape[0]
    tabs = rope_tables(s)
    saved = []
    for l in range(depth):
        i = l // 2
        mix_g, mlp_g = p["mix_norm_g"][l][None, :], p["mlp_norm_g"][l][None, :]
        st = dict(x_in=x)
        if l % 2 == 0:
            sp = _mixer_params(p, conv_full, i)
            z, h = norm_matmul(x, mix_g, wg["ab_w_in"], i, f"ab_in_{l}")
            ycat, gc = ab_mid_forward(z, sp, f"ab_mid_{l}")
            x = matmul_residual(x, ycat, wg["ab_w_out"], i, f"ab_out_{l}")
            st.update(z=z, h=h, y=ycat, gc=gc, sp=sp)
        else:
            gains = _head_gains(p, i)
            qkv, h = norm_matmul(x, mix_g, wg["c_w_qkv"], i, f"c_qkv_{l}")
            qkvn = qk_prep_forward(qkv, tabs, gains, f"c_prep_{l}")
            outs = [band_attention_forward(qkvn, dil, f"c_attn_{l}_d{dil}") for dil in DILATIONS]
            o, lse = attention_merge([t[0] for t in outs], [t[1] for t in outs], f"c_merge_{l}")
            x = matmul_residual(x, o, wg["c_w_out"], i, f"c_out_{l}")
            st.update(qkv=qkv, h=h, qkvn=qkvn, y=o, lse=lse, gains=gains)
        st["x_mid"] = x
        x, a, h2 = mlp_forward(x, mlp_g, wg["mlp_w1"], wg["mlp_w2"], l, f"mlp_{l}")
        st.update(a=a, h2=h2)
        saved.append(st)

    dy, loss_part = loss_and_grad(x, target, "loss")

    big = {n: [None] * p_len for n, p_len in (("mlp_w1", depth), ("mlp_w2", depth), ("ab_w_in", depth // 2 + depth % 2),
                                              ("ab_w_out", depth // 2 + depth % 2), ("c_w_qkv", depth // 2),
                                              ("c_w_out", depth // 2))}
    small = {n: [None] * p[n].shape[0] for n in SMALL}
    conv_grads = [None] * p["b_conv_b"].shape[0]
    for l in reversed(range(depth)):
        i, st = l // 2, saved[l]
        mix_g, mlp_g = p["mix_norm_g"][l][None, :], p["mlp_norm_g"][l][None, :]
        dxm, da, dg = mlp_backward(dy, st["a"], st["x_mid"], mlp_g, wg["mlp_w1"], wg["mlp_w2"], l, f"mlp_bwd_{l}")
        small["mlp_norm_g"][l] = dg[0]
        big["mlp_w1"][l] = matmul_tn(st["h2"], da, "cols", f"mlp_dw1_{l}")
        big["mlp_w2"][l] = matmul_tn(st["a"], dy, "rows", f"mlp_dw2_{l}", relu2=True)
        if l % 2 == 0:
            sp = st["sp"]
            dycat = matmul_nt(dxm, wg["ab_w_out"], i, f"ab_out_bwd_{l}")
            big["ab_w_out"][i] = matmul_tn(st["y"], dxm, "rows", f"ab_dwout_{l}")
            dz, dgc, dw, dsb, dvg, dvb, dcg, dcbn, dcb = ab_mid_backward(dycat, st["z"], st["gc"], sp, f"ab_mid_bwd_{l}")
            dz, dcw = conv_backward(dgc, st["z"], dz, sp, f"ab_conv_bwd_{l}")
            small["a_spatial_w"][i], small["a_spatial_b"][i] = dw, dsb
            small["a_vnorm_g"][i], small["a_vnorm_b"][i] = dvg[0], dvb[0]
            small["b_norm_g"][i], small["b_norm_b"][i], small["b_conv_b"][i] = dcg[0], dcbn[0], dcb[0]
            conv_grads[i] = dcw[:CONV_W]
            dy, dg = matmul_nt_norm_backward(dz, wg["ab_w_in"], i, st["x_in"], mix_g, dxm, f"ab_in_bwd_{l}")
            big["ab_w_in"][i] = matmul_tn(st["h"], dz, "cols", f"ab_dwin_{l}")
        else:
            do = matmul_nt(dxm, wg["c_w_out"], i, f"c_out_bwd_{l}")
            big["c_w_out"][i] = matmul_tn(st["y"], dxm, "rows", f"c_dwout_{l}")
            delta, dob = attention_delta(do, st["y"], f"c_delta_{l}")
            dqs, dks, dvs = [], [], []
            for dil in DILATIONS:
                dqs.append(band_attention_dq(st["qkvn"], dob, st["lse"], delta, dil, f"c_attn_dq_{l}_d{dil}"))
                dk, dv = band_attention_dkv(st["qkvn"], dob, st["lse"], delta, dil, f"c_attn_dkv_{l}_d{dil}")
                dks.append(dk)
                dvs.append(dv)
            dqkv, dgn = qk_prep_backward(dqs, dks, dvs, st["qkv"], tabs, st["gains"], f"c_prep_bwd_{l}")
            small["c_q_norm_g"][i] = dgn[0, 0, :HEAD_DIM] + dgn[0, 0, HEAD_DIM:]
            small["c_k_norm_g"][i] = dgn[1, 0, :HEAD_DIM] + dgn[1, 0, HEAD_DIM:]
            dy, dg = matmul_nt_norm_backward(dqkv, wg["c_w_qkv"], i, st["x_in"], mix_g, dxm, f"c_qkv_bwd_{l}")
            big["c_w_qkv"][i] = matmul_tn(st["h"], dqkv, "cols", f"c_dwqkv_{l}")
        small["mix_norm_g"][l] = dg[0]
    small = {n: jnp.stack(v) for n, v in small.items()}
    return loss_part, dy, big, small, jnp.stack(conv_grads)


def _pack(d):
    flat = jnp.concatenate([d[n].reshape(-1) for n in SMALL])
    rows = -(-flat.shape[0] // (8 * LANES)) * 8
    return jnp.pad(flat, (0, rows * LANES - flat.shape[0])).reshape(rows, LANES)


def _unpack(packed, like):
    flat, out, pos = packed.reshape(-1), {}, 0
    for n in SMALL:
        size = math.prod(like[n].shape)
        out[n] = flat[pos:pos + size].reshape(like[n].shape)
        pos += size
    return out


def _unused_kernel(x, mix_norm_g, mlp_norm_g, mlp_w1, mlp_w2, ab_w_in, a_spatial_w, a_spatial_b, a_vnorm_g, a_vnorm_b, b_conv_w, b_conv_b, b_norm_g, b_norm_b, ab_w_out, c_w_qkv, c_q_norm_g, c_k_norm_g, c_w_out, loss_target, m_mix_norm_g, m_mlp_norm_g, m_mlp_w1, m_mlp_w2, m_ab_w_in, m_a_spatial_w, m_a_spatial_b, m_a_vnorm_g, m_a_vnorm_b, m_b_conv_w, m_b_conv_b, m_b_norm_g, m_b_norm_b, m_ab_w_out, m_c_w_qkv, m_c_q_norm_g, m_c_k_norm_g, m_c_w_out, v_mix_norm_g, v_mlp_norm_g, v_mlp_w1, v_mlp_w2, v_ab_w_in, v_a_spatial_w, v_a_spatial_b, v_a_vnorm_g, v_a_vnorm_b, v_b_conv_w, v_b_conv_b, v_b_norm_g, v_b_norm_b, v_ab_w_out, v_c_w_qkv, v_c_q_norm_g, v_c_k_norm_g, v_c_w_out):
    args = dict(locals())
    w = {n: args[n] for n in WEIGHTS}
    m = {n: args["m_" + n] for n in WEIGHTS}
    v = {n: args["v_" + n] for n in WEIGHTS}

    wg = {n: all_gather(w[n].astype(BF16), "gather_" + n) for n in BIG}
    for n in ("ab_w_out", "c_w_out"):
        t = wg[n]
        wg[n] = t.reshape(t.shape[0], t.shape[1] * t.shape[2], t.shape[3])
    conv = all_gather(w["b_conv_w"], "gather_b_conv_w")
    conv_full = jnp.swapaxes(conv, 1, 2).reshape(conv.shape[0], conv.shape[2], N_DEV * conv.shape[3])

    loss_part, dx, big, small, conv_grad = forward_backward(x[0], loss_target[0], w, wg, conv_full)
    loss = lax.psum(jnp.sum(loss_part), ("x", "y", "c"))

    grads, deltas, new_m, new_v = {}, {}, {}, {}

    def update(n, recv):
        shape = w[n].shape
        flat = lambda t: t.reshape(-1, shape[-1])
        outs = adamw_update(recv.reshape((N_DEV, -1, shape[-1])), flat(w[n]), flat(m[n]), flat(v[n]), "adamw_" + n)
        grads[n], deltas[n], new_m[n], new_v[n] = (t.reshape(shape) for t in outs)

    for n in BIG:
        update(n, exchange(big[n], "exchange_" + n))
    nl, kw, cw = conv_grad.shape
    conv_parts = jnp.transpose(conv_grad.reshape(nl, kw, N_DEV, cw // N_DEV), (2, 0, 1, 3))
    update("b_conv_w", exchange([conv_parts], "exchange_b_conv_w"))

    packed = all_gather(_pack(small)[None], "gather_small_grads")[0]
    outs = adamw_update(packed, _pack(w), _pack(m), _pack(v), "adamw_small")
    for dst, t in zip((grads, deltas, new_m, new_v), outs):
        dst.update(_unpack(t, w))

    return (loss, dx[None], *[grads[n] for n in WEIGHTS], *[deltas[n] for n in WEIGHTS],
            *[new_m[n] for n in WEIGHTS], *[new_v[n] for n in WEIGHTS])


class Traffic:
    def __init__(self, shards, full=()):
        self.shards, self.w, self.queue, self.parts = shards, dict(full), [], {}

    def run(self, fn, *args, gather=(), send=False, **kw):
        operands, flags, dest = [], [], []
        if self.shards is None:
            if send:
                self.parts.update(self.queue)
                self.queue = []
        else:
            for k in gather:
                if k not in self.w:
                    operands.append(self.shards[k])
                    flags.append(False)
                    dest.append((self.w, k))
            if send:
                for k, t in self.queue:
                    operands.append(t)
                    flags.append(True)
                    dest.append((self.parts, k))
                self.queue = []
        outs, couts = fn(*args, comm=PeerCopies(operands, flags) if operands else None, **kw)
        for (table, k), t in zip(dest, couts):
            table[k] = t
        return outs

    def flush(self, name, extra=()):
        self.queue += list(extra)
        self.run(lambda comm: ([], run_copies(comm, name) if comm is not None else []), send=True)


def _square(t):
    return t.reshape(t.shape[0] * t.shape[1], t.shape[2])


def forward_backward(x, target, p, tr, conv_full):
    s, d = x.shape
    depth = p["mix_norm_g"].shape[0]
    tabs = rope_tables(s)
    saved = []
    for l in range(depth):
        i = l // 2
        mix_g, mlp_g = p["mix_norm_g"][l][None, :], p["mlp_norm_g"][l][None, :]
        st = dict(x_in=x)
        nxt = () if l + 1 == depth else ((("c_w_qkv", i), ("c_w_out", i)) if l % 2 == 0 else
                                        (("ab_w_in", i + 1), ("ab_w_out", i + 1)))
        if l % 2 == 0:
            sp = _mixer_params(p, conv_full, i)
            z, h = tr.run(norm_matmul, x, mix_g, tr.w["ab_w_in", i], f"ab_in_{l}", gather=[("mlp_w1", l)])
            ycat, gc = tr.run(ab_mid_forward, z, sp, f"ab_mid_{l}", gather=[("mlp_w2", l)])
            x = matmul_residual(x, ycat, _square(tr.w["ab_w_out", i]), f"ab_out_{l}")
            st.update(z=z, h=h, y=ycat, gc=gc, sp=sp)
        else:
            gains = _head_gains(p, i)
            qkv, h = tr.run(norm_matmul, x, mix_g, tr.w["c_w_qkv", i], f"c_qkv_{l}", gather=[("mlp_w1", l)])
            qk = tr.run(lambda *a, comm: (lambda o, c: ([o], c))(*qk_prep_forward(*a, comm=comm)),
                        qkv, tabs, gains, f"c_prep_{l}", gather=[("mlp_w2", l)])[0]
            o, lse = attention_forward(qk, qkv, f"c_attn_{l}")
            x = matmul_residual(x, o, _square(tr.w["c_w_out", i]), f"c_out_{l}")
            st.update(qkv=qkv, h=h, qk=qk, y=o, lse=lse, gains=gains)
        st["x_mid"] = x
        x, a, h2 = tr.run(mlp_forward, x, mlp_g, tr.w["mlp_w1", l], tr.w["mlp_w2", l], f"mlp_{l}", gather=nxt)
        st.update(a=a, h2=h2)
        saved.append(st)

    dy, loss_part = loss_and_grad(x, target, "loss")

    def tn(*a, comm, **kw):
        out, couts = matmul_tn(*a, comm=comm, **kw)
        return [out], couts

    small = {n: [None] * p[n].shape[0] for n in SMALL}
    conv_grads = [None] * p["b_conv_b"].shape[0]
    for l in reversed(range(depth)):
        i, st = l // 2, saved[l]
        mix_g, mlp_g = p["mix_norm_g"][l][None, :], p["mlp_norm_g"][l][None, :]
        w1, w2 = tr.w["mlp_w1", l], tr.w["mlp_w2", l]
        dxm, da, dg = tr.run(mlp_backward, dy, st["a"], st["x_mid"], mlp_g, w1, w2, f"mlp_bwd_{l}", send=True)
        small["mlp_norm_g"][l] = dg[0]
        tr.queue.append((("mlp_w1", l), matmul_tn(st["h2"], da, "cols", f"mlp_dw1_{l}")[0]))
        dw2 = tr.run(tn, st["a"], dy, "rows", f"mlp_dw2_{l}", relu2=True, send=True)[0]
        tr.queue.append((("mlp_w2", l), dw2))
        if l % 2 == 0:
            sp = st["sp"]
            wout = _square(tr.w["ab_w_out", i])
            dycat = matmul_nt(dxm, wout, f"ab_out_bwd_{l}")
            dwout = tr.run(tn, st["y"], dxm, "rows", f"ab_dwout_{l}", send=True)[0]
            tr.queue.append((("ab_w_out", i), dwout))
            dz, dgc, dw, dsb, dvg, dvb, dcg, dcbn, dcb = ab_mid_backward(dycat, st["z"], st["gc"], sp, f"ab_mid_bwd_{l}")
            dz, dcw = conv_backward(dgc, st["z"], dz, sp, f"ab_conv_bwd_{l}")
            small["a_spatial_w"][i], small["a_spatial_b"][i] = dw, dsb
            small["a_vnorm_g"][i], small["a_vnorm_b"][i] = dvg[0], dvb[0]
            small["b_norm_g"][i], small["b_norm_b"][i], small["b_conv_b"][i] = dcg[0], dcbn[0], dcb[0]
            conv_grads[i] = dcw[:CONV_W]
            dy, dg = matmul_nt_norm_backward(dz, tr.w["ab_w_in", i], st["x_in"], mix_g, dxm, f"ab_in_bwd_{l}")
            small["mix_norm_g"][l] = dg[0]
            last = []
            if l == 0 and tr.shards is not None:
                tr.shards["small_grads", 0] = _pack({n: jnp.stack(t) for n, t in small.items()})
                last = [("small_grads", 0)]
            dwin = tr.run(tn, st["h"], dz, "cols", f"ab_dwin_{l}", send=True, gather=last)[0]
            tr.queue.append((("ab_w_in", i), dwin))
        else:
            wout = _square(tr.w["c_w_out", i])
            do = matmul_nt(dxm, wout, f"c_out_bwd_{l}")
            dwout = tr.run(tn, st["y"], dxm, "rows", f"c_dwout_{l}", send=True)[0]
            tr.queue.append((("c_w_out", i), dwout))
            delta, stats = attention_delta(do, st["y"], st["lse"], f"c_delta_{l}")
            dq = attention_dq(st["qk"], st["qkv"], do, st["lse"], delta, f"c_attn_dq_{l}")
            dk, dv = attention_dkv(st["qk"], st["qkv"], do, stats, f"c_attn_dkv_{l}")
            dqkv, dgn = qk_prep_backward(dq, dk, dv, st["qkv"], tabs, st["gains"], f"c_prep_bwd_{l}")
            small["c_q_norm_g"][i] = dgn[0, 0, :HEAD_DIM] + dgn[0, 0, HEAD_DIM:]
            small["c_k_norm_g"][i] = dgn[1, 0, :HEAD_DIM] + dgn[1, 0, HEAD_DIM:]
            dy, dg = matmul_nt_norm_backward(dqkv, tr.w["c_w_qkv", i], st["x_in"], mix_g, dxm, f"c_qkv_bwd_{l}")
            small["mix_norm_g"][l] = dg[0]
            dwqkv = tr.run(tn, st["h"], dqkv, "cols", f"c_dwqkv_{l}", send=True)[0]
            tr.queue.append((("c_w_qkv", i), dwqkv))
    small = {n: jnp.stack(v) for n, v in small.items()}
    return loss_part, dy, small, jnp.stack(conv_grads)


def kernel(x, mix_norm_g, mlp_norm_g, mlp_w1, mlp_w2, ab_w_in, a_spatial_w, a_spatial_b, a_vnorm_g, a_vnorm_b, b_conv_w, b_conv_b, b_norm_g, b_norm_b, ab_w_out, c_w_qkv, c_q_norm_g, c_k_norm_g, c_w_out, loss_target, m_mix_norm_g, m_mlp_norm_g, m_mlp_w1, m_mlp_w2, m_ab_w_in, m_a_spatial_w, m_a_spatial_b, m_a_vnorm_g, m_a_vnorm_b, m_b_conv_w, m_b_conv_b, m_b_norm_g, m_b_norm_b, m_ab_w_out, m_c_w_qkv, m_c_q_norm_g, m_c_k_norm_g, m_c_w_out, v_mix_norm_g, v_mlp_norm_g, v_mlp_w1, v_mlp_w2, v_ab_w_in, v_a_spatial_w, v_a_spatial_b, v_a_vnorm_g, v_a_vnorm_b, v_b_conv_w, v_b_conv_b, v_b_norm_g, v_b_norm_b, v_ab_w_out, v_c_w_qkv, v_c_q_norm_g, v_c_k_norm_g, v_c_w_out):
    args = dict(locals())
    w = {n: args[n] for n in WEIGHTS}
    m = {n: args["m_" + n] for n in WEIGHTS}
    v = {n: args["v_" + n] for n in WEIGHTS}

    shards = {(n, l): w[n][l].astype(BF16) for n in BIG for l in range(w[n].shape[0])}
    shards["b_conv_w", 0] = w["b_conv_w"]
    tr = Traffic(shards)
    first = [("ab_w_in", 0), ("ab_w_out", 0), ("b_conv_w", 0)]
    tr.run(lambda comm: ([], run_copies(comm, "gather_first")), gather=first)
    conv = tr.w["b_conv_w", 0]
    conv_full = jnp.transpose(conv, (1, 2, 0, 3)).reshape(conv.shape[1], conv.shape[2], -1)

    loss_part, dx, small, conv_grad = forward_backward(x[0], loss_target[0], w, tr, conv_full)
    loss = lax.psum(jnp.sum(loss_part), ("x", "y", "c"))
    nl, kw, cw = conv_grad.shape
    conv_parts = jnp.transpose(conv_grad.reshape(nl, kw, N_DEV, cw // N_DEV), (2, 0, 1, 3))
    tr.flush("exchange_last", [(("b_conv_w", 0), conv_parts)])
    packed = tr.w["small_grads", 0]

    grads, deltas, new_m, new_v = {}, {}, {}, {}

    def update(n, recvs):
        shape = w[n].shape
        flat = lambda t: t.reshape(-1, shape[-1])
        recvs = [t.reshape(N_DEV, -1, shape[-1]) for t in recvs]
        outs = adamw_update(recvs, flat(w[n]), flat(m[n]), flat(v[n]), "adamw_" + n)
        grads[n], deltas[n], new_m[n], new_v[n] = (t.reshape(shape) for t in outs)

    for n in BIG:
        update(n, [tr.parts[n, l] for l in range(w[n].shape[0])])
    update("b_conv_w", [tr.parts["b_conv_w", 0]])
    outs = adamw_update([packed], _pack(w), _pack(m), _pack(v), "adamw_small")
    for dst, t in zip((grads, deltas, new_m, new_v), outs):
        dst.update(_unpack(t, w))

    return (loss, dx[None], *[grads[n] for n in WEIGHTS], *[deltas[n] for n in WEIGHTS],
            *[new_m[n] for n in WEIGHTS], *[new_v[n] for n in WEIGHTS])
```

```python
import functools
import math

import jax
import jax.numpy as jnp
from jax import lax
from jax.experimental import pallas as pl
from jax.experimental.pallas import tpu as pltpu

F32, BF16 = jnp.float32, jnp.bfloat16
N_DEV = 8
EPS = 1e-6
NEG = -1e30
LANES = 128
HEAD_DIM = 64
CHUNK = 128
CONV_W = 31
CONV_HALO = 16
BAND = 64
DILATIONS = (1, 4, 16)
ROT_DIM = 16
ROPE_THETA = 500000.0
VMEM_LIMIT = 56 * 1024 * 1024
MLP_CHUNK = 1024
MLP_BWD_CHUNK = 512
ADAM_LR, ADAM_B1, ADAM_B2, ADAM_EPS, ADAM_WD, ADAM_STEP = 0.001, 0.9, 0.999, 1e-08, 0.01, 10
MESH = pl.DeviceIdType.MESH


def _params(*sem):
    return pltpu.CompilerParams(dimension_semantics=sem, vmem_limit_bytes=VMEM_LIMIT)


def _dot(a, b):
    return jnp.dot(a, b, preferred_element_type=F32)


def _dot_nt(a, b):
    return lax.dot_general(a, b, (((1,), (1,)), ((), ())), preferred_element_type=F32)


def _dot_tn(a, b):
    return lax.dot_general(a, b, (((0,), (0,)), ((), ())), preferred_element_type=F32)


def _rms_r(x):
    return lax.rsqrt(jnp.mean(x * x, axis=-1, keepdims=True) + EPS)


def _sigmoid(x):
    return 1.0 / (1.0 + jnp.exp(-x))


_GK = math.sqrt(2.0 / math.pi)


def _gelu(x):
    return 0.5 * x * (1.0 + jnp.tanh(_GK * (x + 0.044715 * x * x * x)))


def _gelu_grad(x):
    t = jnp.tanh(_GK * (x + 0.044715 * x * x * x))
    return 0.5 * (1.0 + t) + 0.5 * x * (1.0 - t * t) * (_GK * (1.0 + 3.0 * 0.044715 * x * x))


def _seg_sum(x, bd):
    hi = x.astype(BF16)
    r1 = x - hi.astype(F32)
    mid = r1.astype(BF16)
    lo = (r1 - mid.astype(F32)).astype(BF16)
    return _dot(hi, bd) + _dot(mid, bd) + _dot(lo, bd)


def _block_diag(n):
    i = lax.broadcasted_iota(jnp.int32, (n, n), 0) // HEAD_DIM
    j = lax.broadcasted_iota(jnp.int32, (n, n), 1) // HEAD_DIM
    return jnp.where(i == j, 1.0, 0.0).astype(BF16)


def _tile(s, cap):
    t = min(s, cap)
    assert s % t == 0
    return t


def _my_index():
    return 4 * lax.axis_index("x") + 2 * lax.axis_index("y") + lax.axis_index("c")


def _device(i):
    return (i // 4, (i // 2) % 2, i % 2)


_HBM = pl.BlockSpec(memory_space=pl.ANY)


class PeerCopies:
    def __init__(self, operands, modes):
        self.inputs, self.modes = list(operands), list(modes)
        self.out_shape = []
        for t, (kind, axis) in zip(self.inputs, self.modes):
            shape = list(t.shape)
            if kind == "gather":
                shape = [N_DEV] + shape if axis is None else shape[:axis] + [N_DEV * shape[axis]] + shape[axis + 1:]
            elif axis is not None:
                shape = [N_DEV] + shape[:axis] + [shape[axis] // N_DEV] + shape[axis + 1:]
            self.out_shape.append(jax.ShapeDtypeStruct(tuple(shape), t.dtype))
        n = len(self.inputs)
        self.scratch = [pltpu.SemaphoreType.DMA((n, N_DEV - 1)), pltpu.SemaphoreType.DMA((n, N_DEV - 1)),
                        pltpu.SemaphoreType.DMA((n,))]

    @staticmethod
    def _block(ref, axis, size, j):
        if axis is None:
            return ref.at[j]
        return ref.at[tuple([slice(None)] * axis + [pl.ds(j * size, size)])]

    def _copies(self, in_refs, out_refs, sems, arrivals):
        send_sems, recv_sems, local_sems = sems
        me = _my_index()
        local, sends, recvs = [], [], []
        for t, (src, dst) in enumerate(zip(in_refs, out_refs)):
            kind, axis = self.modes[t]
            if kind == "gather":
                size = None if axis is None else src.shape[axis]
                source = lambda j, src=src: src
                place = lambda j, dst=dst, axis=axis, size=size: self._block(dst, axis, size, j)
            else:
                size = None if axis is None else src.shape[axis] // N_DEV
                source = lambda j, src=src, axis=axis, size=size: self._block(src, axis, size, j)
                place = lambda j, dst=dst: dst.at[j]
            local.append(pltpu.make_async_copy(source(me), place(me), local_sems.at[t]))
            for k in range(N_DEV - 1):
                to, frm = (me + k + 1) % N_DEV, (me + N_DEV - k - 1) % N_DEV
                sends.append(pltpu.make_async_remote_copy(
                    src_ref=source(to), dst_ref=place(me), send_sem=send_sems.at[t, k], recv_sem=recv_sems.at[t, k],
                    device_id=_device(to), device_id_type=MESH))
                if arrivals:
                    recvs.append(pltpu.make_async_remote_copy(
                        src_ref=source(me), dst_ref=place(frm), send_sem=send_sems.at[t, k], recv_sem=recv_sems.at[t, k],
                        device_id=_device(frm), device_id_type=MESH))
        return local, sends, recvs

    def start(self, in_refs, out_refs, sems):
        local, sends, _ = self._copies(in_refs, out_refs, sems, False)
        for cp in local + sends:
            cp.start()

    def finish(self, in_refs, out_refs, sems):
        local, sends, recvs = self._copies(in_refs, out_refs, sems, True)
        for cp in recvs:
            cp.wait_recv()
        for cp in sends:
            cp.wait_send()
        for cp in local:
            cp.wait()


def _pallas(body, *, name, args, in_specs, out_specs, out_shape, grid=(), scratch=(), sem=(), comm=None, aliases=None):
    n_in, n_out, n_scr = len(args), len(out_shape), len(scratch)
    if comm is None:
        outs = pl.pallas_call(
            body, name=name, grid=grid, in_specs=in_specs, out_specs=out_specs, out_shape=out_shape,
            scratch_shapes=list(scratch), input_output_aliases=aliases or {}, compiler_params=_params(*sem))(*args)
        return list(outs), []
    ci, co = len(comm.inputs), len(comm.out_shape)

    def hosted(*refs):
        ins, cins = refs[:n_in], refs[n_in:n_in + ci]
        outs, couts = refs[n_in + ci:n_in + ci + n_out], refs[n_in + ci + n_out:n_in + ci + n_out + co]
        rest = refs[n_in + ci + n_out + co:]
        scr, sems = rest[:n_scr], rest[n_scr:]
        if not grid:
            comm.start(cins, couts, sems)
            comm.finish(cins, couts, sems)
            return
        first = last = None
        for axis, size in enumerate(grid):
            f, l = pl.program_id(axis) == 0, pl.program_id(axis) == size - 1
            first, last = (f, l) if first is None else (first & f, last & l)
        pl.when(first)(lambda: comm.start(cins, couts, sems))
        body(*ins, *outs, *scr)
        pl.when(last)(lambda: comm.finish(cins, couts, sems))

    outs = pl.pallas_call(
        hosted, name=name, grid=grid, in_specs=list(in_specs) + [_HBM] * ci, out_specs=list(out_specs) + [_HBM] * co,
        out_shape=list(out_shape) + comm.out_shape, scratch_shapes=list(scratch) + comm.scratch,
        input_output_aliases=aliases or {}, compiler_params=_params(*["arbitrary"] * len(grid)))(*args, *comm.inputs)
    return list(outs[:n_out]), list(outs[n_out:])


def run_copies(comm, name):
    return _pallas(None, name=name, args=[], in_specs=[], out_specs=[], out_shape=[], comm=comm)[1]


def norm_matmul(x, g, wg, name, comm=None):
    s, d = x.shape
    n = wg.shape[-1]
    ns = 1024 if n % 1024 == 0 else n // 4
    tm = _tile(s, 1024)

    def body(x_ref, g_ref, w_ref, z_ref, h_ref):
        @pl.when(pl.program_id(1) == 0)
        def _():
            xv = x_ref[...]
            h_ref[...] = (xv * _rms_r(xv) * g_ref[...]).astype(BF16)
        z_ref[...] = _dot(h_ref[...], w_ref[...])

    return _pallas(
        body, name=name, grid=(s // tm, n // ns), args=[x, g, wg], comm=comm,
        in_specs=[pl.BlockSpec((tm, d), lambda i, j: (i, 0)),
                  pl.BlockSpec((1, d), lambda i, j: (0, 0)),
                  pl.BlockSpec((d, ns), lambda i, j: (0, j))],
        out_specs=[pl.BlockSpec((tm, ns), lambda i, j: (i, j)),
                   pl.BlockSpec((tm, d), lambda i, j: (i, 0))],
        out_shape=[jax.ShapeDtypeStruct((s, n), F32), jax.ShapeDtypeStruct((s, d), BF16)],
        sem=("parallel", "arbitrary"))


def mlp_forward(x, g, w1g, w2g, name, comm=None):
    s, d = x.shape
    f = w1g.shape[-1]
    fs = MLP_CHUNK
    tm = _tile(s, 1024)

    def body(x_ref, g_ref, w1_ref, w2_ref, xo_ref, a_ref, h_ref):
        @pl.when(pl.program_id(1) == 0)
        def _():
            xv = x_ref[...]
            h_ref[...] = (xv * _rms_r(xv) * g_ref[...]).astype(BF16)
            xo_ref[...] = xv
        a = _dot(h_ref[...], w1_ref[...])
        a_ref[...] = a.astype(BF16)
        r = jnp.maximum(a, 0.0)
        xo_ref[...] += _dot((r * r).astype(BF16), w2_ref[...])

    return _pallas(
        body, name=name, grid=(s // tm, f // fs), args=[x, g, w1g, w2g], comm=comm,
        in_specs=[pl.BlockSpec((tm, d), lambda i, j: (i, 0)),
                  pl.BlockSpec((1, d), lambda i, j: (0, 0)),
                  pl.BlockSpec((d, fs), lambda i, j: (0, j)),
                  pl.BlockSpec((fs, d), lambda i, j: (j, 0))],
        out_specs=[pl.BlockSpec((tm, d), lambda i, j: (i, 0)),
                   pl.BlockSpec((tm, fs), lambda i, j: (i, j)),
                   pl.BlockSpec((tm, d), lambda i, j: (i, 0))],
        out_shape=[jax.ShapeDtypeStruct((s, d), F32), jax.ShapeDtypeStruct((s, f), BF16),
                   jax.ShapeDtypeStruct((s, d), BF16)],
        sem=("parallel", "arbitrary"))


def _norm_backward(dh, xv, g, dres):
    r = _rms_r(xv)
    xh = xv * r
    t = dh * g
    dx = dres + r * (t - xh * jnp.mean(t * xh, axis=-1, keepdims=True))
    return dx, jnp.sum(dh * xh, axis=0, keepdims=True)


def mlp_backward(dy, a, x, g, w1g, w2g, name, comm=None):
    s, d = x.shape
    f = w1g.shape[-1]
    fs = MLP_BWD_CHUNK
    tm = _tile(s, 1024)

    def body(dy_ref, a_ref, x_ref, g_ref, w1_ref, w2_ref, dx_ref, da_ref, dg_ref, dyb_ref, dh_ref):
        i, j = pl.program_id(0), pl.program_id(1)

        @pl.when(j == 0)
        def _():
            dyb_ref[...] = dy_ref[...].astype(BF16)
            dh_ref[...] = jnp.zeros_like(dh_ref)

        dr = _dot_nt(dyb_ref[...], w2_ref[...])
        da = (dr * (2.0 * jnp.maximum(a_ref[...].astype(F32), 0.0))).astype(BF16)
        da_ref[...] = da
        dh_ref[...] += _dot_nt(da, w1_ref[...])

        @pl.when(j == f // fs - 1)
        def _():
            dx, dgp = _norm_backward(dh_ref[...], x_ref[...], g_ref[...], dy_ref[...])
            dx_ref[...] = dx

            @pl.when(i == 0)
            def _():
                dg_ref[...] = dgp

            @pl.when(i > 0)
            def _():
                dg_ref[...] += dgp

    return _pallas(
        body, name=name, grid=(s // tm, f // fs), args=[dy, a, x, g, w1g, w2g], comm=comm,
        in_specs=[pl.BlockSpec((tm, d), lambda i, j: (i, 0)),
                  pl.BlockSpec((tm, fs), lambda i, j: (i, j)),
                  pl.BlockSpec((tm, d), lambda i, j: (i, 0)),
                  pl.BlockSpec((1, d), lambda i, j: (0, 0)),
                  pl.BlockSpec((d, fs), lambda i, j: (0, j)),
                  pl.BlockSpec((fs, d), lambda i, j: (j, 0))],
        out_specs=[pl.BlockSpec((tm, d), lambda i, j: (i, 0)),
                   pl.BlockSpec((tm, fs), lambda i, j: (i, j)),
                   pl.BlockSpec((1, d), lambda i, j: (0, 0)),
                   pl.BlockSpec((tm, d), lambda i, j: (i, 0))],
        out_shape=[jax.ShapeDtypeStruct((s, d), F32), jax.ShapeDtypeStruct((s, f), BF16),
                   jax.ShapeDtypeStruct((1, d), F32), jax.ShapeDtypeStruct((s, d), BF16)],
        scratch=[pltpu.VMEM((tm, d), F32)],
        sem=("arbitrary", "arbitrary"))


def matmul_tn(a, b, name, m_split=1, n_split=1, relu2=False, comm=None):
    s, m = a.shape
    n = b.shape[1]
    ts = _tile(s, 1024)
    bm, bn = m // m_split, n // n_split
    a_map = lambda j, k: (k, j // n_split)
    b_map = lambda j, k: (k, j % n_split)

    def body(a_ref, b_ref, o_ref, acc_ref):
        k = pl.program_id(1)
        av = a_ref[...]
        if relu2:
            af = jnp.maximum(av.astype(F32), 0.0)
            av = af * af
        p = _dot_tn(av.astype(BF16), b_ref[...].astype(BF16))

        @pl.when(k == 0)
        def _():
            acc_ref[...] = p

        @pl.when(k > 0)
        def _():
            acc_ref[...] += p

        @pl.when(k == s // ts - 1)
        def _():
            o_ref[...] = acc_ref[...].astype(BF16)

    outs, couts = _pallas(
        body, name=name, grid=(m_split * n_split, s // ts), args=[a, b], comm=comm,
        in_specs=[pl.BlockSpec((ts, bm), a_map), pl.BlockSpec((ts, bn), b_map)],
        out_specs=[pl.BlockSpec((bm, bn), lambda j, k: (j // n_split, j % n_split))],
        out_shape=[jax.ShapeDtypeStruct((m, n), BF16)],
        scratch=[pltpu.VMEM((bm, bn), F32)],
        sem=("parallel", "arbitrary"))
    return outs[0], couts


def matmul_residual(x, y, w, name):
    s, n = x.shape
    k = y.shape[1]
    tm = _tile(s, 1024)

    def body(x_ref, y_ref, w_ref, o_ref):
        o_ref[...] = x_ref[...] + _dot(y_ref[...], w_ref[...])

    return pl.pallas_call(
        body, name=name, grid=(s // tm,),
        in_specs=[pl.BlockSpec((tm, n), lambda i: (i, 0)),
                  pl.BlockSpec((tm, k), lambda i: (i, 0)),
                  pl.BlockSpec((k, n), lambda i: (0, 0))],
        out_specs=pl.BlockSpec((tm, n), lambda i: (i, 0)),
        out_shape=jax.ShapeDtypeStruct((s, n), F32),
        compiler_params=_params("parallel"),
    )(x, y, w)


def matmul_nt(dy, wg, name):
    s, n = dy.shape
    k = wg.shape[0]
    tm = _tile(s, 1024)

    def body(dy_ref, w_ref, o_ref):
        o_ref[...] = _dot_nt(dy_ref[...].astype(BF16), w_ref[...])

    return pl.pallas_call(
        body, name=name, grid=(s // tm,),
        in_specs=[pl.BlockSpec((tm, n), lambda i: (i, 0)),
                  pl.BlockSpec((k, n), lambda i: (0, 0))],
        out_specs=pl.BlockSpec((tm, k), lambda i: (i, 0)),
        out_shape=jax.ShapeDtypeStruct((s, k), F32),
        compiler_params=_params("parallel"),
    )(dy, wg)


def matmul_nt_norm_backward(dz, wg, x, g, dres, name):
    s, d = x.shape
    n = wg.shape[-1]
    tm = _tile(s, 512)

    def body(dz_ref, w_ref, x_ref, g_ref, dres_ref, dx_ref, dg_ref):
        dx, dgp = _norm_backward(_dot_nt(dz_ref[...], w_ref[...]), x_ref[...], g_ref[...], dres_ref[...])
        dx_ref[...] = dx
        _accumulate(dg_ref, dgp, pl.program_id(0) == 0)

    return pl.pallas_call(
        body, name=name, grid=(s // tm,),
        in_specs=[pl.BlockSpec((tm, n), lambda i: (i, 0)),
                  pl.BlockSpec((d, n), lambda i: (0, 0)),
                  pl.BlockSpec((tm, d), lambda i: (i, 0)),
                  pl.BlockSpec((1, d), lambda i: (0, 0)),
                  pl.BlockSpec((tm, d), lambda i: (i, 0))],
        out_specs=[pl.BlockSpec((tm, d), lambda i: (i, 0)),
                   pl.BlockSpec((1, d), lambda i: (0, 0))],
        out_shape=[jax.ShapeDtypeStruct((s, d), F32), jax.ShapeDtypeStruct((1, d), F32)],
        compiler_params=_params("arbitrary"),
    )(dz, wg, x, g, dres)


def loss_and_grad(y, target, name):
    s, d = y.shape
    tm = _tile(s, 1024)

    def body(y_ref, t_ref, dy_ref, l_ref):
        e = y_ref[...] - t_ref[...]
        dy_ref[...] = e / d
        part = jnp.sum(e * e, axis=0, keepdims=True) * (0.5 / d)

        @pl.when(pl.program_id(0) == 0)
        def _():
            l_ref[...] = part

        @pl.when(pl.program_id(0) > 0)
        def _():
            l_ref[...] += part

    return pl.pallas_call(
        body, name=name, grid=(s // tm,),
        in_specs=[pl.BlockSpec((tm, d), lambda i: (i, 0)), pl.BlockSpec((tm, d), lambda i: (i, 0))],
        out_specs=[pl.BlockSpec((tm, d), lambda i: (i, 0)), pl.BlockSpec((1, d), lambda i: (0, 0))],
        out_shape=[jax.ShapeDtypeStruct((s, d), F32), jax.ShapeDtypeStruct((1, d), F32)],
        compiler_params=_params("arbitrary"),
    )(y, target)


def _layernorm(x, g, b):
    mu = jnp.mean(x, axis=-1, keepdims=True)
    xc = x - mu
    rstd = lax.rsqrt(jnp.mean(xc * xc, axis=-1, keepdims=True) + EPS)
    xn = xc * rstd
    return xn * g + b, xn, rstd


def _layernorm_backward(dy, xn, rstd, g):
    dxn = dy * g
    return rstd * (dxn - jnp.mean(dxn, axis=-1, keepdims=True) - xn * jnp.mean(dxn * xn, axis=-1, keepdims=True))


def _group_halves(x_ref, jp, nch):
    blk = jnp.concatenate([x_ref[c * CHUNK:(c + 1) * CHUNK, jp * LANES:(jp + 1) * LANES] for c in range(nch)], axis=1)
    low = (lax.broadcasted_iota(jnp.int32, blk.shape, 1) % LANES) < HEAD_DIM
    return jnp.where(low, blk, 0.0).astype(BF16), jnp.where(low, 0.0, blk).astype(BF16)


def _spatial_apply(src_ref, w_ref, dst_ref, nch, bias_ref=None):
    for jp in range(4):
        lo, hi = _group_halves(src_ref, jp, nch)
        r = _dot(w_ref[2 * jp], lo) + _dot(w_ref[2 * jp + 1], hi)
        for c in range(nch):
            v = r[:, c * LANES:(c + 1) * LANES]
            if bias_ref is not None:
                v = v + bias_ref[:, jp * LANES:(jp + 1) * LANES]
            dst_ref[c * CHUNK:(c + 1) * CHUNK, jp * LANES:(jp + 1) * LANES] = v


def _glu(zb):
    w = zb.shape[1] // 2
    return zb[:, :w] * _sigmoid(zb[:, w:])


def _fill_padded(pad_ref, prev, cur, nxt, i, nt, tm):
    pad_ref[0:CONV_HALO, :] = jnp.where(i > 0, prev, 0.0)
    pad_ref[CONV_HALO:CONV_HALO + tm, :] = cur
    pad_ref[CONV_HALO + tm:2 * CONV_HALO + tm, :] = jnp.where(i < nt - 1, nxt, 0.0)


def _halo_specs(tm, s, width, col):
    hb, nhb = tm // CONV_HALO, s // CONV_HALO
    return [pl.BlockSpec((tm, width), lambda i: (i, col)),
            pl.BlockSpec((CONV_HALO, width), lambda i: (jnp.maximum(i * hb - 1, 0), col)),
            pl.BlockSpec((CONV_HALO, width), lambda i: (jnp.minimum((i + 1) * hb, nhb - 1), col))]


def _const_spec(shape):
    nd = len(shape)
    return pl.BlockSpec(shape, lambda i: (0,) * nd)


SUBLANES = 8


def _shift_scratch(tm, width):
    return pltpu.VMEM((SUBLANES - 1, tm + 2 * CONV_HALO - SUBLANES, width), F32)


def _fill_shifts(sh_ref, pad_ref):
    rows = sh_ref.shape[1]
    for sft in range(1, SUBLANES):
        sh_ref[sft - 1] = pad_ref[pl.ds(sft, rows), :]


def _tap(pad_ref, sh_ref, offset, cols):
    sft = offset % SUBLANES
    rows = pl.ds(offset - sft, CHUNK)
    return pad_ref[rows, cols] if sft == 0 else sh_ref[sft - 1, rows, cols]


def ab_mid_forward(z, sp, name, comm=None):
    s = z.shape[0]
    aw = z.shape[1] // 4
    tm = _tile(s, 512)
    nch, nt = tm // CHUNK, s // tm

    def body(zu_ref, zv_ref, zb_ref, zp_ref, zn_ref, w_ref, bias_ref, vg_ref, vb_ref, cw_ref, cb_ref, cg_ref, cbn_ref,
             y_ref, gc_ref, vl_ref, sv_ref, pad_ref, sh_ref):
        i = pl.program_id(0)
        vl_ref[...] = _layernorm(_gelu(zv_ref[...]), vg_ref[...], vb_ref[...])[0]
        _spatial_apply(vl_ref, w_ref, sv_ref, nch, bias_ref)
        y_ref[:, :aw] = (_gelu(zu_ref[...]) * sv_ref[...]).astype(BF16)

        _fill_padded(pad_ref, _glu(zp_ref[...]), _glu(zb_ref[...]), _glu(zn_ref[...]), i, nt, tm)
        _fill_shifts(sh_ref, pad_ref)
        for rb in range(tm // CHUNK):
            for lb in range(aw // LANES):
                cols = pl.ds(lb * LANES, LANES)
                acc = jnp.broadcast_to(cb_ref[:, cols], (CHUNK, LANES))
                for k in range(CONV_W):
                    acc = acc + cw_ref[k:k + 1, cols] * _tap(pad_ref, sh_ref, rb * CHUNK + CONV_HALO - CONV_W // 2 + k, cols)
                gc_ref[rb * CHUNK:(rb + 1) * CHUNK, cols] = acc
        yl = _layernorm(gc_ref[...], cg_ref[...], cbn_ref[...])[0]
        y_ref[:, aw:] = (yl * _sigmoid(yl)).astype(BF16)

    return _pallas(
        body, name=name, grid=(nt,), comm=comm,
        args=[z, z, z, z, z, sp["w"], sp["bias"], sp["vg"], sp["vb"], sp["cw"], sp["cb"], sp["cg"], sp["cbn"]],
        in_specs=[pl.BlockSpec((tm, aw), lambda i: (i, 0)), pl.BlockSpec((tm, aw), lambda i: (i, 1))]
        + _halo_specs(tm, s, 2 * aw, 1)
        + [_const_spec(sp["w"].shape), _const_spec(sp["bias"].shape)]
        + [_const_spec((1, aw))] * 2 + [_const_spec(sp["cw"].shape)] + [_const_spec((1, aw))] * 3,
        out_specs=[pl.BlockSpec((tm, 2 * aw), lambda i: (i, 0)), pl.BlockSpec((tm, aw), lambda i: (i, 0))],
        out_shape=[jax.ShapeDtypeStruct((s, 2 * aw), BF16), jax.ShapeDtypeStruct((s, aw), F32)],
        scratch=[pltpu.VMEM((tm, aw), F32), pltpu.VMEM((tm, aw), F32), pltpu.VMEM((tm + 2 * CONV_HALO, aw), F32),
                 _shift_scratch(tm, aw)],
        sem=("parallel",))


def _accumulate(ref, val, first):
    @pl.when(first)
    def _():
        ref[...] = val

    @pl.when(jnp.logical_not(first))
    def _():
        ref[...] += val


def ab_mid_backward(dy, z, gc, sp, name):
    s = z.shape[0]
    aw = z.shape[1] // 4
    tm = _tile(s, 512)
    nch, nt = tm // CHUNK, s // tm

    def body(dya_ref, dyb_ref, zu_ref, zv_ref, gc_ref, w_ref, wt_ref, bias_ref, vg_ref, vb_ref, cg_ref, cbn_ref,
             dz_ref, dgc_ref, dw_ref, dsb_ref, dvg_ref, dvb_ref, dcg_ref, dcbn_ref, dcb_ref,
             vl_ref, sv_ref, dsv_ref, dvl_ref):
        first = pl.program_id(0) == 0
        zu, zv = zu_ref[...], zv_ref[...]
        u = _gelu(zu)
        vl, vn, vrstd = _layernorm(_gelu(zv), vg_ref[...], vb_ref[...])
        vl_ref[...] = vl
        _spatial_apply(vl_ref, w_ref, sv_ref, nch, bias_ref)
        dya = dya_ref[...]
        dz_ref[:, :aw] = (dya * sv_ref[...] * _gelu_grad(zu)).astype(BF16)
        dsv = dya * u
        dsv_ref[...] = dsv
        _spatial_apply(dsv_ref, wt_ref, dvl_ref, nch)

        for jp in range(4):
            dlo, dhi = _group_halves(dsv_ref, jp, nch)
            vlo, vhi = _group_halves(vl_ref, jp, nch)
            vall = vlo + vhi
            _accumulate(dw_ref.at[2 * jp], _dot_nt(dlo, vall), first)
            _accumulate(dw_ref.at[2 * jp + 1], _dot_nt(dhi, vall), first)
        rows = dsv[0:CHUNK]
        for c in range(1, nch):
            rows = rows + dsv[c * CHUNK:(c + 1) * CHUNK]
        grp = lax.broadcasted_iota(jnp.int32, (8, aw), 0) == lax.broadcasted_iota(jnp.int32, (8, aw), 1) // HEAD_DIM
        e = jnp.where(grp, 1.0, 0.0).astype(BF16)
        hi = rows.astype(BF16)
        r1 = rows - hi.astype(F32)
        mid = r1.astype(BF16)
        lo = (r1 - mid.astype(F32)).astype(BF16)
        _accumulate(dsb_ref, _dot_nt(e, hi) + _dot_nt(e, mid) + _dot_nt(e, lo), first)

        dvl = dvl_ref[...]
        _accumulate(dvg_ref, jnp.sum(dvl * vn, axis=0, keepdims=True), first)
        _accumulate(dvb_ref, jnp.sum(dvl, axis=0, keepdims=True), first)
        dz_ref[:, aw:] = (_layernorm_backward(dvl, vn, vrstd, vg_ref[...]) * _gelu_grad(zv)).astype(BF16)

        yl, yn, yrstd = _layernorm(gc_ref[...], cg_ref[...], cbn_ref[...])
        sg = _sigmoid(yl)
        dyl = dyb_ref[...] * (sg + yl * sg * (1.0 - sg))
        _accumulate(dcg_ref, jnp.sum(dyl * yn, axis=0, keepdims=True), first)
        _accumulate(dcbn_ref, jnp.sum(dyl, axis=0, keepdims=True), first)
        dgc = _layernorm_backward(dyl, yn, yrstd, cg_ref[...])
        dgc_ref[...] = dgc
        _accumulate(dcb_ref, jnp.sum(dgc, axis=0, keepdims=True), first)

    vec = jax.ShapeDtypeStruct((1, aw), F32)
    return pl.pallas_call(
        body, name=name, grid=(nt,),
        in_specs=[pl.BlockSpec((tm, aw), lambda i: (i, 0)), pl.BlockSpec((tm, aw), lambda i: (i, 1)),
                  pl.BlockSpec((tm, aw), lambda i: (i, 0)), pl.BlockSpec((tm, aw), lambda i: (i, 1)),
                  pl.BlockSpec((tm, aw), lambda i: (i, 0)),
                  _const_spec(sp["w"].shape), _const_spec(sp["w"].shape), _const_spec(sp["bias"].shape)]
        + [_const_spec((1, aw))] * 4,
        out_specs=[pl.BlockSpec((tm, 2 * aw), lambda i: (i, 0)), pl.BlockSpec((tm, aw), lambda i: (i, 0)),
                   _const_spec(sp["w"].shape), _const_spec((8, CHUNK))] + [_const_spec((1, aw))] * 5,
        out_shape=[jax.ShapeDtypeStruct((s, 4 * aw), BF16), jax.ShapeDtypeStruct((s, aw), F32),
                   jax.ShapeDtypeStruct(sp["w"].shape, F32), jax.ShapeDtypeStruct((8, CHUNK), F32)] + [vec] * 5,
        scratch_shapes=[pltpu.VMEM((tm, aw), F32)] * 4,
        compiler_params=_params("arbitrary"),
    )(dy, dy, z, z, gc, sp["w"], sp["wt"], sp["bias"], sp["vg"], sp["vb"], sp["cg"], sp["cbn"])


def conv_backward(dgc, z, dz_in, sp, name):
    s = z.shape[0]
    aw = z.shape[1] // 4
    tm = _tile(s, 512)
    nt = s // tm
    off = CONV_HALO - CONV_W // 2

    def body(d_ref, dp_ref, dn_ref, zb_ref, zp_ref, zn_ref, cw_ref, dzin_ref, dz_ref, dcw_ref,
             padd_ref, padg_ref, dgg_ref, shd_ref, shg_ref):
        i = pl.program_id(0)
        _fill_padded(padd_ref, dp_ref[...], d_ref[...], dn_ref[...], i, nt, tm)
        _fill_padded(padg_ref, _glu(zp_ref[...]), _glu(zb_ref[...]), _glu(zn_ref[...]), i, nt, tm)
        _fill_shifts(shd_ref, padd_ref)
        _fill_shifts(shg_ref, padg_ref)

        @pl.when(i == 0)
        def _():
            dcw_ref[...] = jnp.zeros_like(dcw_ref)

        for lb in range(aw // LANES):
            cols = pl.ds(lb * LANES, LANES)
            for rb in range(tm // CHUNK):
                acc = jnp.zeros((CHUNK, LANES), F32)
                for k in range(CONV_W):
                    acc = acc + cw_ref[k:k + 1, cols] * _tap(padd_ref, shd_ref, rb * CHUNK + CONV_HALO + CONV_W // 2 - k, cols)
                dgg_ref[rb * CHUNK:(rb + 1) * CHUNK, cols] = acc
            for k in range(CONV_W):
                part = jnp.zeros((1, LANES), F32)
                for rb in range(tm // CHUNK):
                    prod = d_ref[rb * CHUNK:(rb + 1) * CHUNK, cols] * _tap(padg_ref, shg_ref, rb * CHUNK + off + k, cols)
                    part = part + jnp.sum(prod, axis=0, keepdims=True)
                dcw_ref[k:k + 1, cols] += part

        zb = zb_ref[...]
        val, sg = zb[:, :aw], _sigmoid(zb[:, aw:])
        dgg = dgg_ref[...]
        dz_ref[:, :aw] = (dgg * sg).astype(BF16)
        dz_ref[:, aw:] = (dgg * val * sg * (1.0 - sg)).astype(BF16)

    return pl.pallas_call(
        body, name=name, grid=(nt,),
        in_specs=_halo_specs(tm, s, aw, 0) + _halo_specs(tm, s, 2 * aw, 1)
        + [_const_spec(sp["cw"].shape), pl.BlockSpec(memory_space=pl.ANY)],
        out_specs=[pl.BlockSpec((tm, 2 * aw), lambda i: (i, 1)), _const_spec(sp["cw"].shape)],
        out_shape=[jax.ShapeDtypeStruct((s, 4 * aw), BF16), jax.ShapeDtypeStruct(sp["cw"].shape, F32)],
        scratch_shapes=[pltpu.VMEM((tm + 2 * CONV_HALO, aw), F32)] * 2 + [pltpu.VMEM((tm, aw), F32)]
        + [_shift_scratch(tm, aw)] * 2,
        input_output_aliases={7: 0},
        compiler_params=_params("arbitrary"),
    )(dgc, dgc, dgc, z, z, z, sp["cw"], dz_in)


def rope_tables(s):
    pos = jnp.arange(s, dtype=F32)
    inv_freq = ROPE_THETA ** (-jnp.arange(0, ROT_DIM, 2, dtype=F32) / ROT_DIM)
    ang = pos[:, None] * inv_freq[None, :]
    cos, sin = jnp.cos(ang), jnp.sin(ang)
    half = ROT_DIM // 2
    rest = HEAD_DIM - ROT_DIM
    one, zero, zrest = jnp.ones((s, rest), F32), jnp.zeros((s, half), F32), jnp.zeros((s, rest), F32)
    c = jnp.concatenate([cos, cos, one], axis=1)
    s1 = jnp.concatenate([-sin, zero, zrest], axis=1)
    s2 = jnp.concatenate([zero, sin, zrest], axis=1)
    return tuple(jnp.tile(t, (1, LANES // HEAD_DIM)) for t in (c, s1, s2))


def qk_prep_forward(qkv, tabs, gains, name, comm=None):
    s, w3 = qkv.shape
    w = w3 // 3
    tm = _tile(s, 512)

    def body(x_ref, c_ref, s1_ref, s2_ref, g_ref, o_ref):
        bd = _block_diag(LANES)
        for b in range(w // LANES):
            cols = pl.ds(b * LANES, LANES)
            t = x_ref[:, cols]
            r = lax.rsqrt(_seg_sum(t * t, bd) * (1.0 / HEAD_DIM) + EPS)
            y = t * r * g_ref[...]
            o_ref[:, cols] = (y * c_ref[...] + pltpu.roll(y, LANES - ROT_DIM // 2, 1) * s1_ref[...]
                              + pltpu.roll(y, ROT_DIM // 2, 1) * s2_ref[...])

    tab = pl.BlockSpec((tm, LANES), lambda i, p: (i, 0))
    outs, couts = _pallas(
        body, name=name, grid=(s // tm, 2), args=[qkv, *tabs, gains], comm=comm,
        in_specs=[pl.BlockSpec((tm, w), lambda i, p: (i, p)), tab, tab, tab,
                  pl.BlockSpec((None, 1, LANES), lambda i, p: (p, 0, 0))],
        out_specs=[pl.BlockSpec((tm, w), lambda i, p: (i, p))],
        out_shape=[jax.ShapeDtypeStruct((s, 2 * w), F32)],
        sem=("parallel", "arbitrary"))
    return outs[0], couts


def qk_prep_backward(dq, dk, dv, qkv, tabs, gains, name):
    s, w3 = qkv.shape
    w = w3 // 3
    tm = _tile(s, 512)

    def body(*refs):
        grads = ((refs[0],), (refs[1],), (refs[2],))
        x_ref, c_ref, s1_ref, s2_ref, g_ref, o_ref, dg_ref = refs[3:]
        part, first = pl.program_id(0), pl.program_id(1) == 0

        def normed(ds):
            bd = _block_diag(LANES)
            acc = jnp.zeros((1, LANES), F32)
            for b in range(w // LANES):
                cols = pl.ds(b * LANES, LANES)
                dout = ds[0][:, cols]
                dy = (dout * c_ref[...] + pltpu.roll(dout * s1_ref[...], ROT_DIM // 2, 1)
                      + pltpu.roll(dout * s2_ref[...], LANES - ROT_DIM // 2, 1))
                t = x_ref[:, cols]
                r = lax.rsqrt(_seg_sum(t * t, bd) * (1.0 / HEAD_DIM) + EPS)
                xh = t * r
                acc = acc + jnp.sum(dy * xh, axis=0, keepdims=True)
                tt = dy * g_ref[...]
                o_ref[:, cols] = (r * (tt - xh * (_seg_sum(tt * xh, bd) * (1.0 / HEAD_DIM)))).astype(BF16)
            _accumulate(dg_ref, acc, first)

        for p in range(2):
            pl.when(part == p)(functools.partial(normed, grads[p]))

        @pl.when(part == 2)
        def _():
            o_ref[...] = grads[2][0][...].astype(BF16)
            _accumulate(dg_ref, jnp.zeros((1, LANES), F32), first)

    def gspec(p):
        return pl.BlockSpec((tm, w), lambda q, i: (jnp.where(q == p, i, 0), 0))

    tab = pl.BlockSpec((tm, LANES), lambda q, i: (i, 0))
    return pl.pallas_call(
        body, name=name, grid=(3, s // tm),
        in_specs=[gspec(0), gspec(1), gspec(2)]
        + [pl.BlockSpec((tm, w), lambda q, i: (i, q)), tab, tab, tab,
           pl.BlockSpec((None, 1, LANES), lambda q, i: (q, 0, 0))],
        out_specs=[pl.BlockSpec((tm, w), lambda q, i: (i, q)), pl.BlockSpec((None, 1, LANES), lambda q, i: (q, 0, 0))],
        out_shape=[jax.ShapeDtypeStruct((s, w3), BF16), jax.ShapeDtypeStruct((3, 1, LANES), F32)],
        compiler_params=_params("arbitrary", "arbitrary"),
    )(dq, dk, dv, qkv, *tabs, gains)


def _window_specs(tq, l, col_fn):
    hb, nhb = tq // BAND, l // BAND
    return [pl.BlockSpec((tq, LANES), lambda c, i: (i, col_fn(c))),
            pl.BlockSpec((BAND, LANES), lambda c, i: (jnp.maximum(i * hb - 1, 0), col_fn(c))),
            pl.BlockSpec((BAND, LANES), lambda c, i: (jnp.minimum((i + 1) * hb, nhb - 1), col_fn(c)))]


def _window(cur_ref, prev_ref, next_ref):
    return jnp.concatenate([prev_ref[...], cur_ref[...], next_ref[...]], axis=0)


def _band_mask(shape, centre_axis, first_row, length):
    ctr = lax.broadcasted_iota(jnp.int32, shape, centre_axis)
    win = lax.broadcasted_iota(jnp.int32, shape, 1 - centre_axis)
    row = first_row + win
    return (jnp.abs(win - BAND - ctr) <= BAND) & (row >= 0) & (row < length)


def _col_q(d):
    return lambda c: (c // 8) * 24 + c % 8


def _col_k(d):
    return lambda c: (c // 8) * 24 + 8 + c % 8


def _col_v(d):
    return lambda c: (c // 8) * 24 + 16 + c % 8


def band_attention_forward(qkvn, d, name):
    s, w3 = qkvn.shape
    w = w3 // 3
    l = s // d
    tq = _tile(l, 512)
    scale = HEAD_DIM ** -0.5
    xv = qkvn.reshape(l, d * w3)

    def body(q_ref, k_ref, kp_ref, kn_ref, v_ref, vp_ref, vn_ref, o_ref, lse_ref):
        i = pl.program_id(1)
        kw, vw = _window(k_ref, kp_ref, kn_ref), _window(v_ref, vp_ref, vn_ref)
        head0 = lax.broadcasted_iota(jnp.int32, (CHUNK, LANES), 1) < HEAD_DIM
        for b in range(tq // CHUNK):
            rows = pl.ds(b * CHUNK, CHUNK)
            mask = _band_mask((CHUNK, 2 * CHUNK), 0, i * tq + b * CHUNK - BAND, l)
            qb = q_ref[rows, :]
            kb, vb = kw[b * CHUNK:(b + 2) * CHUNK], vw[b * CHUNK:(b + 2) * CHUNK]
            outs, lses = [], []
            for hm in (head0, jnp.logical_not(head0)):
                sc = jnp.where(mask, _dot_nt(jnp.where(hm, qb, jnp.zeros_like(qb)), kb) * scale, NEG)
                m = jnp.max(sc, axis=1, keepdims=True)
                p = jnp.exp(sc - m)
                den = jnp.sum(p, axis=1, keepdims=True)
                outs.append(_dot(p.astype(BF16), vb) / den)
                lses.append(jnp.broadcast_to(m + jnp.log(den), (CHUNK, LANES)))
            o_ref[rows, :] = jnp.where(head0, outs[0], outs[1]).astype(BF16)
            lse_ref[rows, :] = jnp.where(head0, lses[0], lses[1])

    ospec = pl.BlockSpec((tq, LANES), lambda c, i: (i, c))
    o, lse = pl.pallas_call(
        body, name=name, grid=(d * w // LANES, l // tq),
        in_specs=[pl.BlockSpec((tq, LANES), lambda c, i: (i, _col_q(d)(c)))]
        + _window_specs(tq, l, _col_k(d)) + _window_specs(tq, l, _col_v(d)),
        out_specs=[ospec, ospec],
        out_shape=[jax.ShapeDtypeStruct((l, d * w), BF16), jax.ShapeDtypeStruct((l, d * w), F32)],
        compiler_params=_params("parallel", "parallel"),
    )(xv, xv, xv, xv, xv, xv, xv)
    return o.reshape(s, w), lse.reshape(s, w)


def attention_merge(os_, lses, name):
    s, w = os_[0].shape
    tm = _tile(s, 512)

    def body(o0, o1, o2, l0, l1, l2, o_ref, lse_ref):
        la, lb, lc = l0[...], l1[...], l2[...]
        m = jnp.maximum(jnp.maximum(la, lb), lc)
        wa, wb, wc = jnp.exp(la - m), jnp.exp(lb - m), jnp.exp(lc - m)
        den = wa + wb + wc
        o = (wa * o0[...].astype(F32) + wb * o1[...].astype(F32) + wc * o2[...].astype(F32)) / den
        o_ref[...] = o.astype(BF16)
        lse_ref[...] = m + jnp.log(den)

    spec = pl.BlockSpec((tm, w), lambda i: (i, 0))
    return pl.pallas_call(
        body, name=name, grid=(s // tm,), in_specs=[spec] * 6, out_specs=[spec, spec],
        out_shape=[jax.ShapeDtypeStruct((s, w), BF16), jax.ShapeDtypeStruct((s, w), F32)],
        compiler_params=_params("parallel"),
    )(*os_, *lses)


def attention_delta(do, o, name):
    s, w = do.shape
    tm = _tile(s, 512)

    def body(do_ref, o_ref, dl_ref, dob_ref):
        bd = _block_diag(LANES)
        for b in range(w // LANES):
            cols = pl.ds(b * LANES, LANES)
            dv = do_ref[:, cols]
            dl_ref[:, cols] = _seg_sum(dv * o_ref[:, cols].astype(F32), bd)
            dob_ref[:, cols] = dv.astype(BF16)

    spec = pl.BlockSpec((tm, w), lambda i: (i, 0))
    return pl.pallas_call(
        body, name=name, grid=(s // tm,), in_specs=[spec, spec], out_specs=[spec, spec],
        out_shape=[jax.ShapeDtypeStruct((s, w), F32), jax.ShapeDtypeStruct((s, w), BF16)],
        compiler_params=_params("parallel"),
    )(do, o)


def band_attention_dq(qkvn, dob, lse, delta, d, name):
    s, w3 = qkvn.shape
    w = w3 // 3
    l = s // d
    tq = _tile(l, 512)
    scale = HEAD_DIM ** -0.5
    xv = qkvn.reshape(l, d * w3)

    def body(q_ref, k_ref, kp_ref, kn_ref, v_ref, vp_ref, vn_ref, do_ref, lse_ref, dl_ref, dq_ref):
        i = pl.program_id(1)
        kw, vw = _window(k_ref, kp_ref, kn_ref), _window(v_ref, vp_ref, vn_ref)
        head0 = lax.broadcasted_iota(jnp.int32, (CHUNK, LANES), 1) < HEAD_DIM
        for b in range(tq // CHUNK):
            rows = pl.ds(b * CHUNK, CHUNK)
            mask = _band_mask((CHUNK, 2 * CHUNK), 0, i * tq + b * CHUNK - BAND, l)
            qb, dob_ = q_ref[rows, :], do_ref[rows, :]
            kb, vb = kw[b * CHUNK:(b + 2) * CHUNK], vw[b * CHUNK:(b + 2) * CHUNK]
            outs = []
            for h, hm in enumerate((head0, jnp.logical_not(head0))):
                col = pl.ds(h * HEAD_DIM, 1)
                sc = jnp.where(mask, _dot_nt(jnp.where(hm, qb, jnp.zeros_like(qb)), kb) * scale, NEG)
                p = jnp.exp(sc - lse_ref[rows, col])
                dp = _dot_nt(jnp.where(hm, dob_, jnp.zeros_like(dob_)), vb)
                ds = p * (dp - dl_ref[rows, col]) * scale
                outs.append(_dot(ds.astype(BF16), kb))
            dq_ref[rows, :] = jnp.where(head0, outs[0], outs[1])

    ospec = pl.BlockSpec((tq, LANES), lambda c, i: (i, c))
    dq = pl.pallas_call(
        body, name=name, grid=(d * w // LANES, l // tq),
        in_specs=[pl.BlockSpec((tq, LANES), lambda c, i: (i, _col_q(d)(c)))]
        + _window_specs(tq, l, _col_k(d)) + _window_specs(tq, l, _col_v(d)) + [ospec, ospec, ospec],
        out_specs=ospec,
        out_shape=jax.ShapeDtypeStruct((l, d * w), F32),
        compiler_params=_params("parallel", "parallel"),
    )(xv, xv, xv, xv, xv, xv, xv, dob.reshape(l, d * w), lse.reshape(l, d * w), delta.reshape(l, d * w))
    return dq.reshape(s, w)


def band_attention_dkv(qkvn, dob, lse, delta, d, name):
    s, w3 = qkvn.shape
    w = w3 // 3
    l = s // d
    tq = _tile(l, 512)
    scale = HEAD_DIM ** -0.5
    xv = qkvn.reshape(l, d * w3)

    def body(k_ref, v_ref, q_ref, qp_ref, qn_ref, do_ref, dop_ref, don_ref, lse_ref, lsep_ref, lsen_ref,
             dl_ref, dlp_ref, dln_ref, dk_ref, dv_ref):
        i = pl.program_id(1)
        qw, dow = _window(q_ref, qp_ref, qn_ref), _window(do_ref, dop_ref, don_ref)
        lsew, dlw = _window(lse_ref, lsep_ref, lsen_ref), _window(dl_ref, dlp_ref, dln_ref)
        head0 = lax.broadcasted_iota(jnp.int32, (2 * CHUNK, LANES), 1) < HEAD_DIM
        for b in range(tq // CHUNK):
            rows = pl.ds(b * CHUNK, CHUNK)
            mask = _band_mask((2 * CHUNK, CHUNK), 1, i * tq + b * CHUNK - BAND, l)
            kb, vb = k_ref[rows, :], v_ref[rows, :]
            win = slice(b * CHUNK, (b + 2) * CHUNK)
            qb, dob_, lseb, dlb = qw[win], dow[win], lsew[win], dlw[win]
            dk = jnp.zeros((CHUNK, LANES), F32)
            dv = jnp.zeros((CHUNK, LANES), F32)
            for h, hm in enumerate((head0, jnp.logical_not(head0))):
                col = slice(h * HEAD_DIM, h * HEAD_DIM + 1)
                qm = jnp.where(hm, qb, jnp.zeros_like(qb))
                dom = jnp.where(hm, dob_, jnp.zeros_like(dob_))
                sc = jnp.where(mask, _dot_nt(qm, kb) * scale, NEG)
                p = jnp.exp(sc - lseb[:, col])
                ds = p * (_dot_nt(dom, vb) - dlb[:, col]) * scale
                dv = dv + _dot_tn(p.astype(BF16), dom)
                dk = dk + _dot_tn(ds.astype(BF16), qm)
            dk_ref[rows, :] = dk
            dv_ref[rows, :] = dv

    ident = lambda c: c
    ospec = pl.BlockSpec((tq, LANES), lambda c, i: (i, c))
    dk, dv = pl.pallas_call(
        body, name=name, grid=(d * w // LANES, l // tq),
        in_specs=[pl.BlockSpec((tq, LANES), lambda c, i: (i, _col_k(d)(c))),
                  pl.BlockSpec((tq, LANES), lambda c, i: (i, _col_v(d)(c)))]
        + _window_specs(tq, l, _col_q(d)) + _window_specs(tq, l, ident) * 3,
        out_specs=[ospec, ospec],
        out_shape=[jax.ShapeDtypeStruct((l, d * w), F32)] * 2,
        compiler_params=_params("parallel", "parallel"),
    )(xv, xv, xv, xv, xv, *[dob.reshape(l, d * w)] * 3, *[lse.reshape(l, d * w)] * 3, *[delta.reshape(l, d * w)] * 3)
    return dk.reshape(s, w), dv.reshape(s, w)


ATT_TILE = 2048
ATT_HALO = BAND * max(DILATIONS)
ROWS_PER_COPY = 256


def _att_specs(s, t, col_fn, halo=True):
    hb, nhb = t // ATT_HALO, s // ATT_HALO
    specs = [pl.BlockSpec((t, LANES), lambda hp, i: (i, col_fn(hp)))]
    if halo:
        specs += [pl.BlockSpec((ATT_HALO, LANES), lambda hp, i: (jnp.maximum(i * hb - 1, 0), col_fn(hp))),
                  pl.BlockSpec((ATT_HALO, LANES), lambda hp, i: (jnp.minimum((i + 1) * hb, nhb - 1), col_fn(hp)))]
    return specs


def _gather_rows(dst_ref, dst_row, src_ref, start, count, stride, scale=None):
    for c in range(0, count, ROWS_PER_COPY):
        m = min(ROWS_PER_COPY, count - c)
        v = src_ref[pl.ds(start + c * stride, m, stride=stride), :]
        if scale is not None:
            v = v * scale
        dst_ref[dst_row + c:dst_row + c + m, :] = v.astype(dst_ref.dtype)


def _stage(dst_ref, cur_ref, d, t, scale=None):
    n = t // d
    for r in range(d):
        _gather_rows(dst_ref, r * n, cur_ref, r, n, d, scale)


def _stage_window(dst_ref, refs, d, t, scale=None, edges=None):
    cur_ref, prev_ref, next_ref = refs
    n = t // d
    nw = n + 2 * BAND
    for r in range(d):
        _gather_rows(dst_ref, r * nw, prev_ref, ATT_HALO - BAND * d + r, BAND, d, scale)
        _gather_rows(dst_ref, r * nw + BAND, cur_ref, r, n, d, scale)
        _gather_rows(dst_ref, r * nw + BAND + n, next_ref, r, BAND, d, scale)
    if edges is not None:
        first, last, value = edges
        fill = jnp.full((BAND, LANES), value, dst_ref.dtype)

        @pl.when(first)
        def _():
            for r in range(d):
                dst_ref[r * nw:r * nw + BAND, :] = fill

        @pl.when(last)
        def _():
            for r in range(d):
                dst_ref[r * nw + BAND + n:(r + 1) * nw, :] = fill


def _band_bias(rows, cols, centre_axis):
    shape = (rows // 2, cols)
    ctr = lax.broadcasted_iota(jnp.int32, shape, centre_axis)
    win = lax.broadcasted_iota(jnp.int32, shape, 1 - centre_axis)
    bias = jnp.where(jnp.abs(win - BAND - ctr) <= BAND, 0.0, NEG).astype(F32)
    return jnp.concatenate([bias, bias], axis=0)


def _window_bias(first_row, length):
    row = first_row + lax.broadcasted_iota(jnp.int32, (1, 2 * CHUNK), 1)
    return jnp.where((row >= 0) & (row < length), 0.0, NEG).astype(F32)


UNITS_PER_TRIP = 8


def _scatter_rows(dst_ref, src_ref, d, t, combine):
    n = t // d
    for r in range(d):
        for c in range(0, n, ROWS_PER_COPY):
            m = min(ROWS_PER_COPY, n - c)
            idx = pl.ds(r + c * d, m, stride=d)
            combine(idx, slice(r * n + c, r * n + c + m))


def _unit_rows(u, n):
    upr = n // CHUNK
    r = u // upr
    b = u - r * upr
    return pl.multiple_of(u * CHUNK, CHUNK), pl.multiple_of((u + r) * CHUNK, CHUNK), b * CHUNK - BAND


def _two_heads(x):
    head0 = lax.broadcasted_iota(jnp.int32, x.shape, 1) < HEAD_DIM
    zero = jnp.zeros_like(x)
    return jnp.concatenate([jnp.where(head0, x, zero), jnp.where(head0, zero, x)], axis=0)


def _head_columns(x):
    return jnp.concatenate([x[:, 0:1], x[:, HEAD_DIM:HEAD_DIM + 1]], axis=0)


def _merge_heads(x2):
    rows = x2.shape[0] // 2
    head0 = lax.broadcasted_iota(jnp.int32, (rows, LANES), 1) < HEAD_DIM
    return jnp.where(head0, jnp.broadcast_to(x2[:rows], (rows, LANES)), jnp.broadcast_to(x2[rows:], (rows, LANES)))


def _col(part):
    return lambda hp: part * 8 + hp


def attention_forward(qk, qkv, name):
    s, w2 = qk.shape
    w = w2 // 2
    t = _tile(s, ATT_TILE)
    scale = HEAD_DIM ** -0.5

    def body(q_ref, k_ref, kp_ref, kn_ref, v_ref, vp_ref, vn_ref, o_ref, lse_ref, qs_ref, ks_ref, vs_ref, os_ref, ls_ref, or_ref):
        i = pl.program_id(1)
        band = _band_bias(2 * CHUNK, 2 * CHUNK, 0)
        for pi, d in enumerate(DILATIONS):
            n = t // d
            _stage(qs_ref, q_ref, d, t, scale)
            _stage_window(ks_ref, (k_ref, kp_ref, kn_ref), d, t)
            _stage_window(vs_ref, (v_ref, vp_ref, vn_ref), d, t)

            def unit(u, carry, n=n, d=d):
                qrow, wrow, first = _unit_rows(u, n)
                kb, vb = ks_ref[pl.ds(wrow, 2 * CHUNK), :], vs_ref[pl.ds(wrow, 2 * CHUNK), :]
                sc = _dot_nt(_two_heads(qs_ref[pl.ds(qrow, CHUNK), :]), kb) + band + _window_bias(i * n + first, s // d)
                m = jnp.max(sc, axis=1, keepdims=True)
                p = jnp.exp(sc - m)
                den = jnp.sum(p, axis=1, keepdims=True)
                os_ref[pl.ds(qrow, CHUNK), :] = _merge_heads(_dot(p.astype(BF16), vb) / den)
                ls_ref[pl.ds(qrow, CHUNK), :] = _merge_heads(m + jnp.log(den))
                return carry

            lax.fori_loop(0, t // CHUNK, unit, 0, unroll=UNITS_PER_TRIP)

            if pi == 0:
                def assign(idx, rows):
                    or_ref[idx, :] = os_ref[rows, :]
                    lse_ref[idx, :] = ls_ref[rows, :]
                _scatter_rows(None, None, d, t, assign)
            else:
                def merge(idx, rows):
                    la, lb = lse_ref[idx, :], ls_ref[rows, :]
                    mx = jnp.maximum(la, lb)
                    wa, wb = jnp.exp(la - mx), jnp.exp(lb - mx)
                    den = wa + wb
                    or_ref[idx, :] = (wa * or_ref[idx, :] + wb * os_ref[rows, :]) / den
                    lse_ref[idx, :] = mx + jnp.log(den)
                _scatter_rows(None, None, d, t, merge)
        o_ref[...] = or_ref[...].astype(BF16)

    ospec = pl.BlockSpec((t, LANES), lambda hp, i: (i, hp))
    win_rows = t + 2 * ATT_HALO
    o, lse = pl.pallas_call(
        body, name=name, grid=(w // LANES, s // t),
        in_specs=_att_specs(s, t, _col(0), halo=False) + _att_specs(s, t, _col(1)) + _att_specs(s, t, _col(2)),
        out_specs=[ospec, ospec],
        out_shape=[jax.ShapeDtypeStruct((s, w), BF16), jax.ShapeDtypeStruct((s, w), F32)],
        scratch_shapes=[pltpu.VMEM((t, LANES), BF16), pltpu.VMEM((win_rows, LANES), BF16), pltpu.VMEM((win_rows, LANES), BF16),
                        pltpu.VMEM((t, LANES), F32), pltpu.VMEM((t, LANES), F32), pltpu.VMEM((t, LANES), F32)],
        compiler_params=_params("parallel", "parallel"),
    )(qk, qk, qk, qk, qkv, qkv, qkv)
    return o, lse


def attention_delta(do, o, lse, name):
    s, w = do.shape
    tm = _tile(s, 512)

    def body(do_ref, o_ref, lse_ref, dl_ref, st_ref):
        bd = _block_diag(LANES)
        lane = lax.broadcasted_iota(jnp.int32, (tm, LANES), 1)
        for b in range(w // LANES):
            cols = pl.ds(b * LANES, LANES)
            dl = _seg_sum(do_ref[:, cols] * o_ref[:, cols].astype(F32), bd)
            dl_ref[:, cols] = dl
            ls = lse_ref[:, cols]
            st_ref[:, cols] = jnp.where(lane == 0, ls, jnp.where(
                lane == 1, pltpu.roll(ls, HEAD_DIM - 1, 1), jnp.where(
                    lane == 2, pltpu.roll(dl, 2, 1), pltpu.roll(dl, HEAD_DIM + 3, 1))))

    spec = pl.BlockSpec((tm, w), lambda i: (i, 0))
    return pl.pallas_call(
        body, name=name, grid=(s // tm,), in_specs=[spec, spec, spec], out_specs=[spec, spec],
        out_shape=[jax.ShapeDtypeStruct((s, w), F32)] * 2, compiler_params=_params("parallel"),
    )(do, o, lse)


def attention_dq(qk, qkv, do, lse, delta, name):
    s, w2 = qk.shape
    w = w2 // 2
    t = _tile(s, ATT_TILE)
    scale = HEAD_DIM ** -0.5

    def body(q_ref, k_ref, kp_ref, kn_ref, v_ref, vp_ref, vn_ref, do_ref, lse_ref, dl_ref, dq_ref,
             qs_ref, ks_ref, vs_ref, dos_ref, ls_ref, dls_ref, dqs_ref):
        i = pl.program_id(1)
        band = _band_bias(2 * CHUNK, 2 * CHUNK, 0)
        for pi, d in enumerate(DILATIONS):
            n = t // d
            _stage(qs_ref, q_ref, d, t, scale)
            _stage(dos_ref, do_ref, d, t)
            _stage(ls_ref, lse_ref, d, t)
            _stage(dls_ref, dl_ref, d, t)
            _stage_window(ks_ref, (k_ref, kp_ref, kn_ref), d, t)
            _stage_window(vs_ref, (v_ref, vp_ref, vn_ref), d, t)

            def unit(u, carry, n=n, d=d):
                qrow, wrow, first = _unit_rows(u, n)
                rows = pl.ds(qrow, CHUNK)
                kb, vb = ks_ref[pl.ds(wrow, 2 * CHUNK), :], vs_ref[pl.ds(wrow, 2 * CHUNK), :]
                sc = _dot_nt(_two_heads(qs_ref[rows, :]), kb) + band + _window_bias(i * n + first, s // d)
                p = jnp.exp(sc - _head_columns(ls_ref[rows, :]))
                dp = _dot_nt(_two_heads(dos_ref[rows, :]), vb)
                ds = p * (dp - _head_columns(dls_ref[rows, :]))
                dqs_ref[rows, :] = _merge_heads(_dot(ds.astype(BF16), kb)) * scale
                return carry

            lax.fori_loop(0, t // CHUNK, unit, 0, unroll=UNITS_PER_TRIP)

            def add(idx, rows, pi=pi):
                dq_ref[idx, :] = dqs_ref[rows, :] if pi == 0 else dq_ref[idx, :] + dqs_ref[rows, :]
            _scatter_rows(None, None, d, t, add)

    ospec = pl.BlockSpec((t, LANES), lambda hp, i: (i, hp))
    win_rows = t + 2 * ATT_HALO
    return pl.pallas_call(
        body, name=name, grid=(w // LANES, s // t),
        in_specs=_att_specs(s, t, _col(0), halo=False) + _att_specs(s, t, _col(1)) + _att_specs(s, t, _col(2))
        + [ospec, ospec, ospec],
        out_specs=ospec,
        out_shape=jax.ShapeDtypeStruct((s, w), F32),
        scratch_shapes=[pltpu.VMEM((t, LANES), BF16), pltpu.VMEM((win_rows, LANES), BF16), pltpu.VMEM((win_rows, LANES), BF16),
                        pltpu.VMEM((t, LANES), BF16), pltpu.VMEM((t, LANES), F32), pltpu.VMEM((t, LANES), F32),
                        pltpu.VMEM((t, LANES), F32)],
        compiler_params=_params("parallel", "parallel"),
    )(qk, qk, qk, qk, qkv, qkv, qkv, do, lse, delta)


def attention_dkv(qk, qkv, do, stats, name):
    s, w2 = qk.shape
    w = w2 // 2
    t = _tile(s, ATT_TILE)
    scale = HEAD_DIM ** -0.5

    def body(k_ref, v_ref, q_ref, qp_ref, qn_ref, do_ref, dop_ref, don_ref, st_ref, stp_ref, stn_ref,
             dk_ref, dv_ref, ks_ref, vs_ref, qs_ref, dos_ref, sts_ref, dks_ref, dvs_ref):
        i = pl.program_id(1)
        key = lax.broadcasted_iota(jnp.int32, (CHUNK, 2 * CHUNK), 0)
        win_ = lax.broadcasted_iota(jnp.int32, (CHUNK, 2 * CHUNK), 1)
        half = jnp.where(jnp.abs(win_ - BAND - key) <= BAND, 0.0, NEG).astype(F32)
        band = jnp.concatenate([half, half], axis=1)
        edges = (i == 0, i == s // t - 1, -NEG)
        for pi, d in enumerate(DILATIONS):
            n = t // d
            _stage(ks_ref, k_ref, d, t)
            _stage(vs_ref, v_ref, d, t)
            _stage_window(qs_ref, (q_ref, qp_ref, qn_ref), d, t, scale)
            _stage_window(dos_ref, (do_ref, dop_ref, don_ref), d, t)
            _stage_window(sts_ref, (st_ref, stp_ref, stn_ref), d, t, edges=edges)

            def unit(u, carry, n=n, d=d):
                krow, wrow, _ = _unit_rows(u, n)
                rows, win = pl.ds(krow, CHUNK), pl.ds(wrow, 2 * CHUNK)
                q2, do2 = _two_heads(qs_ref[win, :]), _two_heads(dos_ref[win, :])
                st = jnp.transpose(sts_ref[win, :])
                lse2 = jnp.concatenate([st[0:1, :], st[1:2, :]], axis=1)
                dl2 = jnp.concatenate([st[2:3, :], st[3:4, :]], axis=1)
                p = jnp.exp(_dot_nt(ks_ref[rows, :], q2) + band - lse2)
                ds = p * (_dot_nt(vs_ref[rows, :], do2) - dl2)
                dvs_ref[rows, :] = _dot(p.astype(BF16), do2)
                dks_ref[rows, :] = _dot(ds.astype(BF16), q2)
                return carry

            lax.fori_loop(0, t // CHUNK, unit, 0, unroll=UNITS_PER_TRIP)

            def add(idx, rows, pi=pi):
                dk_ref[idx, :] = dks_ref[rows, :] if pi == 0 else dk_ref[idx, :] + dks_ref[rows, :]
                dv_ref[idx, :] = dvs_ref[rows, :] if pi == 0 else dv_ref[idx, :] + dvs_ref[rows, :]
            _scatter_rows(None, None, d, t, add)

    ident = lambda hp: hp
    ospec = pl.BlockSpec((t, LANES), lambda hp, i: (i, hp))
    win_rows = t + 2 * ATT_HALO
    return pl.pallas_call(
        body, name=name, grid=(w // LANES, s // t),
        in_specs=_att_specs(s, t, _col(1), halo=False) + _att_specs(s, t, _col(2), halo=False)
        + _att_specs(s, t, _col(0)) + _att_specs(s, t, ident) * 2,
        out_specs=[ospec, ospec],
        out_shape=[jax.ShapeDtypeStruct((s, w), F32)] * 2,
        scratch_shapes=[pltpu.VMEM((t, LANES), BF16), pltpu.VMEM((t, LANES), BF16),
                        pltpu.VMEM((win_rows, LANES), BF16), pltpu.VMEM((win_rows, LANES), BF16),
                        pltpu.VMEM((win_rows, LANES), F32),
                        pltpu.VMEM((t, LANES), F32), pltpu.VMEM((t, LANES), F32)],
        compiler_params=_params("parallel", "parallel"),
    )(qk, qkv, qk, qk, qk, do, do, do, stats, stats, stats)


def adamw_update(recvs, w, m, v, name):
    nl = len(recvs)
    r, c = recvs[0].shape[1:]
    tr = 256 if (r > 256 and r % 256 == 0) else r
    nt = r // tr
    c1 = 1.0 - ADAM_B1 ** ADAM_STEP
    c2 = 1.0 - ADAM_B2 ** ADAM_STEP

    def body(*refs):
        g_refs = refs[:nl]
        w_ref, m_ref, v_ref, go_ref, d_ref, mo_ref, vo_ref = refs[nl:]

        def update(g_ref):
            g = g_ref[0].astype(F32)
            for j in range(1, N_DEV):
                g = g + g_ref[j].astype(F32)
            mn = ADAM_B1 * m_ref[...] + (1.0 - ADAM_B1) * g
            vn = ADAM_B2 * v_ref[...] + (1.0 - ADAM_B2) * (g * g)
            go_ref[...] = g
            mo_ref[...] = mn
            vo_ref[...] = vn
            d_ref[...] = -ADAM_LR * ((mn / c1) / (jnp.sqrt(vn / c2) + ADAM_EPS) + ADAM_WD * w_ref[...])

        for layer in range(nl):
            pl.when(pl.program_id(0) == layer)(functools.partial(update, g_refs[layer]))

    def gspec(layer):
        return pl.BlockSpec((N_DEV, tr, c), lambda l, i: (0, jnp.where(l == layer, i, 0), 0))

    spec = pl.BlockSpec((tr, c), lambda l, i: (l * nt + i, 0))
    return pl.pallas_call(
        body, name=name, grid=(nl, nt),
        in_specs=[gspec(layer) for layer in range(nl)] + [spec, spec, spec],
        out_specs=[spec] * 4, out_shape=[jax.ShapeDtypeStruct((nl * r, c), F32)] * 4,
        compiler_params=_params("arbitrary", "arbitrary"),
    )(*recvs, w, m, v)


BIG = ("mlp_w1", "mlp_w2", "ab_w_in", "ab_w_out", "c_w_qkv", "c_w_out")
SMALL = ("mix_norm_g", "mlp_norm_g", "a_spatial_w", "a_spatial_b", "a_vnorm_g", "a_vnorm_b", "b_conv_b", "b_norm_g",
         "b_norm_b", "c_q_norm_g", "c_k_norm_g")
WEIGHTS = ("mix_norm_g", "mlp_norm_g", "mlp_w1", "mlp_w2", "ab_w_in", "a_spatial_w", "a_spatial_b", "a_vnorm_g",
           "a_vnorm_b", "b_conv_w", "b_conv_b", "b_norm_g", "b_norm_b", "ab_w_out", "c_w_qkv", "c_q_norm_g",
           "c_k_norm_g", "c_w_out")


def _mixer_params(p, conv_full, i):
    aw = p["a_vnorm_g"].shape[1]
    row = lambda t: t[i][None, :]
    return dict(
        w=p["a_spatial_w"][i].astype(BF16), wt=jnp.swapaxes(p["a_spatial_w"][i], 1, 2).astype(BF16),
        bias=jnp.repeat(p["a_spatial_b"][i].T, aw // p["a_spatial_b"].shape[1], axis=1),
        vg=row(p["a_vnorm_g"]), vb=row(p["a_vnorm_b"]), cw=jnp.pad(conv_full[i], ((0, 1), (0, 0))),
        cb=row(p["b_conv_b"]), cg=row(p["b_norm_g"]), cbn=row(p["b_norm_b"]))


def _head_gains(p, i):
    rep = LANES // HEAD_DIM
    return jnp.stack([jnp.tile(p["c_q_norm_g"][i], rep), jnp.tile(p["c_k_norm_g"][i], rep),
                      jnp.ones((LANES,), F32)])[:, None, :]


def _unused_forward_backward(x, target, p, wg, conv_full):
    s, d = x.shape
    depth = p["mix_norm_g"].shape[0]
    tabs = rope_tables(s)
    saved = []
    for l in range(depth):
        i = l // 2
        mix_g, mlp_g = p["mix_norm_g"][l][None, :], p["mlp_norm_g"][l][None, :]
        st = dict(x_in=x)
        if l % 2 == 0:
            sp = _mixer_params(p, conv_full, i)
            z, h = norm_matmul(x, mix_g, wg["ab_w_in"], i, f"ab_in_{l}")
            ycat, gc = ab_mid_forward(z, sp, f"ab_mid_{l}")
            x = matmul_residual(x, ycat, wg["ab_w_out"], i, f"ab_out_{l}")
            st.update(z=z, h=h, y=ycat, gc=gc, sp=sp)
        else:
            gains = _head_gains(p, i)
            qkv, h = norm_matmul(x, mix_g, wg["c_w_qkv"], i, f"c_qkv_{l}")
            qkvn = qk_prep_forward(qkv, tabs, gains, f"c_prep_{l}")
            outs = [band_attention_forward(qkvn, dil, f"c_attn_{l}_d{dil}") for dil in DILATIONS]
            o, lse = attention_merge([t[0] for t in outs], [t[1] for t in outs], f"c_merge_{l}")
            x = matmul_residual(x, o, wg["c_w_out"], i, f"c_out_{l}")
            st.update(qkv=qkv, h=h, qkvn=qkvn, y=o, lse=lse, gains=gains)
        st["x_mid"] = x
        x, a, h2 = mlp_forward(x, mlp_g, wg["mlp_w1"], wg["mlp_w2"], l, f"mlp_{l}")
        st.update(a=a, h2=h2)
        saved.append(st)

    dy, loss_part = loss_and_grad(x, target, "loss")

    big = {n: [None] * p_len for n, p_len in (("mlp_w1", depth), ("mlp_w2", depth), ("ab_w_in", depth // 2 + depth % 2),
                                              ("ab_w_out", depth // 2 + depth % 2), ("c_w_qkv", depth // 2),
                                              ("c_w_out", depth // 2))}
    small = {n: [None] * p[n].shape[0] for n in SMALL}
    conv_grads = [None] * p["b_conv_b"].shape[0]
    for l in reversed(range(depth)):
        i, st = l // 2, saved[l]
        mix_g, mlp_g = p["mix_norm_g"][l][None, :], p["mlp_norm_g"][l][None, :]
        dxm, da, dg = mlp_backward(dy, st["a"], st["x_mid"], mlp_g, wg["mlp_w1"], wg["mlp_w2"], l, f"mlp_bwd_{l}")
        small["mlp_norm_g"][l] = dg[0]
        big["mlp_w1"][l] = matmul_tn(st["h2"], da, "cols", f"mlp_dw1_{l}")
        big["mlp_w2"][l] = matmul_tn(st["a"], dy, "rows", f"mlp_dw2_{l}", relu2=True)
        if l % 2 == 0:
            sp = st["sp"]
            dycat = matmul_nt(dxm, wg["ab_w_out"], i, f"ab_out_bwd_{l}")
            big["ab_w_out"][i] = matmul_tn(st["y"], dxm, "rows", f"ab_dwout_{l}")
            dz, dgc, dw, dsb, dvg, dvb, dcg, dcbn, dcb = ab_mid_backward(dycat, st["z"], st["gc"], sp, f"ab_mid_bwd_{l}")
            dz, dcw = conv_backward(dgc, st["z"], dz, sp, f"ab_conv_bwd_{l}")
            small["a_spatial_w"][i], small["a_spatial_b"][i] = dw, dsb
            small["a_vnorm_g"][i], small["a_vnorm_b"][i] = dvg[0], dvb[0]
            small["b_norm_g"][i], small["b_norm_b"][i], small["b_conv_b"][i] = dcg[0], dcbn[0], dcb[0]
            conv_grads[i] = dcw[:CONV_W]
            dy, dg = matmul_nt_norm_backward(dz, wg["ab_w_in"], i, st["x_in"], mix_g, dxm, f"ab_in_bwd_{l}")
            big["ab_w_in"][i] = matmul_tn(st["h"], dz, "cols", f"ab_dwin_{l}")
        else:
            do = matmul_nt(dxm, wg["c_w_out"], i, f"c_out_bwd_{l}")
            big["c_w_out"][i] = matmul_tn(st["y"], dxm, "rows", f"c_dwout_{l}")
            delta, dob = attention_delta(do, st["y"], f"c_delta_{l}")
            dqs, dks, dvs = [], [], []
            for dil in DILATIONS:
                dqs.append(band_attention_dq(st["qkvn"], dob, st["lse"], delta, dil, f"c_attn_dq_{l}_d{dil}"))
                dk, dv = band_attention_dkv(st["qkvn"], dob, st["lse"], delta, dil, f"c_attn_dkv_{l}_d{dil}")
                dks.append(dk)
                dvs.append(dv)
            dqkv, dgn = qk_prep_backward(dqs, dks, dvs, st["qkv"], tabs, st["gains"], f"c_prep_bwd_{l}")
            small["c_q_norm_g"][i] = dgn[0, 0, :HEAD_DIM] + dgn[0, 0, HEAD_DIM:]
            small["c_k_norm_g"][i] = dgn[1, 0, :HEAD_DIM] + dgn[1, 0, HEAD_DIM:]
            dy, dg = matmul_nt_norm_backward(dqkv, wg["c_w_qkv"], i, st["x_in"], mix_g, dxm, f"c_qkv_bwd_{l}")
            big["c_w_qkv"][i] = matmul_tn(st["h"], dqkv, "cols", f"c_dwqkv_{l}")
        small["mix_norm_g"][l] = dg[0]
    small = {n: jnp.stack(v) for n, v in small.items()}
    return loss_part, dy, big, small, jnp.stack(conv_grads)


def _pack(d):
    flat = jnp.concatenate([d[n].reshape(-1) for n in SMALL])
    rows = -(-flat.shape[0] // (8 * LANES)) * 8
    return jnp.pad(flat, (0, rows * LANES - flat.shape[0])).reshape(rows, LANES)


def _unpack(packed, like):
    flat, out, pos = packed.reshape(-1), {}, 0
    for n in SMALL:
        size = math.prod(like[n].shape)
        out[n] = flat[pos:pos + size].reshape(like[n].shape)
        pos += size
    return out


def _unused_kernel(x, mix_norm_g, mlp_norm_g, mlp_w1, mlp_w2, ab_w_in, a_spatial_w, a_spatial_b, a_vnorm_g, a_vnorm_b, b_conv_w, b_conv_b, b_norm_g, b_norm_b, ab_w_out, c_w_qkv, c_q_norm_g, c_k_norm_g, c_w_out, loss_target, m_mix_norm_g, m_mlp_norm_g, m_mlp_w1, m_mlp_w2, m_ab_w_in, m_a_spatial_w, m_a_spatial_b, m_a_vnorm_g, m_a_vnorm_b, m_b_conv_w, m_b_conv_b, m_b_norm_g, m_b_norm_b, m_ab_w_out, m_c_w_qkv, m_c_q_norm_g, m_c_k_norm_g, m_c_w_out, v_mix_norm_g, v_mlp_norm_g, v_mlp_w1, v_mlp_w2, v_ab_w_in, v_a_spatial_w, v_a_spatial_b, v_a_vnorm_g, v_a_vnorm_b, v_b_conv_w, v_b_conv_b, v_b_norm_g, v_b_norm_b, v_ab_w_out, v_c_w_qkv, v_c_q_norm_g, v_c_k_norm_g, v_c_w_out):
    args = dict(locals())
    w = {n: args[n] for n in WEIGHTS}
    m = {n: args["m_" + n] for n in WEIGHTS}
    v = {n: args["v_" + n] for n in WEIGHTS}

    wg = {n: all_gather(w[n].astype(BF16), "gather_" + n) for n in BIG}
    for n in ("ab_w_out", "c_w_out"):
        t = wg[n]
        wg[n] = t.reshape(t.shape[0], t.shape[1] * t.shape[2], t.shape[3])
    conv = all_gather(w["b_conv_w"], "gather_b_conv_w")
    conv_full = jnp.swapaxes(conv, 1, 2).reshape(conv.shape[0], conv.shape[2], N_DEV * conv.shape[3])

    loss_part, dx, big, small, conv_grad = forward_backward(x[0], loss_target[0], w, wg, conv_full)
    loss = lax.psum(jnp.sum(loss_part), ("x", "y", "c"))

    grads, deltas, new_m, new_v = {}, {}, {}, {}

    def update(n, recv):
        shape = w[n].shape
        flat = lambda t: t.reshape(-1, shape[-1])
        outs = adamw_update(recv.reshape((N_DEV, -1, shape[-1])), flat(w[n]), flat(m[n]), flat(v[n]), "adamw_" + n)
        grads[n], deltas[n], new_m[n], new_v[n] = (t.reshape(shape) for t in outs)

    for n in BIG:
        update(n, exchange(big[n], "exchange_" + n))
    nl, kw, cw = conv_grad.shape
    conv_parts = jnp.transpose(conv_grad.reshape(nl, kw, N_DEV, cw // N_DEV), (2, 0, 1, 3))
    update("b_conv_w", exchange([conv_parts], "exchange_b_conv_w"))

    packed = all_gather(_pack(small)[None], "gather_small_grads")[0]
    outs = adamw_update(packed, _pack(w), _pack(m), _pack(v), "adamw_small")
    for dst, t in zip((grads, deltas, new_m, new_v), outs):
        dst.update(_unpack(t, w))

    return (loss, dx[None], *[grads[n] for n in WEIGHTS], *[deltas[n] for n in WEIGHTS],
            *[new_m[n] for n in WEIGHTS], *[new_v[n] for n in WEIGHTS])


class Traffic:
    def __init__(self, shards, full=()):
        self.shards, self.w, self.queue, self.parts = shards, dict(full), [], {}

    def run(self, fn, *args, gather=(), send=False, **kw):
        operands, flags, dest = [], [], []
        if self.shards is None:
            if send:
                self.parts.update(self.queue)
                self.queue = []
        else:
            for k in gather:
                if k not in self.w:
                    operands.append(self.shards[k])
                    flags.append(("gather", SHARD_AXIS.get(k[0])))
                    dest.append((self.w, k))
            if send:
                for k, t in self.queue:
                    operands.append(t)
                    flags.append(("scatter", SHARD_AXIS.get(k[0])))
                    dest.append((self.parts, k))
                self.queue = []
        outs, couts = fn(*args, comm=PeerCopies(operands, flags) if operands else None, **kw)
        for (table, k), t in zip(dest, couts):
            table[k] = t
        return outs

    def flush(self, name, extra=()):
        self.queue += list(extra)
        self.run(lambda comm: ([], run_copies(comm, name) if comm is not None else []), send=True)


SHARD_AXIS = {"mlp_w1": 1, "mlp_w2": 0, "ab_w_in": 1, "ab_w_out": 0, "c_w_qkv": 1, "c_w_out": 0}


def forward_backward(x, target, p, tr, conv_full):
    s, d = x.shape
    depth = p["mix_norm_g"].shape[0]
    tabs = rope_tables(s)
    saved = []
    for l in range(depth):
        i = l // 2
        mix_g, mlp_g = p["mix_norm_g"][l][None, :], p["mlp_norm_g"][l][None, :]
        st = dict(x_in=x)
        nxt = () if l + 1 == depth else ((("c_w_qkv", i), ("c_w_out", i)) if l % 2 == 0 else
                                        (("ab_w_in", i + 1), ("ab_w_out", i + 1)))
        if l % 2 == 0:
            sp = _mixer_params(p, conv_full, i)
            z, h = tr.run(norm_matmul, x, mix_g, tr.w["ab_w_in", i], f"ab_in_{l}", gather=[("mlp_w1", l)])
            ycat, gc = tr.run(ab_mid_forward, z, sp, f"ab_mid_{l}", gather=[("mlp_w2", l)])
            x = matmul_residual(x, ycat, tr.w["ab_w_out", i], f"ab_out_{l}")
            st.update(z=z, h=h, y=ycat, gc=gc, sp=sp)
        else:
            gains = _head_gains(p, i)
            qkv, h = tr.run(norm_matmul, x, mix_g, tr.w["c_w_qkv", i], f"c_qkv_{l}", gather=[("mlp_w1", l)])
            qk = tr.run(lambda *a, comm: (lambda o, c: ([o], c))(*qk_prep_forward(*a, comm=comm)),
                        qkv, tabs, gains, f"c_prep_{l}", gather=[("mlp_w2", l)])[0]
            o, lse = attention_forward(qk, qkv, f"c_attn_{l}")
            x = matmul_residual(x, o, tr.w["c_w_out", i], f"c_out_{l}")
            st.update(qkv=qkv, h=h, qk=qk, y=o, lse=lse, gains=gains)
        st["x_mid"] = x
        x, a, h2 = tr.run(mlp_forward, x, mlp_g, tr.w["mlp_w1", l], tr.w["mlp_w2", l], f"mlp_{l}", gather=nxt)
        st.update(a=a, h2=h2)
        saved.append(st)

    dy, loss_part = loss_and_grad(x, target, "loss")

    def tn(*a, comm, **kw):
        out, couts = matmul_tn(*a, comm=comm, **kw)
        return [out], couts

    small = {n: [None] * p[n].shape[0] for n in SMALL}
    conv_grads = [None] * p["b_conv_b"].shape[0]
    for l in reversed(range(depth)):
        i, st = l // 2, saved[l]
        mix_g, mlp_g = p["mix_norm_g"][l][None, :], p["mlp_norm_g"][l][None, :]
        w1, w2 = tr.w["mlp_w1", l], tr.w["mlp_w2", l]
        dxm, da, dg, dyb = tr.run(mlp_backward, dy, st["a"], st["x_mid"], mlp_g, w1, w2, f"mlp_bwd_{l}", send=True)
        small["mlp_norm_g"][l] = dg[0]
        tr.queue.append((("mlp_w1", l), matmul_tn(st["h2"], da, f"mlp_dw1_{l}", n_split=2)[0]))
        dw2 = tr.run(tn, st["a"], dyb, f"mlp_dw2_{l}", m_split=2, relu2=True, send=True)[0]
        tr.queue.append((("mlp_w2", l), dw2))
        if l % 2 == 0:
            sp = st["sp"]
            wout = tr.w["ab_w_out", i]
            dycat = matmul_nt(dxm, wout, f"ab_out_bwd_{l}")
            dwout = tr.run(tn, st["y"], dxm, f"ab_dwout_{l}", send=True)[0]
            tr.queue.append((("ab_w_out", i), dwout))
            dz, dgc, dw, dsb, dvg, dvb, dcg, dcbn, dcb = ab_mid_backward(dycat, st["z"], st["gc"], sp, f"ab_mid_bwd_{l}")
            dz, dcw = conv_backward(dgc, st["z"], dz, sp, f"ab_conv_bwd_{l}")
            small["a_spatial_w"][i], small["a_spatial_b"][i] = dw, dsb
            small["a_vnorm_g"][i], small["a_vnorm_b"][i] = dvg[0], dvb[0]
            small["b_norm_g"][i], small["b_norm_b"][i], small["b_conv_b"][i] = dcg[0], dcbn[0], dcb[0]
            conv_grads[i] = dcw[:CONV_W]
            dy, dg = matmul_nt_norm_backward(dz, tr.w["ab_w_in", i], st["x_in"], mix_g, dxm, f"ab_in_bwd_{l}")
            small["mix_norm_g"][l] = dg[0]
            last = []
            if l == 0 and tr.shards is not None:
                tr.shards["small_grads", 0] = _pack({n: jnp.stack(t) for n, t in small.items()})
                last = [("small_grads", 0)]
            dwin = tr.run(tn, st["h"], dz, f"ab_dwin_{l}", send=True, gather=last)[0]
            tr.queue.append((("ab_w_in", i), dwin))
        else:
            wout = tr.w["c_w_out", i]
            do = matmul_nt(dxm, wout, f"c_out_bwd_{l}")
            dwout = tr.run(tn, st["y"], dxm, f"c_dwout_{l}", send=True)[0]
            tr.queue.append((("c_w_out", i), dwout))
            delta, stats = attention_delta(do, st["y"], st["lse"], f"c_delta_{l}")
            dq = attention_dq(st["qk"], st["qkv"], do, st["lse"], delta, f"c_attn_dq_{l}")
            dk, dv = attention_dkv(st["qk"], st["qkv"], do, stats, f"c_attn_dkv_{l}")
            dqkv, dgn = qk_prep_backward(dq, dk, dv, st["qkv"], tabs, st["gains"], f"c_prep_bwd_{l}")
            small["c_q_norm_g"][i] = dgn[0, 0, :HEAD_DIM] + dgn[0, 0, HEAD_DIM:]
            small["c_k_norm_g"][i] = dgn[1, 0, :HEAD_DIM] + dgn[1, 0, HEAD_DIM:]
            dy, dg = matmul_nt_norm_backward(dqkv, tr.w["c_w_qkv", i], st["x_in"], mix_g, dxm, f"c_qkv_bwd_{l}")
            small["mix_norm_g"][l] = dg[0]
            dwqkv = tr.run(tn, st["h"], dqkv, f"c_dwqkv_{l}", n_split=2, send=True)[0]
            tr.queue.append((("c_w_qkv", i), dwqkv))
    small = {n: jnp.stack(v) for n, v in small.items()}
    return loss_part, dy, small, jnp.stack(conv_grads)


def kernel(x, mix_norm_g, mlp_norm_g, mlp_w1, mlp_w2, ab_w_in, a_spatial_w, a_spatial_b, a_vnorm_g, a_vnorm_b, b_conv_w, b_conv_b, b_norm_g, b_norm_b, ab_w_out, c_w_qkv, c_q_norm_g, c_k_norm_g, c_w_out, loss_target, m_mix_norm_g, m_mlp_norm_g, m_mlp_w1, m_mlp_w2, m_ab_w_in, m_a_spatial_w, m_a_spatial_b, m_a_vnorm_g, m_a_vnorm_b, m_b_conv_w, m_b_conv_b, m_b_norm_g, m_b_norm_b, m_ab_w_out, m_c_w_qkv, m_c_q_norm_g, m_c_k_norm_g, m_c_w_out, v_mix_norm_g, v_mlp_norm_g, v_mlp_w1, v_mlp_w2, v_ab_w_in, v_a_spatial_w, v_a_spatial_b, v_a_vnorm_g, v_a_vnorm_b, v_b_conv_w, v_b_conv_b, v_b_norm_g, v_b_norm_b, v_ab_w_out, v_c_w_qkv, v_c_q_norm_g, v_c_k_norm_g, v_c_w_out):
    args = dict(locals())
    w = {n: args[n] for n in WEIGHTS}
    m = {n: args["m_" + n] for n in WEIGHTS}
    v = {n: args["v_" + n] for n in WEIGHTS}

    shards = {(n, l): w[n][l].astype(BF16) for n in BIG for l in range(w[n].shape[0])}
    shards["b_conv_w", 0] = w["b_conv_w"]
    tr = Traffic(shards)
    first = [("ab_w_in", 0), ("ab_w_out", 0), ("b_conv_w", 0)]
    tr.run(lambda comm: ([], run_copies(comm, "gather_first")), gather=first)
    conv = tr.w["b_conv_w", 0]
    conv_full = jnp.transpose(conv, (1, 2, 0, 3)).reshape(conv.shape[1], conv.shape[2], -1)

    loss_part, dx, small, conv_grad = forward_backward(x[0], loss_target[0], w, tr, conv_full)
    loss = lax.psum(jnp.sum(loss_part), ("x", "y", "c"))
    nl, kw, cw = conv_grad.shape
    conv_parts = jnp.transpose(conv_grad.reshape(nl, kw, N_DEV, cw // N_DEV), (2, 0, 1, 3))
    tr.flush("exchange_last", [(("b_conv_w", 0), conv_parts)])
    packed = tr.w["small_grads", 0]

    grads, deltas, new_m, new_v = {}, {}, {}, {}

    def update(n, recvs):
        shape = w[n].shape
        flat = lambda t: t.reshape(-1, shape[-1])
        recvs = [t.reshape(N_DEV, -1, shape[-1]) for t in recvs]
        outs = adamw_update(recvs, flat(w[n]), flat(m[n]), flat(v[n]), "adamw_" + n)
        grads[n], deltas[n], new_m[n], new_v[n] = (t.reshape(shape) for t in outs)

    for n in BIG:
        update(n, [tr.parts[n, l] for l in range(w[n].shape[0])])
    update("b_conv_w", [tr.parts["b_conv_w", 0]])
    outs = adamw_update([packed], _pack(w), _pack(m), _pack(v), "adamw_small")
    for dst, t in zip((grads, deltas, new_m, new_v), outs):
        dst.update(_unpack(t, w))

    return (loss, dx[None], *[grads[n] for n in WEIGHTS], *[deltas[n] for n in WEIGHTS],
            *[new_m[n] for n in WEIGHTS], *[new_v[n] for n in WEIGHTS])
```

```python
import functools
import math

import jax
import jax.numpy as jnp
from jax import lax
from jax.experimental import pallas as pl
from jax.experimental.pallas import tpu as pltpu

F32, BF16 = jnp.float32, jnp.bfloat16
N_DEV = 8
EPS = 1e-6
NEG = -1e30
LANES = 128
HEAD_DIM = 64
CHUNK = 128
CONV_W = 31
CONV_HALO = 16
BAND = 64
DILATIONS = (1, 4, 16)
ROT_DIM = 16
ROPE_THETA = 500000.0
VMEM_LIMIT = 56 * 1024 * 1024
MLP_CHUNK = 1024
MLP_BWD_CHUNK = 512
ADAM_LR, ADAM_B1, ADAM_B2, ADAM_EPS, ADAM_WD, ADAM_STEP = 0.001, 0.9, 0.999, 1e-08, 0.01, 10
MESH = pl.DeviceIdType.MESH


def _params(*sem):
    return pltpu.CompilerParams(dimension_semantics=sem, vmem_limit_bytes=VMEM_LIMIT)


def _dot(a, b):
    return jnp.dot(a, b, preferred_element_type=F32)


def _dot_nt(a, b):
    return lax.dot_general(a, b, (((1,), (1,)), ((), ())), preferred_element_type=F32)


def _dot_tn(a, b):
    return lax.dot_general(a, b, (((0,), (0,)), ((), ())), preferred_element_type=F32)


def _rms_r(x):
    return lax.rsqrt(jnp.mean(x * x, axis=-1, keepdims=True) + EPS)


def _sigmoid(x):
    return 1.0 / (1.0 + jnp.exp(-x))


_GK = math.sqrt(2.0 / math.pi)


def _gelu(x):
    return 0.5 * x * (1.0 + jnp.tanh(_GK * (x + 0.044715 * x * x * x)))


def _gelu_grad(x):
    t = jnp.tanh(_GK * (x + 0.044715 * x * x * x))
    return 0.5 * (1.0 + t) + 0.5 * x * (1.0 - t * t) * (_GK * (1.0 + 3.0 * 0.044715 * x * x))


def _seg_sum(x, bd):
    hi = x.astype(BF16)
    r1 = x - hi.astype(F32)
    mid = r1.astype(BF16)
    lo = (r1 - mid.astype(F32)).astype(BF16)
    return _dot(hi, bd) + _dot(mid, bd) + _dot(lo, bd)


def _block_diag(n):
    i = lax.broadcasted_iota(jnp.int32, (n, n), 0) // HEAD_DIM
    j = lax.broadcasted_iota(jnp.int32, (n, n), 1) // HEAD_DIM
    return jnp.where(i == j, 1.0, 0.0).astype(BF16)


def _tile(s, cap):
    t = min(s, cap)
    assert s % t == 0
    return t


def _my_index():
    return 4 * lax.axis_index("x") + 2 * lax.axis_index("y") + lax.axis_index("c")


def _device(i):
    return (i // 4, (i // 2) % 2, i % 2)


_HBM = pl.BlockSpec(memory_space=pl.ANY)


class PeerCopies:
    def __init__(self, operands, modes):
        self.inputs, self.modes = list(operands), list(modes)
        self.out_shape = []
        for t, (kind, axis) in zip(self.inputs, self.modes):
            shape = list(t.shape)
            if kind == "gather":
                shape = [N_DEV] + shape if axis is None else shape[:axis] + [N_DEV * shape[axis]] + shape[axis + 1:]
            elif axis is not None:
                shape = [N_DEV] + shape[:axis] + [shape[axis] // N_DEV] + shape[axis + 1:]
            self.out_shape.append(jax.ShapeDtypeStruct(tuple(shape), t.dtype))
        n = len(self.inputs)
        self.scratch = [pltpu.SemaphoreType.DMA((n, N_DEV - 1)), pltpu.SemaphoreType.DMA((n, N_DEV - 1)),
                        pltpu.SemaphoreType.DMA((n,))]

    @staticmethod
    def _block(ref, axis, size, j):
        if axis is None:
            return ref.at[j]
        return ref.at[tuple([slice(None)] * axis + [pl.ds(j * size, size)])]

    def _copies(self, in_refs, out_refs, sems, arrivals):
        send_sems, recv_sems, local_sems = sems
        me = _my_index()
        local, sends, recvs = [], [], []
        for t, (src, dst) in enumerate(zip(in_refs, out_refs)):
            kind, axis = self.modes[t]
            if kind == "gather":
                size = None if axis is None else src.shape[axis]
                source = lambda j, src=src: src
                place = lambda j, dst=dst, axis=axis, size=size: self._block(dst, axis, size, j)
            else:
                size = None if axis is None else src.shape[axis] // N_DEV
                source = lambda j, src=src, axis=axis, size=size: self._block(src, axis, size, j)
                place = lambda j, dst=dst: dst.at[j]
            local.append(pltpu.make_async_copy(source(me), place(me), local_sems.at[t]))
            for k in range(N_DEV - 1):
                to, frm = (me + k + 1) % N_DEV, (me + N_DEV - k - 1) % N_DEV
                sends.append(pltpu.make_async_remote_copy(
                    src_ref=source(to), dst_ref=place(me), send_sem=send_sems.at[t, k], recv_sem=recv_sems.at[t, k],
                    device_id=_device(to), device_id_type=MESH))
                if arrivals:
                    recvs.append(pltpu.make_async_remote_copy(
                        src_ref=source(me), dst_ref=place(frm), send_sem=send_sems.at[t, k], recv_sem=recv_sems.at[t, k],
                        device_id=_device(frm), device_id_type=MESH))
        return local, sends, recvs

    def start(self, in_refs, out_refs, sems):
        local, sends, _ = self._copies(in_refs, out_refs, sems, False)
        for cp in local + sends:
            cp.start()

    def finish(self, in_refs, out_refs, sems):
        local, sends, recvs = self._copies(in_refs, out_refs, sems, True)
        for cp in recvs:
            cp.wait_recv()
        for cp in sends:
            cp.wait_send()
        for cp in local:
            cp.wait()


def _pallas(body, *, name, args, in_specs, out_specs, out_shape, grid=(), scratch=(), sem=(), comm=None, aliases=None):
    n_in, n_out, n_scr = len(args), len(out_shape), len(scratch)
    if comm is None:
        outs = pl.pallas_call(
            body, name=name, grid=grid, in_specs=in_specs, out_specs=out_specs, out_shape=out_shape,
            scratch_shapes=list(scratch), input_output_aliases=aliases or {}, compiler_params=_params(*sem))(*args)
        return list(outs), []
    ci, co = len(comm.inputs), len(comm.out_shape)

    def hosted(*refs):
        ins, cins = refs[:n_in], refs[n_in:n_in + ci]
        outs, couts = refs[n_in + ci:n_in + ci + n_out], refs[n_in + ci + n_out:n_in + ci + n_out + co]
        rest = refs[n_in + ci + n_out + co:]
        scr, sems = rest[:n_scr], rest[n_scr:]
        if not grid:
            comm.start(cins, couts, sems)
            comm.finish(cins, couts, sems)
            return
        first = last = None
        for axis, size in enumerate(grid):
            f, l = pl.program_id(axis) == 0, pl.program_id(axis) == size - 1
            first, last = (f, l) if first is None else (first & f, last & l)
        pl.when(first)(lambda: comm.start(cins, couts, sems))
        body(*ins, *outs, *scr)
        pl.when(last)(lambda: comm.finish(cins, couts, sems))

    outs = pl.pallas_call(
        hosted, name=name, grid=grid, in_specs=list(in_specs) + [_HBM] * ci, out_specs=list(out_specs) + [_HBM] * co,
        out_shape=list(out_shape) + comm.out_shape, scratch_shapes=list(scratch) + comm.scratch,
        input_output_aliases=aliases or {}, compiler_params=_params(*["arbitrary"] * len(grid)))(*args, *comm.inputs)
    return list(outs[:n_out]), list(outs[n_out:])


def run_copies(comm, name):
    return _pallas(None, name=name, args=[], in_specs=[], out_specs=[], out_shape=[], comm=comm)[1]


def norm_matmul(x, g, wg, name, comm=None):
    s, d = x.shape
    n = wg.shape[-1]
    ns = 1024 if n % 1024 == 0 else n // 4
    tm = _tile(s, 1024)

    def body(x_ref, g_ref, w_ref, z_ref, h_ref):
        @pl.when(pl.program_id(1) == 0)
        def _():
            xv = x_ref[...]
            h_ref[...] = (xv * _rms_r(xv) * g_ref[...]).astype(BF16)
        z_ref[...] = _dot(h_ref[...], w_ref[...])

    return _pallas(
        body, name=name, grid=(s // tm, n // ns), args=[x, g, wg], comm=comm,
        in_specs=[pl.BlockSpec((tm, d), lambda i, j: (i, 0)),
                  pl.BlockSpec((1, d), lambda i, j: (0, 0)),
                  pl.BlockSpec((d, ns), lambda i, j: (0, j))],
        out_specs=[pl.BlockSpec((tm, ns), lambda i, j: (i, j)),
                   pl.BlockSpec((tm, d), lambda i, j: (i, 0))],
        out_shape=[jax.ShapeDtypeStruct((s, n), F32), jax.ShapeDtypeStruct((s, d), BF16)],
        sem=("parallel", "arbitrary"))


def mlp_forward(x, g, w1g, w2g, name, comm=None):
    s, d = x.shape
    f = w1g.shape[-1]
    fs = MLP_CHUNK
    tm = _tile(s, 1024)

    def body(x_ref, g_ref, w1_ref, w2_ref, xo_ref, a_ref, h_ref):
        @pl.when(pl.program_id(1) == 0)
        def _():
            xv = x_ref[...]
            h_ref[...] = (xv * _rms_r(xv) * g_ref[...]).astype(BF16)
            xo_ref[...] = xv
        a = _dot(h_ref[...], w1_ref[...])
        a_ref[...] = a.astype(BF16)
        r = jnp.maximum(a, 0.0)
        xo_ref[...] += _dot((r * r).astype(BF16), w2_ref[...])

    return _pallas(
        body, name=name, grid=(s // tm, f // fs), args=[x, g, w1g, w2g], comm=comm,
        in_specs=[pl.BlockSpec((tm, d), lambda i, j: (i, 0)),
                  pl.BlockSpec((1, d), lambda i, j: (0, 0)),
                  pl.BlockSpec((d, fs), lambda i, j: (0, j)),
                  pl.BlockSpec((fs, d), lambda i, j: (j, 0))],
        out_specs=[pl.BlockSpec((tm, d), lambda i, j: (i, 0)),
                   pl.BlockSpec((tm, fs), lambda i, j: (i, j)),
                   pl.BlockSpec((tm, d), lambda i, j: (i, 0))],
        out_shape=[jax.ShapeDtypeStruct((s, d), F32), jax.ShapeDtypeStruct((s, f), BF16),
                   jax.ShapeDtypeStruct((s, d), BF16)],
        sem=("parallel", "arbitrary"))


def _norm_backward(dh, xv, g, dres):
    r = _rms_r(xv)
    xh = xv * r
    t = dh * g
    dx = dres + r * (t - xh * jnp.mean(t * xh, axis=-1, keepdims=True))
    return dx, jnp.sum(dh * xh, axis=0, keepdims=True)


def mlp_backward(dy, a, x, g, w1g, w2g, name, comm=None):
    s, d = x.shape
    f = w1g.shape[-1]
    fs = MLP_BWD_CHUNK
    tm = _tile(s, 1024)

    def body(dy_ref, a_ref, x_ref, g_ref, w1_ref, w2_ref, dx_ref, da_ref, dg_ref, dyb_ref, dh_ref):
        i, j = pl.program_id(0), pl.program_id(1)

        @pl.when(j == 0)
        def _():
            dyb_ref[...] = dy_ref[...].astype(BF16)
            dh_ref[...] = jnp.zeros_like(dh_ref)

        dr = _dot_nt(dyb_ref[...], w2_ref[...])
        da = (dr * (2.0 * jnp.maximum(a_ref[...].astype(F32), 0.0))).astype(BF16)
        da_ref[...] = da
        dh_ref[...] += _dot_nt(da, w1_ref[...])

        @pl.when(j == f // fs - 1)
        def _():
            dx, dgp = _norm_backward(dh_ref[...], x_ref[...], g_ref[...], dy_ref[...])
            dx_ref[...] = dx

            @pl.when(i == 0)
            def _():
                dg_ref[...] = dgp

            @pl.when(i > 0)
            def _():
                dg_ref[...] += dgp

    return _pallas(
        body, name=name, grid=(s // tm, f // fs), args=[dy, a, x, g, w1g, w2g], comm=comm,
        in_specs=[pl.BlockSpec((tm, d), lambda i, j: (i, 0)),
                  pl.BlockSpec((tm, fs), lambda i, j: (i, j)),
                  pl.BlockSpec((tm, d), lambda i, j: (i, 0)),
                  pl.BlockSpec((1, d), lambda i, j: (0, 0)),
                  pl.BlockSpec((d, fs), lambda i, j: (0, j)),
                  pl.BlockSpec((fs, d), lambda i, j: (j, 0))],
        out_specs=[pl.BlockSpec((tm, d), lambda i, j: (i, 0)),
                   pl.BlockSpec((tm, fs), lambda i, j: (i, j)),
                   pl.BlockSpec((1, d), lambda i, j: (0, 0)),
                   pl.BlockSpec((tm, d), lambda i, j: (i, 0))],
        out_shape=[jax.ShapeDtypeStruct((s, d), F32), jax.ShapeDtypeStruct((s, f), BF16),
                   jax.ShapeDtypeStruct((1, d), F32), jax.ShapeDtypeStruct((s, d), BF16)],
        scratch=[pltpu.VMEM((tm, d), F32)],
        sem=("arbitrary", "arbitrary"))


def matmul_tn(a, b, name, m_split=1, n_split=1, relu2=False, comm=None):
    s, m = a.shape
    n = b.shape[1]
    ts = _tile(s, 1024)
    bm, bn = m // m_split, n // n_split
    a_map = lambda j, k: (k, j // n_split)
    b_map = lambda j, k: (k, j % n_split)

    def body(a_ref, b_ref, o_ref, acc_ref):
        k = pl.program_id(1)
        av = a_ref[...]
        if relu2:
            af = jnp.maximum(av.astype(F32), 0.0)
            av = af * af
        p = _dot_tn(av.astype(BF16), b_ref[...].astype(BF16))

        @pl.when(k == 0)
        def _():
            acc_ref[...] = p

        @pl.when(k > 0)
        def _():
            acc_ref[...] += p

        @pl.when(k == s // ts - 1)
        def _():
            o_ref[...] = acc_ref[...].astype(BF16)

    outs, couts = _pallas(
        body, name=name, grid=(m_split * n_split, s // ts), args=[a, b], comm=comm,
        in_specs=[pl.BlockSpec((ts, bm), a_map), pl.BlockSpec((ts, bn), b_map)],
        out_specs=[pl.BlockSpec((bm, bn), lambda j, k: (j // n_split, j % n_split))],
        out_shape=[jax.ShapeDtypeStruct((m, n), BF16)],
        scratch=[pltpu.VMEM((bm, bn), F32)],
        sem=("parallel", "arbitrary"))
    return outs[0], couts


def matmul_residual(x, y, w, name):
    s, n = x.shape
    k = y.shape[1]
    tm = _tile(s, 1024)

    def body(x_ref, y_ref, w_ref, o_ref):
        o_ref[...] = x_ref[...] + _dot(y_ref[...], w_ref[...])

    return pl.pallas_call(
        body, name=name, grid=(s // tm,),
        in_specs=[pl.BlockSpec((tm, n), lambda i: (i, 0)),
                  pl.BlockSpec((tm, k), lambda i: (i, 0)),
                  pl.BlockSpec((k, n), lambda i: (0, 0))],
        out_specs=pl.BlockSpec((tm, n), lambda i: (i, 0)),
        out_shape=jax.ShapeDtypeStruct((s, n), F32),
        compiler_params=_params("parallel"),
    )(x, y, w)


def matmul_nt(dy, wg, name):
    s, n = dy.shape
    k = wg.shape[0]
    tm = _tile(s, 1024)

    def body(dy_ref, w_ref, o_ref):
        o_ref[...] = _dot_nt(dy_ref[...].astype(BF16), w_ref[...])

    return pl.pallas_call(
        body, name=name, grid=(s // tm,),
        in_specs=[pl.BlockSpec((tm, n), lambda i: (i, 0)),
                  pl.BlockSpec((k, n), lambda i: (0, 0))],
        out_specs=pl.BlockSpec((tm, k), lambda i: (i, 0)),
        out_shape=jax.ShapeDtypeStruct((s, k), F32),
        compiler_params=_params("parallel"),
    )(dy, wg)


def matmul_nt_norm_backward(dz, wg, x, g, dres, name):
    s, d = x.shape
    n = wg.shape[-1]
    tm = _tile(s, 512)

    def body(dz_ref, w_ref, x_ref, g_ref, dres_ref, dx_ref, dg_ref):
        dx, dgp = _norm_backward(_dot_nt(dz_ref[...], w_ref[...]), x_ref[...], g_ref[...], dres_ref[...])
        dx_ref[...] = dx
        _accumulate(dg_ref, dgp, pl.program_id(0) == 0)

    return pl.pallas_call(
        body, name=name, grid=(s // tm,),
        in_specs=[pl.BlockSpec((tm, n), lambda i: (i, 0)),
                  pl.BlockSpec((d, n), lambda i: (0, 0)),
                  pl.BlockSpec((tm, d), lambda i: (i, 0)),
                  pl.BlockSpec((1, d), lambda i: (0, 0)),
                  pl.BlockSpec((tm, d), lambda i: (i, 0))],
        out_specs=[pl.BlockSpec((tm, d), lambda i: (i, 0)),
                   pl.BlockSpec((1, d), lambda i: (0, 0))],
        out_shape=[jax.ShapeDtypeStruct((s, d), F32), jax.ShapeDtypeStruct((1, d), F32)],
        compiler_params=_params("arbitrary"),
    )(dz, wg, x, g, dres)


def loss_and_grad(y, target, name):
    s, d = y.shape
    tm = _tile(s, 1024)

    def body(y_ref, t_ref, dy_ref, l_ref):
        e = y_ref[...] - t_ref[...]
        dy_ref[...] = e / d
        part = jnp.sum(e * e, axis=0, keepdims=True) * (0.5 / d)

        @pl.when(pl.program_id(0) == 0)
        def _():
            l_ref[...] = part

        @pl.when(pl.program_id(0) > 0)
        def _():
            l_ref[...] += part

    return pl.pallas_call(
        body, name=name, grid=(s // tm,),
        in_specs=[pl.BlockSpec((tm, d), lambda i: (i, 0)), pl.BlockSpec((tm, d), lambda i: (i, 0))],
        out_specs=[pl.BlockSpec((tm, d), lambda i: (i, 0)), pl.BlockSpec((1, d), lambda i: (0, 0))],
        out_shape=[jax.ShapeDtypeStruct((s, d), F32), jax.ShapeDtypeStruct((1, d), F32)],
        compiler_params=_params("arbitrary"),
    )(y, target)


def _layernorm(x, g, b):
    mu = jnp.mean(x, axis=-1, keepdims=True)
    xc = x - mu
    rstd = lax.rsqrt(jnp.mean(xc * xc, axis=-1, keepdims=True) + EPS)
    xn = xc * rstd
    return xn * g + b, xn, rstd


def _layernorm_backward(dy, xn, rstd, g):
    dxn = dy * g
    return rstd * (dxn - jnp.mean(dxn, axis=-1, keepdims=True) - xn * jnp.mean(dxn * xn, axis=-1, keepdims=True))


def _group_halves(x_ref, jp, nch):
    blk = jnp.concatenate([x_ref[c * CHUNK:(c + 1) * CHUNK, jp * LANES:(jp + 1) * LANES] for c in range(nch)], axis=1)
    low = (lax.broadcasted_iota(jnp.int32, blk.shape, 1) % LANES) < HEAD_DIM
    return jnp.where(low, blk, 0.0).astype(BF16), jnp.where(low, 0.0, blk).astype(BF16)


def _spatial_apply(src_ref, w_ref, dst_ref, nch, bias_ref=None):
    for jp in range(4):
        lo, hi = _group_halves(src_ref, jp, nch)
        r = _dot(w_ref[2 * jp], lo) + _dot(w_ref[2 * jp + 1], hi)
        for c in range(nch):
            v = r[:, c * LANES:(c + 1) * LANES]
            if bias_ref is not None:
                v = v + bias_ref[:, jp * LANES:(jp + 1) * LANES]
            dst_ref[c * CHUNK:(c + 1) * CHUNK, jp * LANES:(jp + 1) * LANES] = v


def _glu(zb):
    w = zb.shape[1] // 2
    return zb[:, :w] * _sigmoid(zb[:, w:])


def _fill_padded(pad_ref, prev, cur, nxt, i, nt, tm):
    pad_ref[0:CONV_HALO, :] = jnp.where(i > 0, prev, 0.0)
    pad_ref[CONV_HALO:CONV_HALO + tm, :] = cur
    pad_ref[CONV_HALO + tm:2 * CONV_HALO + tm, :] = jnp.where(i < nt - 1, nxt, 0.0)


def _halo_specs(tm, s, width, col):
    hb, nhb = tm // CONV_HALO, s // CONV_HALO
    return [pl.BlockSpec((tm, width), lambda i: (i, col)),
            pl.BlockSpec((CONV_HALO, width), lambda i: (jnp.maximum(i * hb - 1, 0), col)),
            pl.BlockSpec((CONV_HALO, width), lambda i: (jnp.minimum((i + 1) * hb, nhb - 1), col))]


def _const_spec(shape):
    nd = len(shape)
    return pl.BlockSpec(shape, lambda i: (0,) * nd)


SUBLANES = 8


def _shift_scratch(tm, width):
    return pltpu.VMEM((SUBLANES - 1, tm + 2 * CONV_HALO - SUBLANES, width), F32)


def _fill_shifts(sh_ref, pad_ref):
    rows = sh_ref.shape[1]
    for sft in range(1, SUBLANES):
        sh_ref[sft - 1] = pad_ref[pl.ds(sft, rows), :]


def _tap(pad_ref, sh_ref, offset, cols):
    sft = offset % SUBLANES
    rows = pl.ds(offset - sft, CHUNK)
    return pad_ref[rows, cols] if sft == 0 else sh_ref[sft - 1, rows, cols]


def ab_mid_forward(z, sp, name, comm=None):
    s = z.shape[0]
    aw = z.shape[1] // 4
    tm = _tile(s, 512)
    nch, nt = tm // CHUNK, s // tm

    def body(zu_ref, zv_ref, zb_ref, zp_ref, zn_ref, w_ref, bias_ref, vg_ref, vb_ref, cw_ref, cb_ref, cg_ref, cbn_ref,
             y_ref, gc_ref, vl_ref, sv_ref, pad_ref, sh_ref):
        i = pl.program_id(0)
        vl_ref[...] = _layernorm(_gelu(zv_ref[...]), vg_ref[...], vb_ref[...])[0]
        _spatial_apply(vl_ref, w_ref, sv_ref, nch, bias_ref)
        y_ref[:, :aw] = (_gelu(zu_ref[...]) * sv_ref[...]).astype(BF16)

        _fill_padded(pad_ref, _glu(zp_ref[...]), _glu(zb_ref[...]), _glu(zn_ref[...]), i, nt, tm)
        _fill_shifts(sh_ref, pad_ref)
        for rb in range(tm // CHUNK):
            for lb in range(aw // LANES):
                cols = pl.ds(lb * LANES, LANES)
                acc = jnp.broadcast_to(cb_ref[:, cols], (CHUNK, LANES))
                for k in range(CONV_W):
                    acc = acc + cw_ref[k:k + 1, cols] * _tap(pad_ref, sh_ref, rb * CHUNK + CONV_HALO - CONV_W // 2 + k, cols)
                gc_ref[rb * CHUNK:(rb + 1) * CHUNK, cols] = acc
        yl = _layernorm(gc_ref[...], cg_ref[...], cbn_ref[...])[0]
        y_ref[:, aw:] = (yl * _sigmoid(yl)).astype(BF16)

    return _pallas(
        body, name=name, grid=(nt,), comm=comm,
        args=[z, z, z, z, z, sp["w"], sp["bias"], sp["vg"], sp["vb"], sp["cw"], sp["cb"], sp["cg"], sp["cbn"]],
        in_specs=[pl.BlockSpec((tm, aw), lambda i: (i, 0)), pl.BlockSpec((tm, aw), lambda i: (i, 1))]
        + _halo_specs(tm, s, 2 * aw, 1)
        + [_const_spec(sp["w"].shape), _const_spec(sp["bias"].shape)]
        + [_const_spec((1, aw))] * 2 + [_const_spec(sp["cw"].shape)] + [_const_spec((1, aw))] * 3,
        out_specs=[pl.BlockSpec((tm, 2 * aw), lambda i: (i, 0)), pl.BlockSpec((tm, aw), lambda i: (i, 0))],
        out_shape=[jax.ShapeDtypeStruct((s, 2 * aw), BF16), jax.ShapeDtypeStruct((s, aw), F32)],
        scratch=[pltpu.VMEM((tm, aw), F32), pltpu.VMEM((tm, aw), F32), pltpu.VMEM((tm + 2 * CONV_HALO, aw), F32),
                 _shift_scratch(tm, aw)],
        sem=("parallel",))


def _accumulate(ref, val, first):
    @pl.when(first)
    def _():
        ref[...] = val

    @pl.when(jnp.logical_not(first))
    def _():
        ref[...] += val


def ab_mid_backward(dy, z, gc, sp, name):
    s = z.shape[0]
    aw = z.shape[1] // 4
    tm = _tile(s, 512)
    nch, nt = tm // CHUNK, s // tm

    def body(dya_ref, dyb_ref, zu_ref, zv_ref, gc_ref, w_ref, wt_ref, bias_ref, vg_ref, vb_ref, cg_ref, cbn_ref,
             dz_ref, dgc_ref, dw_ref, dsb_ref, dvg_ref, dvb_ref, dcg_ref, dcbn_ref, dcb_ref,
             vl_ref, sv_ref, dsv_ref, dvl_ref):
        first = pl.program_id(0) == 0
        zu, zv = zu_ref[...], zv_ref[...]
        u = _gelu(zu)
        vl, vn, vrstd = _layernorm(_gelu(zv), vg_ref[...], vb_ref[...])
        vl_ref[...] = vl
        _spatial_apply(vl_ref, w_ref, sv_ref, nch, bias_ref)
        dya = dya_ref[...]
        dz_ref[:, :aw] = (dya * sv_ref[...] * _gelu_grad(zu)).astype(BF16)
        dsv = dya * u
        dsv_ref[...] = dsv
        _spatial_apply(dsv_ref, wt_ref, dvl_ref, nch)

        for jp in range(4):
            dlo, dhi = _group_halves(dsv_ref, jp, nch)
            vlo, vhi = _group_halves(vl_ref, jp, nch)
            vall = vlo + vhi
            _accumulate(dw_ref.at[2 * jp], _dot_nt(dlo, vall), first)
            _accumulate(dw_ref.at[2 * jp + 1], _dot_nt(dhi, vall), first)
        rows = dsv[0:CHUNK]
        for c in range(1, nch):
            rows = rows + dsv[c * CHUNK:(c + 1) * CHUNK]
        grp = lax.broadcasted_iota(jnp.int32, (8, aw), 0) == lax.broadcasted_iota(jnp.int32, (8, aw), 1) // HEAD_DIM
        e = jnp.where(grp, 1.0, 0.0).astype(BF16)
        hi = rows.astype(BF16)
        r1 = rows - hi.astype(F32)
        mid = r1.astype(BF16)
        lo = (r1 - mid.astype(F32)).astype(BF16)
        _accumulate(dsb_ref, _dot_nt(e, hi) + _dot_nt(e, mid) + _dot_nt(e, lo), first)

        dvl = dvl_ref[...]
        _accumulate(dvg_ref, jnp.sum(dvl * vn, axis=0, keepdims=True), first)
        _accumulate(dvb_ref, jnp.sum(dvl, axis=0, keepdims=True), first)
        dz_ref[:, aw:] = (_layernorm_backward(dvl, vn, vrstd, vg_ref[...]) * _gelu_grad(zv)).astype(BF16)

        yl, yn, yrstd = _layernorm(gc_ref[...], cg_ref[...], cbn_ref[...])
        sg = _sigmoid(yl)
        dyl = dyb_ref[...] * (sg + yl * sg * (1.0 - sg))
        _accumulate(dcg_ref, jnp.sum(dyl * yn, axis=0, keepdims=True), first)
        _accumulate(dcbn_ref, jnp.sum(dyl, axis=0, keepdims=True), first)
        dgc = _layernorm_backward(dyl, yn, yrstd, cg_ref[...])
        dgc_ref[...] = dgc
        _accumulate(dcb_ref, jnp.sum(dgc, axis=0, keepdims=True), first)

    vec = jax.ShapeDtypeStruct((1, aw), F32)
    return pl.pallas_call(
        body, name=name, grid=(nt,),
        in_specs=[pl.BlockSpec((tm, aw), lambda i: (i, 0)), pl.BlockSpec((tm, aw), lambda i: (i, 1)),
                  pl.BlockSpec((tm, aw), lambda i: (i, 0)), pl.BlockSpec((tm, aw), lambda i: (i, 1)),
                  pl.BlockSpec((tm, aw), lambda i: (i, 0)),
                  _const_spec(sp["w"].shape), _const_spec(sp["w"].shape), _const_spec(sp["bias"].shape)]
        + [_const_spec((1, aw))] * 4,
        out_specs=[pl.BlockSpec((tm, 2 * aw), lambda i: (i, 0)), pl.BlockSpec((tm, aw), lambda i: (i, 0)),
                   _const_spec(sp["w"].shape), _const_spec((8, CHUNK))] + [_const_spec((1, aw))] * 5,
        out_shape=[jax.ShapeDtypeStruct((s, 4 * aw), BF16), jax.ShapeDtypeStruct((s, aw), F32),
                   jax.ShapeDtypeStruct(sp["w"].shape, F32), jax.ShapeDtypeStruct((8, CHUNK), F32)] + [vec] * 5,
        scratch_shapes=[pltpu.VMEM((tm, aw), F32)] * 4,
        compiler_params=_params("arbitrary"),
    )(dy, dy, z, z, gc, sp["w"], sp["wt"], sp["bias"], sp["vg"], sp["vb"], sp["cg"], sp["cbn"])


def conv_backward(dgc, z, dz_in, sp, name, comm=None):
    s = z.shape[0]
    aw = z.shape[1] // 4
    tm = _tile(s, 512)
    nt = s // tm
    off = CONV_HALO - CONV_W // 2

    def body(d_ref, dp_ref, dn_ref, zb_ref, zp_ref, zn_ref, cw_ref, dzin_ref, dz_ref, dcw_ref,
             padd_ref, padg_ref, dgg_ref, shd_ref, shg_ref):
        i = pl.program_id(0)
        _fill_padded(padd_ref, dp_ref[...], d_ref[...], dn_ref[...], i, nt, tm)
        _fill_padded(padg_ref, _glu(zp_ref[...]), _glu(zb_ref[...]), _glu(zn_ref[...]), i, nt, tm)
        _fill_shifts(shd_ref, padd_ref)
        _fill_shifts(shg_ref, padg_ref)

        @pl.when(i == 0)
        def _():
            dcw_ref[...] = jnp.zeros_like(dcw_ref)

        def grad_input(cols, rb):
            acc = jnp.zeros((CHUNK, LANES), F32)
            for k in range(CONV_W):
                acc = acc + cw_ref[k:k + 1, cols] * _tap(padd_ref, shd_ref, rb * CHUNK + CONV_HALO + CONV_W // 2 - k, cols)
            dgg_ref[rb * CHUNK:(rb + 1) * CHUNK, cols] = acc

        def grad_taps(cols, rb):
            dblk = d_ref[rb * CHUNK:(rb + 1) * CHUNK, cols]
            for k in range(CONV_W):
                prod = dblk * _tap(padg_ref, shg_ref, rb * CHUNK + off + k, cols)
                dcw_ref[k:k + 1, cols] += jnp.sum(prod, axis=0, keepdims=True)

        for lb in range(aw // LANES):
            for rb in range(tm // CHUNK):
                pl.when(i >= 0)(functools.partial(grad_input, pl.ds(lb * LANES, LANES), rb))
                pl.when(i >= 0)(functools.partial(grad_taps, pl.ds(lb * LANES, LANES), rb))

        zb = zb_ref[...]
        val, sg = zb[:, :aw], _sigmoid(zb[:, aw:])
        dgg = dgg_ref[...]
        dz_ref[:, :aw] = (dgg * sg).astype(BF16)
        dz_ref[:, aw:] = (dgg * val * sg * (1.0 - sg)).astype(BF16)

    return _pallas(
        body, name=name, grid=(nt,), args=[dgc, dgc, dgc, z, z, z, sp["cw"], dz_in], comm=comm,
        in_specs=_halo_specs(tm, s, aw, 0) + _halo_specs(tm, s, 2 * aw, 1)
        + [_const_spec(sp["cw"].shape), pl.BlockSpec(memory_space=pl.ANY)],
        out_specs=[pl.BlockSpec((tm, 2 * aw), lambda i: (i, 1)), _const_spec(sp["cw"].shape)],
        out_shape=[jax.ShapeDtypeStruct((s, 4 * aw), BF16), jax.ShapeDtypeStruct(sp["cw"].shape, F32)],
        scratch=[pltpu.VMEM((tm + 2 * CONV_HALO, aw), F32)] * 2 + [pltpu.VMEM((tm, aw), F32)]
        + [_shift_scratch(tm, aw)] * 2,
        aliases={7: 0}, sem=("arbitrary",))


def rope_tables(s):
    pos = jnp.arange(s, dtype=F32)
    inv_freq = ROPE_THETA ** (-jnp.arange(0, ROT_DIM, 2, dtype=F32) / ROT_DIM)
    ang = pos[:, None] * inv_freq[None, :]
    cos, sin = jnp.cos(ang), jnp.sin(ang)
    half = ROT_DIM // 2
    rest = HEAD_DIM - ROT_DIM
    one, zero, zrest = jnp.ones((s, rest), F32), jnp.zeros((s, half), F32), jnp.zeros((s, rest), F32)
    c = jnp.concatenate([cos, cos, one], axis=1)
    s1 = jnp.concatenate([-sin, zero, zrest], axis=1)
    s2 = jnp.concatenate([zero, sin, zrest], axis=1)
    return tuple(jnp.tile(t, (1, LANES // HEAD_DIM)) for t in (c, s1, s2))


def qk_prep_forward(qkv, tabs, gains, name, comm=None):
    s, w3 = qkv.shape
    w = w3 // 3
    tm = _tile(s, 512)

    def body(x_ref, c_ref, s1_ref, s2_ref, g_ref, o_ref):
        bd = _block_diag(LANES)
        for rb in range(tm // CHUNK):
            rows = pl.ds(rb * CHUNK, CHUNK)
            c, s1, s2 = c_ref[rows, :], s1_ref[rows, :], s2_ref[rows, :]
            for b in range(w // LANES):
                cols = pl.ds(b * LANES, LANES)
                t = x_ref[rows, cols]
                r = lax.rsqrt(_seg_sum(t * t, bd) * (1.0 / HEAD_DIM) + EPS)
                y = t * r * g_ref[...]
                o_ref[rows, cols] = (y * c + pltpu.roll(y, LANES - ROT_DIM // 2, 1) * s1
                                     + pltpu.roll(y, ROT_DIM // 2, 1) * s2)

    tab = pl.BlockSpec((tm, LANES), lambda i, p: (i, 0))
    outs, couts = _pallas(
        body, name=name, grid=(s // tm, 2), args=[qkv, *tabs, gains], comm=comm,
        in_specs=[pl.BlockSpec((tm, w), lambda i, p: (i, p)), tab, tab, tab,
                  pl.BlockSpec((None, 1, LANES), lambda i, p: (p, 0, 0))],
        out_specs=[pl.BlockSpec((tm, w), lambda i, p: (i, p))],
        out_shape=[jax.ShapeDtypeStruct((s, 2 * w), F32)],
        sem=("parallel", "arbitrary"))
    return outs[0], couts


def qk_prep_backward(dq, dk, dv, qkv, tabs, gains, name):
    s, w3 = qkv.shape
    w = w3 // 3
    tm = _tile(s, 512)

    def body(*refs):
        grads = ((refs[0],), (refs[1],), (refs[2],))
        x_ref, c_ref, s1_ref, s2_ref, g_ref, o_ref, dg_ref = refs[3:]
        part, first = pl.program_id(0), pl.program_id(1) == 0

        def normed(ds):
            bd = _block_diag(LANES)
            acc = jnp.zeros((1, LANES), F32)
            for rb in range(tm // CHUNK):
                rows = pl.ds(rb * CHUNK, CHUNK)
                c, s1, s2 = c_ref[rows, :], s1_ref[rows, :], s2_ref[rows, :]
                for b in range(w // LANES):
                    cols = pl.ds(b * LANES, LANES)
                    dout = ds[0][rows, cols]
                    dy = (dout * c + pltpu.roll(dout * s1, ROT_DIM // 2, 1)
                          + pltpu.roll(dout * s2, LANES - ROT_DIM // 2, 1))
                    t = x_ref[rows, cols]
                    r = lax.rsqrt(_seg_sum(t * t, bd) * (1.0 / HEAD_DIM) + EPS)
                    xh = t * r
                    acc = acc + jnp.sum(dy * xh, axis=0, keepdims=True)
                    tt = dy * g_ref[...]
                    o_ref[rows, cols] = (r * (tt - xh * (_seg_sum(tt * xh, bd) * (1.0 / HEAD_DIM)))).astype(BF16)
            _accumulate(dg_ref, acc, first)

        for p in range(2):
            pl.when(part == p)(functools.partial(normed, grads[p]))

        @pl.when(part == 2)
        def _():
            o_ref[...] = grads[2][0][...].astype(BF16)
            _accumulate(dg_ref, jnp.zeros((1, LANES), F32), first)

    def gspec(p):
        return pl.BlockSpec((tm, w), lambda q, i: (jnp.where(q == p, i, 0), 0))

    tab = pl.BlockSpec((tm, LANES), lambda q, i: (i, 0))
    return pl.pallas_call(
        body, name=name, grid=(3, s // tm),
        in_specs=[gspec(0), gspec(1), gspec(2)]
        + [pl.BlockSpec((tm, w), lambda q, i: (i, q)), tab, tab, tab,
           pl.BlockSpec((None, 1, LANES), lambda q, i: (q, 0, 0))],
        out_specs=[pl.BlockSpec((tm, w), lambda q, i: (i, q)), pl.BlockSpec((None, 1, LANES), lambda q, i: (q, 0, 0))],
        out_shape=[jax.ShapeDtypeStruct((s, w3), BF16), jax.ShapeDtypeStruct((3, 1, LANES), F32)],
        compiler_params=_params("arbitrary", "arbitrary"),
    )(dq, dk, dv, qkv, *tabs, gains)


def _window_specs(tq, l, col_fn):
    hb, nhb = tq // BAND, l // BAND
    return [pl.BlockSpec((tq, LANES), lambda c, i: (i, col_fn(c))),
            pl.BlockSpec((BAND, LANES), lambda c, i: (jnp.maximum(i * hb - 1, 0), col_fn(c))),
            pl.BlockSpec((BAND, LANES), lambda c, i: (jnp.minimum((i + 1) * hb, nhb - 1), col_fn(c)))]


def _window(cur_ref, prev_ref, next_ref):
    return jnp.concatenate([prev_ref[...], cur_ref[...], next_ref[...]], axis=0)


def _band_mask(shape, centre_axis, first_row, length):
    ctr = lax.broadcasted_iota(jnp.int32, shape, centre_axis)
    win = lax.broadcasted_iota(jnp.int32, shape, 1 - centre_axis)
    row = first_row + win
    return (jnp.abs(win - BAND - ctr) <= BAND) & (row >= 0) & (row < length)


def _col_q(d):
    return lambda c: (c // 8) * 24 + c % 8


def _col_k(d):
    return lambda c: (c // 8) * 24 + 8 + c % 8


def _col_v(d):
    return lambda c: (c // 8) * 24 + 16 + c % 8


def band_attention_forward(qkvn, d, name):
    s, w3 = qkvn.shape
    w = w3 // 3
    l = s // d
    tq = _tile(l, 512)
    scale = HEAD_DIM ** -0.5
    xv = qkvn.reshape(l, d * w3)

    def body(q_ref, k_ref, kp_ref, kn_ref, v_ref, vp_ref, vn_ref, o_ref, lse_ref):
        i = pl.program_id(1)
        kw, vw = _window(k_ref, kp_ref, kn_ref), _window(v_ref, vp_ref, vn_ref)
        head0 = lax.broadcasted_iota(jnp.int32, (CHUNK, LANES), 1) < HEAD_DIM
        for b in range(tq // CHUNK):
            rows = pl.ds(b * CHUNK, CHUNK)
            mask = _band_mask((CHUNK, 2 * CHUNK), 0, i * tq + b * CHUNK - BAND, l)
            qb = q_ref[rows, :]
            kb, vb = kw[b * CHUNK:(b + 2) * CHUNK], vw[b * CHUNK:(b + 2) * CHUNK]
            outs, lses = [], []
            for hm in (head0, jnp.logical_not(head0)):
                sc = jnp.where(mask, _dot_nt(jnp.where(hm, qb, jnp.zeros_like(qb)), kb) * scale, NEG)
                m = jnp.max(sc, axis=1, keepdims=True)
                p = jnp.exp(sc - m)
                den = jnp.sum(p, axis=1, keepdims=True)
                outs.append(_dot(p.astype(BF16), vb) / den)
                lses.append(jnp.broadcast_to(m + jnp.log(den), (CHUNK, LANES)))
            o_ref[rows, :] = jnp.where(head0, outs[0], outs[1]).astype(BF16)
            lse_ref[rows, :] = jnp.where(head0, lses[0], lses[1])

    ospec = pl.BlockSpec((tq, LANES), lambda c, i: (i, c))
    o, lse = pl.pallas_call(
        body, name=name, grid=(d * w // LANES, l // tq),
        in_specs=[pl.BlockSpec((tq, LANES), lambda c, i: (i, _col_q(d)(c)))]
        + _window_specs(tq, l, _col_k(d)) + _window_specs(tq, l, _col_v(d)),
        out_specs=[ospec, ospec],
        out_shape=[jax.ShapeDtypeStruct((l, d * w), BF16), jax.ShapeDtypeStruct((l, d * w), F32)],
        compiler_params=_params("parallel", "parallel"),
    )(xv, xv, xv, xv, xv, xv, xv)
    return o.reshape(s, w), lse.reshape(s, w)


def attention_merge(os_, lses, name):
    s, w = os_[0].shape
    tm = _tile(s, 512)

    def body(o0, o1, o2, l0, l1, l2, o_ref, lse_ref):
        la, lb, lc = l0[...], l1[...], l2[...]
        m = jnp.maximum(jnp.maximum(la, lb), lc)
        wa, wb, wc = jnp.exp(la - m), jnp.exp(lb - m), jnp.exp(lc - m)
        den = wa + wb + wc
        o = (wa * o0[...].astype(F32) + wb * o1[...].astype(F32) + wc * o2[...].astype(F32)) / den
        o_ref[...] = o.astype(BF16)
        lse_ref[...] = m + jnp.log(den)

    spec = pl.BlockSpec((tm, w), lambda i: (i, 0))
    return pl.pallas_call(
        body, name=name, grid=(s // tm,), in_specs=[spec] * 6, out_specs=[spec, spec],
        out_shape=[jax.ShapeDtypeStruct((s, w), BF16), jax.ShapeDtypeStruct((s, w), F32)],
        compiler_params=_params("parallel"),
    )(*os_, *lses)


def attention_delta(do, o, name):
    s, w = do.shape
    tm = _tile(s, 512)

    def body(do_ref, o_ref, dl_ref, dob_ref):
        bd = _block_diag(LANES)
        for b in range(w // LANES):
            cols = pl.ds(b * LANES, LANES)
            dv = do_ref[:, cols]
            dl_ref[:, cols] = _seg_sum(dv * o_ref[:, cols].astype(F32), bd)
            dob_ref[:, cols] = dv.astype(BF16)

    spec = pl.BlockSpec((tm, w), lambda i: (i, 0))
    return pl.pallas_call(
        body, name=name, grid=(s // tm,), in_specs=[spec, spec], out_specs=[spec, spec],
        out_shape=[jax.ShapeDtypeStruct((s, w), F32), jax.ShapeDtypeStruct((s, w), BF16)],
        compiler_params=_params("parallel"),
    )(do, o)


def band_attention_dq(qkvn, dob, lse, delta, d, name):
    s, w3 = qkvn.shape
    w = w3 // 3
    l = s // d
    tq = _tile(l, 512)
    scale = HEAD_DIM ** -0.5
    xv = qkvn.reshape(l, d * w3)

    def body(q_ref, k_ref, kp_ref, kn_ref, v_ref, vp_ref, vn_ref, do_ref, lse_ref, dl_ref, dq_ref):
        i = pl.program_id(1)
        kw, vw = _window(k_ref, kp_ref, kn_ref), _window(v_ref, vp_ref, vn_ref)
        head0 = lax.broadcasted_iota(jnp.int32, (CHUNK, LANES), 1) < HEAD_DIM
        for b in range(tq // CHUNK):
            rows = pl.ds(b * CHUNK, CHUNK)
            mask = _band_mask((CHUNK, 2 * CHUNK), 0, i * tq + b * CHUNK - BAND, l)
            qb, dob_ = q_ref[rows, :], do_ref[rows, :]
            kb, vb = kw[b * CHUNK:(b + 2) * CHUNK], vw[b * CHUNK:(b + 2) * CHUNK]
            outs = []
            for h, hm in enumerate((head0, jnp.logical_not(head0))):
                col = pl.ds(h * HEAD_DIM, 1)
                sc = jnp.where(mask, _dot_nt(jnp.where(hm, qb, jnp.zeros_like(qb)), kb) * scale, NEG)
                p = jnp.exp(sc - lse_ref[rows, col])
                dp = _dot_nt(jnp.where(hm, dob_, jnp.zeros_like(dob_)), vb)
                ds = p * (dp - dl_ref[rows, col]) * scale
                outs.append(_dot(ds.astype(BF16), kb))
            dq_ref[rows, :] = jnp.where(head0, outs[0], outs[1])

    ospec = pl.BlockSpec((tq, LANES), lambda c, i: (i, c))
    dq = pl.pallas_call(
        body, name=name, grid=(d * w // LANES, l // tq),
        in_specs=[pl.BlockSpec((tq, LANES), lambda c, i: (i, _col_q(d)(c)))]
        + _window_specs(tq, l, _col_k(d)) + _window_specs(tq, l, _col_v(d)) + [ospec, ospec, ospec],
        out_specs=ospec,
        out_shape=jax.ShapeDtypeStruct((l, d * w), F32),
        compiler_params=_params("parallel", "parallel"),
    )(xv, xv, xv, xv, xv, xv, xv, dob.reshape(l, d * w), lse.reshape(l, d * w), delta.reshape(l, d * w))
    return dq.reshape(s, w)


def band_attention_dkv(qkvn, dob, lse, delta, d, name):
    s, w3 = qkvn.shape
    w = w3 // 3
    l = s // d
    tq = _tile(l, 512)
    scale = HEAD_DIM ** -0.5
    xv = qkvn.reshape(l, d * w3)

    def body(k_ref, v_ref, q_ref, qp_ref, qn_ref, do_ref, dop_ref, don_ref, lse_ref, lsep_ref, lsen_ref,
             dl_ref, dlp_ref, dln_ref, dk_ref, dv_ref):
        i = pl.program_id(1)
        qw, dow = _window(q_ref, qp_ref, qn_ref), _window(do_ref, dop_ref, don_ref)
        lsew, dlw = _window(lse_ref, lsep_ref, lsen_ref), _window(dl_ref, dlp_ref, dln_ref)
        head0 = lax.broadcasted_iota(jnp.int32, (2 * CHUNK, LANES), 1) < HEAD_DIM
        for b in range(tq // CHUNK):
            rows = pl.ds(b * CHUNK, CHUNK)
            mask = _band_mask((2 * CHUNK, CHUNK), 1, i * tq + b * CHUNK - BAND, l)
            kb, vb = k_ref[rows, :], v_ref[rows, :]
            win = slice(b * CHUNK, (b + 2) * CHUNK)
            qb, dob_, lseb, dlb = qw[win], dow[win], lsew[win], dlw[win]
            dk = jnp.zeros((CHUNK, LANES), F32)
            dv = jnp.zeros((CHUNK, LANES), F32)
            for h, hm in enumerate((head0, jnp.logical_not(head0))):
                col = slice(h * HEAD_DIM, h * HEAD_DIM + 1)
                qm = jnp.where(hm, qb, jnp.zeros_like(qb))
                dom = jnp.where(hm, dob_, jnp.zeros_like(dob_))
                sc = jnp.where(mask, _dot_nt(qm, kb) * scale, NEG)
                p = jnp.exp(sc - lseb[:, col])
                ds = p * (_dot_nt(dom, vb) - dlb[:, col]) * scale
                dv = dv + _dot_tn(p.astype(BF16), dom)
                dk = dk + _dot_tn(ds.astype(BF16), qm)
            dk_ref[rows, :] = dk
            dv_ref[rows, :] = dv

    ident = lambda c: c
    ospec = pl.BlockSpec((tq, LANES), lambda c, i: (i, c))
    dk, dv = pl.pallas_call(
        body, name=name, grid=(d * w // LANES, l // tq),
        in_specs=[pl.BlockSpec((tq, LANES), lambda c, i: (i, _col_k(d)(c))),
                  pl.BlockSpec((tq, LANES), lambda c, i: (i, _col_v(d)(c)))]
        + _window_specs(tq, l, _col_q(d)) + _window_specs(tq, l, ident) * 3,
        out_specs=[ospec, ospec],
        out_shape=[jax.ShapeDtypeStruct((l, d * w), F32)] * 2,
        compiler_params=_params("parallel", "parallel"),
    )(xv, xv, xv, xv, xv, *[dob.reshape(l, d * w)] * 3, *[lse.reshape(l, d * w)] * 3, *[delta.reshape(l, d * w)] * 3)
    return dk.reshape(s, w), dv.reshape(s, w)


ATT_TILE = 2048
ATT_HALO = BAND * max(DILATIONS)
ROWS_PER_COPY = 256


def _att_specs(s, t, col_fn, halo=True):
    hb, nhb = t // ATT_HALO, s // ATT_HALO
    specs = [pl.BlockSpec((t, LANES), lambda hp, i: (i, col_fn(hp)))]
    if halo:
        specs += [pl.BlockSpec((ATT_HALO, LANES), lambda hp, i: (jnp.maximum(i * hb - 1, 0), col_fn(hp))),
                  pl.BlockSpec((ATT_HALO, LANES), lambda hp, i: (jnp.minimum((i + 1) * hb, nhb - 1), col_fn(hp)))]
    return specs


def _gather_rows(dst_ref, dst_row, src_ref, start, count, stride, scale=None):
    for c in range(0, count, ROWS_PER_COPY):
        m = min(ROWS_PER_COPY, count - c)
        v = src_ref[pl.ds(start + c * stride, m, stride=stride), :]
        if scale is not None:
            v = v * scale
        dst_ref[dst_row + c:dst_row + c + m, :] = v.astype(dst_ref.dtype)


def _stage(dst_ref, cur_ref, d, t, scale=None):
    n = t // d
    for r in range(d):
        _gather_rows(dst_ref, r * n, cur_ref, r, n, d, scale)


def _stage_window(dst_ref, refs, d, t, scale=None, edges=None):
    cur_ref, prev_ref, next_ref = refs
    n = t // d
    nw = n + 2 * BAND
    for r in range(d):
        _gather_rows(dst_ref, r * nw, prev_ref, ATT_HALO - BAND * d + r, BAND, d, scale)
        _gather_rows(dst_ref, r * nw + BAND, cur_ref, r, n, d, scale)
        _gather_rows(dst_ref, r * nw + BAND + n, next_ref, r, BAND, d, scale)
    if edges is not None:
        first, last, value = edges
        fill = jnp.full((BAND, LANES), value, dst_ref.dtype)

        @pl.when(first)
        def _():
            for r in range(d):
                dst_ref[r * nw:r * nw + BAND, :] = fill

        @pl.when(last)
        def _():
            for r in range(d):
                dst_ref[r * nw + BAND + n:(r + 1) * nw, :] = fill


def _band_bias(rows, cols, centre_axis):
    shape = (rows // 2, cols)
    ctr = lax.broadcasted_iota(jnp.int32, shape, centre_axis)
    win = lax.broadcasted_iota(jnp.int32, shape, 1 - centre_axis)
    bias = jnp.where(jnp.abs(win - BAND - ctr) <= BAND, 0.0, NEG).astype(F32)
    return jnp.concatenate([bias, bias], axis=0)


def _window_bias(first_row, length):
    row = first_row + lax.broadcasted_iota(jnp.int32, (1, 2 * CHUNK), 1)
    return jnp.where((row >= 0) & (row < length), 0.0, NEG).astype(F32)


UNITS_PER_TRIP = 8


def _scatter_rows(dst_ref, src_ref, d, t, combine):
    n = t // d
    for r in range(d):
        for c in range(0, n, ROWS_PER_COPY):
            m = min(ROWS_PER_COPY, n - c)
            idx = pl.ds(r + c * d, m, stride=d)
            combine(idx, slice(r * n + c, r * n + c + m))


def _unit_rows(u, n):
    upr = n // CHUNK
    r = u // upr
    b = u - r * upr
    return pl.multiple_of(u * CHUNK, CHUNK), pl.multiple_of((u + r) * CHUNK, CHUNK), b * CHUNK - BAND


def _two_heads(x):
    head0 = lax.broadcasted_iota(jnp.int32, x.shape, 1) < HEAD_DIM
    zero = jnp.zeros_like(x)
    return jnp.concatenate([jnp.where(head0, x, zero), jnp.where(head0, zero, x)], axis=0)


def _head_columns(x):
    return jnp.concatenate([x[:, 0:1], x[:, HEAD_DIM:HEAD_DIM + 1]], axis=0)


def _merge_heads(x2):
    rows = x2.shape[0] // 2
    head0 = lax.broadcasted_iota(jnp.int32, (rows, LANES), 1) < HEAD_DIM
    return jnp.where(head0, jnp.broadcast_to(x2[:rows], (rows, LANES)), jnp.broadcast_to(x2[rows:], (rows, LANES)))


def _col(part):
    return lambda hp: part * 8 + hp


def attention_forward(qk, qkv, name, comm=None):
    s, w2 = qk.shape
    w = w2 // 2
    t = _tile(s, ATT_TILE)
    scale = HEAD_DIM ** -0.5

    def body(q_ref, k_ref, kp_ref, kn_ref, v_ref, vp_ref, vn_ref, o_ref, lse_ref, qs_ref, ks_ref, vs_ref, os_ref, ls_ref, or_ref):
        i = pl.program_id(1)
        band = _band_bias(2 * CHUNK, 2 * CHUNK, 0)
        for pi, d in enumerate(DILATIONS):
            n = t // d
            _stage(qs_ref, q_ref, d, t, scale)
            _stage_window(ks_ref, (k_ref, kp_ref, kn_ref), d, t)
            _stage_window(vs_ref, (v_ref, vp_ref, vn_ref), d, t)

            def unit(u, carry, n=n, d=d):
                qrow, wrow, first = _unit_rows(u, n)
                kb, vb = ks_ref[pl.ds(wrow, 2 * CHUNK), :], vs_ref[pl.ds(wrow, 2 * CHUNK), :]
                sc = _dot_nt(_two_heads(qs_ref[pl.ds(qrow, CHUNK), :]), kb) + band + _window_bias(i * n + first, s // d)
                m = jnp.max(sc, axis=1, keepdims=True)
                p = jnp.exp(sc - m)
                den = jnp.sum(p, axis=1, keepdims=True)
                os_ref[pl.ds(qrow, CHUNK), :] = _merge_heads(_dot(p.astype(BF16), vb) / den)
                ls_ref[pl.ds(qrow, CHUNK), :] = _merge_heads(m + jnp.log(den))
                return carry

            lax.fori_loop(0, t // CHUNK, unit, 0, unroll=UNITS_PER_TRIP)

            if pi == 0:
                def assign(idx, rows):
                    or_ref[idx, :] = os_ref[rows, :]
                    lse_ref[idx, :] = ls_ref[rows, :]
                _scatter_rows(None, None, d, t, assign)
            else:
                def merge(idx, rows):
                    la, lb = lse_ref[idx, :], ls_ref[rows, :]
                    mx = jnp.maximum(la, lb)
                    wa, wb = jnp.exp(la - mx), jnp.exp(lb - mx)
                    den = wa + wb
                    or_ref[idx, :] = (wa * or_ref[idx, :] + wb * os_ref[rows, :]) / den
                    lse_ref[idx, :] = mx + jnp.log(den)
                _scatter_rows(None, None, d, t, merge)
        o_ref[...] = or_ref[...].astype(BF16)

    ospec = pl.BlockSpec((t, LANES), lambda hp, i: (i, hp))
    win_rows = t + 2 * ATT_HALO
    return _pallas(
        body, name=name, grid=(w // LANES, s // t), args=[qk, qk, qk, qk, qkv, qkv, qkv], comm=comm,
        in_specs=_att_specs(s, t, _col(0), halo=False) + _att_specs(s, t, _col(1)) + _att_specs(s, t, _col(2)),
        out_specs=[ospec, ospec],
        out_shape=[jax.ShapeDtypeStruct((s, w), BF16), jax.ShapeDtypeStruct((s, w), F32)],
        scratch=[pltpu.VMEM((t, LANES), BF16), pltpu.VMEM((win_rows, LANES), BF16), pltpu.VMEM((win_rows, LANES), BF16),
                 pltpu.VMEM((t, LANES), F32), pltpu.VMEM((t, LANES), F32), pltpu.VMEM((t, LANES), F32)],
        sem=("parallel", "parallel"))


def attention_delta(do, o, lse, name):
    s, w = do.shape
    tm = _tile(s, 512)

    def body(do_ref, o_ref, lse_ref, dl_ref, st_ref):
        bd = _block_diag(LANES)
        lane = lax.broadcasted_iota(jnp.int32, (CHUNK, LANES), 1)
        for rb in range(tm // CHUNK):
            rows = pl.ds(rb * CHUNK, CHUNK)
            for b in range(w // LANES):
                cols = pl.ds(b * LANES, LANES)
                dl = _seg_sum(do_ref[rows, cols] * o_ref[rows, cols].astype(F32), bd)
                dl_ref[rows, cols] = dl
                ls = lse_ref[rows, cols]
                st_ref[rows, cols] = jnp.where(lane == 0, ls, jnp.where(
                    lane == 1, pltpu.roll(ls, HEAD_DIM - 1, 1), jnp.where(
                        lane == 2, pltpu.roll(dl, 2, 1), pltpu.roll(dl, HEAD_DIM + 3, 1))))

    spec = pl.BlockSpec((tm, w), lambda i: (i, 0))
    return pl.pallas_call(
        body, name=name, grid=(s // tm,), in_specs=[spec, spec, spec], out_specs=[spec, spec],
        out_shape=[jax.ShapeDtypeStruct((s, w), F32)] * 2, compiler_params=_params("parallel"),
    )(do, o, lse)


def attention_dq(qk, qkv, do, lse, delta, name, comm=None):
    s, w2 = qk.shape
    w = w2 // 2
    t = _tile(s, ATT_TILE)
    scale = HEAD_DIM ** -0.5

    def body(q_ref, k_ref, kp_ref, kn_ref, v_ref, vp_ref, vn_ref, do_ref, lse_ref, dl_ref, dq_ref,
             qs_ref, ks_ref, vs_ref, dos_ref, ls_ref, dls_ref, dqs_ref):
        i = pl.program_id(1)
        band = _band_bias(2 * CHUNK, 2 * CHUNK, 0)
        for pi, d in enumerate(DILATIONS):
            n = t // d
            _stage(qs_ref, q_ref, d, t, scale)
            _stage(dos_ref, do_ref, d, t)
            _stage(ls_ref, lse_ref, d, t)
            _stage(dls_ref, dl_ref, d, t)
            _stage_window(ks_ref, (k_ref, kp_ref, kn_ref), d, t)
            _stage_window(vs_ref, (v_ref, vp_ref, vn_ref), d, t)

            def unit(u, carry, n=n, d=d):
                qrow, wrow, first = _unit_rows(u, n)
                rows = pl.ds(qrow, CHUNK)
                kb, vb = ks_ref[pl.ds(wrow, 2 * CHUNK), :], vs_ref[pl.ds(wrow, 2 * CHUNK), :]
                sc = _dot_nt(_two_heads(qs_ref[rows, :]), kb) + band + _window_bias(i * n + first, s // d)
                p = jnp.exp(sc - _head_columns(ls_ref[rows, :]))
                dp = _dot_nt(_two_heads(dos_ref[rows, :]), vb)
                ds = p * (dp - _head_columns(dls_ref[rows, :]))
                dqs_ref[rows, :] = _merge_heads(_dot(ds.astype(BF16), kb)) * scale
                return carry

            lax.fori_loop(0, t // CHUNK, unit, 0, unroll=UNITS_PER_TRIP)

            def add(idx, rows, pi=pi):
                dq_ref[idx, :] = dqs_ref[rows, :] if pi == 0 else dq_ref[idx, :] + dqs_ref[rows, :]
            _scatter_rows(None, None, d, t, add)

    ospec = pl.BlockSpec((t, LANES), lambda hp, i: (i, hp))
    win_rows = t + 2 * ATT_HALO
    return _pallas(
        body, name=name, grid=(w // LANES, s // t), args=[qk, qk, qk, qk, qkv, qkv, qkv, do, lse, delta], comm=comm,
        in_specs=_att_specs(s, t, _col(0), halo=False) + _att_specs(s, t, _col(1)) + _att_specs(s, t, _col(2))
        + [ospec, ospec, ospec],
        out_specs=[ospec],
        out_shape=[jax.ShapeDtypeStruct((s, w), F32)],
        scratch=[pltpu.VMEM((t, LANES), BF16), pltpu.VMEM((win_rows, LANES), BF16), pltpu.VMEM((win_rows, LANES), BF16),
                 pltpu.VMEM((t, LANES), BF16), pltpu.VMEM((t, LANES), F32), pltpu.VMEM((t, LANES), F32),
                 pltpu.VMEM((t, LANES), F32)],
        sem=("parallel", "parallel"))


def attention_dkv(qk, qkv, do, stats, name):
    s, w2 = qk.shape
    w = w2 // 2
    t = _tile(s, ATT_TILE)
    scale = HEAD_DIM ** -0.5

    def body(k_ref, v_ref, q_ref, qp_ref, qn_ref, do_ref, dop_ref, don_ref, st_ref, stp_ref, stn_ref,
             dk_ref, dv_ref, ks_ref, vs_ref, qs_ref, dos_ref, sts_ref, dks_ref, dvs_ref):
        i = pl.program_id(1)
        key = lax.broadcasted_iota(jnp.int32, (CHUNK, 2 * CHUNK), 0)
        win_ = lax.broadcasted_iota(jnp.int32, (CHUNK, 2 * CHUNK), 1)
        half = jnp.where(jnp.abs(win_ - BAND - key) <= BAND, 0.0, NEG).astype(F32)
        band = jnp.concatenate([half, half], axis=1)
        edges = (i == 0, i == s // t - 1, -NEG)
        for pi, d in enumerate(DILATIONS):
            n = t // d
            _stage(ks_ref, k_ref, d, t)
            _stage(vs_ref, v_ref, d, t)
            _stage_window(qs_ref, (q_ref, qp_ref, qn_ref), d, t, scale)
            _stage_window(dos_ref, (do_ref, dop_ref, don_ref), d, t)
            _stage_window(sts_ref, (st_ref, stp_ref, stn_ref), d, t, edges=edges)

            def unit(u, carry, n=n, d=d):
                krow, wrow, _ = _unit_rows(u, n)
                rows, win = pl.ds(krow, CHUNK), pl.ds(wrow, 2 * CHUNK)
                q2, do2 = _two_heads(qs_ref[win, :]), _two_heads(dos_ref[win, :])
                st = jnp.transpose(sts_ref[win, :])
                lse2 = jnp.concatenate([st[0:1, :], st[1:2, :]], axis=1)
                dl2 = jnp.concatenate([st[2:3, :], st[3:4, :]], axis=1)
                p = jnp.exp(_dot_nt(ks_ref[rows, :], q2) + band - lse2)
                ds = p * (_dot_nt(vs_ref[rows, :], do2) - dl2)
                dvs_ref[rows, :] = _dot(p.astype(BF16), do2)
                dks_ref[rows, :] = _dot(ds.astype(BF16), q2)
                return carry

            lax.fori_loop(0, t // CHUNK, unit, 0, unroll=UNITS_PER_TRIP)

            def add(idx, rows, pi=pi):
                dk_ref[idx, :] = dks_ref[rows, :] if pi == 0 else dk_ref[idx, :] + dks_ref[rows, :]
                dv_ref[idx, :] = dvs_ref[rows, :] if pi == 0 else dv_ref[idx, :] + dvs_ref[rows, :]
            _scatter_rows(None, None, d, t, add)

    ident = lambda hp: hp
    ospec = pl.BlockSpec((t, LANES), lambda hp, i: (i, hp))
    win_rows = t + 2 * ATT_HALO
    return pl.pallas_call(
        body, name=name, grid=(w // LANES, s // t),
        in_specs=_att_specs(s, t, _col(1), halo=False) + _att_specs(s, t, _col(2), halo=False)
        + _att_specs(s, t, _col(0)) + _att_specs(s, t, ident) * 2,
        out_specs=[ospec, ospec],
        out_shape=[jax.ShapeDtypeStruct((s, w), F32)] * 2,
        scratch_shapes=[pltpu.VMEM((t, LANES), BF16), pltpu.VMEM((t, LANES), BF16),
                        pltpu.VMEM((win_rows, LANES), BF16), pltpu.VMEM((win_rows, LANES), BF16),
                        pltpu.VMEM((win_rows, LANES), F32),
                        pltpu.VMEM((t, LANES), F32), pltpu.VMEM((t, LANES), F32)],
        compiler_params=_params("parallel", "parallel"),
    )(qk, qkv, qk, qk, qk, do, do, do, stats, stats, stats)


def adamw_update(recvs, w, m, v, name):
    nl = len(recvs)
    r, c = recvs[0].shape[1:]
    tr = 256 if (r > 256 and r % 256 == 0) else r
    nt = r // tr
    c1 = 1.0 - ADAM_B1 ** ADAM_STEP
    c2 = 1.0 - ADAM_B2 ** ADAM_STEP

    def body(*refs):
        g_refs = refs[:nl]
        w_ref, m_ref, v_ref, go_ref, d_ref, mo_ref, vo_ref = refs[nl:]

        def update(g_ref):
            g = g_ref[0].astype(F32)
            for j in range(1, N_DEV):
                g = g + g_ref[j].astype(F32)
            mn = ADAM_B1 * m_ref[...] + (1.0 - ADAM_B1) * g
            vn = ADAM_B2 * v_ref[...] + (1.0 - ADAM_B2) * (g * g)
            go_ref[...] = g
            mo_ref[...] = mn
            vo_ref[...] = vn
            d_ref[...] = -ADAM_LR * ((mn / c1) / (jnp.sqrt(vn / c2) + ADAM_EPS) + ADAM_WD * w_ref[...])

        for layer in range(nl):
            pl.when(pl.program_id(0) == layer)(functools.partial(update, g_refs[layer]))

    def gspec(layer):
        return pl.BlockSpec((N_DEV, tr, c), lambda l, i: (0, jnp.where(l == layer, i, 0), 0))

    spec = pl.BlockSpec((tr, c), lambda l, i: (l * nt + i, 0))
    return pl.pallas_call(
        body, name=name, grid=(nl, nt),
        in_specs=[gspec(layer) for layer in range(nl)] + [spec, spec, spec],
        out_specs=[spec] * 4, out_shape=[jax.ShapeDtypeStruct((nl * r, c), F32)] * 4,
        compiler_params=_params("arbitrary", "arbitrary"),
    )(*recvs, w, m, v)


BIG = ("mlp_w1", "mlp_w2", "ab_w_in", "ab_w_out", "c_w_qkv", "c_w_out")
SMALL = ("mix_norm_g", "mlp_norm_g", "a_spatial_w", "a_spatial_b", "a_vnorm_g", "a_vnorm_b", "b_conv_b", "b_norm_g",
         "b_norm_b", "c_q_norm_g", "c_k_norm_g")
WEIGHTS = ("mix_norm_g", "mlp_norm_g", "mlp_w1", "mlp_w2", "ab_w_in", "a_spatial_w", "a_spatial_b", "a_vnorm_g",
           "a_vnorm_b", "b_conv_w", "b_conv_b", "b_norm_g", "b_norm_b", "ab_w_out", "c_w_qkv", "c_q_norm_g",
           "c_k_norm_g", "c_w_out")


def _mixer_params(p, conv_full, i):
    aw = p["a_vnorm_g"].shape[1]
    row = lambda t: t[i][None, :]
    return dict(
        w=p["a_spatial_w"][i].astype(BF16), wt=jnp.swapaxes(p["a_spatial_w"][i], 1, 2).astype(BF16),
        bias=jnp.repeat(p["a_spatial_b"][i].T, aw // p["a_spatial_b"].shape[1], axis=1),
        vg=row(p["a_vnorm_g"]), vb=row(p["a_vnorm_b"]), cw=jnp.pad(conv_full[i], ((0, 1), (0, 0))),
        cb=row(p["b_conv_b"]), cg=row(p["b_norm_g"]), cbn=row(p["b_norm_b"]))


def _head_gains(p, i):
    rep = LANES // HEAD_DIM
    return jnp.stack([jnp.tile(p["c_q_norm_g"][i], rep), jnp.tile(p["c_k_norm_g"][i], rep),
                      jnp.ones((LANES,), F32)])[:, None, :]


def _unused_forward_backward(x, target, p, wg, conv_full):
    s, d = x.shape
    depth = p["mix_norm_g"].shape[0]
    tabs = rope_tables(s)
    saved = []
    for l in range(depth):
        i = l // 2
        mix_g, mlp_g = p["mix_norm_g"][l][None, :], p["mlp_norm_g"][l][None, :]
        st = dict(x_in=x)
        if l % 2 == 0:
            sp = _mixer_params(p, conv_full, i)
            z, h = norm_matmul(x, mix_g, wg["ab_w_in"], i, f"ab_in_{l}")
            ycat, gc = ab_mid_forward(z, sp, f"ab_mid_{l}")
            x = matmul_residual(x, ycat, wg["ab_w_out"], i, f"ab_out_{l}")
            st.update(z=z, h=h, y=ycat, gc=gc, sp=sp)
        else:
            gains = _head_gains(p, i)
            qkv, h = norm_matmul(x, mix_g, wg["c_w_qkv"], i, f"c_qkv_{l}")
            qkvn = qk_prep_forward(qkv, tabs, gains, f"c_prep_{l}")
            outs = [band_attention_forward(qkvn, dil, f"c_attn_{l}_d{dil}") for dil in DILATIONS]
            o, lse = attention_merge([t[0] for t in outs], [t[1] for t in outs], f"c_merge_{l}")
            x = matmul_residual(x, o, wg["c_w_out"], i, f"c_out_{l}")
            st.update(qkv=qkv, h=h, qkvn=qkvn, y=o, lse=lse, gains=gains)
        st["x_mid"] = x
        x, a, h2 = mlp_forward(x, mlp_g, wg["mlp_w1"], wg["mlp_w2"], l, f"mlp_{l}")
        st.update(a=a, h2=h2)
        saved.append(st)

    dy, loss_part = loss_and_grad(x, target, "loss")

    big = {n: [None] * p_len for n, p_len in (("mlp_w1", depth), ("mlp_w2", depth), ("ab_w_in", depth // 2 + depth % 2),
                                              ("ab_w_out", depth // 2 + depth % 2), ("c_w_qkv", depth // 2),
                                              ("c_w_out", depth // 2))}
    small = {n: [None] * p[n].shape[0] for n in SMALL}
    conv_grads = [None] * p["b_conv_b"].shape[0]
    for l in reversed(range(depth)):
        i, st = l // 2, saved[l]
        mix_g, mlp_g = p["mix_norm_g"][l][None, :], p["mlp_norm_g"][l][None, :]
        dxm, da, dg = mlp_backward(dy, st["a"], st["x_mid"], mlp_g, wg["mlp_w1"], wg["mlp_w2"], l, f"mlp_bwd_{l}")
        small["mlp_norm_g"][l] = dg[0]
        big["mlp_w1"][l] = matmul_tn(st["h2"], da, "cols", f"mlp_dw1_{l}")
        big["mlp_w2"][l] = matmul_tn(st["a"], dy, "rows", f"mlp_dw2_{l}", relu2=True)
        if l % 2 == 0:
            sp = st["sp"]
            dycat = matmul_nt(dxm, wg["ab_w_out"], i, f"ab_out_bwd_{l}")
            big["ab_w_out"][i] = matmul_tn(st["y"], dxm, "rows", f"ab_dwout_{l}")
            dz, dgc, dw, dsb, dvg, dvb, dcg, dcbn, dcb = ab_mid_backward(dycat, st["z"], st["gc"], sp, f"ab_mid_bwd_{l}")
            dz, dcw = conv_backward(dgc, st["z"], dz, sp, f"ab_conv_bwd_{l}")
            small["a_spatial_w"][i], small["a_spatial_b"][i] = dw, dsb
            small["a_vnorm_g"][i], small["a_vnorm_b"][i] = dvg[0], dvb[0]
            small["b_norm_g"][i], small["b_norm_b"][i], small["b_conv_b"][i] = dcg[0], dcbn[0], dcb[0]
            conv_grads[i] = dcw[:CONV_W]
            dy, dg = matmul_nt_norm_backward(dz, wg["ab_w_in"], i, st["x_in"], mix_g, dxm, f"ab_in_bwd_{l}")
            big["ab_w_in"][i] = matmul_tn(st["h"], dz, "cols", f"ab_dwin_{l}")
        else:
            do = matmul_nt(dxm, wg["c_w_out"], i, f"c_out_bwd_{l}")
            big["c_w_out"][i] = matmul_tn(st["y"], dxm, "rows", f"c_dwout_{l}")
            delta, dob = attention_delta(do, st["y"], f"c_delta_{l}")
            dqs, dks, dvs = [], [], []
            for dil in DILATIONS:
                dqs.append(band_attention_dq(st["qkvn"], dob, st["lse"], delta, dil, f"c_attn_dq_{l}_d{dil}"))
                dk, dv = band_attention_dkv(st["qkvn"], dob, st["lse"], delta, dil, f"c_attn_dkv_{l}_d{dil}")
                dks.append(dk)
                dvs.append(dv)
            dqkv, dgn = qk_prep_backward(dqs, dks, dvs, st["qkv"], tabs, st["gains"], f"c_prep_bwd_{l}")
            small["c_q_norm_g"][i] = dgn[0, 0, :HEAD_DIM] + dgn[0, 0, HEAD_DIM:]
            small["c_k_norm_g"][i] = dgn[1, 0, :HEAD_DIM] + dgn[1, 0, HEAD_DIM:]
            dy, dg = matmul_nt_norm_backward(dqkv, wg["c_w_qkv"], i, st["x_in"], mix_g, dxm, f"c_qkv_bwd_{l}")
            big["c_w_qkv"][i] = matmul_tn(st["h"], dqkv, "cols", f"c_dwqkv_{l}")
        small["mix_norm_g"][l] = dg[0]
    small = {n: jnp.stack(v) for n, v in small.items()}
    return loss_part, dy, big, small, jnp.stack(conv_grads)


def _pack(d):
    flat = jnp.concatenate([d[n].reshape(-1) for n in SMALL])
    rows = -(-flat.shape[0] // (8 * LANES)) * 8
    return jnp.pad(flat, (0, rows * LANES - flat.shape[0])).reshape(rows, LANES)


def _unpack(packed, like):
    flat, out, pos = packed.reshape(-1), {}, 0
    for n in SMALL:
        size = math.prod(like[n].shape)
        out[n] = flat[pos:pos + size].reshape(like[n].shape)
        pos += size
    return out


def _unused_kernel(x, mix_norm_g, mlp_norm_g, mlp_w1, mlp_w2, ab_w_in, a_spatial_w, a_spatial_b, a_vnorm_g, a_vnorm_b, b_conv_w, b_conv_b, b_norm_g, b_norm_b, ab_w_out, c_w_qkv, c_q_norm_g, c_k_norm_g, c_w_out, loss_target, m_mix_norm_g, m_mlp_norm_g, m_mlp_w1, m_mlp_w2, m_ab_w_in, m_a_spatial_w, m_a_spatial_b, m_a_vnorm_g, m_a_vnorm_b, m_b_conv_w, m_b_conv_b, m_b_norm_g, m_b_norm_b, m_ab_w_out, m_c_w_qkv, m_c_q_norm_g, m_c_k_norm_g, m_c_w_out, v_mix_norm_g, v_mlp_norm_g, v_mlp_w1, v_mlp_w2, v_ab_w_in, v_a_spatial_w, v_a_spatial_b, v_a_vnorm_g, v_a_vnorm_b, v_b_conv_w, v_b_conv_b, v_b_norm_g, v_b_norm_b, v_ab_w_out, v_c_w_qkv, v_c_q_norm_g, v_c_k_norm_g, v_c_w_out):
    args = dict(locals())
    w = {n: args[n] for n in WEIGHTS}
    m = {n: args["m_" + n] for n in WEIGHTS}
    v = {n: args["v_" + n] for n in WEIGHTS}

    wg = {n: all_gather(w[n].astype(BF16), "gather_" + n) for n in BIG}
    for n in ("ab_w_out", "c_w_out"):
        t = wg[n]
        wg[n] = t.reshape(t.shape[0], t.shape[1] * t.shape[2], t.shape[3])
    conv = all_gather(w["b_conv_w"], "gather_b_conv_w")
    conv_full = jnp.swapaxes(conv, 1, 2).reshape(conv.shape[0], conv.shape[2], N_DEV * conv.shape[3])

    loss_part, dx, big, small, conv_grad = forward_backward(x[0], loss_target[0], w, wg, conv_full)
    loss = lax.psum(jnp.sum(loss_part), ("x", "y", "c"))

    grads, deltas, new_m, new_v = {}, {}, {}, {}

    def update(n, recv):
        shape = w[n].shape
        flat = lambda t: t.reshape(-1, shape[-1])
        outs = adamw_update(recv.reshape((N_DEV, -1, shape[-1])), flat(w[n]), flat(m[n]), flat(v[n]), "adamw_" + n)
        grads[n], deltas[n], new_m[n], new_v[n] = (t.reshape(shape) for t in outs)

    for n in BIG:
        update(n, exchange(big[n], "exchange_" + n))
    nl, kw, cw = conv_grad.shape
    conv_parts = jnp.transpose(conv_grad.reshape(nl, kw, N_DEV, cw // N_DEV), (2, 0, 1, 3))
    update("b_conv_w", exchange([conv_parts], "exchange_b_conv_w"))

    packed = all_gather(_pack(small)[None], "gather_small_grads")[0]
    outs = adamw_update(packed, _pack(w), _pack(m), _pack(v), "adamw_small")
    for dst, t in zip((grads, deltas, new_m, new_v), outs):
        dst.update(_unpack(t, w))

    return (loss, dx[None], *[grads[n] for n in WEIGHTS], *[deltas[n] for n in WEIGHTS],
            *[new_m[n] for n in WEIGHTS], *[new_v[n] for n in WEIGHTS])


class Traffic:
    def __init__(self, shards, full=()):
        self.shards, self.w, self.queue, self.parts = shards, dict(full), [], {}

    def run(self, fn, *args, gather=(), send=False, **kw):
        operands, flags, dest = [], [], []
        if self.shards is None:
            if send:
                self.parts.update(self.queue)
                self.queue = []
        else:
            for k in gather:
                if k not in self.w:
                    operands.append(self.shards[k])
                    flags.append(("gather", SHARD_AXIS.get(k[0])))
                    dest.append((self.w, k))
            if send:
                for k, t in self.queue:
                    operands.append(t)
                    flags.append(("scatter", SHARD_AXIS.get(k[0])))
                    dest.append((self.parts, k))
                self.queue = []
        outs, couts = fn(*args, comm=PeerCopies(operands, flags) if operands else None, **kw)
        for (table, k), t in zip(dest, couts):
            table[k] = t
        return outs

    def flush(self, name, extra=()):
        self.queue += list(extra)
        self.run(lambda comm: ([], run_copies(comm, name) if comm is not None else []), send=True)


SHARD_AXIS = {"mlp_w1": 1, "mlp_w2": 0, "ab_w_in": 1, "ab_w_out": 0, "c_w_qkv": 1, "c_w_out": 0}


def forward_backward(x, target, p, tr, conv_full):
    s, d = x.shape
    depth = p["mix_norm_g"].shape[0]
    tabs = rope_tables(s)
    saved = []
    for l in range(depth):
        i = l // 2
        mix_g, mlp_g = p["mix_norm_g"][l][None, :], p["mlp_norm_g"][l][None, :]
        st = dict(x_in=x)
        nxt = () if l + 1 == depth else ((("c_w_qkv", i), ("c_w_out", i)) if l % 2 == 0 else
                                        (("ab_w_in", i + 1), ("ab_w_out", i + 1)))
        if l % 2 == 0:
            sp = _mixer_params(p, conv_full, i)
            z, h = tr.run(norm_matmul, x, mix_g, tr.w["ab_w_in", i], f"ab_in_{l}", gather=[("mlp_w1", l)])
            ycat, gc = tr.run(ab_mid_forward, z, sp, f"ab_mid_{l}", gather=[("mlp_w2", l)])
            x = matmul_residual(x, ycat, tr.w["ab_w_out", i], f"ab_out_{l}")
            st.update(z=z, h=h, y=ycat, gc=gc, sp=sp)
        else:
            gains = _head_gains(p, i)
            qkv, h = norm_matmul(x, mix_g, tr.w["c_w_qkv", i], f"c_qkv_{l}")[0]
            qk = qk_prep_forward(qkv, tabs, gains, f"c_prep_{l}")[0]
            ahead = [(n, j) for j in (l, l + 1) if j < depth for n in ("mlp_w1", "mlp_w2")]
            o, lse = tr.run(attention_forward, qk, qkv, f"c_attn_{l}", gather=ahead)
            x = matmul_residual(x, o, tr.w["c_w_out", i], f"c_out_{l}")
            st.update(qkv=qkv, h=h, qk=qk, y=o, lse=lse, gains=gains)
        st["x_mid"] = x
        x, a, h2 = tr.run(mlp_forward, x, mlp_g, tr.w["mlp_w1", l], tr.w["mlp_w2", l], f"mlp_{l}", gather=nxt)
        st.update(a=a, h2=h2)
        saved.append(st)

    dy, loss_part = loss_and_grad(x, target, "loss")

    def tn(*a, comm, **kw):
        out, couts = matmul_tn(*a, comm=comm, **kw)
        return [out], couts

    small = {n: [None] * p[n].shape[0] for n in SMALL}
    conv_grads = [None] * p["b_conv_b"].shape[0]
    for l in reversed(range(depth)):
        i, st = l // 2, saved[l]
        mix_g, mlp_g = p["mix_norm_g"][l][None, :], p["mlp_norm_g"][l][None, :]
        w1, w2 = tr.w["mlp_w1", l], tr.w["mlp_w2", l]
        dxm, da, dg, dyb = tr.run(mlp_backward, dy, st["a"], st["x_mid"], mlp_g, w1, w2, f"mlp_bwd_{l}", send=True)
        small["mlp_norm_g"][l] = dg[0]
        tr.queue.append((("mlp_w1", l), matmul_tn(st["h2"], da, f"mlp_dw1_{l}", n_split=2)[0]))
        tr.queue.append((("mlp_w2", l), matmul_tn(st["a"], dyb, f"mlp_dw2_{l}", m_split=2, relu2=True)[0]))
        if l % 2 == 0:
            sp = st["sp"]
            wout = tr.w["ab_w_out", i]
            dycat = matmul_nt(dxm, wout, f"ab_out_bwd_{l}")
            tr.queue.append((("ab_w_out", i), matmul_tn(st["y"], dxm, f"ab_dwout_{l}")[0]))
            dz, dgc, dw, dsb, dvg, dvb, dcg, dcbn, dcb = ab_mid_backward(dycat, st["z"], st["gc"], sp, f"ab_mid_bwd_{l}")
            dz, dcw = tr.run(conv_backward, dgc, st["z"], dz, sp, f"ab_conv_bwd_{l}", send=True)
            small["a_spatial_w"][i], small["a_spatial_b"][i] = dw, dsb
            small["a_vnorm_g"][i], small["a_vnorm_b"][i] = dvg[0], dvb[0]
            small["b_norm_g"][i], small["b_norm_b"][i], small["b_conv_b"][i] = dcg[0], dcbn[0], dcb[0]
            conv_grads[i] = dcw[:CONV_W]
            dy, dg = matmul_nt_norm_backward(dz, tr.w["ab_w_in", i], st["x_in"], mix_g, dxm, f"ab_in_bwd_{l}")
            small["mix_norm_g"][l] = dg[0]
            last = []
            if l == 0 and tr.shards is not None:
                tr.shards["small_grads", 0] = _pack({n: jnp.stack(t) for n, t in small.items()})
                last = [("small_grads", 0)]
            dwin = tr.run(tn, st["h"], dz, f"ab_dwin_{l}", send=True, gather=last)[0]
            tr.queue.append((("ab_w_in", i), dwin))
        else:
            wout = tr.w["c_w_out", i]
            do = matmul_nt(dxm, wout, f"c_out_bwd_{l}")
            tr.queue.append((("c_w_out", i), matmul_tn(st["y"], dxm, f"c_dwout_{l}")[0]))
            delta, stats = attention_delta(do, st["y"], st["lse"], f"c_delta_{l}")
            dq = tr.run(attention_dq, st["qk"], st["qkv"], do, st["lse"], delta, f"c_attn_dq_{l}", send=True)[0]
            dk, dv = attention_dkv(st["qk"], st["qkv"], do, stats, f"c_attn_dkv_{l}")
            dqkv, dgn = qk_prep_backward(dq, dk, dv, st["qkv"], tabs, st["gains"], f"c_prep_bwd_{l}")
            small["c_q_norm_g"][i] = dgn[0, 0, :HEAD_DIM] + dgn[0, 0, HEAD_DIM:]
            small["c_k_norm_g"][i] = dgn[1, 0, :HEAD_DIM] + dgn[1, 0, HEAD_DIM:]
            dy, dg = matmul_nt_norm_backward(dqkv, tr.w["c_w_qkv", i], st["x_in"], mix_g, dxm, f"c_qkv_bwd_{l}")
            small["mix_norm_g"][l] = dg[0]
            tr.queue.append((("c_w_qkv", i), matmul_tn(st["h"], dqkv, f"c_dwqkv_{l}", n_split=2)[0]))
    small = {n: jnp.stack(v) for n, v in small.items()}
    return loss_part, dy, small, jnp.stack(conv_grads)


def kernel(x, mix_norm_g, mlp_norm_g, mlp_w1, mlp_w2, ab_w_in, a_spatial_w, a_spatial_b, a_vnorm_g, a_vnorm_b, b_conv_w, b_conv_b, b_norm_g, b_norm_b, ab_w_out, c_w_qkv, c_q_norm_g, c_k_norm_g, c_w_out, loss_target, m_mix_norm_g, m_mlp_norm_g, m_mlp_w1, m_mlp_w2, m_ab_w_in, m_a_spatial_w, m_a_spatial_b, m_a_vnorm_g, m_a_vnorm_b, m_b_conv_w, m_b_conv_b, m_b_norm_g, m_b_norm_b, m_ab_w_out, m_c_w_qkv, m_c_q_norm_g, m_c_k_norm_g, m_c_w_out, v_mix_norm_g, v_mlp_norm_g, v_mlp_w1, v_mlp_w2, v_ab_w_in, v_a_spatial_w, v_a_spatial_b, v_a_vnorm_g, v_a_vnorm_b, v_b_conv_w, v_b_conv_b, v_b_norm_g, v_b_norm_b, v_ab_w_out, v_c_w_qkv, v_c_q_norm_g, v_c_k_norm_g, v_c_w_out):
    args = dict(locals())
    w = {n: args[n] for n in WEIGHTS}
    m = {n: args["m_" + n] for n in WEIGHTS}
    v = {n: args["v_" + n] for n in WEIGHTS}

    shards = {(n, l): w[n][l].astype(BF16) for n in BIG for l in range(w[n].shape[0])}
    shards["b_conv_w", 0] = w["b_conv_w"]
    tr = Traffic(shards)
    first = [("ab_w_in", 0), ("ab_w_out", 0), ("b_conv_w", 0)]
    tr.run(lambda comm: ([], run_copies(comm, "gather_first")), gather=first)
    conv = tr.w["b_conv_w", 0]
    conv_full = jnp.transpose(conv, (1, 2, 0, 3)).reshape(conv.shape[1], conv.shape[2], -1)

    loss_part, dx, small, conv_grad = forward_backward(x[0], loss_target[0], w, tr, conv_full)
    loss = lax.psum(jnp.sum(loss_part), ("x", "y", "c"))
    nl, kw, cw = conv_grad.shape
    conv_parts = jnp.transpose(conv_grad.reshape(nl, kw, N_DEV, cw // N_DEV), (2, 0, 1, 3))
    tr.flush("exchange_last", [(("b_conv_w", 0), conv_parts)])
    packed = tr.w["small_grads", 0]

    grads, deltas, new_m, new_v = {}, {}, {}, {}

    def update(n, recvs):
        shape = w[n].shape
        flat = lambda t: t.reshape(-1, shape[-1])
        recvs = [t.reshape(N_DEV, -1, shape[-1]) for t in recvs]
        outs = adamw_update(recvs, flat(w[n]), flat(m[n]), flat(v[n]), "adamw_" + n)
        grads[n], deltas[n], new_m[n], new_v[n] = (t.reshape(shape) for t in outs)

    for n in BIG:
        update(n, [tr.parts[n, l] for l in range(w[n].shape[0])])
    update("b_conv_w", [tr.parts["b_conv_w", 0]])
    outs = adamw_update([packed], _pack(w), _pack(m), _pack(v), "adamw_small")
    for dst, t in zip((grads, deltas, new_m, new_v), outs):
        dst.update(_unpack(t, w))

    return (loss, dx[None], *[grads[n] for n in WEIGHTS], *[deltas[n] for n in WEIGHTS],
            *[new_m[n] for n in WEIGHTS], *[new_v[n] for n in WEIGHTS])
```

```python
import functools
import math

import jax
import jax.numpy as jnp
from jax import lax
from jax.experimental import pallas as pl
from jax.experimental.pallas import tpu as pltpu

F32, BF16 = jnp.float32, jnp.bfloat16
N_DEV = 8
EPS = 1e-6
NEG = -1e30
LANES = 128
HEAD_DIM = 64
CHUNK = 128
CONV_W = 31
CONV_HALO = 16
BAND = 64
DILATIONS = (1, 4, 16)
ROT_DIM = 16
ROPE_THETA = 500000.0
VMEM_LIMIT = 56 * 1024 * 1024
MLP_CHUNK = 1024
MLP_BWD_CHUNK = 512
ADAM_LR, ADAM_B1, ADAM_B2, ADAM_EPS, ADAM_WD, ADAM_STEP = 0.001, 0.9, 0.999, 1e-08, 0.01, 10
MESH = pl.DeviceIdType.MESH


def _params(*sem):
    return pltpu.CompilerParams(dimension_semantics=sem, vmem_limit_bytes=VMEM_LIMIT)


def _dot(a, b):
    return jnp.dot(a, b, preferred_element_type=F32)


def _dot_nt(a, b):
    return lax.dot_general(a, b, (((1,), (1,)), ((), ())), preferred_element_type=F32)


def _dot_tn(a, b):
    return lax.dot_general(a, b, (((0,), (0,)), ((), ())), preferred_element_type=F32)


def _rms_r(x):
    return lax.rsqrt(jnp.mean(x * x, axis=-1, keepdims=True) + EPS)


def _sigmoid(x):
    return 1.0 / (1.0 + jnp.exp(-x))


_GK = math.sqrt(2.0 / math.pi)


def _gelu(x):
    return 0.5 * x * (1.0 + jnp.tanh(_GK * (x + 0.044715 * x * x * x)))


def _gelu_grad(x):
    t = jnp.tanh(_GK * (x + 0.044715 * x * x * x))
    return 0.5 * (1.0 + t) + 0.5 * x * (1.0 - t * t) * (_GK * (1.0 + 3.0 * 0.044715 * x * x))


def _seg_sum(x, bd):
    hi = x.astype(BF16)
    r1 = x - hi.astype(F32)
    mid = r1.astype(BF16)
    lo = (r1 - mid.astype(F32)).astype(BF16)
    return _dot(hi, bd) + _dot(mid, bd) + _dot(lo, bd)


def _block_diag(n):
    i = lax.broadcasted_iota(jnp.int32, (n, n), 0) // HEAD_DIM
    j = lax.broadcasted_iota(jnp.int32, (n, n), 1) // HEAD_DIM
    return jnp.where(i == j, 1.0, 0.0).astype(BF16)


def _tile(s, cap):
    t = min(s, cap)
    assert s % t == 0
    return t


def _my_index():
    return 4 * lax.axis_index("x") + 2 * lax.axis_index("y") + lax.axis_index("c")


def _device(i):
    return (i // 4, (i // 2) % 2, i % 2)


_HBM = pl.BlockSpec(memory_space=pl.ANY)


class PeerCopies:
    def __init__(self, operands, modes):
        self.inputs, self.modes = list(operands), list(modes)
        self.out_shape = []
        for t, (kind, axis) in zip(self.inputs, self.modes):
            shape = list(t.shape)
            if kind == "gather":
                shape = [N_DEV] + shape if axis is None else shape[:axis] + [N_DEV * shape[axis]] + shape[axis + 1:]
            elif axis is not None:
                shape = [N_DEV] + shape[:axis] + [shape[axis] // N_DEV] + shape[axis + 1:]
            self.out_shape.append(jax.ShapeDtypeStruct(tuple(shape), t.dtype))
        n = len(self.inputs)
        self.scratch = [pltpu.SemaphoreType.DMA((n, N_DEV - 1)), pltpu.SemaphoreType.DMA((n, N_DEV - 1)),
                        pltpu.SemaphoreType.DMA((n,))]

    @staticmethod
    def _block(ref, axis, size, j):
        if axis is None:
            return ref.at[j]
        return ref.at[tuple([slice(None)] * axis + [pl.ds(j * size, size)])]

    def _copies(self, in_refs, out_refs, sems, arrivals):
        send_sems, recv_sems, local_sems = sems
        me = _my_index()
        local, sends, recvs = [], [], []
        for t, (src, dst) in enumerate(zip(in_refs, out_refs)):
            kind, axis = self.modes[t]
            if kind == "gather":
                size = None if axis is None else src.shape[axis]
                source = lambda j, src=src: src
                place = lambda j, dst=dst, axis=axis, size=size: self._block(dst, axis, size, j)
            else:
                size = None if axis is None else src.shape[axis] // N_DEV
                source = lambda j, src=src, axis=axis, size=size: self._block(src, axis, size, j)
                place = lambda j, dst=dst: dst.at[j]
            local.append(pltpu.make_async_copy(source(me), place(me), local_sems.at[t]))
            for k in range(N_DEV - 1):
                to, frm = (me + k + 1) % N_DEV, (me + N_DEV - k - 1) % N_DEV
                sends.append(pltpu.make_async_remote_copy(
                    src_ref=source(to), dst_ref=place(me), send_sem=send_sems.at[t, k], recv_sem=recv_sems.at[t, k],
                    device_id=_device(to), device_id_type=MESH))
                if arrivals:
                    recvs.append(pltpu.make_async_remote_copy(
                        src_ref=source(me), dst_ref=place(frm), send_sem=send_sems.at[t, k], recv_sem=recv_sems.at[t, k],
                        device_id=_device(frm), device_id_type=MESH))
        return local, sends, recvs

    def start(self, in_refs, out_refs, sems):
        local, sends, _ = self._copies(in_refs, out_refs, sems, False)
        for cp in local + sends:
            cp.start()

    def finish(self, in_refs, out_refs, sems):
        local, sends, recvs = self._copies(in_refs, out_refs, sems, True)
        for cp in recvs:
            cp.wait_recv()
        for cp in sends:
            cp.wait_send()
        for cp in local:
            cp.wait()


def _pallas(body, *, name, args, in_specs, out_specs, out_shape, grid=(), scratch=(), sem=(), comm=None, aliases=None):
    n_in, n_out, n_scr = len(args), len(out_shape), len(scratch)
    if comm is None:
        outs = pl.pallas_call(
            body, name=name, grid=grid, in_specs=in_specs, out_specs=out_specs, out_shape=out_shape,
            scratch_shapes=list(scratch), input_output_aliases=aliases or {}, compiler_params=_params(*sem))(*args)
        return list(outs), []
    ci, co = len(comm.inputs), len(comm.out_shape)

    def hosted(*refs):
        ins, cins = refs[:n_in], refs[n_in:n_in + ci]
        outs, couts = refs[n_in + ci:n_in + ci + n_out], refs[n_in + ci + n_out:n_in + ci + n_out + co]
        rest = refs[n_in + ci + n_out + co:]
        scr, sems = rest[:n_scr], rest[n_scr:]
        if not grid:
            comm.start(cins, couts, sems)
            comm.finish(cins, couts, sems)
            return
        first = last = None
        for axis, size in enumerate(grid):
            f, l = pl.program_id(axis) == 0, pl.program_id(axis) == size - 1
            first, last = (f, l) if first is None else (first & f, last & l)
        pl.when(first)(lambda: comm.start(cins, couts, sems))
        body(*ins, *outs, *scr)
        pl.when(last)(lambda: comm.finish(cins, couts, sems))

    outs = pl.pallas_call(
        hosted, name=name, grid=grid, in_specs=list(in_specs) + [_HBM] * ci, out_specs=list(out_specs) + [_HBM] * co,
        out_shape=list(out_shape) + comm.out_shape, scratch_shapes=list(scratch) + comm.scratch,
        input_output_aliases=aliases or {}, compiler_params=_params(*["arbitrary"] * len(grid)))(*args, *comm.inputs)
    return list(outs[:n_out]), list(outs[n_out:])


def run_copies(comm, name):
    return _pallas(None, name=name, args=[], in_specs=[], out_specs=[], out_shape=[], comm=comm)[1]


def norm_matmul(x, g, wg, name, comm=None):
    s, d = x.shape
    n = wg.shape[-1]
    ns = 1024 if n % 1024 == 0 else n // 4
    tm = _tile(s, 1024)

    def body(x_ref, g_ref, w_ref, z_ref, h_ref):
        @pl.when(pl.program_id(1) == 0)
        def _():
            xv = x_ref[...]
            h_ref[...] = (xv * _rms_r(xv) * g_ref[...]).astype(BF16)
        z_ref[...] = _dot(h_ref[...], w_ref[...])

    return _pallas(
        body, name=name, grid=(s // tm, n // ns), args=[x, g, wg], comm=comm,
        in_specs=[pl.BlockSpec((tm, d), lambda i, j: (i, 0)),
                  pl.BlockSpec((1, d), lambda i, j: (0, 0)),
                  pl.BlockSpec((d, ns), lambda i, j: (0, j))],
        out_specs=[pl.BlockSpec((tm, ns), lambda i, j: (i, j)),
                   pl.BlockSpec((tm, d), lambda i, j: (i, 0))],
        out_shape=[jax.ShapeDtypeStruct((s, n), F32), jax.ShapeDtypeStruct((s, d), BF16)],
        sem=("parallel", "arbitrary"))


def mlp_forward(x, g, w1g, w2g, name, comm=None):
    s, d = x.shape
    f = w1g.shape[-1]
    fs = MLP_CHUNK
    tm = _tile(s, 1024)

    def body(x_ref, g_ref, w1_ref, w2_ref, xo_ref, a_ref, h_ref):
        @pl.when(pl.program_id(1) == 0)
        def _():
            xv = x_ref[...]
            h_ref[...] = (xv * _rms_r(xv) * g_ref[...]).astype(BF16)
            xo_ref[...] = xv
        a = _dot(h_ref[...], w1_ref[...])
        a_ref[...] = a.astype(BF16)
        r = jnp.maximum(a, 0.0)
        xo_ref[...] += _dot((r * r).astype(BF16), w2_ref[...])

    return _pallas(
        body, name=name, grid=(s // tm, f // fs), args=[x, g, w1g, w2g], comm=comm,
        in_specs=[pl.BlockSpec((tm, d), lambda i, j: (i, 0)),
                  pl.BlockSpec((1, d), lambda i, j: (0, 0)),
                  pl.BlockSpec((d, fs), lambda i, j: (0, j)),
                  pl.BlockSpec((fs, d), lambda i, j: (j, 0))],
        out_specs=[pl.BlockSpec((tm, d), lambda i, j: (i, 0)),
                   pl.BlockSpec((tm, fs), lambda i, j: (i, j)),
                   pl.BlockSpec((tm, d), lambda i, j: (i, 0))],
        out_shape=[jax.ShapeDtypeStruct((s, d), F32), jax.ShapeDtypeStruct((s, f), BF16),
                   jax.ShapeDtypeStruct((s, d), BF16)],
        sem=("parallel", "arbitrary"))


def _norm_backward(dh, xv, g, dres):
    r = _rms_r(xv)
    xh = xv * r
    t = dh * g
    dx = dres + r * (t - xh * jnp.mean(t * xh, axis=-1, keepdims=True))
    return dx, jnp.sum(dh * xh, axis=0, keepdims=True)


def mlp_backward(dy, a, x, g, w1g, w2g, name, comm=None):
    s, d = x.shape
    f = w1g.shape[-1]
    fs = MLP_BWD_CHUNK
    tm = _tile(s, 1024)

    def body(dy_ref, a_ref, x_ref, g_ref, w1_ref, w2_ref, dx_ref, da_ref, dg_ref, dyb_ref, dh_ref):
        i, j = pl.program_id(0), pl.program_id(1)

        @pl.when(j == 0)
        def _():
            dyb_ref[...] = dy_ref[...].astype(BF16)
            dh_ref[...] = jnp.zeros_like(dh_ref)

        dr = _dot_nt(dyb_ref[...], w2_ref[...])
        da = (dr * (2.0 * jnp.maximum(a_ref[...].astype(F32), 0.0))).astype(BF16)
        da_ref[...] = da
        dh_ref[...] += _dot_nt(da, w1_ref[...])

        @pl.when(j == f // fs - 1)
        def _():
            dx, dgp = _norm_backward(dh_ref[...], x_ref[...], g_ref[...], dy_ref[...])
            dx_ref[...] = dx

            @pl.when(i == 0)
            def _():
                dg_ref[...] = dgp

            @pl.when(i > 0)
            def _():
                dg_ref[...] += dgp

    return _pallas(
        body, name=name, grid=(s // tm, f // fs), args=[dy, a, x, g, w1g, w2g], comm=comm,
        in_specs=[pl.BlockSpec((tm, d), lambda i, j: (i, 0)),
                  pl.BlockSpec((tm, fs), lambda i, j: (i, j)),
                  pl.BlockSpec((tm, d), lambda i, j: (i, 0)),
                  pl.BlockSpec((1, d), lambda i, j: (0, 0)),
                  pl.BlockSpec((d, fs), lambda i, j: (0, j)),
                  pl.BlockSpec((fs, d), lambda i, j: (j, 0))],
        out_specs=[pl.BlockSpec((tm, d), lambda i, j: (i, 0)),
                   pl.BlockSpec((tm, fs), lambda i, j: (i, j)),
                   pl.BlockSpec((1, d), lambda i, j: (0, 0)),
                   pl.BlockSpec((tm, d), lambda i, j: (i, 0))],
        out_shape=[jax.ShapeDtypeStruct((s, d), F32), jax.ShapeDtypeStruct((s, f), BF16),
                   jax.ShapeDtypeStruct((1, d), F32), jax.ShapeDtypeStruct((s, d), BF16)],
        scratch=[pltpu.VMEM((tm, d), F32)],
        sem=("arbitrary", "arbitrary"))


def matmul_tn(a, b, name, m_split=1, n_split=1, relu2=False, comm=None):
    s, m = a.shape
    n = b.shape[1]
    ts = _tile(s, 1024)
    bm, bn = m // m_split, n // n_split
    a_map = lambda j, k: (k, j // n_split)
    b_map = lambda j, k: (k, j % n_split)

    def body(a_ref, b_ref, o_ref, acc_ref):
        k = pl.program_id(1)
        av = a_ref[...]
        if relu2:
            af = jnp.maximum(av.astype(F32), 0.0)
            av = af * af
        p = _dot_tn(av.astype(BF16), b_ref[...].astype(BF16))

        @pl.when(k == 0)
        def _():
            acc_ref[...] = p

        @pl.when(k > 0)
        def _():
            acc_ref[...] += p

        @pl.when(k == s // ts - 1)
        def _():
            o_ref[...] = acc_ref[...].astype(BF16)

    outs, couts = _pallas(
        body, name=name, grid=(m_split * n_split, s // ts), args=[a, b], comm=comm,
        in_specs=[pl.BlockSpec((ts, bm), a_map), pl.BlockSpec((ts, bn), b_map)],
        out_specs=[pl.BlockSpec((bm, bn), lambda j, k: (j // n_split, j % n_split))],
        out_shape=[jax.ShapeDtypeStruct((m, n), BF16)],
        scratch=[pltpu.VMEM((bm, bn), F32)],
        sem=("parallel", "arbitrary"))
    return outs[0], couts


def matmul_residual(x, y, w, name):
    s, n = x.shape
    k = y.shape[1]
    tm = _tile(s, 1024)

    def body(x_ref, y_ref, w_ref, o_ref):
        o_ref[...] = x_ref[...] + _dot(y_ref[...], w_ref[...])

    return pl.pallas_call(
        body, name=name, grid=(s // tm,),
        in_specs=[pl.BlockSpec((tm, n), lambda i: (i, 0)),
                  pl.BlockSpec((tm, k), lambda i: (i, 0)),
                  pl.BlockSpec((k, n), lambda i: (0, 0))],
        out_specs=pl.BlockSpec((tm, n), lambda i: (i, 0)),
        out_shape=jax.ShapeDtypeStruct((s, n), F32),
        compiler_params=_params("parallel"),
    )(x, y, w)


def matmul_nt(dy, wg, name):
    s, n = dy.shape
    k = wg.shape[0]
    tm = _tile(s, 1024)

    def body(dy_ref, w_ref, o_ref):
        o_ref[...] = _dot_nt(dy_ref[...].astype(BF16), w_ref[...])

    return pl.pallas_call(
        body, name=name, grid=(s // tm,),
        in_specs=[pl.BlockSpec((tm, n), lambda i: (i, 0)),
                  pl.BlockSpec((k, n), lambda i: (0, 0))],
        out_specs=pl.BlockSpec((tm, k), lambda i: (i, 0)),
        out_shape=jax.ShapeDtypeStruct((s, k), F32),
        compiler_params=_params("parallel"),
    )(dy, wg)


def matmul_nt_norm_backward(dz, wg, x, g, dres, name):
    s, d = x.shape
    n = wg.shape[-1]
    tm = _tile(s, 512)

    def body(dz_ref, w_ref, x_ref, g_ref, dres_ref, dx_ref, dg_ref):
        dx, dgp = _norm_backward(_dot_nt(dz_ref[...], w_ref[...]), x_ref[...], g_ref[...], dres_ref[...])
        dx_ref[...] = dx
        _accumulate(dg_ref, dgp, pl.program_id(0) == 0)

    return pl.pallas_call(
        body, name=name, grid=(s // tm,),
        in_specs=[pl.BlockSpec((tm, n), lambda i: (i, 0)),
                  pl.BlockSpec((d, n), lambda i: (0, 0)),
                  pl.BlockSpec((tm, d), lambda i: (i, 0)),
                  pl.BlockSpec((1, d), lambda i: (0, 0)),
                  pl.BlockSpec((tm, d), lambda i: (i, 0))],
        out_specs=[pl.BlockSpec((tm, d), lambda i: (i, 0)),
                   pl.BlockSpec((1, d), lambda i: (0, 0))],
        out_shape=[jax.ShapeDtypeStruct((s, d), F32), jax.ShapeDtypeStruct((1, d), F32)],
        compiler_params=_params("arbitrary"),
    )(dz, wg, x, g, dres)


def loss_and_grad(y, target, name):
    s, d = y.shape
    tm = _tile(s, 1024)

    def body(y_ref, t_ref, dy_ref, l_ref):
        e = y_ref[...] - t_ref[...]
        dy_ref[...] = e / d
        part = jnp.sum(e * e, axis=0, keepdims=True) * (0.5 / d)

        @pl.when(pl.program_id(0) == 0)
        def _():
            l_ref[...] = part

        @pl.when(pl.program_id(0) > 0)
        def _():
            l_ref[...] += part

    return pl.pallas_call(
        body, name=name, grid=(s // tm,),
        in_specs=[pl.BlockSpec((tm, d), lambda i: (i, 0)), pl.BlockSpec((tm, d), lambda i: (i, 0))],
        out_specs=[pl.BlockSpec((tm, d), lambda i: (i, 0)), pl.BlockSpec((1, d), lambda i: (0, 0))],
        out_shape=[jax.ShapeDtypeStruct((s, d), F32), jax.ShapeDtypeStruct((1, d), F32)],
        compiler_params=_params("arbitrary"),
    )(y, target)


def _layernorm(x, g, b):
    mu = jnp.mean(x, axis=-1, keepdims=True)
    xc = x - mu
    rstd = lax.rsqrt(jnp.mean(xc * xc, axis=-1, keepdims=True) + EPS)
    xn = xc * rstd
    return xn * g + b, xn, rstd


def _layernorm_backward(dy, xn, rstd, g):
    dxn = dy * g
    return rstd * (dxn - jnp.mean(dxn, axis=-1, keepdims=True) - xn * jnp.mean(dxn * xn, axis=-1, keepdims=True))


def _group_halves(x_ref, jp, nch):
    blk = jnp.concatenate([x_ref[c * CHUNK:(c + 1) * CHUNK, jp * LANES:(jp + 1) * LANES] for c in range(nch)], axis=1)
    low = (lax.broadcasted_iota(jnp.int32, blk.shape, 1) % LANES) < HEAD_DIM
    return jnp.where(low, blk, 0.0).astype(BF16), jnp.where(low, 0.0, blk).astype(BF16)


def _spatial_apply(src_ref, w_ref, dst_ref, nch, bias_ref=None):
    for jp in range(4):
        lo, hi = _group_halves(src_ref, jp, nch)
        r = _dot(w_ref[2 * jp], lo) + _dot(w_ref[2 * jp + 1], hi)
        for c in range(nch):
            v = r[:, c * LANES:(c + 1) * LANES]
            if bias_ref is not None:
                v = v + bias_ref[:, jp * LANES:(jp + 1) * LANES]
            dst_ref[c * CHUNK:(c + 1) * CHUNK, jp * LANES:(jp + 1) * LANES] = v


def _glu(zb):
    w = zb.shape[1] // 2
    return zb[:, :w] * _sigmoid(zb[:, w:])


def _fill_padded(pad_ref, prev, cur, nxt, i, nt, tm):
    pad_ref[0:CONV_HALO, :] = jnp.where(i > 0, prev, 0.0)
    pad_ref[CONV_HALO:CONV_HALO + tm, :] = cur
    pad_ref[CONV_HALO + tm:2 * CONV_HALO + tm, :] = jnp.where(i < nt - 1, nxt, 0.0)


def _halo_specs(tm, s, width, col):
    hb, nhb = tm // CONV_HALO, s // CONV_HALO
    return [pl.BlockSpec((tm, width), lambda i: (i, col)),
            pl.BlockSpec((CONV_HALO, width), lambda i: (jnp.maximum(i * hb - 1, 0), col)),
            pl.BlockSpec((CONV_HALO, width), lambda i: (jnp.minimum((i + 1) * hb, nhb - 1), col))]


def _const_spec(shape):
    nd = len(shape)
    return pl.BlockSpec(shape, lambda i: (0,) * nd)


SUBLANES = 8


def _shift_scratch(tm, width):
    return pltpu.VMEM((SUBLANES - 1, tm + 2 * CONV_HALO - SUBLANES, width), F32)


def _fill_shifts(sh_ref, pad_ref):
    rows = sh_ref.shape[1]
    for sft in range(1, SUBLANES):
        sh_ref[sft - 1] = pad_ref[pl.ds(sft, rows), :]


def _tap(pad_ref, sh_ref, offset, cols):
    sft = offset % SUBLANES
    rows = pl.ds(offset - sft, CHUNK)
    return pad_ref[rows, cols] if sft == 0 else sh_ref[sft - 1, rows, cols]


def ab_mid_forward(z, sp, name, comm=None):
    s = z.shape[0]
    aw = z.shape[1] // 4
    tm = _tile(s, 512)
    nch, nt = tm // CHUNK, s // tm

    def body(zu_ref, zv_ref, zb_ref, zp_ref, zn_ref, w_ref, bias_ref, vg_ref, vb_ref, cw_ref, cb_ref, cg_ref, cbn_ref,
             y_ref, gc_ref, vl_ref, sv_ref, pad_ref, sh_ref):
        i = pl.program_id(0)
        vl_ref[...] = _layernorm(_gelu(zv_ref[...]), vg_ref[...], vb_ref[...])[0]
        _spatial_apply(vl_ref, w_ref, sv_ref, nch, bias_ref)
        y_ref[:, :aw] = (_gelu(zu_ref[...]) * sv_ref[...]).astype(BF16)

        _fill_padded(pad_ref, _glu(zp_ref[...]), _glu(zb_ref[...]), _glu(zn_ref[...]), i, nt, tm)
        _fill_shifts(sh_ref, pad_ref)
        for rb in range(tm // CHUNK):
            for lb in range(aw // LANES):
                cols = pl.ds(lb * LANES, LANES)
                acc = jnp.broadcast_to(cb_ref[:, cols], (CHUNK, LANES))
                for k in range(CONV_W):
                    acc = acc + cw_ref[k:k + 1, cols] * _tap(pad_ref, sh_ref, rb * CHUNK + CONV_HALO - CONV_W // 2 + k, cols)
                gc_ref[rb * CHUNK:(rb + 1) * CHUNK, cols] = acc
        yl = _layernorm(gc_ref[...], cg_ref[...], cbn_ref[...])[0]
        y_ref[:, aw:] = (yl * _sigmoid(yl)).astype(BF16)

    return _pallas(
        body, name=name, grid=(nt,), comm=comm,
        args=[z, z, z, z, z, sp["w"], sp["bias"], sp["vg"], sp["vb"], sp["cw"], sp["cb"], sp["cg"], sp["cbn"]],
        in_specs=[pl.BlockSpec((tm, aw), lambda i: (i, 0)), pl.BlockSpec((tm, aw), lambda i: (i, 1))]
        + _halo_specs(tm, s, 2 * aw, 1)
        + [_const_spec(sp["w"].shape), _const_spec(sp["bias"].shape)]
        + [_const_spec((1, aw))] * 2 + [_const_spec(sp["cw"].shape)] + [_const_spec((1, aw))] * 3,
        out_specs=[pl.BlockSpec((tm, 2 * aw), lambda i: (i, 0)), pl.BlockSpec((tm, aw), lambda i: (i, 0))],
        out_shape=[jax.ShapeDtypeStruct((s, 2 * aw), BF16), jax.ShapeDtypeStruct((s, aw), F32)],
        scratch=[pltpu.VMEM((tm, aw), F32), pltpu.VMEM((tm, aw), F32), pltpu.VMEM((tm + 2 * CONV_HALO, aw), F32),
                 _shift_scratch(tm, aw)],
        sem=("parallel",))


def _accumulate(ref, val, first):
    @pl.when(first)
    def _():
        ref[...] = val

    @pl.when(jnp.logical_not(first))
    def _():
        ref[...] += val


def ab_mid_backward(dy, z, gc, sp, name):
    s = z.shape[0]
    aw = z.shape[1] // 4
    tm = _tile(s, 512)
    nch, nt = tm // CHUNK, s // tm

    def body(dya_ref, dyb_ref, zu_ref, zv_ref, gc_ref, w_ref, wt_ref, bias_ref, vg_ref, vb_ref, cg_ref, cbn_ref,
             dz_ref, dgc_ref, dw_ref, dsb_ref, dvg_ref, dvb_ref, dcg_ref, dcbn_ref, dcb_ref,
             vl_ref, sv_ref, dsv_ref, dvl_ref):
        first = pl.program_id(0) == 0
        zu, zv = zu_ref[...], zv_ref[...]
        u = _gelu(zu)
        vl, vn, vrstd = _layernorm(_gelu(zv), vg_ref[...], vb_ref[...])
        vl_ref[...] = vl
        _spatial_apply(vl_ref, w_ref, sv_ref, nch, bias_ref)
        dya = dya_ref[...]
        dz_ref[:, :aw] = (dya * sv_ref[...] * _gelu_grad(zu)).astype(BF16)
        dsv = dya * u
        dsv_ref[...] = dsv
        _spatial_apply(dsv_ref, wt_ref, dvl_ref, nch)

        for jp in range(4):
            dlo, dhi = _group_halves(dsv_ref, jp, nch)
            vlo, vhi = _group_halves(vl_ref, jp, nch)
            vall = vlo + vhi
            _accumulate(dw_ref.at[2 * jp], _dot_nt(dlo, vall), first)
            _accumulate(dw_ref.at[2 * jp + 1], _dot_nt(dhi, vall), first)
        rows = dsv[0:CHUNK]
        for c in range(1, nch):
            rows = rows + dsv[c * CHUNK:(c + 1) * CHUNK]
        grp = lax.broadcasted_iota(jnp.int32, (8, aw), 0) == lax.broadcasted_iota(jnp.int32, (8, aw), 1) // HEAD_DIM
        e = jnp.where(grp, 1.0, 0.0).astype(BF16)
        hi = rows.astype(BF16)
        r1 = rows - hi.astype(F32)
        mid = r1.astype(BF16)
        lo = (r1 - mid.astype(F32)).astype(BF16)
        _accumulate(dsb_ref, _dot_nt(e, hi) + _dot_nt(e, mid) + _dot_nt(e, lo), first)

        dvl = dvl_ref[...]
        _accumulate(dvg_ref, jnp.sum(dvl * vn, axis=0, keepdims=True), first)
        _accumulate(dvb_ref, jnp.sum(dvl, axis=0, keepdims=True), first)
        dz_ref[:, aw:] = (_layernorm_backward(dvl, vn, vrstd, vg_ref[...]) * _gelu_grad(zv)).astype(BF16)

        yl, yn, yrstd = _layernorm(gc_ref[...], cg_ref[...], cbn_ref[...])
        sg = _sigmoid(yl)
        dyl = dyb_ref[...] * (sg + yl * sg * (1.0 - sg))
        _accumulate(dcg_ref, jnp.sum(dyl * yn, axis=0, keepdims=True), first)
        _accumulate(dcbn_ref, jnp.sum(dyl, axis=0, keepdims=True), first)
        dgc = _layernorm_backward(dyl, yn, yrstd, cg_ref[...])
        dgc_ref[...] = dgc
        _accumulate(dcb_ref, jnp.sum(dgc, axis=0, keepdims=True), first)

    vec = jax.ShapeDtypeStruct((1, aw), F32)
    return pl.pallas_call(
        body, name=name, grid=(nt,),
        in_specs=[pl.BlockSpec((tm, aw), lambda i: (i, 0)), pl.BlockSpec((tm, aw), lambda i: (i, 1)),
                  pl.BlockSpec((tm, aw), lambda i: (i, 0)), pl.BlockSpec((tm, aw), lambda i: (i, 1)),
                  pl.BlockSpec((tm, aw), lambda i: (i, 0)),
                  _const_spec(sp["w"].shape), _const_spec(sp["w"].shape), _const_spec(sp["bias"].shape)]
        + [_const_spec((1, aw))] * 4,
        out_specs=[pl.BlockSpec((tm, 2 * aw), lambda i: (i, 0)), pl.BlockSpec((tm, aw), lambda i: (i, 0)),
                   _const_spec(sp["w"].shape), _const_spec((8, CHUNK))] + [_const_spec((1, aw))] * 5,
        out_shape=[jax.ShapeDtypeStruct((s, 4 * aw), BF16), jax.ShapeDtypeStruct((s, aw), F32),
                   jax.ShapeDtypeStruct(sp["w"].shape, F32), jax.ShapeDtypeStruct((8, CHUNK), F32)] + [vec] * 5,
        scratch_shapes=[pltpu.VMEM((tm, aw), F32)] * 4,
        compiler_params=_params("arbitrary"),
    )(dy, dy, z, z, gc, sp["w"], sp["wt"], sp["bias"], sp["vg"], sp["vb"], sp["cg"], sp["cbn"])


def conv_backward(dgc, z, dz_in, sp, name, comm=None):
    s = z.shape[0]
    aw = z.shape[1] // 4
    tm = _tile(s, 512)
    nt = s // tm
    off = CONV_HALO - CONV_W // 2

    def body(d_ref, dp_ref, dn_ref, zb_ref, zp_ref, zn_ref, cw_ref, dzin_ref, dz_ref, dcw_ref,
             padd_ref, padg_ref, dgg_ref, shd_ref, shg_ref):
        i = pl.program_id(0)
        _fill_padded(padd_ref, dp_ref[...], d_ref[...], dn_ref[...], i, nt, tm)
        _fill_padded(padg_ref, _glu(zp_ref[...]), _glu(zb_ref[...]), _glu(zn_ref[...]), i, nt, tm)
        _fill_shifts(shd_ref, padd_ref)
        _fill_shifts(shg_ref, padg_ref)

        @pl.when(i == 0)
        def _():
            dcw_ref[...] = jnp.zeros_like(dcw_ref)

        def grad_input(cols, rb):
            acc = jnp.zeros((CHUNK, LANES), F32)
            for k in range(CONV_W):
                acc = acc + cw_ref[k:k + 1, cols] * _tap(padd_ref, shd_ref, rb * CHUNK + CONV_HALO + CONV_W // 2 - k, cols)
            dgg_ref[rb * CHUNK:(rb + 1) * CHUNK, cols] = acc

        def grad_taps(cols, rb):
            dblk = d_ref[rb * CHUNK:(rb + 1) * CHUNK, cols]
            for k in range(CONV_W):
                prod = dblk * _tap(padg_ref, shg_ref, rb * CHUNK + off + k, cols)
                dcw_ref[k:k + 1, cols] += jnp.sum(prod, axis=0, keepdims=True)

        for lb in range(aw // LANES):
            for rb in range(tm // CHUNK):
                pl.when(i >= 0)(functools.partial(grad_input, pl.ds(lb * LANES, LANES), rb))
                pl.when(i >= 0)(functools.partial(grad_taps, pl.ds(lb * LANES, LANES), rb))

        zb = zb_ref[...]
        val, sg = zb[:, :aw], _sigmoid(zb[:, aw:])
        dgg = dgg_ref[...]
        dz_ref[:, :aw] = (dgg * sg).astype(BF16)
        dz_ref[:, aw:] = (dgg * val * sg * (1.0 - sg)).astype(BF16)

    return _pallas(
        body, name=name, grid=(nt,), args=[dgc, dgc, dgc, z, z, z, sp["cw"], dz_in], comm=comm,
        in_specs=_halo_specs(tm, s, aw, 0) + _halo_specs(tm, s, 2 * aw, 1)
        + [_const_spec(sp["cw"].shape), pl.BlockSpec(memory_space=pl.ANY)],
        out_specs=[pl.BlockSpec((tm, 2 * aw), lambda i: (i, 1)), _const_spec(sp["cw"].shape)],
        out_shape=[jax.ShapeDtypeStruct((s, 4 * aw), BF16), jax.ShapeDtypeStruct(sp["cw"].shape, F32)],
        scratch=[pltpu.VMEM((tm + 2 * CONV_HALO, aw), F32)] * 2 + [pltpu.VMEM((tm, aw), F32)]
        + [_shift_scratch(tm, aw)] * 2,
        aliases={7: 0}, sem=("arbitrary",))


def rope_tables(s):
    pos = jnp.arange(s, dtype=F32)
    inv_freq = ROPE_THETA ** (-jnp.arange(0, ROT_DIM, 2, dtype=F32) / ROT_DIM)
    ang = pos[:, None] * inv_freq[None, :]
    cos, sin = jnp.cos(ang), jnp.sin(ang)
    half = ROT_DIM // 2
    rest = HEAD_DIM - ROT_DIM
    one, zero, zrest = jnp.ones((s, rest), F32), jnp.zeros((s, half), F32), jnp.zeros((s, rest), F32)
    c = jnp.concatenate([cos, cos, one], axis=1)
    s1 = jnp.concatenate([-sin, zero, zrest], axis=1)
    s2 = jnp.concatenate([zero, sin, zrest], axis=1)
    return tuple(jnp.tile(t, (1, LANES // HEAD_DIM)) for t in (c, s1, s2))


def qk_prep_forward(qkv, tabs, gains, name, comm=None):
    s, w3 = qkv.shape
    w = w3 // 3
    tm = _tile(s, 512)

    def body(x_ref, c_ref, s1_ref, s2_ref, g_ref, o_ref):
        bd = _block_diag(LANES)
        for rb in range(tm // CHUNK):
            rows = pl.ds(rb * CHUNK, CHUNK)
            c, s1, s2 = c_ref[rows, :], s1_ref[rows, :], s2_ref[rows, :]
            for b in range(w // LANES):
                cols = pl.ds(b * LANES, LANES)
                t = x_ref[rows, cols]
                r = lax.rsqrt(_seg_sum(t * t, bd) * (1.0 / HEAD_DIM) + EPS)
                y = t * r * g_ref[...]
                o_ref[rows, cols] = (y * c + pltpu.roll(y, LANES - ROT_DIM // 2, 1) * s1
                                     + pltpu.roll(y, ROT_DIM // 2, 1) * s2)

    tab = pl.BlockSpec((tm, LANES), lambda i, p: (i, 0))
    outs, couts = _pallas(
        body, name=name, grid=(s // tm, 2), args=[qkv, *tabs, gains], comm=comm,
        in_specs=[pl.BlockSpec((tm, w), lambda i, p: (i, p)), tab, tab, tab,
                  pl.BlockSpec((None, 1, LANES), lambda i, p: (p, 0, 0))],
        out_specs=[pl.BlockSpec((tm, w), lambda i, p: (i, p))],
        out_shape=[jax.ShapeDtypeStruct((s, 2 * w), F32)],
        sem=("parallel", "arbitrary"))
    return outs[0], couts


def qk_prep_backward(dq, dq_prev, dq_next, dk, dv, qkv, tabs, gains, name):
    s, w3 = qkv.shape
    w = w3 // 3
    tm = _tile(s, 512)
    t = _tile(s, ATT_TILE)
    per_tile, per_halo, tiles = t // tm, ATT_HALO // tm, s // t

    def neighbours(i):
        tile, c = i // per_tile, i % per_tile
        from_before = (c < per_halo) & (tile >= 1)
        from_after = (c >= per_tile - per_halo) & (tile + 1 < tiles)
        return (from_before, from_after, jnp.where(from_before, (tile - 1) * per_halo + c, 0),
                jnp.where(from_after, (tile + 1) * per_halo + c - (per_tile - per_halo), 0))

    def body(*refs):
        grads = ((refs[0], refs[1], refs[2]), (refs[3],), (refs[4],))
        x_ref, c_ref, s1_ref, s2_ref, g_ref, o_ref, dg_ref = refs[5:]
        part, first = pl.program_id(0), pl.program_id(1) == 0
        from_before, from_after, _, _ = neighbours(pl.program_id(1))

        def normed(ds):
            bd = _block_diag(LANES)
            acc = jnp.zeros((1, LANES), F32)
            for rb in range(tm // CHUNK):
                rows = pl.ds(rb * CHUNK, CHUNK)
                c, s1, s2 = c_ref[rows, :], s1_ref[rows, :], s2_ref[rows, :]
                for b in range(w // LANES):
                    cols = pl.ds(b * LANES, LANES)
                    dout = ds[0][rows, cols]
                    if len(ds) == 3:
                        dout = (dout + jnp.where(from_after, ds[1][rows, cols], 0.0)
                                + jnp.where(from_before, ds[2][rows, cols], 0.0))
                    dy = (dout * c + pltpu.roll(dout * s1, ROT_DIM // 2, 1)
                          + pltpu.roll(dout * s2, LANES - ROT_DIM // 2, 1))
                    t = x_ref[rows, cols]
                    r = lax.rsqrt(_seg_sum(t * t, bd) * (1.0 / HEAD_DIM) + EPS)
                    xh = t * r
                    acc = acc + jnp.sum(dy * xh, axis=0, keepdims=True)
                    tt = dy * g_ref[...]
                    o_ref[rows, cols] = (r * (tt - xh * (_seg_sum(tt * xh, bd) * (1.0 / HEAD_DIM)))).astype(BF16)
            _accumulate(dg_ref, acc, first)

        for p in range(2):
            pl.when(part == p)(functools.partial(normed, grads[p]))

        @pl.when(part == 2)
        def _():
            o_ref[...] = grads[2][0][...].astype(BF16)
            _accumulate(dg_ref, jnp.zeros((1, LANES), F32), first)

    def gspec(p):
        return pl.BlockSpec((tm, w), lambda q, i: (jnp.where(q == p, i, 0), 0))

    prev_spec = pl.BlockSpec((tm, w), lambda q, i: (jnp.where(q == 0, neighbours(i)[3], 0), 0))
    next_spec = pl.BlockSpec((tm, w), lambda q, i: (jnp.where(q == 0, neighbours(i)[2], 0), 0))
    tab = pl.BlockSpec((tm, LANES), lambda q, i: (i, 0))
    return pl.pallas_call(
        body, name=name, grid=(3, s // tm),
        in_specs=[gspec(0), prev_spec, next_spec, gspec(1), gspec(2)]
        + [pl.BlockSpec((tm, w), lambda q, i: (i, q)), tab, tab, tab,
           pl.BlockSpec((None, 1, LANES), lambda q, i: (q, 0, 0))],
        out_specs=[pl.BlockSpec((tm, w), lambda q, i: (i, q)), pl.BlockSpec((None, 1, LANES), lambda q, i: (q, 0, 0))],
        out_shape=[jax.ShapeDtypeStruct((s, w3), BF16), jax.ShapeDtypeStruct((3, 1, LANES), F32)],
        compiler_params=_params("arbitrary", "arbitrary"),
    )(dq, dq_prev, dq_next, dk, dv, qkv, *tabs, gains)


def _window_specs(tq, l, col_fn):
    hb, nhb = tq // BAND, l // BAND
    return [pl.BlockSpec((tq, LANES), lambda c, i: (i, col_fn(c))),
            pl.BlockSpec((BAND, LANES), lambda c, i: (jnp.maximum(i * hb - 1, 0), col_fn(c))),
            pl.BlockSpec((BAND, LANES), lambda c, i: (jnp.minimum((i + 1) * hb, nhb - 1), col_fn(c)))]


def _window(cur_ref, prev_ref, next_ref):
    return jnp.concatenate([prev_ref[...], cur_ref[...], next_ref[...]], axis=0)


def _band_mask(shape, centre_axis, first_row, length):
    ctr = lax.broadcasted_iota(jnp.int32, shape, centre_axis)
    win = lax.broadcasted_iota(jnp.int32, shape, 1 - centre_axis)
    row = first_row + win
    return (jnp.abs(win - BAND - ctr) <= BAND) & (row >= 0) & (row < length)


def _col_q(d):
    return lambda c: (c // 8) * 24 + c % 8


def _col_k(d):
    return lambda c: (c // 8) * 24 + 8 + c % 8


def _col_v(d):
    return lambda c: (c // 8) * 24 + 16 + c % 8


def band_attention_forward(qkvn, d, name):
    s, w3 = qkvn.shape
    w = w3 // 3
    l = s // d
    tq = _tile(l, 512)
    scale = HEAD_DIM ** -0.5
    xv = qkvn.reshape(l, d * w3)

    def body(q_ref, k_ref, kp_ref, kn_ref, v_ref, vp_ref, vn_ref, o_ref, lse_ref):
        i = pl.program_id(1)
        kw, vw = _window(k_ref, kp_ref, kn_ref), _window(v_ref, vp_ref, vn_ref)
        head0 = lax.broadcasted_iota(jnp.int32, (CHUNK, LANES), 1) < HEAD_DIM
        for b in range(tq // CHUNK):
            rows = pl.ds(b * CHUNK, CHUNK)
            mask = _band_mask((CHUNK, 2 * CHUNK), 0, i * tq + b * CHUNK - BAND, l)
            qb = q_ref[rows, :]
            kb, vb = kw[b * CHUNK:(b + 2) * CHUNK], vw[b * CHUNK:(b + 2) * CHUNK]
            outs, lses = [], []
            for hm in (head0, jnp.logical_not(head0)):
                sc = jnp.where(mask, _dot_nt(jnp.where(hm, qb, jnp.zeros_like(qb)), kb) * scale, NEG)
                m = jnp.max(sc, axis=1, keepdims=True)
                p = jnp.exp(sc - m)
                den = jnp.sum(p, axis=1, keepdims=True)
                outs.append(_dot(p.astype(BF16), vb) / den)
                lses.append(jnp.broadcast_to(m + jnp.log(den), (CHUNK, LANES)))
            o_ref[rows, :] = jnp.where(head0, outs[0], outs[1]).astype(BF16)
            lse_ref[rows, :] = jnp.where(head0, lses[0], lses[1])

    ospec = pl.BlockSpec((tq, LANES), lambda c, i: (i, c))
    o, lse = pl.pallas_call(
        body, name=name, grid=(d * w // LANES, l // tq),
        in_specs=[pl.BlockSpec((tq, LANES), lambda c, i: (i, _col_q(d)(c)))]
        + _window_specs(tq, l, _col_k(d)) + _window_specs(tq, l, _col_v(d)),
        out_specs=[ospec, ospec],
        out_shape=[jax.ShapeDtypeStruct((l, d * w), BF16), jax.ShapeDtypeStruct((l, d * w), F32)],
        compiler_params=_params("parallel", "parallel"),
    )(xv, xv, xv, xv, xv, xv, xv)
    return o.reshape(s, w), lse.reshape(s, w)


def attention_merge(os_, lses, name):
    s, w = os_[0].shape
    tm = _tile(s, 512)

    def body(o0, o1, o2, l0, l1, l2, o_ref, lse_ref):
        la, lb, lc = l0[...], l1[...], l2[...]
        m = jnp.maximum(jnp.maximum(la, lb), lc)
        wa, wb, wc = jnp.exp(la - m), jnp.exp(lb - m), jnp.exp(lc - m)
        den = wa + wb + wc
        o = (wa * o0[...].astype(F32) + wb * o1[...].astype(F32) + wc * o2[...].astype(F32)) / den
        o_ref[...] = o.astype(BF16)
        lse_ref[...] = m + jnp.log(den)

    spec = pl.BlockSpec((tm, w), lambda i: (i, 0))
    return pl.pallas_call(
        body, name=name, grid=(s // tm,), in_specs=[spec] * 6, out_specs=[spec, spec],
        out_shape=[jax.ShapeDtypeStruct((s, w), BF16), jax.ShapeDtypeStruct((s, w), F32)],
        compiler_params=_params("parallel"),
    )(*os_, *lses)


def attention_delta(do, o, name):
    s, w = do.shape
    tm = _tile(s, 512)

    def body(do_ref, o_ref, dl_ref, dob_ref):
        bd = _block_diag(LANES)
        for b in range(w // LANES):
            cols = pl.ds(b * LANES, LANES)
            dv = do_ref[:, cols]
            dl_ref[:, cols] = _seg_sum(dv * o_ref[:, cols].astype(F32), bd)
            dob_ref[:, cols] = dv.astype(BF16)

    spec = pl.BlockSpec((tm, w), lambda i: (i, 0))
    return pl.pallas_call(
        body, name=name, grid=(s // tm,), in_specs=[spec, spec], out_specs=[spec, spec],
        out_shape=[jax.ShapeDtypeStruct((s, w), F32), jax.ShapeDtypeStruct((s, w), BF16)],
        compiler_params=_params("parallel"),
    )(do, o)


def band_attention_dq(qkvn, dob, lse, delta, d, name):
    s, w3 = qkvn.shape
    w = w3 // 3
    l = s // d
    tq = _tile(l, 512)
    scale = HEAD_DIM ** -0.5
    xv = qkvn.reshape(l, d * w3)

    def body(q_ref, k_ref, kp_ref, kn_ref, v_ref, vp_ref, vn_ref, do_ref, lse_ref, dl_ref, dq_ref):
        i = pl.program_id(1)
        kw, vw = _window(k_ref, kp_ref, kn_ref), _window(v_ref, vp_ref, vn_ref)
        head0 = lax.broadcasted_iota(jnp.int32, (CHUNK, LANES), 1) < HEAD_DIM
        for b in range(tq // CHUNK):
            rows = pl.ds(b * CHUNK, CHUNK)
            mask = _band_mask((CHUNK, 2 * CHUNK), 0, i * tq + b * CHUNK - BAND, l)
            qb, dob_ = q_ref[rows, :], do_ref[rows, :]
            kb, vb = kw[b * CHUNK:(b + 2) * CHUNK], vw[b * CHUNK:(b + 2) * CHUNK]
            outs = []
            for h, hm in enumerate((head0, jnp.logical_not(head0))):
                col = pl.ds(h * HEAD_DIM, 1)
                sc = jnp.where(mask, _dot_nt(jnp.where(hm, qb, jnp.zeros_like(qb)), kb) * scale, NEG)
                p = jnp.exp(sc - lse_ref[rows, col])
                dp = _dot_nt(jnp.where(hm, dob_, jnp.zeros_like(dob_)), vb)
                ds = p * (dp - dl_ref[rows, col]) * scale
                outs.append(_dot(ds.astype(BF16), kb))
            dq_ref[rows, :] = jnp.where(head0, outs[0], outs[1])

    ospec = pl.BlockSpec((tq, LANES), lambda c, i: (i, c))
    dq = pl.pallas_call(
        body, name=name, grid=(d * w // LANES, l // tq),
        in_specs=[pl.BlockSpec((tq, LANES), lambda c, i: (i, _col_q(d)(c)))]
        + _window_specs(tq, l, _col_k(d)) + _window_specs(tq, l, _col_v(d)) + [ospec, ospec, ospec],
        out_specs=ospec,
        out_shape=jax.ShapeDtypeStruct((l, d * w), F32),
        compiler_params=_params("parallel", "parallel"),
    )(xv, xv, xv, xv, xv, xv, xv, dob.reshape(l, d * w), lse.reshape(l, d * w), delta.reshape(l, d * w))
    return dq.reshape(s, w)


def band_attention_dkv(qkvn, dob, lse, delta, d, name):
    s, w3 = qkvn.shape
    w = w3 // 3
    l = s // d
    tq = _tile(l, 512)
    scale = HEAD_DIM ** -0.5
    xv = qkvn.reshape(l, d * w3)

    def body(k_ref, v_ref, q_ref, qp_ref, qn_ref, do_ref, dop_ref, don_ref, lse_ref, lsep_ref, lsen_ref,
             dl_ref, dlp_ref, dln_ref, dk_ref, dv_ref):
        i = pl.program_id(1)
        qw, dow = _window(q_ref, qp_ref, qn_ref), _window(do_ref, dop_ref, don_ref)
        lsew, dlw = _window(lse_ref, lsep_ref, lsen_ref), _window(dl_ref, dlp_ref, dln_ref)
        head0 = lax.broadcasted_iota(jnp.int32, (2 * CHUNK, LANES), 1) < HEAD_DIM
        for b in range(tq // CHUNK):
            rows = pl.ds(b * CHUNK, CHUNK)
            mask = _band_mask((2 * CHUNK, CHUNK), 1, i * tq + b * CHUNK - BAND, l)
            kb, vb = k_ref[rows, :], v_ref[rows, :]
            win = slice(b * CHUNK, (b + 2) * CHUNK)
            qb, dob_, lseb, dlb = qw[win], dow[win], lsew[win], dlw[win]
            dk = jnp.zeros((CHUNK, LANES), F32)
            dv = jnp.zeros((CHUNK, LANES), F32)
            for h, hm in enumerate((head0, jnp.logical_not(head0))):
                col = slice(h * HEAD_DIM, h * HEAD_DIM + 1)
                qm = jnp.where(hm, qb, jnp.zeros_like(qb))
                dom = jnp.where(hm, dob_, jnp.zeros_like(dob_))
                sc = jnp.where(mask, _dot_nt(qm, kb) * scale, NEG)
                p = jnp.exp(sc - lseb[:, col])
                ds = p * (_dot_nt(dom, vb) - dlb[:, col]) * scale
                dv = dv + _dot_tn(p.astype(BF16), dom)
                dk = dk + _dot_tn(ds.astype(BF16), qm)
            dk_ref[rows, :] = dk
            dv_ref[rows, :] = dv

    ident = lambda c: c
    ospec = pl.BlockSpec((tq, LANES), lambda c, i: (i, c))
    dk, dv = pl.pallas_call(
        body, name=name, grid=(d * w // LANES, l // tq),
        in_specs=[pl.BlockSpec((tq, LANES), lambda c, i: (i, _col_k(d)(c))),
                  pl.BlockSpec((tq, LANES), lambda c, i: (i, _col_v(d)(c)))]
        + _window_specs(tq, l, _col_q(d)) + _window_specs(tq, l, ident) * 3,
        out_specs=[ospec, ospec],
        out_shape=[jax.ShapeDtypeStruct((l, d * w), F32)] * 2,
        compiler_params=_params("parallel", "parallel"),
    )(xv, xv, xv, xv, xv, *[dob.reshape(l, d * w)] * 3, *[lse.reshape(l, d * w)] * 3, *[delta.reshape(l, d * w)] * 3)
    return dk.reshape(s, w), dv.reshape(s, w)


ATT_TILE = 2048
ATT_HALO = BAND * max(DILATIONS)
ROWS_PER_COPY = 256


def _att_specs(s, t, col_fn, halo=True):
    hb, nhb = t // ATT_HALO, s // ATT_HALO
    specs = [pl.BlockSpec((t, LANES), lambda hp, i: (i, col_fn(hp)))]
    if halo:
        specs += [pl.BlockSpec((ATT_HALO, LANES), lambda hp, i: (jnp.maximum(i * hb - 1, 0), col_fn(hp))),
                  pl.BlockSpec((ATT_HALO, LANES), lambda hp, i: (jnp.minimum((i + 1) * hb, nhb - 1), col_fn(hp)))]
    return specs


def _gather_rows(dst_ref, dst_row, src_ref, start, count, stride, scale=None):
    for c in range(0, count, ROWS_PER_COPY):
        m = min(ROWS_PER_COPY, count - c)
        v = src_ref[pl.ds(start + c * stride, m, stride=stride), :]
        if scale is not None:
            v = v * scale
        dst_ref[dst_row + c:dst_row + c + m, :] = v.astype(dst_ref.dtype)


def _stage(dst_ref, cur_ref, d, t, scale=None):
    n = t // d
    for r in range(d):
        _gather_rows(dst_ref, r * n, cur_ref, r, n, d, scale)


def _stage_window(dst_ref, refs, d, t, scale=None, edges=None):
    cur_ref, prev_ref, next_ref = refs
    n = t // d
    nw = n + 2 * BAND
    for r in range(d):
        _gather_rows(dst_ref, r * nw, prev_ref, ATT_HALO - BAND * d + r, BAND, d, scale)
        _gather_rows(dst_ref, r * nw + BAND, cur_ref, r, n, d, scale)
        _gather_rows(dst_ref, r * nw + BAND + n, next_ref, r, BAND, d, scale)
    if edges is not None:
        first, last, value = edges
        fill = jnp.full((BAND, LANES), value, dst_ref.dtype)

        @pl.when(first)
        def _():
            for r in range(d):
                dst_ref[r * nw:r * nw + BAND, :] = fill

        @pl.when(last)
        def _():
            for r in range(d):
                dst_ref[r * nw + BAND + n:(r + 1) * nw, :] = fill


def _band_bias(rows, cols, centre_axis):
    shape = (rows // 2, cols)
    ctr = lax.broadcasted_iota(jnp.int32, shape, centre_axis)
    win = lax.broadcasted_iota(jnp.int32, shape, 1 - centre_axis)
    bias = jnp.where(jnp.abs(win - BAND - ctr) <= BAND, 0.0, NEG).astype(F32)
    return jnp.concatenate([bias, bias], axis=0)


def _window_bias(first_row, length):
    row = first_row + lax.broadcasted_iota(jnp.int32, (1, 2 * CHUNK), 1)
    return jnp.where((row >= 0) & (row < length), 0.0, NEG).astype(F32)


UNITS_PER_TRIP = 8


def _scatter_rows(dst_ref, src_ref, d, t, combine):
    n = t // d
    for r in range(d):
        for c in range(0, n, ROWS_PER_COPY):
            m = min(ROWS_PER_COPY, n - c)
            idx = pl.ds(r + c * d, m, stride=d)
            combine(idx, slice(r * n + c, r * n + c + m))


def _unit_rows(u, n):
    upr = n // CHUNK
    r = u // upr
    b = u - r * upr
    return pl.multiple_of(u * CHUNK, CHUNK), pl.multiple_of((u + r) * CHUNK, CHUNK), b * CHUNK - BAND


def _two_heads(x):
    head0 = lax.broadcasted_iota(jnp.int32, x.shape, 1) < HEAD_DIM
    zero = jnp.zeros_like(x)
    return jnp.concatenate([jnp.where(head0, x, zero), jnp.where(head0, zero, x)], axis=0)


def _head_columns(x):
    return jnp.concatenate([x[:, 0:1], x[:, HEAD_DIM:HEAD_DIM + 1]], axis=0)


def _merge_heads(x2):
    rows = x2.shape[0] // 2
    head0 = lax.broadcasted_iota(jnp.int32, (rows, LANES), 1) < HEAD_DIM
    return jnp.where(head0, jnp.broadcast_to(x2[:rows], (rows, LANES)), jnp.broadcast_to(x2[rows:], (rows, LANES)))


def _col(part):
    return lambda hp: part * 8 + hp


def attention_forward(qk, qkv, name, comm=None):
    s, w2 = qk.shape
    w = w2 // 2
    t = _tile(s, ATT_TILE)
    scale = HEAD_DIM ** -0.5

    def body(q_ref, k_ref, kp_ref, kn_ref, v_ref, vp_ref, vn_ref, o_ref, lse_ref, qs_ref, ks_ref, vs_ref, os_ref, ls_ref, or_ref):
        i = pl.program_id(1)
        band = _band_bias(2 * CHUNK, 2 * CHUNK, 0)
        for pi, d in enumerate(DILATIONS):
            n = t // d
            _stage(qs_ref, q_ref, d, t, scale)
            _stage_window(ks_ref, (k_ref, kp_ref, kn_ref), d, t)
            _stage_window(vs_ref, (v_ref, vp_ref, vn_ref), d, t)

            def unit(u, carry, n=n, d=d):
                qrow, wrow, first = _unit_rows(u, n)
                kb, vb = ks_ref[pl.ds(wrow, 2 * CHUNK), :], vs_ref[pl.ds(wrow, 2 * CHUNK), :]
                sc = _dot_nt(_two_heads(qs_ref[pl.ds(qrow, CHUNK), :]), kb) + band + _window_bias(i * n + first, s // d)
                m = jnp.max(sc, axis=1, keepdims=True)
                p = jnp.exp(sc - m)
                den = jnp.sum(p, axis=1, keepdims=True)
                os_ref[pl.ds(qrow, CHUNK), :] = _merge_heads(_dot(p.astype(BF16), vb) / den)
                ls_ref[pl.ds(qrow, CHUNK), :] = _merge_heads(m + jnp.log(den))
                return carry

            lax.fori_loop(0, t // CHUNK, unit, 0, unroll=UNITS_PER_TRIP)

            if pi == 0:
                def assign(idx, rows):
                    or_ref[idx, :] = os_ref[rows, :]
                    lse_ref[idx, :] = ls_ref[rows, :]
                _scatter_rows(None, None, d, t, assign)
            else:
                def merge(idx, rows):
                    la, lb = lse_ref[idx, :], ls_ref[rows, :]
                    mx = jnp.maximum(la, lb)
                    wa, wb = jnp.exp(la - mx), jnp.exp(lb - mx)
                    den = wa + wb
                    or_ref[idx, :] = (wa * or_ref[idx, :] + wb * os_ref[rows, :]) / den
                    lse_ref[idx, :] = mx + jnp.log(den)
                _scatter_rows(None, None, d, t, merge)
        o_ref[...] = or_ref[...].astype(BF16)

    ospec = pl.BlockSpec((t, LANES), lambda hp, i: (i, hp))
    win_rows = t + 2 * ATT_HALO
    return _pallas(
        body, name=name, grid=(w // LANES, s // t), args=[qk, qk, qk, qk, qkv, qkv, qkv], comm=comm,
        in_specs=_att_specs(s, t, _col(0), halo=False) + _att_specs(s, t, _col(1)) + _att_specs(s, t, _col(2)),
        out_specs=[ospec, ospec],
        out_shape=[jax.ShapeDtypeStruct((s, w), BF16), jax.ShapeDtypeStruct((s, w), F32)],
        scratch=[pltpu.VMEM((t, LANES), BF16), pltpu.VMEM((win_rows, LANES), BF16), pltpu.VMEM((win_rows, LANES), BF16),
                 pltpu.VMEM((t, LANES), F32), pltpu.VMEM((t, LANES), F32), pltpu.VMEM((t, LANES), F32)],
        sem=("parallel", "parallel"))


def attention_delta(do, o, lse, name):
    s, w = do.shape
    tm = _tile(s, 512)

    def body(do_ref, o_ref, lse_ref, st_ref):
        bd = _block_diag(LANES)
        lane = lax.broadcasted_iota(jnp.int32, (CHUNK, LANES), 1)
        for rb in range(tm // CHUNK):
            rows = pl.ds(rb * CHUNK, CHUNK)
            for b in range(w // LANES):
                cols = pl.ds(b * LANES, LANES)
                dl = _seg_sum(do_ref[rows, cols] * o_ref[rows, cols].astype(F32), bd)
                ls = lse_ref[rows, cols]
                st_ref[rows, cols] = jnp.where(lane == 0, ls, jnp.where(
                    lane == 1, pltpu.roll(ls, HEAD_DIM - 1, 1), jnp.where(
                        lane == 2, pltpu.roll(dl, 2, 1), pltpu.roll(dl, HEAD_DIM + 3, 1))))

    spec = pl.BlockSpec((tm, w), lambda i: (i, 0))
    return pl.pallas_call(
        body, name=name, grid=(s // tm,), in_specs=[spec, spec, spec], out_specs=spec,
        out_shape=jax.ShapeDtypeStruct((s, w), F32), compiler_params=_params("parallel"),
    )(do, o, lse)


def attention_dq(qk, qkv, do, lse, delta, name, comm=None):
    s, w2 = qk.shape
    w = w2 // 2
    t = _tile(s, ATT_TILE)
    scale = HEAD_DIM ** -0.5

    def body(q_ref, k_ref, kp_ref, kn_ref, v_ref, vp_ref, vn_ref, do_ref, lse_ref, dl_ref, dq_ref,
             qs_ref, ks_ref, vs_ref, dos_ref, ls_ref, dls_ref, dqs_ref):
        i = pl.program_id(1)
        band = _band_bias(2 * CHUNK, 2 * CHUNK, 0)
        for pi, d in enumerate(DILATIONS):
            n = t // d
            _stage(qs_ref, q_ref, d, t, scale)
            _stage(dos_ref, do_ref, d, t)
            _stage(ls_ref, lse_ref, d, t)
            _stage(dls_ref, dl_ref, d, t)
            _stage_window(ks_ref, (k_ref, kp_ref, kn_ref), d, t)
            _stage_window(vs_ref, (v_ref, vp_ref, vn_ref), d, t)

            def unit(u, carry, n=n, d=d):
                qrow, wrow, first = _unit_rows(u, n)
                rows = pl.ds(qrow, CHUNK)
                kb, vb = ks_ref[pl.ds(wrow, 2 * CHUNK), :], vs_ref[pl.ds(wrow, 2 * CHUNK), :]
                sc = _dot_nt(_two_heads(qs_ref[rows, :]), kb) + band + _window_bias(i * n + first, s // d)
                p = jnp.exp(sc - _head_columns(ls_ref[rows, :]))
                dp = _dot_nt(_two_heads(dos_ref[rows, :]), vb)
                ds = p * (dp - _head_columns(dls_ref[rows, :]))
                dqs_ref[rows, :] = _merge_heads(_dot(ds.astype(BF16), kb)) * scale
                return carry

            lax.fori_loop(0, t // CHUNK, unit, 0, unroll=UNITS_PER_TRIP)

            def add(idx, rows, pi=pi):
                dq_ref[idx, :] = dqs_ref[rows, :] if pi == 0 else dq_ref[idx, :] + dqs_ref[rows, :]
            _scatter_rows(None, None, d, t, add)

    ospec = pl.BlockSpec((t, LANES), lambda hp, i: (i, hp))
    win_rows = t + 2 * ATT_HALO
    return _pallas(
        body, name=name, grid=(w // LANES, s // t), args=[qk, qk, qk, qk, qkv, qkv, qkv, do, lse, delta], comm=comm,
        in_specs=_att_specs(s, t, _col(0), halo=False) + _att_specs(s, t, _col(1)) + _att_specs(s, t, _col(2))
        + [ospec, ospec, ospec],
        out_specs=[ospec],
        out_shape=[jax.ShapeDtypeStruct((s, w), F32)],
        scratch=[pltpu.VMEM((t, LANES), BF16), pltpu.VMEM((win_rows, LANES), BF16), pltpu.VMEM((win_rows, LANES), BF16),
                 pltpu.VMEM((t, LANES), BF16), pltpu.VMEM((t, LANES), F32), pltpu.VMEM((t, LANES), F32),
                 pltpu.VMEM((t, LANES), F32)],
        sem=("parallel", "parallel"))


def attention_dkv(qk, qkv, do, stats, name):
    s, w2 = qk.shape
    w = w2 // 2
    t = _tile(s, ATT_TILE)
    scale = HEAD_DIM ** -0.5

    def body(k_ref, v_ref, q_ref, qp_ref, qn_ref, do_ref, dop_ref, don_ref, st_ref, stp_ref, stn_ref,
             dk_ref, dv_ref, ks_ref, vs_ref, qs_ref, dos_ref, sts_ref, dks_ref, dvs_ref):
        i = pl.program_id(1)
        key = lax.broadcasted_iota(jnp.int32, (CHUNK, 2 * CHUNK), 0)
        win_ = lax.broadcasted_iota(jnp.int32, (CHUNK, 2 * CHUNK), 1)
        half = jnp.where(jnp.abs(win_ - BAND - key) <= BAND, 0.0, NEG).astype(F32)
        band = jnp.concatenate([half, half], axis=1)
        edges = (i == 0, i == s // t - 1, -NEG)
        for pi, d in enumerate(DILATIONS):
            n = t // d
            _stage(ks_ref, k_ref, d, t)
            _stage(vs_ref, v_ref, d, t)
            _stage_window(qs_ref, (q_ref, qp_ref, qn_ref), d, t, scale)
            _stage_window(dos_ref, (do_ref, dop_ref, don_ref), d, t)
            _stage_window(sts_ref, (st_ref, stp_ref, stn_ref), d, t, edges=edges)

            def unit(u, carry, n=n, d=d):
                krow, wrow, _ = _unit_rows(u, n)
                rows, win = pl.ds(krow, CHUNK), pl.ds(wrow, 2 * CHUNK)
                q2, do2 = _two_heads(qs_ref[win, :]), _two_heads(dos_ref[win, :])
                st = jnp.transpose(sts_ref[win, :])
                lse2 = jnp.concatenate([st[0:1, :], st[1:2, :]], axis=1)
                dl2 = jnp.concatenate([st[2:3, :], st[3:4, :]], axis=1)
                p = jnp.exp(_dot_nt(ks_ref[rows, :], q2) + band - lse2)
                ds = p * (_dot_nt(vs_ref[rows, :], do2) - dl2)
                dvs_ref[rows, :] = _dot(p.astype(BF16), do2)
                dks_ref[rows, :] = _dot(ds.astype(BF16), q2)
                return carry

            lax.fori_loop(0, t // CHUNK, unit, 0, unroll=UNITS_PER_TRIP)

            def add(idx, rows, pi=pi):
                dk_ref[idx, :] = dks_ref[rows, :] if pi == 0 else dk_ref[idx, :] + dks_ref[rows, :]
                dv_ref[idx, :] = dvs_ref[rows, :] if pi == 0 else dv_ref[idx, :] + dvs_ref[rows, :]
            _scatter_rows(None, None, d, t, add)

    ident = lambda hp: hp
    ospec = pl.BlockSpec((t, LANES), lambda hp, i: (i, hp))
    win_rows = t + 2 * ATT_HALO
    return pl.pallas_call(
        body, name=name, grid=(w // LANES, s // t),
        in_specs=_att_specs(s, t, _col(1), halo=False) + _att_specs(s, t, _col(2), halo=False)
        + _att_specs(s, t, _col(0)) + _att_specs(s, t, ident) * 2,
        out_specs=[ospec, ospec],
        out_shape=[jax.ShapeDtypeStruct((s, w), F32)] * 2,
        scratch_shapes=[pltpu.VMEM((t, LANES), BF16), pltpu.VMEM((t, LANES), BF16),
                        pltpu.VMEM((win_rows, LANES), BF16), pltpu.VMEM((win_rows, LANES), BF16),
                        pltpu.VMEM((win_rows, LANES), F32),
                        pltpu.VMEM((t, LANES), F32), pltpu.VMEM((t, LANES), F32)],
        compiler_params=_params("parallel", "parallel"),
    )(qk, qkv, qk, qk, qk, do, do, do, stats, stats, stats)


def attention_backward(qk, qkv, do, stats, name, comm=None):
    s, w2 = qk.shape
    w = w2 // 2
    t = _tile(s, ATT_TILE)
    tiles = s // t
    scale = HEAD_DIM ** -0.5

    def body(k_ref, v_ref, q_ref, qp_ref, qn_ref, do_ref, dop_ref, don_ref, st_ref, stp_ref, stn_ref,
             dk_ref, dv_ref, dq_ref, dqp_ref, dqn_ref, ks_ref, vs_ref, qs_ref, dos_ref, sts_ref, dks_ref, dvs_ref, dqw_ref):
        i = pl.program_id(1)
        key = lax.broadcasted_iota(jnp.int32, (CHUNK, 2 * CHUNK), 0)
        win_ = lax.broadcasted_iota(jnp.int32, (CHUNK, 2 * CHUNK), 1)
        half = jnp.where(jnp.abs(win_ - BAND - key) <= BAND, 0.0, NEG).astype(F32)
        band = jnp.concatenate([half, half], axis=1)
        edges = (i == 0, i == tiles - 1, -NEG)
        dqp_ref[...] = jnp.zeros_like(dqp_ref)
        dqn_ref[...] = jnp.zeros_like(dqn_ref)
        for pi, d in enumerate(DILATIONS):
            n = t // d
            nw = n + 2 * BAND
            _stage(ks_ref, k_ref, d, t)
            _stage(vs_ref, v_ref, d, t)
            _stage_window(qs_ref, (q_ref, qp_ref, qn_ref), d, t, scale)
            _stage_window(dos_ref, (do_ref, dop_ref, don_ref), d, t)
            _stage_window(sts_ref, (st_ref, stp_ref, stn_ref), d, t, edges=edges)
            dqw_ref[...] = jnp.zeros_like(dqw_ref)

            def unit(u, carry, n=n, d=d):
                krow, wrow, _ = _unit_rows(u, n)
                rows, win = pl.ds(krow, CHUNK), pl.ds(wrow, 2 * CHUNK)
                kb = ks_ref[rows, :]
                q2, do2 = _two_heads(qs_ref[win, :]), _two_heads(dos_ref[win, :])
                st = jnp.transpose(sts_ref[win, :])
                lse2 = jnp.concatenate([st[0:1, :], st[1:2, :]], axis=1)
                dl2 = jnp.concatenate([st[2:3, :], st[3:4, :]], axis=1)
                p = jnp.exp(_dot_nt(kb, q2) + band - lse2)
                ds = (p * (_dot_nt(vs_ref[rows, :], do2) - dl2)).astype(BF16)
                dvs_ref[rows, :] = _dot(p.astype(BF16), do2)
                dks_ref[rows, :] = _dot(ds, q2)
                k2 = _two_heads(kb)
                dqw = _dot_tn(ds[:, :2 * CHUNK], k2[:CHUNK]) + _dot_tn(ds[:, 2 * CHUNK:], k2[CHUNK:])
                dqw_ref[win, :] += dqw * scale
                return carry

            lax.fori_loop(0, t // CHUNK, unit, 0, unroll=UNITS_PER_TRIP)

            def add(idx, rows, pi=pi):
                dk_ref[idx, :] = dks_ref[rows, :] if pi == 0 else dk_ref[idx, :] + dks_ref[rows, :]
                dv_ref[idx, :] = dvs_ref[rows, :] if pi == 0 else dv_ref[idx, :] + dvs_ref[rows, :]
            _scatter_rows(None, None, d, t, add)

            for r in range(d):
                before = pl.ds(ATT_HALO - BAND * d + r, BAND, stride=d)
                after = pl.ds(r, BAND, stride=d)
                dqp_ref[before, :] += dqw_ref[r * nw:r * nw + BAND, :]
                dqn_ref[after, :] += dqw_ref[r * nw + BAND + n:(r + 1) * nw, :]
                for c in range(0, n, ROWS_PER_COPY):
                    m = min(ROWS_PER_COPY, n - c)
                    idx = pl.ds(r + c * d, m, stride=d)
                    val = dqw_ref[r * nw + BAND + c:r * nw + BAND + c + m, :]
                    dq_ref[idx, :] = val if pi == 0 else dq_ref[idx, :] + val

    ident = lambda hp: hp
    ospec = pl.BlockSpec((t, LANES), lambda hp, i: (i, hp))
    hspec = pl.BlockSpec((ATT_HALO, LANES), lambda hp, i: (i, hp))
    win_rows = t + 2 * ATT_HALO
    halo = jax.ShapeDtypeStruct((tiles * ATT_HALO, w), F32)
    return _pallas(
        body, name=name, grid=(w // LANES, tiles), comm=comm,
        args=[qk, qkv, qk, qk, qk, do, do, do, stats, stats, stats],
        in_specs=_att_specs(s, t, _col(1), halo=False) + _att_specs(s, t, _col(2), halo=False)
        + _att_specs(s, t, _col(0)) + _att_specs(s, t, ident) * 2,
        out_specs=[ospec, ospec, ospec, hspec, hspec],
        out_shape=[jax.ShapeDtypeStruct((s, w), F32)] * 3 + [halo, halo],
        scratch=[pltpu.VMEM((t, LANES), BF16), pltpu.VMEM((t, LANES), BF16),
                 pltpu.VMEM((win_rows, LANES), BF16), pltpu.VMEM((win_rows, LANES), BF16),
                 pltpu.VMEM((win_rows, LANES), F32),
                 pltpu.VMEM((t, LANES), F32), pltpu.VMEM((t, LANES), F32), pltpu.VMEM((win_rows, LANES), F32)],
        sem=("parallel", "parallel"))


def adamw_update(recvs, w, m, v, name):
    nl = len(recvs)
    r, c = recvs[0].shape[1:]
    tr = 256 if (r > 256 and r % 256 == 0) else r
    nt = r // tr
    c1 = 1.0 - ADAM_B1 ** ADAM_STEP
    c2 = 1.0 - ADAM_B2 ** ADAM_STEP

    def body(*refs):
        g_refs = refs[:nl]
        w_ref, m_ref, v_ref, go_ref, d_ref, mo_ref, vo_ref = refs[nl:]

        def update(g_ref):
            g = g_ref[0].astype(F32)
            for j in range(1, N_DEV):
                g = g + g_ref[j].astype(F32)
            mn = ADAM_B1 * m_ref[...] + (1.0 - ADAM_B1) * g
            vn = ADAM_B2 * v_ref[...] + (1.0 - ADAM_B2) * (g * g)
            go_ref[...] = g
            mo_ref[...] = mn
            vo_ref[...] = vn
            d_ref[...] = -ADAM_LR * ((mn / c1) / (jnp.sqrt(vn / c2) + ADAM_EPS) + ADAM_WD * w_ref[...])

        for layer in range(nl):
            pl.when(pl.program_id(0) == layer)(functools.partial(update, g_refs[layer]))

    def gspec(layer):
        return pl.BlockSpec((N_DEV, tr, c), lambda l, i: (0, jnp.where(l == layer, i, 0), 0))

    spec = pl.BlockSpec((tr, c), lambda l, i: (l * nt + i, 0))
    return pl.pallas_call(
        body, name=name, grid=(nl, nt),
        in_specs=[gspec(layer) for layer in range(nl)] + [spec, spec, spec],
        out_specs=[spec] * 4, out_shape=[jax.ShapeDtypeStruct((nl * r, c), F32)] * 4,
        compiler_params=_params("arbitrary", "arbitrary"),
    )(*recvs, w, m, v)


BIG = ("mlp_w1", "mlp_w2", "ab_w_in", "ab_w_out", "c_w_qkv", "c_w_out")
SMALL = ("mix_norm_g", "mlp_norm_g", "a_spatial_w", "a_spatial_b", "a_vnorm_g", "a_vnorm_b", "b_conv_b", "b_norm_g",
         "b_norm_b", "c_q_norm_g", "c_k_norm_g")
WEIGHTS = ("mix_norm_g", "mlp_norm_g", "mlp_w1", "mlp_w2", "ab_w_in", "a_spatial_w", "a_spatial_b", "a_vnorm_g",
           "a_vnorm_b", "b_conv_w", "b_conv_b", "b_norm_g", "b_norm_b", "ab_w_out", "c_w_qkv", "c_q_norm_g",
           "c_k_norm_g", "c_w_out")


def _mixer_params(p, conv_full, i):
    aw = p["a_vnorm_g"].shape[1]
    row = lambda t: t[i][None, :]
    return dict(
        w=p["a_spatial_w"][i].astype(BF16), wt=jnp.swapaxes(p["a_spatial_w"][i], 1, 2).astype(BF16),
        bias=jnp.repeat(p["a_spatial_b"][i].T, aw // p["a_spatial_b"].shape[1], axis=1),
        vg=row(p["a_vnorm_g"]), vb=row(p["a_vnorm_b"]), cw=jnp.pad(conv_full[i], ((0, 1), (0, 0))),
        cb=row(p["b_conv_b"]), cg=row(p["b_norm_g"]), cbn=row(p["b_norm_b"]))


def _head_gains(p, i):
    rep = LANES // HEAD_DIM
    return jnp.stack([jnp.tile(p["c_q_norm_g"][i], rep), jnp.tile(p["c_k_norm_g"][i], rep),
                      jnp.ones((LANES,), F32)])[:, None, :]


def _unused_forward_backward(x, target, p, wg, conv_full):
    s, d = x.shape
    depth = p["mix_norm_g"].shape[0]
    tabs = rope_tables(s)
    saved = []
    for l in range(depth):
        i = l // 2
        mix_g, mlp_g = p["mix_norm_g"][l][None, :], p["mlp_norm_g"][l][None, :]
        st = dict(x_in=x)
        if l % 2 == 0:
            sp = _mixer_params(p, conv_full, i)
            z, h = norm_matmul(x, mix_g, wg["ab_w_in"], i, f"ab_in_{l}")
            ycat, gc = ab_mid_forward(z, sp, f"ab_mid_{l}")
            x = matmul_residual(x, ycat, wg["ab_w_out"], i, f"ab_out_{l}")
            st.update(z=z, h=h, y=ycat, gc=gc, sp=sp)
        else:
            gains = _head_gains(p, i)
            qkv, h = norm_matmul(x, mix_g, wg["c_w_qkv"], i, f"c_qkv_{l}")
            qkvn = qk_prep_forward(qkv, tabs, gains, f"c_prep_{l}")
            outs = [band_attention_forward(qkvn, dil, f"c_attn_{l}_d{dil}") for dil in DILATIONS]
            o, lse = attention_merge([t[0] for t in outs], [t[1] for t in outs], f"c_merge_{l}")
            x = matmul_residual(x, o, wg["c_w_out"], i, f"c_out_{l}")
            st.update(qkv=qkv, h=h, qkvn=qkvn, y=o, lse=lse, gains=gains)
        st["x_mid"] = x
        x, a, h2 = mlp_forward(x, mlp_g, wg["mlp_w1"], wg["mlp_w2"], l, f"mlp_{l}")
        st.update(a=a, h2=h2)
        saved.append(st)

    dy, loss_part = loss_and_grad(x, target, "loss")

    big = {n: [None] * p_len for n, p_len in (("mlp_w1", depth), ("mlp_w2", depth), ("ab_w_in", depth // 2 + depth % 2),
                                              ("ab_w_out", depth // 2 + depth % 2), ("c_w_qkv", depth // 2),
                                              ("c_w_out", depth // 2))}
    small = {n: [None] * p[n].shape[0] for n in SMALL}
    conv_grads = [None] * p["b_conv_b"].shape[0]
    for l in reversed(range(depth)):
        i, st = l // 2, saved[l]
        mix_g, mlp_g = p["mix_norm_g"][l][None, :], p["mlp_norm_g"][l][None, :]
        dxm, da, dg = mlp_backward(dy, st["a"], st["x_mid"], mlp_g, wg["mlp_w1"], wg["mlp_w2"], l, f"mlp_bwd_{l}")
        small["mlp_norm_g"][l] = dg[0]
        big["mlp_w1"][l] = matmul_tn(st["h2"], da, "cols", f"mlp_dw1_{l}")
        big["mlp_w2"][l] = matmul_tn(st["a"], dy, "rows", f"mlp_dw2_{l}", relu2=True)
        if l % 2 == 0:
            sp = st["sp"]
            dycat = matmul_nt(dxm, wg["ab_w_out"], i, f"ab_out_bwd_{l}")
            big["ab_w_out"][i] = matmul_tn(st["y"], dxm, "rows", f"ab_dwout_{l}")
            dz, dgc, dw, dsb, dvg, dvb, dcg, dcbn, dcb = ab_mid_backward(dycat, st["z"], st["gc"], sp, f"ab_mid_bwd_{l}")
            dz, dcw = conv_backward(dgc, st["z"], dz, sp, f"ab_conv_bwd_{l}")
            small["a_spatial_w"][i], small["a_spatial_b"][i] = dw, dsb
            small["a_vnorm_g"][i], small["a_vnorm_b"][i] = dvg[0], dvb[0]
            small["b_norm_g"][i], small["b_norm_b"][i], small["b_conv_b"][i] = dcg[0], dcbn[0], dcb[0]
            conv_grads[i] = dcw[:CONV_W]
            dy, dg = matmul_nt_norm_backward(dz, wg["ab_w_in"], i, st["x_in"], mix_g, dxm, f"ab_in_bwd_{l}")
            big["ab_w_in"][i] = matmul_tn(st["h"], dz, "cols", f"ab_dwin_{l}")
        else:
            do = matmul_nt(dxm, wg["c_w_out"], i, f"c_out_bwd_{l}")
            big["c_w_out"][i] = matmul_tn(st["y"], dxm, "rows", f"c_dwout_{l}")
            delta, dob = attention_delta(do, st["y"], f"c_delta_{l}")
            dqs, dks, dvs = [], [], []
            for dil in DILATIONS:
                dqs.append(band_attention_dq(st["qkvn"], dob, st["lse"], delta, dil, f"c_attn_dq_{l}_d{dil}"))
                dk, dv = band_attention_dkv(st["qkvn"], dob, st["lse"], delta, dil, f"c_attn_dkv_{l}_d{dil}")
                dks.append(dk)
                dvs.append(dv)
            dqkv, dgn = qk_prep_backward(dqs, dks, dvs, st["qkv"], tabs, st["gains"], f"c_prep_bwd_{l}")
            small["c_q_norm_g"][i] = dgn[0, 0, :HEAD_DIM] + dgn[0, 0, HEAD_DIM:]
            small["c_k_norm_g"][i] = dgn[1, 0, :HEAD_DIM] + dgn[1, 0, HEAD_DIM:]
            dy, dg = matmul_nt_norm_backward(dqkv, wg["c_w_qkv"], i, st["x_in"], mix_g, dxm, f"c_qkv_bwd_{l}")
            big["c_w_qkv"][i] = matmul_tn(st["h"], dqkv, "cols", f"c_dwqkv_{l}")
        small["mix_norm_g"][l] = dg[0]
    small = {n: jnp.stack(v) for n, v in small.items()}
    return loss_part, dy, big, small, jnp.stack(conv_grads)


def _pack(d):
    flat = jnp.concatenate([d[n].reshape(-1) for n in SMALL])
    rows = -(-flat.shape[0] // (8 * LANES)) * 8
    return jnp.pad(flat, (0, rows * LANES - flat.shape[0])).reshape(rows, LANES)


def _unpack(packed, like):
    flat, out, pos = packed.reshape(-1), {}, 0
    for n in SMALL:
        size = math.prod(like[n].shape)
        out[n] = flat[pos:pos + size].reshape(like[n].shape)
        pos += size
    return out


def _unused_kernel(x, mix_norm_g, mlp_norm_g, mlp_w1, mlp_w2, ab_w_in, a_spatial_w, a_spatial_b, a_vnorm_g, a_vnorm_b, b_conv_w, b_conv_b, b_norm_g, b_norm_b, ab_w_out, c_w_qkv, c_q_norm_g, c_k_norm_g, c_w_out, loss_target, m_mix_norm_g, m_mlp_norm_g, m_mlp_w1, m_mlp_w2, m_ab_w_in, m_a_spatial_w, m_a_spatial_b, m_a_vnorm_g, m_a_vnorm_b, m_b_conv_w, m_b_conv_b, m_b_norm_g, m_b_norm_b, m_ab_w_out, m_c_w_qkv, m_c_q_norm_g, m_c_k_norm_g, m_c_w_out, v_mix_norm_g, v_mlp_norm_g, v_mlp_w1, v_mlp_w2, v_ab_w_in, v_a_spatial_w, v_a_spatial_b, v_a_vnorm_g, v_a_vnorm_b, v_b_conv_w, v_b_conv_b, v_b_norm_g, v_b_norm_b, v_ab_w_out, v_c_w_qkv, v_c_q_norm_g, v_c_k_norm_g, v_c_w_out):
    args = dict(locals())
    w = {n: args[n] for n in WEIGHTS}
    m = {n: args["m_" + n] for n in WEIGHTS}
    v = {n: args["v_" + n] for n in WEIGHTS}

    wg = {n: all_gather(w[n].astype(BF16), "gather_" + n) for n in BIG}
    for n in ("ab_w_out", "c_w_out"):
        t = wg[n]
        wg[n] = t.reshape(t.shape[0], t.shape[1] * t.shape[2], t.shape[3])
    conv = all_gather(w["b_conv_w"], "gather_b_conv_w")
    conv_full = jnp.swapaxes(conv, 1, 2).reshape(conv.shape[0], conv.shape[2], N_DEV * conv.shape[3])

    loss_part, dx, big, small, conv_grad = forward_backward(x[0], loss_target[0], w, wg, conv_full)
    loss = lax.psum(jnp.sum(loss_part), ("x", "y", "c"))

    grads, deltas, new_m, new_v = {}, {}, {}, {}

    def update(n, recv):
        shape = w[n].shape
        flat = lambda t: t.reshape(-1, shape[-1])
        outs = adamw_update(recv.reshape((N_DEV, -1, shape[-1])), flat(w[n]), flat(m[n]), flat(v[n]), "adamw_" + n)
        grads[n], deltas[n], new_m[n], new_v[n] = (t.reshape(shape) for t in outs)

    for n in BIG:
        update(n, exchange(big[n], "exchange_" + n))
    nl, kw, cw = conv_grad.shape
    conv_parts = jnp.transpose(conv_grad.reshape(nl, kw, N_DEV, cw // N_DEV), (2, 0, 1, 3))
    update("b_conv_w", exchange([conv_parts], "exchange_b_conv_w"))

    packed = all_gather(_pack(small)[None], "gather_small_grads")[0]
    outs = adamw_update(packed, _pack(w), _pack(m), _pack(v), "adamw_small")
    for dst, t in zip((grads, deltas, new_m, new_v), outs):
        dst.update(_unpack(t, w))

    return (loss, dx[None], *[grads[n] for n in WEIGHTS], *[deltas[n] for n in WEIGHTS],
            *[new_m[n] for n in WEIGHTS], *[new_v[n] for n in WEIGHTS])


class Traffic:
    def __init__(self, shards, full=()):
        self.shards, self.w, self.queue, self.parts = shards, dict(full), [], {}

    def run(self, fn, *args, gather=(), send=False, **kw):
        operands, flags, dest = [], [], []
        if self.shards is None:
            if send:
                self.parts.update(self.queue)
                self.queue = []
        else:
            for k in gather:
                if k not in self.w:
                    operands.append(self.shards[k])
                    flags.append(("gather", SHARD_AXIS.get(k[0])))
                    dest.append((self.w, k))
            if send:
                for k, t in self.queue:
                    operands.append(t)
                    flags.append(("scatter", SHARD_AXIS.get(k[0])))
                    dest.append((self.parts, k))
                self.queue = []
        outs, couts = fn(*args, comm=PeerCopies(operands, flags) if operands else None, **kw)
        for (table, k), t in zip(dest, couts):
            table[k] = t
        return outs

    def flush(self, name, extra=()):
        self.queue += list(extra)
        self.run(lambda comm: ([], run_copies(comm, name) if comm is not None else []), send=True)


SHARD_AXIS = {"mlp_w1": 1, "mlp_w2": 0, "ab_w_in": 1, "ab_w_out": 0, "c_w_qkv": 1, "c_w_out": 0}


def forward_backward(x, target, p, tr, conv_full):
    s, d = x.shape
    depth = p["mix_norm_g"].shape[0]
    tabs = rope_tables(s)
    saved = []
    for l in range(depth):
        i = l // 2
        mix_g, mlp_g = p["mix_norm_g"][l][None, :], p["mlp_norm_g"][l][None, :]
        st = dict(x_in=x)
        nxt = () if l + 1 == depth else ((("c_w_qkv", i), ("c_w_out", i)) if l % 2 == 0 else
                                        (("ab_w_in", i + 1), ("ab_w_out", i + 1)))
        if l % 2 == 0:
            sp = _mixer_params(p, conv_full, i)
            z, h = tr.run(norm_matmul, x, mix_g, tr.w["ab_w_in", i], f"ab_in_{l}", gather=[("mlp_w1", l)])
            ycat, gc = tr.run(ab_mid_forward, z, sp, f"ab_mid_{l}", gather=[("mlp_w2", l)])
            x = matmul_residual(x, ycat, tr.w["ab_w_out", i], f"ab_out_{l}")
            st.update(z=z, h=h, y=ycat, gc=gc, sp=sp)
        else:
            gains = _head_gains(p, i)
            qkv, h = norm_matmul(x, mix_g, tr.w["c_w_qkv", i], f"c_qkv_{l}")[0]
            qk = qk_prep_forward(qkv, tabs, gains, f"c_prep_{l}")[0]
            ahead = [(n, j) for j in (l, l + 1) if j < depth for n in ("mlp_w1", "mlp_w2")]
            o, lse = tr.run(attention_forward, qk, qkv, f"c_attn_{l}", gather=ahead)
            x = matmul_residual(x, o, tr.w["c_w_out", i], f"c_out_{l}")
            st.update(qkv=qkv, h=h, qk=qk, y=o, lse=lse, gains=gains)
        st["x_mid"] = x
        x, a, h2 = tr.run(mlp_forward, x, mlp_g, tr.w["mlp_w1", l], tr.w["mlp_w2", l], f"mlp_{l}", gather=nxt)
        st.update(a=a, h2=h2)
        saved.append(st)

    dy, loss_part = loss_and_grad(x, target, "loss")

    def tn(*a, comm, **kw):
        out, couts = matmul_tn(*a, comm=comm, **kw)
        return [out], couts

    small = {n: [None] * p[n].shape[0] for n in SMALL}
    conv_grads = [None] * p["b_conv_b"].shape[0]
    for l in reversed(range(depth)):
        i, st = l // 2, saved[l]
        mix_g, mlp_g = p["mix_norm_g"][l][None, :], p["mlp_norm_g"][l][None, :]
        w1, w2 = tr.w["mlp_w1", l], tr.w["mlp_w2", l]
        dxm, da, dg, dyb = tr.run(mlp_backward, dy, st["a"], st["x_mid"], mlp_g, w1, w2, f"mlp_bwd_{l}", send=True)
        small["mlp_norm_g"][l] = dg[0]
        tr.queue.append((("mlp_w1", l), matmul_tn(st["h2"], da, f"mlp_dw1_{l}", n_split=2)[0]))
        tr.queue.append((("mlp_w2", l), matmul_tn(st["a"], dyb, f"mlp_dw2_{l}", m_split=2, relu2=True)[0]))
        if l % 2 == 0:
            sp = st["sp"]
            wout = tr.w["ab_w_out", i]
            dycat = matmul_nt(dxm, wout, f"ab_out_bwd_{l}")
            tr.queue.append((("ab_w_out", i), matmul_tn(st["y"], dxm, f"ab_dwout_{l}")[0]))
            dz, dgc, dw, dsb, dvg, dvb, dcg, dcbn, dcb = ab_mid_backward(dycat, st["z"], st["gc"], sp, f"ab_mid_bwd_{l}")
            dz, dcw = tr.run(conv_backward, dgc, st["z"], dz, sp, f"ab_conv_bwd_{l}", send=True)
            small["a_spatial_w"][i], small["a_spatial_b"][i] = dw, dsb
            small["a_vnorm_g"][i], small["a_vnorm_b"][i] = dvg[0], dvb[0]
            small["b_norm_g"][i], small["b_norm_b"][i], small["b_conv_b"][i] = dcg[0], dcbn[0], dcb[0]
            conv_grads[i] = dcw[:CONV_W]
            dy, dg = matmul_nt_norm_backward(dz, tr.w["ab_w_in", i], st["x_in"], mix_g, dxm, f"ab_in_bwd_{l}")
            small["mix_norm_g"][l] = dg[0]
            last = []
            if l == 0 and tr.shards is not None:
                tr.shards["small_grads", 0] = _pack({n: jnp.stack(t) for n, t in small.items()})
                last = [("small_grads", 0)]
            dwin = tr.run(tn, st["h"], dz, f"ab_dwin_{l}", send=True, gather=last)[0]
            tr.queue.append((("ab_w_in", i), dwin))
        else:
            wout = tr.w["c_w_out", i]
            do = matmul_nt(dxm, wout, f"c_out_bwd_{l}")
            tr.queue.append((("c_w_out", i), matmul_tn(st["y"], dxm, f"c_dwout_{l}")[0]))
            stats = attention_delta(do, st["y"], st["lse"], f"c_delta_{l}")
            dk, dv, dq, dqp, dqn = tr.run(attention_backward, st["qk"], st["qkv"], do, stats, f"c_attn_bwd_{l}", send=True)
            dqkv, dgn = qk_prep_backward(dq, dqp, dqn, dk, dv, st["qkv"], tabs, st["gains"], f"c_prep_bwd_{l}")
            small["c_q_norm_g"][i] = dgn[0, 0, :HEAD_DIM] + dgn[0, 0, HEAD_DIM:]
            small["c_k_norm_g"][i] = dgn[1, 0, :HEAD_DIM] + dgn[1, 0, HEAD_DIM:]
            dy, dg = matmul_nt_norm_backward(dqkv, tr.w["c_w_qkv", i], st["x_in"], mix_g, dxm, f"c_qkv_bwd_{l}")
            small["mix_norm_g"][l] = dg[0]
            tr.queue.append((("c_w_qkv", i), matmul_tn(st["h"], dqkv, f"c_dwqkv_{l}", n_split=2)[0]))
    small = {n: jnp.stack(v) for n, v in small.items()}
    return loss_part, dy, small, jnp.stack(conv_grads)


def kernel(x, mix_norm_g, mlp_norm_g, mlp_w1, mlp_w2, ab_w_in, a_spatial_w, a_spatial_b, a_vnorm_g, a_vnorm_b, b_conv_w, b_conv_b, b_norm_g, b_norm_b, ab_w_out, c_w_qkv, c_q_norm_g, c_k_norm_g, c_w_out, loss_target, m_mix_norm_g, m_mlp_norm_g, m_mlp_w1, m_mlp_w2, m_ab_w_in, m_a_spatial_w, m_a_spatial_b, m_a_vnorm_g, m_a_vnorm_b, m_b_conv_w, m_b_conv_b, m_b_norm_g, m_b_norm_b, m_ab_w_out, m_c_w_qkv, m_c_q_norm_g, m_c_k_norm_g, m_c_w_out, v_mix_norm_g, v_mlp_norm_g, v_mlp_w1, v_mlp_w2, v_ab_w_in, v_a_spatial_w, v_a_spatial_b, v_a_vnorm_g, v_a_vnorm_b, v_b_conv_w, v_b_conv_b, v_b_norm_g, v_b_norm_b, v_ab_w_out, v_c_w_qkv, v_c_q_norm_g, v_c_k_norm_g, v_c_w_out):
    args = dict(locals())
    w = {n: args[n] for n in WEIGHTS}
    m = {n: args["m_" + n] for n in WEIGHTS}
    v = {n: args["v_" + n] for n in WEIGHTS}

    shards = {(n, l): w[n][l].astype(BF16) for n in BIG for l in range(w[n].shape[0])}
    shards["b_conv_w", 0] = w["b_conv_w"]
    tr = Traffic(shards)
    first = [("ab_w_in", 0), ("ab_w_out", 0), ("b_conv_w", 0)]
    tr.run(lambda comm: ([], run_copies(comm, "gather_first")), gather=first)
    conv = tr.w["b_conv_w", 0]
    conv_full = jnp.transpose(conv, (1, 2, 0, 3)).reshape(conv.shape[1], conv.shape[2], -1)

    loss_part, dx, small, conv_grad = forward_backward(x[0], loss_target[0], w, tr, conv_full)
    loss = lax.psum(jnp.sum(loss_part), ("x", "y", "c"))
    nl, kw, cw = conv_grad.shape
    conv_parts = jnp.transpose(conv_grad.reshape(nl, kw, N_DEV, cw // N_DEV), (2, 0, 1, 3))
    tr.flush("exchange_last", [(("b_conv_w", 0), conv_parts)])
    packed = tr.w["small_grads", 0]

    grads, deltas, new_m, new_v = {}, {}, {}, {}

    def update(n, recvs):
        shape = w[n].shape
        flat = lambda t: t.reshape(-1, shape[-1])
        recvs = [t.reshape(N_DEV, -1, shape[-1]) for t in recvs]
        outs = adamw_update(recvs, flat(w[n]), flat(m[n]), flat(v[n]), "adamw_" + n)
        grads[n], deltas[n], new_m[n], new_v[n] = (t.reshape(shape) for t in outs)

    for n in BIG:
        update(n, [tr.parts[n, l] for l in range(w[n].shape[0])])
    update("b_conv_w", [tr.parts["b_conv_w", 0]])
    outs = adamw_update([packed], _pack(w), _pack(m), _pack(v), "adamw_small")
    for dst, t in zip((grads, deltas, new_m, new_v), outs):
        dst.update(_unpack(t, w))

    return (loss, dx[None], *[grads[n] for n in WEIGHTS], *[deltas[n] for n in WEIGHTS],
            *[new_m[n] for n in WEIGHTS], *[new_v[n] for n in WEIGHTS])
```

```python
import functools
import math

import jax
import jax.numpy as jnp
from jax import lax
from jax.experimental import pallas as pl
from jax.experimental.pallas import tpu as pltpu

F32, BF16 = jnp.float32, jnp.bfloat16
N_DEV = 8
EPS = 1e-6
NEG = -1e30
LANES = 128
HEAD_DIM = 64
CHUNK = 128
CONV_W = 31
CONV_HALO = 16
BAND = 64
DILATIONS = (1, 4, 16)
ROT_DIM = 16
ROPE_THETA = 500000.0
VMEM_LIMIT = 56 * 1024 * 1024
MLP_CHUNK = 1024
MLP_BWD_CHUNK = 512
ADAM_LR, ADAM_B1, ADAM_B2, ADAM_EPS, ADAM_WD, ADAM_STEP = 0.001, 0.9, 0.999, 1e-08, 0.01, 10
MESH = pl.DeviceIdType.MESH


def _params(*sem):
    return pltpu.CompilerParams(dimension_semantics=sem, vmem_limit_bytes=VMEM_LIMIT)


def _dot(a, b):
    return jnp.dot(a, b, preferred_element_type=F32)


def _dot_nt(a, b):
    return lax.dot_general(a, b, (((1,), (1,)), ((), ())), preferred_element_type=F32)


def _dot_tn(a, b):
    return lax.dot_general(a, b, (((0,), (0,)), ((), ())), preferred_element_type=F32)


def _rms_r(x):
    return lax.rsqrt(jnp.mean(x * x, axis=-1, keepdims=True) + EPS)


def _sigmoid(x):
    return 1.0 / (1.0 + jnp.exp(-x))


_GK = math.sqrt(2.0 / math.pi)


def _gelu(x):
    return 0.5 * x * (1.0 + jnp.tanh(_GK * (x + 0.044715 * x * x * x)))


def _gelu_grad(x):
    t = jnp.tanh(_GK * (x + 0.044715 * x * x * x))
    return 0.5 * (1.0 + t) + 0.5 * x * (1.0 - t * t) * (_GK * (1.0 + 3.0 * 0.044715 * x * x))


def _seg_sum(x, bd):
    hi = x.astype(BF16)
    r1 = x - hi.astype(F32)
    mid = r1.astype(BF16)
    lo = (r1 - mid.astype(F32)).astype(BF16)
    return _dot(hi, bd) + _dot(mid, bd) + _dot(lo, bd)


def _block_diag(n):
    i = lax.broadcasted_iota(jnp.int32, (n, n), 0) // HEAD_DIM
    j = lax.broadcasted_iota(jnp.int32, (n, n), 1) // HEAD_DIM
    return jnp.where(i == j, 1.0, 0.0).astype(BF16)


def _tile(s, cap):
    t = min(s, cap)
    assert s % t == 0
    return t


def _my_index():
    return 4 * lax.axis_index("x") + 2 * lax.axis_index("y") + lax.axis_index("c")


def _device(i):
    return (i // 4, (i // 2) % 2, i % 2)


_HBM = pl.BlockSpec(memory_space=pl.ANY)


class PeerCopies:
    def __init__(self, operands, modes):
        self.inputs, self.modes = list(operands), list(modes)
        self.out_shape = []
        for t, (kind, axis) in zip(self.inputs, self.modes):
            shape = list(t.shape)
            if kind == "gather":
                shape = [N_DEV] + shape if axis is None else shape[:axis] + [N_DEV * shape[axis]] + shape[axis + 1:]
            elif axis is not None:
                shape = [N_DEV] + shape[:axis] + [shape[axis] // N_DEV] + shape[axis + 1:]
            self.out_shape.append(jax.ShapeDtypeStruct(tuple(shape), t.dtype))
        n = len(self.inputs)
        self.scratch = [pltpu.SemaphoreType.DMA((n, N_DEV - 1)), pltpu.SemaphoreType.DMA((n, N_DEV - 1)),
                        pltpu.SemaphoreType.DMA((n,))]

    @staticmethod
    def _block(ref, axis, size, j):
        if axis is None:
            return ref.at[j]
        return ref.at[tuple([slice(None)] * axis + [pl.ds(j * size, size)])]

    def _copies(self, in_refs, out_refs, sems, arrivals):
        send_sems, recv_sems, local_sems = sems
        me = _my_index()
        local, sends, recvs = [], [], []
        for t, (src, dst) in enumerate(zip(in_refs, out_refs)):
            kind, axis = self.modes[t]
            if kind == "gather":
                size = None if axis is None else src.shape[axis]
                source = lambda j, src=src: src
                place = lambda j, dst=dst, axis=axis, size=size: self._block(dst, axis, size, j)
            else:
                size = None if axis is None else src.shape[axis] // N_DEV
                source = lambda j, src=src, axis=axis, size=size: self._block(src, axis, size, j)
                place = lambda j, dst=dst: dst.at[j]
            local.append(pltpu.make_async_copy(source(me), place(me), local_sems.at[t]))
            for k in range(N_DEV - 1):
                to, frm = (me + k + 1) % N_DEV, (me + N_DEV - k - 1) % N_DEV
                sends.append(pltpu.make_async_remote_copy(
                    src_ref=source(to), dst_ref=place(me), send_sem=send_sems.at[t, k], recv_sem=recv_sems.at[t, k],
                    device_id=_device(to), device_id_type=MESH))
                if arrivals:
                    recvs.append(pltpu.make_async_remote_copy(
                        src_ref=source(me), dst_ref=place(frm), send_sem=send_sems.at[t, k], recv_sem=recv_sems.at[t, k],
                        device_id=_device(frm), device_id_type=MESH))
        return local, sends, recvs

    def start(self, in_refs, out_refs, sems):
        local, sends, _ = self._copies(in_refs, out_refs, sems, False)
        for cp in local + sends:
            cp.start()

    def finish(self, in_refs, out_refs, sems):
        local, sends, recvs = self._copies(in_refs, out_refs, sems, True)
        for cp in recvs:
            cp.wait_recv()
        for cp in sends:
            cp.wait_send()
        for cp in local:
            cp.wait()


def _pallas(body, *, name, args, in_specs, out_specs, out_shape, grid=(), scratch=(), sem=(), comm=None, aliases=None):
    n_in, n_out, n_scr = len(args), len(out_shape), len(scratch)
    if comm is None:
        outs = pl.pallas_call(
            body, name=name, grid=grid, in_specs=in_specs, out_specs=out_specs, out_shape=out_shape,
            scratch_shapes=list(scratch), input_output_aliases=aliases or {}, compiler_params=_params(*sem))(*args)
        return list(outs), []
    ci, co = len(comm.inputs), len(comm.out_shape)

    def hosted(*refs):
        ins, cins = refs[:n_in], refs[n_in:n_in + ci]
        outs, couts = refs[n_in + ci:n_in + ci + n_out], refs[n_in + ci + n_out:n_in + ci + n_out + co]
        rest = refs[n_in + ci + n_out + co:]
        scr, sems = rest[:n_scr], rest[n_scr:]
        if not grid:
            comm.start(cins, couts, sems)
            comm.finish(cins, couts, sems)
            return
        first = last = None
        for axis, size in enumerate(grid):
            f, l = pl.program_id(axis) == 0, pl.program_id(axis) == size - 1
            first, last = (f, l) if first is None else (first & f, last & l)
        pl.when(first)(lambda: comm.start(cins, couts, sems))
        body(*ins, *outs, *scr)
        pl.when(last)(lambda: comm.finish(cins, couts, sems))

    outs = pl.pallas_call(
        hosted, name=name, grid=grid, in_specs=list(in_specs) + [_HBM] * ci, out_specs=list(out_specs) + [_HBM] * co,
        out_shape=list(out_shape) + comm.out_shape, scratch_shapes=list(scratch) + comm.scratch,
        input_output_aliases=aliases or {}, compiler_params=_params(*["arbitrary"] * len(grid)))(*args, *comm.inputs)
    return list(outs[:n_out]), list(outs[n_out:])


def run_copies(comm, name):
    return _pallas(None, name=name, args=[], in_specs=[], out_specs=[], out_shape=[], comm=comm)[1]


def norm_matmul(x, g, wg, name, comm=None):
    s, d = x.shape
    n = wg.shape[-1]
    ns = 1024 if n % 1024 == 0 else n // 4
    tm = _tile(s, 1024)

    def body(x_ref, g_ref, w_ref, z_ref, h_ref):
        @pl.when(pl.program_id(1) == 0)
        def _():
            xv = x_ref[...]
            h_ref[...] = (xv * _rms_r(xv) * g_ref[...]).astype(BF16)
        z_ref[...] = _dot(h_ref[...], w_ref[...])

    return _pallas(
        body, name=name, grid=(s // tm, n // ns), args=[x, g, wg], comm=comm,
        in_specs=[pl.BlockSpec((tm, d), lambda i, j: (i, 0)),
                  pl.BlockSpec((1, d), lambda i, j: (0, 0)),
                  pl.BlockSpec((d, ns), lambda i, j: (0, j))],
        out_specs=[pl.BlockSpec((tm, ns), lambda i, j: (i, j)),
                   pl.BlockSpec((tm, d), lambda i, j: (i, 0))],
        out_shape=[jax.ShapeDtypeStruct((s, n), F32), jax.ShapeDtypeStruct((s, d), BF16)],
        sem=("parallel", "arbitrary"))


def mlp_forward(x, g, w1g, w2g, name, comm=None):
    s, d = x.shape
    f = w1g.shape[-1]
    fs = MLP_CHUNK
    tm = _tile(s, 1024)

    def body(x_ref, g_ref, w1_ref, w2_ref, xo_ref, a_ref, h_ref):
        @pl.when(pl.program_id(1) == 0)
        def _():
            xv = x_ref[...]
            h_ref[...] = (xv * _rms_r(xv) * g_ref[...]).astype(BF16)
            xo_ref[...] = xv
        a = _dot(h_ref[...], w1_ref[...])
        a_ref[...] = a.astype(BF16)
        r = jnp.maximum(a, 0.0)
        xo_ref[...] += _dot((r * r).astype(BF16), w2_ref[...])

    return _pallas(
        body, name=name, grid=(s // tm, f // fs), args=[x, g, w1g, w2g], comm=comm,
        in_specs=[pl.BlockSpec((tm, d), lambda i, j: (i, 0)),
                  pl.BlockSpec((1, d), lambda i, j: (0, 0)),
                  pl.BlockSpec((d, fs), lambda i, j: (0, j)),
                  pl.BlockSpec((fs, d), lambda i, j: (j, 0))],
        out_specs=[pl.BlockSpec((tm, d), lambda i, j: (i, 0)),
                   pl.BlockSpec((tm, fs), lambda i, j: (i, j)),
                   pl.BlockSpec((tm, d), lambda i, j: (i, 0))],
        out_shape=[jax.ShapeDtypeStruct((s, d), F32), jax.ShapeDtypeStruct((s, f), BF16),
                   jax.ShapeDtypeStruct((s, d), BF16)],
        sem=("parallel", "arbitrary"))


def _norm_backward(dh, xv, g, dres):
    r = _rms_r(xv)
    xh = xv * r
    t = dh * g
    dx = dres + r * (t - xh * jnp.mean(t * xh, axis=-1, keepdims=True))
    return dx, jnp.sum(dh * xh, axis=0, keepdims=True)


def mlp_backward(dy, a, x, g, w1g, w2g, name, comm=None):
    s, d = x.shape
    f = w1g.shape[-1]
    fs = MLP_BWD_CHUNK
    tm = _tile(s, 1024)

    def body(dy_ref, a_ref, x_ref, g_ref, w1_ref, w2_ref, dx_ref, da_ref, dg_ref, dyb_ref, dh_ref):
        i, j = pl.program_id(0), pl.program_id(1)

        @pl.when(j == 0)
        def _():
            dyb_ref[...] = dy_ref[...].astype(BF16)
            dh_ref[...] = jnp.zeros_like(dh_ref)

        dr = _dot_nt(dyb_ref[...], w2_ref[...])
        da = (dr * (2.0 * jnp.maximum(a_ref[...].astype(F32), 0.0))).astype(BF16)
        da_ref[...] = da
        dh_ref[...] += _dot_nt(da, w1_ref[...])

        @pl.when(j == f // fs - 1)
        def _():
            dx, dgp = _norm_backward(dh_ref[...], x_ref[...], g_ref[...], dy_ref[...])
            dx_ref[...] = dx

            @pl.when(i == 0)
            def _():
                dg_ref[...] = dgp

            @pl.when(i > 0)
            def _():
                dg_ref[...] += dgp

    return _pallas(
        body, name=name, grid=(s // tm, f // fs), args=[dy, a, x, g, w1g, w2g], comm=comm,
        in_specs=[pl.BlockSpec((tm, d), lambda i, j: (i, 0)),
                  pl.BlockSpec((tm, fs), lambda i, j: (i, j)),
                  pl.BlockSpec((tm, d), lambda i, j: (i, 0)),
                  pl.BlockSpec((1, d), lambda i, j: (0, 0)),
                  pl.BlockSpec((d, fs), lambda i, j: (0, j)),
                  pl.BlockSpec((fs, d), lambda i, j: (j, 0))],
        out_specs=[pl.BlockSpec((tm, d), lambda i, j: (i, 0)),
                   pl.BlockSpec((tm, fs), lambda i, j: (i, j)),
                   pl.BlockSpec((1, d), lambda i, j: (0, 0)),
                   pl.BlockSpec((tm, d), lambda i, j: (i, 0))],
        out_shape=[jax.ShapeDtypeStruct((s, d), F32), jax.ShapeDtypeStruct((s, f), BF16),
                   jax.ShapeDtypeStruct((1, d), F32), jax.ShapeDtypeStruct((s, d), BF16)],
        scratch=[pltpu.VMEM((tm, d), F32)],
        sem=("arbitrary", "arbitrary"))


def matmul_tn(a, b, name, m_split=1, n_split=1, relu2=False, comm=None):
    s, m = a.shape
    n = b.shape[1]
    ts = _tile(s, 1024)
    bm, bn = m // m_split, n // n_split
    a_map = lambda j, k: (k, j // n_split)
    b_map = lambda j, k: (k, j % n_split)

    def body(a_ref, b_ref, o_ref, acc_ref):
        k = pl.program_id(1)
        av = a_ref[...]
        if relu2:
            af = jnp.maximum(av.astype(F32), 0.0)
            av = af * af
        p = _dot_tn(av.astype(BF16), b_ref[...].astype(BF16))

        @pl.when(k == 0)
        def _():
            acc_ref[...] = p

        @pl.when(k > 0)
        def _():
            acc_ref[...] += p

        @pl.when(k == s // ts - 1)
        def _():
            o_ref[...] = acc_ref[...].astype(BF16)

    outs, couts = _pallas(
        body, name=name, grid=(m_split * n_split, s // ts), args=[a, b], comm=comm,
        in_specs=[pl.BlockSpec((ts, bm), a_map), pl.BlockSpec((ts, bn), b_map)],
        out_specs=[pl.BlockSpec((bm, bn), lambda j, k: (j // n_split, j % n_split))],
        out_shape=[jax.ShapeDtypeStruct((m, n), BF16)],
        scratch=[pltpu.VMEM((bm, bn), F32)],
        sem=("parallel", "arbitrary"))
    return outs[0], couts


def matmul_residual(x, y, w, name):
    s, n = x.shape
    k = y.shape[1]
    tm = _tile(s, 1024)

    def body(x_ref, y_ref, w_ref, o_ref):
        o_ref[...] = x_ref[...] + _dot(y_ref[...], w_ref[...])

    return pl.pallas_call(
        body, name=name, grid=(s // tm,),
        in_specs=[pl.BlockSpec((tm, n), lambda i: (i, 0)),
                  pl.BlockSpec((tm, k), lambda i: (i, 0)),
                  pl.BlockSpec((k, n), lambda i: (0, 0))],
        out_specs=pl.BlockSpec((tm, n), lambda i: (i, 0)),
        out_shape=jax.ShapeDtypeStruct((s, n), F32),
        compiler_params=_params("parallel"),
    )(x, y, w)


def matmul_nt(dy, wg, name):
    s, n = dy.shape
    k = wg.shape[0]
    tm = _tile(s, 1024)

    def body(dy_ref, w_ref, o_ref):
        o_ref[...] = _dot_nt(dy_ref[...].astype(BF16), w_ref[...])

    return pl.pallas_call(
        body, name=name, grid=(s // tm,),
        in_specs=[pl.BlockSpec((tm, n), lambda i: (i, 0)),
                  pl.BlockSpec((k, n), lambda i: (0, 0))],
        out_specs=pl.BlockSpec((tm, k), lambda i: (i, 0)),
        out_shape=jax.ShapeDtypeStruct((s, k), F32),
        compiler_params=_params("parallel"),
    )(dy, wg)


def matmul_nt_norm_backward(dz, wg, x, g, dres, name):
    s, d = x.shape
    n = wg.shape[-1]
    tm = _tile(s, 512)

    def body(dz_ref, w_ref, x_ref, g_ref, dres_ref, dx_ref, dg_ref):
        dx, dgp = _norm_backward(_dot_nt(dz_ref[...], w_ref[...]), x_ref[...], g_ref[...], dres_ref[...])
        dx_ref[...] = dx
        _accumulate(dg_ref, dgp, pl.program_id(0) == 0)

    return pl.pallas_call(
        body, name=name, grid=(s // tm,),
        in_specs=[pl.BlockSpec((tm, n), lambda i: (i, 0)),
                  pl.BlockSpec((d, n), lambda i: (0, 0)),
                  pl.BlockSpec((tm, d), lambda i: (i, 0)),
                  pl.BlockSpec((1, d), lambda i: (0, 0)),
                  pl.BlockSpec((tm, d), lambda i: (i, 0))],
        out_specs=[pl.BlockSpec((tm, d), lambda i: (i, 0)),
                   pl.BlockSpec((1, d), lambda i: (0, 0))],
        out_shape=[jax.ShapeDtypeStruct((s, d), F32), jax.ShapeDtypeStruct((1, d), F32)],
        compiler_params=_params("arbitrary"),
    )(dz, wg, x, g, dres)


def loss_and_grad(y, target, name):
    s, d = y.shape
    tm = _tile(s, 1024)

    def body(y_ref, t_ref, dy_ref, l_ref):
        e = y_ref[...] - t_ref[...]
        dy_ref[...] = e / d
        part = jnp.sum(e * e, axis=0, keepdims=True) * (0.5 / d)

        @pl.when(pl.program_id(0) == 0)
        def _():
            l_ref[...] = part

        @pl.when(pl.program_id(0) > 0)
        def _():
            l_ref[...] += part

    return pl.pallas_call(
        body, name=name, grid=(s // tm,),
        in_specs=[pl.BlockSpec((tm, d), lambda i: (i, 0)), pl.BlockSpec((tm, d), lambda i: (i, 0))],
        out_specs=[pl.BlockSpec((tm, d), lambda i: (i, 0)), pl.BlockSpec((1, d), lambda i: (0, 0))],
        out_shape=[jax.ShapeDtypeStruct((s, d), F32), jax.ShapeDtypeStruct((1, d), F32)],
        compiler_params=_params("arbitrary"),
    )(y, target)


def _layernorm(x, g, b):
    mu = jnp.mean(x, axis=-1, keepdims=True)
    xc = x - mu
    rstd = lax.rsqrt(jnp.mean(xc * xc, axis=-1, keepdims=True) + EPS)
    xn = xc * rstd
    return xn * g + b, xn, rstd


def _layernorm_backward(dy, xn, rstd, g):
    dxn = dy * g
    return rstd * (dxn - jnp.mean(dxn, axis=-1, keepdims=True) - xn * jnp.mean(dxn * xn, axis=-1, keepdims=True))


def _group_halves(x_ref, jp, nch):
    blk = jnp.concatenate([x_ref[c * CHUNK:(c + 1) * CHUNK, jp * LANES:(jp + 1) * LANES] for c in range(nch)], axis=1)
    low = (lax.broadcasted_iota(jnp.int32, blk.shape, 1) % LANES) < HEAD_DIM
    return jnp.where(low, blk, 0.0).astype(BF16), jnp.where(low, 0.0, blk).astype(BF16)


def _spatial_apply(src_ref, w_ref, dst_ref, nch, bias_ref=None):
    for jp in range(4):
        lo, hi = _group_halves(src_ref, jp, nch)
        r = _dot(w_ref[2 * jp], lo) + _dot(w_ref[2 * jp + 1], hi)
        for c in range(nch):
            v = r[:, c * LANES:(c + 1) * LANES]
            if bias_ref is not None:
                v = v + bias_ref[:, jp * LANES:(jp + 1) * LANES]
            dst_ref[c * CHUNK:(c + 1) * CHUNK, jp * LANES:(jp + 1) * LANES] = v


def _glu(zb):
    w = zb.shape[1] // 2
    return zb[:, :w] * _sigmoid(zb[:, w:])


def _fill_padded(pad_ref, prev, cur, nxt, i, nt, tm):
    pad_ref[0:CONV_HALO, :] = jnp.where(i > 0, prev, 0.0)
    pad_ref[CONV_HALO:CONV_HALO + tm, :] = cur
    pad_ref[CONV_HALO + tm:2 * CONV_HALO + tm, :] = jnp.where(i < nt - 1, nxt, 0.0)


def _halo_specs(tm, s, width, col):
    hb, nhb = tm // CONV_HALO, s // CONV_HALO
    return [pl.BlockSpec((tm, width), lambda i: (i, col)),
            pl.BlockSpec((CONV_HALO, width), lambda i: (jnp.maximum(i * hb - 1, 0), col)),
            pl.BlockSpec((CONV_HALO, width), lambda i: (jnp.minimum((i + 1) * hb, nhb - 1), col))]


def _const_spec(shape):
    nd = len(shape)
    return pl.BlockSpec(shape, lambda i: (0,) * nd)


SUBLANES = 8


def _shift_scratch(tm, width):
    return pltpu.VMEM((SUBLANES - 1, tm + 2 * CONV_HALO - SUBLANES, width), F32)


def _fill_shifts(sh_ref, pad_ref):
    rows = sh_ref.shape[1]
    for sft in range(1, SUBLANES):
        sh_ref[sft - 1] = pad_ref[pl.ds(sft, rows), :]


def _tap(pad_ref, sh_ref, offset, cols):
    sft = offset % SUBLANES
    rows = pl.ds(offset - sft, CHUNK)
    return pad_ref[rows, cols] if sft == 0 else sh_ref[sft - 1, rows, cols]


def ab_mid_forward(z, sp, name, comm=None):
    s = z.shape[0]
    aw = z.shape[1] // 4
    tm = _tile(s, 512)
    nch, nt = tm // CHUNK, s // tm

    def body(zu_ref, zv_ref, zb_ref, zp_ref, zn_ref, w_ref, bias_ref, vg_ref, vb_ref, cw_ref, cb_ref, cg_ref, cbn_ref,
             y_ref, gc_ref, vl_ref, sv_ref, pad_ref, sh_ref):
        i = pl.program_id(0)
        vl_ref[...] = _layernorm(_gelu(zv_ref[...]), vg_ref[...], vb_ref[...])[0]
        _spatial_apply(vl_ref, w_ref, sv_ref, nch, bias_ref)
        y_ref[:, :aw] = (_gelu(zu_ref[...]) * sv_ref[...]).astype(BF16)

        _fill_padded(pad_ref, _glu(zp_ref[...]), _glu(zb_ref[...]), _glu(zn_ref[...]), i, nt, tm)
        _fill_shifts(sh_ref, pad_ref)
        for rb in range(tm // CHUNK):
            for lb in range(aw // LANES):
                cols = pl.ds(lb * LANES, LANES)
                acc = jnp.broadcast_to(cb_ref[:, cols], (CHUNK, LANES))
                for k in range(CONV_W):
                    acc = acc + cw_ref[k:k + 1, cols] * _tap(pad_ref, sh_ref, rb * CHUNK + CONV_HALO - CONV_W // 2 + k, cols)
                gc_ref[rb * CHUNK:(rb + 1) * CHUNK, cols] = acc
        yl = _layernorm(gc_ref[...], cg_ref[...], cbn_ref[...])[0]
        y_ref[:, aw:] = (yl * _sigmoid(yl)).astype(BF16)

    return _pallas(
        body, name=name, grid=(nt,), comm=comm,
        args=[z, z, z, z, z, sp["w"], sp["bias"], sp["vg"], sp["vb"], sp["cw"], sp["cb"], sp["cg"], sp["cbn"]],
        in_specs=[pl.BlockSpec((tm, aw), lambda i: (i, 0)), pl.BlockSpec((tm, aw), lambda i: (i, 1))]
        + _halo_specs(tm, s, 2 * aw, 1)
        + [_const_spec(sp["w"].shape), _const_spec(sp["bias"].shape)]
        + [_const_spec((1, aw))] * 2 + [_const_spec(sp["cw"].shape)] + [_const_spec((1, aw))] * 3,
        out_specs=[pl.BlockSpec((tm, 2 * aw), lambda i: (i, 0)), pl.BlockSpec((tm, aw), lambda i: (i, 0))],
        out_shape=[jax.ShapeDtypeStruct((s, 2 * aw), BF16), jax.ShapeDtypeStruct((s, aw), F32)],
        scratch=[pltpu.VMEM((tm, aw), F32), pltpu.VMEM((tm, aw), F32), pltpu.VMEM((tm + 2 * CONV_HALO, aw), F32),
                 _shift_scratch(tm, aw)],
        sem=("parallel",))


def _accumulate(ref, val, first):
    @pl.when(first)
    def _():
        ref[...] = val

    @pl.when(jnp.logical_not(first))
    def _():
        ref[...] += val


def ab_mid_backward(dy, z, gc, sp, name):
    s = z.shape[0]
    aw = z.shape[1] // 4
    tm = _tile(s, 512)
    nch, nt = tm // CHUNK, s // tm

    def body(dya_ref, dyb_ref, zu_ref, zv_ref, gc_ref, w_ref, wt_ref, bias_ref, vg_ref, vb_ref, cg_ref, cbn_ref,
             dz_ref, dgc_ref, dw_ref, dsb_ref, dvg_ref, dvb_ref, dcg_ref, dcbn_ref, dcb_ref,
             vl_ref, sv_ref, dsv_ref, dvl_ref):
        first = pl.program_id(0) == 0
        zu, zv = zu_ref[...], zv_ref[...]
        u = _gelu(zu)
        vl, vn, vrstd = _layernorm(_gelu(zv), vg_ref[...], vb_ref[...])
        vl_ref[...] = vl
        _spatial_apply(vl_ref, w_ref, sv_ref, nch, bias_ref)
        dya = dya_ref[...]
        dz_ref[:, :aw] = (dya * sv_ref[...] * _gelu_grad(zu)).astype(BF16)
        dsv = dya * u
        dsv_ref[...] = dsv
        _spatial_apply(dsv_ref, wt_ref, dvl_ref, nch)

        for jp in range(4):
            dlo, dhi = _group_halves(dsv_ref, jp, nch)
            vlo, vhi = _group_halves(vl_ref, jp, nch)
            vall = vlo + vhi
            _accumulate(dw_ref.at[2 * jp], _dot_nt(dlo, vall), first)
            _accumulate(dw_ref.at[2 * jp + 1], _dot_nt(dhi, vall), first)
        rows = dsv[0:CHUNK]
        for c in range(1, nch):
            rows = rows + dsv[c * CHUNK:(c + 1) * CHUNK]
        grp = lax.broadcasted_iota(jnp.int32, (8, aw), 0) == lax.broadcasted_iota(jnp.int32, (8, aw), 1) // HEAD_DIM
        e = jnp.where(grp, 1.0, 0.0).astype(BF16)
        hi = rows.astype(BF16)
        r1 = rows - hi.astype(F32)
        mid = r1.astype(BF16)
        lo = (r1 - mid.astype(F32)).astype(BF16)
        _accumulate(dsb_ref, _dot_nt(e, hi) + _dot_nt(e, mid) + _dot_nt(e, lo), first)

        dvl = dvl_ref[...]
        _accumulate(dvg_ref, jnp.sum(dvl * vn, axis=0, keepdims=True), first)
        _accumulate(dvb_ref, jnp.sum(dvl, axis=0, keepdims=True), first)
        dz_ref[:, aw:] = (_layernorm_backward(dvl, vn, vrstd, vg_ref[...]) * _gelu_grad(zv)).astype(BF16)

        yl, yn, yrstd = _layernorm(gc_ref[...], cg_ref[...], cbn_ref[...])
        sg = _sigmoid(yl)
        dyl = dyb_ref[...] * (sg + yl * sg * (1.0 - sg))
        _accumulate(dcg_ref, jnp.sum(dyl * yn, axis=0, keepdims=True), first)
        _accumulate(dcbn_ref, jnp.sum(dyl, axis=0, keepdims=True), first)
        dgc = _layernorm_backward(dyl, yn, yrstd, cg_ref[...])
        dgc_ref[...] = dgc
        _accumulate(dcb_ref, jnp.sum(dgc, axis=0, keepdims=True), first)

    vec = jax.ShapeDtypeStruct((1, aw), F32)
    return pl.pallas_call(
        body, name=name, grid=(nt,),
        in_specs=[pl.BlockSpec((tm, aw), lambda i: (i, 0)), pl.BlockSpec((tm, aw), lambda i: (i, 1)),
                  pl.BlockSpec((tm, aw), lambda i: (i, 0)), pl.BlockSpec((tm, aw), lambda i: (i, 1)),
                  pl.BlockSpec((tm, aw), lambda i: (i, 0)),
                  _const_spec(sp["w"].shape), _const_spec(sp["w"].shape), _const_spec(sp["bias"].shape)]
        + [_const_spec((1, aw))] * 4,
        out_specs=[pl.BlockSpec((tm, 2 * aw), lambda i: (i, 0)), pl.BlockSpec((tm, aw), lambda i: (i, 0)),
                   _const_spec(sp["w"].shape), _const_spec((8, CHUNK))] + [_const_spec((1, aw))] * 5,
        out_shape=[jax.ShapeDtypeStruct((s, 4 * aw), BF16), jax.ShapeDtypeStruct((s, aw), F32),
                   jax.ShapeDtypeStruct(sp["w"].shape, F32), jax.ShapeDtypeStruct((8, CHUNK), F32)] + [vec] * 5,
        scratch_shapes=[pltpu.VMEM((tm, aw), F32)] * 4,
        compiler_params=_params("arbitrary"),
    )(dy, dy, z, z, gc, sp["w"], sp["wt"], sp["bias"], sp["vg"], sp["vb"], sp["cg"], sp["cbn"])


def conv_backward(dgc, z, dz_in, sp, name, comm=None):
    s = z.shape[0]
    aw = z.shape[1] // 4
    tm = _tile(s, 512)
    nt = s // tm
    off = CONV_HALO - CONV_W // 2

    def body(d_ref, dp_ref, dn_ref, zb_ref, zp_ref, zn_ref, cw_ref, dzin_ref, dz_ref, dcw_ref,
             padd_ref, padg_ref, dgg_ref, shd_ref, shg_ref):
        i = pl.program_id(0)
        _fill_padded(padd_ref, dp_ref[...], d_ref[...], dn_ref[...], i, nt, tm)
        _fill_padded(padg_ref, _glu(zp_ref[...]), _glu(zb_ref[...]), _glu(zn_ref[...]), i, nt, tm)
        _fill_shifts(shd_ref, padd_ref)
        _fill_shifts(shg_ref, padg_ref)

        @pl.when(i == 0)
        def _():
            dcw_ref[...] = jnp.zeros_like(dcw_ref)

        def grad_input(cols, rb):
            acc = jnp.zeros((CHUNK, LANES), F32)
            for k in range(CONV_W):
                acc = acc + cw_ref[k:k + 1, cols] * _tap(padd_ref, shd_ref, rb * CHUNK + CONV_HALO + CONV_W // 2 - k, cols)
            dgg_ref[rb * CHUNK:(rb + 1) * CHUNK, cols] = acc

        def grad_taps(cols, rb):
            dblk = d_ref[rb * CHUNK:(rb + 1) * CHUNK, cols]
            for k in range(CONV_W):
                prod = dblk * _tap(padg_ref, shg_ref, rb * CHUNK + off + k, cols)
                dcw_ref[k:k + 1, cols] += jnp.sum(prod, axis=0, keepdims=True)

        for lb in range(aw // LANES):
            for rb in range(tm // CHUNK):
                pl.when(i >= 0)(functools.partial(grad_input, pl.ds(lb * LANES, LANES), rb))
                pl.when(i >= 0)(functools.partial(grad_taps, pl.ds(lb * LANES, LANES), rb))

        zb = zb_ref[...]
        val, sg = zb[:, :aw], _sigmoid(zb[:, aw:])
        dgg = dgg_ref[...]
        dz_ref[:, :aw] = (dgg * sg).astype(BF16)
        dz_ref[:, aw:] = (dgg * val * sg * (1.0 - sg)).astype(BF16)

    return _pallas(
        body, name=name, grid=(nt,), args=[dgc, dgc, dgc, z, z, z, sp["cw"], dz_in], comm=comm,
        in_specs=_halo_specs(tm, s, aw, 0) + _halo_specs(tm, s, 2 * aw, 1)
        + [_const_spec(sp["cw"].shape), pl.BlockSpec(memory_space=pl.ANY)],
        out_specs=[pl.BlockSpec((tm, 2 * aw), lambda i: (i, 1)), _const_spec(sp["cw"].shape)],
        out_shape=[jax.ShapeDtypeStruct((s, 4 * aw), BF16), jax.ShapeDtypeStruct(sp["cw"].shape, F32)],
        scratch=[pltpu.VMEM((tm + 2 * CONV_HALO, aw), F32)] * 2 + [pltpu.VMEM((tm, aw), F32)]
        + [_shift_scratch(tm, aw)] * 2,
        aliases={7: 0}, sem=("arbitrary",))


def rope_tables(s):
    pos = jnp.arange(s, dtype=F32)
    inv_freq = ROPE_THETA ** (-jnp.arange(0, ROT_DIM, 2, dtype=F32) / ROT_DIM)
    ang = pos[:, None] * inv_freq[None, :]
    cos, sin = jnp.cos(ang), jnp.sin(ang)
    half = ROT_DIM // 2
    rest = HEAD_DIM - ROT_DIM
    one, zero, zrest = jnp.ones((s, rest), F32), jnp.zeros((s, half), F32), jnp.zeros((s, rest), F32)
    c = jnp.concatenate([cos, cos, one], axis=1)
    s1 = jnp.concatenate([-sin, zero, zrest], axis=1)
    s2 = jnp.concatenate([zero, sin, zrest], axis=1)
    return tuple(jnp.tile(t, (1, LANES // HEAD_DIM)) for t in (c, s1, s2))


def qk_prep_forward(qkv, tabs, gains, name, comm=None):
    s, w3 = qkv.shape
    w = w3 // 3
    tm = _tile(s, 512)

    def body(x_ref, c_ref, s1_ref, s2_ref, g_ref, o_ref):
        bd = _block_diag(LANES)
        for rb in range(tm // CHUNK):
            rows = pl.ds(rb * CHUNK, CHUNK)
            c, s1, s2 = c_ref[rows, :], s1_ref[rows, :], s2_ref[rows, :]
            for b in range(w // LANES):
                cols = pl.ds(b * LANES, LANES)
                t = x_ref[rows, cols]
                r = lax.rsqrt(_seg_sum(t * t, bd) * (1.0 / HEAD_DIM) + EPS)
                y = t * r * g_ref[...]
                o_ref[rows, cols] = (y * c + pltpu.roll(y, LANES - ROT_DIM // 2, 1) * s1
                                     + pltpu.roll(y, ROT_DIM // 2, 1) * s2)

    tab = pl.BlockSpec((tm, LANES), lambda i, p: (i, 0))
    outs, couts = _pallas(
        body, name=name, grid=(s // tm, 2), args=[qkv, *tabs, gains], comm=comm,
        in_specs=[pl.BlockSpec((tm, w), lambda i, p: (i, p)), tab, tab, tab,
                  pl.BlockSpec((None, 1, LANES), lambda i, p: (p, 0, 0))],
        out_specs=[pl.BlockSpec((tm, w), lambda i, p: (i, p))],
        out_shape=[jax.ShapeDtypeStruct((s, 2 * w), F32)],
        sem=("parallel", "arbitrary"))
    return outs[0], couts


def qk_prep_backward(dq, dq_prev, dq_next, dk, dv, qkv, tabs, gains, name):
    s, w3 = qkv.shape
    w = w3 // 3
    tm = _tile(s, 512)
    t = _tile(s, ATT_TILE)
    per_tile, per_halo, tiles = t // tm, ATT_HALO // tm, s // t

    def neighbours(i):
        tile, c = i // per_tile, i % per_tile
        from_before = (c < per_halo) & (tile >= 1)
        from_after = (c >= per_tile - per_halo) & (tile + 1 < tiles)
        return (from_before, from_after, jnp.where(from_before, (tile - 1) * per_halo + c, 0),
                jnp.where(from_after, (tile + 1) * per_halo + c - (per_tile - per_halo), 0))

    def body(*refs):
        grads = ((refs[0], refs[1], refs[2]), (refs[3],), (refs[4],))
        x_ref, c_ref, s1_ref, s2_ref, g_ref, o_ref, dg_ref = refs[5:]
        part, first = pl.program_id(0), pl.program_id(1) == 0
        from_before, from_after, _, _ = neighbours(pl.program_id(1))

        def normed(ds):
            bd = _block_diag(LANES)
            acc = jnp.zeros((1, LANES), F32)
            for rb in range(tm // CHUNK):
                rows = pl.ds(rb * CHUNK, CHUNK)
                c, s1, s2 = c_ref[rows, :], s1_ref[rows, :], s2_ref[rows, :]
                for b in range(w // LANES):
                    cols = pl.ds(b * LANES, LANES)
                    dout = ds[0][rows, cols]
                    if len(ds) == 3:
                        dout = (dout + jnp.where(from_after, ds[1][rows, cols], 0.0)
                                + jnp.where(from_before, ds[2][rows, cols], 0.0))
                    dy = (dout * c + pltpu.roll(dout * s1, ROT_DIM // 2, 1)
                          + pltpu.roll(dout * s2, LANES - ROT_DIM // 2, 1))
                    t = x_ref[rows, cols]
                    r = lax.rsqrt(_seg_sum(t * t, bd) * (1.0 / HEAD_DIM) + EPS)
                    xh = t * r
                    acc = acc + jnp.sum(dy * xh, axis=0, keepdims=True)
                    tt = dy * g_ref[...]
                    o_ref[rows, cols] = (r * (tt - xh * (_seg_sum(tt * xh, bd) * (1.0 / HEAD_DIM)))).astype(BF16)
            _accumulate(dg_ref, acc, first)

        for p in range(2):
            pl.when(part == p)(functools.partial(normed, grads[p]))

        @pl.when(part == 2)
        def _():
            o_ref[...] = grads[2][0][...].astype(BF16)
            _accumulate(dg_ref, jnp.zeros((1, LANES), F32), first)

    def gspec(p):
        return pl.BlockSpec((tm, w), lambda q, i: (jnp.where(q == p, i, 0), 0))

    prev_spec = pl.BlockSpec((tm, w), lambda q, i: (jnp.where(q == 0, neighbours(i)[3], 0), 0))
    next_spec = pl.BlockSpec((tm, w), lambda q, i: (jnp.where(q == 0, neighbours(i)[2], 0), 0))
    tab = pl.BlockSpec((tm, LANES), lambda q, i: (i, 0))
    return pl.pallas_call(
        body, name=name, grid=(3, s // tm),
        in_specs=[gspec(0), prev_spec, next_spec, gspec(1), gspec(2)]
        + [pl.BlockSpec((tm, w), lambda q, i: (i, q)), tab, tab, tab,
           pl.BlockSpec((None, 1, LANES), lambda q, i: (q, 0, 0))],
        out_specs=[pl.BlockSpec((tm, w), lambda q, i: (i, q)), pl.BlockSpec((None, 1, LANES), lambda q, i: (q, 0, 0))],
        out_shape=[jax.ShapeDtypeStruct((s, w3), BF16), jax.ShapeDtypeStruct((3, 1, LANES), F32)],
        compiler_params=_params("arbitrary", "arbitrary"),
    )(dq, dq_prev, dq_next, dk, dv, qkv, *tabs, gains)


def _window_specs(tq, l, col_fn):
    hb, nhb = tq // BAND, l // BAND
    return [pl.BlockSpec((tq, LANES), lambda c, i: (i, col_fn(c))),
            pl.BlockSpec((BAND, LANES), lambda c, i: (jnp.maximum(i * hb - 1, 0), col_fn(c))),
            pl.BlockSpec((BAND, LANES), lambda c, i: (jnp.minimum((i + 1) * hb, nhb - 1), col_fn(c)))]


def _window(cur_ref, prev_ref, next_ref):
    return jnp.concatenate([prev_ref[...], cur_ref[...], next_ref[...]], axis=0)


def _band_mask(shape, centre_axis, first_row, length):
    ctr = lax.broadcasted_iota(jnp.int32, shape, centre_axis)
    win = lax.broadcasted_iota(jnp.int32, shape, 1 - centre_axis)
    row = first_row + win
    return (jnp.abs(win - BAND - ctr) <= BAND) & (row >= 0) & (row < length)


def _col_q(d):
    return lambda c: (c // 8) * 24 + c % 8


def _col_k(d):
    return lambda c: (c // 8) * 24 + 8 + c % 8


def _col_v(d):
    return lambda c: (c // 8) * 24 + 16 + c % 8


def band_attention_forward(qkvn, d, name):
    s, w3 = qkvn.shape
    w = w3 // 3
    l = s // d
    tq = _tile(l, 512)
    scale = HEAD_DIM ** -0.5
    xv = qkvn.reshape(l, d * w3)

    def body(q_ref, k_ref, kp_ref, kn_ref, v_ref, vp_ref, vn_ref, o_ref, lse_ref):
        i = pl.program_id(1)
        kw, vw = _window(k_ref, kp_ref, kn_ref), _window(v_ref, vp_ref, vn_ref)
        head0 = lax.broadcasted_iota(jnp.int32, (CHUNK, LANES), 1) < HEAD_DIM
        for b in range(tq // CHUNK):
            rows = pl.ds(b * CHUNK, CHUNK)
            mask = _band_mask((CHUNK, 2 * CHUNK), 0, i * tq + b * CHUNK - BAND, l)
            qb = q_ref[rows, :]
            kb, vb = kw[b * CHUNK:(b + 2) * CHUNK], vw[b * CHUNK:(b + 2) * CHUNK]
            outs, lses = [], []
            for hm in (head0, jnp.logical_not(head0)):
                sc = jnp.where(mask, _dot_nt(jnp.where(hm, qb, jnp.zeros_like(qb)), kb) * scale, NEG)
                m = jnp.max(sc, axis=1, keepdims=True)
                p = jnp.exp(sc - m)
                den = jnp.sum(p, axis=1, keepdims=True)
                outs.append(_dot(p.astype(BF16), vb) / den)
                lses.append(jnp.broadcast_to(m + jnp.log(den), (CHUNK, LANES)))
            o_ref[rows, :] = jnp.where(head0, outs[0], outs[1]).astype(BF16)
            lse_ref[rows, :] = jnp.where(head0, lses[0], lses[1])

    ospec = pl.BlockSpec((tq, LANES), lambda c, i: (i, c))
    o, lse = pl.pallas_call(
        body, name=name, grid=(d * w // LANES, l // tq),
        in_specs=[pl.BlockSpec((tq, LANES), lambda c, i: (i, _col_q(d)(c)))]
        + _window_specs(tq, l, _col_k(d)) + _window_specs(tq, l, _col_v(d)),
        out_specs=[ospec, ospec],
        out_shape=[jax.ShapeDtypeStruct((l, d * w), BF16), jax.ShapeDtypeStruct((l, d * w), F32)],
        compiler_params=_params("parallel", "parallel"),
    )(xv, xv, xv, xv, xv, xv, xv)
    return o.reshape(s, w), lse.reshape(s, w)


def attention_merge(os_, lses, name):
    s, w = os_[0].shape
    tm = _tile(s, 512)

    def body(o0, o1, o2, l0, l1, l2, o_ref, lse_ref):
        la, lb, lc = l0[...], l1[...], l2[...]
        m = jnp.maximum(jnp.maximum(la, lb), lc)
        wa, wb, wc = jnp.exp(la - m), jnp.exp(lb - m), jnp.exp(lc - m)
        den = wa + wb + wc
        o = (wa * o0[...].astype(F32) + wb * o1[...].astype(F32) + wc * o2[...].astype(F32)) / den
        o_ref[...] = o.astype(BF16)
        lse_ref[...] = m + jnp.log(den)

    spec = pl.BlockSpec((tm, w), lambda i: (i, 0))
    return pl.pallas_call(
        body, name=name, grid=(s // tm,), in_specs=[spec] * 6, out_specs=[spec, spec],
        out_shape=[jax.ShapeDtypeStruct((s, w), BF16), jax.ShapeDtypeStruct((s, w), F32)],
        compiler_params=_params("parallel"),
    )(*os_, *lses)


def attention_delta(do, o, name):
    s, w = do.shape
    tm = _tile(s, 512)

    def body(do_ref, o_ref, dl_ref, dob_ref):
        bd = _block_diag(LANES)
        for b in range(w // LANES):
            cols = pl.ds(b * LANES, LANES)
            dv = do_ref[:, cols]
            dl_ref[:, cols] = _seg_sum(dv * o_ref[:, cols].astype(F32), bd)
            dob_ref[:, cols] = dv.astype(BF16)

    spec = pl.BlockSpec((tm, w), lambda i: (i, 0))
    return pl.pallas_call(
        body, name=name, grid=(s // tm,), in_specs=[spec, spec], out_specs=[spec, spec],
        out_shape=[jax.ShapeDtypeStruct((s, w), F32), jax.ShapeDtypeStruct((s, w), BF16)],
        compiler_params=_params("parallel"),
    )(do, o)


def band_attention_dq(qkvn, dob, lse, delta, d, name):
    s, w3 = qkvn.shape
    w = w3 // 3
    l = s // d
    tq = _tile(l, 512)
    scale = HEAD_DIM ** -0.5
    xv = qkvn.reshape(l, d * w3)

    def body(q_ref, k_ref, kp_ref, kn_ref, v_ref, vp_ref, vn_ref, do_ref, lse_ref, dl_ref, dq_ref):
        i = pl.program_id(1)
        kw, vw = _window(k_ref, kp_ref, kn_ref), _window(v_ref, vp_ref, vn_ref)
        head0 = lax.broadcasted_iota(jnp.int32, (CHUNK, LANES), 1) < HEAD_DIM
        for b in range(tq // CHUNK):
            rows = pl.ds(b * CHUNK, CHUNK)
            mask = _band_mask((CHUNK, 2 * CHUNK), 0, i * tq + b * CHUNK - BAND, l)
            qb, dob_ = q_ref[rows, :], do_ref[rows, :]
            kb, vb = kw[b * CHUNK:(b + 2) * CHUNK], vw[b * CHUNK:(b + 2) * CHUNK]
            outs = []
            for h, hm in enumerate((head0, jnp.logical_not(head0))):
                col = pl.ds(h * HEAD_DIM, 1)
                sc = jnp.where(mask, _dot_nt(jnp.where(hm, qb, jnp.zeros_like(qb)), kb) * scale, NEG)
                p = jnp.exp(sc - lse_ref[rows, col])
                dp = _dot_nt(jnp.where(hm, dob_, jnp.zeros_like(dob_)), vb)
                ds = p * (dp - dl_ref[rows, col]) * scale
                outs.append(_dot(ds.astype(BF16), kb))
            dq_ref[rows, :] = jnp.where(head0, outs[0], outs[1])

    ospec = pl.BlockSpec((tq, LANES), lambda c, i: (i, c))
    dq = pl.pallas_call(
        body, name=name, grid=(d * w // LANES, l // tq),
        in_specs=[pl.BlockSpec((tq, LANES), lambda c, i: (i, _col_q(d)(c)))]
        + _window_specs(tq, l, _col_k(d)) + _window_specs(tq, l, _col_v(d)) + [ospec, ospec, ospec],
        out_specs=ospec,
        out_shape=jax.ShapeDtypeStruct((l, d * w), F32),
        compiler_params=_params("parallel", "parallel"),
    )(xv, xv, xv, xv, xv, xv, xv, dob.reshape(l, d * w), lse.reshape(l, d * w), delta.reshape(l, d * w))
    return dq.reshape(s, w)


def band_attention_dkv(qkvn, dob, lse, delta, d, name):
    s, w3 = qkvn.shape
    w = w3 // 3
    l = s // d
    tq = _tile(l, 512)
    scale = HEAD_DIM ** -0.5
    xv = qkvn.reshape(l, d * w3)

    def body(k_ref, v_ref, q_ref, qp_ref, qn_ref, do_ref, dop_ref, don_ref, lse_ref, lsep_ref, lsen_ref,
             dl_ref, dlp_ref, dln_ref, dk_ref, dv_ref):
        i = pl.program_id(1)
        qw, dow = _window(q_ref, qp_ref, qn_ref), _window(do_ref, dop_ref, don_ref)
        lsew, dlw = _window(lse_ref, lsep_ref, lsen_ref), _window(dl_ref, dlp_ref, dln_ref)
        head0 = lax.broadcasted_iota(jnp.int32, (2 * CHUNK, LANES), 1) < HEAD_DIM
        for b in range(tq // CHUNK):
            rows = pl.ds(b * CHUNK, CHUNK)
            mask = _band_mask((2 * CHUNK, CHUNK), 1, i * tq + b * CHUNK - BAND, l)
            kb, vb = k_ref[rows, :], v_ref[rows, :]
            win = slice(b * CHUNK, (b + 2) * CHUNK)
            qb, dob_, lseb, dlb = qw[win], dow[win], lsew[win], dlw[win]
            dk = jnp.zeros((CHUNK, LANES), F32)
            dv = jnp.zeros((CHUNK, LANES), F32)
            for h, hm in enumerate((head0, jnp.logical_not(head0))):
                col = slice(h * HEAD_DIM, h * HEAD_DIM + 1)
                qm = jnp.where(hm, qb, jnp.zeros_like(qb))
                dom = jnp.where(hm, dob_, jnp.zeros_like(dob_))
                sc = jnp.where(mask, _dot_nt(qm, kb) * scale, NEG)
                p = jnp.exp(sc - lseb[:, col])
                ds = p * (_dot_nt(dom, vb) - dlb[:, col]) * scale
                dv = dv + _dot_tn(p.astype(BF16), dom)
                dk = dk + _dot_tn(ds.astype(BF16), qm)
            dk_ref[rows, :] = dk
            dv_ref[rows, :] = dv

    ident = lambda c: c
    ospec = pl.BlockSpec((tq, LANES), lambda c, i: (i, c))
    dk, dv = pl.pallas_call(
        body, name=name, grid=(d * w // LANES, l // tq),
        in_specs=[pl.BlockSpec((tq, LANES), lambda c, i: (i, _col_k(d)(c))),
                  pl.BlockSpec((tq, LANES), lambda c, i: (i, _col_v(d)(c)))]
        + _window_specs(tq, l, _col_q(d)) + _window_specs(tq, l, ident) * 3,
        out_specs=[ospec, ospec],
        out_shape=[jax.ShapeDtypeStruct((l, d * w), F32)] * 2,
        compiler_params=_params("parallel", "parallel"),
    )(xv, xv, xv, xv, xv, *[dob.reshape(l, d * w)] * 3, *[lse.reshape(l, d * w)] * 3, *[delta.reshape(l, d * w)] * 3)
    return dk.reshape(s, w), dv.reshape(s, w)


ATT_TILE = 2048
ATT_HALO = BAND * max(DILATIONS)
ROWS_PER_COPY = 256


def _att_specs(s, t, col_fn, halo=True):
    hb, nhb = t // ATT_HALO, s // ATT_HALO
    specs = [pl.BlockSpec((t, LANES), lambda hp, i: (i, col_fn(hp)))]
    if halo:
        specs += [pl.BlockSpec((ATT_HALO, LANES), lambda hp, i: (jnp.maximum(i * hb - 1, 0), col_fn(hp))),
                  pl.BlockSpec((ATT_HALO, LANES), lambda hp, i: (jnp.minimum((i + 1) * hb, nhb - 1), col_fn(hp)))]
    return specs


def _gather_rows(dst_ref, dst_row, src_ref, start, count, stride, scale=None):
    for c in range(0, count, ROWS_PER_COPY):
        m = min(ROWS_PER_COPY, count - c)
        v = src_ref[pl.ds(start + c * stride, m, stride=stride), :]
        if scale is not None:
            v = v * scale
        dst_ref[dst_row + c:dst_row + c + m, :] = v.astype(dst_ref.dtype)


SPLIT = 4


def _split_rows(tmp_ref, base, src_ref, rows):
    part = rows // SPLIT
    for b in range(SPLIT):
        _gather_rows(tmp_ref, base + b * part, src_ref, b, part, SPLIT)


def _stage(dst_ref, cur_ref, d, t, scale=None, tmp_ref=None):
    n = t // d
    if tmp_ref is None or d != SPLIT * SPLIT:
        for r in range(d):
            _gather_rows(dst_ref, r * n, cur_ref, r, n, d, scale)
        return
    _split_rows(tmp_ref, 0, cur_ref, t)
    for r in range(d):
        _gather_rows(dst_ref, r * n, tmp_ref, (r % SPLIT) * (t // SPLIT) + r // SPLIT, n, SPLIT, scale)


def _stage_window(dst_ref, refs, d, t, scale=None, edges=None, tmp_ref=None):
    cur_ref, prev_ref, next_ref = refs
    n = t // d
    nw = n + 2 * BAND
    if tmp_ref is None or d != SPLIT * SPLIT:
        for r in range(d):
            _gather_rows(dst_ref, r * nw, prev_ref, ATT_HALO - BAND * d + r, BAND, d, scale)
            _gather_rows(dst_ref, r * nw + BAND, cur_ref, r, n, d, scale)
            _gather_rows(dst_ref, r * nw + BAND + n, next_ref, r, BAND, d, scale)
    else:
        assert ATT_HALO == BAND * d
        _split_rows(tmp_ref, 0, cur_ref, t)
        _split_rows(tmp_ref, t, prev_ref, ATT_HALO)
        _split_rows(tmp_ref, t + ATT_HALO, next_ref, ATT_HALO)
        for r in range(d):
            a, b = r // SPLIT, r % SPLIT
            _gather_rows(dst_ref, r * nw, tmp_ref, t + b * (ATT_HALO // SPLIT) + a, BAND, SPLIT, scale)
            _gather_rows(dst_ref, r * nw + BAND, tmp_ref, b * (t // SPLIT) + a, n, SPLIT, scale)
            _gather_rows(dst_ref, r * nw + BAND + n, tmp_ref, t + ATT_HALO + b * (ATT_HALO // SPLIT) + a, BAND, SPLIT, scale)
    if edges is not None:
        first, last, value = edges
        fill = jnp.full((BAND, LANES), value, dst_ref.dtype)

        @pl.when(first)
        def _():
            for r in range(d):
                dst_ref[r * nw:r * nw + BAND, :] = fill

        @pl.when(last)
        def _():
            for r in range(d):
                dst_ref[r * nw + BAND + n:(r + 1) * nw, :] = fill


def _band_bias(rows, cols, centre_axis):
    shape = (rows // 2, cols)
    ctr = lax.broadcasted_iota(jnp.int32, shape, centre_axis)
    win = lax.broadcasted_iota(jnp.int32, shape, 1 - centre_axis)
    bias = jnp.where(jnp.abs(win - BAND - ctr) <= BAND, 0.0, NEG).astype(F32)
    return jnp.concatenate([bias, bias], axis=0)


def _window_bias(first_row, length):
    row = first_row + lax.broadcasted_iota(jnp.int32, (1, 2 * CHUNK), 1)
    return jnp.where((row >= 0) & (row < length), 0.0, NEG).astype(F32)


UNITS_PER_TRIP = 8


def _scatter_rows(dst_ref, src_ref, d, t, combine):
    n = t // d
    for r in range(d):
        for c in range(0, n, ROWS_PER_COPY):
            m = min(ROWS_PER_COPY, n - c)
            idx = pl.ds(r + c * d, m, stride=d)
            combine(idx, slice(r * n + c, r * n + c + m))


def _unit_rows(u, n):
    upr = n // CHUNK
    r = u // upr
    b = u - r * upr
    return pl.multiple_of(u * CHUNK, CHUNK), pl.multiple_of((u + r) * CHUNK, CHUNK), b * CHUNK - BAND


def _two_heads(x):
    head0 = lax.broadcasted_iota(jnp.int32, x.shape, 1) < HEAD_DIM
    zero = jnp.zeros_like(x)
    return jnp.concatenate([jnp.where(head0, x, zero), jnp.where(head0, zero, x)], axis=0)


def _head_columns(x):
    return jnp.concatenate([x[:, 0:1], x[:, HEAD_DIM:HEAD_DIM + 1]], axis=0)


def _merge_heads(x2):
    rows = x2.shape[0] // 2
    head0 = lax.broadcasted_iota(jnp.int32, (rows, LANES), 1) < HEAD_DIM
    return jnp.where(head0, jnp.broadcast_to(x2[:rows], (rows, LANES)), jnp.broadcast_to(x2[rows:], (rows, LANES)))


def _col(part):
    return lambda hp: part * 8 + hp


def attention_forward(qk, qkv, name, comm=None):
    s, w2 = qk.shape
    w = w2 // 2
    t = _tile(s, ATT_TILE)
    scale = HEAD_DIM ** -0.5

    def body(q_ref, k_ref, kp_ref, kn_ref, v_ref, vp_ref, vn_ref, o_ref, lse_ref,
             qs_ref, ks_ref, vs_ref, os_ref, ls_ref, or_ref, tmp_ref):
        i = pl.program_id(1)
        band = _band_bias(2 * CHUNK, 2 * CHUNK, 0)
        for pi, d in enumerate(DILATIONS):
            n = t // d
            _stage(qs_ref, q_ref, d, t, scale, tmp_ref=tmp_ref)
            _stage_window(ks_ref, (k_ref, kp_ref, kn_ref), d, t, tmp_ref=tmp_ref)
            _stage_window(vs_ref, (v_ref, vp_ref, vn_ref), d, t, tmp_ref=tmp_ref)

            def unit(u, carry, n=n, d=d):
                qrow, wrow, first = _unit_rows(u, n)
                kb, vb = ks_ref[pl.ds(wrow, 2 * CHUNK), :], vs_ref[pl.ds(wrow, 2 * CHUNK), :]
                sc = _dot_nt(_two_heads(qs_ref[pl.ds(qrow, CHUNK), :]), kb) + band + _window_bias(i * n + first, s // d)
                m = jnp.max(sc, axis=1, keepdims=True)
                p = jnp.exp(sc - m)
                den = jnp.sum(p, axis=1, keepdims=True)
                os_ref[pl.ds(qrow, CHUNK), :] = _merge_heads(_dot(p.astype(BF16), vb) / den)
                ls_ref[pl.ds(qrow, CHUNK), :] = _merge_heads(m + jnp.log(den))
                return carry

            lax.fori_loop(0, t // CHUNK, unit, 0, unroll=UNITS_PER_TRIP)

            if pi == 0:
                def assign(idx, rows):
                    or_ref[idx, :] = os_ref[rows, :]
                    lse_ref[idx, :] = ls_ref[rows, :]
                _scatter_rows(None, None, d, t, assign)
            else:
                def merge(idx, rows):
                    la, lb = lse_ref[idx, :], ls_ref[rows, :]
                    mx = jnp.maximum(la, lb)
                    wa, wb = jnp.exp(la - mx), jnp.exp(lb - mx)
                    den = wa + wb
                    or_ref[idx, :] = (wa * or_ref[idx, :] + wb * os_ref[rows, :]) / den
                    lse_ref[idx, :] = mx + jnp.log(den)
                _scatter_rows(None, None, d, t, merge)
        o_ref[...] = or_ref[...].astype(BF16)

    ospec = pl.BlockSpec((t, LANES), lambda hp, i: (i, hp))
    win_rows = t + 2 * ATT_HALO
    return _pallas(
        body, name=name, grid=(w // LANES, s // t), args=[qk, qk, qk, qk, qkv, qkv, qkv], comm=comm,
        in_specs=_att_specs(s, t, _col(0), halo=False) + _att_specs(s, t, _col(1)) + _att_specs(s, t, _col(2)),
        out_specs=[ospec, ospec],
        out_shape=[jax.ShapeDtypeStruct((s, w), BF16), jax.ShapeDtypeStruct((s, w), F32)],
        scratch=[pltpu.VMEM((t, LANES), BF16), pltpu.VMEM((win_rows, LANES), BF16), pltpu.VMEM((win_rows, LANES), BF16),
                 pltpu.VMEM((t, LANES), F32), pltpu.VMEM((t, LANES), F32), pltpu.VMEM((t, LANES), F32),
                 pltpu.VMEM((win_rows, LANES), F32)],
        sem=("parallel", "parallel"))


def attention_delta(do, o, lse, name):
    s, w = do.shape
    tm = _tile(s, 512)

    def body(do_ref, o_ref, lse_ref, st_ref):
        bd = _block_diag(LANES)
        lane = lax.broadcasted_iota(jnp.int32, (CHUNK, LANES), 1)
        for rb in range(tm // CHUNK):
            rows = pl.ds(rb * CHUNK, CHUNK)
            for b in range(w // LANES):
                cols = pl.ds(b * LANES, LANES)
                dl = _seg_sum(do_ref[rows, cols] * o_ref[rows, cols].astype(F32), bd)
                ls = lse_ref[rows, cols]
                st_ref[rows, cols] = jnp.where(lane == 0, ls, jnp.where(
                    lane == 1, pltpu.roll(ls, HEAD_DIM - 1, 1), jnp.where(
                        lane == 2, pltpu.roll(dl, 2, 1), pltpu.roll(dl, HEAD_DIM + 3, 1))))

    spec = pl.BlockSpec((tm, w), lambda i: (i, 0))
    return pl.pallas_call(
        body, name=name, grid=(s // tm,), in_specs=[spec, spec, spec], out_specs=spec,
        out_shape=jax.ShapeDtypeStruct((s, w), F32), compiler_params=_params("parallel"),
    )(do, o, lse)


def attention_dq(qk, qkv, do, lse, delta, name, comm=None):
    s, w2 = qk.shape
    w = w2 // 2
    t = _tile(s, ATT_TILE)
    scale = HEAD_DIM ** -0.5

    def body(q_ref, k_ref, kp_ref, kn_ref, v_ref, vp_ref, vn_ref, do_ref, lse_ref, dl_ref, dq_ref,
             qs_ref, ks_ref, vs_ref, dos_ref, ls_ref, dls_ref, dqs_ref):
        i = pl.program_id(1)
        band = _band_bias(2 * CHUNK, 2 * CHUNK, 0)
        for pi, d in enumerate(DILATIONS):
            n = t // d
            _stage(qs_ref, q_ref, d, t, scale)
            _stage(dos_ref, do_ref, d, t)
            _stage(ls_ref, lse_ref, d, t)
            _stage(dls_ref, dl_ref, d, t)
            _stage_window(ks_ref, (k_ref, kp_ref, kn_ref), d, t)
            _stage_window(vs_ref, (v_ref, vp_ref, vn_ref), d, t)

            def unit(u, carry, n=n, d=d):
                qrow, wrow, first = _unit_rows(u, n)
                rows = pl.ds(qrow, CHUNK)
                kb, vb = ks_ref[pl.ds(wrow, 2 * CHUNK), :], vs_ref[pl.ds(wrow, 2 * CHUNK), :]
                sc = _dot_nt(_two_heads(qs_ref[rows, :]), kb) + band + _window_bias(i * n + first, s // d)
                p = jnp.exp(sc - _head_columns(ls_ref[rows, :]))
                dp = _dot_nt(_two_heads(dos_ref[rows, :]), vb)
                ds = p * (dp - _head_columns(dls_ref[rows, :]))
                dqs_ref[rows, :] = _merge_heads(_dot(ds.astype(BF16), kb)) * scale
                return carry

            lax.fori_loop(0, t // CHUNK, unit, 0, unroll=UNITS_PER_TRIP)

            def add(idx, rows, pi=pi):
                dq_ref[idx, :] = dqs_ref[rows, :] if pi == 0 else dq_ref[idx, :] + dqs_ref[rows, :]
            _scatter_rows(None, None, d, t, add)

    ospec = pl.BlockSpec((t, LANES), lambda hp, i: (i, hp))
    win_rows = t + 2 * ATT_HALO
    return _pallas(
        body, name=name, grid=(w // LANES, s // t), args=[qk, qk, qk, qk, qkv, qkv, qkv, do, lse, delta], comm=comm,
        in_specs=_att_specs(s, t, _col(0), halo=False) + _att_specs(s, t, _col(1)) + _att_specs(s, t, _col(2))
        + [ospec, ospec, ospec],
        out_specs=[ospec],
        out_shape=[jax.ShapeDtypeStruct((s, w), F32)],
        scratch=[pltpu.VMEM((t, LANES), BF16), pltpu.VMEM((win_rows, LANES), BF16), pltpu.VMEM((win_rows, LANES), BF16),
                 pltpu.VMEM((t, LANES), BF16), pltpu.VMEM((t, LANES), F32), pltpu.VMEM((t, LANES), F32),
                 pltpu.VMEM((t, LANES), F32)],
        sem=("parallel", "parallel"))


def attention_dkv(qk, qkv, do, stats, name):
    s, w2 = qk.shape
    w = w2 // 2
    t = _tile(s, ATT_TILE)
    scale = HEAD_DIM ** -0.5

    def body(k_ref, v_ref, q_ref, qp_ref, qn_ref, do_ref, dop_ref, don_ref, st_ref, stp_ref, stn_ref,
             dk_ref, dv_ref, ks_ref, vs_ref, qs_ref, dos_ref, sts_ref, dks_ref, dvs_ref):
        i = pl.program_id(1)
        key = lax.broadcasted_iota(jnp.int32, (CHUNK, 2 * CHUNK), 0)
        win_ = lax.broadcasted_iota(jnp.int32, (CHUNK, 2 * CHUNK), 1)
        half = jnp.where(jnp.abs(win_ - BAND - key) <= BAND, 0.0, NEG).astype(F32)
        band = jnp.concatenate([half, half], axis=1)
        edges = (i == 0, i == s // t - 1, -NEG)
        for pi, d in enumerate(DILATIONS):
            n = t // d
            _stage(ks_ref, k_ref, d, t)
            _stage(vs_ref, v_ref, d, t)
            _stage_window(qs_ref, (q_ref, qp_ref, qn_ref), d, t, scale)
            _stage_window(dos_ref, (do_ref, dop_ref, don_ref), d, t)
            _stage_window(sts_ref, (st_ref, stp_ref, stn_ref), d, t, edges=edges)

            def unit(u, carry, n=n, d=d):
                krow, wrow, _ = _unit_rows(u, n)
                rows, win = pl.ds(krow, CHUNK), pl.ds(wrow, 2 * CHUNK)
                q2, do2 = _two_heads(qs_ref[win, :]), _two_heads(dos_ref[win, :])
                st = jnp.transpose(sts_ref[win, :])
                lse2 = jnp.concatenate([st[0:1, :], st[1:2, :]], axis=1)
                dl2 = jnp.concatenate([st[2:3, :], st[3:4, :]], axis=1)
                p = jnp.exp(_dot_nt(ks_ref[rows, :], q2) + band - lse2)
                ds = p * (_dot_nt(vs_ref[rows, :], do2) - dl2)
                dvs_ref[rows, :] = _dot(p.astype(BF16), do2)
                dks_ref[rows, :] = _dot(ds.astype(BF16), q2)
                return carry

            lax.fori_loop(0, t // CHUNK, unit, 0, unroll=UNITS_PER_TRIP)

            def add(idx, rows, pi=pi):
                dk_ref[idx, :] = dks_ref[rows, :] if pi == 0 else dk_ref[idx, :] + dks_ref[rows, :]
                dv_ref[idx, :] = dvs_ref[rows, :] if pi == 0 else dv_ref[idx, :] + dvs_ref[rows, :]
            _scatter_rows(None, None, d, t, add)

    ident = lambda hp: hp
    ospec = pl.BlockSpec((t, LANES), lambda hp, i: (i, hp))
    win_rows = t + 2 * ATT_HALO
    return pl.pallas_call(
        body, name=name, grid=(w // LANES, s // t),
        in_specs=_att_specs(s, t, _col(1), halo=False) + _att_specs(s, t, _col(2), halo=False)
        + _att_specs(s, t, _col(0)) + _att_specs(s, t, ident) * 2,
        out_specs=[ospec, ospec],
        out_shape=[jax.ShapeDtypeStruct((s, w), F32)] * 2,
        scratch_shapes=[pltpu.VMEM((t, LANES), BF16), pltpu.VMEM((t, LANES), BF16),
                        pltpu.VMEM((win_rows, LANES), BF16), pltpu.VMEM((win_rows, LANES), BF16),
                        pltpu.VMEM((win_rows, LANES), F32),
                        pltpu.VMEM((t, LANES), F32), pltpu.VMEM((t, LANES), F32)],
        compiler_params=_params("parallel", "parallel"),
    )(qk, qkv, qk, qk, qk, do, do, do, stats, stats, stats)


def attention_backward(qk, qkv, do, stats, name, comm=None):
    s, w2 = qk.shape
    w = w2 // 2
    t = _tile(s, ATT_TILE)
    tiles = s // t
    scale = HEAD_DIM ** -0.5

    def body(k_ref, v_ref, q_ref, qp_ref, qn_ref, do_ref, dop_ref, don_ref, st_ref, stp_ref, stn_ref,
             dk_ref, dv_ref, dq_ref, dqp_ref, dqn_ref, ks_ref, vs_ref, qs_ref, dos_ref, sts_ref, dks_ref, dvs_ref, dqw_ref):
        i = pl.program_id(1)
        key = lax.broadcasted_iota(jnp.int32, (CHUNK, 2 * CHUNK), 0)
        win_ = lax.broadcasted_iota(jnp.int32, (CHUNK, 2 * CHUNK), 1)
        half = jnp.where(jnp.abs(win_ - BAND - key) <= BAND, 0.0, NEG).astype(F32)
        band = jnp.concatenate([half, half], axis=1)
        edges = (i == 0, i == tiles - 1, -NEG)
        dqp_ref[...] = jnp.zeros_like(dqp_ref)
        dqn_ref[...] = jnp.zeros_like(dqn_ref)
        for pi, d in enumerate(DILATIONS):
            n = t // d
            nw = n + 2 * BAND
            _stage(ks_ref, k_ref, d, t, tmp_ref=dqw_ref)
            _stage(vs_ref, v_ref, d, t, tmp_ref=dqw_ref)
            _stage_window(qs_ref, (q_ref, qp_ref, qn_ref), d, t, scale, tmp_ref=dqw_ref)
            _stage_window(dos_ref, (do_ref, dop_ref, don_ref), d, t, tmp_ref=dqw_ref)
            _stage_window(sts_ref, (st_ref, stp_ref, stn_ref), d, t, edges=edges, tmp_ref=dqw_ref)
            dqw_ref[...] = jnp.zeros_like(dqw_ref)

            def unit(u, carry, n=n, d=d):
                krow, wrow, _ = _unit_rows(u, n)
                rows, win = pl.ds(krow, CHUNK), pl.ds(wrow, 2 * CHUNK)
                kb = ks_ref[rows, :]
                q2, do2 = _two_heads(qs_ref[win, :]), _two_heads(dos_ref[win, :])
                st = jnp.transpose(sts_ref[win, :])
                lse2 = jnp.concatenate([st[0:1, :], st[1:2, :]], axis=1)
                dl2 = jnp.concatenate([st[2:3, :], st[3:4, :]], axis=1)
                p = jnp.exp(_dot_nt(kb, q2) + band - lse2)
                ds = (p * (_dot_nt(vs_ref[rows, :], do2) - dl2)).astype(BF16)
                dvs_ref[rows, :] = _dot(p.astype(BF16), do2)
                dks_ref[rows, :] = _dot(ds, q2)
                k2 = _two_heads(kb)
                dqw = _dot_tn(ds[:, :2 * CHUNK], k2[:CHUNK]) + _dot_tn(ds[:, 2 * CHUNK:], k2[CHUNK:])
                dqw_ref[win, :] += dqw * scale
                return carry

            lax.fori_loop(0, t // CHUNK, unit, 0, unroll=UNITS_PER_TRIP)

            def add(idx, rows, pi=pi):
                dk_ref[idx, :] = dks_ref[rows, :] if pi == 0 else dk_ref[idx, :] + dks_ref[rows, :]
                dv_ref[idx, :] = dvs_ref[rows, :] if pi == 0 else dv_ref[idx, :] + dvs_ref[rows, :]
            _scatter_rows(None, None, d, t, add)

            for r in range(d):
                before = pl.ds(ATT_HALO - BAND * d + r, BAND, stride=d)
                after = pl.ds(r, BAND, stride=d)
                dqp_ref[before, :] += dqw_ref[r * nw:r * nw + BAND, :]
                dqn_ref[after, :] += dqw_ref[r * nw + BAND + n:(r + 1) * nw, :]
                for c in range(0, n, ROWS_PER_COPY):
                    m = min(ROWS_PER_COPY, n - c)
                    idx = pl.ds(r + c * d, m, stride=d)
                    val = dqw_ref[r * nw + BAND + c:r * nw + BAND + c + m, :]
                    dq_ref[idx, :] = val if pi == 0 else dq_ref[idx, :] + val

    ident = lambda hp: hp
    ospec = pl.BlockSpec((t, LANES), lambda hp, i: (i, hp))
    hspec = pl.BlockSpec((ATT_HALO, LANES), lambda hp, i: (i, hp))
    win_rows = t + 2 * ATT_HALO
    halo = jax.ShapeDtypeStruct((tiles * ATT_HALO, w), F32)
    return _pallas(
        body, name=name, grid=(w // LANES, tiles), comm=comm,
        args=[qk, qkv, qk, qk, qk, do, do, do, stats, stats, stats],
        in_specs=_att_specs(s, t, _col(1), halo=False) + _att_specs(s, t, _col(2), halo=False)
        + _att_specs(s, t, _col(0)) + _att_specs(s, t, ident) * 2,
        out_specs=[ospec, ospec, ospec, hspec, hspec],
        out_shape=[jax.ShapeDtypeStruct((s, w), F32)] * 3 + [halo, halo],
        scratch=[pltpu.VMEM((t, LANES), BF16), pltpu.VMEM((t, LANES), BF16),
                 pltpu.VMEM((win_rows, LANES), BF16), pltpu.VMEM((win_rows, LANES), BF16),
                 pltpu.VMEM((win_rows, LANES), F32),
                 pltpu.VMEM((t, LANES), F32), pltpu.VMEM((t, LANES), F32), pltpu.VMEM((win_rows, LANES), F32)],
        sem=("parallel", "parallel"))


def adamw_update(recvs, w, m, v, name, comm=None):
    nl = len(recvs)
    r, c = recvs[0].shape[1:]
    tr = 256 if (r > 256 and r % 256 == 0) else r
    nt = r // tr
    c1 = 1.0 - ADAM_B1 ** ADAM_STEP
    c2 = 1.0 - ADAM_B2 ** ADAM_STEP

    def body(*refs):
        g_refs = refs[:nl]
        w_ref, m_ref, v_ref, go_ref, d_ref, mo_ref, vo_ref = refs[nl:]

        def update(g_ref):
            g = g_ref[0].astype(F32)
            for j in range(1, N_DEV):
                g = g + g_ref[j].astype(F32)
            mn = ADAM_B1 * m_ref[...] + (1.0 - ADAM_B1) * g
            vn = ADAM_B2 * v_ref[...] + (1.0 - ADAM_B2) * (g * g)
            go_ref[...] = g
            mo_ref[...] = mn
            vo_ref[...] = vn
            d_ref[...] = -ADAM_LR * ((mn / c1) / (jnp.sqrt(vn / c2) + ADAM_EPS) + ADAM_WD * w_ref[...])

        for layer in range(nl):
            pl.when(pl.program_id(0) == layer)(functools.partial(update, g_refs[layer]))

    def gspec(layer):
        return pl.BlockSpec((N_DEV, tr, c), lambda l, i: (0, jnp.where(l == layer, i, 0), 0))

    spec = pl.BlockSpec((tr, c), lambda l, i: (l * nt + i, 0))
    return _pallas(
        body, name=name, grid=(nl, nt), args=[*recvs, w, m, v], comm=comm,
        in_specs=[gspec(layer) for layer in range(nl)] + [spec, spec, spec],
        out_specs=[spec] * 4, out_shape=[jax.ShapeDtypeStruct((nl * r, c), F32)] * 4,
        sem=("arbitrary", "arbitrary"))


BIG = ("mlp_w1", "mlp_w2", "ab_w_in", "ab_w_out", "c_w_qkv", "c_w_out")
SMALL = ("mix_norm_g", "mlp_norm_g", "a_spatial_w", "a_spatial_b", "a_vnorm_g", "a_vnorm_b", "b_conv_b", "b_norm_g",
         "b_norm_b", "c_q_norm_g", "c_k_norm_g")
WEIGHTS = ("mix_norm_g", "mlp_norm_g", "mlp_w1", "mlp_w2", "ab_w_in", "a_spatial_w", "a_spatial_b", "a_vnorm_g",
           "a_vnorm_b", "b_conv_w", "b_conv_b", "b_norm_g", "b_norm_b", "ab_w_out", "c_w_qkv", "c_q_norm_g",
           "c_k_norm_g", "c_w_out")


def _mixer_params(p, conv_full, i):
    aw = p["a_vnorm_g"].shape[1]
    row = lambda t: t[i][None, :]
    return dict(
        w=p["a_spatial_w"][i].astype(BF16), wt=jnp.swapaxes(p["a_spatial_w"][i], 1, 2).astype(BF16),
        bias=jnp.repeat(p["a_spatial_b"][i].T, aw // p["a_spatial_b"].shape[1], axis=1),
        vg=row(p["a_vnorm_g"]), vb=row(p["a_vnorm_b"]), cw=jnp.pad(conv_full[i], ((0, 1), (0, 0))),
        cb=row(p["b_conv_b"]), cg=row(p["b_norm_g"]), cbn=row(p["b_norm_b"]))


def _head_gains(p, i):
    rep = LANES // HEAD_DIM
    return jnp.stack([jnp.tile(p["c_q_norm_g"][i], rep), jnp.tile(p["c_k_norm_g"][i], rep),
                      jnp.ones((LANES,), F32)])[:, None, :]


def _unused_forward_backward(x, target, p, wg, conv_full):
    s, d = x.shape
    depth = p["mix_norm_g"].shape[0]
    tabs = rope_tables(s)
    saved = []
    for l in range(depth):
        i = l // 2
        mix_g, mlp_g = p["mix_norm_g"][l][None, :], p["mlp_norm_g"][l][None, :]
        st = dict(x_in=x)
        if l % 2 == 0:
            sp = _mixer_params(p, conv_full, i)
            z, h = norm_matmul(x, mix_g, wg["ab_w_in"], i, f"ab_in_{l}")
            ycat, gc = ab_mid_forward(z, sp, f"ab_mid_{l}")
            x = matmul_residual(x, ycat, wg["ab_w_out"], i, f"ab_out_{l}")
            st.update(z=z, h=h, y=ycat, gc=gc, sp=sp)
        else:
            gains = _head_gains(p, i)
            qkv, h = norm_matmul(x, mix_g, wg["c_w_qkv"], i, f"c_qkv_{l}")
            qkvn = qk_prep_forward(qkv, tabs, gains, f"c_prep_{l}")
            outs = [band_attention_forward(qkvn, dil, f"c_attn_{l}_d{dil}") for dil in DILATIONS]
            o, lse = attention_merge([t[0] for t in outs], [t[1] for t in outs], f"c_merge_{l}")
            x = matmul_residual(x, o, wg["c_w_out"], i, f"c_out_{l}")
            st.update(qkv=qkv, h=h, qkvn=qkvn, y=o, lse=lse, gains=gains)
        st["x_mid"] = x
        x, a, h2 = mlp_forward(x, mlp_g, wg["mlp_w1"], wg["mlp_w2"], l, f"mlp_{l}")
        st.update(a=a, h2=h2)
        saved.append(st)

    dy, loss_part = loss_and_grad(x, target, "loss")

    big = {n: [None] * p_len for n, p_len in (("mlp_w1", depth), ("mlp_w2", depth), ("ab_w_in", depth // 2 + depth % 2),
                                              ("ab_w_out", depth // 2 + depth % 2), ("c_w_qkv", depth // 2),
                                              ("c_w_out", depth // 2))}
    small = {n: [None] * p[n].shape[0] for n in SMALL}
    conv_grads = [None] * p["b_conv_b"].shape[0]
    for l in reversed(range(depth)):
        i, st = l // 2, saved[l]
        mix_g, mlp_g = p["mix_norm_g"][l][None, :], p["mlp_norm_g"][l][None, :]
        dxm, da, dg = mlp_backward(dy, st["a"], st["x_mid"], mlp_g, wg["mlp_w1"], wg["mlp_w2"], l, f"mlp_bwd_{l}")
        small["mlp_norm_g"][l] = dg[0]
        big["mlp_w1"][l] = matmul_tn(st["h2"], da, "cols", f"mlp_dw1_{l}")
        big["mlp_w2"][l] = matmul_tn(st["a"], dy, "rows", f"mlp_dw2_{l}", relu2=True)
        if l % 2 == 0:
            sp = st["sp"]
            dycat = matmul_nt(dxm, wg["ab_w_out"], i, f"ab_out_bwd_{l}")
            big["ab_w_out"][i] = matmul_tn(st["y"], dxm, "rows", f"ab_dwout_{l}")
            dz, dgc, dw, dsb, dvg, dvb, dcg, dcbn, dcb = ab_mid_backward(dycat, st["z"], st["gc"], sp, f"ab_mid_bwd_{l}")
            dz, dcw = conv_backward(dgc, st["z"], dz, sp, f"ab_conv_bwd_{l}")
            small["a_spatial_w"][i], small["a_spatial_b"][i] = dw, dsb
            small["a_vnorm_g"][i], small["a_vnorm_b"][i] = dvg[0], dvb[0]
            small["b_norm_g"][i], small["b_norm_b"][i], small["b_conv_b"][i] = dcg[0], dcbn[0], dcb[0]
            conv_grads[i] = dcw[:CONV_W]
            dy, dg = matmul_nt_norm_backward(dz, wg["ab_w_in"], i, st["x_in"], mix_g, dxm, f"ab_in_bwd_{l}")
            big["ab_w_in"][i] = matmul_tn(st["h"], dz, "cols", f"ab_dwin_{l}")
        else:
            do = matmul_nt(dxm, wg["c_w_out"], i, f"c_out_bwd_{l}")
            big["c_w_out"][i] = matmul_tn(st["y"], dxm, "rows", f"c_dwout_{l}")
            delta, dob = attention_delta(do, st["y"], f"c_delta_{l}")
            dqs, dks, dvs = [], [], []
            for dil in DILATIONS:
                dqs.append(band_attention_dq(st["qkvn"], dob, st["lse"], delta, dil, f"c_attn_dq_{l}_d{dil}"))
                dk, dv = band_attention_dkv(st["qkvn"], dob, st["lse"], delta, dil, f"c_attn_dkv_{l}_d{dil}")
                dks.append(dk)
                dvs.append(dv)
            dqkv, dgn = qk_prep_backward(dqs, dks, dvs, st["qkv"], tabs, st["gains"], f"c_prep_bwd_{l}")
            small["c_q_norm_g"][i] = dgn[0, 0, :HEAD_DIM] + dgn[0, 0, HEAD_DIM:]
            small["c_k_norm_g"][i] = dgn[1, 0, :HEAD_DIM] + dgn[1, 0, HEAD_DIM:]
            dy, dg = matmul_nt_norm_backward(dqkv, wg["c_w_qkv"], i, st["x_in"], mix_g, dxm, f"c_qkv_bwd_{l}")
            big["c_w_qkv"][i] = matmul_tn(st["h"], dqkv, "cols", f"c_dwqkv_{l}")
        small["mix_norm_g"][l] = dg[0]
    small = {n: jnp.stack(v) for n, v in small.items()}
    return loss_part, dy, big, small, jnp.stack(conv_grads)


PACKED = tuple(n for n in SMALL if n != "a_spatial_w")


def _pack(d):
    flat = jnp.concatenate([d[n].reshape(-1) for n in PACKED])
    rows = -(-flat.shape[0] // (8 * LANES)) * 8
    return jnp.pad(flat, (0, rows * LANES - flat.shape[0])).reshape(rows, LANES)


def _unpack(packed, like):
    flat, out, pos = packed.reshape(-1), {}, 0
    for n in PACKED:
        size = math.prod(like[n].shape)
        out[n] = flat[pos:pos + size].reshape(like[n].shape)
        pos += size
    return out


def _unused_kernel(x, mix_norm_g, mlp_norm_g, mlp_w1, mlp_w2, ab_w_in, a_spatial_w, a_spatial_b, a_vnorm_g, a_vnorm_b, b_conv_w, b_conv_b, b_norm_g, b_norm_b, ab_w_out, c_w_qkv, c_q_norm_g, c_k_norm_g, c_w_out, loss_target, m_mix_norm_g, m_mlp_norm_g, m_mlp_w1, m_mlp_w2, m_ab_w_in, m_a_spatial_w, m_a_spatial_b, m_a_vnorm_g, m_a_vnorm_b, m_b_conv_w, m_b_conv_b, m_b_norm_g, m_b_norm_b, m_ab_w_out, m_c_w_qkv, m_c_q_norm_g, m_c_k_norm_g, m_c_w_out, v_mix_norm_g, v_mlp_norm_g, v_mlp_w1, v_mlp_w2, v_ab_w_in, v_a_spatial_w, v_a_spatial_b, v_a_vnorm_g, v_a_vnorm_b, v_b_conv_w, v_b_conv_b, v_b_norm_g, v_b_norm_b, v_ab_w_out, v_c_w_qkv, v_c_q_norm_g, v_c_k_norm_g, v_c_w_out):
    args = dict(locals())
    w = {n: args[n] for n in WEIGHTS}
    m = {n: args["m_" + n] for n in WEIGHTS}
    v = {n: args["v_" + n] for n in WEIGHTS}

    wg = {n: all_gather(w[n].astype(BF16), "gather_" + n) for n in BIG}
    for n in ("ab_w_out", "c_w_out"):
        t = wg[n]
        wg[n] = t.reshape(t.shape[0], t.shape[1] * t.shape[2], t.shape[3])
    conv = all_gather(w["b_conv_w"], "gather_b_conv_w")
    conv_full = jnp.swapaxes(conv, 1, 2).reshape(conv.shape[0], conv.shape[2], N_DEV * conv.shape[3])

    loss_part, dx, big, small, conv_grad = forward_backward(x[0], loss_target[0], w, wg, conv_full)
    loss = lax.psum(jnp.sum(loss_part), ("x", "y", "c"))

    grads, deltas, new_m, new_v = {}, {}, {}, {}

    def update(n, recv):
        shape = w[n].shape
        flat = lambda t: t.reshape(-1, shape[-1])
        outs = adamw_update(recv.reshape((N_DEV, -1, shape[-1])), flat(w[n]), flat(m[n]), flat(v[n]), "adamw_" + n)
        grads[n], deltas[n], new_m[n], new_v[n] = (t.reshape(shape) for t in outs)

    for n in BIG:
        update(n, exchange(big[n], "exchange_" + n))
    nl, kw, cw = conv_grad.shape
    conv_parts = jnp.transpose(conv_grad.reshape(nl, kw, N_DEV, cw // N_DEV), (2, 0, 1, 3))
    update("b_conv_w", exchange([conv_parts], "exchange_b_conv_w"))

    packed = all_gather(_pack(small)[None], "gather_small_grads")[0]
    outs = adamw_update(packed, _pack(w), _pack(m), _pack(v), "adamw_small")
    for dst, t in zip((grads, deltas, new_m, new_v), outs):
        dst.update(_unpack(t, w))

    return (loss, dx[None], *[grads[n] for n in WEIGHTS], *[deltas[n] for n in WEIGHTS],
            *[new_m[n] for n in WEIGHTS], *[new_v[n] for n in WEIGHTS])


class Traffic:
    def __init__(self, shards, full=()):
        self.shards, self.w, self.queue, self.parts = shards, dict(full), [], {}

    def run(self, fn, *args, gather=(), send=False, **kw):
        operands, flags, dest = [], [], []
        if self.shards is None:
            if send:
                self.parts.update(self.queue)
                self.queue = []
        else:
            for k in gather:
                if k not in self.w:
                    operands.append(self.shards[k])
                    flags.append(("gather", SHARD_AXIS.get(k[0])))
                    dest.append((self.w, k))
            if send:
                for k, t in self.queue:
                    operands.append(t)
                    flags.append(("scatter", SHARD_AXIS.get(k[0])))
                    dest.append((self.parts, k))
                self.queue = []
        outs, couts = fn(*args, comm=PeerCopies(operands, flags) if operands else None, **kw)
        for (table, k), t in zip(dest, couts):
            table[k] = t
        return outs

    def flush(self, name, extra=()):
        self.queue += list(extra)
        self.run(lambda comm: ([], run_copies(comm, name) if comm is not None else []), send=True)


SHARD_AXIS = {"mlp_w1": 1, "mlp_w2": 0, "ab_w_in": 1, "ab_w_out": 0, "c_w_qkv": 1, "c_w_out": 0}


def forward_backward(x, target, p, tr, conv_full):
    s, d = x.shape
    depth = p["mix_norm_g"].shape[0]
    tabs = rope_tables(s)
    saved = []
    for l in range(depth):
        i = l // 2
        mix_g, mlp_g = p["mix_norm_g"][l][None, :], p["mlp_norm_g"][l][None, :]
        st = dict(x_in=x)
        nxt = () if l + 1 == depth else ((("c_w_qkv", i), ("c_w_out", i)) if l % 2 == 0 else
                                        (("ab_w_in", i + 1), ("ab_w_out", i + 1)))
        if l % 2 == 0:
            sp = _mixer_params(p, conv_full, i)
            z, h = tr.run(norm_matmul, x, mix_g, tr.w["ab_w_in", i], f"ab_in_{l}", gather=[("mlp_w1", l)])
            ycat, gc = tr.run(ab_mid_forward, z, sp, f"ab_mid_{l}", gather=[("mlp_w2", l)])
            x = matmul_residual(x, ycat, tr.w["ab_w_out", i], f"ab_out_{l}")
            st.update(z=z, h=h, y=ycat, gc=gc, sp=sp)
        else:
            gains = _head_gains(p, i)
            qkv, h = norm_matmul(x, mix_g, tr.w["c_w_qkv", i], f"c_qkv_{l}")[0]
            qk = qk_prep_forward(qkv, tabs, gains, f"c_prep_{l}")[0]
            ahead = [(n, j) for j in (l, l + 1) if j < depth for n in ("mlp_w1", "mlp_w2")]
            o, lse = tr.run(attention_forward, qk, qkv, f"c_attn_{l}", gather=ahead)
            x = matmul_residual(x, o, tr.w["c_w_out", i], f"c_out_{l}")
            st.update(qkv=qkv, h=h, qk=qk, y=o, lse=lse, gains=gains)
        st["x_mid"] = x
        x, a, h2 = tr.run(mlp_forward, x, mlp_g, tr.w["mlp_w1", l], tr.w["mlp_w2", l], f"mlp_{l}", gather=nxt)
        st.update(a=a, h2=h2)
        saved.append(st)

    dy, loss_part = loss_and_grad(x, target, "loss")

    def tn(*a, comm, **kw):
        out, couts = matmul_tn(*a, comm=comm, **kw)
        return [out], couts

    small = {n: [None] * p[n].shape[0] for n in SMALL}
    conv_grads = [None] * p["b_conv_b"].shape[0]
    for l in reversed(range(depth)):
        i, st = l // 2, saved[l]
        mix_g, mlp_g = p["mix_norm_g"][l][None, :], p["mlp_norm_g"][l][None, :]
        w1, w2 = tr.w["mlp_w1", l], tr.w["mlp_w2", l]
        dxm, da, dg, dyb = tr.run(mlp_backward, dy, st["a"], st["x_mid"], mlp_g, w1, w2, f"mlp_bwd_{l}", send=True)
        small["mlp_norm_g"][l] = dg[0]
        tr.queue.append((("mlp_w1", l), matmul_tn(st["h2"], da, f"mlp_dw1_{l}", n_split=2)[0]))
        tr.queue.append((("mlp_w2", l), matmul_tn(st["a"], dyb, f"mlp_dw2_{l}", m_split=2, relu2=True)[0]))
        if l % 2 == 0:
            sp = st["sp"]
            wout = tr.w["ab_w_out", i]
            dycat = matmul_nt(dxm, wout, f"ab_out_bwd_{l}")
            tr.queue.append((("ab_w_out", i), matmul_tn(st["y"], dxm, f"ab_dwout_{l}")[0]))
            dz, dgc, dw, dsb, dvg, dvb, dcg, dcbn, dcb = ab_mid_backward(dycat, st["z"], st["gc"], sp, f"ab_mid_bwd_{l}")
            dz, dcw = tr.run(conv_backward, dgc, st["z"], dz, sp, f"ab_conv_bwd_{l}", send=True)
            small["a_spatial_w"][i], small["a_spatial_b"][i] = dw, dsb
            small["a_vnorm_g"][i], small["a_vnorm_b"][i] = dvg[0], dvb[0]
            small["b_norm_g"][i], small["b_norm_b"][i], small["b_conv_b"][i] = dcg[0], dcbn[0], dcb[0]
            conv_grads[i] = dcw[:CONV_W]
            dy, dg = matmul_nt_norm_backward(dz, tr.w["ab_w_in", i], st["x_in"], mix_g, dxm, f"ab_in_bwd_{l}")
            small["mix_norm_g"][l] = dg[0]
            last = []
            if l == 0 and tr.shards is not None:
                tr.shards["small_grads", 0] = _pack({n: jnp.stack(small[n]) for n in PACKED})
                tr.shards["spatial_grads", 0] = jnp.stack(small["a_spatial_w"]).astype(BF16)
                last = [("small_grads", 0), ("spatial_grads", 0)]
            dwin = tr.run(tn, st["h"], dz, f"ab_dwin_{l}", send=True, gather=last)[0]
            tr.queue.append((("ab_w_in", i), dwin))
        else:
            wout = tr.w["c_w_out", i]
            do = matmul_nt(dxm, wout, f"c_out_bwd_{l}")
            tr.queue.append((("c_w_out", i), matmul_tn(st["y"], dxm, f"c_dwout_{l}")[0]))
            stats = attention_delta(do, st["y"], st["lse"], f"c_delta_{l}")
            dk, dv, dq, dqp, dqn = tr.run(attention_backward, st["qk"], st["qkv"], do, stats, f"c_attn_bwd_{l}", send=True)
            dqkv, dgn = qk_prep_backward(dq, dqp, dqn, dk, dv, st["qkv"], tabs, st["gains"], f"c_prep_bwd_{l}")
            small["c_q_norm_g"][i] = dgn[0, 0, :HEAD_DIM] + dgn[0, 0, HEAD_DIM:]
            small["c_k_norm_g"][i] = dgn[1, 0, :HEAD_DIM] + dgn[1, 0, HEAD_DIM:]
            dy, dg = matmul_nt_norm_backward(dqkv, tr.w["c_w_qkv", i], st["x_in"], mix_g, dxm, f"c_qkv_bwd_{l}")
            small["mix_norm_g"][l] = dg[0]
            tr.queue.append((("c_w_qkv", i), matmul_tn(st["h"], dqkv, f"c_dwqkv_{l}", n_split=2)[0]))
    small = {n: jnp.stack(v) for n, v in small.items()}
    return loss_part, dy, small, jnp.stack(conv_grads)


def kernel(x, mix_norm_g, mlp_norm_g, mlp_w1, mlp_w2, ab_w_in, a_spatial_w, a_spatial_b, a_vnorm_g, a_vnorm_b, b_conv_w, b_conv_b, b_norm_g, b_norm_b, ab_w_out, c_w_qkv, c_q_norm_g, c_k_norm_g, c_w_out, loss_target, m_mix_norm_g, m_mlp_norm_g, m_mlp_w1, m_mlp_w2, m_ab_w_in, m_a_spatial_w, m_a_spatial_b, m_a_vnorm_g, m_a_vnorm_b, m_b_conv_w, m_b_conv_b, m_b_norm_g, m_b_norm_b, m_ab_w_out, m_c_w_qkv, m_c_q_norm_g, m_c_k_norm_g, m_c_w_out, v_mix_norm_g, v_mlp_norm_g, v_mlp_w1, v_mlp_w2, v_ab_w_in, v_a_spatial_w, v_a_spatial_b, v_a_vnorm_g, v_a_vnorm_b, v_b_conv_w, v_b_conv_b, v_b_norm_g, v_b_norm_b, v_ab_w_out, v_c_w_qkv, v_c_q_norm_g, v_c_k_norm_g, v_c_w_out):
    args = dict(locals())
    w = {n: args[n] for n in WEIGHTS}
    m = {n: args["m_" + n] for n in WEIGHTS}
    v = {n: args["v_" + n] for n in WEIGHTS}

    shards = {(n, l): w[n][l].astype(BF16) for n in BIG for l in range(w[n].shape[0])}
    shards["b_conv_w", 0] = w["b_conv_w"]
    tr = Traffic(shards)
    first = [("ab_w_in", 0), ("ab_w_out", 0), ("b_conv_w", 0)]
    tr.run(lambda comm: ([], run_copies(comm, "gather_first")), gather=first)
    conv = tr.w["b_conv_w", 0]
    conv_full = jnp.transpose(conv, (1, 2, 0, 3)).reshape(conv.shape[1], conv.shape[2], -1)

    loss_part, dx, small, conv_grad = forward_backward(x[0], loss_target[0], w, tr, conv_full)
    loss = lax.psum(jnp.sum(loss_part), ("x", "y", "c"))
    nl, kw, cw = conv_grad.shape
    conv_parts = jnp.transpose(conv_grad.reshape(nl, kw, N_DEV, cw // N_DEV), (2, 0, 1, 3))
    tr.queue.append((("b_conv_w", 0), conv_parts))

    grads, deltas, new_m, new_v = {}, {}, {}, {}

    def update(n, recvs):
        shape = w[n].shape
        flat = lambda t: t.reshape(-1, shape[-1])
        recvs = [t.reshape(N_DEV, -1, shape[-1]) for t in recvs]
        outs = tr.run(adamw_update, recvs, flat(w[n]), flat(m[n]), flat(v[n]), "adamw_" + n, send=True)
        grads[n], deltas[n], new_m[n], new_v[n] = (t.reshape(shape) for t in outs)

    for n in BIG:
        update(n, [tr.parts[n, l] for l in range(w[n].shape[0])])
    update("b_conv_w", [tr.parts["b_conv_w", 0]])
    update("a_spatial_w", [tr.w["spatial_grads", 0]])
    outs = adamw_update([tr.w["small_grads", 0]], _pack(w), _pack(m), _pack(v), "adamw_small")[0]
    for dst, t in zip((grads, deltas, new_m, new_v), outs):
        dst.update(_unpack(t, w))

    return (loss, dx[None], *[grads[n] for n in WEIGHTS], *[deltas[n] for n in WEIGHTS],
            *[new_m[n] for n in WEIGHTS], *[new_v[n] for n in WEIGHTS])
```

```python
import functools
import math

import jax
import jax.numpy as jnp
from jax import lax
from jax.experimental import pallas as pl
from jax.experimental.pallas import tpu as pltpu

F32, BF16 = jnp.float32, jnp.bfloat16
N_DEV = 8
EPS = 1e-6
NEG = -1e30
LANES = 128
HEAD_DIM = 64
CHUNK = 128
CONV_W = 31
CONV_HALO = 16
BAND = 64
DILATIONS = (1, 4, 16)
ROT_DIM = 16
ROPE_THETA = 500000.0
VMEM_LIMIT = 56 * 1024 * 1024
MLP_CHUNK = 1024
MLP_BWD_CHUNK = 1024
MLP_BWD_ROWS = 512
ADAM_LR, ADAM_B1, ADAM_B2, ADAM_EPS, ADAM_WD, ADAM_STEP = 0.001, 0.9, 0.999, 1e-08, 0.01, 10
MESH = pl.DeviceIdType.MESH


def _params(*sem):
    return pltpu.CompilerParams(dimension_semantics=sem, vmem_limit_bytes=VMEM_LIMIT)


def _dot(a, b):
    return jnp.dot(a, b, preferred_element_type=F32)


def _dot_nt(a, b):
    return lax.dot_general(a, b, (((1,), (1,)), ((), ())), preferred_element_type=F32)


def _dot_tn(a, b):
    return lax.dot_general(a, b, (((0,), (0,)), ((), ())), preferred_element_type=F32)


def _rms_r(x):
    return lax.rsqrt(jnp.mean(x * x, axis=-1, keepdims=True) + EPS)


def _sigmoid(x):
    return 1.0 / (1.0 + jnp.exp(-x))


_GK = math.sqrt(2.0 / math.pi)


def _gelu(x):
    return 0.5 * x * (1.0 + jnp.tanh(_GK * (x + 0.044715 * x * x * x)))


def _gelu_grad(x):
    t = jnp.tanh(_GK * (x + 0.044715 * x * x * x))
    return 0.5 * (1.0 + t) + 0.5 * x * (1.0 - t * t) * (_GK * (1.0 + 3.0 * 0.044715 * x * x))


def _seg_sum(x, bd):
    hi = x.astype(BF16)
    lo = (x - hi.astype(F32)).astype(BF16)
    return _dot(hi, bd) + _dot(lo, bd)


def _block_diag(n):
    i = lax.broadcasted_iota(jnp.int32, (n, n), 0) // HEAD_DIM
    j = lax.broadcasted_iota(jnp.int32, (n, n), 1) // HEAD_DIM
    return jnp.where(i == j, 1.0, 0.0).astype(BF16)


def _tile(s, cap):
    t = min(s, cap)
    assert s % t == 0
    return t


def _my_index():
    return 4 * lax.axis_index("x") + 2 * lax.axis_index("y") + lax.axis_index("c")


def _device(i):
    return (i // 4, (i // 2) % 2, i % 2)


_HBM = pl.BlockSpec(memory_space=pl.ANY)


class PeerCopies:
    def __init__(self, operands, modes):
        self.inputs, self.modes = list(operands), list(modes)
        self.out_shape = []
        for t, (kind, axis) in zip(self.inputs, self.modes):
            shape = list(t.shape)
            if kind == "gather":
                shape = [N_DEV] + shape if axis is None else shape[:axis] + [N_DEV * shape[axis]] + shape[axis + 1:]
            elif axis is not None:
                shape = [N_DEV] + shape[:axis] + [shape[axis] // N_DEV] + shape[axis + 1:]
            self.out_shape.append(jax.ShapeDtypeStruct(tuple(shape), t.dtype))
        n = len(self.inputs)
        self.scratch = [pltpu.SemaphoreType.DMA((n, N_DEV - 1)), pltpu.SemaphoreType.DMA((n, N_DEV - 1)),
                        pltpu.SemaphoreType.DMA((n,))]

    @staticmethod
    def _block(ref, axis, size, j):
        if axis is None:
            return ref.at[j]
        return ref.at[tuple([slice(None)] * axis + [pl.ds(j * size, size)])]

    def _copies(self, in_refs, out_refs, sems, arrivals):
        send_sems, recv_sems, local_sems = sems
        me = _my_index()
        local, sends, recvs = [], [], []
        for t, (src, dst) in enumerate(zip(in_refs, out_refs)):
            kind, axis = self.modes[t]
            if kind == "gather":
                size = None if axis is None else src.shape[axis]
                source = lambda j, src=src: src
                place = lambda j, dst=dst, axis=axis, size=size: self._block(dst, axis, size, j)
            else:
                size = None if axis is None else src.shape[axis] // N_DEV
                source = lambda j, src=src, axis=axis, size=size: self._block(src, axis, size, j)
                place = lambda j, dst=dst: dst.at[j]
            local.append(pltpu.make_async_copy(source(me), place(me), local_sems.at[t]))
            for k in range(N_DEV - 1):
                to, frm = (me + k + 1) % N_DEV, (me + N_DEV - k - 1) % N_DEV
                sends.append(pltpu.make_async_remote_copy(
                    src_ref=source(to), dst_ref=place(me), send_sem=send_sems.at[t, k], recv_sem=recv_sems.at[t, k],
                    device_id=_device(to), device_id_type=MESH))
                if arrivals:
                    recvs.append(pltpu.make_async_remote_copy(
                        src_ref=source(me), dst_ref=place(frm), send_sem=send_sems.at[t, k], recv_sem=recv_sems.at[t, k],
                        device_id=_device(frm), device_id_type=MESH))
        return local, sends, recvs

    def start(self, in_refs, out_refs, sems):
        local, sends, _ = self._copies(in_refs, out_refs, sems, False)
        for cp in local + sends:
            cp.start()

    def finish(self, in_refs, out_refs, sems):
        local, sends, recvs = self._copies(in_refs, out_refs, sems, True)
        for cp in recvs:
            cp.wait_recv()
        for cp in sends:
            cp.wait_send()
        for cp in local:
            cp.wait()


def _pallas(body, *, name, args, in_specs, out_specs, out_shape, grid=(), scratch=(), sem=(), comm=None, aliases=None):
    n_in, n_out, n_scr = len(args), len(out_shape), len(scratch)
    if comm is None:
        outs = pl.pallas_call(
            body, name=name, grid=grid, in_specs=in_specs, out_specs=out_specs, out_shape=out_shape,
            scratch_shapes=list(scratch), input_output_aliases=aliases or {}, compiler_params=_params(*sem))(*args)
        return list(outs), []
    ci, co = len(comm.inputs), len(comm.out_shape)

    def hosted(*refs):
        ins, cins = refs[:n_in], refs[n_in:n_in + ci]
        outs, couts = refs[n_in + ci:n_in + ci + n_out], refs[n_in + ci + n_out:n_in + ci + n_out + co]
        rest = refs[n_in + ci + n_out + co:]
        scr, sems = rest[:n_scr], rest[n_scr:]
        if not grid:
            comm.start(cins, couts, sems)
            comm.finish(cins, couts, sems)
            return
        first = last = None
        for axis, size in enumerate(grid):
            f, l = pl.program_id(axis) == 0, pl.program_id(axis) == size - 1
            first, last = (f, l) if first is None else (first & f, last & l)
        pl.when(first)(lambda: comm.start(cins, couts, sems))
        body(*ins, *outs, *scr)
        pl.when(last)(lambda: comm.finish(cins, couts, sems))

    outs = pl.pallas_call(
        hosted, name=name, grid=grid, in_specs=list(in_specs) + [_HBM] * ci, out_specs=list(out_specs) + [_HBM] * co,
        out_shape=list(out_shape) + comm.out_shape, scratch_shapes=list(scratch) + comm.scratch,
        input_output_aliases=aliases or {}, compiler_params=_params(*["arbitrary"] * len(grid)))(*args, *comm.inputs)
    return list(outs[:n_out]), list(outs[n_out:])


def run_copies(comm, name):
    return _pallas(None, name=name, args=[], in_specs=[], out_specs=[], out_shape=[], comm=comm)[1]


def norm_matmul(x, g, wg, name, comm=None):
    s, d = x.shape
    n = wg.shape[-1]
    ns = 1024 if n % 1024 == 0 else n // 4
    tm = _tile(s, 1024)

    def body(x_ref, g_ref, w_ref, z_ref, h_ref):
        @pl.when(pl.program_id(1) == 0)
        def _():
            xv = x_ref[...]
            h_ref[...] = (xv * _rms_r(xv) * g_ref[...]).astype(BF16)
        z_ref[...] = _dot(h_ref[...], w_ref[...])

    return _pallas(
        body, name=name, grid=(s // tm, n // ns), args=[x, g, wg], comm=comm,
        in_specs=[pl.BlockSpec((tm, d), lambda i, j: (i, 0)),
                  pl.BlockSpec((1, d), lambda i, j: (0, 0)),
                  pl.BlockSpec((d, ns), lambda i, j: (0, j))],
        out_specs=[pl.BlockSpec((tm, ns), lambda i, j: (i, j)),
                   pl.BlockSpec((tm, d), lambda i, j: (i, 0))],
        out_shape=[jax.ShapeDtypeStruct((s, n), F32), jax.ShapeDtypeStruct((s, d), BF16)],
        sem=("parallel", "arbitrary"))


def mlp_forward(x, g, w1g, w2g, name, comm=None):
    s, d = x.shape
    f = w1g.shape[-1]
    fs = MLP_CHUNK
    tm = _tile(s, 1024)

    def body(x_ref, g_ref, w1_ref, w2_ref, xo_ref, a_ref, h_ref):
        @pl.when(pl.program_id(1) == 0)
        def _():
            xv = x_ref[...]
            h_ref[...] = (xv * _rms_r(xv) * g_ref[...]).astype(BF16)
            xo_ref[...] = xv
        a = _dot(h_ref[...], w1_ref[...])
        a_ref[...] = a.astype(BF16)
        r = jnp.maximum(a, 0.0)
        xo_ref[...] += _dot((r * r).astype(BF16), w2_ref[...])

    return _pallas(
        body, name=name, grid=(s // tm, f // fs), args=[x, g, w1g, w2g], comm=comm,
        in_specs=[pl.BlockSpec((tm, d), lambda i, j: (i, 0)),
                  pl.BlockSpec((1, d), lambda i, j: (0, 0)),
                  pl.BlockSpec((d, fs), lambda i, j: (0, j)),
                  pl.BlockSpec((fs, d), lambda i, j: (j, 0))],
        out_specs=[pl.BlockSpec((tm, d), lambda i, j: (i, 0)),
                   pl.BlockSpec((tm, fs), lambda i, j: (i, j)),
                   pl.BlockSpec((tm, d), lambda i, j: (i, 0))],
        out_shape=[jax.ShapeDtypeStruct((s, d), F32), jax.ShapeDtypeStruct((s, f), BF16),
                   jax.ShapeDtypeStruct((s, d), BF16)],
        sem=("parallel", "arbitrary"))


def _norm_backward(dh, xv, g, dres):
    r = _rms_r(xv)
    xh = xv * r
    t = dh * g
    dx = dres + r * (t - xh * jnp.mean(t * xh, axis=-1, keepdims=True))
    return dx, jnp.sum(dh * xh, axis=0, keepdims=True)


def mlp_backward(dy, a, x, g, w1g, w2g, name, comm=None):
    s, d = x.shape
    f = w1g.shape[-1]
    fs = MLP_BWD_CHUNK
    tm = _tile(s, MLP_BWD_ROWS)

    def body(dy_ref, a_ref, x_ref, g_ref, w1_ref, w2_ref, dx_ref, da_ref, dg_ref, dyb_ref, dh_ref):
        i, j = pl.program_id(0), pl.program_id(1)

        @pl.when(j == 0)
        def _():
            dyb_ref[...] = dy_ref[...].astype(BF16)
            dh_ref[...] = jnp.zeros_like(dh_ref)

        dr = _dot_nt(dyb_ref[...], w2_ref[...])
        da = (dr * (2.0 * jnp.maximum(a_ref[...].astype(F32), 0.0))).astype(BF16)
        da_ref[...] = da
        dh_ref[...] += _dot_nt(da, w1_ref[...])

        @pl.when(j == f // fs - 1)
        def _():
            dx, dgp = _norm_backward(dh_ref[...], x_ref[...], g_ref[...], dy_ref[...])
            dx_ref[...] = dx

            @pl.when(i == 0)
            def _():
                dg_ref[...] = dgp

            @pl.when(i > 0)
            def _():
                dg_ref[...] += dgp

    return _pallas(
        body, name=name, grid=(s // tm, f // fs), args=[dy, a, x, g, w1g, w2g], comm=comm,
        in_specs=[pl.BlockSpec((tm, d), lambda i, j: (i, 0)),
                  pl.BlockSpec((tm, fs), lambda i, j: (i, j)),
                  pl.BlockSpec((tm, d), lambda i, j: (i, 0)),
                  pl.BlockSpec((1, d), lambda i, j: (0, 0)),
                  pl.BlockSpec((d, fs), lambda i, j: (0, j)),
                  pl.BlockSpec((fs, d), lambda i, j: (j, 0))],
        out_specs=[pl.BlockSpec((tm, d), lambda i, j: (i, 0)),
                   pl.BlockSpec((tm, fs), lambda i, j: (i, j)),
                   pl.BlockSpec((1, d), lambda i, j: (0, 0)),
                   pl.BlockSpec((tm, d), lambda i, j: (i, 0))],
        out_shape=[jax.ShapeDtypeStruct((s, d), F32), jax.ShapeDtypeStruct((s, f), BF16),
                   jax.ShapeDtypeStruct((1, d), F32), jax.ShapeDtypeStruct((s, d), BF16)],
        scratch=[pltpu.VMEM((tm, d), F32)],
        sem=("arbitrary", "arbitrary"))


def matmul_tn(a, b, name, m_split=1, n_split=1, relu2=False, comm=None):
    s, m = a.shape
    n = b.shape[1]
    ts = _tile(s, 1024 if relu2 else 2048)
    bm, bn = m // m_split, n // n_split
    a_map = lambda j, k: (k, j // n_split)
    b_map = lambda j, k: (k, j % n_split)

    def body(a_ref, b_ref, o_ref, acc_ref):
        k = pl.program_id(1)
        av = a_ref[...]
        if relu2:
            af = jnp.maximum(av.astype(F32), 0.0)
            av = af * af
        p = _dot_tn(av.astype(BF16), b_ref[...].astype(BF16))

        @pl.when(k == 0)
        def _():
            acc_ref[...] = p

        @pl.when(k > 0)
        def _():
            acc_ref[...] += p

        @pl.when(k == s // ts - 1)
        def _():
            o_ref[...] = acc_ref[...].astype(BF16)

    outs, couts = _pallas(
        body, name=name, grid=(m_split * n_split, s // ts), args=[a, b], comm=comm,
        in_specs=[pl.BlockSpec((ts, bm), a_map), pl.BlockSpec((ts, bn), b_map)],
        out_specs=[pl.BlockSpec((bm, bn), lambda j, k: (j // n_split, j % n_split))],
        out_shape=[jax.ShapeDtypeStruct((m, n), BF16)],
        scratch=[pltpu.VMEM((bm, bn), F32)],
        sem=("parallel", "arbitrary"))
    return outs[0], couts


def matmul_residual(x, y, w, name):
    s, n = x.shape
    k = y.shape[1]
    tm = _tile(s, 1024)

    def body(x_ref, y_ref, w_ref, o_ref):
        o_ref[...] = x_ref[...] + _dot(y_ref[...], w_ref[...])

    return pl.pallas_call(
        body, name=name, grid=(s // tm,),
        in_specs=[pl.BlockSpec((tm, n), lambda i: (i, 0)),
                  pl.BlockSpec((tm, k), lambda i: (i, 0)),
                  pl.BlockSpec((k, n), lambda i: (0, 0))],
        out_specs=pl.BlockSpec((tm, n), lambda i: (i, 0)),
        out_shape=jax.ShapeDtypeStruct((s, n), F32),
        compiler_params=_params("parallel"),
    )(x, y, w)


def matmul_nt(dy, wg, name):
    s, n = dy.shape
    k = wg.shape[0]
    tm = _tile(s, 1024)

    def body(dy_ref, w_ref, o_ref):
        o_ref[...] = _dot_nt(dy_ref[...].astype(BF16), w_ref[...])

    return pl.pallas_call(
        body, name=name, grid=(s // tm,),
        in_specs=[pl.BlockSpec((tm, n), lambda i: (i, 0)),
                  pl.BlockSpec((k, n), lambda i: (0, 0))],
        out_specs=pl.BlockSpec((tm, k), lambda i: (i, 0)),
        out_shape=jax.ShapeDtypeStruct((s, k), F32),
        compiler_params=_params("parallel"),
    )(dy, wg)


def matmul_nt_norm_backward(dz, wg, x, g, dres, name):
    s, d = x.shape
    n = wg.shape[-1]
    tm = _tile(s, 512)

    def body(dz_ref, w_ref, x_ref, g_ref, dres_ref, dx_ref, dg_ref):
        dx, dgp = _norm_backward(_dot_nt(dz_ref[...], w_ref[...]), x_ref[...], g_ref[...], dres_ref[...])
        dx_ref[...] = dx
        _accumulate(dg_ref, dgp, pl.program_id(0) == 0)

    return pl.pallas_call(
        body, name=name, grid=(s // tm,),
        in_specs=[pl.BlockSpec((tm, n), lambda i: (i, 0)),
                  pl.BlockSpec((d, n), lambda i: (0, 0)),
                  pl.BlockSpec((tm, d), lambda i: (i, 0)),
                  pl.BlockSpec((1, d), lambda i: (0, 0)),
                  pl.BlockSpec((tm, d), lambda i: (i, 0))],
        out_specs=[pl.BlockSpec((tm, d), lambda i: (i, 0)),
                   pl.BlockSpec((1, d), lambda i: (0, 0))],
        out_shape=[jax.ShapeDtypeStruct((s, d), F32), jax.ShapeDtypeStruct((1, d), F32)],
        compiler_params=_params("arbitrary"),
    )(dz, wg, x, g, dres)


def loss_and_grad(y, target, name):
    s, d = y.shape
    tm = _tile(s, 1024)

    def body(y_ref, t_ref, dy_ref, l_ref):
        e = y_ref[...] - t_ref[...]
        dy_ref[...] = e / d
        part = jnp.sum(e * e, axis=0, keepdims=True) * (0.5 / d)

        @pl.when(pl.program_id(0) == 0)
        def _():
            l_ref[...] = part

        @pl.when(pl.program_id(0) > 0)
        def _():
            l_ref[...] += part

    return pl.pallas_call(
        body, name=name, grid=(s // tm,),
        in_specs=[pl.BlockSpec((tm, d), lambda i: (i, 0)), pl.BlockSpec((tm, d), lambda i: (i, 0))],
        out_specs=[pl.BlockSpec((tm, d), lambda i: (i, 0)), pl.BlockSpec((1, d), lambda i: (0, 0))],
        out_shape=[jax.ShapeDtypeStruct((s, d), F32), jax.ShapeDtypeStruct((1, d), F32)],
        compiler_params=_params("arbitrary"),
    )(y, target)


def _layernorm(x, g, b):
    mu = jnp.mean(x, axis=-1, keepdims=True)
    xc = x - mu
    rstd = lax.rsqrt(jnp.mean(xc * xc, axis=-1, keepdims=True) + EPS)
    xn = xc * rstd
    return xn * g + b, xn, rstd


def _layernorm_backward(dy, xn, rstd, g):
    dxn = dy * g
    return rstd * (dxn - jnp.mean(dxn, axis=-1, keepdims=True) - xn * jnp.mean(dxn * xn, axis=-1, keepdims=True))


def _group_halves(x_ref, jp, nch):
    blk = jnp.concatenate([x_ref[c * CHUNK:(c + 1) * CHUNK, jp * LANES:(jp + 1) * LANES] for c in range(nch)], axis=1)
    low = (lax.broadcasted_iota(jnp.int32, blk.shape, 1) % LANES) < HEAD_DIM
    return jnp.where(low, blk, 0.0).astype(BF16), jnp.where(low, 0.0, blk).astype(BF16)


def _spatial_apply(src_ref, w_ref, dst_ref, nch, bias_ref=None):
    for jp in range(4):
        lo, hi = _group_halves(src_ref, jp, nch)
        r = _dot(w_ref[2 * jp], lo) + _dot(w_ref[2 * jp + 1], hi)
        for c in range(nch):
            v = r[:, c * LANES:(c + 1) * LANES]
            if bias_ref is not None:
                v = v + bias_ref[:, jp * LANES:(jp + 1) * LANES]
            dst_ref[c * CHUNK:(c + 1) * CHUNK, jp * LANES:(jp + 1) * LANES] = v


def _glu(zb):
    w = zb.shape[1] // 2
    return zb[:, :w] * _sigmoid(zb[:, w:])


def _fill_padded(pad_ref, prev, cur, nxt, i, nt, tm):
    pad_ref[0:CONV_HALO, :] = jnp.where(i > 0, prev, 0.0)
    pad_ref[CONV_HALO:CONV_HALO + tm, :] = cur
    pad_ref[CONV_HALO + tm:2 * CONV_HALO + tm, :] = jnp.where(i < nt - 1, nxt, 0.0)


def _halo_specs(tm, s, width, col):
    hb, nhb = tm // CONV_HALO, s // CONV_HALO
    return [pl.BlockSpec((tm, width), lambda i: (i, col)),
            pl.BlockSpec((CONV_HALO, width), lambda i: (jnp.maximum(i * hb - 1, 0), col)),
            pl.BlockSpec((CONV_HALO, width), lambda i: (jnp.minimum((i + 1) * hb, nhb - 1), col))]


def _const_spec(shape):
    nd = len(shape)
    return pl.BlockSpec(shape, lambda i: (0,) * nd)


SUBLANES = 8


def _shift_scratch(tm, width):
    return pltpu.VMEM((SUBLANES - 1, tm + 2 * CONV_HALO - SUBLANES, width), F32)


def _fill_shifts(sh_ref, pad_ref):
    rows = sh_ref.shape[1]
    for sft in range(1, SUBLANES):
        sh_ref[sft - 1] = pad_ref[pl.ds(sft, rows), :]


def _tap(pad_ref, sh_ref, offset, cols):
    sft = offset % SUBLANES
    rows = pl.ds(offset - sft, CHUNK)
    return pad_ref[rows, cols] if sft == 0 else sh_ref[sft - 1, rows, cols]


def ab_mid_forward(z, sp, name, comm=None):
    s = z.shape[0]
    aw = z.shape[1] // 4
    tm = _tile(s, 512)
    nch, nt = tm // CHUNK, s // tm

    def body(zu_ref, zv_ref, zb_ref, zp_ref, zn_ref, w_ref, bias_ref, vg_ref, vb_ref, cw_ref, cb_ref, cg_ref, cbn_ref,
             y_ref, gc_ref, vl_ref, sv_ref, pad_ref, sh_ref):
        i = pl.program_id(0)
        vl_ref[...] = _layernorm(_gelu(zv_ref[...]), vg_ref[...], vb_ref[...])[0]
        _spatial_apply(vl_ref, w_ref, sv_ref, nch, bias_ref)
        y_ref[:, :aw] = (_gelu(zu_ref[...]) * sv_ref[...]).astype(BF16)

        _fill_padded(pad_ref, _glu(zp_ref[...]), _glu(zb_ref[...]), _glu(zn_ref[...]), i, nt, tm)
        _fill_shifts(sh_ref, pad_ref)
        for rb in range(tm // CHUNK):
            for lb in range(aw // LANES):
                cols = pl.ds(lb * LANES, LANES)
                acc = jnp.broadcast_to(cb_ref[:, cols], (CHUNK, LANES))
                for k in range(CONV_W):
                    acc = acc + cw_ref[k:k + 1, cols] * _tap(pad_ref, sh_ref, rb * CHUNK + CONV_HALO - CONV_W // 2 + k, cols)
                gc_ref[rb * CHUNK:(rb + 1) * CHUNK, cols] = acc
        yl = _layernorm(gc_ref[...], cg_ref[...], cbn_ref[...])[0]
        y_ref[:, aw:] = (yl * _sigmoid(yl)).astype(BF16)

    return _pallas(
        body, name=name, grid=(nt,), comm=comm,
        args=[z, z, z, z, z, sp["w"], sp["bias"], sp["vg"], sp["vb"], sp["cw"], sp["cb"], sp["cg"], sp["cbn"]],
        in_specs=[pl.BlockSpec((tm, aw), lambda i: (i, 0)), pl.BlockSpec((tm, aw), lambda i: (i, 1))]
        + _halo_specs(tm, s, 2 * aw, 1)
        + [_const_spec(sp["w"].shape), _const_spec(sp["bias"].shape)]
        + [_const_spec((1, aw))] * 2 + [_const_spec(sp["cw"].shape)] + [_const_spec((1, aw))] * 3,
        out_specs=[pl.BlockSpec((tm, 2 * aw), lambda i: (i, 0)), pl.BlockSpec((tm, aw), lambda i: (i, 0))],
        out_shape=[jax.ShapeDtypeStruct((s, 2 * aw), BF16), jax.ShapeDtypeStruct((s, aw), F32)],
        scratch=[pltpu.VMEM((tm, aw), F32), pltpu.VMEM((tm, aw), F32), pltpu.VMEM((tm + 2 * CONV_HALO, aw), F32),
                 _shift_scratch(tm, aw)],
        sem=("parallel",))


def _accumulate(ref, val, first):
    @pl.when(first)
    def _():
        ref[...] = val

    @pl.when(jnp.logical_not(first))
    def _():
        ref[...] += val


def ab_mid_backward(dy, z, gc, sp, name):
    s = z.shape[0]
    aw = z.shape[1] // 4
    tm = _tile(s, 512)
    nch, nt = tm // CHUNK, s // tm

    def body(dya_ref, dyb_ref, zu_ref, zv_ref, gc_ref, w_ref, wt_ref, bias_ref, vg_ref, vb_ref, cg_ref, cbn_ref,
             dz_ref, dgc_ref, dw_ref, dsb_ref, dvg_ref, dvb_ref, dcg_ref, dcbn_ref, dcb_ref,
             vl_ref, sv_ref, dsv_ref, dvl_ref):
        first = pl.program_id(0) == 0
        zu, zv = zu_ref[...], zv_ref[...]
        u = _gelu(zu)
        vl, vn, vrstd = _layernorm(_gelu(zv), vg_ref[...], vb_ref[...])
        vl_ref[...] = vl
        _spatial_apply(vl_ref, w_ref, sv_ref, nch, bias_ref)
        dya = dya_ref[...]
        dz_ref[:, :aw] = (dya * sv_ref[...] * _gelu_grad(zu)).astype(BF16)
        dsv = dya * u
        dsv_ref[...] = dsv
        _spatial_apply(dsv_ref, wt_ref, dvl_ref, nch)

        for jp in range(4):
            dlo, dhi = _group_halves(dsv_ref, jp, nch)
            vlo, vhi = _group_halves(vl_ref, jp, nch)
            vall = vlo + vhi
            _accumulate(dw_ref.at[2 * jp], _dot_nt(dlo, vall), first)
            _accumulate(dw_ref.at[2 * jp + 1], _dot_nt(dhi, vall), first)
        rows = dsv[0:CHUNK]
        for c in range(1, nch):
            rows = rows + dsv[c * CHUNK:(c + 1) * CHUNK]
        grp = lax.broadcasted_iota(jnp.int32, (8, aw), 0) == lax.broadcasted_iota(jnp.int32, (8, aw), 1) // HEAD_DIM
        e = jnp.where(grp, 1.0, 0.0).astype(BF16)
        hi = rows.astype(BF16)
        r1 = rows - hi.astype(F32)
        mid = r1.astype(BF16)
        lo = (r1 - mid.astype(F32)).astype(BF16)
        _accumulate(dsb_ref, _dot_nt(e, hi) + _dot_nt(e, mid) + _dot_nt(e, lo), first)

        dvl = dvl_ref[...]
        _accumulate(dvg_ref, jnp.sum(dvl * vn, axis=0, keepdims=True), first)
        _accumulate(dvb_ref, jnp.sum(dvl, axis=0, keepdims=True), first)
        dz_ref[:, aw:] = (_layernorm_backward(dvl, vn, vrstd, vg_ref[...]) * _gelu_grad(zv)).astype(BF16)

        yl, yn, yrstd = _layernorm(gc_ref[...], cg_ref[...], cbn_ref[...])
        sg = _sigmoid(yl)
        dyl = dyb_ref[...] * (sg + yl * sg * (1.0 - sg))
        _accumulate(dcg_ref, jnp.sum(dyl * yn, axis=0, keepdims=True), first)
        _accumulate(dcbn_ref, jnp.sum(dyl, axis=0, keepdims=True), first)
        dgc = _layernorm_backward(dyl, yn, yrstd, cg_ref[...])
        dgc_ref[...] = dgc
        _accumulate(dcb_ref, jnp.sum(dgc, axis=0, keepdims=True), first)

    vec = jax.ShapeDtypeStruct((1, aw), F32)
    return pl.pallas_call(
        body, name=name, grid=(nt,),
        in_specs=[pl.BlockSpec((tm, aw), lambda i: (i, 0)), pl.BlockSpec((tm, aw), lambda i: (i, 1)),
                  pl.BlockSpec((tm, aw), lambda i: (i, 0)), pl.BlockSpec((tm, aw), lambda i: (i, 1)),
                  pl.BlockSpec((tm, aw), lambda i: (i, 0)),
                  _const_spec(sp["w"].shape), _const_spec(sp["w"].shape), _const_spec(sp["bias"].shape)]
        + [_const_spec((1, aw))] * 4,
        out_specs=[pl.BlockSpec((tm, 2 * aw), lambda i: (i, 0)), pl.BlockSpec((tm, aw), lambda i: (i, 0)),
                   _const_spec(sp["w"].shape), _const_spec((8, CHUNK))] + [_const_spec((1, aw))] * 5,
        out_shape=[jax.ShapeDtypeStruct((s, 4 * aw), BF16), jax.ShapeDtypeStruct((s, aw), F32),
                   jax.ShapeDtypeStruct(sp["w"].shape, F32), jax.ShapeDtypeStruct((8, CHUNK), F32)] + [vec] * 5,
        scratch_shapes=[pltpu.VMEM((tm, aw), F32)] * 4,
        compiler_params=_params("arbitrary"),
    )(dy, dy, z, z, gc, sp["w"], sp["wt"], sp["bias"], sp["vg"], sp["vb"], sp["cg"], sp["cbn"])


def conv_backward(dgc, z, dz_in, sp, name, comm=None):
    s = z.shape[0]
    aw = z.shape[1] // 4
    tm = _tile(s, 512)
    nt = s // tm
    off = CONV_HALO - CONV_W // 2

    def body(d_ref, dp_ref, dn_ref, zb_ref, zp_ref, zn_ref, cw_ref, dzin_ref, dz_ref, dcw_ref,
             padd_ref, padg_ref, dgg_ref, shd_ref, shg_ref):
        i = pl.program_id(0)
        _fill_padded(padd_ref, dp_ref[...], d_ref[...], dn_ref[...], i, nt, tm)
        _fill_padded(padg_ref, _glu(zp_ref[...]), _glu(zb_ref[...]), _glu(zn_ref[...]), i, nt, tm)
        _fill_shifts(shd_ref, padd_ref)
        _fill_shifts(shg_ref, padg_ref)

        @pl.when(i == 0)
        def _():
            dcw_ref[...] = jnp.zeros_like(dcw_ref)

        def grad_input(cols, rb):
            acc = jnp.zeros((CHUNK, LANES), F32)
            for k in range(CONV_W):
                acc = acc + cw_ref[k:k + 1, cols] * _tap(padd_ref, shd_ref, rb * CHUNK + CONV_HALO + CONV_W // 2 - k, cols)
            dgg_ref[rb * CHUNK:(rb + 1) * CHUNK, cols] = acc

        def grad_taps(cols, rb):
            dblk = d_ref[rb * CHUNK:(rb + 1) * CHUNK, cols]
            for k in range(CONV_W):
                prod = dblk * _tap(padg_ref, shg_ref, rb * CHUNK + off + k, cols)
                dcw_ref[k:k + 1, cols] += jnp.sum(prod, axis=0, keepdims=True)

        for lb in range(aw // LANES):
            for rb in range(tm // CHUNK):
                pl.when(i >= 0)(functools.partial(grad_input, pl.ds(lb * LANES, LANES), rb))
                pl.when(i >= 0)(functools.partial(grad_taps, pl.ds(lb * LANES, LANES), rb))

        zb = zb_ref[...]
        val, sg = zb[:, :aw], _sigmoid(zb[:, aw:])
        dgg = dgg_ref[...]
        dz_ref[:, :aw] = (dgg * sg).astype(BF16)
        dz_ref[:, aw:] = (dgg * val * sg * (1.0 - sg)).astype(BF16)

    return _pallas(
        body, name=name, grid=(nt,), args=[dgc, dgc, dgc, z, z, z, sp["cw"], dz_in], comm=comm,
        in_specs=_halo_specs(tm, s, aw, 0) + _halo_specs(tm, s, 2 * aw, 1)
        + [_const_spec(sp["cw"].shape), pl.BlockSpec(memory_space=pl.ANY)],
        out_specs=[pl.BlockSpec((tm, 2 * aw), lambda i: (i, 1)), _const_spec(sp["cw"].shape)],
        out_shape=[jax.ShapeDtypeStruct((s, 4 * aw), BF16), jax.ShapeDtypeStruct(sp["cw"].shape, F32)],
        scratch=[pltpu.VMEM((tm + 2 * CONV_HALO, aw), F32)] * 2 + [pltpu.VMEM((tm, aw), F32)]
        + [_shift_scratch(tm, aw)] * 2,
        aliases={7: 0}, sem=("arbitrary",))


def rope_tables(s):
    pos = jnp.arange(s, dtype=F32)
    inv_freq = ROPE_THETA ** (-jnp.arange(0, ROT_DIM, 2, dtype=F32) / ROT_DIM)
    ang = pos[:, None] * inv_freq[None, :]
    cos, sin = jnp.cos(ang), jnp.sin(ang)
    half = ROT_DIM // 2
    rest = HEAD_DIM - ROT_DIM
    one, zero, zrest = jnp.ones((s, rest), F32), jnp.zeros((s, half), F32), jnp.zeros((s, rest), F32)
    c = jnp.concatenate([cos, cos, one], axis=1)
    s1 = jnp.concatenate([-sin, zero, zrest], axis=1)
    s2 = jnp.concatenate([zero, sin, zrest], axis=1)
    return tuple(jnp.tile(t, (1, LANES // HEAD_DIM)) for t in (c, s1, s2))


def qk_prep_forward(qkv, tabs, gains, name, comm=None):
    s, w3 = qkv.shape
    w = w3 // 3
    tm = _tile(s, 512)

    def body(x_ref, c_ref, s1_ref, s2_ref, g_ref, o_ref):
        bd = _block_diag(LANES)
        for rb in range(tm // CHUNK):
            rows = pl.ds(rb * CHUNK, CHUNK)
            c, s1, s2 = c_ref[rows, :], s1_ref[rows, :], s2_ref[rows, :]
            for b in range(w // LANES):
                cols = pl.ds(b * LANES, LANES)
                t = x_ref[rows, cols]
                r = lax.rsqrt(_seg_sum(t * t, bd) * (1.0 / HEAD_DIM) + EPS)
                y = t * r * g_ref[...]
                o_ref[rows, cols] = (y * c + pltpu.roll(y, LANES - ROT_DIM // 2, 1) * s1
                                     + pltpu.roll(y, ROT_DIM // 2, 1) * s2)

    tab = pl.BlockSpec((tm, LANES), lambda i, p: (i, 0))
    outs, couts = _pallas(
        body, name=name, grid=(s // tm, 2), args=[qkv, *tabs, gains], comm=comm,
        in_specs=[pl.BlockSpec((tm, w), lambda i, p: (i, p)), tab, tab, tab,
                  pl.BlockSpec((None, 1, LANES), lambda i, p: (p, 0, 0))],
        out_specs=[pl.BlockSpec((tm, w), lambda i, p: (i, p))],
        out_shape=[jax.ShapeDtypeStruct((s, 2 * w), F32)],
        sem=("parallel", "arbitrary"))
    return outs[0], couts


def qk_prep_backward(dq, dq_prev, dq_next, dk, dv, qkv, tabs, gains, name):
    s, w3 = qkv.shape
    w = w3 // 3
    tm = _tile(s, 512)
    t = _tile(s, ATT_TILE)
    per_tile, per_halo, tiles = t // tm, ATT_HALO // tm, s // t

    def neighbours(i):
        tile, c = i // per_tile, i % per_tile
        from_before = (c < per_halo) & (tile >= 1)
        from_after = (c >= per_tile - per_halo) & (tile + 1 < tiles)
        return (from_before, from_after, jnp.where(from_before, (tile - 1) * per_halo + c, 0),
                jnp.where(from_after, (tile + 1) * per_halo + c - (per_tile - per_halo), 0))

    def body(*refs):
        grads = ((refs[0], refs[1], refs[2]), (refs[3],), (refs[4],))
        x_ref, c_ref, s1_ref, s2_ref, g_ref, o_ref, dg_ref = refs[5:]
        part, first = pl.program_id(0), pl.program_id(1) == 0
        from_before, from_after, _, _ = neighbours(pl.program_id(1))

        def normed(ds):
            bd = _block_diag(LANES)
            acc = jnp.zeros((1, LANES), F32)
            for rb in range(tm // CHUNK):
                rows = pl.ds(rb * CHUNK, CHUNK)
                c, s1, s2 = c_ref[rows, :], s1_ref[rows, :], s2_ref[rows, :]
                for b in range(w // LANES):
                    cols = pl.ds(b * LANES, LANES)
                    dout = ds[0][rows, cols]
                    if len(ds) == 3:
                        dout = (dout + jnp.where(from_after, ds[1][rows, cols], 0.0)
                                + jnp.where(from_before, ds[2][rows, cols], 0.0))
                    dy = (dout * c + pltpu.roll(dout * s1, ROT_DIM // 2, 1)
                          + pltpu.roll(dout * s2, LANES - ROT_DIM // 2, 1))
                    t = x_ref[rows, cols]
                    r = lax.rsqrt(_seg_sum(t * t, bd) * (1.0 / HEAD_DIM) + EPS)
                    xh = t * r
                    acc = acc + jnp.sum(dy * xh, axis=0, keepdims=True)
                    tt = dy * g_ref[...]
                    o_ref[rows, cols] = (r * (tt - xh * (_seg_sum(tt * xh, bd) * (1.0 / HEAD_DIM)))).astype(BF16)
            _accumulate(dg_ref, acc, first)

        for p in range(2):
            pl.when(part == p)(functools.partial(normed, grads[p]))

        @pl.when(part == 2)
        def _():
            o_ref[...] = grads[2][0][...].astype(BF16)
            _accumulate(dg_ref, jnp.zeros((1, LANES), F32), first)

    def gspec(p):
        return pl.BlockSpec((tm, w), lambda q, i: (jnp.where(q == p, i, 0), 0))

    prev_spec = pl.BlockSpec((tm, w), lambda q, i: (jnp.where(q == 0, neighbours(i)[3], 0), 0))
    next_spec = pl.BlockSpec((tm, w), lambda q, i: (jnp.where(q == 0, neighbours(i)[2], 0), 0))
    tab = pl.BlockSpec((tm, LANES), lambda q, i: (i, 0))
    return pl.pallas_call(
        body, name=name, grid=(3, s // tm),
        in_specs=[gspec(0), prev_spec, next_spec, gspec(1), gspec(2)]
        + [pl.BlockSpec((tm, w), lambda q, i: (i, q)), tab, tab, tab,
           pl.BlockSpec((None, 1, LANES), lambda q, i: (q, 0, 0))],
        out_specs=[pl.BlockSpec((tm, w), lambda q, i: (i, q)), pl.BlockSpec((None, 1, LANES), lambda q, i: (q, 0, 0))],
        out_shape=[jax.ShapeDtypeStruct((s, w3), BF16), jax.ShapeDtypeStruct((3, 1, LANES), F32)],
        compiler_params=_params("arbitrary", "arbitrary"),
    )(dq, dq_prev, dq_next, dk, dv, qkv, *tabs, gains)


def _window_specs(tq, l, col_fn):
    hb, nhb = tq // BAND, l // BAND
    return [pl.BlockSpec((tq, LANES), lambda c, i: (i, col_fn(c))),
            pl.BlockSpec((BAND, LANES), lambda c, i: (jnp.maximum(i * hb - 1, 0), col_fn(c))),
            pl.BlockSpec((BAND, LANES), lambda c, i: (jnp.minimum((i + 1) * hb, nhb - 1), col_fn(c)))]


def _window(cur_ref, prev_ref, next_ref):
    return jnp.concatenate([prev_ref[...], cur_ref[...], next_ref[...]], axis=0)


def _band_mask(shape, centre_axis, first_row, length):
    ctr = lax.broadcasted_iota(jnp.int32, shape, centre_axis)
    win = lax.broadcasted_iota(jnp.int32, shape, 1 - centre_axis)
    row = first_row + win
    return (jnp.abs(win - BAND - ctr) <= BAND) & (row >= 0) & (row < length)


def _col_q(d):
    return lambda c: (c // 8) * 24 + c % 8


def _col_k(d):
    return lambda c: (c // 8) * 24 + 8 + c % 8


def _col_v(d):
    return lambda c: (c // 8) * 24 + 16 + c % 8


def band_attention_forward(qkvn, d, name):
    s, w3 = qkvn.shape
    w = w3 // 3
    l = s // d
    tq = _tile(l, 512)
    scale = HEAD_DIM ** -0.5
    xv = qkvn.reshape(l, d * w3)

    def body(q_ref, k_ref, kp_ref, kn_ref, v_ref, vp_ref, vn_ref, o_ref, lse_ref):
        i = pl.program_id(1)
        kw, vw = _window(k_ref, kp_ref, kn_ref), _window(v_ref, vp_ref, vn_ref)
        head0 = lax.broadcasted_iota(jnp.int32, (CHUNK, LANES), 1) < HEAD_DIM
        for b in range(tq // CHUNK):
            rows = pl.ds(b * CHUNK, CHUNK)
            mask = _band_mask((CHUNK, 2 * CHUNK), 0, i * tq + b * CHUNK - BAND, l)
            qb = q_ref[rows, :]
            kb, vb = kw[b * CHUNK:(b + 2) * CHUNK], vw[b * CHUNK:(b + 2) * CHUNK]
            outs, lses = [], []
            for hm in (head0, jnp.logical_not(head0)):
                sc = jnp.where(mask, _dot_nt(jnp.where(hm, qb, jnp.zeros_like(qb)), kb) * scale, NEG)
                m = jnp.max(sc, axis=1, keepdims=True)
                p = jnp.exp(sc - m)
                den = jnp.sum(p, axis=1, keepdims=True)
                outs.append(_dot(p.astype(BF16), vb) / den)
                lses.append(jnp.broadcast_to(m + jnp.log(den), (CHUNK, LANES)))
            o_ref[rows, :] = jnp.where(head0, outs[0], outs[1]).astype(BF16)
            lse_ref[rows, :] = jnp.where(head0, lses[0], lses[1])

    ospec = pl.BlockSpec((tq, LANES), lambda c, i: (i, c))
    o, lse = pl.pallas_call(
        body, name=name, grid=(d * w // LANES, l // tq),
        in_specs=[pl.BlockSpec((tq, LANES), lambda c, i: (i, _col_q(d)(c)))]
        + _window_specs(tq, l, _col_k(d)) + _window_specs(tq, l, _col_v(d)),
        out_specs=[ospec, ospec],
        out_shape=[jax.ShapeDtypeStruct((l, d * w), BF16), jax.ShapeDtypeStruct((l, d * w), F32)],
        compiler_params=_params("parallel", "parallel"),
    )(xv, xv, xv, xv, xv, xv, xv)
    return o.reshape(s, w), lse.reshape(s, w)


def attention_merge(os_, lses, name):
    s, w = os_[0].shape
    tm = _tile(s, 512)

    def body(o0, o1, o2, l0, l1, l2, o_ref, lse_ref):
        la, lb, lc = l0[...], l1[...], l2[...]
        m = jnp.maximum(jnp.maximum(la, lb), lc)
        wa, wb, wc = jnp.exp(la - m), jnp.exp(lb - m), jnp.exp(lc - m)
        den = wa + wb + wc
        o = (wa * o0[...].astype(F32) + wb * o1[...].astype(F32) + wc * o2[...].astype(F32)) / den
        o_ref[...] = o.astype(BF16)
        lse_ref[...] = m + jnp.log(den)

    spec = pl.BlockSpec((tm, w), lambda i: (i, 0))
    return pl.pallas_call(
        body, name=name, grid=(s // tm,), in_specs=[spec] * 6, out_specs=[spec, spec],
        out_shape=[jax.ShapeDtypeStruct((s, w), BF16), jax.ShapeDtypeStruct((s, w), F32)],
        compiler_params=_params("parallel"),
    )(*os_, *lses)


def attention_delta(do, o, name):
    s, w = do.shape
    tm = _tile(s, 512)

    def body(do_ref, o_ref, dl_ref, dob_ref):
        bd = _block_diag(LANES)
        for b in range(w // LANES):
            cols = pl.ds(b * LANES, LANES)
            dv = do_ref[:, cols]
            dl_ref[:, cols] = _seg_sum(dv * o_ref[:, cols].astype(F32), bd)
            dob_ref[:, cols] = dv.astype(BF16)

    spec = pl.BlockSpec((tm, w), lambda i: (i, 0))
    return pl.pallas_call(
        body, name=name, grid=(s // tm,), in_specs=[spec, spec], out_specs=[spec, spec],
        out_shape=[jax.ShapeDtypeStruct((s, w), F32), jax.ShapeDtypeStruct((s, w), BF16)],
        compiler_params=_params("parallel"),
    )(do, o)


def band_attention_dq(qkvn, dob, lse, delta, d, name):
    s, w3 = qkvn.shape
    w = w3 // 3
    l = s // d
    tq = _tile(l, 512)
    scale = HEAD_DIM ** -0.5
    xv = qkvn.reshape(l, d * w3)

    def body(q_ref, k_ref, kp_ref, kn_ref, v_ref, vp_ref, vn_ref, do_ref, lse_ref, dl_ref, dq_ref):
        i = pl.program_id(1)
        kw, vw = _window(k_ref, kp_ref, kn_ref), _window(v_ref, vp_ref, vn_ref)
        head0 = lax.broadcasted_iota(jnp.int32, (CHUNK, LANES), 1) < HEAD_DIM
        for b in range(tq // CHUNK):
            rows = pl.ds(b * CHUNK, CHUNK)
            mask = _band_mask((CHUNK, 2 * CHUNK), 0, i * tq + b * CHUNK - BAND, l)
            qb, dob_ = q_ref[rows, :], do_ref[rows, :]
            kb, vb = kw[b * CHUNK:(b + 2) * CHUNK], vw[b * CHUNK:(b + 2) * CHUNK]
            outs = []
            for h, hm in enumerate((head0, jnp.logical_not(head0))):
                col = pl.ds(h * HEAD_DIM, 1)
                sc = jnp.where(mask, _dot_nt(jnp.where(hm, qb, jnp.zeros_like(qb)), kb) * scale, NEG)
                p = jnp.exp(sc - lse_ref[rows, col])
                dp = _dot_nt(jnp.where(hm, dob_, jnp.zeros_like(dob_)), vb)
                ds = p * (dp - dl_ref[rows, col]) * scale
                outs.append(_dot(ds.astype(BF16), kb))
            dq_ref[rows, :] = jnp.where(head0, outs[0], outs[1])

    ospec = pl.BlockSpec((tq, LANES), lambda c, i: (i, c))
    dq = pl.pallas_call(
        body, name=name, grid=(d * w // LANES, l // tq),
        in_specs=[pl.BlockSpec((tq, LANES), lambda c, i: (i, _col_q(d)(c)))]
        + _window_specs(tq, l, _col_k(d)) + _window_specs(tq, l, _col_v(d)) + [ospec, ospec, ospec],
        out_specs=ospec,
        out_shape=jax.ShapeDtypeStruct((l, d * w), F32),
        compiler_params=_params("parallel", "parallel"),
    )(xv, xv, xv, xv, xv, xv, xv, dob.reshape(l, d * w), lse.reshape(l, d * w), delta.reshape(l, d * w))
    return dq.reshape(s, w)


def band_attention_dkv(qkvn, dob, lse, delta, d, name):
    s, w3 = qkvn.shape
    w = w3 // 3
    l = s // d
    tq = _tile(l, 512)
    scale = HEAD_DIM ** -0.5
    xv = qkvn.reshape(l, d * w3)

    def body(k_ref, v_ref, q_ref, qp_ref, qn_ref, do_ref, dop_ref, don_ref, lse_ref, lsep_ref, lsen_ref,
             dl_ref, dlp_ref, dln_ref, dk_ref, dv_ref):
        i = pl.program_id(1)
        qw, dow = _window(q_ref, qp_ref, qn_ref), _window(do_ref, dop_ref, don_ref)
        lsew, dlw = _window(lse_ref, lsep_ref, lsen_ref), _window(dl_ref, dlp_ref, dln_ref)
        head0 = lax.broadcasted_iota(jnp.int32, (2 * CHUNK, LANES), 1) < HEAD_DIM
        for b in range(tq // CHUNK):
            rows = pl.ds(b * CHUNK, CHUNK)
            mask = _band_mask((2 * CHUNK, CHUNK), 1, i * tq + b * CHUNK - BAND, l)
            kb, vb = k_ref[rows, :], v_ref[rows, :]
            win = slice(b * CHUNK, (b + 2) * CHUNK)
            qb, dob_, lseb, dlb = qw[win], dow[win], lsew[win], dlw[win]
            dk = jnp.zeros((CHUNK, LANES), F32)
            dv = jnp.zeros((CHUNK, LANES), F32)
            for h, hm in enumerate((head0, jnp.logical_not(head0))):
                col = slice(h * HEAD_DIM, h * HEAD_DIM + 1)
                qm = jnp.where(hm, qb, jnp.zeros_like(qb))
                dom = jnp.where(hm, dob_, jnp.zeros_like(dob_))
                sc = jnp.where(mask, _dot_nt(qm, kb) * scale, NEG)
                p = jnp.exp(sc - lseb[:, col])
                ds = p * (_dot_nt(dom, vb) - dlb[:, col]) * scale
                dv = dv + _dot_tn(p.astype(BF16), dom)
                dk = dk + _dot_tn(ds.astype(BF16), qm)
            dk_ref[rows, :] = dk
            dv_ref[rows, :] = dv

    ident = lambda c: c
    ospec = pl.BlockSpec((tq, LANES), lambda c, i: (i, c))
    dk, dv = pl.pallas_call(
        body, name=name, grid=(d * w // LANES, l // tq),
        in_specs=[pl.BlockSpec((tq, LANES), lambda c, i: (i, _col_k(d)(c))),
                  pl.BlockSpec((tq, LANES), lambda c, i: (i, _col_v(d)(c)))]
        + _window_specs(tq, l, _col_q(d)) + _window_specs(tq, l, ident) * 3,
        out_specs=[ospec, ospec],
        out_shape=[jax.ShapeDtypeStruct((l, d * w), F32)] * 2,
        compiler_params=_params("parallel", "parallel"),
    )(xv, xv, xv, xv, xv, *[dob.reshape(l, d * w)] * 3, *[lse.reshape(l, d * w)] * 3, *[delta.reshape(l, d * w)] * 3)
    return dk.reshape(s, w), dv.reshape(s, w)


ATT_TILE = 2048
ATT_HALO = BAND * max(DILATIONS)
ROWS_PER_COPY = 256


def _att_specs(s, t, col_fn, halo=True):
    hb, nhb = t // ATT_HALO, s // ATT_HALO
    specs = [pl.BlockSpec((t, LANES), lambda hp, i: (i, col_fn(hp)))]
    if halo:
        specs += [pl.BlockSpec((ATT_HALO, LANES), lambda hp, i: (jnp.maximum(i * hb - 1, 0), col_fn(hp))),
                  pl.BlockSpec((ATT_HALO, LANES), lambda hp, i: (jnp.minimum((i + 1) * hb, nhb - 1), col_fn(hp)))]
    return specs


def _gather_rows(dst_ref, dst_row, src_ref, start, count, stride, scale=None):
    for c in range(0, count, ROWS_PER_COPY):
        m = min(ROWS_PER_COPY, count - c)
        v = src_ref[pl.ds(start + c * stride, m, stride=stride), :]
        if scale is not None:
            v = v * scale
        dst_ref[dst_row + c:dst_row + c + m, :] = v.astype(dst_ref.dtype)


SPLIT = 4


def _split_rows(tmp_ref, base, src_ref, rows):
    part = rows // SPLIT
    for b in range(SPLIT):
        _gather_rows(tmp_ref, base + b * part, src_ref, b, part, SPLIT)


def _stage(dst_ref, cur_ref, d, t, scale=None, tmp_ref=None):
    n = t // d
    if tmp_ref is None or d != SPLIT * SPLIT:
        for r in range(d):
            _gather_rows(dst_ref, r * n, cur_ref, r, n, d, scale)
        return
    _split_rows(tmp_ref, 0, cur_ref, t)
    for r in range(d):
        _gather_rows(dst_ref, r * n, tmp_ref, (r % SPLIT) * (t // SPLIT) + r // SPLIT, n, SPLIT, scale)


def _stage_window(dst_ref, refs, d, t, scale=None, edges=None, tmp_ref=None):
    cur_ref, prev_ref, next_ref = refs
    n = t // d
    nw = n + 2 * BAND
    if tmp_ref is None or d != SPLIT * SPLIT:
        for r in range(d):
            _gather_rows(dst_ref, r * nw, prev_ref, ATT_HALO - BAND * d + r, BAND, d, scale)
            _gather_rows(dst_ref, r * nw + BAND, cur_ref, r, n, d, scale)
            _gather_rows(dst_ref, r * nw + BAND + n, next_ref, r, BAND, d, scale)
    else:
        assert ATT_HALO == BAND * d
        _split_rows(tmp_ref, 0, cur_ref, t)
        _split_rows(tmp_ref, t, prev_ref, ATT_HALO)
        _split_rows(tmp_ref, t + ATT_HALO, next_ref, ATT_HALO)
        for r in range(d):
            a, b = r // SPLIT, r % SPLIT
            _gather_rows(dst_ref, r * nw, tmp_ref, t + b * (ATT_HALO // SPLIT) + a, BAND, SPLIT, scale)
            _gather_rows(dst_ref, r * nw + BAND, tmp_ref, b * (t // SPLIT) + a, n, SPLIT, scale)
            _gather_rows(dst_ref, r * nw + BAND + n, tmp_ref, t + ATT_HALO + b * (ATT_HALO // SPLIT) + a, BAND, SPLIT, scale)
    if edges is not None:
        first, last, value = edges
        fill = jnp.full((BAND, LANES), value, dst_ref.dtype)

        @pl.when(first)
        def _():
            for r in range(d):
                dst_ref[r * nw:r * nw + BAND, :] = fill

        @pl.when(last)
        def _():
            for r in range(d):
                dst_ref[r * nw + BAND + n:(r + 1) * nw, :] = fill


def _band_bias(rows, cols, centre_axis):
    shape = (rows // 2, cols)
    ctr = lax.broadcasted_iota(jnp.int32, shape, centre_axis)
    win = lax.broadcasted_iota(jnp.int32, shape, 1 - centre_axis)
    bias = jnp.where(jnp.abs(win - BAND - ctr) <= BAND, 0.0, NEG).astype(F32)
    return jnp.concatenate([bias, bias], axis=0)


def _window_bias(first_row, length):
    row = first_row + lax.broadcasted_iota(jnp.int32, (1, 2 * CHUNK), 1)
    return jnp.where((row >= 0) & (row < length), 0.0, NEG).astype(F32)


UNITS_PER_TRIP = 8


def _scatter_rows(dst_ref, src_ref, d, t, combine):
    n = t // d
    for r in range(d):
        for c in range(0, n, ROWS_PER_COPY):
            m = min(ROWS_PER_COPY, n - c)
            idx = pl.ds(r + c * d, m, stride=d)
            combine(idx, slice(r * n + c, r * n + c + m))


def _unit_rows(u, n):
    upr = n // CHUNK
    r = u // upr
    b = u - r * upr
    return pl.multiple_of(u * CHUNK, CHUNK), pl.multiple_of((u + r) * CHUNK, CHUNK), b * CHUNK - BAND


def _two_heads(x):
    head0 = lax.broadcasted_iota(jnp.int32, x.shape, 1) < HEAD_DIM
    zero = jnp.zeros_like(x)
    return jnp.concatenate([jnp.where(head0, x, zero), jnp.where(head0, zero, x)], axis=0)


def _head_columns(x):
    return jnp.concatenate([x[:, 0:1], x[:, HEAD_DIM:HEAD_DIM + 1]], axis=0)


def _merge_heads(x2):
    rows = x2.shape[0] // 2
    head0 = lax.broadcasted_iota(jnp.int32, (rows, LANES), 1) < HEAD_DIM
    return jnp.where(head0, jnp.broadcast_to(x2[:rows], (rows, LANES)), jnp.broadcast_to(x2[rows:], (rows, LANES)))


def _col(part):
    return lambda hp: part * 8 + hp


def attention_forward(qk, qkv, name, comm=None):
    s, w2 = qk.shape
    w = w2 // 2
    t = _tile(s, ATT_TILE)
    scale = HEAD_DIM ** -0.5

    def body(q_ref, k_ref, kp_ref, kn_ref, v_ref, vp_ref, vn_ref, o_ref, lse_ref,
             qs_ref, ks_ref, vs_ref, os_ref, ls_ref, or_ref, tmp_ref):
        i = pl.program_id(1)
        band = _band_bias(2 * CHUNK, 2 * CHUNK, 0)
        for pi, d in enumerate(DILATIONS):
            n = t // d
            _stage(qs_ref, q_ref, d, t, scale, tmp_ref=tmp_ref)
            _stage_window(ks_ref, (k_ref, kp_ref, kn_ref), d, t, tmp_ref=tmp_ref)
            _stage_window(vs_ref, (v_ref, vp_ref, vn_ref), d, t, tmp_ref=tmp_ref)

            def unit(u, carry, n=n, d=d):
                qrow, wrow, first = _unit_rows(u, n)
                kb, vb = ks_ref[pl.ds(wrow, 2 * CHUNK), :], vs_ref[pl.ds(wrow, 2 * CHUNK), :]
                sc = _dot_nt(_two_heads(qs_ref[pl.ds(qrow, CHUNK), :]), kb) + band + _window_bias(i * n + first, s // d)
                m = jnp.max(sc, axis=1, keepdims=True)
                p = jnp.exp(sc - m)
                den = jnp.sum(p, axis=1, keepdims=True)
                os_ref[pl.ds(qrow, CHUNK), :] = _merge_heads(_dot(p.astype(BF16), vb) / den)
                ls_ref[pl.ds(qrow, CHUNK), :] = _merge_heads(m + jnp.log(den))
                return carry

            lax.fori_loop(0, t // CHUNK, unit, 0, unroll=2 * UNITS_PER_TRIP)

            if pi == 0:
                def assign(idx, rows):
                    or_ref[idx, :] = os_ref[rows, :]
                    lse_ref[idx, :] = ls_ref[rows, :]
                _scatter_rows(None, None, d, t, assign)
            else:
                def merge(idx, rows):
                    la, lb = lse_ref[idx, :], ls_ref[rows, :]
                    mx = jnp.maximum(la, lb)
                    wa, wb = jnp.exp(la - mx), jnp.exp(lb - mx)
                    den = wa + wb
                    or_ref[idx, :] = (wa * or_ref[idx, :] + wb * os_ref[rows, :]) / den
                    lse_ref[idx, :] = mx + jnp.log(den)
                _scatter_rows(None, None, d, t, merge)
        o_ref[...] = or_ref[...].astype(BF16)

    ospec = pl.BlockSpec((t, LANES), lambda hp, i: (i, hp))
    win_rows = t + 2 * ATT_HALO
    return _pallas(
        body, name=name, grid=(w // LANES, s // t), args=[qk, qk, qk, qk, qkv, qkv, qkv], comm=comm,
        in_specs=_att_specs(s, t, _col(0), halo=False) + _att_specs(s, t, _col(1)) + _att_specs(s, t, _col(2)),
        out_specs=[ospec, ospec],
        out_shape=[jax.ShapeDtypeStruct((s, w), BF16), jax.ShapeDtypeStruct((s, w), F32)],
        scratch=[pltpu.VMEM((t, LANES), BF16), pltpu.VMEM((win_rows, LANES), BF16), pltpu.VMEM((win_rows, LANES), BF16),
                 pltpu.VMEM((t, LANES), F32), pltpu.VMEM((t, LANES), F32), pltpu.VMEM((t, LANES), F32),
                 pltpu.VMEM((win_rows, LANES), F32)],
        sem=("parallel", "parallel"))


def attention_delta(do, o, lse, name):
    s, w = do.shape
    tm = _tile(s, 512)

    def body(do_ref, o_ref, lse_ref, st_ref):
        bd = _block_diag(LANES)
        lane = lax.broadcasted_iota(jnp.int32, (CHUNK, LANES), 1)
        for rb in range(tm // CHUNK):
            rows = pl.ds(rb * CHUNK, CHUNK)
            for b in range(w // LANES):
                cols = pl.ds(b * LANES, LANES)
                dl = _seg_sum(do_ref[rows, cols] * o_ref[rows, cols].astype(F32), bd)
                ls = lse_ref[rows, cols]
                st_ref[rows, cols] = jnp.where(lane == 0, ls, jnp.where(
                    lane == 1, pltpu.roll(ls, HEAD_DIM - 1, 1), jnp.where(
                        lane == 2, pltpu.roll(dl, 2, 1), pltpu.roll(dl, HEAD_DIM + 3, 1))))

    spec = pl.BlockSpec((tm, w), lambda i: (i, 0))
    return pl.pallas_call(
        body, name=name, grid=(s // tm,), in_specs=[spec, spec, spec], out_specs=spec,
        out_shape=jax.ShapeDtypeStruct((s, w), F32), compiler_params=_params("parallel"),
    )(do, o, lse)


def attention_dq(qk, qkv, do, lse, delta, name, comm=None):
    s, w2 = qk.shape
    w = w2 // 2
    t = _tile(s, ATT_TILE)
    scale = HEAD_DIM ** -0.5

    def body(q_ref, k_ref, kp_ref, kn_ref, v_ref, vp_ref, vn_ref, do_ref, lse_ref, dl_ref, dq_ref,
             qs_ref, ks_ref, vs_ref, dos_ref, ls_ref, dls_ref, dqs_ref):
        i = pl.program_id(1)
        band = _band_bias(2 * CHUNK, 2 * CHUNK, 0)
        for pi, d in enumerate(DILATIONS):
            n = t // d
            _stage(qs_ref, q_ref, d, t, scale)
            _stage(dos_ref, do_ref, d, t)
            _stage(ls_ref, lse_ref, d, t)
            _stage(dls_ref, dl_ref, d, t)
            _stage_window(ks_ref, (k_ref, kp_ref, kn_ref), d, t)
            _stage_window(vs_ref, (v_ref, vp_ref, vn_ref), d, t)

            def unit(u, carry, n=n, d=d):
                qrow, wrow, first = _unit_rows(u, n)
                rows = pl.ds(qrow, CHUNK)
                kb, vb = ks_ref[pl.ds(wrow, 2 * CHUNK), :], vs_ref[pl.ds(wrow, 2 * CHUNK), :]
                sc = _dot_nt(_two_heads(qs_ref[rows, :]), kb) + band + _window_bias(i * n + first, s // d)
                p = jnp.exp(sc - _head_columns(ls_ref[rows, :]))
                dp = _dot_nt(_two_heads(dos_ref[rows, :]), vb)
                ds = p * (dp - _head_columns(dls_ref[rows, :]))
                dqs_ref[rows, :] = _merge_heads(_dot(ds.astype(BF16), kb)) * scale
                return carry

            lax.fori_loop(0, t // CHUNK, unit, 0, unroll=UNITS_PER_TRIP)

            def add(idx, rows, pi=pi):
                dq_ref[idx, :] = dqs_ref[rows, :] if pi == 0 else dq_ref[idx, :] + dqs_ref[rows, :]
            _scatter_rows(None, None, d, t, add)

    ospec = pl.BlockSpec((t, LANES), lambda hp, i: (i, hp))
    win_rows = t + 2 * ATT_HALO
    return _pallas(
        body, name=name, grid=(w // LANES, s // t), args=[qk, qk, qk, qk, qkv, qkv, qkv, do, lse, delta], comm=comm,
        in_specs=_att_specs(s, t, _col(0), halo=False) + _att_specs(s, t, _col(1)) + _att_specs(s, t, _col(2))
        + [ospec, ospec, ospec],
        out_specs=[ospec],
        out_shape=[jax.ShapeDtypeStruct((s, w), F32)],
        scratch=[pltpu.VMEM((t, LANES), BF16), pltpu.VMEM((win_rows, LANES), BF16), pltpu.VMEM((win_rows, LANES), BF16),
                 pltpu.VMEM((t, LANES), BF16), pltpu.VMEM((t, LANES), F32), pltpu.VMEM((t, LANES), F32),
                 pltpu.VMEM((t, LANES), F32)],
        sem=("parallel", "parallel"))


def attention_dkv(qk, qkv, do, stats, name):
    s, w2 = qk.shape
    w = w2 // 2
    t = _tile(s, ATT_TILE)
    scale = HEAD_DIM ** -0.5

    def body(k_ref, v_ref, q_ref, qp_ref, qn_ref, do_ref, dop_ref, don_ref, st_ref, stp_ref, stn_ref,
             dk_ref, dv_ref, ks_ref, vs_ref, qs_ref, dos_ref, sts_ref, dks_ref, dvs_ref):
        i = pl.program_id(1)
        key = lax.broadcasted_iota(jnp.int32, (CHUNK, 2 * CHUNK), 0)
        win_ = lax.broadcasted_iota(jnp.int32, (CHUNK, 2 * CHUNK), 1)
        half = jnp.where(jnp.abs(win_ - BAND - key) <= BAND, 0.0, NEG).astype(F32)
        band = jnp.concatenate([half, half], axis=1)
        edges = (i == 0, i == s // t - 1, -NEG)
        for pi, d in enumerate(DILATIONS):
            n = t // d
            _stage(ks_ref, k_ref, d, t)
            _stage(vs_ref, v_ref, d, t)
            _stage_window(qs_ref, (q_ref, qp_ref, qn_ref), d, t, scale)
            _stage_window(dos_ref, (do_ref, dop_ref, don_ref), d, t)
            _stage_window(sts_ref, (st_ref, stp_ref, stn_ref), d, t, edges=edges)

            def unit(u, carry, n=n, d=d):
                krow, wrow, _ = _unit_rows(u, n)
                rows, win = pl.ds(krow, CHUNK), pl.ds(wrow, 2 * CHUNK)
                q2, do2 = _two_heads(qs_ref[win, :]), _two_heads(dos_ref[win, :])
                st = jnp.transpose(sts_ref[win, :])
                lse2 = jnp.concatenate([st[0:1, :], st[1:2, :]], axis=1)
                dl2 = jnp.concatenate([st[2:3, :], st[3:4, :]], axis=1)
                p = jnp.exp(_dot_nt(ks_ref[rows, :], q2) + band - lse2)
                ds = p * (_dot_nt(vs_ref[rows, :], do2) - dl2)
                dvs_ref[rows, :] = _dot(p.astype(BF16), do2)
                dks_ref[rows, :] = _dot(ds.astype(BF16), q2)
                return carry

            lax.fori_loop(0, t // CHUNK, unit, 0, unroll=UNITS_PER_TRIP)

            def add(idx, rows, pi=pi):
                dk_ref[idx, :] = dks_ref[rows, :] if pi == 0 else dk_ref[idx, :] + dks_ref[rows, :]
                dv_ref[idx, :] = dvs_ref[rows, :] if pi == 0 else dv_ref[idx, :] + dvs_ref[rows, :]
            _scatter_rows(None, None, d, t, add)

    ident = lambda hp: hp
    ospec = pl.BlockSpec((t, LANES), lambda hp, i: (i, hp))
    win_rows = t + 2 * ATT_HALO
    return pl.pallas_call(
        body, name=name, grid=(w // LANES, s // t),
        in_specs=_att_specs(s, t, _col(1), halo=False) + _att_specs(s, t, _col(2), halo=False)
        + _att_specs(s, t, _col(0)) + _att_specs(s, t, ident) * 2,
        out_specs=[ospec, ospec],
        out_shape=[jax.ShapeDtypeStruct((s, w), F32)] * 2,
        scratch_shapes=[pltpu.VMEM((t, LANES), BF16), pltpu.VMEM((t, LANES), BF16),
                        pltpu.VMEM((win_rows, LANES), BF16), pltpu.VMEM((win_rows, LANES), BF16),
                        pltpu.VMEM((win_rows, LANES), F32),
                        pltpu.VMEM((t, LANES), F32), pltpu.VMEM((t, LANES), F32)],
        compiler_params=_params("parallel", "parallel"),
    )(qk, qkv, qk, qk, qk, do, do, do, stats, stats, stats)


def attention_backward(qk, qkv, do, stats, name, comm=None):
    s, w2 = qk.shape
    w = w2 // 2
    t = _tile(s, ATT_TILE)
    tiles = s // t
    scale = HEAD_DIM ** -0.5

    def body(k_ref, v_ref, q_ref, qp_ref, qn_ref, do_ref, dop_ref, don_ref, st_ref, stp_ref, stn_ref,
             dk_ref, dv_ref, dq_ref, dqp_ref, dqn_ref, ks_ref, vs_ref, qs_ref, dos_ref, sts_ref, dks_ref, dvs_ref, dqw_ref):
        i = pl.program_id(1)
        key = lax.broadcasted_iota(jnp.int32, (CHUNK, 2 * CHUNK), 0)
        win_ = lax.broadcasted_iota(jnp.int32, (CHUNK, 2 * CHUNK), 1)
        half = jnp.where(jnp.abs(win_ - BAND - key) <= BAND, 0.0, NEG).astype(F32)
        band = jnp.concatenate([half, half], axis=1)
        edges = (i == 0, i == tiles - 1, -NEG)
        dqp_ref[...] = jnp.zeros_like(dqp_ref)
        dqn_ref[...] = jnp.zeros_like(dqn_ref)
        for pi, d in enumerate(DILATIONS):
            n = t // d
            nw = n + 2 * BAND
            _stage(ks_ref, k_ref, d, t, tmp_ref=dqw_ref)
            _stage(vs_ref, v_ref, d, t, tmp_ref=dqw_ref)
            _stage_window(qs_ref, (q_ref, qp_ref, qn_ref), d, t, scale, tmp_ref=dqw_ref)
            _stage_window(dos_ref, (do_ref, dop_ref, don_ref), d, t, tmp_ref=dqw_ref)
            _stage_window(sts_ref, (st_ref, stp_ref, stn_ref), d, t, edges=edges, tmp_ref=dqw_ref)
            dqw_ref[...] = jnp.zeros_like(dqw_ref)

            def unit(u, carry, n=n, d=d):
                krow, wrow, _ = _unit_rows(u, n)
                rows, win = pl.ds(krow, CHUNK), pl.ds(wrow, 2 * CHUNK)
                kb = ks_ref[rows, :]
                q2, do2 = _two_heads(qs_ref[win, :]), _two_heads(dos_ref[win, :])
                st = jnp.transpose(sts_ref[win, :])
                lse2 = jnp.concatenate([st[0:1, :], st[1:2, :]], axis=1)
                dl2 = jnp.concatenate([st[2:3, :], st[3:4, :]], axis=1)
                p = jnp.exp(_dot_nt(kb, q2) + band - lse2)
                ds = (p * (_dot_nt(vs_ref[rows, :], do2) - dl2)).astype(BF16)
                dvs_ref[rows, :] = _dot(p.astype(BF16), do2)
                dks_ref[rows, :] = _dot(ds, q2)
                k2 = _two_heads(kb)
                dqw = _dot_tn(ds[:, :2 * CHUNK], k2[:CHUNK]) + _dot_tn(ds[:, 2 * CHUNK:], k2[CHUNK:])
                dqw_ref[win, :] += dqw * scale
                return carry

            lax.fori_loop(0, t // CHUNK, unit, 0, unroll=2 * UNITS_PER_TRIP)

            def add(idx, rows, pi=pi):
                dk_ref[idx, :] = dks_ref[rows, :] if pi == 0 else dk_ref[idx, :] + dks_ref[rows, :]
                dv_ref[idx, :] = dvs_ref[rows, :] if pi == 0 else dv_ref[idx, :] + dvs_ref[rows, :]
            _scatter_rows(None, None, d, t, add)

            for r in range(d):
                before = pl.ds(ATT_HALO - BAND * d + r, BAND, stride=d)
                after = pl.ds(r, BAND, stride=d)
                dqp_ref[before, :] += dqw_ref[r * nw:r * nw + BAND, :]
                dqn_ref[after, :] += dqw_ref[r * nw + BAND + n:(r + 1) * nw, :]
                for c in range(0, n, ROWS_PER_COPY):
                    m = min(ROWS_PER_COPY, n - c)
                    idx = pl.ds(r + c * d, m, stride=d)
                    val = dqw_ref[r * nw + BAND + c:r * nw + BAND + c + m, :]
                    dq_ref[idx, :] = val if pi == 0 else dq_ref[idx, :] + val

    ident = lambda hp: hp
    ospec = pl.BlockSpec((t, LANES), lambda hp, i: (i, hp))
    hspec = pl.BlockSpec((ATT_HALO, LANES), lambda hp, i: (i, hp))
    win_rows = t + 2 * ATT_HALO
    halo = jax.ShapeDtypeStruct((tiles * ATT_HALO, w), F32)
    return _pallas(
        body, name=name, grid=(w // LANES, tiles), comm=comm,
        args=[qk, qkv, qk, qk, qk, do, do, do, stats, stats, stats],
        in_specs=_att_specs(s, t, _col(1), halo=False) + _att_specs(s, t, _col(2), halo=False)
        + _att_specs(s, t, _col(0)) + _att_specs(s, t, ident) * 2,
        out_specs=[ospec, ospec, ospec, hspec, hspec],
        out_shape=[jax.ShapeDtypeStruct((s, w), F32)] * 3 + [halo, halo],
        scratch=[pltpu.VMEM((t, LANES), BF16), pltpu.VMEM((t, LANES), BF16),
                 pltpu.VMEM((win_rows, LANES), BF16), pltpu.VMEM((win_rows, LANES), BF16),
                 pltpu.VMEM((win_rows, LANES), F32),
                 pltpu.VMEM((t, LANES), F32), pltpu.VMEM((t, LANES), F32), pltpu.VMEM((win_rows, LANES), F32)],
        sem=("parallel", "parallel"))


def adamw_update(recvs, w, m, v, name, comm=None):
    nl = len(recvs)
    r, c = recvs[0].shape[1:]
    tr = 256 if (r > 256 and r % 256 == 0) else r
    nt = r // tr
    c1 = 1.0 - ADAM_B1 ** ADAM_STEP
    c2 = 1.0 - ADAM_B2 ** ADAM_STEP

    def body(*refs):
        g_refs = refs[:nl]
        w_ref, m_ref, v_ref, go_ref, d_ref, mo_ref, vo_ref = refs[nl:]

        def update(g_ref):
            g = g_ref[0].astype(F32)
            for j in range(1, N_DEV):
                g = g + g_ref[j].astype(F32)
            mn = ADAM_B1 * m_ref[...] + (1.0 - ADAM_B1) * g
            vn = ADAM_B2 * v_ref[...] + (1.0 - ADAM_B2) * (g * g)
            go_ref[...] = g
            mo_ref[...] = mn
            vo_ref[...] = vn
            d_ref[...] = -ADAM_LR * ((mn / c1) / (jnp.sqrt(vn / c2) + ADAM_EPS) + ADAM_WD * w_ref[...])

        for layer in range(nl):
            pl.when(pl.program_id(0) == layer)(functools.partial(update, g_refs[layer]))

    def gspec(layer):
        return pl.BlockSpec((N_DEV, tr, c), lambda l, i: (0, jnp.where(l == layer, i, 0), 0))

    spec = pl.BlockSpec((tr, c), lambda l, i: (l * nt + i, 0))
    return _pallas(
        body, name=name, grid=(nl, nt), args=[*recvs, w, m, v], comm=comm,
        in_specs=[gspec(layer) for layer in range(nl)] + [spec, spec, spec],
        out_specs=[spec] * 4, out_shape=[jax.ShapeDtypeStruct((nl * r, c), F32)] * 4,
        sem=("arbitrary", "arbitrary"))


BIG = ("mlp_w1", "mlp_w2", "ab_w_in", "ab_w_out", "c_w_qkv", "c_w_out")
SMALL = ("mix_norm_g", "mlp_norm_g", "a_spatial_w", "a_spatial_b", "a_vnorm_g", "a_vnorm_b", "b_conv_b", "b_norm_g",
         "b_norm_b", "c_q_norm_g", "c_k_norm_g")
WEIGHTS = ("mix_norm_g", "mlp_norm_g", "mlp_w1", "mlp_w2", "ab_w_in", "a_spatial_w", "a_spatial_b", "a_vnorm_g",
           "a_vnorm_b", "b_conv_w", "b_conv_b", "b_norm_g", "b_norm_b", "ab_w_out", "c_w_qkv", "c_q_norm_g",
           "c_k_norm_g", "c_w_out")


def _mixer_params(p, conv_full, i):
    aw = p["a_vnorm_g"].shape[1]
    row = lambda t: t[i][None, :]
    return dict(
        w=p["a_spatial_w"][i].astype(BF16), wt=jnp.swapaxes(p["a_spatial_w"][i], 1, 2).astype(BF16),
        bias=jnp.repeat(p["a_spatial_b"][i].T, aw // p["a_spatial_b"].shape[1], axis=1),
        vg=row(p["a_vnorm_g"]), vb=row(p["a_vnorm_b"]), cw=jnp.pad(conv_full[i], ((0, 1), (0, 0))),
        cb=row(p["b_conv_b"]), cg=row(p["b_norm_g"]), cbn=row(p["b_norm_b"]))


def _head_gains(p, i):
    rep = LANES // HEAD_DIM
    return jnp.stack([jnp.tile(p["c_q_norm_g"][i], rep), jnp.tile(p["c_k_norm_g"][i], rep),
                      jnp.ones((LANES,), F32)])[:, None, :]


def _unused_forward_backward(x, target, p, wg, conv_full):
    s, d = x.shape
    depth = p["mix_norm_g"].shape[0]
    tabs = rope_tables(s)
    saved = []
    for l in range(depth):
        i = l // 2
        mix_g, mlp_g = p["mix_norm_g"][l][None, :], p["mlp_norm_g"][l][None, :]
        st = dict(x_in=x)
        if l % 2 == 0:
            sp = _mixer_params(p, conv_full, i)
            z, h = norm_matmul(x, mix_g, wg["ab_w_in"], i, f"ab_in_{l}")
            ycat, gc = ab_mid_forward(z, sp, f"ab_mid_{l}")
            x = matmul_residual(x, ycat, wg["ab_w_out"], i, f"ab_out_{l}")
            st.update(z=z, h=h, y=ycat, gc=gc, sp=sp)
        else:
            gains = _head_gains(p, i)
            qkv, h = norm_matmul(x, mix_g, wg["c_w_qkv"], i, f"c_qkv_{l}")
            qkvn = qk_prep_forward(qkv, tabs, gains, f"c_prep_{l}")
            outs = [band_attention_forward(qkvn, dil, f"c_attn_{l}_d{dil}") for dil in DILATIONS]
            o, lse = attention_merge([t[0] for t in outs], [t[1] for t in outs], f"c_merge_{l}")
            x = matmul_residual(x, o, wg["c_w_out"], i, f"c_out_{l}")
            st.update(qkv=qkv, h=h, qkvn=qkvn, y=o, lse=lse, gains=gains)
        st["x_mid"] = x
        x, a, h2 = mlp_forward(x, mlp_g, wg["mlp_w1"], wg["mlp_w2"], l, f"mlp_{l}")
        st.update(a=a, h2=h2)
        saved.append(st)

    dy, loss_part = loss_and_grad(x, target, "loss")

    big = {n: [None] * p_len for n, p_len in (("mlp_w1", depth), ("mlp_w2", depth), ("ab_w_in", depth // 2 + depth % 2),
                                              ("ab_w_out", depth // 2 + depth % 2), ("c_w_qkv", depth // 2),
                                              ("c_w_out", depth // 2))}
    small = {n: [None] * p[n].shape[0] for n in SMALL}
    conv_grads = [None] * p["b_conv_b"].shape[0]
    for l in reversed(range(depth)):
        i, st = l // 2, saved[l]
        mix_g, mlp_g = p["mix_norm_g"][l][None, :], p["mlp_norm_g"][l][None, :]
        dxm, da, dg = mlp_backward(dy, st["a"], st["x_mid"], mlp_g, wg["mlp_w1"], wg["mlp_w2"], l, f"mlp_bwd_{l}")
        small["mlp_norm_g"][l] = dg[0]
        big["mlp_w1"][l] = matmul_tn(st["h2"], da, "cols", f"mlp_dw1_{l}")
        big["mlp_w2"][l] = matmul_tn(st["a"], dy, "rows", f"mlp_dw2_{l}", relu2=True)
        if l % 2 == 0:
            sp = st["sp"]
            dycat = matmul_nt(dxm, wg["ab_w_out"], i, f"ab_out_bwd_{l}")
            big["ab_w_out"][i] = matmul_tn(st["y"], dxm, "rows", f"ab_dwout_{l}")
            dz, dgc, dw, dsb, dvg, dvb, dcg, dcbn, dcb = ab_mid_backward(dycat, st["z"], st["gc"], sp, f"ab_mid_bwd_{l}")
            dz, dcw = conv_backward(dgc, st["z"], dz, sp, f"ab_conv_bwd_{l}")
            small["a_spatial_w"][i], small["a_spatial_b"][i] = dw, dsb
            small["a_vnorm_g"][i], small["a_vnorm_b"][i] = dvg[0], dvb[0]
            small["b_norm_g"][i], small["b_norm_b"][i], small["b_conv_b"][i] = dcg[0], dcbn[0], dcb[0]
            conv_grads[i] = dcw[:CONV_W]
            dy, dg = matmul_nt_norm_backward(dz, wg["ab_w_in"], i, st["x_in"], mix_g, dxm, f"ab_in_bwd_{l}")
            big["ab_w_in"][i] = matmul_tn(st["h"], dz, "cols", f"ab_dwin_{l}")
        else:
            do = matmul_nt(dxm, wg["c_w_out"], i, f"c_out_bwd_{l}")
            big["c_w_out"][i] = matmul_tn(st["y"], dxm, "rows", f"c_dwout_{l}")
            delta, dob = attention_delta(do, st["y"], f"c_delta_{l}")
            dqs, dks, dvs = [], [], []
            for dil in DILATIONS:
                dqs.append(band_attention_dq(st["qkvn"], dob, st["lse"], delta, dil, f"c_attn_dq_{l}_d{dil}"))
                dk, dv = band_attention_dkv(st["qkvn"], dob, st["lse"], delta, dil, f"c_attn_dkv_{l}_d{dil}")
                dks.append(dk)
                dvs.append(dv)
            dqkv, dgn = qk_prep_backward(dqs, dks, dvs, st["qkv"], tabs, st["gains"], f"c_prep_bwd_{l}")
            small["c_q_norm_g"][i] = dgn[0, 0, :HEAD_DIM] + dgn[0, 0, HEAD_DIM:]
            small["c_k_norm_g"][i] = dgn[1, 0, :HEAD_DIM] + dgn[1, 0, HEAD_DIM:]
            dy, dg = matmul_nt_norm_backward(dqkv, wg["c_w_qkv"], i, st["x_in"], mix_g, dxm, f"c_qkv_bwd_{l}")
            big["c_w_qkv"][i] = matmul_tn(st["h"], dqkv, "cols", f"c_dwqkv_{l}")
        small["mix_norm_g"][l] = dg[0]
    small = {n: jnp.stack(v) for n, v in small.items()}
    return loss_part, dy, big, small, jnp.stack(conv_grads)


PACKED = tuple(n for n in SMALL if n != "a_spatial_w")


def _pack(d):
    flat = jnp.concatenate([d[n].reshape(-1) for n in PACKED])
    rows = -(-flat.shape[0] // (8 * LANES)) * 8
    return jnp.pad(flat, (0, rows * LANES - flat.shape[0])).reshape(rows, LANES)


def _unpack(packed, like):
    flat, out, pos = packed.reshape(-1), {}, 0
    for n in PACKED:
        size = math.prod(like[n].shape)
        out[n] = flat[pos:pos + size].reshape(like[n].shape)
        pos += size
    return out


def _unused_kernel(x, mix_norm_g, mlp_norm_g, mlp_w1, mlp_w2, ab_w_in, a_spatial_w, a_spatial_b, a_vnorm_g, a_vnorm_b, b_conv_w, b_conv_b, b_norm_g, b_norm_b, ab_w_out, c_w_qkv, c_q_norm_g, c_k_norm_g, c_w_out, loss_target, m_mix_norm_g, m_mlp_norm_g, m_mlp_w1, m_mlp_w2, m_ab_w_in, m_a_spatial_w, m_a_spatial_b, m_a_vnorm_g, m_a_vnorm_b, m_b_conv_w, m_b_conv_b, m_b_norm_g, m_b_norm_b, m_ab_w_out, m_c_w_qkv, m_c_q_norm_g, m_c_k_norm_g, m_c_w_out, v_mix_norm_g, v_mlp_norm_g, v_mlp_w1, v_mlp_w2, v_ab_w_in, v_a_spatial_w, v_a_spatial_b, v_a_vnorm_g, v_a_vnorm_b, v_b_conv_w, v_b_conv_b, v_b_norm_g, v_b_norm_b, v_ab_w_out, v_c_w_qkv, v_c_q_norm_g, v_c_k_norm_g, v_c_w_out):
    args = dict(locals())
    w = {n: args[n] for n in WEIGHTS}
    m = {n: args["m_" + n] for n in WEIGHTS}
    v = {n: args["v_" + n] for n in WEIGHTS}

    wg = {n: all_gather(w[n].astype(BF16), "gather_" + n) for n in BIG}
    for n in ("ab_w_out", "c_w_out"):
        t = wg[n]
        wg[n] = t.reshape(t.shape[0], t.shape[1] * t.shape[2], t.shape[3])
    conv = all_gather(w["b_conv_w"], "gather_b_conv_w")
    conv_full = jnp.swapaxes(conv, 1, 2).reshape(conv.shape[0], conv.shape[2], N_DEV * conv.shape[3])

    loss_part, dx, big, small, conv_grad = forward_backward(x[0], loss_target[0], w, wg, conv_full)
    loss = lax.psum(jnp.sum(loss_part), ("x", "y", "c"))

    grads, deltas, new_m, new_v = {}, {}, {}, {}

    def update(n, recv):
        shape = w[n].shape
        flat = lambda t: t.reshape(-1, shape[-1])
        outs = adamw_update(recv.reshape((N_DEV, -1, shape[-1])), flat(w[n]), flat(m[n]), flat(v[n]), "adamw_" + n)
        grads[n], deltas[n], new_m[n], new_v[n] = (t.reshape(shape) for t in outs)

    for n in BIG:
        update(n, exchange(big[n], "exchange_" + n))
    nl, kw, cw = conv_grad.shape
    conv_parts = jnp.transpose(conv_grad.reshape(nl, kw, N_DEV, cw // N_DEV), (2, 0, 1, 3))
    update("b_conv_w", exchange([conv_parts], "exchange_b_conv_w"))

    packed = all_gather(_pack(small)[None], "gather_small_grads")[0]
    outs = adamw_update(packed, _pack(w), _pack(m), _pack(v), "adamw_small")
    for dst, t in zip((grads, deltas, new_m, new_v), outs):
        dst.update(_unpack(t, w))

    return (loss, dx[None], *[grads[n] for n in WEIGHTS], *[deltas[n] for n in WEIGHTS],
            *[new_m[n] for n in WEIGHTS], *[new_v[n] for n in WEIGHTS])


class Traffic:
    def __init__(self, shards, full=()):
        self.shards, self.w, self.queue, self.parts = shards, dict(full), [], {}

    def run(self, fn, *args, gather=(), send=False, **kw):
        operands, flags, dest = [], [], []
        if self.shards is None:
            if send:
                self.parts.update(self.queue)
                self.queue = []
        else:
            for k in gather:
                if k not in self.w:
                    operands.append(self.shards[k])
                    flags.append(("gather", SHARD_AXIS.get(k[0])))
                    dest.append((self.w, k))
            if send:
                for k, t in self.queue:
                    operands.append(t)
                    flags.append(("scatter", SHARD_AXIS.get(k[0])))
                    dest.append((self.parts, k))
                self.queue = []
        outs, couts = fn(*args, comm=PeerCopies(operands, flags) if operands else None, **kw)
        for (table, k), t in zip(dest, couts):
            table[k] = t
        return outs

    def flush(self, name, extra=()):
        self.queue += list(extra)
        self.run(lambda comm: ([], run_copies(comm, name) if comm is not None else []), send=True)


SHARD_AXIS = {"mlp_w1": 1, "mlp_w2": 0, "ab_w_in": 1, "ab_w_out": 0, "c_w_qkv": 1, "c_w_out": 0}


def forward_backward(x, target, p, tr, conv_full):
    s, d = x.shape
    depth = p["mix_norm_g"].shape[0]
    tabs = rope_tables(s)
    saved = []
    for l in range(depth):
        i = l // 2
        mix_g, mlp_g = p["mix_norm_g"][l][None, :], p["mlp_norm_g"][l][None, :]
        st = dict(x_in=x)
        nxt = () if l + 1 == depth else ((("c_w_qkv", i), ("c_w_out", i)) if l % 2 == 0 else
                                        (("ab_w_in", i + 1), ("ab_w_out", i + 1)))
        if l % 2 == 0:
            sp = _mixer_params(p, conv_full, i)
            z, h = tr.run(norm_matmul, x, mix_g, tr.w["ab_w_in", i], f"ab_in_{l}", gather=[("mlp_w1", l)])
            ycat, gc = tr.run(ab_mid_forward, z, sp, f"ab_mid_{l}", gather=[("mlp_w2", l)])
            x = matmul_residual(x, ycat, tr.w["ab_w_out", i], f"ab_out_{l}")
            st.update(z=z, h=h, y=ycat, gc=gc, sp=sp)
        else:
            gains = _head_gains(p, i)
            qkv, h = norm_matmul(x, mix_g, tr.w["c_w_qkv", i], f"c_qkv_{l}")[0]
            qk = qk_prep_forward(qkv, tabs, gains, f"c_prep_{l}")[0]
            ahead = [(n, j) for j in (l, l + 1) if j < depth for n in ("mlp_w1", "mlp_w2")]
            o, lse = tr.run(attention_forward, qk, qkv, f"c_attn_{l}", gather=ahead)
            x = matmul_residual(x, o, tr.w["c_w_out", i], f"c_out_{l}")
            st.update(qkv=qkv, h=h, qk=qk, y=o, lse=lse, gains=gains)
        st["x_mid"] = x
        x, a, h2 = tr.run(mlp_forward, x, mlp_g, tr.w["mlp_w1", l], tr.w["mlp_w2", l], f"mlp_{l}", gather=nxt)
        st.update(a=a, h2=h2)
        saved.append(st)

    dy, loss_part = loss_and_grad(x, target, "loss")

    def tn(*a, comm, **kw):
        out, couts = matmul_tn(*a, comm=comm, **kw)
        return [out], couts

    small = {n: [None] * p[n].shape[0] for n in SMALL}
    conv_grads = [None] * p["b_conv_b"].shape[0]
    for l in reversed(range(depth)):
        i, st = l // 2, saved[l]
        mix_g, mlp_g = p["mix_norm_g"][l][None, :], p["mlp_norm_g"][l][None, :]
        w1, w2 = tr.w["mlp_w1", l], tr.w["mlp_w2", l]
        dxm, da, dg, dyb = tr.run(mlp_backward, dy, st["a"], st["x_mid"], mlp_g, w1, w2, f"mlp_bwd_{l}", send=True)
        small["mlp_norm_g"][l] = dg[0]
        tr.queue.append((("mlp_w1", l), matmul_tn(st["h2"], da, f"mlp_dw1_{l}", n_split=2)[0]))
        tr.queue.append((("mlp_w2", l), matmul_tn(st["a"], dyb, f"mlp_dw2_{l}", m_split=2, relu2=True)[0]))
        if l % 2 == 0:
            sp = st["sp"]
            wout = tr.w["ab_w_out", i]
            dycat = matmul_nt(dxm, wout, f"ab_out_bwd_{l}")
            tr.queue.append((("ab_w_out", i), matmul_tn(st["y"], dxm, f"ab_dwout_{l}")[0]))
            dz, dgc, dw, dsb, dvg, dvb, dcg, dcbn, dcb = ab_mid_backward(dycat, st["z"], st["gc"], sp, f"ab_mid_bwd_{l}")
            dz, dcw = tr.run(conv_backward, dgc, st["z"], dz, sp, f"ab_conv_bwd_{l}", send=True)
            small["a_spatial_w"][i], small["a_spatial_b"][i] = dw, dsb
            small["a_vnorm_g"][i], small["a_vnorm_b"][i] = dvg[0], dvb[0]
            small["b_norm_g"][i], small["b_norm_b"][i], small["b_conv_b"][i] = dcg[0], dcbn[0], dcb[0]
            conv_grads[i] = dcw[:CONV_W]
            dy, dg = matmul_nt_norm_backward(dz, tr.w["ab_w_in", i], st["x_in"], mix_g, dxm, f"ab_in_bwd_{l}")
            small["mix_norm_g"][l] = dg[0]
            last = []
            if l == 0 and tr.shards is not None:
                tr.shards["small_grads", 0] = _pack({n: jnp.stack(small[n]) for n in PACKED})
                tr.shards["spatial_grads", 0] = jnp.stack(small["a_spatial_w"]).astype(BF16)
                last = [("small_grads", 0), ("spatial_grads", 0)]
            dwin = tr.run(tn, st["h"], dz, f"ab_dwin_{l}", send=True, gather=last)[0]
            tr.queue.append((("ab_w_in", i), dwin))
        else:
            wout = tr.w["c_w_out", i]
            do = matmul_nt(dxm, wout, f"c_out_bwd_{l}")
            tr.queue.append((("c_w_out", i), matmul_tn(st["y"], dxm, f"c_dwout_{l}")[0]))
            stats = attention_delta(do, st["y"], st["lse"], f"c_delta_{l}")
            dk, dv, dq, dqp, dqn = tr.run(attention_backward, st["qk"], st["qkv"], do, stats, f"c_attn_bwd_{l}", send=True)
            dqkv, dgn = qk_prep_backward(dq, dqp, dqn, dk, dv, st["qkv"], tabs, st["gains"], f"c_prep_bwd_{l}")
            small["c_q_norm_g"][i] = dgn[0, 0, :HEAD_DIM] + dgn[0, 0, HEAD_DIM:]
            small["c_k_norm_g"][i] = dgn[1, 0, :HEAD_DIM] + dgn[1, 0, HEAD_DIM:]
            dy, dg = matmul_nt_norm_backward(dqkv, tr.w["c_w_qkv", i], st["x_in"], mix_g, dxm, f"c_qkv_bwd_{l}")
            small["mix_norm_g"][l] = dg[0]
            tr.queue.append((("c_w_qkv", i), matmul_tn(st["h"], dqkv, f"c_dwqkv_{l}", n_split=2)[0]))
    small = {n: jnp.stack(v) for n, v in small.items()}
    return loss_part, dy, small, jnp.stack(conv_grads)


def kernel(x, mix_norm_g, mlp_norm_g, mlp_w1, mlp_w2, ab_w_in, a_spatial_w, a_spatial_b, a_vnorm_g, a_vnorm_b, b_conv_w, b_conv_b, b_norm_g, b_norm_b, ab_w_out, c_w_qkv, c_q_norm_g, c_k_norm_g, c_w_out, loss_target, m_mix_norm_g, m_mlp_norm_g, m_mlp_w1, m_mlp_w2, m_ab_w_in, m_a_spatial_w, m_a_spatial_b, m_a_vnorm_g, m_a_vnorm_b, m_b_conv_w, m_b_conv_b, m_b_norm_g, m_b_norm_b, m_ab_w_out, m_c_w_qkv, m_c_q_norm_g, m_c_k_norm_g, m_c_w_out, v_mix_norm_g, v_mlp_norm_g, v_mlp_w1, v_mlp_w2, v_ab_w_in, v_a_spatial_w, v_a_spatial_b, v_a_vnorm_g, v_a_vnorm_b, v_b_conv_w, v_b_conv_b, v_b_norm_g, v_b_norm_b, v_ab_w_out, v_c_w_qkv, v_c_q_norm_g, v_c_k_norm_g, v_c_w_out):
    args = dict(locals())
    w = {n: args[n] for n in WEIGHTS}
    m = {n: args["m_" + n] for n in WEIGHTS}
    v = {n: args["v_" + n] for n in WEIGHTS}

    shards = {(n, l): w[n][l].astype(BF16) for n in BIG for l in range(w[n].shape[0])}
    shards["b_conv_w", 0] = w["b_conv_w"]
    tr = Traffic(shards)
    first = [("ab_w_in", 0), ("ab_w_out", 0), ("b_conv_w", 0)]
    tr.run(lambda comm: ([], run_copies(comm, "gather_first")), gather=first)
    conv = tr.w["b_conv_w", 0]
    conv_full = jnp.transpose(conv, (1, 2, 0, 3)).reshape(conv.shape[1], conv.shape[2], -1)

    loss_part, dx, small, conv_grad = forward_backward(x[0], loss_target[0], w, tr, conv_full)
    loss = lax.psum(jnp.sum(loss_part), ("x", "y", "c"))
    nl, kw, cw = conv_grad.shape
    conv_parts = jnp.transpose(conv_grad.reshape(nl, kw, N_DEV, cw // N_DEV), (2, 0, 1, 3))
    tr.queue.append((("b_conv_w", 0), conv_parts))

    grads, deltas, new_m, new_v = {}, {}, {}, {}

    def update(n, recvs):
        shape = w[n].shape
        flat = lambda t: t.reshape(-1, shape[-1])
        recvs = [t.reshape(N_DEV, -1, shape[-1]) for t in recvs]
        outs = tr.run(adamw_update, recvs, flat(w[n]), flat(m[n]), flat(v[n]), "adamw_" + n, send=True)
        grads[n], deltas[n], new_m[n], new_v[n] = (t.reshape(shape) for t in outs)

    for n in BIG:
        update(n, [tr.parts[n, l] for l in range(w[n].shape[0])])
    update("b_conv_w", [tr.parts["b_conv_w", 0]])
    update("a_spatial_w", [tr.w["spatial_grads", 0]])
    outs = adamw_update([tr.w["small_grads", 0]], _pack(w), _pack(m), _pack(v), "adamw_small")[0]
    for dst, t in zip((grads, deltas, new_m, new_v), outs):
        dst.update(_unpack(t, w))

    return (loss, dx[None], *[grads[n] for n in WEIGHTS], *[deltas[n] for n in WEIGHTS],
            *[new_m[n] for n in WEIGHTS], *[new_v[n] for n in WEIGHTS])
```

```python
import functools
import math

import jax
import jax.numpy as jnp
from jax import lax
from jax.experimental import pallas as pl
from jax.experimental.pallas import tpu as pltpu

F32, BF16 = jnp.float32, jnp.bfloat16
N_DEV = 8
EPS = 1e-6
NEG = -1e30
LANES = 128
HEAD_DIM = 64
CHUNK = 128
CONV_W = 31
CONV_HALO = 16
BAND = 64
DILATIONS = (1, 4, 16)
ROT_DIM = 16
ROPE_THETA = 500000.0
VMEM_LIMIT = 56 * 1024 * 1024
MLP_CHUNK = 1024
MLP_BWD_CHUNK = 512
MLP_BWD_ROWS = 1024
ADAM_LR, ADAM_B1, ADAM_B2, ADAM_EPS, ADAM_WD, ADAM_STEP = 0.001, 0.9, 0.999, 1e-08, 0.01, 10
MESH = pl.DeviceIdType.MESH


def _params(*sem):
    return pltpu.CompilerParams(dimension_semantics=sem, vmem_limit_bytes=VMEM_LIMIT)


def _dot(a, b):
    return jnp.dot(a, b, preferred_element_type=F32)


def _dot_nt(a, b):
    return lax.dot_general(a, b, (((1,), (1,)), ((), ())), preferred_element_type=F32)


def _dot_tn(a, b):
    return lax.dot_general(a, b, (((0,), (0,)), ((), ())), preferred_element_type=F32)


def _rms_r(x):
    return lax.rsqrt(jnp.mean(x * x, axis=-1, keepdims=True) + EPS)


def _sigmoid(x):
    return 1.0 / (1.0 + jnp.exp(-x))


_GK = math.sqrt(2.0 / math.pi)


def _gelu(x):
    return 0.5 * x * (1.0 + jnp.tanh(_GK * (x + 0.044715 * x * x * x)))


def _gelu_grad(x):
    t = jnp.tanh(_GK * (x + 0.044715 * x * x * x))
    return 0.5 * (1.0 + t) + 0.5 * x * (1.0 - t * t) * (_GK * (1.0 + 3.0 * 0.044715 * x * x))


def _seg_sum(x, bd):
    hi = x.astype(BF16)
    lo = (x - hi.astype(F32)).astype(BF16)
    return _dot(hi, bd) + _dot(lo, bd)


def _block_diag(n):
    i = lax.broadcasted_iota(jnp.int32, (n, n), 0) // HEAD_DIM
    j = lax.broadcasted_iota(jnp.int32, (n, n), 1) // HEAD_DIM
    return jnp.where(i == j, 1.0, 0.0).astype(BF16)


def _tile(s, cap):
    t = min(s, cap)
    assert s % t == 0
    return t


def _my_index():
    return 4 * lax.axis_index("x") + 2 * lax.axis_index("y") + lax.axis_index("c")


def _device(i):
    return (i // 4, (i // 2) % 2, i % 2)


_HBM = pl.BlockSpec(memory_space=pl.ANY)


class PeerCopies:
    def __init__(self, operands, modes):
        self.inputs, self.modes = list(operands), list(modes)
        self.out_shape = []
        for t, (kind, axis) in zip(self.inputs, self.modes):
            shape = list(t.shape)
            if kind == "gather":
                shape = [N_DEV] + shape if axis is None else shape[:axis] + [N_DEV * shape[axis]] + shape[axis + 1:]
            elif axis is not None:
                shape = [N_DEV] + shape[:axis] + [shape[axis] // N_DEV] + shape[axis + 1:]
            self.out_shape.append(jax.ShapeDtypeStruct(tuple(shape), t.dtype))
        n = len(self.inputs)
        self.scratch = [pltpu.SemaphoreType.DMA((n, N_DEV - 1)), pltpu.SemaphoreType.DMA((n, N_DEV - 1)),
                        pltpu.SemaphoreType.DMA((n,))]

    @staticmethod
    def _block(ref, axis, size, j):
        if axis is None:
            return ref.at[j]
        return ref.at[tuple([slice(None)] * axis + [pl.ds(j * size, size)])]

    def _copies(self, in_refs, out_refs, sems, arrivals):
        send_sems, recv_sems, local_sems = sems
        me = _my_index()
        local, sends, recvs = [], [], []
        for t, (src, dst) in enumerate(zip(in_refs, out_refs)):
            kind, axis = self.modes[t]
            if kind == "gather":
                size = None if axis is None else src.shape[axis]
                source = lambda j, src=src: src
                place = lambda j, dst=dst, axis=axis, size=size: self._block(dst, axis, size, j)
            else:
                size = None if axis is None else src.shape[axis] // N_DEV
                source = lambda j, src=src, axis=axis, size=size: self._block(src, axis, size, j)
                place = lambda j, dst=dst: dst.at[j]
            local.append(pltpu.make_async_copy(source(me), place(me), local_sems.at[t]))
            for k in range(N_DEV - 1):
                to, frm = (me + k + 1) % N_DEV, (me + N_DEV - k - 1) % N_DEV
                sends.append(pltpu.make_async_remote_copy(
                    src_ref=source(to), dst_ref=place(me), send_sem=send_sems.at[t, k], recv_sem=recv_sems.at[t, k],
                    device_id=_device(to), device_id_type=MESH))
                if arrivals:
                    recvs.append(pltpu.make_async_remote_copy(
                        src_ref=source(me), dst_ref=place(frm), send_sem=send_sems.at[t, k], recv_sem=recv_sems.at[t, k],
                        device_id=_device(frm), device_id_type=MESH))
        return local, sends, recvs

    def start(self, in_refs, out_refs, sems):
        local, sends, _ = self._copies(in_refs, out_refs, sems, False)
        for cp in local + sends:
            cp.start()

    def finish(self, in_refs, out_refs, sems):
        local, sends, recvs = self._copies(in_refs, out_refs, sems, True)
        for cp in recvs:
            cp.wait_recv()
        for cp in sends:
            cp.wait_send()
        for cp in local:
            cp.wait()


def _pallas(body, *, name, args, in_specs, out_specs, out_shape, grid=(), scratch=(), sem=(), comm=None, aliases=None):
    n_in, n_out, n_scr = len(args), len(out_shape), len(scratch)
    if comm is None:
        outs = pl.pallas_call(
            body, name=name, grid=grid, in_specs=in_specs, out_specs=out_specs, out_shape=out_shape,
            scratch_shapes=list(scratch), input_output_aliases=aliases or {}, compiler_params=_params(*sem))(*args)
        return list(outs), []
    ci, co = len(comm.inputs), len(comm.out_shape)

    def hosted(*refs):
        ins, cins = refs[:n_in], refs[n_in:n_in + ci]
        outs, couts = refs[n_in + ci:n_in + ci + n_out], refs[n_in + ci + n_out:n_in + ci + n_out + co]
        rest = refs[n_in + ci + n_out + co:]
        scr, sems = rest[:n_scr], rest[n_scr:]
        if not grid:
            comm.start(cins, couts, sems)
            comm.finish(cins, couts, sems)
            return
        first = last = None
        for axis, size in enumerate(grid):
            f, l = pl.program_id(axis) == 0, pl.program_id(axis) == size - 1
            first, last = (f, l) if first is None else (first & f, last & l)
        pl.when(first)(lambda: comm.start(cins, couts, sems))
        body(*ins, *outs, *scr)
        pl.when(last)(lambda: comm.finish(cins, couts, sems))

    outs = pl.pallas_call(
        hosted, name=name, grid=grid, in_specs=list(in_specs) + [_HBM] * ci, out_specs=list(out_specs) + [_HBM] * co,
        out_shape=list(out_shape) + comm.out_shape, scratch_shapes=list(scratch) + comm.scratch,
        input_output_aliases=aliases or {}, compiler_params=_params(*["arbitrary"] * len(grid)))(*args, *comm.inputs)
    return list(outs[:n_out]), list(outs[n_out:])


def run_copies(comm, name):
    return _pallas(None, name=name, args=[], in_specs=[], out_specs=[], out_shape=[], comm=comm)[1]


def norm_matmul(x, g, wg, name, comm=None):
    s, d = x.shape
    n = wg.shape[-1]
    ns = 1024 if n % 1024 == 0 else n // 4
    tm = _tile(s, 1024)

    def body(x_ref, g_ref, w_ref, z_ref, h_ref):
        @pl.when(pl.program_id(1) == 0)
        def _():
            xv = x_ref[...]
            h_ref[...] = (xv * _rms_r(xv) * g_ref[...]).astype(BF16)
        z_ref[...] = _dot(h_ref[...], w_ref[...])

    return _pallas(
        body, name=name, grid=(s // tm, n // ns), args=[x, g, wg], comm=comm,
        in_specs=[pl.BlockSpec((tm, d), lambda i, j: (i, 0)),
                  pl.BlockSpec((1, d), lambda i, j: (0, 0)),
                  pl.BlockSpec((d, ns), lambda i, j: (0, j))],
        out_specs=[pl.BlockSpec((tm, ns), lambda i, j: (i, j)),
                   pl.BlockSpec((tm, d), lambda i, j: (i, 0))],
        out_shape=[jax.ShapeDtypeStruct((s, n), F32), jax.ShapeDtypeStruct((s, d), BF16)],
        sem=("parallel", "arbitrary"))


def mlp_forward(x, g, w1g, w2g, name, comm=None):
    s, d = x.shape
    f = w1g.shape[-1]
    fs = MLP_CHUNK
    tm = _tile(s, 1024)

    def body(x_ref, g_ref, w1_ref, w2_ref, xo_ref, a_ref, h_ref):
        @pl.when(pl.program_id(1) == 0)
        def _():
            xv = x_ref[...]
            h_ref[...] = (xv * _rms_r(xv) * g_ref[...]).astype(BF16)
            xo_ref[...] = xv
        a = _dot(h_ref[...], w1_ref[...])
        a_ref[...] = a.astype(BF16)
        r = jnp.maximum(a, 0.0)
        xo_ref[...] += _dot((r * r).astype(BF16), w2_ref[...])

    return _pallas(
        body, name=name, grid=(s // tm, f // fs), args=[x, g, w1g, w2g], comm=comm,
        in_specs=[pl.BlockSpec((tm, d), lambda i, j: (i, 0)),
                  pl.BlockSpec((1, d), lambda i, j: (0, 0)),
                  pl.BlockSpec((d, fs), lambda i, j: (0, j)),
                  pl.BlockSpec((fs, d), lambda i, j: (j, 0))],
        out_specs=[pl.BlockSpec((tm, d), lambda i, j: (i, 0)),
                   pl.BlockSpec((tm, fs), lambda i, j: (i, j)),
                   pl.BlockSpec((tm, d), lambda i, j: (i, 0))],
        out_shape=[jax.ShapeDtypeStruct((s, d), F32), jax.ShapeDtypeStruct((s, f), BF16),
                   jax.ShapeDtypeStruct((s, d), BF16)],
        sem=("parallel", "arbitrary"))


def _norm_backward(dh, xv, g, dres):
    r = _rms_r(xv)
    xh = xv * r
    t = dh * g
    dx = dres + r * (t - xh * jnp.mean(t * xh, axis=-1, keepdims=True))
    return dx, jnp.sum(dh * xh, axis=0, keepdims=True)


def mlp_backward(dy, a, x, g, w1g, w2g, name, comm=None):
    s, d = x.shape
    f = w1g.shape[-1]
    fs = MLP_BWD_CHUNK
    tm = _tile(s, MLP_BWD_ROWS)

    def body(dy_ref, a_ref, x_ref, g_ref, w1_ref, w2_ref, dx_ref, da_ref, dg_ref, dyb_ref, dh_ref):
        i, j = pl.program_id(0), pl.program_id(1)

        @pl.when(j == 0)
        def _():
            dyb_ref[...] = dy_ref[...].astype(BF16)
            dh_ref[...] = jnp.zeros_like(dh_ref)

        dr = _dot_nt(dyb_ref[...], w2_ref[...])
        da = (dr * (2.0 * jnp.maximum(a_ref[...].astype(F32), 0.0))).astype(BF16)
        da_ref[...] = da
        dh_ref[...] += _dot_nt(da, w1_ref[...])

        @pl.when(j == f // fs - 1)
        def _():
            dx, dgp = _norm_backward(dh_ref[...], x_ref[...], g_ref[...], dy_ref[...])
            dx_ref[...] = dx

            @pl.when(i == 0)
            def _():
                dg_ref[...] = dgp

            @pl.when(i > 0)
            def _():
                dg_ref[...] += dgp

    return _pallas(
        body, name=name, grid=(s // tm, f // fs), args=[dy, a, x, g, w1g, w2g], comm=comm,
        in_specs=[pl.BlockSpec((tm, d), lambda i, j: (i, 0)),
                  pl.BlockSpec((tm, fs), lambda i, j: (i, j)),
                  pl.BlockSpec((tm, d), lambda i, j: (i, 0)),
                  pl.BlockSpec((1, d), lambda i, j: (0, 0)),
                  pl.BlockSpec((d, fs), lambda i, j: (0, j)),
                  pl.BlockSpec((fs, d), lambda i, j: (j, 0))],
        out_specs=[pl.BlockSpec((tm, d), lambda i, j: (i, 0)),
                   pl.BlockSpec((tm, fs), lambda i, j: (i, j)),
                   pl.BlockSpec((1, d), lambda i, j: (0, 0)),
                   pl.BlockSpec((tm, d), lambda i, j: (i, 0))],
        out_shape=[jax.ShapeDtypeStruct((s, d), F32), jax.ShapeDtypeStruct((s, f), BF16),
                   jax.ShapeDtypeStruct((1, d), F32), jax.ShapeDtypeStruct((s, d), BF16)],
        scratch=[pltpu.VMEM((tm, d), F32)],
        sem=("arbitrary", "arbitrary"))


def matmul_tn(a, b, name, m_split=1, n_split=1, relu2=False, comm=None):
    s, m = a.shape
    n = b.shape[1]
    ts = _tile(s, 1024 if relu2 else 2048)
    bm, bn = m // m_split, n // n_split
    a_map = lambda j, k: (k, j // n_split)
    b_map = lambda j, k: (k, j % n_split)

    def body(a_ref, b_ref, o_ref, acc_ref):
        k = pl.program_id(1)
        av = a_ref[...]
        if relu2:
            af = jnp.maximum(av.astype(F32), 0.0)
            av = af * af
        p = _dot_tn(av.astype(BF16), b_ref[...].astype(BF16))

        @pl.when(k == 0)
        def _():
            acc_ref[...] = p

        @pl.when(k > 0)
        def _():
            acc_ref[...] += p

        @pl.when(k == s // ts - 1)
        def _():
            o_ref[...] = acc_ref[...].astype(BF16)

    outs, couts = _pallas(
        body, name=name, grid=(m_split * n_split, s // ts), args=[a, b], comm=comm,
        in_specs=[pl.BlockSpec((ts, bm), a_map), pl.BlockSpec((ts, bn), b_map)],
        out_specs=[pl.BlockSpec((bm, bn), lambda j, k: (j // n_split, j % n_split))],
        out_shape=[jax.ShapeDtypeStruct((m, n), BF16)],
        scratch=[pltpu.VMEM((bm, bn), F32)],
        sem=("parallel", "arbitrary"))
    return outs[0], couts


def matmul_residual(x, y, w, name):
    s, n = x.shape
    k = y.shape[1]
    tm = _tile(s, 1024)

    def body(x_ref, y_ref, w_ref, o_ref):
        o_ref[...] = x_ref[...] + _dot(y_ref[...], w_ref[...])

    return pl.pallas_call(
        body, name=name, grid=(s // tm,),
        in_specs=[pl.BlockSpec((tm, n), lambda i: (i, 0)),
                  pl.BlockSpec((tm, k), lambda i: (i, 0)),
                  pl.BlockSpec((k, n), lambda i: (0, 0))],
        out_specs=pl.BlockSpec((tm, n), lambda i: (i, 0)),
        out_shape=jax.ShapeDtypeStruct((s, n), F32),
        compiler_params=_params("parallel"),
    )(x, y, w)


def matmul_nt(dy, wg, name):
    s, n = dy.shape
    k = wg.shape[0]
    tm = _tile(s, 1024)

    def body(dy_ref, w_ref, o_ref):
        o_ref[...] = _dot_nt(dy_ref[...].astype(BF16), w_ref[...])

    return pl.pallas_call(
        body, name=name, grid=(s // tm,),
        in_specs=[pl.BlockSpec((tm, n), lambda i: (i, 0)),
                  pl.BlockSpec((k, n), lambda i: (0, 0))],
        out_specs=pl.BlockSpec((tm, k), lambda i: (i, 0)),
        out_shape=jax.ShapeDtypeStruct((s, k), F32),
        compiler_params=_params("parallel"),
    )(dy, wg)


def matmul_nt_norm_backward(dz, wg, x, g, dres, name):
    s, d = x.shape
    n = wg.shape[-1]
    tm = _tile(s, 512)

    def body(dz_ref, w_ref, x_ref, g_ref, dres_ref, dx_ref, dg_ref):
        dx, dgp = _norm_backward(_dot_nt(dz_ref[...], w_ref[...]), x_ref[...], g_ref[...], dres_ref[...])
        dx_ref[...] = dx
        _accumulate(dg_ref, dgp, pl.program_id(0) == 0)

    return pl.pallas_call(
        body, name=name, grid=(s // tm,),
        in_specs=[pl.BlockSpec((tm, n), lambda i: (i, 0)),
                  pl.BlockSpec((d, n), lambda i: (0, 0)),
                  pl.BlockSpec((tm, d), lambda i: (i, 0)),
                  pl.BlockSpec((1, d), lambda i: (0, 0)),
                  pl.BlockSpec((tm, d), lambda i: (i, 0))],
        out_specs=[pl.BlockSpec((tm, d), lambda i: (i, 0)),
                   pl.BlockSpec((1, d), lambda i: (0, 0))],
        out_shape=[jax.ShapeDtypeStruct((s, d), F32), jax.ShapeDtypeStruct((1, d), F32)],
        compiler_params=_params("arbitrary"),
    )(dz, wg, x, g, dres)


def loss_and_grad(y, target, name):
    s, d = y.shape
    tm = _tile(s, 1024)

    def body(y_ref, t_ref, dy_ref, l_ref):
        e = y_ref[...] - t_ref[...]
        dy_ref[...] = e / d
        part = jnp.sum(e * e, axis=0, keepdims=True) * (0.5 / d)

        @pl.when(pl.program_id(0) == 0)
        def _():
            l_ref[...] = part

        @pl.when(pl.program_id(0) > 0)
        def _():
            l_ref[...] += part

    return pl.pallas_call(
        body, name=name, grid=(s // tm,),
        in_specs=[pl.BlockSpec((tm, d), lambda i: (i, 0)), pl.BlockSpec((tm, d), lambda i: (i, 0))],
        out_specs=[pl.BlockSpec((tm, d), lambda i: (i, 0)), pl.BlockSpec((1, d), lambda i: (0, 0))],
        out_shape=[jax.ShapeDtypeStruct((s, d), F32), jax.ShapeDtypeStruct((1, d), F32)],
        compiler_params=_params("arbitrary"),
    )(y, target)


def _layernorm(x, g, b):
    mu = jnp.mean(x, axis=-1, keepdims=True)
    xc = x - mu
    rstd = lax.rsqrt(jnp.mean(xc * xc, axis=-1, keepdims=True) + EPS)
    xn = xc * rstd
    return xn * g + b, xn, rstd


def _layernorm_backward(dy, xn, rstd, g):
    dxn = dy * g
    return rstd * (dxn - jnp.mean(dxn, axis=-1, keepdims=True) - xn * jnp.mean(dxn * xn, axis=-1, keepdims=True))


def _group_halves(x_ref, jp, nch):
    blk = jnp.concatenate([x_ref[c * CHUNK:(c + 1) * CHUNK, jp * LANES:(jp + 1) * LANES] for c in range(nch)], axis=1)
    low = (lax.broadcasted_iota(jnp.int32, blk.shape, 1) % LANES) < HEAD_DIM
    return jnp.where(low, blk, 0.0).astype(BF16), jnp.where(low, 0.0, blk).astype(BF16)


def _spatial_apply(src_ref, w_ref, dst_ref, nch, bias_ref=None):
    for jp in range(4):
        lo, hi = _group_halves(src_ref, jp, nch)
        r = _dot(w_ref[2 * jp], lo) + _dot(w_ref[2 * jp + 1], hi)
        for c in range(nch):
            v = r[:, c * LANES:(c + 1) * LANES]
            if bias_ref is not None:
                v = v + bias_ref[:, jp * LANES:(jp + 1) * LANES]
            dst_ref[c * CHUNK:(c + 1) * CHUNK, jp * LANES:(jp + 1) * LANES] = v


def _glu(zb):
    w = zb.shape[1] // 2
    return zb[:, :w] * _sigmoid(zb[:, w:])


def _fill_padded(pad_ref, prev, cur, nxt, i, nt, tm):
    pad_ref[0:CONV_HALO, :] = jnp.where(i > 0, prev, 0.0)
    pad_ref[CONV_HALO:CONV_HALO + tm, :] = cur
    pad_ref[CONV_HALO + tm:2 * CONV_HALO + tm, :] = jnp.where(i < nt - 1, nxt, 0.0)


def _halo_specs(tm, s, width, col):
    hb, nhb = tm // CONV_HALO, s // CONV_HALO
    return [pl.BlockSpec((tm, width), lambda i: (i, col)),
            pl.BlockSpec((CONV_HALO, width), lambda i: (jnp.maximum(i * hb - 1, 0), col)),
            pl.BlockSpec((CONV_HALO, width), lambda i: (jnp.minimum((i + 1) * hb, nhb - 1), col))]


def _const_spec(shape):
    nd = len(shape)
    return pl.BlockSpec(shape, lambda i: (0,) * nd)


SUBLANES = 8


def _shift_scratch(tm, width):
    return pltpu.VMEM((SUBLANES - 1, tm + 2 * CONV_HALO - SUBLANES, width), F32)


def _fill_shifts(sh_ref, pad_ref):
    rows = sh_ref.shape[1]
    for sft in range(1, SUBLANES):
        sh_ref[sft - 1] = pad_ref[pl.ds(sft, rows), :]


def _tap(pad_ref, sh_ref, offset, cols):
    sft = offset % SUBLANES
    rows = pl.ds(offset - sft, CHUNK)
    return pad_ref[rows, cols] if sft == 0 else sh_ref[sft - 1, rows, cols]


def ab_mid_forward(z, sp, name, comm=None):
    s = z.shape[0]
    aw = z.shape[1] // 4
    tm = _tile(s, 512)
    nch, nt = tm // CHUNK, s // tm

    def body(zu_ref, zv_ref, zb_ref, zp_ref, zn_ref, w_ref, bias_ref, vg_ref, vb_ref, cw_ref, cb_ref, cg_ref, cbn_ref,
             y_ref, gc_ref, vl_ref, sv_ref, pad_ref, sh_ref):
        i = pl.program_id(0)
        vl_ref[...] = _layernorm(_gelu(zv_ref[...]), vg_ref[...], vb_ref[...])[0]
        _spatial_apply(vl_ref, w_ref, sv_ref, nch, bias_ref)
        y_ref[:, :aw] = (_gelu(zu_ref[...]) * sv_ref[...]).astype(BF16)

        _fill_padded(pad_ref, _glu(zp_ref[...]), _glu(zb_ref[...]), _glu(zn_ref[...]), i, nt, tm)
        _fill_shifts(sh_ref, pad_ref)
        for rb in range(tm // CHUNK):
            for lb in range(aw // LANES):
                cols = pl.ds(lb * LANES, LANES)
                acc = jnp.broadcast_to(cb_ref[:, cols], (CHUNK, LANES))
                for k in range(CONV_W):
                    acc = acc + cw_ref[k:k + 1, cols] * _tap(pad_ref, sh_ref, rb * CHUNK + CONV_HALO - CONV_W // 2 + k, cols)
                gc_ref[rb * CHUNK:(rb + 1) * CHUNK, cols] = acc
        yl = _layernorm(gc_ref[...], cg_ref[...], cbn_ref[...])[0]
        y_ref[:, aw:] = (yl * _sigmoid(yl)).astype(BF16)

    return _pallas(
        body, name=name, grid=(nt,), comm=comm,
        args=[z, z, z, z, z, sp["w"], sp["bias"], sp["vg"], sp["vb"], sp["cw"], sp["cb"], sp["cg"], sp["cbn"]],
        in_specs=[pl.BlockSpec((tm, aw), lambda i: (i, 0)), pl.BlockSpec((tm, aw), lambda i: (i, 1))]
        + _halo_specs(tm, s, 2 * aw, 1)
        + [_const_spec(sp["w"].shape), _const_spec(sp["bias"].shape)]
        + [_const_spec((1, aw))] * 2 + [_const_spec(sp["cw"].shape)] + [_const_spec((1, aw))] * 3,
        out_specs=[pl.BlockSpec((tm, 2 * aw), lambda i: (i, 0)), pl.BlockSpec((tm, aw), lambda i: (i, 0))],
        out_shape=[jax.ShapeDtypeStruct((s, 2 * aw), BF16), jax.ShapeDtypeStruct((s, aw), F32)],
        scratch=[pltpu.VMEM((tm, aw), F32), pltpu.VMEM((tm, aw), F32), pltpu.VMEM((tm + 2 * CONV_HALO, aw), F32),
                 _shift_scratch(tm, aw)],
        sem=("parallel",))


def _accumulate(ref, val, first):
    @pl.when(first)
    def _():
        ref[...] = val

    @pl.when(jnp.logical_not(first))
    def _():
        ref[...] += val


def ab_mid_backward(dy, z, gc, sp, name):
    s = z.shape[0]
    aw = z.shape[1] // 4
    tm = _tile(s, 512)
    nch, nt = tm // CHUNK, s // tm

    def body(dya_ref, dyb_ref, zu_ref, zv_ref, gc_ref, w_ref, wt_ref, bias_ref, vg_ref, vb_ref, cg_ref, cbn_ref,
             dz_ref, dgc_ref, dw_ref, dsb_ref, dvg_ref, dvb_ref, dcg_ref, dcbn_ref, dcb_ref,
             vl_ref, sv_ref, dsv_ref, dvl_ref):
        first = pl.program_id(0) == 0
        zu, zv = zu_ref[...], zv_ref[...]
        u = _gelu(zu)
        vl, vn, vrstd = _layernorm(_gelu(zv), vg_ref[...], vb_ref[...])
        vl_ref[...] = vl
        _spatial_apply(vl_ref, w_ref, sv_ref, nch, bias_ref)
        dya = dya_ref[...]
        dz_ref[:, :aw] = (dya * sv_ref[...] * _gelu_grad(zu)).astype(BF16)
        dsv = dya * u
        dsv_ref[...] = dsv
        _spatial_apply(dsv_ref, wt_ref, dvl_ref, nch)

        for jp in range(4):
            dlo, dhi = _group_halves(dsv_ref, jp, nch)
            vlo, vhi = _group_halves(vl_ref, jp, nch)
            vall = vlo + vhi
            _accumulate(dw_ref.at[2 * jp], _dot_nt(dlo, vall), first)
            _accumulate(dw_ref.at[2 * jp + 1], _dot_nt(dhi, vall), first)
        rows = dsv[0:CHUNK]
        for c in range(1, nch):
            rows = rows + dsv[c * CHUNK:(c + 1) * CHUNK]
        grp = lax.broadcasted_iota(jnp.int32, (8, aw), 0) == lax.broadcasted_iota(jnp.int32, (8, aw), 1) // HEAD_DIM
        e = jnp.where(grp, 1.0, 0.0).astype(BF16)
        hi = rows.astype(BF16)
        r1 = rows - hi.astype(F32)
        mid = r1.astype(BF16)
        lo = (r1 - mid.astype(F32)).astype(BF16)
        _accumulate(dsb_ref, _dot_nt(e, hi) + _dot_nt(e, mid) + _dot_nt(e, lo), first)

        dvl = dvl_ref[...]
        _accumulate(dvg_ref, jnp.sum(dvl * vn, axis=0, keepdims=True), first)
        _accumulate(dvb_ref, jnp.sum(dvl, axis=0, keepdims=True), first)
        dz_ref[:, aw:] = (_layernorm_backward(dvl, vn, vrstd, vg_ref[...]) * _gelu_grad(zv)).astype(BF16)

        yl, yn, yrstd = _layernorm(gc_ref[...], cg_ref[...], cbn_ref[...])
        sg = _sigmoid(yl)
        dyl = dyb_ref[...] * (sg + yl * sg * (1.0 - sg))
        _accumulate(dcg_ref, jnp.sum(dyl * yn, axis=0, keepdims=True), first)
        _accumulate(dcbn_ref, jnp.sum(dyl, axis=0, keepdims=True), first)
        dgc = _layernorm_backward(dyl, yn, yrstd, cg_ref[...])
        dgc_ref[...] = dgc
        _accumulate(dcb_ref, jnp.sum(dgc, axis=0, keepdims=True), first)

    vec = jax.ShapeDtypeStruct((1, aw), F32)
    return pl.pallas_call(
        body, name=name, grid=(nt,),
        in_specs=[pl.BlockSpec((tm, aw), lambda i: (i, 0)), pl.BlockSpec((tm, aw), lambda i: (i, 1)),
                  pl.BlockSpec((tm, aw), lambda i: (i, 0)), pl.BlockSpec((tm, aw), lambda i: (i, 1)),
                  pl.BlockSpec((tm, aw), lambda i: (i, 0)),
                  _const_spec(sp["w"].shape), _const_spec(sp["w"].shape), _const_spec(sp["bias"].shape)]
        + [_const_spec((1, aw))] * 4,
        out_specs=[pl.BlockSpec((tm, 2 * aw), lambda i: (i, 0)), pl.BlockSpec((tm, aw), lambda i: (i, 0)),
                   _const_spec(sp["w"].shape), _const_spec((8, CHUNK))] + [_const_spec((1, aw))] * 5,
        out_shape=[jax.ShapeDtypeStruct((s, 4 * aw), BF16), jax.ShapeDtypeStruct((s, aw), F32),
                   jax.ShapeDtypeStruct(sp["w"].shape, F32), jax.ShapeDtypeStruct((8, CHUNK), F32)] + [vec] * 5,
        scratch_shapes=[pltpu.VMEM((tm, aw), F32)] * 4,
        compiler_params=_params("arbitrary"),
    )(dy, dy, z, z, gc, sp["w"], sp["wt"], sp["bias"], sp["vg"], sp["vb"], sp["cg"], sp["cbn"])


def conv_backward(dgc, z, dz_in, sp, name, comm=None):
    s = z.shape[0]
    aw = z.shape[1] // 4
    tm = _tile(s, 512)
    nt = s // tm
    off = CONV_HALO - CONV_W // 2

    def body(d_ref, dp_ref, dn_ref, zb_ref, zp_ref, zn_ref, cw_ref, dzin_ref, dz_ref, dcw_ref,
             padd_ref, padg_ref, dgg_ref, shd_ref, shg_ref):
        i = pl.program_id(0)
        _fill_padded(padd_ref, dp_ref[...], d_ref[...], dn_ref[...], i, nt, tm)
        _fill_padded(padg_ref, _glu(zp_ref[...]), _glu(zb_ref[...]), _glu(zn_ref[...]), i, nt, tm)
        _fill_shifts(shd_ref, padd_ref)
        _fill_shifts(shg_ref, padg_ref)

        @pl.when(i == 0)
        def _():
            dcw_ref[...] = jnp.zeros_like(dcw_ref)

        def grad_input(cols, rb):
            acc = jnp.zeros((CHUNK, LANES), F32)
            for k in range(CONV_W):
                acc = acc + cw_ref[k:k + 1, cols] * _tap(padd_ref, shd_ref, rb * CHUNK + CONV_HALO + CONV_W // 2 - k, cols)
            dgg_ref[rb * CHUNK:(rb + 1) * CHUNK, cols] = acc

        def grad_taps(cols, rb):
            dblk = d_ref[rb * CHUNK:(rb + 1) * CHUNK, cols]
            for k in range(CONV_W):
                prod = dblk * _tap(padg_ref, shg_ref, rb * CHUNK + off + k, cols)
                dcw_ref[k:k + 1, cols] += jnp.sum(prod, axis=0, keepdims=True)

        for lb in range(aw // LANES):
            for rb in range(tm // CHUNK):
                pl.when(i >= 0)(functools.partial(grad_input, pl.ds(lb * LANES, LANES), rb))
                pl.when(i >= 0)(functools.partial(grad_taps, pl.ds(lb * LANES, LANES), rb))

        zb = zb_ref[...]
        val, sg = zb[:, :aw], _sigmoid(zb[:, aw:])
        dgg = dgg_ref[...]
        dz_ref[:, :aw] = (dgg * sg).astype(BF16)
        dz_ref[:, aw:] = (dgg * val * sg * (1.0 - sg)).astype(BF16)

    return _pallas(
        body, name=name, grid=(nt,), args=[dgc, dgc, dgc, z, z, z, sp["cw"], dz_in], comm=comm,
        in_specs=_halo_specs(tm, s, aw, 0) + _halo_specs(tm, s, 2 * aw, 1)
        + [_const_spec(sp["cw"].shape), pl.BlockSpec(memory_space=pl.ANY)],
        out_specs=[pl.BlockSpec((tm, 2 * aw), lambda i: (i, 1)), _const_spec(sp["cw"].shape)],
        out_shape=[jax.ShapeDtypeStruct((s, 4 * aw), BF16), jax.ShapeDtypeStruct(sp["cw"].shape, F32)],
        scratch=[pltpu.VMEM((tm + 2 * CONV_HALO, aw), F32)] * 2 + [pltpu.VMEM((tm, aw), F32)]
        + [_shift_scratch(tm, aw)] * 2,
        aliases={7: 0}, sem=("arbitrary",))


def rope_tables(s):
    pos = jnp.arange(s, dtype=F32)
    inv_freq = ROPE_THETA ** (-jnp.arange(0, ROT_DIM, 2, dtype=F32) / ROT_DIM)
    ang = pos[:, None] * inv_freq[None, :]
    cos, sin = jnp.cos(ang), jnp.sin(ang)
    half = ROT_DIM // 2
    rest = HEAD_DIM - ROT_DIM
    one, zero, zrest = jnp.ones((s, rest), F32), jnp.zeros((s, half), F32), jnp.zeros((s, rest), F32)
    c = jnp.concatenate([cos, cos, one], axis=1)
    s1 = jnp.concatenate([-sin, zero, zrest], axis=1)
    s2 = jnp.concatenate([zero, sin, zrest], axis=1)
    return tuple(jnp.tile(t, (1, LANES // HEAD_DIM)) for t in (c, s1, s2))


def qk_prep_forward(qkv, tabs, gains, name, comm=None):
    s, w3 = qkv.shape
    w = w3 // 3
    tm = _tile(s, 512)

    def body(x_ref, c_ref, s1_ref, s2_ref, g_ref, o_ref):
        bd = _block_diag(LANES)
        for rb in range(tm // CHUNK):
            rows = pl.ds(rb * CHUNK, CHUNK)
            c, s1, s2 = c_ref[rows, :], s1_ref[rows, :], s2_ref[rows, :]
            for b in range(w // LANES):
                cols = pl.ds(b * LANES, LANES)
                t = x_ref[rows, cols]
                r = lax.rsqrt(_seg_sum(t * t, bd) * (1.0 / HEAD_DIM) + EPS)
                y = t * r * g_ref[...]
                o_ref[rows, cols] = (y * c + pltpu.roll(y, LANES - ROT_DIM // 2, 1) * s1
                                     + pltpu.roll(y, ROT_DIM // 2, 1) * s2)

    tab = pl.BlockSpec((tm, LANES), lambda i, p: (i, 0))
    outs, couts = _pallas(
        body, name=name, grid=(s // tm, 2), args=[qkv, *tabs, gains], comm=comm,
        in_specs=[pl.BlockSpec((tm, w), lambda i, p: (i, p)), tab, tab, tab,
                  pl.BlockSpec((None, 1, LANES), lambda i, p: (p, 0, 0))],
        out_specs=[pl.BlockSpec((tm, w), lambda i, p: (i, p))],
        out_shape=[jax.ShapeDtypeStruct((s, 2 * w), F32)],
        sem=("parallel", "arbitrary"))
    return outs[0], couts


def qk_prep_backward(dq, dq_prev, dq_next, dk, dv, qkv, tabs, gains, name):
    s, w3 = qkv.shape
    w = w3 // 3
    tm = _tile(s, 512)
    t = _tile(s, ATT_TILE)
    per_tile, per_halo, tiles = t // tm, ATT_HALO // tm, s // t

    def neighbours(i):
        tile, c = i // per_tile, i % per_tile
        from_before = (c < per_halo) & (tile >= 1)
        from_after = (c >= per_tile - per_halo) & (tile + 1 < tiles)
        return (from_before, from_after, jnp.where(from_before, (tile - 1) * per_halo + c, 0),
                jnp.where(from_after, (tile + 1) * per_halo + c - (per_tile - per_halo), 0))

    def body(*refs):
        grads = ((refs[0], refs[1], refs[2]), (refs[3],), (refs[4],))
        x_ref, c_ref, s1_ref, s2_ref, g_ref, o_ref, dg_ref = refs[5:]
        part, first = pl.program_id(0), pl.program_id(1) == 0
        from_before, from_after, _, _ = neighbours(pl.program_id(1))

        def normed(ds):
            bd = _block_diag(LANES)
            acc = jnp.zeros((1, LANES), F32)
            for rb in range(tm // CHUNK):
                rows = pl.ds(rb * CHUNK, CHUNK)
                c, s1, s2 = c_ref[rows, :], s1_ref[rows, :], s2_ref[rows, :]
                for b in range(w // LANES):
                    cols = pl.ds(b * LANES, LANES)
                    dout = ds[0][rows, cols]
                    if len(ds) == 3:
                        dout = (dout + jnp.where(from_after, ds[1][rows, cols], 0.0)
                                + jnp.where(from_before, ds[2][rows, cols], 0.0))
                    dy = (dout * c + pltpu.roll(dout * s1, ROT_DIM // 2, 1)
                          + pltpu.roll(dout * s2, LANES - ROT_DIM // 2, 1))
                    t = x_ref[rows, cols]
                    r = lax.rsqrt(_seg_sum(t * t, bd) * (1.0 / HEAD_DIM) + EPS)
                    xh = t * r
                    acc = acc + jnp.sum(dy * xh, axis=0, keepdims=True)
                    tt = dy * g_ref[...]
                    o_ref[rows, cols] = (r * (tt - xh * (_seg_sum(tt * xh, bd) * (1.0 / HEAD_DIM)))).astype(BF16)
            _accumulate(dg_ref, acc, first)

        for p in range(2):
            pl.when(part == p)(functools.partial(normed, grads[p]))

        @pl.when(part == 2)
        def _():
            o_ref[...] = grads[2][0][...].astype(BF16)
            _accumulate(dg_ref, jnp.zeros((1, LANES), F32), first)

    def gspec(p):
        return pl.BlockSpec((tm, w), lambda q, i: (jnp.where(q == p, i, 0), 0))

    prev_spec = pl.BlockSpec((tm, w), lambda q, i: (jnp.where(q == 0, neighbours(i)[3], 0), 0))
    next_spec = pl.BlockSpec((tm, w), lambda q, i: (jnp.where(q == 0, neighbours(i)[2], 0), 0))
    tab = pl.BlockSpec((tm, LANES), lambda q, i: (i, 0))
    return pl.pallas_call(
        body, name=name, grid=(3, s // tm),
        in_specs=[gspec(0), prev_spec, next_spec, gspec(1), gspec(2)]
        + [pl.BlockSpec((tm, w), lambda q, i: (i, q)), tab, tab, tab,
           pl.BlockSpec((None, 1, LANES), lambda q, i: (q, 0, 0))],
        out_specs=[pl.BlockSpec((tm, w), lambda q, i: (i, q)), pl.BlockSpec((None, 1, LANES), lambda q, i: (q, 0, 0))],
        out_shape=[jax.ShapeDtypeStruct((s, w3), BF16), jax.ShapeDtypeStruct((3, 1, LANES), F32)],
        compiler_params=_params("arbitrary", "arbitrary"),
    )(dq, dq_prev, dq_next, dk, dv, qkv, *tabs, gains)


def _window_specs(tq, l, col_fn):
    hb, nhb = tq // BAND, l // BAND
    return [pl.BlockSpec((tq, LANES), lambda c, i: (i, col_fn(c))),
            pl.BlockSpec((BAND, LANES), lambda c, i: (jnp.maximum(i * hb - 1, 0), col_fn(c))),
            pl.BlockSpec((BAND, LANES), lambda c, i: (jnp.minimum((i + 1) * hb, nhb - 1), col_fn(c)))]


def _window(cur_ref, prev_ref, next_ref):
    return jnp.concatenate([prev_ref[...], cur_ref[...], next_ref[...]], axis=0)


def _band_mask(shape, centre_axis, first_row, length):
    ctr = lax.broadcasted_iota(jnp.int32, shape, centre_axis)
    win = lax.broadcasted_iota(jnp.int32, shape, 1 - centre_axis)
    row = first_row + win
    return (jnp.abs(win - BAND - ctr) <= BAND) & (row >= 0) & (row < length)


def _col_q(d):
    return lambda c: (c // 8) * 24 + c % 8


def _col_k(d):
    return lambda c: (c // 8) * 24 + 8 + c % 8


def _col_v(d):
    return lambda c: (c // 8) * 24 + 16 + c % 8


def band_attention_forward(qkvn, d, name):
    s, w3 = qkvn.shape
    w = w3 // 3
    l = s // d
    tq = _tile(l, 512)
    scale = HEAD_DIM ** -0.5
    xv = qkvn.reshape(l, d * w3)

    def body(q_ref, k_ref, kp_ref, kn_ref, v_ref, vp_ref, vn_ref, o_ref, lse_ref):
        i = pl.program_id(1)
        kw, vw = _window(k_ref, kp_ref, kn_ref), _window(v_ref, vp_ref, vn_ref)
        head0 = lax.broadcasted_iota(jnp.int32, (CHUNK, LANES), 1) < HEAD_DIM
        for b in range(tq // CHUNK):
            rows = pl.ds(b * CHUNK, CHUNK)
            mask = _band_mask((CHUNK, 2 * CHUNK), 0, i * tq + b * CHUNK - BAND, l)
            qb = q_ref[rows, :]
            kb, vb = kw[b * CHUNK:(b + 2) * CHUNK], vw[b * CHUNK:(b + 2) * CHUNK]
            outs, lses = [], []
            for hm in (head0, jnp.logical_not(head0)):
                sc = jnp.where(mask, _dot_nt(jnp.where(hm, qb, jnp.zeros_like(qb)), kb) * scale, NEG)
                m = jnp.max(sc, axis=1, keepdims=True)
                p = jnp.exp(sc - m)
                den = jnp.sum(p, axis=1, keepdims=True)
                outs.append(_dot(p.astype(BF16), vb) / den)
                lses.append(jnp.broadcast_to(m + jnp.log(den), (CHUNK, LANES)))
            o_ref[rows, :] = jnp.where(head0, outs[0], outs[1]).astype(BF16)
            lse_ref[rows, :] = jnp.where(head0, lses[0], lses[1])

    ospec = pl.BlockSpec((tq, LANES), lambda c, i: (i, c))
    o, lse = pl.pallas_call(
        body, name=name, grid=(d * w // LANES, l // tq),
        in_specs=[pl.BlockSpec((tq, LANES), lambda c, i: (i, _col_q(d)(c)))]
        + _window_specs(tq, l, _col_k(d)) + _window_specs(tq, l, _col_v(d)),
        out_specs=[ospec, ospec],
        out_shape=[jax.ShapeDtypeStruct((l, d * w), BF16), jax.ShapeDtypeStruct((l, d * w), F32)],
        compiler_params=_params("parallel", "parallel"),
    )(xv, xv, xv, xv, xv, xv, xv)
    return o.reshape(s, w), lse.reshape(s, w)


def attention_merge(os_, lses, name):
    s, w = os_[0].shape
    tm = _tile(s, 512)

    def body(o0, o1, o2, l0, l1, l2, o_ref, lse_ref):
        la, lb, lc = l0[...], l1[...], l2[...]
        m = jnp.maximum(jnp.maximum(la, lb), lc)
        wa, wb, wc = jnp.exp(la - m), jnp.exp(lb - m), jnp.exp(lc - m)
        den = wa + wb + wc
        o = (wa * o0[...].astype(F32) + wb * o1[...].astype(F32) + wc * o2[...].astype(F32)) / den
        o_ref[...] = o.astype(BF16)
        lse_ref[...] = m + jnp.log(den)

    spec = pl.BlockSpec((tm, w), lambda i: (i, 0))
    return pl.pallas_call(
        body, name=name, grid=(s // tm,), in_specs=[spec] * 6, out_specs=[spec, spec],
        out_shape=[jax.ShapeDtypeStruct((s, w), BF16), jax.ShapeDtypeStruct((s, w), F32)],
        compiler_params=_params("parallel"),
    )(*os_, *lses)


def attention_delta(do, o, name):
    s, w = do.shape
    tm = _tile(s, 512)

    def body(do_ref, o_ref, dl_ref, dob_ref):
        bd = _block_diag(LANES)
        for b in range(w // LANES):
            cols = pl.ds(b * LANES, LANES)
            dv = do_ref[:, cols]
            dl_ref[:, cols] = _seg_sum(dv * o_ref[:, cols].astype(F32), bd)
            dob_ref[:, cols] = dv.astype(BF16)

    spec = pl.BlockSpec((tm, w), lambda i: (i, 0))
    return pl.pallas_call(
        body, name=name, grid=(s // tm,), in_specs=[spec, spec], out_specs=[spec, spec],
        out_shape=[jax.ShapeDtypeStruct((s, w), F32), jax.ShapeDtypeStruct((s, w), BF16)],
        compiler_params=_params("parallel"),
    )(do, o)


def band_attention_dq(qkvn, dob, lse, delta, d, name):
    s, w3 = qkvn.shape
    w = w3 // 3
    l = s // d
    tq = _tile(l, 512)
    scale = HEAD_DIM ** -0.5
    xv = qkvn.reshape(l, d * w3)

    def body(q_ref, k_ref, kp_ref, kn_ref, v_ref, vp_ref, vn_ref, do_ref, lse_ref, dl_ref, dq_ref):
        i = pl.program_id(1)
        kw, vw = _window(k_ref, kp_ref, kn_ref), _window(v_ref, vp_ref, vn_ref)
        head0 = lax.broadcasted_iota(jnp.int32, (CHUNK, LANES), 1) < HEAD_DIM
        for b in range(tq // CHUNK):
            rows = pl.ds(b * CHUNK, CHUNK)
            mask = _band_mask((CHUNK, 2 * CHUNK), 0, i * tq + b * CHUNK - BAND, l)
            qb, dob_ = q_ref[rows, :], do_ref[rows, :]
            kb, vb = kw[b * CHUNK:(b + 2) * CHUNK], vw[b * CHUNK:(b + 2) * CHUNK]
            outs = []
            for h, hm in enumerate((head0, jnp.logical_not(head0))):
                col = pl.ds(h * HEAD_DIM, 1)
                sc = jnp.where(mask, _dot_nt(jnp.where(hm, qb, jnp.zeros_like(qb)), kb) * scale, NEG)
                p = jnp.exp(sc - lse_ref[rows, col])
                dp = _dot_nt(jnp.where(hm, dob_, jnp.zeros_like(dob_)), vb)
                ds = p * (dp - dl_ref[rows, col]) * scale
                outs.append(_dot(ds.astype(BF16), kb))
            dq_ref[rows, :] = jnp.where(head0, outs[0], outs[1])

    ospec = pl.BlockSpec((tq, LANES), lambda c, i: (i, c))
    dq = pl.pallas_call(
        body, name=name, grid=(d * w // LANES, l // tq),
        in_specs=[pl.BlockSpec((tq, LANES), lambda c, i: (i, _col_q(d)(c)))]
        + _window_specs(tq, l, _col_k(d)) + _window_specs(tq, l, _col_v(d)) + [ospec, ospec, ospec],
        out_specs=ospec,
        out_shape=jax.ShapeDtypeStruct((l, d * w), F32),
        compiler_params=_params("parallel", "parallel"),
    )(xv, xv, xv, xv, xv, xv, xv, dob.reshape(l, d * w), lse.reshape(l, d * w), delta.reshape(l, d * w))
    return dq.reshape(s, w)


def band_attention_dkv(qkvn, dob, lse, delta, d, name):
    s, w3 = qkvn.shape
    w = w3 // 3
    l = s // d
    tq = _tile(l, 512)
    scale = HEAD_DIM ** -0.5
    xv = qkvn.reshape(l, d * w3)

    def body(k_ref, v_ref, q_ref, qp_ref, qn_ref, do_ref, dop_ref, don_ref, lse_ref, lsep_ref, lsen_ref,
             dl_ref, dlp_ref, dln_ref, dk_ref, dv_ref):
        i = pl.program_id(1)
        qw, dow = _window(q_ref, qp_ref, qn_ref), _window(do_ref, dop_ref, don_ref)
        lsew, dlw = _window(lse_ref, lsep_ref, lsen_ref), _window(dl_ref, dlp_ref, dln_ref)
        head0 = lax.broadcasted_iota(jnp.int32, (2 * CHUNK, LANES), 1) < HEAD_DIM
        for b in range(tq // CHUNK):
            rows = pl.ds(b * CHUNK, CHUNK)
            mask = _band_mask((2 * CHUNK, CHUNK), 1, i * tq + b * CHUNK - BAND, l)
            kb, vb = k_ref[rows, :], v_ref[rows, :]
            win = slice(b * CHUNK, (b + 2) * CHUNK)
            qb, dob_, lseb, dlb = qw[win], dow[win], lsew[win], dlw[win]
            dk = jnp.zeros((CHUNK, LANES), F32)
            dv = jnp.zeros((CHUNK, LANES), F32)
            for h, hm in enumerate((head0, jnp.logical_not(head0))):
                col = slice(h * HEAD_DIM, h * HEAD_DIM + 1)
                qm = jnp.where(hm, qb, jnp.zeros_like(qb))
                dom = jnp.where(hm, dob_, jnp.zeros_like(dob_))
                sc = jnp.where(mask, _dot_nt(qm, kb) * scale, NEG)
                p = jnp.exp(sc - lseb[:, col])
                ds = p * (_dot_nt(dom, vb) - dlb[:, col]) * scale
                dv = dv + _dot_tn(p.astype(BF16), dom)
                dk = dk + _dot_tn(ds.astype(BF16), qm)
            dk_ref[rows, :] = dk
            dv_ref[rows, :] = dv

    ident = lambda c: c
    ospec = pl.BlockSpec((tq, LANES), lambda c, i: (i, c))
    dk, dv = pl.pallas_call(
        body, name=name, grid=(d * w // LANES, l // tq),
        in_specs=[pl.BlockSpec((tq, LANES), lambda c, i: (i, _col_k(d)(c))),
                  pl.BlockSpec((tq, LANES), lambda c, i: (i, _col_v(d)(c)))]
        + _window_specs(tq, l, _col_q(d)) + _window_specs(tq, l, ident) * 3,
        out_specs=[ospec, ospec],
        out_shape=[jax.ShapeDtypeStruct((l, d * w), F32)] * 2,
        compiler_params=_params("parallel", "parallel"),
    )(xv, xv, xv, xv, xv, *[dob.reshape(l, d * w)] * 3, *[lse.reshape(l, d * w)] * 3, *[delta.reshape(l, d * w)] * 3)
    return dk.reshape(s, w), dv.reshape(s, w)


ATT_TILE = 2048
ATT_HALO = BAND * max(DILATIONS)
ROWS_PER_COPY = 256


def _att_specs(s, t, col_fn, halo=True):
    hb, nhb = t // ATT_HALO, s // ATT_HALO
    specs = [pl.BlockSpec((t, LANES), lambda hp, i: (i, col_fn(hp)))]
    if halo:
        specs += [pl.BlockSpec((ATT_HALO, LANES), lambda hp, i: (jnp.maximum(i * hb - 1, 0), col_fn(hp))),
                  pl.BlockSpec((ATT_HALO, LANES), lambda hp, i: (jnp.minimum((i + 1) * hb, nhb - 1), col_fn(hp)))]
    return specs


def _gather_rows(dst_ref, dst_row, src_ref, start, count, stride, scale=None):
    for c in range(0, count, ROWS_PER_COPY):
        m = min(ROWS_PER_COPY, count - c)
        v = src_ref[pl.ds(start + c * stride, m, stride=stride), :]
        if scale is not None:
            v = v * scale
        dst_ref[dst_row + c:dst_row + c + m, :] = v.astype(dst_ref.dtype)


SPLIT = 4


def _split_rows(tmp_ref, base, src_ref, rows):
    part = rows // SPLIT
    for b in range(SPLIT):
        _gather_rows(tmp_ref, base + b * part, src_ref, b, part, SPLIT)


def _stage(dst_ref, cur_ref, d, t, scale=None, tmp_ref=None):
    n = t // d
    if tmp_ref is None or d != SPLIT * SPLIT:
        for r in range(d):
            _gather_rows(dst_ref, r * n, cur_ref, r, n, d, scale)
        return
    _split_rows(tmp_ref, 0, cur_ref, t)
    for r in range(d):
        _gather_rows(dst_ref, r * n, tmp_ref, (r % SPLIT) * (t // SPLIT) + r // SPLIT, n, SPLIT, scale)


def _stage_window(dst_ref, refs, d, t, scale=None, edges=None, tmp_ref=None):
    cur_ref, prev_ref, next_ref = refs
    n = t // d
    nw = n + 2 * BAND
    if tmp_ref is None or d != SPLIT * SPLIT:
        for r in range(d):
            _gather_rows(dst_ref, r * nw, prev_ref, ATT_HALO - BAND * d + r, BAND, d, scale)
            _gather_rows(dst_ref, r * nw + BAND, cur_ref, r, n, d, scale)
            _gather_rows(dst_ref, r * nw + BAND + n, next_ref, r, BAND, d, scale)
    else:
        assert ATT_HALO == BAND * d
        _split_rows(tmp_ref, 0, cur_ref, t)
        _split_rows(tmp_ref, t, prev_ref, ATT_HALO)
        _split_rows(tmp_ref, t + ATT_HALO, next_ref, ATT_HALO)
        for r in range(d):
            a, b = r // SPLIT, r % SPLIT
            _gather_rows(dst_ref, r * nw, tmp_ref, t + b * (ATT_HALO // SPLIT) + a, BAND, SPLIT, scale)
            _gather_rows(dst_ref, r * nw + BAND, tmp_ref, b * (t // SPLIT) + a, n, SPLIT, scale)
            _gather_rows(dst_ref, r * nw + BAND + n, tmp_ref, t + ATT_HALO + b * (ATT_HALO // SPLIT) + a, BAND, SPLIT, scale)
    if edges is not None:
        first, last, value = edges
        fill = jnp.full((BAND, LANES), value, dst_ref.dtype)

        @pl.when(first)
        def _():
            for r in range(d):
                dst_ref[r * nw:r * nw + BAND, :] = fill

        @pl.when(last)
        def _():
            for r in range(d):
                dst_ref[r * nw + BAND + n:(r + 1) * nw, :] = fill


def _band_bias(rows, cols, centre_axis):
    shape = (rows // 2, cols)
    ctr = lax.broadcasted_iota(jnp.int32, shape, centre_axis)
    win = lax.broadcasted_iota(jnp.int32, shape, 1 - centre_axis)
    bias = jnp.where(jnp.abs(win - BAND - ctr) <= BAND, 0.0, NEG).astype(F32)
    return jnp.concatenate([bias, bias], axis=0)


def _window_bias(first_row, length):
    row = first_row + lax.broadcasted_iota(jnp.int32, (1, 2 * CHUNK), 1)
    return jnp.where((row >= 0) & (row < length), 0.0, NEG).astype(F32)


UNITS_PER_TRIP = 8


def _scatter_rows(dst_ref, src_ref, d, t, combine):
    n = t // d
    for r in range(d):
        for c in range(0, n, ROWS_PER_COPY):
            m = min(ROWS_PER_COPY, n - c)
            idx = pl.ds(r + c * d, m, stride=d)
            combine(idx, slice(r * n + c, r * n + c + m))


def _unit_rows(u, n):
    upr = n // CHUNK
    r = u // upr
    b = u - r * upr
    return pl.multiple_of(u * CHUNK, CHUNK), pl.multiple_of((u + r) * CHUNK, CHUNK), b * CHUNK - BAND


def _two_heads(x):
    head0 = lax.broadcasted_iota(jnp.int32, x.shape, 1) < HEAD_DIM
    zero = jnp.zeros_like(x)
    return jnp.concatenate([jnp.where(head0, x, zero), jnp.where(head0, zero, x)], axis=0)


def _head_columns(x):
    return jnp.concatenate([x[:, 0:1], x[:, HEAD_DIM:HEAD_DIM + 1]], axis=0)


def _merge_heads(x2):
    rows = x2.shape[0] // 2
    head0 = lax.broadcasted_iota(jnp.int32, (rows, LANES), 1) < HEAD_DIM
    return jnp.where(head0, jnp.broadcast_to(x2[:rows], (rows, LANES)), jnp.broadcast_to(x2[rows:], (rows, LANES)))


def _col(part):
    return lambda hp: part * 8 + hp


def attention_forward(qk, qkv, name, comm=None):
    s, w2 = qk.shape
    w = w2 // 2
    t = _tile(s, ATT_TILE)
    scale = HEAD_DIM ** -0.5

    def body(q_ref, k_ref, kp_ref, kn_ref, v_ref, vp_ref, vn_ref, o_ref, lse_ref,
             qs_ref, ks_ref, vs_ref, os_ref, ls_ref, or_ref, tmp_ref):
        i = pl.program_id(1)
        band = _band_bias(2 * CHUNK, 2 * CHUNK, 0)
        for pi, d in enumerate(DILATIONS):
            n = t // d
            _stage(qs_ref, q_ref, d, t, scale, tmp_ref=tmp_ref)
            _stage_window(ks_ref, (k_ref, kp_ref, kn_ref), d, t, tmp_ref=tmp_ref)
            _stage_window(vs_ref, (v_ref, vp_ref, vn_ref), d, t, tmp_ref=tmp_ref)

            def unit(u, carry, n=n, d=d):
                qrow, wrow, first = _unit_rows(u, n)
                kb, vb = ks_ref[pl.ds(wrow, 2 * CHUNK), :], vs_ref[pl.ds(wrow, 2 * CHUNK), :]
                sc = _dot_nt(_two_heads(qs_ref[pl.ds(qrow, CHUNK), :]), kb) + band + _window_bias(i * n + first, s // d)
                m = jnp.max(sc, axis=1, keepdims=True)
                p = jnp.exp(sc - m)
                den = jnp.sum(p, axis=1, keepdims=True)
                os_ref[pl.ds(qrow, CHUNK), :] = _merge_heads(_dot(p.astype(BF16), vb) / den)
                ls_ref[pl.ds(qrow, CHUNK), :] = _merge_heads(m + jnp.log(den))
                return carry

            lax.fori_loop(0, t // CHUNK, unit, 0, unroll=2 * UNITS_PER_TRIP)

            if pi == 0:
                def assign(idx, rows):
                    or_ref[idx, :] = os_ref[rows, :]
                    lse_ref[idx, :] = ls_ref[rows, :]
                _scatter_rows(None, None, d, t, assign)
            else:
                def merge(idx, rows):
                    la, lb = lse_ref[idx, :], ls_ref[rows, :]
                    mx = jnp.maximum(la, lb)
                    wa, wb = jnp.exp(la - mx), jnp.exp(lb - mx)
                    den = wa + wb
                    or_ref[idx, :] = (wa * or_ref[idx, :] + wb * os_ref[rows, :]) / den
                    lse_ref[idx, :] = mx + jnp.log(den)
                _scatter_rows(None, None, d, t, merge)
        o_ref[...] = or_ref[...].astype(BF16)

    ospec = pl.BlockSpec((t, LANES), lambda hp, i: (i, hp))
    win_rows = t + 2 * ATT_HALO
    return _pallas(
        body, name=name, grid=(w // LANES, s // t), args=[qk, qk, qk, qk, qkv, qkv, qkv], comm=comm,
        in_specs=_att_specs(s, t, _col(0), halo=False) + _att_specs(s, t, _col(1)) + _att_specs(s, t, _col(2)),
        out_specs=[ospec, ospec],
        out_shape=[jax.ShapeDtypeStruct((s, w), BF16), jax.ShapeDtypeStruct((s, w), F32)],
        scratch=[pltpu.VMEM((t, LANES), BF16), pltpu.VMEM((win_rows, LANES), BF16), pltpu.VMEM((win_rows, LANES), BF16),
                 pltpu.VMEM((t, LANES), F32), pltpu.VMEM((t, LANES), F32), pltpu.VMEM((t, LANES), F32),
                 pltpu.VMEM((win_rows, LANES), F32)],
        sem=("parallel", "parallel"))


def attention_delta(do, o, lse, name):
    s, w = do.shape
    tm = _tile(s, 512)

    def body(do_ref, o_ref, lse_ref, st_ref):
        bd = _block_diag(LANES)
        lane = lax.broadcasted_iota(jnp.int32, (CHUNK, LANES), 1)
        for rb in range(tm // CHUNK):
            rows = pl.ds(rb * CHUNK, CHUNK)
            for b in range(w // LANES):
                cols = pl.ds(b * LANES, LANES)
                dl = _seg_sum(do_ref[rows, cols] * o_ref[rows, cols].astype(F32), bd)
                ls = lse_ref[rows, cols]
                st_ref[rows, cols] = jnp.where(lane % HEAD_DIM < HEAD_DIM // 2, ls, dl)

    spec = pl.BlockSpec((tm, w), lambda i: (i, 0))
    return pl.pallas_call(
        body, name=name, grid=(s // tm,), in_specs=[spec, spec, spec], out_specs=spec,
        out_shape=jax.ShapeDtypeStruct((s, w), F32), compiler_params=_params("parallel"),
    )(do, o, lse)


def attention_dq(qk, qkv, do, lse, delta, name, comm=None):
    s, w2 = qk.shape
    w = w2 // 2
    t = _tile(s, ATT_TILE)
    scale = HEAD_DIM ** -0.5

    def body(q_ref, k_ref, kp_ref, kn_ref, v_ref, vp_ref, vn_ref, do_ref, lse_ref, dl_ref, dq_ref,
             qs_ref, ks_ref, vs_ref, dos_ref, ls_ref, dls_ref, dqs_ref):
        i = pl.program_id(1)
        band = _band_bias(2 * CHUNK, 2 * CHUNK, 0)
        for pi, d in enumerate(DILATIONS):
            n = t // d
            _stage(qs_ref, q_ref, d, t, scale)
            _stage(dos_ref, do_ref, d, t)
            _stage(ls_ref, lse_ref, d, t)
            _stage(dls_ref, dl_ref, d, t)
            _stage_window(ks_ref, (k_ref, kp_ref, kn_ref), d, t)
            _stage_window(vs_ref, (v_ref, vp_ref, vn_ref), d, t)

            def unit(u, carry, n=n, d=d):
                qrow, wrow, first = _unit_rows(u, n)
                rows = pl.ds(qrow, CHUNK)
                kb, vb = ks_ref[pl.ds(wrow, 2 * CHUNK), :], vs_ref[pl.ds(wrow, 2 * CHUNK), :]
                sc = _dot_nt(_two_heads(qs_ref[rows, :]), kb) + band + _window_bias(i * n + first, s // d)
                p = jnp.exp(sc - _head_columns(ls_ref[rows, :]))
                dp = _dot_nt(_two_heads(dos_ref[rows, :]), vb)
                ds = p * (dp - _head_columns(dls_ref[rows, :]))
                dqs_ref[rows, :] = _merge_heads(_dot(ds.astype(BF16), kb)) * scale
                return carry

            lax.fori_loop(0, t // CHUNK, unit, 0, unroll=UNITS_PER_TRIP)

            def add(idx, rows, pi=pi):
                dq_ref[idx, :] = dqs_ref[rows, :] if pi == 0 else dq_ref[idx, :] + dqs_ref[rows, :]
            _scatter_rows(None, None, d, t, add)

    ospec = pl.BlockSpec((t, LANES), lambda hp, i: (i, hp))
    win_rows = t + 2 * ATT_HALO
    return _pallas(
        body, name=name, grid=(w // LANES, s // t), args=[qk, qk, qk, qk, qkv, qkv, qkv, do, lse, delta], comm=comm,
        in_specs=_att_specs(s, t, _col(0), halo=False) + _att_specs(s, t, _col(1)) + _att_specs(s, t, _col(2))
        + [ospec, ospec, ospec],
        out_specs=[ospec],
        out_shape=[jax.ShapeDtypeStruct((s, w), F32)],
        scratch=[pltpu.VMEM((t, LANES), BF16), pltpu.VMEM((win_rows, LANES), BF16), pltpu.VMEM((win_rows, LANES), BF16),
                 pltpu.VMEM((t, LANES), BF16), pltpu.VMEM((t, LANES), F32), pltpu.VMEM((t, LANES), F32),
                 pltpu.VMEM((t, LANES), F32)],
        sem=("parallel", "parallel"))


def attention_dkv(qk, qkv, do, stats, name):
    s, w2 = qk.shape
    w = w2 // 2
    t = _tile(s, ATT_TILE)
    scale = HEAD_DIM ** -0.5

    def body(k_ref, v_ref, q_ref, qp_ref, qn_ref, do_ref, dop_ref, don_ref, st_ref, stp_ref, stn_ref,
             dk_ref, dv_ref, ks_ref, vs_ref, qs_ref, dos_ref, sts_ref, dks_ref, dvs_ref):
        i = pl.program_id(1)
        key = lax.broadcasted_iota(jnp.int32, (CHUNK, 2 * CHUNK), 0)
        win_ = lax.broadcasted_iota(jnp.int32, (CHUNK, 2 * CHUNK), 1)
        half = jnp.where(jnp.abs(win_ - BAND - key) <= BAND, 0.0, NEG).astype(F32)
        band = jnp.concatenate([half, half], axis=1)
        edges = (i == 0, i == s // t - 1, -NEG)
        for pi, d in enumerate(DILATIONS):
            n = t // d
            _stage(ks_ref, k_ref, d, t)
            _stage(vs_ref, v_ref, d, t)
            _stage_window(qs_ref, (q_ref, qp_ref, qn_ref), d, t, scale)
            _stage_window(dos_ref, (do_ref, dop_ref, don_ref), d, t)
            _stage_window(sts_ref, (st_ref, stp_ref, stn_ref), d, t, edges=edges)

            def unit(u, carry, n=n, d=d):
                krow, wrow, _ = _unit_rows(u, n)
                rows, win = pl.ds(krow, CHUNK), pl.ds(wrow, 2 * CHUNK)
                q2, do2 = _two_heads(qs_ref[win, :]), _two_heads(dos_ref[win, :])
                st = jnp.transpose(sts_ref[win, :])
                lse2 = jnp.concatenate([st[0:1, :], st[1:2, :]], axis=1)
                dl2 = jnp.concatenate([st[2:3, :], st[3:4, :]], axis=1)
                p = jnp.exp(_dot_nt(ks_ref[rows, :], q2) + band - lse2)
                ds = p * (_dot_nt(vs_ref[rows, :], do2) - dl2)
                dvs_ref[rows, :] = _dot(p.astype(BF16), do2)
                dks_ref[rows, :] = _dot(ds.astype(BF16), q2)
                return carry

            lax.fori_loop(0, t // CHUNK, unit, 0, unroll=UNITS_PER_TRIP)

            def add(idx, rows, pi=pi):
                dk_ref[idx, :] = dks_ref[rows, :] if pi == 0 else dk_ref[idx, :] + dks_ref[rows, :]
                dv_ref[idx, :] = dvs_ref[rows, :] if pi == 0 else dv_ref[idx, :] + dvs_ref[rows, :]
            _scatter_rows(None, None, d, t, add)

    ident = lambda hp: hp
    ospec = pl.BlockSpec((t, LANES), lambda hp, i: (i, hp))
    win_rows = t + 2 * ATT_HALO
    return pl.pallas_call(
        body, name=name, grid=(w // LANES, s // t),
        in_specs=_att_specs(s, t, _col(1), halo=False) + _att_specs(s, t, _col(2), halo=False)
        + _att_specs(s, t, _col(0)) + _att_specs(s, t, ident) * 2,
        out_specs=[ospec, ospec],
        out_shape=[jax.ShapeDtypeStruct((s, w), F32)] * 2,
        scratch_shapes=[pltpu.VMEM((t, LANES), BF16), pltpu.VMEM((t, LANES), BF16),
                        pltpu.VMEM((win_rows, LANES), BF16), pltpu.VMEM((win_rows, LANES), BF16),
                        pltpu.VMEM((win_rows, LANES), F32),
                        pltpu.VMEM((t, LANES), F32), pltpu.VMEM((t, LANES), F32)],
        compiler_params=_params("parallel", "parallel"),
    )(qk, qkv, qk, qk, qk, do, do, do, stats, stats, stats)


def attention_backward(qk, qkv, do, stats, name, comm=None):
    s, w2 = qk.shape
    w = w2 // 2
    t = _tile(s, ATT_TILE)
    tiles = s // t
    scale = HEAD_DIM ** -0.5

    def body(k_ref, v_ref, q_ref, qp_ref, qn_ref, do_ref, dop_ref, don_ref, st_ref, stp_ref, stn_ref,
             dk_ref, dv_ref, dq_ref, dqp_ref, dqn_ref, ks_ref, vs_ref, qs_ref, dos_ref, sts_ref, dks_ref, dvs_ref, dqw_ref):
        i = pl.program_id(1)
        key = lax.broadcasted_iota(jnp.int32, (CHUNK, 2 * CHUNK), 0)
        win_ = lax.broadcasted_iota(jnp.int32, (CHUNK, 2 * CHUNK), 1)
        half = jnp.where(jnp.abs(win_ - BAND - key) <= BAND, 0.0, NEG).astype(F32)
        band = jnp.concatenate([half, half], axis=1)
        edges = (i == 0, i == tiles - 1, -NEG)
        dqp_ref[...] = jnp.zeros_like(dqp_ref)
        dqn_ref[...] = jnp.zeros_like(dqn_ref)
        for pi, d in enumerate(DILATIONS):
            n = t // d
            nw = n + 2 * BAND
            _stage(ks_ref, k_ref, d, t, tmp_ref=dqw_ref)
            _stage(vs_ref, v_ref, d, t, tmp_ref=dqw_ref)
            _stage_window(qs_ref, (q_ref, qp_ref, qn_ref), d, t, scale, tmp_ref=dqw_ref)
            _stage_window(dos_ref, (do_ref, dop_ref, don_ref), d, t, tmp_ref=dqw_ref)
            _stage_window(sts_ref, (st_ref, stp_ref, stn_ref), d, t, edges=edges, tmp_ref=dqw_ref)
            dqw_ref[...] = jnp.zeros_like(dqw_ref)

            def unit(u, carry, n=n, d=d):
                krow, wrow, _ = _unit_rows(u, n)
                rows, win = pl.ds(krow, CHUNK), pl.ds(wrow, 2 * CHUNK)
                kb = ks_ref[rows, :]
                q2, do2 = _two_heads(qs_ref[win, :]), _two_heads(dos_ref[win, :])
                st = jnp.transpose(sts_ref[win, :])
                half = HEAD_DIM // 2
                lse2 = jnp.concatenate([st[0:1, :], st[HEAD_DIM:HEAD_DIM + 1, :]], axis=1)
                dl2 = jnp.concatenate([st[half:half + 1, :], st[HEAD_DIM + half:HEAD_DIM + half + 1, :]], axis=1)
                p = jnp.exp(_dot_nt(kb, q2) + band - lse2)
                ds = (p * (_dot_nt(vs_ref[rows, :], do2) - dl2)).astype(BF16)
                dvs_ref[rows, :] = _dot(p.astype(BF16), do2)
                dks_ref[rows, :] = _dot(ds, q2)
                k2 = _two_heads(kb)
                dqw = _dot_tn(ds[:, :2 * CHUNK], k2[:CHUNK]) + _dot_tn(ds[:, 2 * CHUNK:], k2[CHUNK:])
                dqw_ref[win, :] += dqw * scale
                return carry

            lax.fori_loop(0, t // CHUNK, unit, 0, unroll=2 * UNITS_PER_TRIP)

            def add(idx, rows, pi=pi):
                dk_ref[idx, :] = dks_ref[rows, :] if pi == 0 else dk_ref[idx, :] + dks_ref[rows, :]
                dv_ref[idx, :] = dvs_ref[rows, :] if pi == 0 else dv_ref[idx, :] + dvs_ref[rows, :]
            _scatter_rows(None, None, d, t, add)

            for r in range(d):
                before = pl.ds(ATT_HALO - BAND * d + r, BAND, stride=d)
                after = pl.ds(r, BAND, stride=d)
                dqp_ref[before, :] += dqw_ref[r * nw:r * nw + BAND, :]
                dqn_ref[after, :] += dqw_ref[r * nw + BAND + n:(r + 1) * nw, :]
                for c in range(0, n, ROWS_PER_COPY):
                    m = min(ROWS_PER_COPY, n - c)
                    idx = pl.ds(r + c * d, m, stride=d)
                    val = dqw_ref[r * nw + BAND + c:r * nw + BAND + c + m, :]
                    dq_ref[idx, :] = val if pi == 0 else dq_ref[idx, :] + val

    ident = lambda hp: hp
    ospec = pl.BlockSpec((t, LANES), lambda hp, i: (i, hp))
    hspec = pl.BlockSpec((ATT_HALO, LANES), lambda hp, i: (i, hp))
    win_rows = t + 2 * ATT_HALO
    halo = jax.ShapeDtypeStruct((tiles * ATT_HALO, w), F32)
    return _pallas(
        body, name=name, grid=(w // LANES, tiles), comm=comm,
        args=[qk, qkv, qk, qk, qk, do, do, do, stats, stats, stats],
        in_specs=_att_specs(s, t, _col(1), halo=False) + _att_specs(s, t, _col(2), halo=False)
        + _att_specs(s, t, _col(0)) + _att_specs(s, t, ident) * 2,
        out_specs=[ospec, ospec, ospec, hspec, hspec],
        out_shape=[jax.ShapeDtypeStruct((s, w), F32)] * 3 + [halo, halo],
        scratch=[pltpu.VMEM((t, LANES), BF16), pltpu.VMEM((t, LANES), BF16),
                 pltpu.VMEM((win_rows, LANES), BF16), pltpu.VMEM((win_rows, LANES), BF16),
                 pltpu.VMEM((win_rows, LANES), F32),
                 pltpu.VMEM((t, LANES), F32), pltpu.VMEM((t, LANES), F32), pltpu.VMEM((win_rows, LANES), F32)],
        sem=("parallel", "parallel"))


def adamw_update(recvs, w, m, v, name, comm=None):
    nl = len(recvs)
    r, c = recvs[0].shape[1:]
    tr = 256 if (r > 256 and r % 256 == 0) else r
    nt = r // tr
    c1 = 1.0 - ADAM_B1 ** ADAM_STEP
    c2 = 1.0 - ADAM_B2 ** ADAM_STEP

    def body(*refs):
        g_refs = refs[:nl]
        w_ref, m_ref, v_ref, go_ref, d_ref, mo_ref, vo_ref = refs[nl:]

        def update(g_ref):
            g = g_ref[0].astype(F32)
            for j in range(1, N_DEV):
                g = g + g_ref[j].astype(F32)
            mn = ADAM_B1 * m_ref[...] + (1.0 - ADAM_B1) * g
            vn = ADAM_B2 * v_ref[...] + (1.0 - ADAM_B2) * (g * g)
            go_ref[...] = g
            mo_ref[...] = mn
            vo_ref[...] = vn
            d_ref[...] = -ADAM_LR * ((mn / c1) / (jnp.sqrt(vn / c2) + ADAM_EPS) + ADAM_WD * w_ref[...])

        for layer in range(nl):
            pl.when(pl.program_id(0) == layer)(functools.partial(update, g_refs[layer]))

    def gspec(layer):
        return pl.BlockSpec((N_DEV, tr, c), lambda l, i: (0, jnp.where(l == layer, i, 0), 0))

    spec = pl.BlockSpec((tr, c), lambda l, i: (l * nt + i, 0))
    return _pallas(
        body, name=name, grid=(nl, nt), args=[*recvs, w, m, v], comm=comm,
        in_specs=[gspec(layer) for layer in range(nl)] + [spec, spec, spec],
        out_specs=[spec] * 4, out_shape=[jax.ShapeDtypeStruct((nl * r, c), F32)] * 4,
        sem=("arbitrary", "arbitrary"))


BIG = ("mlp_w1", "mlp_w2", "ab_w_in", "ab_w_out", "c_w_qkv", "c_w_out")
SMALL = ("mix_norm_g", "mlp_norm_g", "a_spatial_w", "a_spatial_b", "a_vnorm_g", "a_vnorm_b", "b_conv_b", "b_norm_g",
         "b_norm_b", "c_q_norm_g", "c_k_norm_g")
WEIGHTS = ("mix_norm_g", "mlp_norm_g", "mlp_w1", "mlp_w2", "ab_w_in", "a_spatial_w", "a_spatial_b", "a_vnorm_g",
           "a_vnorm_b", "b_conv_w", "b_conv_b", "b_norm_g", "b_norm_b", "ab_w_out", "c_w_qkv", "c_q_norm_g",
           "c_k_norm_g", "c_w_out")


def _mixer_params(p, conv_full, i):
    aw = p["a_vnorm_g"].shape[1]
    row = lambda t: t[i][None, :]
    return dict(
        w=p["a_spatial_w"][i].astype(BF16), wt=jnp.swapaxes(p["a_spatial_w"][i], 1, 2).astype(BF16),
        bias=jnp.repeat(p["a_spatial_b"][i].T, aw // p["a_spatial_b"].shape[1], axis=1),
        vg=row(p["a_vnorm_g"]), vb=row(p["a_vnorm_b"]), cw=jnp.pad(conv_full[i], ((0, 1), (0, 0))),
        cb=row(p["b_conv_b"]), cg=row(p["b_norm_g"]), cbn=row(p["b_norm_b"]))


def _head_gains(p, i):
    rep = LANES // HEAD_DIM
    return jnp.stack([jnp.tile(p["c_q_norm_g"][i], rep), jnp.tile(p["c_k_norm_g"][i], rep),
                      jnp.ones((LANES,), F32)])[:, None, :]


def _unused_forward_backward(x, target, p, wg, conv_full):
    s, d = x.shape
    depth = p["mix_norm_g"].shape[0]
    tabs = rope_tables(s)
    saved = []
    for l in range(depth):
        i = l // 2
        mix_g, mlp_g = p["mix_norm_g"][l][None, :], p["mlp_norm_g"][l][None, :]
        st = dict(x_in=x)
        if l % 2 == 0:
            sp = _mixer_params(p, conv_full, i)
            z, h = norm_matmul(x, mix_g, wg["ab_w_in"], i, f"ab_in_{l}")
            ycat, gc = ab_mid_forward(z, sp, f"ab_mid_{l}")
            x = matmul_residual(x, ycat, wg["ab_w_out"], i, f"ab_out_{l}")
            st.update(z=z, h=h, y=ycat, gc=gc, sp=sp)
        else:
            gains = _head_gains(p, i)
            qkv, h = norm_matmul(x, mix_g, wg["c_w_qkv"], i, f"c_qkv_{l}")
            qkvn = qk_prep_forward(qkv, tabs, gains, f"c_prep_{l}")
            outs = [band_attention_forward(qkvn, dil, f"c_attn_{l}_d{dil}") for dil in DILATIONS]
            o, lse = attention_merge([t[0] for t in outs], [t[1] for t in outs], f"c_merge_{l}")
            x = matmul_residual(x, o, wg["c_w_out"], i, f"c_out_{l}")
            st.update(qkv=qkv, h=h, qkvn=qkvn, y=o, lse=lse, gains=gains)
        st["x_mid"] = x
        x, a, h2 = mlp_forward(x, mlp_g, wg["mlp_w1"], wg["mlp_w2"], l, f"mlp_{l}")
        st.update(a=a, h2=h2)
        saved.append(st)

    dy, loss_part = loss_and_grad(x, target, "loss")

    big = {n: [None] * p_len for n, p_len in (("mlp_w1", depth), ("mlp_w2", depth), ("ab_w_in", depth // 2 + depth % 2),
                                              ("ab_w_out", depth // 2 + depth % 2), ("c_w_qkv", depth // 2),
                                              ("c_w_out", depth // 2))}
    small = {n: [None] * p[n].shape[0] for n in SMALL}
    conv_grads = [None] * p["b_conv_b"].shape[0]
    for l in reversed(range(depth)):
        i, st = l // 2, saved[l]
        mix_g, mlp_g = p["mix_norm_g"][l][None, :], p["mlp_norm_g"][l][None, :]
        dxm, da, dg = mlp_backward(dy, st["a"], st["x_mid"], mlp_g, wg["mlp_w1"], wg["mlp_w2"], l, f"mlp_bwd_{l}")
        small["mlp_norm_g"][l] = dg[0]
        big["mlp_w1"][l] = matmul_tn(st["h2"], da, "cols", f"mlp_dw1_{l}")
        big["mlp_w2"][l] = matmul_tn(st["a"], dy, "rows", f"mlp_dw2_{l}", relu2=True)
        if l % 2 == 0:
            sp = st["sp"]
            dycat = matmul_nt(dxm, wg["ab_w_out"], i, f"ab_out_bwd_{l}")
            big["ab_w_out"][i] = matmul_tn(st["y"], dxm, "rows", f"ab_dwout_{l}")
            dz, dgc, dw, dsb, dvg, dvb, dcg, dcbn, dcb = ab_mid_backward(dycat, st["z"], st["gc"], sp, f"ab_mid_bwd_{l}")
            dz, dcw = conv_backward(dgc, st["z"], dz, sp, f"ab_conv_bwd_{l}")
            small["a_spatial_w"][i], small["a_spatial_b"][i] = dw, dsb
            small["a_vnorm_g"][i], small["a_vnorm_b"][i] = dvg[0], dvb[0]
            small["b_norm_g"][i], small["b_norm_b"][i], small["b_conv_b"][i] = dcg[0], dcbn[0], dcb[0]
            conv_grads[i] = dcw[:CONV_W]
            dy, dg = matmul_nt_norm_backward(dz, wg["ab_w_in"], i, st["x_in"], mix_g, dxm, f"ab_in_bwd_{l}")
            big["ab_w_in"][i] = matmul_tn(st["h"], dz, "cols", f"ab_dwin_{l}")
        else:
            do = matmul_nt(dxm, wg["c_w_out"], i, f"c_out_bwd_{l}")
            big["c_w_out"][i] = matmul_tn(st["y"], dxm, "rows", f"c_dwout_{l}")
            delta, dob = attention_delta(do, st["y"], f"c_delta_{l}")
            dqs, dks, dvs = [], [], []
            for dil in DILATIONS:
                dqs.append(band_attention_dq(st["qkvn"], dob, st["lse"], delta, dil, f"c_attn_dq_{l}_d{dil}"))
                dk, dv = band_attention_dkv(st["qkvn"], dob, st["lse"], delta, dil, f"c_attn_dkv_{l}_d{dil}")
                dks.append(dk)
                dvs.append(dv)
            dqkv, dgn = qk_prep_backward(dqs, dks, dvs, st["qkv"], tabs, st["gains"], f"c_prep_bwd_{l}")
            small["c_q_norm_g"][i] = dgn[0, 0, :HEAD_DIM] + dgn[0, 0, HEAD_DIM:]
            small["c_k_norm_g"][i] = dgn[1, 0, :HEAD_DIM] + dgn[1, 0, HEAD_DIM:]
            dy, dg = matmul_nt_norm_backward(dqkv, wg["c_w_qkv"], i, st["x_in"], mix_g, dxm, f"c_qkv_bwd_{l}")
            big["c_w_qkv"][i] = matmul_tn(st["h"], dqkv, "cols", f"c_dwqkv_{l}")
        small["mix_norm_g"][l] = dg[0]
    small = {n: jnp.stack(v) for n, v in small.items()}
    return loss_part, dy, big, small, jnp.stack(conv_grads)


PACKED = tuple(n for n in SMALL if n != "a_spatial_w")


def _pack(d):
    flat = jnp.concatenate([d[n].reshape(-1) for n in PACKED])
    rows = -(-flat.shape[0] // (8 * LANES)) * 8
    return jnp.pad(flat, (0, rows * LANES - flat.shape[0])).reshape(rows, LANES)


def _unpack(packed, like):
    flat, out, pos = packed.reshape(-1), {}, 0
    for n in PACKED:
        size = math.prod(like[n].shape)
        out[n] = flat[pos:pos + size].reshape(like[n].shape)
        pos += size
    return out


def _unused_kernel(x, mix_norm_g, mlp_norm_g, mlp_w1, mlp_w2, ab_w_in, a_spatial_w, a_spatial_b, a_vnorm_g, a_vnorm_b, b_conv_w, b_conv_b, b_norm_g, b_norm_b, ab_w_out, c_w_qkv, c_q_norm_g, c_k_norm_g, c_w_out, loss_target, m_mix_norm_g, m_mlp_norm_g, m_mlp_w1, m_mlp_w2, m_ab_w_in, m_a_spatial_w, m_a_spatial_b, m_a_vnorm_g, m_a_vnorm_b, m_b_conv_w, m_b_conv_b, m_b_norm_g, m_b_norm_b, m_ab_w_out, m_c_w_qkv, m_c_q_norm_g, m_c_k_norm_g, m_c_w_out, v_mix_norm_g, v_mlp_norm_g, v_mlp_w1, v_mlp_w2, v_ab_w_in, v_a_spatial_w, v_a_spatial_b, v_a_vnorm_g, v_a_vnorm_b, v_b_conv_w, v_b_conv_b, v_b_norm_g, v_b_norm_b, v_ab_w_out, v_c_w_qkv, v_c_q_norm_g, v_c_k_norm_g, v_c_w_out):
    args = dict(locals())
    w = {n: args[n] for n in WEIGHTS}
    m = {n: args["m_" + n] for n in WEIGHTS}
    v = {n: args["v_" + n] for n in WEIGHTS}

    wg = {n: all_gather(w[n].astype(BF16), "gather_" + n) for n in BIG}
    for n in ("ab_w_out", "c_w_out"):
        t = wg[n]
        wg[n] = t.reshape(t.shape[0], t.shape[1] * t.shape[2], t.shape[3])
    conv = all_gather(w["b_conv_w"], "gather_b_conv_w")
    conv_full = jnp.swapaxes(conv, 1, 2).reshape(conv.shape[0], conv.shape[2], N_DEV * conv.shape[3])

    loss_part, dx, big, small, conv_grad = forward_backward(x[0], loss_target[0], w, wg, conv_full)
    loss = lax.psum(jnp.sum(loss_part), ("x", "y", "c"))

    grads, deltas, new_m, new_v = {}, {}, {}, {}

    def update(n, recv):
        shape = w[n].shape
        flat = lambda t: t.reshape(-1, shape[-1])
        outs = adamw_update(recv.reshape((N_DEV, -1, shape[-1])), flat(w[n]), flat(m[n]), flat(v[n]), "adamw_" + n)
        grads[n], deltas[n], new_m[n], new_v[n] = (t.reshape(shape) for t in outs)

    for n in BIG:
        update(n, exchange(big[n], "exchange_" + n))
    nl, kw, cw = conv_grad.shape
    conv_parts = jnp.transpose(conv_grad.reshape(nl, kw, N_DEV, cw // N_DEV), (2, 0, 1, 3))
    update("b_conv_w", exchange([conv_parts], "exchange_b_conv_w"))

    packed = all_gather(_pack(small)[None], "gather_small_grads")[0]
    outs = adamw_update(packed, _pack(w), _pack(m), _pack(v), "adamw_small")
    for dst, t in zip((grads, deltas, new_m, new_v), outs):
        dst.update(_unpack(t, w))

    return (loss, dx[None], *[grads[n] for n in WEIGHTS], *[deltas[n] for n in WEIGHTS],
            *[new_m[n] for n in WEIGHTS], *[new_v[n] for n in WEIGHTS])


class Traffic:
    def __init__(self, shards, full=()):
        self.shards, self.w, self.queue, self.parts = shards, dict(full), [], {}

    def run(self, fn, *args, gather=(), send=False, **kw):
        operands, flags, dest = [], [], []
        if self.shards is None:
            if send:
                self.parts.update(self.queue)
                self.queue = []
        else:
            for k in gather:
                if k not in self.w:
                    operands.append(self.shards[k])
                    flags.append(("gather", SHARD_AXIS.get(k[0])))
                    dest.append((self.w, k))
            if send:
                for k, t in self.queue:
                    operands.append(t)
                    flags.append(("scatter", SHARD_AXIS.get(k[0])))
                    dest.append((self.parts, k))
                self.queue = []
        outs, couts = fn(*args, comm=PeerCopies(operands, flags) if operands else None, **kw)
        for (table, k), t in zip(dest, couts):
            table[k] = t
        return outs

    def flush(self, name, extra=()):
        self.queue += list(extra)
        self.run(lambda comm: ([], run_copies(comm, name) if comm is not None else []), send=True)


SHARD_AXIS = {"mlp_w1": 1, "mlp_w2": 0, "ab_w_in": 1, "ab_w_out": 0, "c_w_qkv": 1, "c_w_out": 0}


def forward_backward(x, target, p, tr, conv_full):
    s, d = x.shape
    depth = p["mix_norm_g"].shape[0]
    tabs = rope_tables(s)
    saved = []
    for l in range(depth):
        i = l // 2
        mix_g, mlp_g = p["mix_norm_g"][l][None, :], p["mlp_norm_g"][l][None, :]
        st = dict(x_in=x)
        nxt = () if l + 1 == depth else ((("c_w_qkv", i), ("c_w_out", i)) if l % 2 == 0 else
                                        (("ab_w_in", i + 1), ("ab_w_out", i + 1)))
        if l % 2 == 0:
            sp = _mixer_params(p, conv_full, i)
            z, h = tr.run(norm_matmul, x, mix_g, tr.w["ab_w_in", i], f"ab_in_{l}", gather=[("mlp_w1", l)])
            ycat, gc = tr.run(ab_mid_forward, z, sp, f"ab_mid_{l}", gather=[("mlp_w2", l)])
            x = matmul_residual(x, ycat, tr.w["ab_w_out", i], f"ab_out_{l}")
            st.update(z=z, h=h, y=ycat, gc=gc, sp=sp)
        else:
            gains = _head_gains(p, i)
            qkv, h = norm_matmul(x, mix_g, tr.w["c_w_qkv", i], f"c_qkv_{l}")[0]
            qk = qk_prep_forward(qkv, tabs, gains, f"c_prep_{l}")[0]
            ahead = [("mlp_w1", l), ("mlp_w2", l)] + ([("mlp_w1", l + 1)] if l + 1 < depth else [])
            o, lse = tr.run(attention_forward, qk, qkv, f"c_attn_{l}", gather=ahead)
            x = matmul_residual(x, o, tr.w["c_w_out", i], f"c_out_{l}")
            st.update(qkv=qkv, h=h, qk=qk, y=o, lse=lse, gains=gains)
            nxt = nxt + ((("mlp_w2", l + 1),) if l + 1 < depth else ())
        st["x_mid"] = x
        x, a, h2 = tr.run(mlp_forward, x, mlp_g, tr.w["mlp_w1", l], tr.w["mlp_w2", l], f"mlp_{l}", gather=nxt)
        st.update(a=a, h2=h2)
        saved.append(st)

    dy, loss_part = loss_and_grad(x, target, "loss")

    def tn(*a, comm, **kw):
        out, couts = matmul_tn(*a, comm=comm, **kw)
        return [out], couts

    small = {n: [None] * p[n].shape[0] for n in SMALL}
    conv_grads = [None] * p["b_conv_b"].shape[0]
    for l in reversed(range(depth)):
        i, st = l // 2, saved[l]
        mix_g, mlp_g = p["mix_norm_g"][l][None, :], p["mlp_norm_g"][l][None, :]
        w1, w2 = tr.w["mlp_w1", l], tr.w["mlp_w2", l]
        dxm, da, dg, dyb = tr.run(mlp_backward, dy, st["a"], st["x_mid"], mlp_g, w1, w2, f"mlp_bwd_{l}", send=True)
        small["mlp_norm_g"][l] = dg[0]
        tr.queue.append((("mlp_w1", l), matmul_tn(st["h2"], da, f"mlp_dw1_{l}", n_split=2)[0]))
        tr.queue.append((("mlp_w2", l), matmul_tn(st["a"], dyb, f"mlp_dw2_{l}", m_split=2, relu2=True)[0]))
        if l % 2 == 0:
            sp = st["sp"]
            wout = tr.w["ab_w_out", i]
            dycat = matmul_nt(dxm, wout, f"ab_out_bwd_{l}")
            tr.queue.append((("ab_w_out", i), matmul_tn(st["y"], dxm, f"ab_dwout_{l}")[0]))
            dz, dgc, dw, dsb, dvg, dvb, dcg, dcbn, dcb = ab_mid_backward(dycat, st["z"], st["gc"], sp, f"ab_mid_bwd_{l}")
            dz, dcw = tr.run(conv_backward, dgc, st["z"], dz, sp, f"ab_conv_bwd_{l}", send=True)
            small["a_spatial_w"][i], small["a_spatial_b"][i] = dw, dsb
            small["a_vnorm_g"][i], small["a_vnorm_b"][i] = dvg[0], dvb[0]
            small["b_norm_g"][i], small["b_norm_b"][i], small["b_conv_b"][i] = dcg[0], dcbn[0], dcb[0]
            conv_grads[i] = dcw[:CONV_W]
            dy, dg = matmul_nt_norm_backward(dz, tr.w["ab_w_in", i], st["x_in"], mix_g, dxm, f"ab_in_bwd_{l}")
            small["mix_norm_g"][l] = dg[0]
            last = []
            if l == 0 and tr.shards is not None:
                tr.shards["small_grads", 0] = _pack({n: jnp.stack(small[n]) for n in PACKED})
                tr.shards["spatial_grads", 0] = jnp.stack(small["a_spatial_w"]).astype(BF16)
                last = [("small_grads", 0), ("spatial_grads", 0)]
            dwin = tr.run(tn, st["h"], dz, f"ab_dwin_{l}", send=True, gather=last)[0]
            tr.queue.append((("ab_w_in", i), dwin))
        else:
            wout = tr.w["c_w_out", i]
            do = matmul_nt(dxm, wout, f"c_out_bwd_{l}")
            tr.queue.append((("c_w_out", i), matmul_tn(st["y"], dxm, f"c_dwout_{l}")[0]))
            stats = attention_delta(do, st["y"], st["lse"], f"c_delta_{l}")
            dk, dv, dq, dqp, dqn = tr.run(attention_backward, st["qk"], st["qkv"], do, stats, f"c_attn_bwd_{l}", send=True)
            dqkv, dgn = qk_prep_backward(dq, dqp, dqn, dk, dv, st["qkv"], tabs, st["gains"], f"c_prep_bwd_{l}")
            small["c_q_norm_g"][i] = dgn[0, 0, :HEAD_DIM] + dgn[0, 0, HEAD_DIM:]
            small["c_k_norm_g"][i] = dgn[1, 0, :HEAD_DIM] + dgn[1, 0, HEAD_DIM:]
            dy, dg = matmul_nt_norm_backward(dqkv, tr.w["c_w_qkv", i], st["x_in"], mix_g, dxm, f"c_qkv_bwd_{l}")
            small["mix_norm_g"][l] = dg[0]
            tr.queue.append((("c_w_qkv", i), matmul_tn(st["h"], dqkv, f"c_dwqkv_{l}", n_split=2)[0]))
    small = {n: jnp.stack(v) for n, v in small.items()}
    return loss_part, dy, small, jnp.stack(conv_grads)


def kernel(x, mix_norm_g, mlp_norm_g, mlp_w1, mlp_w2, ab_w_in, a_spatial_w, a_spatial_b, a_vnorm_g, a_vnorm_b, b_conv_w, b_conv_b, b_norm_g, b_norm_b, ab_w_out, c_w_qkv, c_q_norm_g, c_k_norm_g, c_w_out, loss_target, m_mix_norm_g, m_mlp_norm_g, m_mlp_w1, m_mlp_w2, m_ab_w_in, m_a_spatial_w, m_a_spatial_b, m_a_vnorm_g, m_a_vnorm_b, m_b_conv_w, m_b_conv_b, m_b_norm_g, m_b_norm_b, m_ab_w_out, m_c_w_qkv, m_c_q_norm_g, m_c_k_norm_g, m_c_w_out, v_mix_norm_g, v_mlp_norm_g, v_mlp_w1, v_mlp_w2, v_ab_w_in, v_a_spatial_w, v_a_spatial_b, v_a_vnorm_g, v_a_vnorm_b, v_b_conv_w, v_b_conv_b, v_b_norm_g, v_b_norm_b, v_ab_w_out, v_c_w_qkv, v_c_q_norm_g, v_c_k_norm_g, v_c_w_out):
    args = dict(locals())
    w = {n: args[n] for n in WEIGHTS}
    m = {n: args["m_" + n] for n in WEIGHTS}
    v = {n: args["v_" + n] for n in WEIGHTS}

    shards = {(n, l): w[n][l].astype(BF16) for n in BIG for l in range(w[n].shape[0])}
    shards["b_conv_w", 0] = w["b_conv_w"]
    tr = Traffic(shards)
    first = [("ab_w_in", 0), ("ab_w_out", 0), ("b_conv_w", 0)]
    tr.run(lambda comm: ([], run_copies(comm, "gather_first")), gather=first)
    conv = tr.w["b_conv_w", 0]
    conv_full = jnp.transpose(conv, (1, 2, 0, 3)).reshape(conv.shape[1], conv.shape[2], -1)

    loss_part, dx, small, conv_grad = forward_backward(x[0], loss_target[0], w, tr, conv_full)
    loss = lax.psum(jnp.sum(loss_part), ("x", "y", "c"))
    nl, kw, cw = conv_grad.shape
    conv_parts = jnp.transpose(conv_grad.reshape(nl, kw, N_DEV, cw // N_DEV), (2, 0, 1, 3))
    tr.queue.append((("b_conv_w", 0), conv_parts))

    grads, deltas, new_m, new_v = {}, {}, {}, {}

    def update(n, recvs):
        shape = w[n].shape
        flat = lambda t: t.reshape(-1, shape[-1])
        recvs = [t.reshape(N_DEV, -1, shape[-1]) for t in recvs]
        outs = tr.run(adamw_update, recvs, flat(w[n]), flat(m[n]), flat(v[n]), "adamw_" + n, send=True)
        grads[n], deltas[n], new_m[n], new_v[n] = (t.reshape(shape) for t in outs)

    for n in BIG:
        update(n, [tr.parts[n, l] for l in range(w[n].shape[0])])
    update("b_conv_w", [tr.parts["b_conv_w", 0]])
    update("a_spatial_w", [tr.w["spatial_grads", 0]])
    outs = adamw_update([tr.w["small_grads", 0]], _pack(w), _pack(m), _pack(v), "adamw_small")[0]
    for dst, t in zip((grads, deltas, new_m, new_v), outs):
        dst.update(_unpack(t, w))

    return (loss, dx[None], *[grads[n] for n in WEIGHTS], *[deltas[n] for n in WEIGHTS],
            *[new_m[n] for n in WEIGHTS], *[new_v[n] for n in WEIGHTS])
```

```python
import functools
import math

import jax
import jax.numpy as jnp
from jax import lax
from jax.experimental import pallas as pl
from jax.experimental.pallas import tpu as pltpu

F32, BF16 = jnp.float32, jnp.bfloat16
N_DEV = 8
EPS = 1e-6
NEG = -1e30
LANES = 128
HEAD_DIM = 64
CHUNK = 128
CONV_W = 31
CONV_HALO = 16
BAND = 64
DILATIONS = (1, 4, 16)
ROT_DIM = 16
ROPE_THETA = 500000.0
VMEM_LIMIT = 56 * 1024 * 1024
MLP_CHUNK = 1024
MLP_BWD_CHUNK = 512
MLP_BWD_ROWS = 1024
ADAM_LR, ADAM_B1, ADAM_B2, ADAM_EPS, ADAM_WD, ADAM_STEP = 0.001, 0.9, 0.999, 1e-08, 0.01, 10
MESH = pl.DeviceIdType.MESH


def _params(*sem):
    return pltpu.CompilerParams(dimension_semantics=sem, vmem_limit_bytes=VMEM_LIMIT)


def _dot(a, b):
    return jnp.dot(a, b, preferred_element_type=F32)


def _dot_nt(a, b):
    return lax.dot_general(a, b, (((1,), (1,)), ((), ())), preferred_element_type=F32)


def _dot_tn(a, b):
    return lax.dot_general(a, b, (((0,), (0,)), ((), ())), preferred_element_type=F32)


def _rms_r(x):
    return lax.rsqrt(jnp.mean(x * x, axis=-1, keepdims=True) + EPS)


def _sigmoid(x):
    return 1.0 / (1.0 + jnp.exp(-x))


_GK = math.sqrt(2.0 / math.pi)


def _gelu(x):
    return 0.5 * x * (1.0 + jnp.tanh(_GK * (x + 0.044715 * x * x * x)))


def _gelu_grad(x):
    t = jnp.tanh(_GK * (x + 0.044715 * x * x * x))
    return 0.5 * (1.0 + t) + 0.5 * x * (1.0 - t * t) * (_GK * (1.0 + 3.0 * 0.044715 * x * x))


def _seg_sum(x, bd):
    hi = x.astype(BF16)
    lo = (x - hi.astype(F32)).astype(BF16)
    return _dot(hi, bd) + _dot(lo, bd)


def _block_diag(n):
    i = lax.broadcasted_iota(jnp.int32, (n, n), 0) // HEAD_DIM
    j = lax.broadcasted_iota(jnp.int32, (n, n), 1) // HEAD_DIM
    return jnp.where(i == j, 1.0, 0.0).astype(BF16)


def _tile(s, cap):
    t = min(s, cap)
    assert s % t == 0
    return t


def _my_index():
    return 4 * lax.axis_index("x") + 2 * lax.axis_index("y") + lax.axis_index("c")


def _device(i):
    return (i // 4, (i // 2) % 2, i % 2)


_HBM = pl.BlockSpec(memory_space=pl.ANY)


class PeerCopies:
    def __init__(self, operands, modes):
        self.inputs, self.modes = list(operands), list(modes)
        self.out_shape = []
        for t, (kind, axis) in zip(self.inputs, self.modes):
            shape = list(t.shape)
            if kind == "gather":
                shape = [N_DEV] + shape if axis is None else shape[:axis] + [N_DEV * shape[axis]] + shape[axis + 1:]
            elif axis is not None:
                shape = [N_DEV] + shape[:axis] + [shape[axis] // N_DEV] + shape[axis + 1:]
            self.out_shape.append(jax.ShapeDtypeStruct(tuple(shape), t.dtype))
        n = len(self.inputs)
        self.scratch = [pltpu.SemaphoreType.DMA((n, N_DEV - 1)), pltpu.SemaphoreType.DMA((n, N_DEV - 1)),
                        pltpu.SemaphoreType.DMA((n,))]

    @staticmethod
    def _block(ref, axis, size, j):
        if axis is None:
            return ref.at[j]
        return ref.at[tuple([slice(None)] * axis + [pl.ds(j * size, size)])]

    def _copies(self, in_refs, out_refs, sems, arrivals):
        send_sems, recv_sems, local_sems = sems
        me = _my_index()
        local, sends, recvs = [], [], []
        for t, (src, dst) in enumerate(zip(in_refs, out_refs)):
            kind, axis = self.modes[t]
            if kind == "gather":
                size = None if axis is None else src.shape[axis]
                source = lambda j, src=src: src
                place = lambda j, dst=dst, axis=axis, size=size: self._block(dst, axis, size, j)
            else:
                size = None if axis is None else src.shape[axis] // N_DEV
                source = lambda j, src=src, axis=axis, size=size: self._block(src, axis, size, j)
                place = lambda j, dst=dst: dst.at[j]
            local.append(pltpu.make_async_copy(source(me), place(me), local_sems.at[t]))
            for k in range(N_DEV - 1):
                to, frm = (me + k + 1) % N_DEV, (me + N_DEV - k - 1) % N_DEV
                sends.append(pltpu.make_async_remote_copy(
                    src_ref=source(to), dst_ref=place(me), send_sem=send_sems.at[t, k], recv_sem=recv_sems.at[t, k],
                    device_id=_device(to), device_id_type=MESH))
                if arrivals:
                    recvs.append(pltpu.make_async_remote_copy(
                        src_ref=source(me), dst_ref=place(frm), send_sem=send_sems.at[t, k], recv_sem=recv_sems.at[t, k],
                        device_id=_device(frm), device_id_type=MESH))
        return local, sends, recvs

    def start(self, in_refs, out_refs, sems):
        local, sends, _ = self._copies(in_refs, out_refs, sems, False)
        for cp in local + sends:
            cp.start()

    def finish(self, in_refs, out_refs, sems):
        local, sends, recvs = self._copies(in_refs, out_refs, sems, True)
        for cp in recvs:
            cp.wait_recv()
        for cp in sends:
            cp.wait_send()
        for cp in local:
            cp.wait()


def _pallas(body, *, name, args, in_specs, out_specs, out_shape, grid=(), scratch=(), sem=(), comm=None, aliases=None):
    n_in, n_out, n_scr = len(args), len(out_shape), len(scratch)
    if comm is None:
        outs = pl.pallas_call(
            body, name=name, grid=grid, in_specs=in_specs, out_specs=out_specs, out_shape=out_shape,
            scratch_shapes=list(scratch), input_output_aliases=aliases or {}, compiler_params=_params(*sem))(*args)
        return list(outs), []
    ci, co = len(comm.inputs), len(comm.out_shape)

    def hosted(*refs):
        ins, cins = refs[:n_in], refs[n_in:n_in + ci]
        outs, couts = refs[n_in + ci:n_in + ci + n_out], refs[n_in + ci + n_out:n_in + ci + n_out + co]
        rest = refs[n_in + ci + n_out + co:]
        scr, sems = rest[:n_scr], rest[n_scr:]
        if not grid:
            comm.start(cins, couts, sems)
            comm.finish(cins, couts, sems)
            return
        first = last = None
        for axis, size in enumerate(grid):
            f, l = pl.program_id(axis) == 0, pl.program_id(axis) == size - 1
            first, last = (f, l) if first is None else (first & f, last & l)
        pl.when(first)(lambda: comm.start(cins, couts, sems))
        body(*ins, *outs, *scr)
        pl.when(last)(lambda: comm.finish(cins, couts, sems))

    outs = pl.pallas_call(
        hosted, name=name, grid=grid, in_specs=list(in_specs) + [_HBM] * ci, out_specs=list(out_specs) + [_HBM] * co,
        out_shape=list(out_shape) + comm.out_shape, scratch_shapes=list(scratch) + comm.scratch,
        input_output_aliases=aliases or {}, compiler_params=_params(*["arbitrary"] * len(grid)))(*args, *comm.inputs)
    return list(outs[:n_out]), list(outs[n_out:])


def run_copies(comm, name):
    return _pallas(None, name=name, args=[], in_specs=[], out_specs=[], out_shape=[], comm=comm)[1]


def norm_matmul(x, g, wg, name, comm=None):
    s, d = x.shape
    n = wg.shape[-1]
    ns = 1024 if n % 1024 == 0 else n // 4
    tm = _tile(s, 1024)

    def body(x_ref, g_ref, w_ref, z_ref, h_ref):
        @pl.when(pl.program_id(1) == 0)
        def _():
            xv = x_ref[...]
            h_ref[...] = (xv * _rms_r(xv) * g_ref[...]).astype(BF16)
        z_ref[...] = _dot(h_ref[...], w_ref[...])

    return _pallas(
        body, name=name, grid=(s // tm, n // ns), args=[x, g, wg], comm=comm,
        in_specs=[pl.BlockSpec((tm, d), lambda i, j: (i, 0)),
                  pl.BlockSpec((1, d), lambda i, j: (0, 0)),
                  pl.BlockSpec((d, ns), lambda i, j: (0, j))],
        out_specs=[pl.BlockSpec((tm, ns), lambda i, j: (i, j)),
                   pl.BlockSpec((tm, d), lambda i, j: (i, 0))],
        out_shape=[jax.ShapeDtypeStruct((s, n), F32), jax.ShapeDtypeStruct((s, d), BF16)],
        sem=("parallel", "arbitrary"))


def mlp_forward(x, g, w1g, w2g, name, comm=None):
    s, d = x.shape
    f = w1g.shape[-1]
    fs = MLP_CHUNK
    tm = _tile(s, 1024)

    def body(x_ref, g_ref, w1_ref, w2_ref, xo_ref, a_ref, h_ref):
        @pl.when(pl.program_id(1) == 0)
        def _():
            xv = x_ref[...]
            h_ref[...] = (xv * _rms_r(xv) * g_ref[...]).astype(BF16)
            xo_ref[...] = xv
        a = _dot(h_ref[...], w1_ref[...])
        a_ref[...] = a.astype(BF16)
        r = jnp.maximum(a, 0.0)
        xo_ref[...] += _dot((r * r).astype(BF16), w2_ref[...])

    return _pallas(
        body, name=name, grid=(s // tm, f // fs), args=[x, g, w1g, w2g], comm=comm,
        in_specs=[pl.BlockSpec((tm, d), lambda i, j: (i, 0)),
                  pl.BlockSpec((1, d), lambda i, j: (0, 0)),
                  pl.BlockSpec((d, fs), lambda i, j: (0, j)),
                  pl.BlockSpec((fs, d), lambda i, j: (j, 0))],
        out_specs=[pl.BlockSpec((tm, d), lambda i, j: (i, 0)),
                   pl.BlockSpec((tm, fs), lambda i, j: (i, j)),
                   pl.BlockSpec((tm, d), lambda i, j: (i, 0))],
        out_shape=[jax.ShapeDtypeStruct((s, d), F32), jax.ShapeDtypeStruct((s, f), BF16),
                   jax.ShapeDtypeStruct((s, d), BF16)],
        sem=("parallel", "arbitrary"))


def _norm_backward(dh, xv, g, dres):
    r = _rms_r(xv)
    xh = xv * r
    t = dh * g
    dx = dres + r * (t - xh * jnp.mean(t * xh, axis=-1, keepdims=True))
    return dx, jnp.sum(dh * xh, axis=0, keepdims=True)


def mlp_backward(dy, a, x, g, w1g, w2g, name, comm=None):
    s, d = x.shape
    f = w1g.shape[-1]
    fs = MLP_BWD_CHUNK
    tm = _tile(s, MLP_BWD_ROWS)

    def body(dy_ref, a_ref, x_ref, g_ref, w1_ref, w2_ref, dx_ref, da_ref, dg_ref, dyb_ref, dh_ref):
        i, j = pl.program_id(0), pl.program_id(1)

        @pl.when(j == 0)
        def _():
            dyb_ref[...] = dy_ref[...].astype(BF16)
            dh_ref[...] = jnp.zeros_like(dh_ref)

        dr = _dot_nt(dyb_ref[...], w2_ref[...])
        da = (dr * (2.0 * jnp.maximum(a_ref[...].astype(F32), 0.0))).astype(BF16)
        da_ref[...] = da
        dh_ref[...] += _dot_nt(da, w1_ref[...])

        @pl.when(j == f // fs - 1)
        def _():
            dx, dgp = _norm_backward(dh_ref[...], x_ref[...], g_ref[...], dy_ref[...])
            dx_ref[...] = dx

            @pl.when(i == 0)
            def _():
                dg_ref[...] = dgp

            @pl.when(i > 0)
            def _():
                dg_ref[...] += dgp

    return _pallas(
        body, name=name, grid=(s // tm, f // fs), args=[dy, a, x, g, w1g, w2g], comm=comm,
        in_specs=[pl.BlockSpec((tm, d), lambda i, j: (i, 0)),
                  pl.BlockSpec((tm, fs), lambda i, j: (i, j)),
                  pl.BlockSpec((tm, d), lambda i, j: (i, 0)),
                  pl.BlockSpec((1, d), lambda i, j: (0, 0)),
                  pl.BlockSpec((d, fs), lambda i, j: (0, j)),
                  pl.BlockSpec((fs, d), lambda i, j: (j, 0))],
        out_specs=[pl.BlockSpec((tm, d), lambda i, j: (i, 0)),
                   pl.BlockSpec((tm, fs), lambda i, j: (i, j)),
                   pl.BlockSpec((1, d), lambda i, j: (0, 0)),
                   pl.BlockSpec((tm, d), lambda i, j: (i, 0))],
        out_shape=[jax.ShapeDtypeStruct((s, d), F32), jax.ShapeDtypeStruct((s, f), BF16),
                   jax.ShapeDtypeStruct((1, d), F32), jax.ShapeDtypeStruct((s, d), BF16)],
        scratch=[pltpu.VMEM((tm, d), F32)],
        sem=("arbitrary", "arbitrary"))


def matmul_tn(a, b, name, m_split=1, n_split=1, relu2=False, comm=None):
    s, m = a.shape
    n = b.shape[1]
    ts = _tile(s, 2048)
    bm, bn = m // m_split, n // n_split
    a_map = lambda j, k: (k, j // n_split)
    b_map = lambda j, k: (k, j % n_split)

    def body(a_ref, b_ref, o_ref, acc_ref):
        k = pl.program_id(1)
        av = a_ref[...]
        if relu2:
            af = jnp.maximum(av.astype(F32), 0.0)
            av = af * af
        p = _dot_tn(av.astype(BF16), b_ref[...].astype(BF16))

        @pl.when(k == 0)
        def _():
            acc_ref[...] = p

        @pl.when(k > 0)
        def _():
            acc_ref[...] += p

        @pl.when(k == s // ts - 1)
        def _():
            o_ref[...] = acc_ref[...].astype(BF16)

    outs, couts = _pallas(
        body, name=name, grid=(m_split * n_split, s // ts), args=[a, b], comm=comm,
        in_specs=[pl.BlockSpec((ts, bm), a_map), pl.BlockSpec((ts, bn), b_map)],
        out_specs=[pl.BlockSpec((bm, bn), lambda j, k: (j // n_split, j % n_split))],
        out_shape=[jax.ShapeDtypeStruct((m, n), BF16)],
        scratch=[pltpu.VMEM((bm, bn), F32)],
        sem=("parallel", "arbitrary"))
    return outs[0], couts


def matmul_residual(x, y, w, name):
    s, n = x.shape
    k = y.shape[1]
    tm = _tile(s, 1024)

    def body(x_ref, y_ref, w_ref, o_ref):
        o_ref[...] = x_ref[...] + _dot(y_ref[...], w_ref[...])

    return pl.pallas_call(
        body, name=name, grid=(s // tm,),
        in_specs=[pl.BlockSpec((tm, n), lambda i: (i, 0)),
                  pl.BlockSpec((tm, k), lambda i: (i, 0)),
                  pl.BlockSpec((k, n), lambda i: (0, 0))],
        out_specs=pl.BlockSpec((tm, n), lambda i: (i, 0)),
        out_shape=jax.ShapeDtypeStruct((s, n), F32),
        compiler_params=_params("parallel"),
    )(x, y, w)


def matmul_nt(dy, wg, name):
    s, n = dy.shape
    k = wg.shape[0]
    tm = _tile(s, 1024)

    def body(dy_ref, w_ref, o_ref):
        o_ref[...] = _dot_nt(dy_ref[...].astype(BF16), w_ref[...])

    return pl.pallas_call(
        body, name=name, grid=(s // tm,),
        in_specs=[pl.BlockSpec((tm, n), lambda i: (i, 0)),
                  pl.BlockSpec((k, n), lambda i: (0, 0))],
        out_specs=pl.BlockSpec((tm, k), lambda i: (i, 0)),
        out_shape=jax.ShapeDtypeStruct((s, k), F32),
        compiler_params=_params("parallel"),
    )(dy, wg)


def matmul_nt_norm_backward(dz, wg, x, g, dres, name):
    s, d = x.shape
    n = wg.shape[-1]
    tm = _tile(s, 512)

    def body(dz_ref, w_ref, x_ref, g_ref, dres_ref, dx_ref, dg_ref):
        dx, dgp = _norm_backward(_dot_nt(dz_ref[...], w_ref[...]), x_ref[...], g_ref[...], dres_ref[...])
        dx_ref[...] = dx
        _accumulate(dg_ref, dgp, pl.program_id(0) == 0)

    return pl.pallas_call(
        body, name=name, grid=(s // tm,),
        in_specs=[pl.BlockSpec((tm, n), lambda i: (i, 0)),
                  pl.BlockSpec((d, n), lambda i: (0, 0)),
                  pl.BlockSpec((tm, d), lambda i: (i, 0)),
                  pl.BlockSpec((1, d), lambda i: (0, 0)),
                  pl.BlockSpec((tm, d), lambda i: (i, 0))],
        out_specs=[pl.BlockSpec((tm, d), lambda i: (i, 0)),
                   pl.BlockSpec((1, d), lambda i: (0, 0))],
        out_shape=[jax.ShapeDtypeStruct((s, d), F32), jax.ShapeDtypeStruct((1, d), F32)],
        compiler_params=_params("arbitrary"),
    )(dz, wg, x, g, dres)


def loss_and_grad(y, target, name):
    s, d = y.shape
    tm = _tile(s, 1024)

    def body(y_ref, t_ref, dy_ref, l_ref):
        e = y_ref[...] - t_ref[...]
        dy_ref[...] = e / d
        part = jnp.sum(e * e, axis=0, keepdims=True) * (0.5 / d)

        @pl.when(pl.program_id(0) == 0)
        def _():
            l_ref[...] = part

        @pl.when(pl.program_id(0) > 0)
        def _():
            l_ref[...] += part

    return pl.pallas_call(
        body, name=name, grid=(s // tm,),
        in_specs=[pl.BlockSpec((tm, d), lambda i: (i, 0)), pl.BlockSpec((tm, d), lambda i: (i, 0))],
        out_specs=[pl.BlockSpec((tm, d), lambda i: (i, 0)), pl.BlockSpec((1, d), lambda i: (0, 0))],
        out_shape=[jax.ShapeDtypeStruct((s, d), F32), jax.ShapeDtypeStruct((1, d), F32)],
        compiler_params=_params("arbitrary"),
    )(y, target)


def _layernorm(x, g, b):
    mu = jnp.mean(x, axis=-1, keepdims=True)
    xc = x - mu
    rstd = lax.rsqrt(jnp.mean(xc * xc, axis=-1, keepdims=True) + EPS)
    xn = xc * rstd
    return xn * g + b, xn, rstd


def _layernorm_backward(dy, xn, rstd, g):
    dxn = dy * g
    return rstd * (dxn - jnp.mean(dxn, axis=-1, keepdims=True) - xn * jnp.mean(dxn * xn, axis=-1, keepdims=True))


def _group_halves(x_ref, jp, nch):
    blk = jnp.concatenate([x_ref[c * CHUNK:(c + 1) * CHUNK, jp * LANES:(jp + 1) * LANES] for c in range(nch)], axis=1)
    low = (lax.broadcasted_iota(jnp.int32, blk.shape, 1) % LANES) < HEAD_DIM
    return jnp.where(low, blk, 0.0).astype(BF16), jnp.where(low, 0.0, blk).astype(BF16)


def _spatial_apply(src_ref, w_ref, dst_ref, nch, bias_ref=None):
    for jp in range(4):
        lo, hi = _group_halves(src_ref, jp, nch)
        r = _dot(w_ref[2 * jp], lo) + _dot(w_ref[2 * jp + 1], hi)
        for c in range(nch):
            v = r[:, c * LANES:(c + 1) * LANES]
            if bias_ref is not None:
                v = v + bias_ref[:, jp * LANES:(jp + 1) * LANES]
            dst_ref[c * CHUNK:(c + 1) * CHUNK, jp * LANES:(jp + 1) * LANES] = v


def _glu(zb):
    w = zb.shape[1] // 2
    return zb[:, :w] * _sigmoid(zb[:, w:])


def _fill_padded(pad_ref, prev, cur, nxt, i, nt, tm):
    pad_ref[0:CONV_HALO, :] = jnp.where(i > 0, prev, 0.0)
    pad_ref[CONV_HALO:CONV_HALO + tm, :] = cur
    pad_ref[CONV_HALO + tm:2 * CONV_HALO + tm, :] = jnp.where(i < nt - 1, nxt, 0.0)


def _halo_specs(tm, s, width, col):
    hb, nhb = tm // CONV_HALO, s // CONV_HALO
    return [pl.BlockSpec((tm, width), lambda i: (i, col)),
            pl.BlockSpec((CONV_HALO, width), lambda i: (jnp.maximum(i * hb - 1, 0), col)),
            pl.BlockSpec((CONV_HALO, width), lambda i: (jnp.minimum((i + 1) * hb, nhb - 1), col))]


def _const_spec(shape):
    nd = len(shape)
    return pl.BlockSpec(shape, lambda i: (0,) * nd)


SUBLANES = 8


def _shift_scratch(tm, width):
    return pltpu.VMEM((SUBLANES - 1, tm + 2 * CONV_HALO - SUBLANES, width), F32)


def _fill_shifts(sh_ref, pad_ref):
    rows = sh_ref.shape[1]
    for sft in range(1, SUBLANES):
        sh_ref[sft - 1] = pad_ref[pl.ds(sft, rows), :]


def _tap(pad_ref, sh_ref, offset, cols):
    sft = offset % SUBLANES
    rows = pl.ds(offset - sft, CHUNK)
    return pad_ref[rows, cols] if sft == 0 else sh_ref[sft - 1, rows, cols]


def ab_mid_forward(z, sp, name, comm=None):
    s = z.shape[0]
    aw = z.shape[1] // 4
    tm = _tile(s, 512)
    nch, nt = tm // CHUNK, s // tm

    def body(zu_ref, zv_ref, zb_ref, zp_ref, zn_ref, w_ref, bias_ref, vg_ref, vb_ref, cw_ref, cb_ref, cg_ref, cbn_ref,
             y_ref, gc_ref, vl_ref, sv_ref, pad_ref, sh_ref):
        i = pl.program_id(0)
        vl_ref[...] = _layernorm(_gelu(zv_ref[...]), vg_ref[...], vb_ref[...])[0]
        _spatial_apply(vl_ref, w_ref, sv_ref, nch, bias_ref)
        y_ref[:, :aw] = (_gelu(zu_ref[...]) * sv_ref[...]).astype(BF16)

        _fill_padded(pad_ref, _glu(zp_ref[...]), _glu(zb_ref[...]), _glu(zn_ref[...]), i, nt, tm)
        _fill_shifts(sh_ref, pad_ref)
        for rb in range(tm // CHUNK):
            for lb in range(aw // LANES):
                cols = pl.ds(lb * LANES, LANES)
                acc = jnp.broadcast_to(cb_ref[:, cols], (CHUNK, LANES))
                for k in range(CONV_W):
                    acc = acc + cw_ref[k:k + 1, cols] * _tap(pad_ref, sh_ref, rb * CHUNK + CONV_HALO - CONV_W // 2 + k, cols)
                gc_ref[rb * CHUNK:(rb + 1) * CHUNK, cols] = acc
        yl = _layernorm(gc_ref[...], cg_ref[...], cbn_ref[...])[0]
        y_ref[:, aw:] = (yl * _sigmoid(yl)).astype(BF16)

    return _pallas(
        body, name=name, grid=(nt,), comm=comm,
        args=[z, z, z, z, z, sp["w"], sp["bias"], sp["vg"], sp["vb"], sp["cw"], sp["cb"], sp["cg"], sp["cbn"]],
        in_specs=[pl.BlockSpec((tm, aw), lambda i: (i, 0)), pl.BlockSpec((tm, aw), lambda i: (i, 1))]
        + _halo_specs(tm, s, 2 * aw, 1)
        + [_const_spec(sp["w"].shape), _const_spec(sp["bias"].shape)]
        + [_const_spec((1, aw))] * 2 + [_const_spec(sp["cw"].shape)] + [_const_spec((1, aw))] * 3,
        out_specs=[pl.BlockSpec((tm, 2 * aw), lambda i: (i, 0)), pl.BlockSpec((tm, aw), lambda i: (i, 0))],
        out_shape=[jax.ShapeDtypeStruct((s, 2 * aw), BF16), jax.ShapeDtypeStruct((s, aw), F32)],
        scratch=[pltpu.VMEM((tm, aw), F32), pltpu.VMEM((tm, aw), F32), pltpu.VMEM((tm + 2 * CONV_HALO, aw), F32),
                 _shift_scratch(tm, aw)],
        sem=("parallel",))


def _accumulate(ref, val, first):
    @pl.when(first)
    def _():
        ref[...] = val

    @pl.when(jnp.logical_not(first))
    def _():
        ref[...] += val


def ab_mid_backward(dy, z, gc, sp, name):
    s = z.shape[0]
    aw = z.shape[1] // 4
    tm = _tile(s, 512)
    nch, nt = tm // CHUNK, s // tm

    def body(dya_ref, dyb_ref, zu_ref, zv_ref, gc_ref, w_ref, wt_ref, bias_ref, vg_ref, vb_ref, cg_ref, cbn_ref,
             dz_ref, dgc_ref, dw_ref, dsb_ref, dvg_ref, dvb_ref, dcg_ref, dcbn_ref, dcb_ref,
             vl_ref, sv_ref, dsv_ref, dvl_ref):
        first = pl.program_id(0) == 0
        zu, zv = zu_ref[...], zv_ref[...]
        u = _gelu(zu)
        vl, vn, vrstd = _layernorm(_gelu(zv), vg_ref[...], vb_ref[...])
        vl_ref[...] = vl
        _spatial_apply(vl_ref, w_ref, sv_ref, nch, bias_ref)
        dya = dya_ref[...]
        dz_ref[:, :aw] = (dya * sv_ref[...] * _gelu_grad(zu)).astype(BF16)
        dsv = dya * u
        dsv_ref[...] = dsv
        _spatial_apply(dsv_ref, wt_ref, dvl_ref, nch)

        for jp in range(4):
            dlo, dhi = _group_halves(dsv_ref, jp, nch)
            vlo, vhi = _group_halves(vl_ref, jp, nch)
            vall = vlo + vhi
            _accumulate(dw_ref.at[2 * jp], _dot_nt(dlo, vall), first)
            _accumulate(dw_ref.at[2 * jp + 1], _dot_nt(dhi, vall), first)
        rows = dsv[0:CHUNK]
        for c in range(1, nch):
            rows = rows + dsv[c * CHUNK:(c + 1) * CHUNK]
        grp = lax.broadcasted_iota(jnp.int32, (8, aw), 0) == lax.broadcasted_iota(jnp.int32, (8, aw), 1) // HEAD_DIM
        e = jnp.where(grp, 1.0, 0.0).astype(BF16)
        hi = rows.astype(BF16)
        r1 = rows - hi.astype(F32)
        mid = r1.astype(BF16)
        lo = (r1 - mid.astype(F32)).astype(BF16)
        _accumulate(dsb_ref, _dot_nt(e, hi) + _dot_nt(e, mid) + _dot_nt(e, lo), first)

        dvl = dvl_ref[...]
        _accumulate(dvg_ref, jnp.sum(dvl * vn, axis=0, keepdims=True), first)
        _accumulate(dvb_ref, jnp.sum(dvl, axis=0, keepdims=True), first)
        dz_ref[:, aw:] = (_layernorm_backward(dvl, vn, vrstd, vg_ref[...]) * _gelu_grad(zv)).astype(BF16)

        yl, yn, yrstd = _layernorm(gc_ref[...], cg_ref[...], cbn_ref[...])
        sg = _sigmoid(yl)
        dyl = dyb_ref[...] * (sg + yl * sg * (1.0 - sg))
        _accumulate(dcg_ref, jnp.sum(dyl * yn, axis=0, keepdims=True), first)
        _accumulate(dcbn_ref, jnp.sum(dyl, axis=0, keepdims=True), first)
        dgc = _layernorm_backward(dyl, yn, yrstd, cg_ref[...])
        dgc_ref[...] = dgc
        _accumulate(dcb_ref, jnp.sum(dgc, axis=0, keepdims=True), first)

    vec = jax.ShapeDtypeStruct((1, aw), F32)
    return pl.pallas_call(
        body, name=name, grid=(nt,),
        in_specs=[pl.BlockSpec((tm, aw), lambda i: (i, 0)), pl.BlockSpec((tm, aw), lambda i: (i, 1)),
                  pl.BlockSpec((tm, aw), lambda i: (i, 0)), pl.BlockSpec((tm, aw), lambda i: (i, 1)),
                  pl.BlockSpec((tm, aw), lambda i: (i, 0)),
                  _const_spec(sp["w"].shape), _const_spec(sp["w"].shape), _const_spec(sp["bias"].shape)]
        + [_const_spec((1, aw))] * 4,
        out_specs=[pl.BlockSpec((tm, 2 * aw), lambda i: (i, 0)), pl.BlockSpec((tm, aw), lambda i: (i, 0)),
                   _const_spec(sp["w"].shape), _const_spec((8, CHUNK))] + [_const_spec((1, aw))] * 5,
        out_shape=[jax.ShapeDtypeStruct((s, 4 * aw), BF16), jax.ShapeDtypeStruct((s, aw), F32),
                   jax.ShapeDtypeStruct(sp["w"].shape, F32), jax.ShapeDtypeStruct((8, CHUNK), F32)] + [vec] * 5,
        scratch_shapes=[pltpu.VMEM((tm, aw), F32)] * 4,
        compiler_params=_params("arbitrary"),
    )(dy, dy, z, z, gc, sp["w"], sp["wt"], sp["bias"], sp["vg"], sp["vb"], sp["cg"], sp["cbn"])


def conv_backward(dgc, z, dz_in, sp, name, comm=None):
    s = z.shape[0]
    aw = z.shape[1] // 4
    tm = _tile(s, 512)
    nt = s // tm
    off = CONV_HALO - CONV_W // 2

    def body(d_ref, dp_ref, dn_ref, zb_ref, zp_ref, zn_ref, cw_ref, dzin_ref, dz_ref, dcw_ref,
             padd_ref, padg_ref, dgg_ref, shd_ref, shg_ref):
        i = pl.program_id(0)
        _fill_padded(padd_ref, dp_ref[...], d_ref[...], dn_ref[...], i, nt, tm)
        _fill_padded(padg_ref, _glu(zp_ref[...]), _glu(zb_ref[...]), _glu(zn_ref[...]), i, nt, tm)
        _fill_shifts(shd_ref, padd_ref)
        _fill_shifts(shg_ref, padg_ref)

        @pl.when(i == 0)
        def _():
            dcw_ref[...] = jnp.zeros_like(dcw_ref)

        def grad_input(cols, rb):
            acc = jnp.zeros((CHUNK, LANES), F32)
            for k in range(CONV_W):
                acc = acc + cw_ref[k:k + 1, cols] * _tap(padd_ref, shd_ref, rb * CHUNK + CONV_HALO + CONV_W // 2 - k, cols)
            dgg_ref[rb * CHUNK:(rb + 1) * CHUNK, cols] = acc

        def grad_taps(cols, rb):
            dblk = d_ref[rb * CHUNK:(rb + 1) * CHUNK, cols]
            for k in range(CONV_W):
                prod = dblk * _tap(padg_ref, shg_ref, rb * CHUNK + off + k, cols)
                dcw_ref[k:k + 1, cols] += jnp.sum(prod, axis=0, keepdims=True)

        for lb in range(aw // LANES):
            for rb in range(tm // CHUNK):
                pl.when(i >= 0)(functools.partial(grad_input, pl.ds(lb * LANES, LANES), rb))
                pl.when(i >= 0)(functools.partial(grad_taps, pl.ds(lb * LANES, LANES), rb))

        zb = zb_ref[...]
        val, sg = zb[:, :aw], _sigmoid(zb[:, aw:])
        dgg = dgg_ref[...]
        dz_ref[:, :aw] = (dgg * sg).astype(BF16)
        dz_ref[:, aw:] = (dgg * val * sg * (1.0 - sg)).astype(BF16)

    return _pallas(
        body, name=name, grid=(nt,), args=[dgc, dgc, dgc, z, z, z, sp["cw"], dz_in], comm=comm,
        in_specs=_halo_specs(tm, s, aw, 0) + _halo_specs(tm, s, 2 * aw, 1)
        + [_const_spec(sp["cw"].shape), pl.BlockSpec(memory_space=pl.ANY)],
        out_specs=[pl.BlockSpec((tm, 2 * aw), lambda i: (i, 1)), _const_spec(sp["cw"].shape)],
        out_shape=[jax.ShapeDtypeStruct((s, 4 * aw), BF16), jax.ShapeDtypeStruct(sp["cw"].shape, F32)],
        scratch=[pltpu.VMEM((tm + 2 * CONV_HALO, aw), F32)] * 2 + [pltpu.VMEM((tm, aw), F32)]
        + [_shift_scratch(tm, aw)] * 2,
        aliases={7: 0}, sem=("arbitrary",))


def rope_tables(s):
    pos = jnp.arange(s, dtype=F32)
    inv_freq = ROPE_THETA ** (-jnp.arange(0, ROT_DIM, 2, dtype=F32) / ROT_DIM)
    ang = pos[:, None] * inv_freq[None, :]
    cos, sin = jnp.cos(ang), jnp.sin(ang)
    half = ROT_DIM // 2
    rest = HEAD_DIM - ROT_DIM
    one, zero, zrest = jnp.ones((s, rest), F32), jnp.zeros((s, half), F32), jnp.zeros((s, rest), F32)
    c = jnp.concatenate([cos, cos, one], axis=1)
    s1 = jnp.concatenate([-sin, zero, zrest], axis=1)
    s2 = jnp.concatenate([zero, sin, zrest], axis=1)
    return tuple(jnp.tile(t, (1, LANES // HEAD_DIM)) for t in (c, s1, s2))


def qk_prep_forward(qkv, tabs, gains, name, comm=None):
    s, w3 = qkv.shape
    w = w3 // 3
    tm = _tile(s, 512)

    def body(x_ref, c_ref, s1_ref, s2_ref, g_ref, o_ref):
        bd = _block_diag(LANES)
        for rb in range(tm // CHUNK):
            rows = pl.ds(rb * CHUNK, CHUNK)
            c, s1, s2 = c_ref[rows, :], s1_ref[rows, :], s2_ref[rows, :]
            for b in range(w // LANES):
                cols = pl.ds(b * LANES, LANES)
                t = x_ref[rows, cols]
                r = lax.rsqrt(_seg_sum(t * t, bd) * (1.0 / HEAD_DIM) + EPS)
                y = t * r * g_ref[...]
                o_ref[rows, cols] = (y * c + pltpu.roll(y, LANES - ROT_DIM // 2, 1) * s1
                                     + pltpu.roll(y, ROT_DIM // 2, 1) * s2)

    tab = pl.BlockSpec((tm, LANES), lambda i, p: (i, 0))
    outs, couts = _pallas(
        body, name=name, grid=(s // tm, 2), args=[qkv, *tabs, gains], comm=comm,
        in_specs=[pl.BlockSpec((tm, w), lambda i, p: (i, p)), tab, tab, tab,
                  pl.BlockSpec((None, 1, LANES), lambda i, p: (p, 0, 0))],
        out_specs=[pl.BlockSpec((tm, w), lambda i, p: (i, p))],
        out_shape=[jax.ShapeDtypeStruct((s, 2 * w), F32)],
        sem=("parallel", "arbitrary"))
    return outs[0], couts


def qk_prep_backward(dq, dq_prev, dq_next, dk, dv, qkv, tabs, gains, name):
    s, w3 = qkv.shape
    w = w3 // 3
    tm = _tile(s, 512)
    t = _tile(s, ATT_TILE)
    per_tile, per_halo, tiles = t // tm, ATT_HALO // tm, s // t

    def neighbours(i):
        tile, c = i // per_tile, i % per_tile
        from_before = (c < per_halo) & (tile >= 1)
        from_after = (c >= per_tile - per_halo) & (tile + 1 < tiles)
        return (from_before, from_after, jnp.where(from_before, (tile - 1) * per_halo + c, 0),
                jnp.where(from_after, (tile + 1) * per_halo + c - (per_tile - per_halo), 0))

    def body(*refs):
        grads = ((refs[0], refs[1], refs[2]), (refs[3],), (refs[4],))
        x_ref, c_ref, s1_ref, s2_ref, g_ref, o_ref, dg_ref = refs[5:]
        part, first = pl.program_id(0), pl.program_id(1) == 0
        from_before, from_after, _, _ = neighbours(pl.program_id(1))

        def normed(ds):
            bd = _block_diag(LANES)
            acc = jnp.zeros((1, LANES), F32)
            for rb in range(tm // CHUNK):
                rows = pl.ds(rb * CHUNK, CHUNK)
                c, s1, s2 = c_ref[rows, :], s1_ref[rows, :], s2_ref[rows, :]
                for b in range(w // LANES):
                    cols = pl.ds(b * LANES, LANES)
                    dout = ds[0][rows, cols]
                    if len(ds) == 3:
                        dout = (dout + jnp.where(from_after, ds[1][rows, cols], 0.0)
                                + jnp.where(from_before, ds[2][rows, cols], 0.0))
                    dy = (dout * c + pltpu.roll(dout * s1, ROT_DIM // 2, 1)
                          + pltpu.roll(dout * s2, LANES - ROT_DIM // 2, 1))
                    t = x_ref[rows, cols]
                    r = lax.rsqrt(_seg_sum(t * t, bd) * (1.0 / HEAD_DIM) + EPS)
                    xh = t * r
                    acc = acc + jnp.sum(dy * xh, axis=0, keepdims=True)
                    tt = dy * g_ref[...]
                    o_ref[rows, cols] = (r * (tt - xh * (_seg_sum(tt * xh, bd) * (1.0 / HEAD_DIM)))).astype(BF16)
            _accumulate(dg_ref, acc, first)

        for p in range(2):
            pl.when(part == p)(functools.partial(normed, grads[p]))

        @pl.when(part == 2)
        def _():
            o_ref[...] = grads[2][0][...].astype(BF16)
            _accumulate(dg_ref, jnp.zeros((1, LANES), F32), first)

    def gspec(p):
        return pl.BlockSpec((tm, w), lambda q, i: (jnp.where(q == p, i, 0), 0))

    prev_spec = pl.BlockSpec((tm, w), lambda q, i: (jnp.where(q == 0, neighbours(i)[3], 0), 0))
    next_spec = pl.BlockSpec((tm, w), lambda q, i: (jnp.where(q == 0, neighbours(i)[2], 0), 0))
    tab = pl.BlockSpec((tm, LANES), lambda q, i: (i, 0))
    return pl.pallas_call(
        body, name=name, grid=(3, s // tm),
        in_specs=[gspec(0), prev_spec, next_spec, gspec(1), gspec(2)]
        + [pl.BlockSpec((tm, w), lambda q, i: (i, q)), tab, tab, tab,
           pl.BlockSpec((None, 1, LANES), lambda q, i: (q, 0, 0))],
        out_specs=[pl.BlockSpec((tm, w), lambda q, i: (i, q)), pl.BlockSpec((None, 1, LANES), lambda q, i: (q, 0, 0))],
        out_shape=[jax.ShapeDtypeStruct((s, w3), BF16), jax.ShapeDtypeStruct((3, 1, LANES), F32)],
        compiler_params=_params("arbitrary", "arbitrary"),
    )(dq, dq_prev, dq_next, dk, dv, qkv, *tabs, gains)


def _window_specs(tq, l, col_fn):
    hb, nhb = tq // BAND, l // BAND
    return [pl.BlockSpec((tq, LANES), lambda c, i: (i, col_fn(c))),
            pl.BlockSpec((BAND, LANES), lambda c, i: (jnp.maximum(i * hb - 1, 0), col_fn(c))),
            pl.BlockSpec((BAND, LANES), lambda c, i: (jnp.minimum((i + 1) * hb, nhb - 1), col_fn(c)))]


def _window(cur_ref, prev_ref, next_ref):
    return jnp.concatenate([prev_ref[...], cur_ref[...], next_ref[...]], axis=0)


def _band_mask(shape, centre_axis, first_row, length):
    ctr = lax.broadcasted_iota(jnp.int32, shape, centre_axis)
    win = lax.broadcasted_iota(jnp.int32, shape, 1 - centre_axis)
    row = first_row + win
    return (jnp.abs(win - BAND - ctr) <= BAND) & (row >= 0) & (row < length)


def _col_q(d):
    return lambda c: (c // 8) * 24 + c % 8


def _col_k(d):
    return lambda c: (c // 8) * 24 + 8 + c % 8


def _col_v(d):
    return lambda c: (c // 8) * 24 + 16 + c % 8


def band_attention_forward(qkvn, d, name):
    s, w3 = qkvn.shape
    w = w3 // 3
    l = s // d
    tq = _tile(l, 512)
    scale = HEAD_DIM ** -0.5
    xv = qkvn.reshape(l, d * w3)

    def body(q_ref, k_ref, kp_ref, kn_ref, v_ref, vp_ref, vn_ref, o_ref, lse_ref):
        i = pl.program_id(1)
        kw, vw = _window(k_ref, kp_ref, kn_ref), _window(v_ref, vp_ref, vn_ref)
        head0 = lax.broadcasted_iota(jnp.int32, (CHUNK, LANES), 1) < HEAD_DIM
        for b in range(tq // CHUNK):
            rows = pl.ds(b * CHUNK, CHUNK)
            mask = _band_mask((CHUNK, 2 * CHUNK), 0, i * tq + b * CHUNK - BAND, l)
            qb = q_ref[rows, :]
            kb, vb = kw[b * CHUNK:(b + 2) * CHUNK], vw[b * CHUNK:(b + 2) * CHUNK]
            outs, lses = [], []
            for hm in (head0, jnp.logical_not(head0)):
                sc = jnp.where(mask, _dot_nt(jnp.where(hm, qb, jnp.zeros_like(qb)), kb) * scale, NEG)
                m = jnp.max(sc, axis=1, keepdims=True)
                p = jnp.exp(sc - m)
                den = jnp.sum(p, axis=1, keepdims=True)
                outs.append(_dot(p.astype(BF16), vb) / den)
                lses.append(jnp.broadcast_to(m + jnp.log(den), (CHUNK, LANES)))
            o_ref[rows, :] = jnp.where(head0, outs[0], outs[1]).astype(BF16)
            lse_ref[rows, :] = jnp.where(head0, lses[0], lses[1])

    ospec = pl.BlockSpec((tq, LANES), lambda c, i: (i, c))
    o, lse = pl.pallas_call(
        body, name=name, grid=(d * w // LANES, l // tq),
        in_specs=[pl.BlockSpec((tq, LANES), lambda c, i: (i, _col_q(d)(c)))]
        + _window_specs(tq, l, _col_k(d)) + _window_specs(tq, l, _col_v(d)),
        out_specs=[ospec, ospec],
        out_shape=[jax.ShapeDtypeStruct((l, d * w), BF16), jax.ShapeDtypeStruct((l, d * w), F32)],
        compiler_params=_params("parallel", "parallel"),
    )(xv, xv, xv, xv, xv, xv, xv)
    return o.reshape(s, w), lse.reshape(s, w)


def attention_merge(os_, lses, name):
    s, w = os_[0].shape
    tm = _tile(s, 512)

    def body(o0, o1, o2, l0, l1, l2, o_ref, lse_ref):
        la, lb, lc = l0[...], l1[...], l2[...]
        m = jnp.maximum(jnp.maximum(la, lb), lc)
        wa, wb, wc = jnp.exp(la - m), jnp.exp(lb - m), jnp.exp(lc - m)
        den = wa + wb + wc
        o = (wa * o0[...].astype(F32) + wb * o1[...].astype(F32) + wc * o2[...].astype(F32)) / den
        o_ref[...] = o.astype(BF16)
        lse_ref[...] = m + jnp.log(den)

    spec = pl.BlockSpec((tm, w), lambda i: (i, 0))
    return pl.pallas_call(
        body, name=name, grid=(s // tm,), in_specs=[spec] * 6, out_specs=[spec, spec],
        out_shape=[jax.ShapeDtypeStruct((s, w), BF16), jax.ShapeDtypeStruct((s, w), F32)],
        compiler_params=_params("parallel"),
    )(*os_, *lses)


def attention_delta(do, o, name):
    s, w = do.shape
    tm = _tile(s, 512)

    def body(do_ref, o_ref, dl_ref, dob_ref):
        bd = _block_diag(LANES)
        for b in range(w // LANES):
            cols = pl.ds(b * LANES, LANES)
            dv = do_ref[:, cols]
            dl_ref[:, cols] = _seg_sum(dv * o_ref[:, cols].astype(F32), bd)
            dob_ref[:, cols] = dv.astype(BF16)

    spec = pl.BlockSpec((tm, w), lambda i: (i, 0))
    return pl.pallas_call(
        body, name=name, grid=(s // tm,), in_specs=[spec, spec], out_specs=[spec, spec],
        out_shape=[jax.ShapeDtypeStruct((s, w), F32), jax.ShapeDtypeStruct((s, w), BF16)],
        compiler_params=_params("parallel"),
    )(do, o)


def band_attention_dq(qkvn, dob, lse, delta, d, name):
    s, w3 = qkvn.shape
    w = w3 // 3
    l = s // d
    tq = _tile(l, 512)
    scale = HEAD_DIM ** -0.5
    xv = qkvn.reshape(l, d * w3)

    def body(q_ref, k_ref, kp_ref, kn_ref, v_ref, vp_ref, vn_ref, do_ref, lse_ref, dl_ref, dq_ref):
        i = pl.program_id(1)
        kw, vw = _window(k_ref, kp_ref, kn_ref), _window(v_ref, vp_ref, vn_ref)
        head0 = lax.broadcasted_iota(jnp.int32, (CHUNK, LANES), 1) < HEAD_DIM
        for b in range(tq // CHUNK):
            rows = pl.ds(b * CHUNK, CHUNK)
            mask = _band_mask((CHUNK, 2 * CHUNK), 0, i * tq + b * CHUNK - BAND, l)
            qb, dob_ = q_ref[rows, :], do_ref[rows, :]
            kb, vb = kw[b * CHUNK:(b + 2) * CHUNK], vw[b * CHUNK:(b + 2) * CHUNK]
            outs = []
            for h, hm in enumerate((head0, jnp.logical_not(head0))):
                col = pl.ds(h * HEAD_DIM, 1)
                sc = jnp.where(mask, _dot_nt(jnp.where(hm, qb, jnp.zeros_like(qb)), kb) * scale, NEG)
                p = jnp.exp(sc - lse_ref[rows, col])
                dp = _dot_nt(jnp.where(hm, dob_, jnp.zeros_like(dob_)), vb)
                ds = p * (dp - dl_ref[rows, col]) * scale
                outs.append(_dot(ds.astype(BF16), kb))
            dq_ref[rows, :] = jnp.where(head0, outs[0], outs[1])

    ospec = pl.BlockSpec((tq, LANES), lambda c, i: (i, c))
    dq = pl.pallas_call(
        body, name=name, grid=(d * w // LANES, l // tq),
        in_specs=[pl.BlockSpec((tq, LANES), lambda c, i: (i, _col_q(d)(c)))]
        + _window_specs(tq, l, _col_k(d)) + _window_specs(tq, l, _col_v(d)) + [ospec, ospec, ospec],
        out_specs=ospec,
        out_shape=jax.ShapeDtypeStruct((l, d * w), F32),
        compiler_params=_params("parallel", "parallel"),
    )(xv, xv, xv, xv, xv, xv, xv, dob.reshape(l, d * w), lse.reshape(l, d * w), delta.reshape(l, d * w))
    return dq.reshape(s, w)


def band_attention_dkv(qkvn, dob, lse, delta, d, name):
    s, w3 = qkvn.shape
    w = w3 // 3
    l = s // d
    tq = _tile(l, 512)
    scale = HEAD_DIM ** -0.5
    xv = qkvn.reshape(l, d * w3)

    def body(k_ref, v_ref, q_ref, qp_ref, qn_ref, do_ref, dop_ref, don_ref, lse_ref, lsep_ref, lsen_ref,
             dl_ref, dlp_ref, dln_ref, dk_ref, dv_ref):
        i = pl.program_id(1)
        qw, dow = _window(q_ref, qp_ref, qn_ref), _window(do_ref, dop_ref, don_ref)
        lsew, dlw = _window(lse_ref, lsep_ref, lsen_ref), _window(dl_ref, dlp_ref, dln_ref)
        head0 = lax.broadcasted_iota(jnp.int32, (2 * CHUNK, LANES), 1) < HEAD_DIM
        for b in range(tq // CHUNK):
            rows = pl.ds(b * CHUNK, CHUNK)
            mask = _band_mask((2 * CHUNK, CHUNK), 1, i * tq + b * CHUNK - BAND, l)
            kb, vb = k_ref[rows, :], v_ref[rows, :]
            win = slice(b * CHUNK, (b + 2) * CHUNK)
            qb, dob_, lseb, dlb = qw[win], dow[win], lsew[win], dlw[win]
            dk = jnp.zeros((CHUNK, LANES), F32)
            dv = jnp.zeros((CHUNK, LANES), F32)
            for h, hm in enumerate((head0, jnp.logical_not(head0))):
                col = slice(h * HEAD_DIM, h * HEAD_DIM + 1)
                qm = jnp.where(hm, qb, jnp.zeros_like(qb))
                dom = jnp.where(hm, dob_, jnp.zeros_like(dob_))
                sc = jnp.where(mask, _dot_nt(qm, kb) * scale, NEG)
                p = jnp.exp(sc - lseb[:, col])
                ds = p * (_dot_nt(dom, vb) - dlb[:, col]) * scale
                dv = dv + _dot_tn(p.astype(BF16), dom)
                dk = dk + _dot_tn(ds.astype(BF16), qm)
            dk_ref[rows, :] = dk
            dv_ref[rows, :] = dv

    ident = lambda c: c
    ospec = pl.BlockSpec((tq, LANES), lambda c, i: (i, c))
    dk, dv = pl.pallas_call(
        body, name=name, grid=(d * w // LANES, l // tq),
        in_specs=[pl.BlockSpec((tq, LANES), lambda c, i: (i, _col_k(d)(c))),
                  pl.BlockSpec((tq, LANES), lambda c, i: (i, _col_v(d)(c)))]
        + _window_specs(tq, l, _col_q(d)) + _window_specs(tq, l, ident) * 3,
        out_specs=[ospec, ospec],
        out_shape=[jax.ShapeDtypeStruct((l, d * w), F32)] * 2,
        compiler_params=_params("parallel", "parallel"),
    )(xv, xv, xv, xv, xv, *[dob.reshape(l, d * w)] * 3, *[lse.reshape(l, d * w)] * 3, *[delta.reshape(l, d * w)] * 3)
    return dk.reshape(s, w), dv.reshape(s, w)


ATT_TILE = 2048
ATT_HALO = BAND * max(DILATIONS)
ROWS_PER_COPY = 256


def _att_specs(s, t, col_fn, halo=True):
    hb, nhb = t // ATT_HALO, s // ATT_HALO
    specs = [pl.BlockSpec((t, LANES), lambda hp, i: (i, col_fn(hp)))]
    if halo:
        specs += [pl.BlockSpec((ATT_HALO, LANES), lambda hp, i: (jnp.maximum(i * hb - 1, 0), col_fn(hp))),
                  pl.BlockSpec((ATT_HALO, LANES), lambda hp, i: (jnp.minimum((i + 1) * hb, nhb - 1), col_fn(hp)))]
    return specs


def _gather_rows(dst_ref, dst_row, src_ref, start, count, stride, scale=None):
    for c in range(0, count, ROWS_PER_COPY):
        m = min(ROWS_PER_COPY, count - c)
        v = src_ref[pl.ds(start + c * stride, m, stride=stride), :]
        if scale is not None:
            v = v * scale
        dst_ref[dst_row + c:dst_row + c + m, :] = v.astype(dst_ref.dtype)


SPLIT = 4


def _split_rows(tmp_ref, base, src_ref, rows):
    part = rows // SPLIT
    for b in range(SPLIT):
        _gather_rows(tmp_ref, base + b * part, src_ref, b, part, SPLIT)


def _stage(dst_ref, cur_ref, d, t, scale=None, tmp_ref=None):
    n = t // d
    if tmp_ref is None or d != SPLIT * SPLIT:
        for r in range(d):
            _gather_rows(dst_ref, r * n, cur_ref, r, n, d, scale)
        return
    _split_rows(tmp_ref, 0, cur_ref, t)
    for r in range(d):
        _gather_rows(dst_ref, r * n, tmp_ref, (r % SPLIT) * (t // SPLIT) + r // SPLIT, n, SPLIT, scale)


def _stage_window(dst_ref, refs, d, t, scale=None, edges=None, tmp_ref=None):
    cur_ref, prev_ref, next_ref = refs
    n = t // d
    nw = n + 2 * BAND
    if tmp_ref is None or d != SPLIT * SPLIT:
        for r in range(d):
            _gather_rows(dst_ref, r * nw, prev_ref, ATT_HALO - BAND * d + r, BAND, d, scale)
            _gather_rows(dst_ref, r * nw + BAND, cur_ref, r, n, d, scale)
            _gather_rows(dst_ref, r * nw + BAND + n, next_ref, r, BAND, d, scale)
    else:
        assert ATT_HALO == BAND * d
        _split_rows(tmp_ref, 0, cur_ref, t)
        _split_rows(tmp_ref, t, prev_ref, ATT_HALO)
        _split_rows(tmp_ref, t + ATT_HALO, next_ref, ATT_HALO)
        for r in range(d):
            a, b = r // SPLIT, r % SPLIT
            _gather_rows(dst_ref, r * nw, tmp_ref, t + b * (ATT_HALO // SPLIT) + a, BAND, SPLIT, scale)
            _gather_rows(dst_ref, r * nw + BAND, tmp_ref, b * (t // SPLIT) + a, n, SPLIT, scale)
            _gather_rows(dst_ref, r * nw + BAND + n, tmp_ref, t + ATT_HALO + b * (ATT_HALO // SPLIT) + a, BAND, SPLIT, scale)
    if edges is not None:
        first, last, value = edges
        fill = jnp.full((BAND, LANES), value, dst_ref.dtype)

        @pl.when(first)
        def _():
            for r in range(d):
                dst_ref[r * nw:r * nw + BAND, :] = fill

        @pl.when(last)
        def _():
            for r in range(d):
                dst_ref[r * nw + BAND + n:(r + 1) * nw, :] = fill


def _band_bias(rows, cols, centre_axis):
    shape = (rows // 2, cols)
    ctr = lax.broadcasted_iota(jnp.int32, shape, centre_axis)
    win = lax.broadcasted_iota(jnp.int32, shape, 1 - centre_axis)
    bias = jnp.where(jnp.abs(win - BAND - ctr) <= BAND, 0.0, NEG).astype(F32)
    return jnp.concatenate([bias, bias], axis=0)


def _window_bias(first_row, length):
    row = first_row + lax.broadcasted_iota(jnp.int32, (1, 2 * CHUNK), 1)
    return jnp.where((row >= 0) & (row < length), 0.0, NEG).astype(F32)


UNITS_PER_TRIP = 8


def _scatter_rows(dst_ref, src_ref, d, t, combine):
    n = t // d
    for r in range(d):
        for c in range(0, n, ROWS_PER_COPY):
            m = min(ROWS_PER_COPY, n - c)
            idx = pl.ds(r + c * d, m, stride=d)
            combine(idx, slice(r * n + c, r * n + c + m))


def _unit_rows(u, n):
    upr = n // CHUNK
    r = u // upr
    b = u - r * upr
    if isinstance(u, int):
        return u * CHUNK, (u + r) * CHUNK, b * CHUNK - BAND
    return pl.multiple_of(u * CHUNK, CHUNK), pl.multiple_of((u + r) * CHUNK, CHUNK), b * CHUNK - BAND


def _two_heads(x):
    head0 = lax.broadcasted_iota(jnp.int32, x.shape, 1) < HEAD_DIM
    zero = jnp.zeros_like(x)
    return jnp.concatenate([jnp.where(head0, x, zero), jnp.where(head0, zero, x)], axis=0)


def _head_columns(x):
    return jnp.concatenate([x[:, 0:1], x[:, HEAD_DIM:HEAD_DIM + 1]], axis=0)


def _merge_heads(x2):
    rows = x2.shape[0] // 2
    head0 = lax.broadcasted_iota(jnp.int32, (rows, LANES), 1) < HEAD_DIM
    return jnp.where(head0, jnp.broadcast_to(x2[:rows], (rows, LANES)), jnp.broadcast_to(x2[rows:], (rows, LANES)))


def _col(part):
    return lambda hp: part * 8 + hp


def attention_forward(qk, qkv, name, comm=None):
    s, w2 = qk.shape
    w = w2 // 2
    t = _tile(s, ATT_TILE)
    scale = HEAD_DIM ** -0.5

    def body(q_ref, k_ref, kp_ref, kn_ref, v_ref, vp_ref, vn_ref, o_ref, lse_ref,
             qs_ref, ks_ref, vs_ref, os_ref, ls_ref, or_ref, tmp_ref):
        i = pl.program_id(1)
        band = _band_bias(2 * CHUNK, 2 * CHUNK, 0)
        for pi, d in enumerate(DILATIONS):
            n = t // d
            _stage(qs_ref, q_ref, d, t, scale, tmp_ref=tmp_ref)
            _stage_window(ks_ref, (k_ref, kp_ref, kn_ref), d, t, tmp_ref=tmp_ref)
            _stage_window(vs_ref, (v_ref, vp_ref, vn_ref), d, t, tmp_ref=tmp_ref)

            for u in range(t // CHUNK):
                qrow, wrow, first = _unit_rows(u, n)
                kb, vb = ks_ref[pl.ds(wrow, 2 * CHUNK), :], vs_ref[pl.ds(wrow, 2 * CHUNK), :]
                sc = _dot_nt(_two_heads(qs_ref[pl.ds(qrow, CHUNK), :]), kb) + band
                if first < 0 or first + 2 * CHUNK > n:
                    sc = sc + _window_bias(i * n + first, s // d)
                m = jnp.max(sc, axis=1, keepdims=True)
                p = jnp.exp(sc - m)
                den = jnp.sum(p, axis=1, keepdims=True)
                os_ref[pl.ds(qrow, CHUNK), :] = _merge_heads(_dot(p.astype(BF16), vb) / den)
                ls_ref[pl.ds(qrow, CHUNK), :] = _merge_heads(m + jnp.log(den))

            if pi == 0:
                def assign(idx, rows):
                    or_ref[idx, :] = os_ref[rows, :]
                    lse_ref[idx, :] = ls_ref[rows, :]
                _scatter_rows(None, None, d, t, assign)
            else:
                def merge(idx, rows):
                    la, lb = lse_ref[idx, :], ls_ref[rows, :]
                    mx = jnp.maximum(la, lb)
                    wa, wb = jnp.exp(la - mx), jnp.exp(lb - mx)
                    den = wa + wb
                    or_ref[idx, :] = (wa * or_ref[idx, :] + wb * os_ref[rows, :]) / den
                    lse_ref[idx, :] = mx + jnp.log(den)
                _scatter_rows(None, None, d, t, merge)
        o_ref[...] = or_ref[...].astype(BF16)

    ospec = pl.BlockSpec((t, LANES), lambda hp, i: (i, hp))
    win_rows = t + 2 * ATT_HALO
    return _pallas(
        body, name=name, grid=(w // LANES, s // t), args=[qk, qk, qk, qk, qkv, qkv, qkv], comm=comm,
        in_specs=_att_specs(s, t, _col(0), halo=False) + _att_specs(s, t, _col(1)) + _att_specs(s, t, _col(2)),
        out_specs=[ospec, ospec],
        out_shape=[jax.ShapeDtypeStruct((s, w), BF16), jax.ShapeDtypeStruct((s, w), F32)],
        scratch=[pltpu.VMEM((t, LANES), BF16), pltpu.VMEM((win_rows, LANES), BF16), pltpu.VMEM((win_rows, LANES), BF16),
                 pltpu.VMEM((t, LANES), F32), pltpu.VMEM((t, LANES), F32), pltpu.VMEM((t, LANES), F32),
                 pltpu.VMEM((win_rows, LANES), F32)],
        sem=("parallel", "parallel"))


def attention_delta(do, o, lse, name):
    s, w = do.shape
    tm = _tile(s, 512)

    def body(do_ref, o_ref, lse_ref, st_ref):
        bd = _block_diag(LANES)
        lane = lax.broadcasted_iota(jnp.int32, (CHUNK, LANES), 1)
        for rb in range(tm // CHUNK):
            rows = pl.ds(rb * CHUNK, CHUNK)
            for b in range(w // LANES):
                cols = pl.ds(b * LANES, LANES)
                dl = _seg_sum(do_ref[rows, cols] * o_ref[rows, cols].astype(F32), bd)
                ls = lse_ref[rows, cols]
                st_ref[rows, cols] = jnp.where(lane % HEAD_DIM < HEAD_DIM // 2, ls, dl)

    spec = pl.BlockSpec((tm, w), lambda i: (i, 0))
    return pl.pallas_call(
        body, name=name, grid=(s // tm,), in_specs=[spec, spec, spec], out_specs=spec,
        out_shape=jax.ShapeDtypeStruct((s, w), F32), compiler_params=_params("parallel"),
    )(do, o, lse)


def attention_dq(qk, qkv, do, lse, delta, name, comm=None):
    s, w2 = qk.shape
    w = w2 // 2
    t = _tile(s, ATT_TILE)
    scale = HEAD_DIM ** -0.5

    def body(q_ref, k_ref, kp_ref, kn_ref, v_ref, vp_ref, vn_ref, do_ref, lse_ref, dl_ref, dq_ref,
             qs_ref, ks_ref, vs_ref, dos_ref, ls_ref, dls_ref, dqs_ref):
        i = pl.program_id(1)
        band = _band_bias(2 * CHUNK, 2 * CHUNK, 0)
        for pi, d in enumerate(DILATIONS):
            n = t // d
            _stage(qs_ref, q_ref, d, t, scale)
            _stage(dos_ref, do_ref, d, t)
            _stage(ls_ref, lse_ref, d, t)
            _stage(dls_ref, dl_ref, d, t)
            _stage_window(ks_ref, (k_ref, kp_ref, kn_ref), d, t)
            _stage_window(vs_ref, (v_ref, vp_ref, vn_ref), d, t)

            def unit(u, carry, n=n, d=d):
                qrow, wrow, first = _unit_rows(u, n)
                rows = pl.ds(qrow, CHUNK)
                kb, vb = ks_ref[pl.ds(wrow, 2 * CHUNK), :], vs_ref[pl.ds(wrow, 2 * CHUNK), :]
                sc = _dot_nt(_two_heads(qs_ref[rows, :]), kb) + band + _window_bias(i * n + first, s // d)
                p = jnp.exp(sc - _head_columns(ls_ref[rows, :]))
                dp = _dot_nt(_two_heads(dos_ref[rows, :]), vb)
                ds = p * (dp - _head_columns(dls_ref[rows, :]))
                dqs_ref[rows, :] = _merge_heads(_dot(ds.astype(BF16), kb)) * scale
                return carry

            lax.fori_loop(0, t // CHUNK, unit, 0, unroll=UNITS_PER_TRIP)

            def add(idx, rows, pi=pi):
                dq_ref[idx, :] = dqs_ref[rows, :] if pi == 0 else dq_ref[idx, :] + dqs_ref[rows, :]
            _scatter_rows(None, None, d, t, add)

    ospec = pl.BlockSpec((t, LANES), lambda hp, i: (i, hp))
    win_rows = t + 2 * ATT_HALO
    return _pallas(
        body, name=name, grid=(w // LANES, s // t), args=[qk, qk, qk, qk, qkv, qkv, qkv, do, lse, delta], comm=comm,
        in_specs=_att_specs(s, t, _col(0), halo=False) + _att_specs(s, t, _col(1)) + _att_specs(s, t, _col(2))
        + [ospec, ospec, ospec],
        out_specs=[ospec],
        out_shape=[jax.ShapeDtypeStruct((s, w), F32)],
        scratch=[pltpu.VMEM((t, LANES), BF16), pltpu.VMEM((win_rows, LANES), BF16), pltpu.VMEM((win_rows, LANES), BF16),
                 pltpu.VMEM((t, LANES), BF16), pltpu.VMEM((t, LANES), F32), pltpu.VMEM((t, LANES), F32),
                 pltpu.VMEM((t, LANES), F32)],
        sem=("parallel", "parallel"))


def attention_dkv(qk, qkv, do, stats, name):
    s, w2 = qk.shape
    w = w2 // 2
    t = _tile(s, ATT_TILE)
    scale = HEAD_DIM ** -0.5

    def body(k_ref, v_ref, q_ref, qp_ref, qn_ref, do_ref, dop_ref, don_ref, st_ref, stp_ref, stn_ref,
             dk_ref, dv_ref, ks_ref, vs_ref, qs_ref, dos_ref, sts_ref, dks_ref, dvs_ref):
        i = pl.program_id(1)
        key = lax.broadcasted_iota(jnp.int32, (CHUNK, 2 * CHUNK), 0)
        win_ = lax.broadcasted_iota(jnp.int32, (CHUNK, 2 * CHUNK), 1)
        half = jnp.where(jnp.abs(win_ - BAND - key) <= BAND, 0.0, NEG).astype(F32)
        band = jnp.concatenate([half, half], axis=1)
        edges = (i == 0, i == s // t - 1, -NEG)
        for pi, d in enumerate(DILATIONS):
            n = t // d
            _stage(ks_ref, k_ref, d, t)
            _stage(vs_ref, v_ref, d, t)
            _stage_window(qs_ref, (q_ref, qp_ref, qn_ref), d, t, scale)
            _stage_window(dos_ref, (do_ref, dop_ref, don_ref), d, t)
            _stage_window(sts_ref, (st_ref, stp_ref, stn_ref), d, t, edges=edges)

            def unit(u, carry, n=n, d=d):
                krow, wrow, _ = _unit_rows(u, n)
                rows, win = pl.ds(krow, CHUNK), pl.ds(wrow, 2 * CHUNK)
                q2, do2 = _two_heads(qs_ref[win, :]), _two_heads(dos_ref[win, :])
                st = jnp.transpose(sts_ref[win, :])
                lse2 = jnp.concatenate([st[0:1, :], st[1:2, :]], axis=1)
                dl2 = jnp.concatenate([st[2:3, :], st[3:4, :]], axis=1)
                p = jnp.exp(_dot_nt(ks_ref[rows, :], q2) + band - lse2)
                ds = p * (_dot_nt(vs_ref[rows, :], do2) - dl2)
                dvs_ref[rows, :] = _dot(p.astype(BF16), do2)
                dks_ref[rows, :] = _dot(ds.astype(BF16), q2)
                return carry

            lax.fori_loop(0, t // CHUNK, unit, 0, unroll=UNITS_PER_TRIP)

            def add(idx, rows, pi=pi):
                dk_ref[idx, :] = dks_ref[rows, :] if pi == 0 else dk_ref[idx, :] + dks_ref[rows, :]
                dv_ref[idx, :] = dvs_ref[rows, :] if pi == 0 else dv_ref[idx, :] + dvs_ref[rows, :]
            _scatter_rows(None, None, d, t, add)

    ident = lambda hp: hp
    ospec = pl.BlockSpec((t, LANES), lambda hp, i: (i, hp))
    win_rows = t + 2 * ATT_HALO
    return pl.pallas_call(
        body, name=name, grid=(w // LANES, s // t),
        in_specs=_att_specs(s, t, _col(1), halo=False) + _att_specs(s, t, _col(2), halo=False)
        + _att_specs(s, t, _col(0)) + _att_specs(s, t, ident) * 2,
        out_specs=[ospec, ospec],
        out_shape=[jax.ShapeDtypeStruct((s, w), F32)] * 2,
        scratch_shapes=[pltpu.VMEM((t, LANES), BF16), pltpu.VMEM((t, LANES), BF16),
                        pltpu.VMEM((win_rows, LANES), BF16), pltpu.VMEM((win_rows, LANES), BF16),
                        pltpu.VMEM((win_rows, LANES), F32),
                        pltpu.VMEM((t, LANES), F32), pltpu.VMEM((t, LANES), F32)],
        compiler_params=_params("parallel", "parallel"),
    )(qk, qkv, qk, qk, qk, do, do, do, stats, stats, stats)


def attention_backward(qk, qkv, do, stats, name, comm=None):
    s, w2 = qk.shape
    w = w2 // 2
    t = _tile(s, ATT_TILE)
    tiles = s // t
    scale = HEAD_DIM ** -0.5

    def body(k_ref, v_ref, q_ref, qp_ref, qn_ref, do_ref, dop_ref, don_ref, st_ref, stp_ref, stn_ref,
             dk_ref, dv_ref, dq_ref, dqp_ref, dqn_ref, ks_ref, vs_ref, qs_ref, dos_ref, sts_ref, dks_ref, dvs_ref, dqw_ref):
        i = pl.program_id(1)
        key = lax.broadcasted_iota(jnp.int32, (CHUNK, 2 * CHUNK), 0)
        win_ = lax.broadcasted_iota(jnp.int32, (CHUNK, 2 * CHUNK), 1)
        half = jnp.where(jnp.abs(win_ - BAND - key) <= BAND, 0.0, NEG).astype(F32)
        band = jnp.concatenate([half, half], axis=1)
        edges = (i == 0, i == tiles - 1, -NEG)
        dqp_ref[...] = jnp.zeros_like(dqp_ref)
        dqn_ref[...] = jnp.zeros_like(dqn_ref)
        for pi, d in enumerate(DILATIONS):
            n = t // d
            nw = n + 2 * BAND
            _stage(ks_ref, k_ref, d, t, tmp_ref=dqw_ref)
            _stage(vs_ref, v_ref, d, t, tmp_ref=dqw_ref)
            _stage_window(qs_ref, (q_ref, qp_ref, qn_ref), d, t, scale, tmp_ref=dqw_ref)
            _stage_window(dos_ref, (do_ref, dop_ref, don_ref), d, t, tmp_ref=dqw_ref)
            _stage_window(sts_ref, (st_ref, stp_ref, stn_ref), d, t, edges=edges, tmp_ref=dqw_ref)

            for u in range(t // CHUNK):
                krow, wrow, first = _unit_rows(u, n)
                rows, win = pl.ds(krow, CHUNK), pl.ds(wrow, 2 * CHUNK)
                kb = ks_ref[rows, :]
                q2, do2 = _two_heads(qs_ref[win, :]), _two_heads(dos_ref[win, :])
                st = jnp.transpose(sts_ref[win, :])
                mid = HEAD_DIM // 2
                lse2 = jnp.concatenate([st[0:1, :], st[HEAD_DIM:HEAD_DIM + 1, :]], axis=1)
                dl2 = jnp.concatenate([st[mid:mid + 1, :], st[HEAD_DIM + mid:HEAD_DIM + mid + 1, :]], axis=1)
                p = jnp.exp(_dot_nt(kb, q2) + band - lse2)
                ds = (p * (_dot_nt(vs_ref[rows, :], do2) - dl2)).astype(BF16)
                dvs_ref[rows, :] = _dot(p.astype(BF16), do2)
                dks_ref[rows, :] = _dot(ds, q2)
                k2 = _two_heads(kb)
                dqw = (_dot_tn(ds[:, :2 * CHUNK], k2[:CHUNK]) + _dot_tn(ds[:, 2 * CHUNK:], k2[CHUNK:])) * scale
                if first < 0:
                    dqw_ref[pl.ds(wrow, CHUNK), :] = dqw[:CHUNK]
                else:
                    dqw_ref[pl.ds(wrow, CHUNK), :] += dqw[:CHUNK]
                dqw_ref[pl.ds(wrow + CHUNK, CHUNK), :] = dqw[CHUNK:]

            def add(idx, rows, pi=pi):
                dk_ref[idx, :] = dks_ref[rows, :] if pi == 0 else dk_ref[idx, :] + dks_ref[rows, :]
                dv_ref[idx, :] = dvs_ref[rows, :] if pi == 0 else dv_ref[idx, :] + dvs_ref[rows, :]
            _scatter_rows(None, None, d, t, add)

            for r in range(d):
                before = pl.ds(ATT_HALO - BAND * d + r, BAND, stride=d)
                after = pl.ds(r, BAND, stride=d)
                dqp_ref[before, :] += dqw_ref[r * nw:r * nw + BAND, :]
                dqn_ref[after, :] += dqw_ref[r * nw + BAND + n:(r + 1) * nw, :]
                for c in range(0, n, ROWS_PER_COPY):
                    m = min(ROWS_PER_COPY, n - c)
                    idx = pl.ds(r + c * d, m, stride=d)
                    val = dqw_ref[r * nw + BAND + c:r * nw + BAND + c + m, :]
                    dq_ref[idx, :] = val if pi == 0 else dq_ref[idx, :] + val

    ident = lambda hp: hp
    ospec = pl.BlockSpec((t, LANES), lambda hp, i: (i, hp))
    hspec = pl.BlockSpec((ATT_HALO, LANES), lambda hp, i: (i, hp))
    win_rows = t + 2 * ATT_HALO
    halo = jax.ShapeDtypeStruct((tiles * ATT_HALO, w), F32)
    return _pallas(
        body, name=name, grid=(w // LANES, tiles), comm=comm,
        args=[qk, qkv, qk, qk, qk, do, do, do, stats, stats, stats],
        in_specs=_att_specs(s, t, _col(1), halo=False) + _att_specs(s, t, _col(2), halo=False)
        + _att_specs(s, t, _col(0)) + _att_specs(s, t, ident) * 2,
        out_specs=[ospec, ospec, ospec, hspec, hspec],
        out_shape=[jax.ShapeDtypeStruct((s, w), F32)] * 3 + [halo, halo],
        scratch=[pltpu.VMEM((t, LANES), BF16), pltpu.VMEM((t, LANES), BF16),
                 pltpu.VMEM((win_rows, LANES), BF16), pltpu.VMEM((win_rows, LANES), BF16),
                 pltpu.VMEM((win_rows, LANES), F32),
                 pltpu.VMEM((t, LANES), F32), pltpu.VMEM((t, LANES), F32), pltpu.VMEM((win_rows, LANES), F32)],
        sem=("parallel", "parallel"))


def adamw_update(recvs, w, m, v, name, comm=None):
    nl = len(recvs)
    r, c = recvs[0].shape[1:]
    tr = 256 if (r > 256 and r % 256 == 0) else r
    nt = r // tr
    c1 = 1.0 - ADAM_B1 ** ADAM_STEP
    c2 = 1.0 - ADAM_B2 ** ADAM_STEP

    def body(*refs):
        g_refs = refs[:nl]
        w_ref, m_ref, v_ref, go_ref, d_ref, mo_ref, vo_ref = refs[nl:]

        def update(g_ref):
            g = g_ref[0].astype(F32)
            for j in range(1, N_DEV):
                g = g + g_ref[j].astype(F32)
            mn = ADAM_B1 * m_ref[...] + (1.0 - ADAM_B1) * g
            vn = ADAM_B2 * v_ref[...] + (1.0 - ADAM_B2) * (g * g)
            go_ref[...] = g
            mo_ref[...] = mn
            vo_ref[...] = vn
            d_ref[...] = -ADAM_LR * ((mn / c1) / (jnp.sqrt(vn / c2) + ADAM_EPS) + ADAM_WD * w_ref[...])

        for layer in range(nl):
            pl.when(pl.program_id(0) == layer)(functools.partial(update, g_refs[layer]))

    def gspec(layer):
        return pl.BlockSpec((N_DEV, tr, c), lambda l, i: (0, jnp.where(l == layer, i, 0), 0))

    spec = pl.BlockSpec((tr, c), lambda l, i: (l * nt + i, 0))
    return _pallas(
        body, name=name, grid=(nl, nt), args=[*recvs, w, m, v], comm=comm,
        in_specs=[gspec(layer) for layer in range(nl)] + [spec, spec, spec],
        out_specs=[spec] * 4, out_shape=[jax.ShapeDtypeStruct((nl * r, c), F32)] * 4,
        sem=("arbitrary", "arbitrary"))


BIG = ("mlp_w1", "mlp_w2", "ab_w_in", "ab_w_out", "c_w_qkv", "c_w_out")
SMALL = ("mix_norm_g", "mlp_norm_g", "a_spatial_w", "a_spatial_b", "a_vnorm_g", "a_vnorm_b", "b_conv_b", "b_norm_g",
         "b_norm_b", "c_q_norm_g", "c_k_norm_g")
WEIGHTS = ("mix_norm_g", "mlp_norm_g", "mlp_w1", "mlp_w2", "ab_w_in", "a_spatial_w", "a_spatial_b", "a_vnorm_g",
           "a_vnorm_b", "b_conv_w", "b_conv_b", "b_norm_g", "b_norm_b", "ab_w_out", "c_w_qkv", "c_q_norm_g",
           "c_k_norm_g", "c_w_out")


def _mixer_params(p, conv_full, i):
    aw = p["a_vnorm_g"].shape[1]
    row = lambda t: t[i][None, :]
    return dict(
        w=p["a_spatial_w"][i].astype(BF16), wt=jnp.swapaxes(p["a_spatial_w"][i], 1, 2).astype(BF16),
        bias=jnp.repeat(p["a_spatial_b"][i].T, aw // p["a_spatial_b"].shape[1], axis=1),
        vg=row(p["a_vnorm_g"]), vb=row(p["a_vnorm_b"]), cw=jnp.pad(conv_full[i], ((0, 1), (0, 0))),
        cb=row(p["b_conv_b"]), cg=row(p["b_norm_g"]), cbn=row(p["b_norm_b"]))


def _head_gains(p, i):
    rep = LANES // HEAD_DIM
    return jnp.stack([jnp.tile(p["c_q_norm_g"][i], rep), jnp.tile(p["c_k_norm_g"][i], rep),
                      jnp.ones((LANES,), F32)])[:, None, :]


def _unused_forward_backward(x, target, p, wg, conv_full):
    s, d = x.shape
    depth = p["mix_norm_g"].shape[0]
    tabs = rope_tables(s)
    saved = []
    for l in range(depth):
        i = l // 2
        mix_g, mlp_g = p["mix_norm_g"][l][None, :], p["mlp_norm_g"][l][None, :]
        st = dict(x_in=x)
        if l % 2 == 0:
            sp = _mixer_params(p, conv_full, i)
            z, h = norm_matmul(x, mix_g, wg["ab_w_in"], i, f"ab_in_{l}")
            ycat, gc = ab_mid_forward(z, sp, f"ab_mid_{l}")
            x = matmul_residual(x, ycat, wg["ab_w_out"], i, f"ab_out_{l}")
            st.update(z=z, h=h, y=ycat, gc=gc, sp=sp)
        else:
            gains = _head_gains(p, i)
            qkv, h = norm_matmul(x, mix_g, wg["c_w_qkv"], i, f"c_qkv_{l}")
            qkvn = qk_prep_forward(qkv, tabs, gains, f"c_prep_{l}")
            outs = [band_attention_forward(qkvn, dil, f"c_attn_{l}_d{dil}") for dil in DILATIONS]
            o, lse = attention_merge([t[0] for t in outs], [t[1] for t in outs], f"c_merge_{l}")
            x = matmul_residual(x, o, wg["c_w_out"], i, f"c_out_{l}")
            st.update(qkv=qkv, h=h, qkvn=qkvn, y=o, lse=lse, gains=gains)
        st["x_mid"] = x
        x, a, h2 = mlp_forward(x, mlp_g, wg["mlp_w1"], wg["mlp_w2"], l, f"mlp_{l}")
        st.update(a=a, h2=h2)
        saved.append(st)

    dy, loss_part = loss_and_grad(x, target, "loss")

    big = {n: [None] * p_len for n, p_len in (("mlp_w1", depth), ("mlp_w2", depth), ("ab_w_in", depth // 2 + depth % 2),
                                              ("ab_w_out", depth // 2 + depth % 2), ("c_w_qkv", depth // 2),
                                              ("c_w_out", depth // 2))}
    small = {n: [None] * p[n].shape[0] for n in SMALL}
    conv_grads = [None] * p["b_conv_b"].shape[0]
    for l in reversed(range(depth)):
        i, st = l // 2, saved[l]
        mix_g, mlp_g = p["mix_norm_g"][l][None, :], p["mlp_norm_g"][l][None, :]
        dxm, da, dg = mlp_backward(dy, st["a"], st["x_mid"], mlp_g, wg["mlp_w1"], wg["mlp_w2"], l, f"mlp_bwd_{l}")
        small["mlp_norm_g"][l] = dg[0]
        big["mlp_w1"][l] = matmul_tn(st["h2"], da, "cols", f"mlp_dw1_{l}")
        big["mlp_w2"][l] = matmul_tn(st["a"], dy, "rows", f"mlp_dw2_{l}", relu2=True)
        if l % 2 == 0:
            sp = st["sp"]
            dycat = matmul_nt(dxm, wg["ab_w_out"], i, f"ab_out_bwd_{l}")
            big["ab_w_out"][i] = matmul_tn(st["y"], dxm, "rows", f"ab_dwout_{l}")
            dz, dgc, dw, dsb, dvg, dvb, dcg, dcbn, dcb = ab_mid_backward(dycat, st["z"], st["gc"], sp, f"ab_mid_bwd_{l}")
            dz, dcw = conv_backward(dgc, st["z"], dz, sp, f"ab_conv_bwd_{l}")
            small["a_spatial_w"][i], small["a_spatial_b"][i] = dw, dsb
            small["a_vnorm_g"][i], small["a_vnorm_b"][i] = dvg[0], dvb[0]
            small["b_norm_g"][i], small["b_norm_b"][i], small["b_conv_b"][i] = dcg[0], dcbn[0], dcb[0]
            conv_grads[i] = dcw[:CONV_W]
            dy, dg = matmul_nt_norm_backward(dz, wg["ab_w_in"], i, st["x_in"], mix_g, dxm, f"ab_in_bwd_{l}")
            big["ab_w_in"][i] = matmul_tn(st["h"], dz, "cols", f"ab_dwin_{l}")
        else:
            do = matmul_nt(dxm, wg["c_w_out"], i, f"c_out_bwd_{l}")
            big["c_w_out"][i] = matmul_tn(st["y"], dxm, "rows", f"c_dwout_{l}")
            delta, dob = attention_delta(do, st["y"], f"c_delta_{l}")
            dqs, dks, dvs = [], [], []
            for dil in DILATIONS:
                dqs.append(band_attention_dq(st["qkvn"], dob, st["lse"], delta, dil, f"c_attn_dq_{l}_d{dil}"))
                dk, dv = band_attention_dkv(st["qkvn"], dob, st["lse"], delta, dil, f"c_attn_dkv_{l}_d{dil}")
                dks.append(dk)
                dvs.append(dv)
            dqkv, dgn = qk_prep_backward(dqs, dks, dvs, st["qkv"], tabs, st["gains"], f"c_prep_bwd_{l}")
            small["c_q_norm_g"][i] = dgn[0, 0, :HEAD_DIM] + dgn[0, 0, HEAD_DIM:]
            small["c_k_norm_g"][i] = dgn[1, 0, :HEAD_DIM] + dgn[1, 0, HEAD_DIM:]
            dy, dg = matmul_nt_norm_backward(dqkv, wg["c_w_qkv"], i, st["x_in"], mix_g, dxm, f"c_qkv_bwd_{l}")
            big["c_w_qkv"][i] = matmul_tn(st["h"], dqkv, "cols", f"c_dwqkv_{l}")
        small["mix_norm_g"][l] = dg[0]
    small = {n: jnp.stack(v) for n, v in small.items()}
    return loss_part, dy, big, small, jnp.stack(conv_grads)


PACKED = tuple(n for n in SMALL if n != "a_spatial_w")


def _pack(d):
    flat = jnp.concatenate([d[n].reshape(-1) for n in PACKED])
    rows = -(-flat.shape[0] // (8 * LANES)) * 8
    return jnp.pad(flat, (0, rows * LANES - flat.shape[0])).reshape(rows, LANES)


def _unpack(packed, like):
    flat, out, pos = packed.reshape(-1), {}, 0
    for n in PACKED:
        size = math.prod(like[n].shape)
        out[n] = flat[pos:pos + size].reshape(like[n].shape)
        pos += size
    return out


def _unused_kernel(x, mix_norm_g, mlp_norm_g, mlp_w1, mlp_w2, ab_w_in, a_spatial_w, a_spatial_b, a_vnorm_g, a_vnorm_b, b_conv_w, b_conv_b, b_norm_g, b_norm_b, ab_w_out, c_w_qkv, c_q_norm_g, c_k_norm_g, c_w_out, loss_target, m_mix_norm_g, m_mlp_norm_g, m_mlp_w1, m_mlp_w2, m_ab_w_in, m_a_spatial_w, m_a_spatial_b, m_a_vnorm_g, m_a_vnorm_b, m_b_conv_w, m_b_conv_b, m_b_norm_g, m_b_norm_b, m_ab_w_out, m_c_w_qkv, m_c_q_norm_g, m_c_k_norm_g, m_c_w_out, v_mix_norm_g, v_mlp_norm_g, v_mlp_w1, v_mlp_w2, v_ab_w_in, v_a_spatial_w, v_a_spatial_b, v_a_vnorm_g, v_a_vnorm_b, v_b_conv_w, v_b_conv_b, v_b_norm_g, v_b_norm_b, v_ab_w_out, v_c_w_qkv, v_c_q_norm_g, v_c_k_norm_g, v_c_w_out):
    args = dict(locals())
    w = {n: args[n] for n in WEIGHTS}
    m = {n: args["m_" + n] for n in WEIGHTS}
    v = {n: args["v_" + n] for n in WEIGHTS}

    wg = {n: all_gather(w[n].astype(BF16), "gather_" + n) for n in BIG}
    for n in ("ab_w_out", "c_w_out"):
        t = wg[n]
        wg[n] = t.reshape(t.shape[0], t.shape[1] * t.shape[2], t.shape[3])
    conv = all_gather(w["b_conv_w"], "gather_b_conv_w")
    conv_full = jnp.swapaxes(conv, 1, 2).reshape(conv.shape[0], conv.shape[2], N_DEV * conv.shape[3])

    loss_part, dx, big, small, conv_grad = forward_backward(x[0], loss_target[0], w, wg, conv_full)
    loss = lax.psum(jnp.sum(loss_part), ("x", "y", "c"))

    grads, deltas, new_m, new_v = {}, {}, {}, {}

    def update(n, recv):
        shape = w[n].shape
        flat = lambda t: t.reshape(-1, shape[-1])
        outs = adamw_update(recv.reshape((N_DEV, -1, shape[-1])), flat(w[n]), flat(m[n]), flat(v[n]), "adamw_" + n)
        grads[n], deltas[n], new_m[n], new_v[n] = (t.reshape(shape) for t in outs)

    for n in BIG:
        update(n, exchange(big[n], "exchange_" + n))
    nl, kw, cw = conv_grad.shape
    conv_parts = jnp.transpose(conv_grad.reshape(nl, kw, N_DEV, cw // N_DEV), (2, 0, 1, 3))
    update("b_conv_w", exchange([conv_parts], "exchange_b_conv_w"))

    packed = all_gather(_pack(small)[None], "gather_small_grads")[0]
    outs = adamw_update(packed, _pack(w), _pack(m), _pack(v), "adamw_small")
    for dst, t in zip((grads, deltas, new_m, new_v), outs):
        dst.update(_unpack(t, w))

    return (loss, dx[None], *[grads[n] for n in WEIGHTS], *[deltas[n] for n in WEIGHTS],
            *[new_m[n] for n in WEIGHTS], *[new_v[n] for n in WEIGHTS])


class Traffic:
    def __init__(self, shards, full=()):
        self.shards, self.w, self.queue, self.parts = shards, dict(full), [], {}

    def run(self, fn, *args, gather=(), send=False, **kw):
        operands, flags, dest = [], [], []
        if self.shards is None:
            if send:
                self.parts.update(self.queue)
                self.queue = []
        else:
            for k in gather:
                if k not in self.w:
                    operands.append(self.shards[k])
                    flags.append(("gather", SHARD_AXIS.get(k[0])))
                    dest.append((self.w, k))
            if send:
                for k, t in self.queue:
                    operands.append(t)
                    flags.append(("scatter", SHARD_AXIS.get(k[0])))
                    dest.append((self.parts, k))
                self.queue = []
        outs, couts = fn(*args, comm=PeerCopies(operands, flags) if operands else None, **kw)
        for (table, k), t in zip(dest, couts):
            table[k] = t
        return outs

    def flush(self, name, extra=()):
        self.queue += list(extra)
        self.run(lambda comm: ([], run_copies(comm, name) if comm is not None else []), send=True)


SHARD_AXIS = {"mlp_w1": 1, "mlp_w2": 0, "ab_w_in": 1, "ab_w_out": 0, "c_w_qkv": 1, "c_w_out": 0}


def forward_backward(x, target, p, tr, conv_full):
    s, d = x.shape
    depth = p["mix_norm_g"].shape[0]
    tabs = rope_tables(s)
    saved = []
    for l in range(depth):
        i = l // 2
        mix_g, mlp_g = p["mix_norm_g"][l][None, :], p["mlp_norm_g"][l][None, :]
        st = dict(x_in=x)
        nxt = () if l + 1 == depth else ((("c_w_qkv", i), ("c_w_out", i)) if l % 2 == 0 else
                                        (("ab_w_in", i + 1), ("ab_w_out", i + 1)))
        if l % 2 == 0:
            sp = _mixer_params(p, conv_full, i)
            z, h = tr.run(norm_matmul, x, mix_g, tr.w["ab_w_in", i], f"ab_in_{l}", gather=[("mlp_w1", l)])
            ycat, gc = tr.run(ab_mid_forward, z, sp, f"ab_mid_{l}", gather=[("mlp_w2", l)])
            x = matmul_residual(x, ycat, tr.w["ab_w_out", i], f"ab_out_{l}")
            st.update(z=z, h=h, y=ycat, gc=gc, sp=sp)
        else:
            gains = _head_gains(p, i)
            qkv, h = norm_matmul(x, mix_g, tr.w["c_w_qkv", i], f"c_qkv_{l}")[0]
            qk = qk_prep_forward(qkv, tabs, gains, f"c_prep_{l}")[0]
            ahead = [("mlp_w1", l), ("mlp_w2", l)] + ([("mlp_w1", l + 1)] if l + 1 < depth else [])
            o, lse = tr.run(attention_forward, qk, qkv, f"c_attn_{l}", gather=ahead)
            x = matmul_residual(x, o, tr.w["c_w_out", i], f"c_out_{l}")
            st.update(qkv=qkv, h=h, qk=qk, y=o, lse=lse, gains=gains)
            nxt = nxt + ((("mlp_w2", l + 1),) if l + 1 < depth else ())
        st["x_mid"] = x
        x, a, h2 = tr.run(mlp_forward, x, mlp_g, tr.w["mlp_w1", l], tr.w["mlp_w2", l], f"mlp_{l}", gather=nxt)
        st.update(a=a, h2=h2)
        saved.append(st)

    dy, loss_part = loss_and_grad(x, target, "loss")

    def tn(*a, comm, **kw):
        out, couts = matmul_tn(*a, comm=comm, **kw)
        return [out], couts

    small = {n: [None] * p[n].shape[0] for n in SMALL}
    conv_grads = [None] * p["b_conv_b"].shape[0]
    for l in reversed(range(depth)):
        i, st = l // 2, saved[l]
        mix_g, mlp_g = p["mix_norm_g"][l][None, :], p["mlp_norm_g"][l][None, :]
        w1, w2 = tr.w["mlp_w1", l], tr.w["mlp_w2", l]
        dxm, da, dg, dyb = tr.run(mlp_backward, dy, st["a"], st["x_mid"], mlp_g, w1, w2, f"mlp_bwd_{l}", send=True)
        small["mlp_norm_g"][l] = dg[0]
        tr.queue.append((("mlp_w1", l), matmul_tn(st["h2"], da, f"mlp_dw1_{l}", n_split=2)[0]))
        tr.queue.append((("mlp_w2", l), matmul_tn(st["a"], dyb, f"mlp_dw2_{l}", m_split=4, relu2=True)[0]))
        if l % 2 == 0:
            sp = st["sp"]
            wout = tr.w["ab_w_out", i]
            dycat = matmul_nt(dxm, wout, f"ab_out_bwd_{l}")
            tr.queue.append((("ab_w_out", i), matmul_tn(st["y"], dxm, f"ab_dwout_{l}")[0]))
            dz, dgc, dw, dsb, dvg, dvb, dcg, dcbn, dcb = ab_mid_backward(dycat, st["z"], st["gc"], sp, f"ab_mid_bwd_{l}")
            dz, dcw = tr.run(conv_backward, dgc, st["z"], dz, sp, f"ab_conv_bwd_{l}", send=True)
            small["a_spatial_w"][i], small["a_spatial_b"][i] = dw, dsb
            small["a_vnorm_g"][i], small["a_vnorm_b"][i] = dvg[0], dvb[0]
            small["b_norm_g"][i], small["b_norm_b"][i], small["b_conv_b"][i] = dcg[0], dcbn[0], dcb[0]
            conv_grads[i] = dcw[:CONV_W]
            dy, dg = matmul_nt_norm_backward(dz, tr.w["ab_w_in", i], st["x_in"], mix_g, dxm, f"ab_in_bwd_{l}")
            small["mix_norm_g"][l] = dg[0]
            last = []
            if l == 0 and tr.shards is not None:
                tr.shards["small_grads", 0] = _pack({n: jnp.stack(small[n]) for n in PACKED})
                tr.shards["spatial_grads", 0] = jnp.stack(small["a_spatial_w"]).astype(BF16)
                last = [("small_grads", 0), ("spatial_grads", 0)]
            dwin = tr.run(tn, st["h"], dz, f"ab_dwin_{l}", send=True, gather=last)[0]
            tr.queue.append((("ab_w_in", i), dwin))
        else:
            wout = tr.w["c_w_out", i]
            do = matmul_nt(dxm, wout, f"c_out_bwd_{l}")
            tr.queue.append((("c_w_out", i), matmul_tn(st["y"], dxm, f"c_dwout_{l}")[0]))
            stats = attention_delta(do, st["y"], st["lse"], f"c_delta_{l}")
            dk, dv, dq, dqp, dqn = tr.run(attention_backward, st["qk"], st["qkv"], do, stats, f"c_attn_bwd_{l}", send=True)
            dqkv, dgn = qk_prep_backward(dq, dqp, dqn, dk, dv, st["qkv"], tabs, st["gains"], f"c_prep_bwd_{l}")
            small["c_q_norm_g"][i] = dgn[0, 0, :HEAD_DIM] + dgn[0, 0, HEAD_DIM:]
            small["c_k_norm_g"][i] = dgn[1, 0, :HEAD_DIM] + dgn[1, 0, HEAD_DIM:]
            dy, dg = matmul_nt_norm_backward(dqkv, tr.w["c_w_qkv", i], st["x_in"], mix_g, dxm, f"c_qkv_bwd_{l}")
            small["mix_norm_g"][l] = dg[0]
            tr.queue.append((("c_w_qkv", i), matmul_tn(st["h"], dqkv, f"c_dwqkv_{l}", n_split=2)[0]))
    small = {n: jnp.stack(v) for n, v in small.items()}
    return loss_part, dy, small, jnp.stack(conv_grads)


def kernel(x, mix_norm_g, mlp_norm_g, mlp_w1, mlp_w2, ab_w_in, a_spatial_w, a_spatial_b, a_vnorm_g, a_vnorm_b, b_conv_w, b_conv_b, b_norm_g, b_norm_b, ab_w_out, c_w_qkv, c_q_norm_g, c_k_norm_g, c_w_out, loss_target, m_mix_norm_g, m_mlp_norm_g, m_mlp_w1, m_mlp_w2, m_ab_w_in, m_a_spatial_w, m_a_spatial_b, m_a_vnorm_g, m_a_vnorm_b, m_b_conv_w, m_b_conv_b, m_b_norm_g, m_b_norm_b, m_ab_w_out, m_c_w_qkv, m_c_q_norm_g, m_c_k_norm_g, m_c_w_out, v_mix_norm_g, v_mlp_norm_g, v_mlp_w1, v_mlp_w2, v_ab_w_in, v_a_spatial_w, v_a_spatial_b, v_a_vnorm_g, v_a_vnorm_b, v_b_conv_w, v_b_conv_b, v_b_norm_g, v_b_norm_b, v_ab_w_out, v_c_w_qkv, v_c_q_norm_g, v_c_k_norm_g, v_c_w_out):
    args = dict(locals())
    w = {n: args[n] for n in WEIGHTS}
    m = {n: args["m_" + n] for n in WEIGHTS}
    v = {n: args["v_" + n] for n in WEIGHTS}

    shards = {(n, l): w[n][l].astype(BF16) for n in BIG for l in range(w[n].shape[0])}
    shards["b_conv_w", 0] = w["b_conv_w"]
    tr = Traffic(shards)
    first = [("ab_w_in", 0), ("ab_w_out", 0), ("b_conv_w", 0)]
    tr.run(lambda comm: ([], run_copies(comm, "gather_first")), gather=first)
    conv = tr.w["b_conv_w", 0]
    conv_full = jnp.transpose(conv, (1, 2, 0, 3)).reshape(conv.shape[1], conv.shape[2], -1)

    loss_part, dx, small, conv_grad = forward_backward(x[0], loss_target[0], w, tr, conv_full)
    loss = lax.psum(jnp.sum(loss_part), ("x", "y", "c"))
    nl, kw, cw = conv_grad.shape
    conv_parts = jnp.transpose(conv_grad.reshape(nl, kw, N_DEV, cw // N_DEV), (2, 0, 1, 3))
    tr.queue.append((("b_conv_w", 0), conv_parts))

    grads, deltas, new_m, new_v = {}, {}, {}, {}

    def update(n, recvs):
        shape = w[n].shape
        flat = lambda t: t.reshape(-1, shape[-1])
        recvs = [t.reshape(N_DEV, -1, shape[-1]) for t in recvs]
        outs = tr.run(adamw_update, recvs, flat(w[n]), flat(m[n]), flat(v[n]), "adamw_" + n, send=True)
        grads[n], deltas[n], new_m[n], new_v[n] = (t.reshape(shape) for t in outs)

    for n in BIG:
        update(n, [tr.parts[n, l] for l in range(w[n].shape[0])])
    update("b_conv_w", [tr.parts["b_conv_w", 0]])
    update("a_spatial_w", [tr.w["spatial_grads", 0]])
    outs = adamw_update([tr.w["small_grads", 0]], _pack(w), _pack(m), _pack(v), "adamw_small")[0]
    for dst, t in zip((grads, deltas, new_m, new_v), outs):
        dst.update(_unpack(t, w))

    return (loss, dx[None], *[grads[n] for n in WEIGHTS], *[deltas[n] for n in WEIGHTS],
            *[new_m[n] for n in WEIGHTS], *[new_v[n] for n in WEIGHTS])
```

```python
import functools
import math

import jax
import jax.numpy as jnp
from jax import lax
from jax.experimental import pallas as pl
from jax.experimental.pallas import tpu as pltpu

F32, BF16 = jnp.float32, jnp.bfloat16
N_DEV = 8
EPS = 1e-6
NEG = -1e30
LANES = 128
HEAD_DIM = 64
CHUNK = 128
CONV_W = 31
CONV_HALO = 16
BAND = 64
DILATIONS = (1, 4, 16)
ROT_DIM = 16
ROPE_THETA = 500000.0
VMEM_LIMIT = 56 * 1024 * 1024
MLP_CHUNK = 1024
MLP_BWD_CHUNK = 512
MLP_BWD_ROWS = 1024
ADAM_LR, ADAM_B1, ADAM_B2, ADAM_EPS, ADAM_WD, ADAM_STEP = 0.001, 0.9, 0.999, 1e-08, 0.01, 10
MESH = pl.DeviceIdType.MESH


def _params(*sem):
    return pltpu.CompilerParams(dimension_semantics=sem, vmem_limit_bytes=VMEM_LIMIT)


def _dot(a, b):
    return jnp.dot(a, b, preferred_element_type=F32)


def _dot_nt(a, b):
    return lax.dot_general(a, b, (((1,), (1,)), ((), ())), preferred_element_type=F32)


def _dot_tn(a, b):
    return lax.dot_general(a, b, (((0,), (0,)), ((), ())), preferred_element_type=F32)


def _rms_r(x):
    return lax.rsqrt(jnp.mean(x * x, axis=-1, keepdims=True) + EPS)


def _sigmoid(x):
    return 1.0 / (1.0 + jnp.exp(-x))


_GK = math.sqrt(2.0 / math.pi)


def _gelu(x):
    return 0.5 * x * (1.0 + jnp.tanh(_GK * (x + 0.044715 * x * x * x)))


def _gelu_grad(x):
    t = jnp.tanh(_GK * (x + 0.044715 * x * x * x))
    return 0.5 * (1.0 + t) + 0.5 * x * (1.0 - t * t) * (_GK * (1.0 + 3.0 * 0.044715 * x * x))


def _seg_sum(x, bd):
    hi = x.astype(BF16)
    lo = (x - hi.astype(F32)).astype(BF16)
    return _dot(hi, bd) + _dot(lo, bd)


def _block_diag(n):
    i = lax.broadcasted_iota(jnp.int32, (n, n), 0) // HEAD_DIM
    j = lax.broadcasted_iota(jnp.int32, (n, n), 1) // HEAD_DIM
    return jnp.where(i == j, 1.0, 0.0).astype(BF16)


def _tile(s, cap):
    t = min(s, cap)
    assert s % t == 0
    return t


def _my_index():
    return 4 * lax.axis_index("x") + 2 * lax.axis_index("y") + lax.axis_index("c")


def _device(i):
    return (i // 4, (i // 2) % 2, i % 2)


_HBM = pl.BlockSpec(memory_space=pl.ANY)


class PeerCopies:
    def __init__(self, operands, modes):
        self.inputs, self.modes = list(operands), list(modes)
        self.out_shape = []
        for t, (kind, axis) in zip(self.inputs, self.modes):
            shape = list(t.shape)
            if kind == "gather":
                shape = [N_DEV] + shape if axis is None else shape[:axis] + [N_DEV * shape[axis]] + shape[axis + 1:]
            elif axis is not None:
                shape = [N_DEV] + shape[:axis] + [shape[axis] // N_DEV] + shape[axis + 1:]
            self.out_shape.append(jax.ShapeDtypeStruct(tuple(shape), t.dtype))
        n = len(self.inputs)
        self.scratch = [pltpu.SemaphoreType.DMA((n, N_DEV - 1)), pltpu.SemaphoreType.DMA((n, N_DEV - 1)),
                        pltpu.SemaphoreType.DMA((n,))]

    @staticmethod
    def _block(ref, axis, size, j):
        if axis is None:
            return ref.at[j]
        return ref.at[tuple([slice(None)] * axis + [pl.ds(j * size, size)])]

    def _copies(self, in_refs, out_refs, sems, arrivals):
        send_sems, recv_sems, local_sems = sems
        me = _my_index()
        local, sends, recvs = [], [], []
        for t, (src, dst) in enumerate(zip(in_refs, out_refs)):
            kind, axis = self.modes[t]
            if kind == "gather":
                size = None if axis is None else src.shape[axis]
                source = lambda j, src=src: src
                place = lambda j, dst=dst, axis=axis, size=size: self._block(dst, axis, size, j)
            else:
                size = None if axis is None else src.shape[axis] // N_DEV
                source = lambda j, src=src, axis=axis, size=size: self._block(src, axis, size, j)
                place = lambda j, dst=dst: dst.at[j]
            local.append(pltpu.make_async_copy(source(me), place(me), local_sems.at[t]))
            for k in range(N_DEV - 1):
                to, frm = (me + k + 1) % N_DEV, (me + N_DEV - k - 1) % N_DEV
                sends.append(pltpu.make_async_remote_copy(
                    src_ref=source(to), dst_ref=place(me), send_sem=send_sems.at[t, k], recv_sem=recv_sems.at[t, k],
                    device_id=_device(to), device_id_type=MESH))
                if arrivals:
                    recvs.append(pltpu.make_async_remote_copy(
                        src_ref=source(me), dst_ref=place(frm), send_sem=send_sems.at[t, k], recv_sem=recv_sems.at[t, k],
                        device_id=_device(frm), device_id_type=MESH))
        return local, sends, recvs

    def start(self, in_refs, out_refs, sems):
        local, sends, _ = self._copies(in_refs, out_refs, sems, False)
        for cp in local + sends:
            cp.start()

    def finish(self, in_refs, out_refs, sems):
        local, sends, recvs = self._copies(in_refs, out_refs, sems, True)
        for cp in recvs:
            cp.wait_recv()
        for cp in sends:
            cp.wait_send()
        for cp in local:
            cp.wait()


def _pallas(body, *, name, args, in_specs, out_specs, out_shape, grid=(), scratch=(), sem=(), comm=None, aliases=None):
    n_in, n_out, n_scr = len(args), len(out_shape), len(scratch)
    if comm is None:
        outs = pl.pallas_call(
            body, name=name, grid=grid, in_specs=in_specs, out_specs=out_specs, out_shape=out_shape,
            scratch_shapes=list(scratch), input_output_aliases=aliases or {}, compiler_params=_params(*sem))(*args)
        return list(outs), []
    ci, co = len(comm.inputs), len(comm.out_shape)

    def hosted(*refs):
        ins, cins = refs[:n_in], refs[n_in:n_in + ci]
        outs, couts = refs[n_in + ci:n_in + ci + n_out], refs[n_in + ci + n_out:n_in + ci + n_out + co]
        rest = refs[n_in + ci + n_out + co:]
        scr, sems = rest[:n_scr], rest[n_scr:]
        if not grid:
            comm.start(cins, couts, sems)
            comm.finish(cins, couts, sems)
            return
        first = last = None
        for axis, size in enumerate(grid):
            f, l = pl.program_id(axis) == 0, pl.program_id(axis) == size - 1
            first, last = (f, l) if first is None else (first & f, last & l)
        pl.when(first)(lambda: comm.start(cins, couts, sems))
        body(*ins, *outs, *scr)
        pl.when(last)(lambda: comm.finish(cins, couts, sems))

    outs = pl.pallas_call(
        hosted, name=name, grid=grid, in_specs=list(in_specs) + [_HBM] * ci, out_specs=list(out_specs) + [_HBM] * co,
        out_shape=list(out_shape) + comm.out_shape, scratch_shapes=list(scratch) + comm.scratch,
        input_output_aliases=aliases or {}, compiler_params=_params(*["arbitrary"] * len(grid)))(*args, *comm.inputs)
    return list(outs[:n_out]), list(outs[n_out:])


def run_copies(comm, name):
    return _pallas(None, name=name, args=[], in_specs=[], out_specs=[], out_shape=[], comm=comm)[1]


def norm_matmul(x, g, wg, name, comm=None):
    s, d = x.shape
    n = wg.shape[-1]
    ns = 1024 if n % 1024 == 0 else n // 4
    tm = _tile(s, 1024)

    def body(x_ref, g_ref, w_ref, z_ref, h_ref):
        @pl.when(pl.program_id(1) == 0)
        def _():
            xv = x_ref[...]
            h_ref[...] = (xv * _rms_r(xv) * g_ref[...]).astype(BF16)
        z_ref[...] = _dot(h_ref[...], w_ref[...])

    return _pallas(
        body, name=name, grid=(s // tm, n // ns), args=[x, g, wg], comm=comm,
        in_specs=[pl.BlockSpec((tm, d), lambda i, j: (i, 0)),
                  pl.BlockSpec((1, d), lambda i, j: (0, 0)),
                  pl.BlockSpec((d, ns), lambda i, j: (0, j))],
        out_specs=[pl.BlockSpec((tm, ns), lambda i, j: (i, j)),
                   pl.BlockSpec((tm, d), lambda i, j: (i, 0))],
        out_shape=[jax.ShapeDtypeStruct((s, n), F32), jax.ShapeDtypeStruct((s, d), BF16)],
        sem=("parallel", "arbitrary"))


def mlp_forward(x, g, w1g, w2g, name, comm=None):
    s, d = x.shape
    f = w1g.shape[-1]
    fs = MLP_CHUNK
    tm = _tile(s, 1024)

    def body(x_ref, g_ref, w1_ref, w2_ref, xo_ref, a_ref, h_ref):
        @pl.when(pl.program_id(1) == 0)
        def _():
            xv = x_ref[...]
            h_ref[...] = (xv * _rms_r(xv) * g_ref[...]).astype(BF16)
            xo_ref[...] = xv
        a = _dot(h_ref[...], w1_ref[...])
        a_ref[...] = a.astype(BF16)
        r = jnp.maximum(a, 0.0)
        xo_ref[...] += _dot((r * r).astype(BF16), w2_ref[...])

    return _pallas(
        body, name=name, grid=(s // tm, f // fs), args=[x, g, w1g, w2g], comm=comm,
        in_specs=[pl.BlockSpec((tm, d), lambda i, j: (i, 0)),
                  pl.BlockSpec((1, d), lambda i, j: (0, 0)),
                  pl.BlockSpec((d, fs), lambda i, j: (0, j)),
                  pl.BlockSpec((fs, d), lambda i, j: (j, 0))],
        out_specs=[pl.BlockSpec((tm, d), lambda i, j: (i, 0)),
                   pl.BlockSpec((tm, fs), lambda i, j: (i, j)),
                   pl.BlockSpec((tm, d), lambda i, j: (i, 0))],
        out_shape=[jax.ShapeDtypeStruct((s, d), F32), jax.ShapeDtypeStruct((s, f), BF16),
                   jax.ShapeDtypeStruct((s, d), BF16)],
        sem=("parallel", "arbitrary"))


def _norm_backward(dh, xv, g, dres):
    r = _rms_r(xv)
    xh = xv * r
    t = dh * g
    dx = dres + r * (t - xh * jnp.mean(t * xh, axis=-1, keepdims=True))
    return dx, jnp.sum(dh * xh, axis=0, keepdims=True)


def mlp_backward(dy, a, x, g, w1g, w2g, name, comm=None):
    s, d = x.shape
    f = w1g.shape[-1]
    fs = MLP_BWD_CHUNK
    tm = _tile(s, MLP_BWD_ROWS)

    def body(dy_ref, a_ref, x_ref, g_ref, w1_ref, w2_ref, dx_ref, da_ref, dg_ref, dyb_ref, dh_ref):
        i, j = pl.program_id(0), pl.program_id(1)

        @pl.when(j == 0)
        def _():
            dyb_ref[...] = dy_ref[...].astype(BF16)
            dh_ref[...] = jnp.zeros_like(dh_ref)

        dr = _dot_nt(dyb_ref[...], w2_ref[...])
        da = (dr * (2.0 * jnp.maximum(a_ref[...].astype(F32), 0.0))).astype(BF16)
        da_ref[...] = da
        dh_ref[...] += _dot_nt(da, w1_ref[...])

        @pl.when(j == f // fs - 1)
        def _():
            dx, dgp = _norm_backward(dh_ref[...], x_ref[...], g_ref[...], dy_ref[...])
            dx_ref[...] = dx

            @pl.when(i == 0)
            def _():
                dg_ref[...] = dgp

            @pl.when(i > 0)
            def _():
                dg_ref[...] += dgp

    return _pallas(
        body, name=name, grid=(s // tm, f // fs), args=[dy, a, x, g, w1g, w2g], comm=comm,
        in_specs=[pl.BlockSpec((tm, d), lambda i, j: (i, 0)),
                  pl.BlockSpec((tm, fs), lambda i, j: (i, j)),
                  pl.BlockSpec((tm, d), lambda i, j: (i, 0)),
                  pl.BlockSpec((1, d), lambda i, j: (0, 0)),
                  pl.BlockSpec((d, fs), lambda i, j: (0, j)),
                  pl.BlockSpec((fs, d), lambda i, j: (j, 0))],
        out_specs=[pl.BlockSpec((tm, d), lambda i, j: (i, 0)),
                   pl.BlockSpec((tm, fs), lambda i, j: (i, j)),
                   pl.BlockSpec((1, d), lambda i, j: (0, 0)),
                   pl.BlockSpec((tm, d), lambda i, j: (i, 0))],
        out_shape=[jax.ShapeDtypeStruct((s, d), F32), jax.ShapeDtypeStruct((s, f), BF16),
                   jax.ShapeDtypeStruct((1, d), F32), jax.ShapeDtypeStruct((s, d), BF16)],
        scratch=[pltpu.VMEM((tm, d), F32)],
        sem=("arbitrary", "arbitrary"))


def matmul_tn(a, b, name, m_split=1, n_split=1, relu2=False, comm=None):
    s, m = a.shape
    n = b.shape[1]
    ts = _tile(s, 2048)
    bm, bn = m // m_split, n // n_split
    a_map = lambda j, k: (k, j // n_split)
    b_map = lambda j, k: (k, j % n_split)

    def body(a_ref, b_ref, o_ref, acc_ref):
        k = pl.program_id(1)
        av = a_ref[...]
        if relu2:
            af = jnp.maximum(av.astype(F32), 0.0)
            av = af * af
        p = _dot_tn(av.astype(BF16), b_ref[...].astype(BF16))

        @pl.when(k == 0)
        def _():
            acc_ref[...] = p

        @pl.when(k > 0)
        def _():
            acc_ref[...] += p

        @pl.when(k == s // ts - 1)
        def _():
            o_ref[...] = acc_ref[...].astype(BF16)

    outs, couts = _pallas(
        body, name=name, grid=(m_split * n_split, s // ts), args=[a, b], comm=comm,
        in_specs=[pl.BlockSpec((ts, bm), a_map), pl.BlockSpec((ts, bn), b_map)],
        out_specs=[pl.BlockSpec((bm, bn), lambda j, k: (j // n_split, j % n_split))],
        out_shape=[jax.ShapeDtypeStruct((m, n), BF16)],
        scratch=[pltpu.VMEM((bm, bn), F32)],
        sem=("parallel", "arbitrary"))
    return outs[0], couts


def matmul_residual(x, y, w, name):
    s, n = x.shape
    k = y.shape[1]
    tm = _tile(s, 1024)

    def body(x_ref, y_ref, w_ref, o_ref):
        o_ref[...] = x_ref[...] + _dot(y_ref[...], w_ref[...])

    return pl.pallas_call(
        body, name=name, grid=(s // tm,),
        in_specs=[pl.BlockSpec((tm, n), lambda i: (i, 0)),
                  pl.BlockSpec((tm, k), lambda i: (i, 0)),
                  pl.BlockSpec((k, n), lambda i: (0, 0))],
        out_specs=pl.BlockSpec((tm, n), lambda i: (i, 0)),
        out_shape=jax.ShapeDtypeStruct((s, n), F32),
        compiler_params=_params("parallel"),
    )(x, y, w)


def matmul_nt(dy, wg, name):
    s, n = dy.shape
    k = wg.shape[0]
    tm = _tile(s, 1024)

    def body(dy_ref, w_ref, o_ref):
        o_ref[...] = _dot_nt(dy_ref[...].astype(BF16), w_ref[...])

    return pl.pallas_call(
        body, name=name, grid=(s // tm,),
        in_specs=[pl.BlockSpec((tm, n), lambda i: (i, 0)),
                  pl.BlockSpec((k, n), lambda i: (0, 0))],
        out_specs=pl.BlockSpec((tm, k), lambda i: (i, 0)),
        out_shape=jax.ShapeDtypeStruct((s, k), F32),
        compiler_params=_params("parallel"),
    )(dy, wg)


def matmul_nt_norm_backward(dz, wg, x, g, dres, name):
    s, d = x.shape
    n = wg.shape[-1]
    tm = _tile(s, 512)

    def body(dz_ref, w_ref, x_ref, g_ref, dres_ref, dx_ref, dg_ref):
        dx, dgp = _norm_backward(_dot_nt(dz_ref[...], w_ref[...]), x_ref[...], g_ref[...], dres_ref[...])
        dx_ref[...] = dx
        _accumulate(dg_ref, dgp, pl.program_id(0) == 0)

    return pl.pallas_call(
        body, name=name, grid=(s // tm,),
        in_specs=[pl.BlockSpec((tm, n), lambda i: (i, 0)),
                  pl.BlockSpec((d, n), lambda i: (0, 0)),
                  pl.BlockSpec((tm, d), lambda i: (i, 0)),
                  pl.BlockSpec((1, d), lambda i: (0, 0)),
                  pl.BlockSpec((tm, d), lambda i: (i, 0))],
        out_specs=[pl.BlockSpec((tm, d), lambda i: (i, 0)),
                   pl.BlockSpec((1, d), lambda i: (0, 0))],
        out_shape=[jax.ShapeDtypeStruct((s, d), F32), jax.ShapeDtypeStruct((1, d), F32)],
        compiler_params=_params("arbitrary"),
    )(dz, wg, x, g, dres)


def loss_and_grad(y, target, name):
    s, d = y.shape
    tm = _tile(s, 1024)

    def body(y_ref, t_ref, dy_ref, l_ref):
        e = y_ref[...] - t_ref[...]
        dy_ref[...] = e / d
        part = jnp.sum(e * e, axis=0, keepdims=True) * (0.5 / d)

        @pl.when(pl.program_id(0) == 0)
        def _():
            l_ref[...] = part

        @pl.when(pl.program_id(0) > 0)
        def _():
            l_ref[...] += part

    return pl.pallas_call(
        body, name=name, grid=(s // tm,),
        in_specs=[pl.BlockSpec((tm, d), lambda i: (i, 0)), pl.BlockSpec((tm, d), lambda i: (i, 0))],
        out_specs=[pl.BlockSpec((tm, d), lambda i: (i, 0)), pl.BlockSpec((1, d), lambda i: (0, 0))],
        out_shape=[jax.ShapeDtypeStruct((s, d), F32), jax.ShapeDtypeStruct((1, d), F32)],
        compiler_params=_params("arbitrary"),
    )(y, target)


def _layernorm(x, g, b):
    mu = jnp.mean(x, axis=-1, keepdims=True)
    xc = x - mu
    rstd = lax.rsqrt(jnp.mean(xc * xc, axis=-1, keepdims=True) + EPS)
    xn = xc * rstd
    return xn * g + b, xn, rstd


def _layernorm_backward(dy, xn, rstd, g):
    dxn = dy * g
    return rstd * (dxn - jnp.mean(dxn, axis=-1, keepdims=True) - xn * jnp.mean(dxn * xn, axis=-1, keepdims=True))


def _group_halves(x_ref, jp, nch):
    blk = jnp.concatenate([x_ref[c * CHUNK:(c + 1) * CHUNK, jp * LANES:(jp + 1) * LANES] for c in range(nch)], axis=1)
    low = (lax.broadcasted_iota(jnp.int32, blk.shape, 1) % LANES) < HEAD_DIM
    return jnp.where(low, blk, 0.0).astype(BF16), jnp.where(low, 0.0, blk).astype(BF16)


def _spatial_apply(src_ref, w_ref, dst_ref, nch, bias_ref=None):
    for jp in range(4):
        lo, hi = _group_halves(src_ref, jp, nch)
        r = _dot(w_ref[2 * jp], lo) + _dot(w_ref[2 * jp + 1], hi)
        for c in range(nch):
            v = r[:, c * LANES:(c + 1) * LANES]
            if bias_ref is not None:
                v = v + bias_ref[:, jp * LANES:(jp + 1) * LANES]
            dst_ref[c * CHUNK:(c + 1) * CHUNK, jp * LANES:(jp + 1) * LANES] = v


def _glu(zb):
    w = zb.shape[1] // 2
    return zb[:, :w] * _sigmoid(zb[:, w:])


def _fill_padded(pad_ref, prev, cur, nxt, i, nt, tm):
    pad_ref[0:CONV_HALO, :] = jnp.where(i > 0, prev, 0.0)
    pad_ref[CONV_HALO:CONV_HALO + tm, :] = cur
    pad_ref[CONV_HALO + tm:2 * CONV_HALO + tm, :] = jnp.where(i < nt - 1, nxt, 0.0)


def _halo_specs(tm, s, width, col):
    hb, nhb = tm // CONV_HALO, s // CONV_HALO
    return [pl.BlockSpec((tm, width), lambda i: (i, col)),
            pl.BlockSpec((CONV_HALO, width), lambda i: (jnp.maximum(i * hb - 1, 0), col)),
            pl.BlockSpec((CONV_HALO, width), lambda i: (jnp.minimum((i + 1) * hb, nhb - 1), col))]


def _const_spec(shape):
    nd = len(shape)
    return pl.BlockSpec(shape, lambda i: (0,) * nd)


SUBLANES = 8


def _shift_scratch(tm, width):
    return pltpu.VMEM((SUBLANES - 1, tm + 2 * CONV_HALO - SUBLANES, width), F32)


def _fill_shifts(sh_ref, pad_ref):
    rows = sh_ref.shape[1]
    for sft in range(1, SUBLANES):
        sh_ref[sft - 1] = pad_ref[pl.ds(sft, rows), :]


def _tap(pad_ref, sh_ref, offset, cols):
    sft = offset % SUBLANES
    rows = pl.ds(offset - sft, CHUNK)
    return pad_ref[rows, cols] if sft == 0 else sh_ref[sft - 1, rows, cols]


def ab_mid_forward(z, sp, name, comm=None):
    s = z.shape[0]
    aw = z.shape[1] // 4
    tm = _tile(s, 512)
    nch, nt = tm // CHUNK, s // tm

    def body(zu_ref, zv_ref, zb_ref, zp_ref, zn_ref, w_ref, bias_ref, vg_ref, vb_ref, cw_ref, cb_ref, cg_ref, cbn_ref,
             y_ref, gc_ref, vl_ref, sv_ref, pad_ref, sh_ref):
        i = pl.program_id(0)
        vl_ref[...] = _layernorm(_gelu(zv_ref[...]), vg_ref[...], vb_ref[...])[0]
        _spatial_apply(vl_ref, w_ref, sv_ref, nch, bias_ref)
        y_ref[:, :aw] = (_gelu(zu_ref[...]) * sv_ref[...]).astype(BF16)

        _fill_padded(pad_ref, _glu(zp_ref[...]), _glu(zb_ref[...]), _glu(zn_ref[...]), i, nt, tm)
        _fill_shifts(sh_ref, pad_ref)
        for rb in range(tm // CHUNK):
            for lb in range(aw // LANES):
                cols = pl.ds(lb * LANES, LANES)
                acc = jnp.broadcast_to(cb_ref[:, cols], (CHUNK, LANES))
                for k in range(CONV_W):
                    acc = acc + cw_ref[k:k + 1, cols] * _tap(pad_ref, sh_ref, rb * CHUNK + CONV_HALO - CONV_W // 2 + k, cols)
                gc_ref[rb * CHUNK:(rb + 1) * CHUNK, cols] = acc
        yl = _layernorm(gc_ref[...], cg_ref[...], cbn_ref[...])[0]
        y_ref[:, aw:] = (yl * _sigmoid(yl)).astype(BF16)

    return _pallas(
        body, name=name, grid=(nt,), comm=comm,
        args=[z, z, z, z, z, sp["w"], sp["bias"], sp["vg"], sp["vb"], sp["cw"], sp["cb"], sp["cg"], sp["cbn"]],
        in_specs=[pl.BlockSpec((tm, aw), lambda i: (i, 0)), pl.BlockSpec((tm, aw), lambda i: (i, 1))]
        + _halo_specs(tm, s, 2 * aw, 1)
        + [_const_spec(sp["w"].shape), _const_spec(sp["bias"].shape)]
        + [_const_spec((1, aw))] * 2 + [_const_spec(sp["cw"].shape)] + [_const_spec((1, aw))] * 3,
        out_specs=[pl.BlockSpec((tm, 2 * aw), lambda i: (i, 0)), pl.BlockSpec((tm, aw), lambda i: (i, 0))],
        out_shape=[jax.ShapeDtypeStruct((s, 2 * aw), BF16), jax.ShapeDtypeStruct((s, aw), F32)],
        scratch=[pltpu.VMEM((tm, aw), F32), pltpu.VMEM((tm, aw), F32), pltpu.VMEM((tm + 2 * CONV_HALO, aw), F32),
                 _shift_scratch(tm, aw)],
        sem=("parallel",))


def _accumulate(ref, val, first):
    @pl.when(first)
    def _():
        ref[...] = val

    @pl.when(jnp.logical_not(first))
    def _():
        ref[...] += val


def ab_mid_backward(dy, z, gc, sp, name):
    s = z.shape[0]
    aw = z.shape[1] // 4
    tm = _tile(s, 512)
    nch, nt = tm // CHUNK, s // tm

    def body(dya_ref, dyb_ref, zu_ref, zv_ref, gc_ref, w_ref, wt_ref, bias_ref, vg_ref, vb_ref, cg_ref, cbn_ref,
             dz_ref, dgc_ref, dw_ref, dsb_ref, dvg_ref, dvb_ref, dcg_ref, dcbn_ref, dcb_ref,
             vl_ref, sv_ref, dsv_ref, dvl_ref):
        first = pl.program_id(0) == 0
        zu, zv = zu_ref[...], zv_ref[...]
        u = _gelu(zu)
        vl, vn, vrstd = _layernorm(_gelu(zv), vg_ref[...], vb_ref[...])
        vl_ref[...] = vl
        _spatial_apply(vl_ref, w_ref, sv_ref, nch, bias_ref)
        dya = dya_ref[...]
        dz_ref[:, :aw] = (dya * sv_ref[...] * _gelu_grad(zu)).astype(BF16)
        dsv = dya * u
        dsv_ref[...] = dsv
        _spatial_apply(dsv_ref, wt_ref, dvl_ref, nch)

        for jp in range(4):
            dlo, dhi = _group_halves(dsv_ref, jp, nch)
            vlo, vhi = _group_halves(vl_ref, jp, nch)
            vall = vlo + vhi
            _accumulate(dw_ref.at[2 * jp], _dot_nt(dlo, vall), first)
            _accumulate(dw_ref.at[2 * jp + 1], _dot_nt(dhi, vall), first)
        rows = dsv[0:CHUNK]
        for c in range(1, nch):
            rows = rows + dsv[c * CHUNK:(c + 1) * CHUNK]
        grp = lax.broadcasted_iota(jnp.int32, (8, aw), 0) == lax.broadcasted_iota(jnp.int32, (8, aw), 1) // HEAD_DIM
        e = jnp.where(grp, 1.0, 0.0).astype(BF16)
        hi = rows.astype(BF16)
        r1 = rows - hi.astype(F32)
        mid = r1.astype(BF16)
        lo = (r1 - mid.astype(F32)).astype(BF16)
        _accumulate(dsb_ref, _dot_nt(e, hi) + _dot_nt(e, mid) + _dot_nt(e, lo), first)

        dvl = dvl_ref[...]
        _accumulate(dvg_ref, jnp.sum(dvl * vn, axis=0, keepdims=True), first)
        _accumulate(dvb_ref, jnp.sum(dvl, axis=0, keepdims=True), first)
        dz_ref[:, aw:] = (_layernorm_backward(dvl, vn, vrstd, vg_ref[...]) * _gelu_grad(zv)).astype(BF16)

        yl, yn, yrstd = _layernorm(gc_ref[...], cg_ref[...], cbn_ref[...])
        sg = _sigmoid(yl)
        dyl = dyb_ref[...] * (sg + yl * sg * (1.0 - sg))
        _accumulate(dcg_ref, jnp.sum(dyl * yn, axis=0, keepdims=True), first)
        _accumulate(dcbn_ref, jnp.sum(dyl, axis=0, keepdims=True), first)
        dgc = _layernorm_backward(dyl, yn, yrstd, cg_ref[...])
        dgc_ref[...] = dgc
        _accumulate(dcb_ref, jnp.sum(dgc, axis=0, keepdims=True), first)

    vec = jax.ShapeDtypeStruct((1, aw), F32)
    return pl.pallas_call(
        body, name=name, grid=(nt,),
        in_specs=[pl.BlockSpec((tm, aw), lambda i: (i, 0)), pl.BlockSpec((tm, aw), lambda i: (i, 1)),
                  pl.BlockSpec((tm, aw), lambda i: (i, 0)), pl.BlockSpec((tm, aw), lambda i: (i, 1)),
                  pl.BlockSpec((tm, aw), lambda i: (i, 0)),
                  _const_spec(sp["w"].shape), _const_spec(sp["w"].shape), _const_spec(sp["bias"].shape)]
        + [_const_spec((1, aw))] * 4,
        out_specs=[pl.BlockSpec((tm, 2 * aw), lambda i: (i, 0)), pl.BlockSpec((tm, aw), lambda i: (i, 0)),
                   _const_spec(sp["w"].shape), _const_spec((8, CHUNK))] + [_const_spec((1, aw))] * 5,
        out_shape=[jax.ShapeDtypeStruct((s, 4 * aw), BF16), jax.ShapeDtypeStruct((s, aw), F32),
                   jax.ShapeDtypeStruct(sp["w"].shape, F32), jax.ShapeDtypeStruct((8, CHUNK), F32)] + [vec] * 5,
        scratch_shapes=[pltpu.VMEM((tm, aw), F32)] * 4,
        compiler_params=_params("arbitrary"),
    )(dy, dy, z, z, gc, sp["w"], sp["wt"], sp["bias"], sp["vg"], sp["vb"], sp["cg"], sp["cbn"])


def conv_backward(dgc, z, dz_in, sp, name, comm=None):
    s = z.shape[0]
    aw = z.shape[1] // 4
    tm = _tile(s, 512)
    nt = s // tm

    def body(d_ref, dp_ref, dn_ref, zb_ref, cw_ref, dzin_ref, dz_ref, dcw_ref, padd_ref, g_ref, dgg_ref, shd_ref):
        i = pl.program_id(0)
        _fill_padded(padd_ref, dp_ref[...], d_ref[...], dn_ref[...], i, nt, tm)
        _fill_shifts(shd_ref, padd_ref)
        g_ref[...] = _glu(zb_ref[...])

        @pl.when(i == 0)
        def _():
            dcw_ref[...] = jnp.zeros_like(dcw_ref)

        def grad_input(cols, rb):
            acc = jnp.zeros((CHUNK, LANES), F32)
            for k in range(CONV_W):
                acc = acc + cw_ref[k:k + 1, cols] * _tap(padd_ref, shd_ref, rb * CHUNK + CONV_HALO + CONV_W // 2 - k, cols)
            dgg_ref[rb * CHUNK:(rb + 1) * CHUNK, cols] = acc

        def grad_taps(cols, rb):
            gblk = g_ref[rb * CHUNK:(rb + 1) * CHUNK, cols]
            for k in range(CONV_W):
                prod = gblk * _tap(padd_ref, shd_ref, rb * CHUNK + CONV_HALO + CONV_W // 2 - k, cols)
                dcw_ref[k:k + 1, cols] += jnp.sum(prod, axis=0, keepdims=True)

        for lb in range(aw // LANES):
            for rb in range(tm // CHUNK):
                pl.when(i >= 0)(functools.partial(grad_input, pl.ds(lb * LANES, LANES), rb))
                pl.when(i >= 0)(functools.partial(grad_taps, pl.ds(lb * LANES, LANES), rb))

        zb = zb_ref[...]
        val, sg = zb[:, :aw], _sigmoid(zb[:, aw:])
        dgg = dgg_ref[...]
        dz_ref[:, :aw] = (dgg * sg).astype(BF16)
        dz_ref[:, aw:] = (dgg * val * sg * (1.0 - sg)).astype(BF16)

    return _pallas(
        body, name=name, grid=(nt,), args=[dgc, dgc, dgc, z, sp["cw"], dz_in], comm=comm,
        in_specs=_halo_specs(tm, s, aw, 0) + [pl.BlockSpec((tm, 2 * aw), lambda i: (i, 1))]
        + [_const_spec(sp["cw"].shape), pl.BlockSpec(memory_space=pl.ANY)],
        out_specs=[pl.BlockSpec((tm, 2 * aw), lambda i: (i, 1)), _const_spec(sp["cw"].shape)],
        out_shape=[jax.ShapeDtypeStruct((s, 4 * aw), BF16), jax.ShapeDtypeStruct(sp["cw"].shape, F32)],
        scratch=[pltpu.VMEM((tm + 2 * CONV_HALO, aw), F32), pltpu.VMEM((tm, aw), F32), pltpu.VMEM((tm, aw), F32),
                 _shift_scratch(tm, aw)],
        aliases={5: 0}, sem=("arbitrary",))


def rope_tables(s):
    pos = jnp.arange(s, dtype=F32)
    inv_freq = ROPE_THETA ** (-jnp.arange(0, ROT_DIM, 2, dtype=F32) / ROT_DIM)
    ang = pos[:, None] * inv_freq[None, :]
    cos, sin = jnp.cos(ang), jnp.sin(ang)
    half = ROT_DIM // 2
    rest = HEAD_DIM - ROT_DIM
    one, zero, zrest = jnp.ones((s, rest), F32), jnp.zeros((s, half), F32), jnp.zeros((s, rest), F32)
    c = jnp.concatenate([cos, cos, one], axis=1)
    s1 = jnp.concatenate([-sin, zero, zrest], axis=1)
    s2 = jnp.concatenate([zero, sin, zrest], axis=1)
    return tuple(jnp.tile(t, (1, LANES // HEAD_DIM)) for t in (c, s1, s2))


def qk_prep_forward(qkv, tabs, gains, name, comm=None):
    s, w3 = qkv.shape
    w = w3 // 3
    tm = _tile(s, 512)

    def body(x_ref, c_ref, s1_ref, s2_ref, g_ref, o_ref):
        bd = _block_diag(LANES)
        for rb in range(tm // CHUNK):
            rows = pl.ds(rb * CHUNK, CHUNK)
            c, s1, s2 = c_ref[rows, :], s1_ref[rows, :], s2_ref[rows, :]
            for b in range(w // LANES):
                cols = pl.ds(b * LANES, LANES)
                t = x_ref[rows, cols]
                r = lax.rsqrt(_seg_sum(t * t, bd) * (1.0 / HEAD_DIM) + EPS)
                y = t * r * g_ref[...]
                o_ref[rows, cols] = (y * c + pltpu.roll(y, LANES - ROT_DIM // 2, 1) * s1
                                     + pltpu.roll(y, ROT_DIM // 2, 1) * s2)

    tab = pl.BlockSpec((tm, LANES), lambda i, p: (i, 0))
    outs, couts = _pallas(
        body, name=name, grid=(s // tm, 2), args=[qkv, *tabs, gains], comm=comm,
        in_specs=[pl.BlockSpec((tm, w), lambda i, p: (i, p)), tab, tab, tab,
                  pl.BlockSpec((None, 1, LANES), lambda i, p: (p, 0, 0))],
        out_specs=[pl.BlockSpec((tm, w), lambda i, p: (i, p))],
        out_shape=[jax.ShapeDtypeStruct((s, 2 * w), F32)],
        sem=("parallel", "arbitrary"))
    return outs[0], couts


def qk_prep_backward(dq, dq_prev, dq_next, dk, dv, qkv, tabs, gains, name):
    s, w3 = qkv.shape
    w = w3 // 3
    tm = _tile(s, 512)
    t = _tile(s, ATT_TILE)
    per_tile, per_halo, tiles = t // tm, ATT_HALO // tm, s // t

    def neighbours(i):
        tile, c = i // per_tile, i % per_tile
        from_before = (c < per_halo) & (tile >= 1)
        from_after = (c >= per_tile - per_halo) & (tile + 1 < tiles)
        return (from_before, from_after, jnp.where(from_before, (tile - 1) * per_halo + c, 0),
                jnp.where(from_after, (tile + 1) * per_halo + c - (per_tile - per_halo), 0))

    def body(*refs):
        grads = ((refs[0], refs[1], refs[2]), (refs[3],), (refs[4],))
        x_ref, c_ref, s1_ref, s2_ref, g_ref, o_ref, dg_ref = refs[5:]
        part, first = pl.program_id(0), pl.program_id(1) == 0
        from_before, from_after, _, _ = neighbours(pl.program_id(1))

        def normed(ds):
            bd = _block_diag(LANES)
            acc = jnp.zeros((1, LANES), F32)
            for rb in range(tm // CHUNK):
                rows = pl.ds(rb * CHUNK, CHUNK)
                c, s1, s2 = c_ref[rows, :], s1_ref[rows, :], s2_ref[rows, :]
                for b in range(w // LANES):
                    cols = pl.ds(b * LANES, LANES)
                    dout = ds[0][rows, cols]
                    if len(ds) == 3:
                        dout = (dout + jnp.where(from_after, ds[1][rows, cols], 0.0)
                                + jnp.where(from_before, ds[2][rows, cols], 0.0))
                    dy = (dout * c + pltpu.roll(dout * s1, ROT_DIM // 2, 1)
                          + pltpu.roll(dout * s2, LANES - ROT_DIM // 2, 1))
                    t = x_ref[rows, cols]
                    r = lax.rsqrt(_seg_sum(t * t, bd) * (1.0 / HEAD_DIM) + EPS)
                    xh = t * r
                    acc = acc + jnp.sum(dy * xh, axis=0, keepdims=True)
                    tt = dy * g_ref[...]
                    o_ref[rows, cols] = (r * (tt - xh * (_seg_sum(tt * xh, bd) * (1.0 / HEAD_DIM)))).astype(BF16)
            _accumulate(dg_ref, acc, first)

        for p in range(2):
            pl.when(part == p)(functools.partial(normed, grads[p]))

        @pl.when(part == 2)
        def _():
            o_ref[...] = grads[2][0][...].astype(BF16)
            _accumulate(dg_ref, jnp.zeros((1, LANES), F32), first)

    def gspec(p):
        return pl.BlockSpec((tm, w), lambda q, i: (jnp.where(q == p, i, 0), 0))

    prev_spec = pl.BlockSpec((tm, w), lambda q, i: (jnp.where(q == 0, neighbours(i)[3], 0), 0))
    next_spec = pl.BlockSpec((tm, w), lambda q, i: (jnp.where(q == 0, neighbours(i)[2], 0), 0))
    tab = pl.BlockSpec((tm, LANES), lambda q, i: (i, 0))
    return pl.pallas_call(
        body, name=name, grid=(3, s // tm),
        in_specs=[gspec(0), prev_spec, next_spec, gspec(1), gspec(2)]
        + [pl.BlockSpec((tm, w), lambda q, i: (i, q)), tab, tab, tab,
           pl.BlockSpec((None, 1, LANES), lambda q, i: (q, 0, 0))],
        out_specs=[pl.BlockSpec((tm, w), lambda q, i: (i, q)), pl.BlockSpec((None, 1, LANES), lambda q, i: (q, 0, 0))],
        out_shape=[jax.ShapeDtypeStruct((s, w3), BF16), jax.ShapeDtypeStruct((3, 1, LANES), F32)],
        compiler_params=_params("arbitrary", "arbitrary"),
    )(dq, dq_prev, dq_next, dk, dv, qkv, *tabs, gains)


ATT_TILE = 2048
ATT_HALO = BAND * max(DILATIONS)
ROWS_PER_COPY = 256


def _att_specs(s, t, col_fn, halo=True):
    hb, nhb = t // ATT_HALO, s // ATT_HALO
    specs = [pl.BlockSpec((t, LANES), lambda hp, i: (i, col_fn(hp)))]
    if halo:
        specs += [pl.BlockSpec((ATT_HALO, LANES), lambda hp, i: (jnp.maximum(i * hb - 1, 0), col_fn(hp))),
                  pl.BlockSpec((ATT_HALO, LANES), lambda hp, i: (jnp.minimum((i + 1) * hb, nhb - 1), col_fn(hp)))]
    return specs


def _gather_rows(dst_ref, dst_row, src_ref, start, count, stride, scale=None):
    for c in range(0, count, ROWS_PER_COPY):
        m = min(ROWS_PER_COPY, count - c)
        v = src_ref[pl.ds(start + c * stride, m, stride=stride), :]
        if scale is not None:
            v = v * scale
        dst_ref[dst_row + c:dst_row + c + m, :] = v.astype(dst_ref.dtype)


SPLIT = 4


def _split_rows(tmp_ref, base, src_ref, rows):
    part = rows // SPLIT
    for b in range(SPLIT):
        _gather_rows(tmp_ref, base + b * part, src_ref, b, part, SPLIT)


def _stage(dst_ref, cur_ref, d, t, scale=None, tmp_ref=None):
    n = t // d
    if tmp_ref is None or d != SPLIT * SPLIT:
        for r in range(d):
            _gather_rows(dst_ref, r * n, cur_ref, r, n, d, scale)
        return
    _split_rows(tmp_ref, 0, cur_ref, t)
    for r in range(d):
        _gather_rows(dst_ref, r * n, tmp_ref, (r % SPLIT) * (t // SPLIT) + r // SPLIT, n, SPLIT, scale)


def _stage_window(dst_ref, refs, d, t, scale=None, edges=None, tmp_ref=None):
    cur_ref, prev_ref, next_ref = refs
    n = t // d
    nw = n + 2 * BAND
    if tmp_ref is None or d != SPLIT * SPLIT:
        for r in range(d):
            _gather_rows(dst_ref, r * nw, prev_ref, ATT_HALO - BAND * d + r, BAND, d, scale)
            _gather_rows(dst_ref, r * nw + BAND, cur_ref, r, n, d, scale)
            _gather_rows(dst_ref, r * nw + BAND + n, next_ref, r, BAND, d, scale)
    else:
        assert ATT_HALO == BAND * d
        _split_rows(tmp_ref, 0, cur_ref, t)
        _split_rows(tmp_ref, t, prev_ref, ATT_HALO)
        _split_rows(tmp_ref, t + ATT_HALO, next_ref, ATT_HALO)
        for r in range(d):
            a, b = r // SPLIT, r % SPLIT
            _gather_rows(dst_ref, r * nw, tmp_ref, t + b * (ATT_HALO // SPLIT) + a, BAND, SPLIT, scale)
            _gather_rows(dst_ref, r * nw + BAND, tmp_ref, b * (t // SPLIT) + a, n, SPLIT, scale)
            _gather_rows(dst_ref, r * nw + BAND + n, tmp_ref, t + ATT_HALO + b * (ATT_HALO // SPLIT) + a, BAND, SPLIT, scale)
    if edges is not None:
        first, last, value = edges
        fill = jnp.full((BAND, LANES), value, dst_ref.dtype)

        @pl.when(first)
        def _():
            for r in range(d):
                dst_ref[r * nw:r * nw + BAND, :] = fill

        @pl.when(last)
        def _():
            for r in range(d):
                dst_ref[r * nw + BAND + n:(r + 1) * nw, :] = fill


def _band_bias(rows, cols, centre_axis):
    shape = (rows // 2, cols)
    ctr = lax.broadcasted_iota(jnp.int32, shape, centre_axis)
    win = lax.broadcasted_iota(jnp.int32, shape, 1 - centre_axis)
    bias = jnp.where(jnp.abs(win - BAND - ctr) <= BAND, 0.0, NEG).astype(F32)
    return jnp.concatenate([bias, bias], axis=0)


def _window_bias(first_row, length):
    row = first_row + lax.broadcasted_iota(jnp.int32, (1, 2 * CHUNK), 1)
    return jnp.where((row >= 0) & (row < length), 0.0, NEG).astype(F32)


def _scatter_rows(dst_ref, src_ref, d, t, combine):
    n = t // d
    for r in range(d):
        for c in range(0, n, ROWS_PER_COPY):
            m = min(ROWS_PER_COPY, n - c)
            idx = pl.ds(r + c * d, m, stride=d)
            combine(idx, slice(r * n + c, r * n + c + m))


def _unit_rows(u, n):
    upr = n // CHUNK
    r = u // upr
    b = u - r * upr
    if isinstance(u, int):
        return u * CHUNK, (u + r) * CHUNK, b * CHUNK - BAND
    return pl.multiple_of(u * CHUNK, CHUNK), pl.multiple_of((u + r) * CHUNK, CHUNK), b * CHUNK - BAND


def _two_heads(x):
    head0 = lax.broadcasted_iota(jnp.int32, x.shape, 1) < HEAD_DIM
    zero = jnp.zeros_like(x)
    return jnp.concatenate([jnp.where(head0, x, zero), jnp.where(head0, zero, x)], axis=0)


def _merge_heads(x2):
    rows = x2.shape[0] // 2
    head0 = lax.broadcasted_iota(jnp.int32, (rows, LANES), 1) < HEAD_DIM
    return jnp.where(head0, jnp.broadcast_to(x2[:rows], (rows, LANES)), jnp.broadcast_to(x2[rows:], (rows, LANES)))


def _col(part):
    return lambda hp: part * 8 + hp


def attention_forward(qk, qkv, name, comm=None):
    s, w2 = qk.shape
    w = w2 // 2
    t = _tile(s, ATT_TILE)
    scale = HEAD_DIM ** -0.5

    def body(q_ref, k_ref, kp_ref, kn_ref, v_ref, vp_ref, vn_ref, o_ref, lse_ref,
             qs_ref, ks_ref, vs_ref, os_ref, ls_ref, or_ref, tmp_ref):
        i = pl.program_id(1)
        band = _band_bias(2 * CHUNK, 2 * CHUNK, 0)
        for pi, d in enumerate(DILATIONS):
            n = t // d
            _stage(qs_ref, q_ref, d, t, scale, tmp_ref=tmp_ref)
            _stage_window(ks_ref, (k_ref, kp_ref, kn_ref), d, t, tmp_ref=tmp_ref)
            _stage_window(vs_ref, (v_ref, vp_ref, vn_ref), d, t, tmp_ref=tmp_ref)

            for u in range(t // CHUNK):
                qrow, wrow, first = _unit_rows(u, n)
                kb, vb = ks_ref[pl.ds(wrow, 2 * CHUNK), :], vs_ref[pl.ds(wrow, 2 * CHUNK), :]
                sc = _dot_nt(_two_heads(qs_ref[pl.ds(qrow, CHUNK), :]), kb) + band
                if first < 0 or first + 2 * CHUNK > n:
                    sc = sc + _window_bias(i * n + first, s // d)
                m = jnp.max(sc, axis=1, keepdims=True)
                p = jnp.exp(sc - m)
                den = jnp.sum(p, axis=1, keepdims=True)
                os_ref[pl.ds(qrow, CHUNK), :] = _merge_heads(_dot(p.astype(BF16), vb) / den)
                ls_ref[pl.ds(qrow, CHUNK), :] = _merge_heads(m + jnp.log(den))

            if pi == 0:
                def assign(idx, rows):
                    or_ref[idx, :] = os_ref[rows, :]
                    lse_ref[idx, :] = ls_ref[rows, :]
                _scatter_rows(None, None, d, t, assign)
            else:
                def merge(idx, rows):
                    la, lb = lse_ref[idx, :], ls_ref[rows, :]
                    mx = jnp.maximum(la, lb)
                    wa, wb = jnp.exp(la - mx), jnp.exp(lb - mx)
                    den = wa + wb
                    or_ref[idx, :] = (wa * or_ref[idx, :] + wb * os_ref[rows, :]) / den
                    lse_ref[idx, :] = mx + jnp.log(den)
                _scatter_rows(None, None, d, t, merge)
        o_ref[...] = or_ref[...].astype(BF16)

    ospec = pl.BlockSpec((t, LANES), lambda hp, i: (i, hp))
    win_rows = t + 2 * ATT_HALO
    return _pallas(
        body, name=name, grid=(w // LANES, s // t), args=[qk, qk, qk, qk, qkv, qkv, qkv], comm=comm,
        in_specs=_att_specs(s, t, _col(0), halo=False) + _att_specs(s, t, _col(1)) + _att_specs(s, t, _col(2)),
        out_specs=[ospec, ospec],
        out_shape=[jax.ShapeDtypeStruct((s, w), BF16), jax.ShapeDtypeStruct((s, w), F32)],
        scratch=[pltpu.VMEM((t, LANES), BF16), pltpu.VMEM((win_rows, LANES), BF16), pltpu.VMEM((win_rows, LANES), BF16),
                 pltpu.VMEM((t, LANES), F32), pltpu.VMEM((t, LANES), F32), pltpu.VMEM((t, LANES), F32),
                 pltpu.VMEM((win_rows, LANES), F32)],
        sem=("parallel", "parallel"))


def attention_delta(do, o, lse, name):
    s, w = do.shape
    tm = _tile(s, 512)

    def body(do_ref, o_ref, lse_ref, st_ref):
        bd = _block_diag(LANES)
        lane = lax.broadcasted_iota(jnp.int32, (CHUNK, LANES), 1)
        for rb in range(tm // CHUNK):
            rows = pl.ds(rb * CHUNK, CHUNK)
            for b in range(w // LANES):
                cols = pl.ds(b * LANES, LANES)
                dl = _seg_sum(do_ref[rows, cols] * o_ref[rows, cols].astype(F32), bd)
                ls = lse_ref[rows, cols]
                st_ref[rows, cols] = jnp.where(lane % HEAD_DIM < HEAD_DIM // 2, ls, dl)

    spec = pl.BlockSpec((tm, w), lambda i: (i, 0))
    return pl.pallas_call(
        body, name=name, grid=(s // tm,), in_specs=[spec, spec, spec], out_specs=spec,
        out_shape=jax.ShapeDtypeStruct((s, w), F32), compiler_params=_params("parallel"),
    )(do, o, lse)


def attention_backward(qk, qkv, do, stats, name, comm=None):
    s, w2 = qk.shape
    w = w2 // 2
    t = _tile(s, ATT_TILE)
    tiles = s // t
    scale = HEAD_DIM ** -0.5

    def body(k_ref, v_ref, q_ref, qp_ref, qn_ref, do_ref, dop_ref, don_ref, st_ref, stp_ref, stn_ref,
             dk_ref, dv_ref, dq_ref, dqp_ref, dqn_ref, ks_ref, vs_ref, qs_ref, dos_ref, sts_ref, dks_ref, dvs_ref, dqw_ref):
        i = pl.program_id(1)
        key = lax.broadcasted_iota(jnp.int32, (CHUNK, 2 * CHUNK), 0)
        win_ = lax.broadcasted_iota(jnp.int32, (CHUNK, 2 * CHUNK), 1)
        half = jnp.where(jnp.abs(win_ - BAND - key) <= BAND, 0.0, NEG).astype(F32)
        band = jnp.concatenate([half, half], axis=1)
        edges = (i == 0, i == tiles - 1, -NEG)
        dqp_ref[...] = jnp.zeros_like(dqp_ref)
        dqn_ref[...] = jnp.zeros_like(dqn_ref)
        for pi, d in enumerate(DILATIONS):
            n = t // d
            nw = n + 2 * BAND
            _stage(ks_ref, k_ref, d, t, tmp_ref=dqw_ref)
            _stage(vs_ref, v_ref, d, t, tmp_ref=dqw_ref)
            _stage_window(qs_ref, (q_ref, qp_ref, qn_ref), d, t, scale, tmp_ref=dqw_ref)
            _stage_window(dos_ref, (do_ref, dop_ref, don_ref), d, t, tmp_ref=dqw_ref)
            _stage_window(sts_ref, (st_ref, stp_ref, stn_ref), d, t, edges=edges, tmp_ref=dqw_ref)

            for u in range(t // CHUNK):
                krow, wrow, first = _unit_rows(u, n)
                rows, win = pl.ds(krow, CHUNK), pl.ds(wrow, 2 * CHUNK)
                kb = ks_ref[rows, :]
                q2, do2 = _two_heads(qs_ref[win, :]), _two_heads(dos_ref[win, :])
                st = jnp.transpose(sts_ref[win, :])
                mid = HEAD_DIM // 2
                lse2 = jnp.concatenate([st[0:1, :], st[HEAD_DIM:HEAD_DIM + 1, :]], axis=1)
                dl2 = jnp.concatenate([st[mid:mid + 1, :], st[HEAD_DIM + mid:HEAD_DIM + mid + 1, :]], axis=1)
                p = jnp.exp(_dot_nt(kb, q2) + band - lse2)
                ds = (p * (_dot_nt(vs_ref[rows, :], do2) - dl2)).astype(BF16)
                dvs_ref[rows, :] = _dot(p.astype(BF16), do2)
                dks_ref[rows, :] = _dot(ds, q2)
                k2 = _two_heads(kb)
                dqw = (_dot_tn(ds[:, :2 * CHUNK], k2[:CHUNK]) + _dot_tn(ds[:, 2 * CHUNK:], k2[CHUNK:])) * scale
                if first < 0:
                    dqw_ref[pl.ds(wrow, CHUNK), :] = dqw[:CHUNK]
                else:
                    dqw_ref[pl.ds(wrow, CHUNK), :] += dqw[:CHUNK]
                dqw_ref[pl.ds(wrow + CHUNK, CHUNK), :] = dqw[CHUNK:]

            def add(idx, rows, pi=pi):
                dk_ref[idx, :] = dks_ref[rows, :] if pi == 0 else dk_ref[idx, :] + dks_ref[rows, :]
                dv_ref[idx, :] = dvs_ref[rows, :] if pi == 0 else dv_ref[idx, :] + dvs_ref[rows, :]
            _scatter_rows(None, None, d, t, add)

            for r in range(d):
                before = pl.ds(ATT_HALO - BAND * d + r, BAND, stride=d)
                after = pl.ds(r, BAND, stride=d)
                dqp_ref[before, :] += dqw_ref[r * nw:r * nw + BAND, :]
                dqn_ref[after, :] += dqw_ref[r * nw + BAND + n:(r + 1) * nw, :]
                for c in range(0, n, ROWS_PER_COPY):
                    m = min(ROWS_PER_COPY, n - c)
                    idx = pl.ds(r + c * d, m, stride=d)
                    val = dqw_ref[r * nw + BAND + c:r * nw + BAND + c + m, :]
                    dq_ref[idx, :] = val if pi == 0 else dq_ref[idx, :] + val

    ident = lambda hp: hp
    ospec = pl.BlockSpec((t, LANES), lambda hp, i: (i, hp))
    hspec = pl.BlockSpec((ATT_HALO, LANES), lambda hp, i: (i, hp))
    win_rows = t + 2 * ATT_HALO
    halo = jax.ShapeDtypeStruct((tiles * ATT_HALO, w), F32)
    return _pallas(
        body, name=name, grid=(w // LANES, tiles), comm=comm,
        args=[qk, qkv, qk, qk, qk, do, do, do, stats, stats, stats],
        in_specs=_att_specs(s, t, _col(1), halo=False) + _att_specs(s, t, _col(2), halo=False)
        + _att_specs(s, t, _col(0)) + _att_specs(s, t, ident) * 2,
        out_specs=[ospec, ospec, ospec, hspec, hspec],
        out_shape=[jax.ShapeDtypeStruct((s, w), F32)] * 3 + [halo, halo],
        scratch=[pltpu.VMEM((t, LANES), BF16), pltpu.VMEM((t, LANES), BF16),
                 pltpu.VMEM((win_rows, LANES), BF16), pltpu.VMEM((win_rows, LANES), BF16),
                 pltpu.VMEM((win_rows, LANES), F32),
                 pltpu.VMEM((t, LANES), F32), pltpu.VMEM((t, LANES), F32), pltpu.VMEM((win_rows, LANES), F32)],
        sem=("parallel", "parallel"))


def adamw_update(recvs, w, m, v, name, comm=None):
    nl = len(recvs)
    r, c = recvs[0].shape[1:]
    tr = 256 if (r > 256 and r % 256 == 0) else r
    nt = r // tr
    c1 = 1.0 - ADAM_B1 ** ADAM_STEP
    c2 = 1.0 - ADAM_B2 ** ADAM_STEP

    def body(*refs):
        g_refs = refs[:nl]
        w_ref, m_ref, v_ref, go_ref, d_ref, mo_ref, vo_ref = refs[nl:]

        def update(g_ref):
            g = g_ref[0].astype(F32)
            for j in range(1, N_DEV):
                g = g + g_ref[j].astype(F32)
            mn = ADAM_B1 * m_ref[...] + (1.0 - ADAM_B1) * g
            vn = ADAM_B2 * v_ref[...] + (1.0 - ADAM_B2) * (g * g)
            go_ref[...] = g
            mo_ref[...] = mn
            vo_ref[...] = vn
            d_ref[...] = -ADAM_LR * ((mn / c1) / (jnp.sqrt(vn / c2) + ADAM_EPS) + ADAM_WD * w_ref[...])

        for layer in range(nl):
            pl.when(pl.program_id(0) == layer)(functools.partial(update, g_refs[layer]))

    def gspec(layer):
        return pl.BlockSpec((N_DEV, tr, c), lambda l, i: (0, jnp.where(l == layer, i, 0), 0))

    spec = pl.BlockSpec((tr, c), lambda l, i: (l * nt + i, 0))
    return _pallas(
        body, name=name, grid=(nl, nt), args=[*recvs, w, m, v], comm=comm,
        in_specs=[gspec(layer) for layer in range(nl)] + [spec, spec, spec],
        out_specs=[spec] * 4, out_shape=[jax.ShapeDtypeStruct((nl * r, c), F32)] * 4,
        sem=("arbitrary", "arbitrary"))


BIG = ("mlp_w1", "mlp_w2", "ab_w_in", "ab_w_out", "c_w_qkv", "c_w_out")
SMALL = ("mix_norm_g", "mlp_norm_g", "a_spatial_w", "a_spatial_b", "a_vnorm_g", "a_vnorm_b", "b_conv_b", "b_norm_g",
         "b_norm_b", "c_q_norm_g", "c_k_norm_g")
WEIGHTS = ("mix_norm_g", "mlp_norm_g", "mlp_w1", "mlp_w2", "ab_w_in", "a_spatial_w", "a_spatial_b", "a_vnorm_g",
           "a_vnorm_b", "b_conv_w", "b_conv_b", "b_norm_g", "b_norm_b", "ab_w_out", "c_w_qkv", "c_q_norm_g",
           "c_k_norm_g", "c_w_out")


def _mixer_params(p, conv_full, i):
    aw = p["a_vnorm_g"].shape[1]
    row = lambda t: t[i][None, :]
    return dict(
        w=p["a_spatial_w"][i].astype(BF16), wt=jnp.swapaxes(p["a_spatial_w"][i], 1, 2).astype(BF16),
        bias=jnp.repeat(p["a_spatial_b"][i].T, aw // p["a_spatial_b"].shape[1], axis=1),
        vg=row(p["a_vnorm_g"]), vb=row(p["a_vnorm_b"]), cw=jnp.pad(conv_full[i], ((0, 1), (0, 0))),
        cb=row(p["b_conv_b"]), cg=row(p["b_norm_g"]), cbn=row(p["b_norm_b"]))


def _head_gains(p, i):
    rep = LANES // HEAD_DIM
    return jnp.stack([jnp.tile(p["c_q_norm_g"][i], rep), jnp.tile(p["c_k_norm_g"][i], rep),
                      jnp.ones((LANES,), F32)])[:, None, :]


PACKED = tuple(n for n in SMALL if n != "a_spatial_w")


def _pack(d):
    flat = jnp.concatenate([d[n].reshape(-1) for n in PACKED])
    rows = -(-flat.shape[0] // (8 * LANES)) * 8
    return jnp.pad(flat, (0, rows * LANES - flat.shape[0])).reshape(rows, LANES)


def _unpack(packed, like):
    flat, out, pos = packed.reshape(-1), {}, 0
    for n in PACKED:
        size = math.prod(like[n].shape)
        out[n] = flat[pos:pos + size].reshape(like[n].shape)
        pos += size
    return out


class Traffic:
    def __init__(self, shards, full=()):
        self.shards, self.w, self.queue, self.parts = shards, dict(full), [], {}

    def run(self, fn, *args, gather=(), send=False, **kw):
        operands, flags, dest = [], [], []
        if self.shards is None:
            if send:
                self.parts.update(self.queue)
                self.queue = []
        else:
            for k in gather:
                if k not in self.w:
                    operands.append(self.shards[k])
                    flags.append(("gather", SHARD_AXIS.get(k[0])))
                    dest.append((self.w, k))
            if send:
                for k, t in self.queue:
                    operands.append(t)
                    flags.append(("scatter", SHARD_AXIS.get(k[0])))
                    dest.append((self.parts, k))
                self.queue = []
        outs, couts = fn(*args, comm=PeerCopies(operands, flags) if operands else None, **kw)
        for (table, k), t in zip(dest, couts):
            table[k] = t
        return outs

    def flush(self, name, extra=()):
        self.queue += list(extra)
        self.run(lambda comm: ([], run_copies(comm, name) if comm is not None else []), send=True)


SHARD_AXIS = {"mlp_w1": 1, "mlp_w2": 0, "ab_w_in": 1, "ab_w_out": 0, "c_w_qkv": 1, "c_w_out": 0}


def forward_backward(x, target, p, tr, conv_full):
    s, d = x.shape
    depth = p["mix_norm_g"].shape[0]
    tabs = rope_tables(s)
    saved = []
    for l in range(depth):
        i = l // 2
        mix_g, mlp_g = p["mix_norm_g"][l][None, :], p["mlp_norm_g"][l][None, :]
        st = dict(x_in=x)
        nxt = () if l + 1 == depth else ((("c_w_qkv", i), ("c_w_out", i)) if l % 2 == 0 else
                                        (("ab_w_in", i + 1), ("ab_w_out", i + 1)))
        if l % 2 == 0:
            sp = _mixer_params(p, conv_full, i)
            z, h = tr.run(norm_matmul, x, mix_g, tr.w["ab_w_in", i], f"ab_in_{l}", gather=[("mlp_w1", l)])
            ycat, gc = tr.run(ab_mid_forward, z, sp, f"ab_mid_{l}", gather=[("mlp_w2", l)])
            x = matmul_residual(x, ycat, tr.w["ab_w_out", i], f"ab_out_{l}")
            st.update(z=z, h=h, y=ycat, gc=gc, sp=sp)
        else:
            gains = _head_gains(p, i)
            qkv, h = norm_matmul(x, mix_g, tr.w["c_w_qkv", i], f"c_qkv_{l}")[0]
            qk = qk_prep_forward(qkv, tabs, gains, f"c_prep_{l}")[0]
            ahead = [("mlp_w1", l), ("mlp_w2", l)] + ([("mlp_w1", l + 1)] if l + 1 < depth else [])
            o, lse = tr.run(attention_forward, qk, qkv, f"c_attn_{l}", gather=ahead)
            x = matmul_residual(x, o, tr.w["c_w_out", i], f"c_out_{l}")
            st.update(qkv=qkv, h=h, qk=qk, y=o, lse=lse, gains=gains)
            nxt = nxt + ((("mlp_w2", l + 1),) if l + 1 < depth else ())
        st["x_mid"] = x
        x, a, h2 = tr.run(mlp_forward, x, mlp_g, tr.w["mlp_w1", l], tr.w["mlp_w2", l], f"mlp_{l}", gather=nxt)
        st.update(a=a, h2=h2)
        saved.append(st)

    dy, loss_part = loss_and_grad(x, target, "loss")

    def tn(*a, comm, **kw):
        out, couts = matmul_tn(*a, comm=comm, **kw)
        return [out], couts

    small = {n: [None] * p[n].shape[0] for n in SMALL}
    conv_grads = [None] * p["b_conv_b"].shape[0]
    for l in reversed(range(depth)):
        i, st = l // 2, saved[l]
        mix_g, mlp_g = p["mix_norm_g"][l][None, :], p["mlp_norm_g"][l][None, :]
        w1, w2 = tr.w["mlp_w1", l], tr.w["mlp_w2", l]
        dxm, da, dg, dyb = tr.run(mlp_backward, dy, st["a"], st["x_mid"], mlp_g, w1, w2, f"mlp_bwd_{l}", send=True)
        small["mlp_norm_g"][l] = dg[0]
        tr.queue.append((("mlp_w1", l), matmul_tn(st["h2"], da, f"mlp_dw1_{l}", n_split=2)[0]))
        tr.queue.append((("mlp_w2", l), matmul_tn(st["a"], dyb, f"mlp_dw2_{l}", m_split=4, relu2=True)[0]))
        if l % 2 == 0:
            sp = st["sp"]
            wout = tr.w["ab_w_out", i]
            dycat = matmul_nt(dxm, wout, f"ab_out_bwd_{l}")
            tr.queue.append((("ab_w_out", i), matmul_tn(st["y"], dxm, f"ab_dwout_{l}")[0]))
            dz, dgc, dw, dsb, dvg, dvb, dcg, dcbn, dcb = ab_mid_backward(dycat, st["z"], st["gc"], sp, f"ab_mid_bwd_{l}")
            dz, dcw = tr.run(conv_backward, dgc, st["z"], dz, sp, f"ab_conv_bwd_{l}", send=True)
            small["a_spatial_w"][i], small["a_spatial_b"][i] = dw, dsb
            small["a_vnorm_g"][i], small["a_vnorm_b"][i] = dvg[0], dvb[0]
            small["b_norm_g"][i], small["b_norm_b"][i], small["b_conv_b"][i] = dcg[0], dcbn[0], dcb[0]
            conv_grads[i] = dcw[:CONV_W]
            dy, dg = matmul_nt_norm_backward(dz, tr.w["ab_w_in", i], st["x_in"], mix_g, dxm, f"ab_in_bwd_{l}")
            small["mix_norm_g"][l] = dg[0]
            last = []
            if l == 0 and tr.shards is not None:
                tr.shards["small_grads", 0] = _pack({n: jnp.stack(small[n]) for n in PACKED})
                tr.shards["spatial_grads", 0] = jnp.stack(small["a_spatial_w"]).astype(BF16)
                last = [("small_grads", 0), ("spatial_grads", 0)]
            dwin = tr.run(tn, st["h"], dz, f"ab_dwin_{l}", send=True, gather=last)[0]
            tr.queue.append((("ab_w_in", i), dwin))
        else:
            wout = tr.w["c_w_out", i]
            do = matmul_nt(dxm, wout, f"c_out_bwd_{l}")
            tr.queue.append((("c_w_out", i), matmul_tn(st["y"], dxm, f"c_dwout_{l}")[0]))
            stats = attention_delta(do, st["y"], st["lse"], f"c_delta_{l}")
            dk, dv, dq, dqp, dqn = tr.run(attention_backward, st["qk"], st["qkv"], do, stats, f"c_attn_bwd_{l}", send=True)
            dqkv, dgn = qk_prep_backward(dq, dqp, dqn, dk, dv, st["qkv"], tabs, st["gains"], f"c_prep_bwd_{l}")
            small["c_q_norm_g"][i] = dgn[0, 0, :HEAD_DIM] + dgn[0, 0, HEAD_DIM:]
            small["c_k_norm_g"][i] = dgn[1, 0, :HEAD_DIM] + dgn[1, 0, HEAD_DIM:]
            dy, dg = matmul_nt_norm_backward(dqkv, tr.w["c_w_qkv", i], st["x_in"], mix_g, dxm, f"c_qkv_bwd_{l}")
            small["mix_norm_g"][l] = dg[0]
            tr.queue.append((("c_w_qkv", i), matmul_tn(st["h"], dqkv, f"c_dwqkv_{l}", n_split=2)[0]))
    small = {n: jnp.stack(v) for n, v in small.items()}
    return loss_part, dy, small, jnp.stack(conv_grads)


def kernel(x, mix_norm_g, mlp_norm_g, mlp_w1, mlp_w2, ab_w_in, a_spatial_w, a_spatial_b, a_vnorm_g, a_vnorm_b, b_conv_w, b_conv_b, b_norm_g, b_norm_b, ab_w_out, c_w_qkv, c_q_norm_g, c_k_norm_g, c_w_out, loss_target, m_mix_norm_g, m_mlp_norm_g, m_mlp_w1, m_mlp_w2, m_ab_w_in, m_a_spatial_w, m_a_spatial_b, m_a_vnorm_g, m_a_vnorm_b, m_b_conv_w, m_b_conv_b, m_b_norm_g, m_b_norm_b, m_ab_w_out, m_c_w_qkv, m_c_q_norm_g, m_c_k_norm_g, m_c_w_out, v_mix_norm_g, v_mlp_norm_g, v_mlp_w1, v_mlp_w2, v_ab_w_in, v_a_spatial_w, v_a_spatial_b, v_a_vnorm_g, v_a_vnorm_b, v_b_conv_w, v_b_conv_b, v_b_norm_g, v_b_norm_b, v_ab_w_out, v_c_w_qkv, v_c_q_norm_g, v_c_k_norm_g, v_c_w_out):
    args = dict(locals())
    w = {n: args[n] for n in WEIGHTS}
    m = {n: args["m_" + n] for n in WEIGHTS}
    v = {n: args["v_" + n] for n in WEIGHTS}

    shards = {(n, l): w[n][l].astype(BF16) for n in BIG for l in range(w[n].shape[0])}
    shards["b_conv_w", 0] = w["b_conv_w"]
    tr = Traffic(shards)
    first = [("ab_w_in", 0), ("ab_w_out", 0), ("b_conv_w", 0)]
    tr.run(lambda comm: ([], run_copies(comm, "gather_first")), gather=first)
    conv = tr.w["b_conv_w", 0]
    conv_full = jnp.transpose(conv, (1, 2, 0, 3)).reshape(conv.shape[1], conv.shape[2], -1)

    loss_part, dx, small, conv_grad = forward_backward(x[0], loss_target[0], w, tr, conv_full)
    loss = lax.psum(jnp.sum(loss_part), ("x", "y", "c"))
    nl, kw, cw = conv_grad.shape
    conv_parts = jnp.transpose(conv_grad.reshape(nl, kw, N_DEV, cw // N_DEV), (2, 0, 1, 3))
    tr.queue.append((("b_conv_w", 0), conv_parts))

    grads, deltas, new_m, new_v = {}, {}, {}, {}

    def update(n, recvs):
        shape = w[n].shape
        flat = lambda t: t.reshape(-1, shape[-1])
        recvs = [t.reshape(N_DEV, -1, shape[-1]) for t in recvs]
        outs = tr.run(adamw_update, recvs, flat(w[n]), flat(m[n]), flat(v[n]), "adamw_" + n, send=True)
        grads[n], deltas[n], new_m[n], new_v[n] = (t.reshape(shape) for t in outs)

    for n in BIG:
        update(n, [tr.parts[n, l] for l in range(w[n].shape[0])])
    update("b_conv_w", [tr.parts["b_conv_w", 0]])
    update("a_spatial_w", [tr.w["spatial_grads", 0]])
    outs = adamw_update([tr.w["small_grads", 0]], _pack(w), _pack(m), _pack(v), "adamw_small")[0]
    for dst, t in zip((grads, deltas, new_m, new_v), outs):
        dst.update(_unpack(t, w))

    return (loss, dx[None], *[grads[n] for n in WEIGHTS], *[deltas[n] for n in WEIGHTS],
            *[new_m[n] for n in WEIGHTS], *[new_v[n] for n in WEIGHTS])
```

```python
import functools
import math

import jax
import jax.numpy as jnp
from jax import lax
from jax.experimental import pallas as pl
from jax.experimental.pallas import tpu as pltpu

F32, BF16 = jnp.float32, jnp.bfloat16
N_DEV = 8
EPS = 1e-6
NEG = -1e30
LANES = 128
HEAD_DIM = 64
CHUNK = 128
CONV_W = 31
CONV_HALO = 16
BAND = 64
DILATIONS = (1, 4, 16)
ROT_DIM = 16
ROPE_THETA = 500000.0
VMEM_LIMIT = 56 * 1024 * 1024
MLP_CHUNK = 1024
MLP_BWD_CHUNK = 512
MLP_BWD_ROWS = 1024
ADAM_LR, ADAM_B1, ADAM_B2, ADAM_EPS, ADAM_WD, ADAM_STEP = 0.001, 0.9, 0.999, 1e-08, 0.01, 10
MESH = pl.DeviceIdType.MESH


def _params(*sem):
    return pltpu.CompilerParams(dimension_semantics=sem, vmem_limit_bytes=VMEM_LIMIT)


def _dot(a, b):
    return jnp.dot(a, b, preferred_element_type=F32)


def _dot_nt(a, b):
    return lax.dot_general(a, b, (((1,), (1,)), ((), ())), preferred_element_type=F32)


def _dot_tn(a, b):
    return lax.dot_general(a, b, (((0,), (0,)), ((), ())), preferred_element_type=F32)


def _rms_r(x):
    return lax.rsqrt(jnp.mean(x * x, axis=-1, keepdims=True) + EPS)


def _sigmoid(x):
    return 1.0 / (1.0 + jnp.exp(-x))


_GK = math.sqrt(2.0 / math.pi)


def _gelu(x):
    return 0.5 * x * (1.0 + jnp.tanh(_GK * (x + 0.044715 * x * x * x)))


def _gelu_grad(x):
    t = jnp.tanh(_GK * (x + 0.044715 * x * x * x))
    return 0.5 * (1.0 + t) + 0.5 * x * (1.0 - t * t) * (_GK * (1.0 + 3.0 * 0.044715 * x * x))


def _seg_sum(x, bd):
    hi = x.astype(BF16)
    lo = (x - hi.astype(F32)).astype(BF16)
    return _dot(hi, bd) + _dot(lo, bd)


def _block_diag(n):
    i = lax.broadcasted_iota(jnp.int32, (n, n), 0) // HEAD_DIM
    j = lax.broadcasted_iota(jnp.int32, (n, n), 1) // HEAD_DIM
    return jnp.where(i == j, 1.0, 0.0).astype(BF16)


def _tile(s, cap):
    t = min(s, cap)
    assert s % t == 0
    return t


def _my_index():
    return 4 * lax.axis_index("x") + 2 * lax.axis_index("y") + lax.axis_index("c")


def _device(i):
    return (i // 4, (i // 2) % 2, i % 2)


_HBM = pl.BlockSpec(memory_space=pl.ANY)


class PeerCopies:
    def __init__(self, operands, modes):
        self.inputs, self.modes = list(operands), list(modes)
        self.out_shape = []
        for t, (kind, axis) in zip(self.inputs, self.modes):
            shape = list(t.shape)
            if kind == "gather":
                shape = [N_DEV] + shape if axis is None else shape[:axis] + [N_DEV * shape[axis]] + shape[axis + 1:]
            elif axis is not None:
                shape = [N_DEV] + shape[:axis] + [shape[axis] // N_DEV] + shape[axis + 1:]
            self.out_shape.append(jax.ShapeDtypeStruct(tuple(shape), t.dtype))
        n = len(self.inputs)
        self.scratch = [pltpu.SemaphoreType.DMA((n, N_DEV - 1)), pltpu.SemaphoreType.DMA((n, N_DEV - 1)),
                        pltpu.SemaphoreType.DMA((n,))]

    @staticmethod
    def _block(ref, axis, size, j):
        if axis is None:
            return ref.at[j]
        return ref.at[tuple([slice(None)] * axis + [pl.ds(j * size, size)])]

    def _copies(self, in_refs, out_refs, sems, arrivals):
        send_sems, recv_sems, local_sems = sems
        me = _my_index()
        local, sends, recvs = [], [], []
        for t, (src, dst) in enumerate(zip(in_refs, out_refs)):
            kind, axis = self.modes[t]
            if kind == "gather":
                size = None if axis is None else src.shape[axis]
                source = lambda j, src=src: src
                place = lambda j, dst=dst, axis=axis, size=size: self._block(dst, axis, size, j)
            else:
                size = None if axis is None else src.shape[axis] // N_DEV
                source = lambda j, src=src, axis=axis, size=size: self._block(src, axis, size, j)
                place = lambda j, dst=dst: dst.at[j]
            local.append(pltpu.make_async_copy(source(me), place(me), local_sems.at[t]))
            for k in range(N_DEV - 1):
                to, frm = (me + k + 1) % N_DEV, (me + N_DEV - k - 1) % N_DEV
                sends.append(pltpu.make_async_remote_copy(
                    src_ref=source(to), dst_ref=place(me), send_sem=send_sems.at[t, k], recv_sem=recv_sems.at[t, k],
                    device_id=_device(to), device_id_type=MESH))
                if arrivals:
                    recvs.append(pltpu.make_async_remote_copy(
                        src_ref=source(me), dst_ref=place(frm), send_sem=send_sems.at[t, k], recv_sem=recv_sems.at[t, k],
                        device_id=_device(frm), device_id_type=MESH))
        return local, sends, recvs

    def start(self, in_refs, out_refs, sems):
        local, sends, _ = self._copies(in_refs, out_refs, sems, False)
        for cp in local + sends:
            cp.start()

    def finish(self, in_refs, out_refs, sems):
        local, sends, recvs = self._copies(in_refs, out_refs, sems, True)
        for cp in recvs:
            cp.wait_recv()
        for cp in sends:
            cp.wait_send()
        for cp in local:
            cp.wait()


def _pallas(body, *, name, args, in_specs, out_specs, out_shape, grid=(), scratch=(), sem=(), comm=None, aliases=None):
    n_in, n_out, n_scr = len(args), len(out_shape), len(scratch)
    if comm is None:
        outs = pl.pallas_call(
            body, name=name, grid=grid, in_specs=in_specs, out_specs=out_specs, out_shape=out_shape,
            scratch_shapes=list(scratch), input_output_aliases=aliases or {}, compiler_params=_params(*sem))(*args)
        return list(outs), []
    ci, co = len(comm.inputs), len(comm.out_shape)

    def hosted(*refs):
        ins, cins = refs[:n_in], refs[n_in:n_in + ci]
        outs, couts = refs[n_in + ci:n_in + ci + n_out], refs[n_in + ci + n_out:n_in + ci + n_out + co]
        rest = refs[n_in + ci + n_out + co:]
        scr, sems = rest[:n_scr], rest[n_scr:]
        if not grid:
            comm.start(cins, couts, sems)
            comm.finish(cins, couts, sems)
            return
        first = last = None
        for axis, size in enumerate(grid):
            f, l = pl.program_id(axis) == 0, pl.program_id(axis) == size - 1
            first, last = (f, l) if first is None else (first & f, last & l)
        pl.when(first)(lambda: comm.start(cins, couts, sems))
        body(*ins, *outs, *scr)
        pl.when(last)(lambda: comm.finish(cins, couts, sems))

    outs = pl.pallas_call(
        hosted, name=name, grid=grid, in_specs=list(in_specs) + [_HBM] * ci, out_specs=list(out_specs) + [_HBM] * co,
        out_shape=list(out_shape) + comm.out_shape, scratch_shapes=list(scratch) + comm.scratch,
        input_output_aliases=aliases or {}, compiler_params=_params(*["arbitrary"] * len(grid)))(*args, *comm.inputs)
    return list(outs[:n_out]), list(outs[n_out:])


def run_copies(comm, name):
    return _pallas(None, name=name, args=[], in_specs=[], out_specs=[], out_shape=[], comm=comm)[1]


def norm_matmul(x, g, wg, name, comm=None):
    s, d = x.shape
    n = wg.shape[-1]
    ns = 1024 if n % 1024 == 0 else n // 4
    tm = _tile(s, 1024)

    def body(x_ref, g_ref, w_ref, z_ref, h_ref):
        @pl.when(pl.program_id(1) == 0)
        def _():
            xv = x_ref[...]
            h_ref[...] = (xv * _rms_r(xv) * g_ref[...]).astype(BF16)
        z_ref[...] = _dot(h_ref[...], w_ref[...])

    return _pallas(
        body, name=name, grid=(s // tm, n // ns), args=[x, g, wg], comm=comm,
        in_specs=[pl.BlockSpec((tm, d), lambda i, j: (i, 0)),
                  pl.BlockSpec((1, d), lambda i, j: (0, 0)),
                  pl.BlockSpec((d, ns), lambda i, j: (0, j))],
        out_specs=[pl.BlockSpec((tm, ns), lambda i, j: (i, j)),
                   pl.BlockSpec((tm, d), lambda i, j: (i, 0))],
        out_shape=[jax.ShapeDtypeStruct((s, n), F32), jax.ShapeDtypeStruct((s, d), BF16)],
        sem=("parallel", "arbitrary"))


def mlp_forward(x, g, w1g, w2g, name, comm=None):
    s, d = x.shape
    f = w1g.shape[-1]
    fs = MLP_CHUNK
    tm = _tile(s, 1024)

    def body(x_ref, g_ref, w1_ref, w2_ref, xo_ref, a_ref, h_ref):
        @pl.when(pl.program_id(1) == 0)
        def _():
            xv = x_ref[...]
            h_ref[...] = (xv * _rms_r(xv) * g_ref[...]).astype(BF16)
            xo_ref[...] = xv
        a = _dot(h_ref[...], w1_ref[...])
        a_ref[...] = a.astype(BF16)
        r = jnp.maximum(a, 0.0)
        xo_ref[...] += _dot((r * r).astype(BF16), w2_ref[...])

    return _pallas(
        body, name=name, grid=(s // tm, f // fs), args=[x, g, w1g, w2g], comm=comm,
        in_specs=[pl.BlockSpec((tm, d), lambda i, j: (i, 0)),
                  pl.BlockSpec((1, d), lambda i, j: (0, 0)),
                  pl.BlockSpec((d, fs), lambda i, j: (0, j)),
                  pl.BlockSpec((fs, d), lambda i, j: (j, 0))],
        out_specs=[pl.BlockSpec((tm, d), lambda i, j: (i, 0)),
                   pl.BlockSpec((tm, fs), lambda i, j: (i, j)),
                   pl.BlockSpec((tm, d), lambda i, j: (i, 0))],
        out_shape=[jax.ShapeDtypeStruct((s, d), F32), jax.ShapeDtypeStruct((s, f), BF16),
                   jax.ShapeDtypeStruct((s, d), BF16)],
        sem=("parallel", "arbitrary"))


def _norm_backward(dh, xv, g, dres):
    r = _rms_r(xv)
    xh = xv * r
    t = dh * g
    dx = dres + r * (t - xh * jnp.mean(t * xh, axis=-1, keepdims=True))
    return dx, jnp.sum(dh * xh, axis=0, keepdims=True)


def mlp_backward(dy, a, x, g, w1g, w2g, name, comm=None):
    s, d = x.shape
    f = w1g.shape[-1]
    fs = MLP_BWD_CHUNK
    tm = _tile(s, MLP_BWD_ROWS)

    def body(dy_ref, a_ref, x_ref, g_ref, w1_ref, w2_ref, dx_ref, da_ref, dg_ref, dyb_ref, dh_ref):
        i, j = pl.program_id(0), pl.program_id(1)

        @pl.when(j == 0)
        def _():
            dyb_ref[...] = dy_ref[...].astype(BF16)
            dh_ref[...] = jnp.zeros_like(dh_ref)

        dr = _dot_nt(dyb_ref[...], w2_ref[...])
        da = (dr * (2.0 * jnp.maximum(a_ref[...].astype(F32), 0.0))).astype(BF16)
        da_ref[...] = da
        dh_ref[...] += _dot_nt(da, w1_ref[...])

        @pl.when(j == f // fs - 1)
        def _():
            dx, dgp = _norm_backward(dh_ref[...], x_ref[...], g_ref[...], dy_ref[...])
            dx_ref[...] = dx

            @pl.when(i == 0)
            def _():
                dg_ref[...] = dgp

            @pl.when(i > 0)
            def _():
                dg_ref[...] += dgp

    return _pallas(
        body, name=name, grid=(s // tm, f // fs), args=[dy, a, x, g, w1g, w2g], comm=comm,
        in_specs=[pl.BlockSpec((tm, d), lambda i, j: (i, 0)),
                  pl.BlockSpec((tm, fs), lambda i, j: (i, j)),
                  pl.BlockSpec((tm, d), lambda i, j: (i, 0)),
                  pl.BlockSpec((1, d), lambda i, j: (0, 0)),
                  pl.BlockSpec((d, fs), lambda i, j: (0, j)),
                  pl.BlockSpec((fs, d), lambda i, j: (j, 0))],
        out_specs=[pl.BlockSpec((tm, d), lambda i, j: (i, 0)),
                   pl.BlockSpec((tm, fs), lambda i, j: (i, j)),
                   pl.BlockSpec((1, d), lambda i, j: (0, 0)),
                   pl.BlockSpec((tm, d), lambda i, j: (i, 0))],
        out_shape=[jax.ShapeDtypeStruct((s, d), F32), jax.ShapeDtypeStruct((s, f), BF16),
                   jax.ShapeDtypeStruct((1, d), F32), jax.ShapeDtypeStruct((s, d), BF16)],
        scratch=[pltpu.VMEM((tm, d), F32)],
        sem=("arbitrary", "arbitrary"))


def matmul_tn(a, b, name, m_split=1, n_split=1, relu2=False, comm=None):
    s, m = a.shape
    n = b.shape[1]
    ts = _tile(s, 2048)
    bm, bn = m // m_split, n // n_split
    a_map = lambda j, k: (k, j // n_split)
    b_map = lambda j, k: (k, j % n_split)

    def body(a_ref, b_ref, o_ref, acc_ref):
        k = pl.program_id(1)
        av = a_ref[...]
        if relu2:
            af = jnp.maximum(av.astype(F32), 0.0)
            av = af * af
        p = _dot_tn(av.astype(BF16), b_ref[...].astype(BF16))

        @pl.when(k == 0)
        def _():
            acc_ref[...] = p

        @pl.when(k > 0)
        def _():
            acc_ref[...] += p

        @pl.when(k == s // ts - 1)
        def _():
            o_ref[...] = acc_ref[...].astype(BF16)

    outs, couts = _pallas(
        body, name=name, grid=(m_split * n_split, s // ts), args=[a, b], comm=comm,
        in_specs=[pl.BlockSpec((ts, bm), a_map), pl.BlockSpec((ts, bn), b_map)],
        out_specs=[pl.BlockSpec((bm, bn), lambda j, k: (j // n_split, j % n_split))],
        out_shape=[jax.ShapeDtypeStruct((m, n), BF16)],
        scratch=[pltpu.VMEM((bm, bn), F32)],
        sem=("parallel", "arbitrary"))
    return outs[0], couts


def matmul_residual(x, y, w, name):
    s, n = x.shape
    k = y.shape[1]
    tm = _tile(s, 1024)

    def body(x_ref, y_ref, w_ref, o_ref):
        o_ref[...] = x_ref[...] + _dot(y_ref[...], w_ref[...])

    return pl.pallas_call(
        body, name=name, grid=(s // tm,),
        in_specs=[pl.BlockSpec((tm, n), lambda i: (i, 0)),
                  pl.BlockSpec((tm, k), lambda i: (i, 0)),
                  pl.BlockSpec((k, n), lambda i: (0, 0))],
        out_specs=pl.BlockSpec((tm, n), lambda i: (i, 0)),
        out_shape=jax.ShapeDtypeStruct((s, n), F32),
        compiler_params=_params("parallel"),
    )(x, y, w)


def matmul_nt(dy, wg, name):
    s, n = dy.shape
    k = wg.shape[0]
    tm = _tile(s, 1024)

    def body(dy_ref, w_ref, o_ref):
        o_ref[...] = _dot_nt(dy_ref[...].astype(BF16), w_ref[...])

    return pl.pallas_call(
        body, name=name, grid=(s // tm,),
        in_specs=[pl.BlockSpec((tm, n), lambda i: (i, 0)),
                  pl.BlockSpec((k, n), lambda i: (0, 0))],
        out_specs=pl.BlockSpec((tm, k), lambda i: (i, 0)),
        out_shape=jax.ShapeDtypeStruct((s, k), F32),
        compiler_params=_params("parallel"),
    )(dy, wg)


def matmul_nt_norm_backward(dz, wg, x, g, dres, name):
    s, d = x.shape
    n = wg.shape[-1]
    tm = _tile(s, 512)

    def body(dz_ref, w_ref, x_ref, g_ref, dres_ref, dx_ref, dg_ref):
        dx, dgp = _norm_backward(_dot_nt(dz_ref[...], w_ref[...]), x_ref[...], g_ref[...], dres_ref[...])
        dx_ref[...] = dx
        _accumulate(dg_ref, dgp, pl.program_id(0) == 0)

    return pl.pallas_call(
        body, name=name, grid=(s // tm,),
        in_specs=[pl.BlockSpec((tm, n), lambda i: (i, 0)),
                  pl.BlockSpec((d, n), lambda i: (0, 0)),
                  pl.BlockSpec((tm, d), lambda i: (i, 0)),
                  pl.BlockSpec((1, d), lambda i: (0, 0)),
                  pl.BlockSpec((tm, d), lambda i: (i, 0))],
        out_specs=[pl.BlockSpec((tm, d), lambda i: (i, 0)),
                   pl.BlockSpec((1, d), lambda i: (0, 0))],
        out_shape=[jax.ShapeDtypeStruct((s, d), F32), jax.ShapeDtypeStruct((1, d), F32)],
        compiler_params=_params("arbitrary"),
    )(dz, wg, x, g, dres)


def loss_and_grad(y, target, name):
    s, d = y.shape
    tm = _tile(s, 1024)

    def body(y_ref, t_ref, dy_ref, l_ref):
        e = y_ref[...] - t_ref[...]
        dy_ref[...] = e / d
        part = jnp.sum(e * e, axis=0, keepdims=True) * (0.5 / d)

        @pl.when(pl.program_id(0) == 0)
        def _():
            l_ref[...] = part

        @pl.when(pl.program_id(0) > 0)
        def _():
            l_ref[...] += part

    return pl.pallas_call(
        body, name=name, grid=(s // tm,),
        in_specs=[pl.BlockSpec((tm, d), lambda i: (i, 0)), pl.BlockSpec((tm, d), lambda i: (i, 0))],
        out_specs=[pl.BlockSpec((tm, d), lambda i: (i, 0)), pl.BlockSpec((1, d), lambda i: (0, 0))],
        out_shape=[jax.ShapeDtypeStruct((s, d), F32), jax.ShapeDtypeStruct((1, d), F32)],
        compiler_params=_params("arbitrary"),
    )(y, target)


def _layernorm(x, g, b):
    mu = jnp.mean(x, axis=-1, keepdims=True)
    xc = x - mu
    rstd = lax.rsqrt(jnp.mean(xc * xc, axis=-1, keepdims=True) + EPS)
    xn = xc * rstd
    return xn * g + b, xn, rstd


def _layernorm_backward(dy, xn, rstd, g):
    dxn = dy * g
    return rstd * (dxn - jnp.mean(dxn, axis=-1, keepdims=True) - xn * jnp.mean(dxn * xn, axis=-1, keepdims=True))


def _group_halves(x_ref, jp, nch):
    blk = jnp.concatenate([x_ref[c * CHUNK:(c + 1) * CHUNK, jp * LANES:(jp + 1) * LANES] for c in range(nch)], axis=1)
    low = (lax.broadcasted_iota(jnp.int32, blk.shape, 1) % LANES) < HEAD_DIM
    return jnp.where(low, blk, 0.0).astype(BF16), jnp.where(low, 0.0, blk).astype(BF16)


def _spatial_apply(src_ref, w_ref, dst_ref, nch, bias_ref=None):
    for jp in range(4):
        lo, hi = _group_halves(src_ref, jp, nch)
        r = _dot(w_ref[2 * jp], lo) + _dot(w_ref[2 * jp + 1], hi)
        for c in range(nch):
            v = r[:, c * LANES:(c + 1) * LANES]
            if bias_ref is not None:
                v = v + bias_ref[:, jp * LANES:(jp + 1) * LANES]
            dst_ref[c * CHUNK:(c + 1) * CHUNK, jp * LANES:(jp + 1) * LANES] = v


def _glu(zb):
    w = zb.shape[1] // 2
    return zb[:, :w] * _sigmoid(zb[:, w:])


def _fill_padded(pad_ref, prev, cur, nxt, i, nt, tm):
    pad_ref[0:CONV_HALO, :] = jnp.where(i > 0, prev, 0.0)
    pad_ref[CONV_HALO:CONV_HALO + tm, :] = cur
    pad_ref[CONV_HALO + tm:2 * CONV_HALO + tm, :] = jnp.where(i < nt - 1, nxt, 0.0)


def _halo_specs(tm, s, width, col):
    hb, nhb = tm // CONV_HALO, s // CONV_HALO
    return [pl.BlockSpec((tm, width), lambda i: (i, col)),
            pl.BlockSpec((CONV_HALO, width), lambda i: (jnp.maximum(i * hb - 1, 0), col)),
            pl.BlockSpec((CONV_HALO, width), lambda i: (jnp.minimum((i + 1) * hb, nhb - 1), col))]


def _const_spec(shape):
    nd = len(shape)
    return pl.BlockSpec(shape, lambda i: (0,) * nd)


SUBLANES = 8


def _shift_scratch(tm, width):
    return pltpu.VMEM((SUBLANES - 1, tm + 2 * CONV_HALO - SUBLANES, width), F32)


def _fill_shifts(sh_ref, pad_ref):
    rows = sh_ref.shape[1]
    for sft in range(1, SUBLANES):
        sh_ref[sft - 1] = pad_ref[pl.ds(sft, rows), :]


def _tap(pad_ref, sh_ref, offset, cols):
    sft = offset % SUBLANES
    rows = pl.ds(offset - sft, CHUNK)
    return pad_ref[rows, cols] if sft == 0 else sh_ref[sft - 1, rows, cols]


def ab_mid_forward(z, sp, name, comm=None):
    s = z.shape[0]
    aw = z.shape[1] // 4
    tm = _tile(s, 512)
    nch, nt = tm // CHUNK, s // tm

    def body(zu_ref, zv_ref, zb_ref, zp_ref, zn_ref, w_ref, bias_ref, vg_ref, vb_ref, cw_ref, cb_ref, cg_ref, cbn_ref,
             y_ref, gc_ref, vl_ref, sv_ref, pad_ref, sh_ref):
        i = pl.program_id(0)
        vl_ref[...] = _layernorm(_gelu(zv_ref[...]), vg_ref[...], vb_ref[...])[0]
        _spatial_apply(vl_ref, w_ref, sv_ref, nch, bias_ref)
        y_ref[:, :aw] = (_gelu(zu_ref[...]) * sv_ref[...]).astype(BF16)

        _fill_padded(pad_ref, _glu(zp_ref[...]), _glu(zb_ref[...]), _glu(zn_ref[...]), i, nt, tm)
        _fill_shifts(sh_ref, pad_ref)
        for rb in range(tm // CHUNK):
            for lb in range(aw // LANES):
                cols = pl.ds(lb * LANES, LANES)
                acc = jnp.broadcast_to(cb_ref[:, cols], (CHUNK, LANES))
                for k in range(CONV_W):
                    acc = acc + cw_ref[k:k + 1, cols] * _tap(pad_ref, sh_ref, rb * CHUNK + CONV_HALO - CONV_W // 2 + k, cols)
                gc_ref[rb * CHUNK:(rb + 1) * CHUNK, cols] = acc
        yl = _layernorm(gc_ref[...], cg_ref[...], cbn_ref[...])[0]
        y_ref[:, aw:] = (yl * _sigmoid(yl)).astype(BF16)

    return _pallas(
        body, name=name, grid=(nt,), comm=comm,
        args=[z, z, z, z, z, sp["w"], sp["bias"], sp["vg"], sp["vb"], sp["cw"], sp["cb"], sp["cg"], sp["cbn"]],
        in_specs=[pl.BlockSpec((tm, aw), lambda i: (i, 0)), pl.BlockSpec((tm, aw), lambda i: (i, 1))]
        + _halo_specs(tm, s, 2 * aw, 1)
        + [_const_spec(sp["w"].shape), _const_spec(sp["bias"].shape)]
        + [_const_spec((1, aw))] * 2 + [_const_spec(sp["cw"].shape)] + [_const_spec((1, aw))] * 3,
        out_specs=[pl.BlockSpec((tm, 2 * aw), lambda i: (i, 0)), pl.BlockSpec((tm, aw), lambda i: (i, 0))],
        out_shape=[jax.ShapeDtypeStruct((s, 2 * aw), BF16), jax.ShapeDtypeStruct((s, aw), F32)],
        scratch=[pltpu.VMEM((tm, aw), F32), pltpu.VMEM((tm, aw), F32), pltpu.VMEM((tm + 2 * CONV_HALO, aw), F32),
                 _shift_scratch(tm, aw)],
        sem=("parallel",))


def _accumulate(ref, val, first):
    @pl.when(first)
    def _():
        ref[...] = val

    @pl.when(jnp.logical_not(first))
    def _():
        ref[...] += val


def ab_mid_backward(dy, z, gc, sp, name, comm=None):
    s = z.shape[0]
    aw = z.shape[1] // 4
    tm = _tile(s, 512)
    nch, nt = tm // CHUNK, s // tm

    def body(dya_ref, dyb_ref, zu_ref, zv_ref, gc_ref, w_ref, wt_ref, bias_ref, vg_ref, vb_ref, cg_ref, cbn_ref,
             dz_ref, dgc_ref, dw_ref, dsb_ref, dvg_ref, dvb_ref, dcg_ref, dcbn_ref, dcb_ref,
             vl_ref, sv_ref, dsv_ref, dvl_ref):
        first = pl.program_id(0) == 0
        zu, zv = zu_ref[...], zv_ref[...]
        u = _gelu(zu)
        vl, vn, vrstd = _layernorm(_gelu(zv), vg_ref[...], vb_ref[...])
        vl_ref[...] = vl
        _spatial_apply(vl_ref, w_ref, sv_ref, nch, bias_ref)
        dya = dya_ref[...]
        dz_ref[:, :aw] = (dya * sv_ref[...] * _gelu_grad(zu)).astype(BF16)
        dsv = dya * u
        dsv_ref[...] = dsv
        _spatial_apply(dsv_ref, wt_ref, dvl_ref, nch)

        for jp in range(4):
            dlo, dhi = _group_halves(dsv_ref, jp, nch)
            vlo, vhi = _group_halves(vl_ref, jp, nch)
            vall = vlo + vhi
            _accumulate(dw_ref.at[2 * jp], _dot_nt(dlo, vall), first)
            _accumulate(dw_ref.at[2 * jp + 1], _dot_nt(dhi, vall), first)
        rows = dsv[0:CHUNK]
        for c in range(1, nch):
            rows = rows + dsv[c * CHUNK:(c + 1) * CHUNK]
        grp = lax.broadcasted_iota(jnp.int32, (8, aw), 0) == lax.broadcasted_iota(jnp.int32, (8, aw), 1) // HEAD_DIM
        e = jnp.where(grp, 1.0, 0.0).astype(BF16)
        hi = rows.astype(BF16)
        r1 = rows - hi.astype(F32)
        mid = r1.astype(BF16)
        lo = (r1 - mid.astype(F32)).astype(BF16)
        _accumulate(dsb_ref, _dot_nt(e, hi) + _dot_nt(e, mid) + _dot_nt(e, lo), first)

        dvl = dvl_ref[...]
        _accumulate(dvg_ref, jnp.sum(dvl * vn, axis=0, keepdims=True), first)
        _accumulate(dvb_ref, jnp.sum(dvl, axis=0, keepdims=True), first)
        dz_ref[:, aw:] = (_layernorm_backward(dvl, vn, vrstd, vg_ref[...]) * _gelu_grad(zv)).astype(BF16)

        yl, yn, yrstd = _layernorm(gc_ref[...], cg_ref[...], cbn_ref[...])
        sg = _sigmoid(yl)
        dyl = dyb_ref[...] * (sg + yl * sg * (1.0 - sg))
        _accumulate(dcg_ref, jnp.sum(dyl * yn, axis=0, keepdims=True), first)
        _accumulate(dcbn_ref, jnp.sum(dyl, axis=0, keepdims=True), first)
        dgc = _layernorm_backward(dyl, yn, yrstd, cg_ref[...])
        dgc_ref[...] = dgc
        _accumulate(dcb_ref, jnp.sum(dgc, axis=0, keepdims=True), first)

    vec = jax.ShapeDtypeStruct((1, aw), F32)
    return _pallas(
        body, name=name, grid=(nt,), comm=comm,
        args=[dy, dy, z, z, gc, sp["w"], sp["wt"], sp["bias"], sp["vg"], sp["vb"], sp["cg"], sp["cbn"]],
        in_specs=[pl.BlockSpec((tm, aw), lambda i: (i, 0)), pl.BlockSpec((tm, aw), lambda i: (i, 1)),
                  pl.BlockSpec((tm, aw), lambda i: (i, 0)), pl.BlockSpec((tm, aw), lambda i: (i, 1)),
                  pl.BlockSpec((tm, aw), lambda i: (i, 0)),
                  _const_spec(sp["w"].shape), _const_spec(sp["w"].shape), _const_spec(sp["bias"].shape)]
        + [_const_spec((1, aw))] * 4,
        out_specs=[pl.BlockSpec((tm, 2 * aw), lambda i: (i, 0)), pl.BlockSpec((tm, aw), lambda i: (i, 0)),
                   _const_spec(sp["w"].shape), _const_spec((8, CHUNK))] + [_const_spec((1, aw))] * 5,
        out_shape=[jax.ShapeDtypeStruct((s, 4 * aw), BF16), jax.ShapeDtypeStruct((s, aw), F32),
                   jax.ShapeDtypeStruct(sp["w"].shape, F32), jax.ShapeDtypeStruct((8, CHUNK), F32)] + [vec] * 5,
        scratch=[pltpu.VMEM((tm, aw), F32)] * 4,
        sem=("arbitrary",))


def conv_backward(dgc, z, dz_in, sp, name, comm=None):
    s = z.shape[0]
    aw = z.shape[1] // 4
    tm = _tile(s, 512)
    nt = s // tm

    def body(d_ref, dp_ref, dn_ref, zb_ref, cw_ref, dzin_ref, dz_ref, dcw_ref, padd_ref, g_ref, dgg_ref, shd_ref):
        i = pl.program_id(0)
        _fill_padded(padd_ref, dp_ref[...], d_ref[...], dn_ref[...], i, nt, tm)
        _fill_shifts(shd_ref, padd_ref)
        g_ref[...] = _glu(zb_ref[...])

        @pl.when(i == 0)
        def _():
            dcw_ref[...] = jnp.zeros_like(dcw_ref)

        def grad_input(cols, rb):
            acc = jnp.zeros((CHUNK, LANES), F32)
            for k in range(CONV_W):
                acc = acc + cw_ref[k:k + 1, cols] * _tap(padd_ref, shd_ref, rb * CHUNK + CONV_HALO + CONV_W // 2 - k, cols)
            dgg_ref[rb * CHUNK:(rb + 1) * CHUNK, cols] = acc

        def grad_taps(cols, rb):
            gblk = g_ref[rb * CHUNK:(rb + 1) * CHUNK, cols]
            for k in range(CONV_W):
                prod = gblk * _tap(padd_ref, shd_ref, rb * CHUNK + CONV_HALO + CONV_W // 2 - k, cols)
                dcw_ref[k:k + 1, cols] += jnp.sum(prod, axis=0, keepdims=True)

        for lb in range(aw // LANES):
            for rb in range(tm // CHUNK):
                pl.when(i >= 0)(functools.partial(grad_input, pl.ds(lb * LANES, LANES), rb))
                pl.when(i >= 0)(functools.partial(grad_taps, pl.ds(lb * LANES, LANES), rb))

        zb = zb_ref[...]
        val, sg = zb[:, :aw], _sigmoid(zb[:, aw:])
        dgg = dgg_ref[...]
        dz_ref[:, :aw] = (dgg * sg).astype(BF16)
        dz_ref[:, aw:] = (dgg * val * sg * (1.0 - sg)).astype(BF16)

    return _pallas(
        body, name=name, grid=(nt,), args=[dgc, dgc, dgc, z, sp["cw"], dz_in], comm=comm,
        in_specs=_halo_specs(tm, s, aw, 0) + [pl.BlockSpec((tm, 2 * aw), lambda i: (i, 1))]
        + [_const_spec(sp["cw"].shape), pl.BlockSpec(memory_space=pl.ANY)],
        out_specs=[pl.BlockSpec((tm, 2 * aw), lambda i: (i, 1)), _const_spec(sp["cw"].shape)],
        out_shape=[jax.ShapeDtypeStruct((s, 4 * aw), BF16), jax.ShapeDtypeStruct(sp["cw"].shape, F32)],
        scratch=[pltpu.VMEM((tm + 2 * CONV_HALO, aw), F32), pltpu.VMEM((tm, aw), F32), pltpu.VMEM((tm, aw), F32),
                 _shift_scratch(tm, aw)],
        aliases={5: 0}, sem=("arbitrary",))


def rope_tables(s):
    pos = jnp.arange(s, dtype=F32)
    inv_freq = ROPE_THETA ** (-jnp.arange(0, ROT_DIM, 2, dtype=F32) / ROT_DIM)
    ang = pos[:, None] * inv_freq[None, :]
    cos, sin = jnp.cos(ang), jnp.sin(ang)
    half = ROT_DIM // 2
    rest = HEAD_DIM - ROT_DIM
    one, zero, zrest = jnp.ones((s, rest), F32), jnp.zeros((s, half), F32), jnp.zeros((s, rest), F32)
    c = jnp.concatenate([cos, cos, one], axis=1)
    s1 = jnp.concatenate([-sin, zero, zrest], axis=1)
    s2 = jnp.concatenate([zero, sin, zrest], axis=1)
    return tuple(jnp.tile(t, (1, LANES // HEAD_DIM)) for t in (c, s1, s2))


def qk_prep_forward(qkv, tabs, gains, name, comm=None):
    s, w3 = qkv.shape
    w = w3 // 3
    tm = _tile(s, 512)

    def body(x_ref, c_ref, s1_ref, s2_ref, g_ref, o_ref):
        bd = _block_diag(LANES)
        for rb in range(tm // CHUNK):
            rows = pl.ds(rb * CHUNK, CHUNK)
            c, s1, s2 = c_ref[rows, :], s1_ref[rows, :], s2_ref[rows, :]
            for b in range(w // LANES):
                cols = pl.ds(b * LANES, LANES)
                t = x_ref[rows, cols]
                r = lax.rsqrt(_seg_sum(t * t, bd) * (1.0 / HEAD_DIM) + EPS)
                y = t * r * g_ref[...]
                o_ref[rows, cols] = (y * c + pltpu.roll(y, LANES - ROT_DIM // 2, 1) * s1
                                     + pltpu.roll(y, ROT_DIM // 2, 1) * s2)

    tab = pl.BlockSpec((tm, LANES), lambda i, p: (i, 0))
    outs, couts = _pallas(
        body, name=name, grid=(s // tm, 2), args=[qkv, *tabs, gains], comm=comm,
        in_specs=[pl.BlockSpec((tm, w), lambda i, p: (i, p)), tab, tab, tab,
                  pl.BlockSpec((None, 1, LANES), lambda i, p: (p, 0, 0))],
        out_specs=[pl.BlockSpec((tm, w), lambda i, p: (i, p))],
        out_shape=[jax.ShapeDtypeStruct((s, 2 * w), F32)],
        sem=("parallel", "arbitrary"))
    return outs[0], couts


def qk_prep_backward(dq, dq_prev, dq_next, dk, dv, qkv, tabs, gains, name):
    s, w3 = qkv.shape
    w = w3 // 3
    tm = _tile(s, 512)
    t = _tile(s, ATT_TILE)
    per_tile, per_halo, tiles = t // tm, ATT_HALO // tm, s // t

    def neighbours(i):
        tile, c = i // per_tile, i % per_tile
        from_before = (c < per_halo) & (tile >= 1)
        from_after = (c >= per_tile - per_halo) & (tile + 1 < tiles)
        return (from_before, from_after, jnp.where(from_before, (tile - 1) * per_halo + c, 0),
                jnp.where(from_after, (tile + 1) * per_halo + c - (per_tile - per_halo), 0))

    def body(*refs):
        grads = ((refs[0], refs[1], refs[2]), (refs[3],), (refs[4],))
        x_ref, c_ref, s1_ref, s2_ref, g_ref, o_ref, dg_ref = refs[5:]
        part, first = pl.program_id(0), pl.program_id(1) == 0
        from_before, from_after, _, _ = neighbours(pl.program_id(1))

        def normed(ds):
            bd = _block_diag(LANES)
            acc = jnp.zeros((1, LANES), F32)
            for rb in range(tm // CHUNK):
                rows = pl.ds(rb * CHUNK, CHUNK)
                c, s1, s2 = c_ref[rows, :], s1_ref[rows, :], s2_ref[rows, :]
                for b in range(w // LANES):
                    cols = pl.ds(b * LANES, LANES)
                    dout = ds[0][rows, cols]
                    if len(ds) == 3:
                        dout = (dout + jnp.where(from_after, ds[1][rows, cols], 0.0)
                                + jnp.where(from_before, ds[2][rows, cols], 0.0))
                    dy = (dout * c + pltpu.roll(dout * s1, ROT_DIM // 2, 1)
                          + pltpu.roll(dout * s2, LANES - ROT_DIM // 2, 1))
                    t = x_ref[rows, cols]
                    r = lax.rsqrt(_seg_sum(t * t, bd) * (1.0 / HEAD_DIM) + EPS)
                    xh = t * r
                    acc = acc + jnp.sum(dy * xh, axis=0, keepdims=True)
                    tt = dy * g_ref[...]
                    o_ref[rows, cols] = (r * (tt - xh * (_seg_sum(tt * xh, bd) * (1.0 / HEAD_DIM)))).astype(BF16)
            _accumulate(dg_ref, acc, first)

        for p in range(2):
            pl.when(part == p)(functools.partial(normed, grads[p]))

        @pl.when(part == 2)
        def _():
            o_ref[...] = grads[2][0][...].astype(BF16)
            _accumulate(dg_ref, jnp.zeros((1, LANES), F32), first)

    def gspec(p):
        return pl.BlockSpec((tm, w), lambda q, i: (jnp.where(q == p, i, 0), 0))

    prev_spec = pl.BlockSpec((tm, w), lambda q, i: (jnp.where(q == 0, neighbours(i)[3], 0), 0))
    next_spec = pl.BlockSpec((tm, w), lambda q, i: (jnp.where(q == 0, neighbours(i)[2], 0), 0))
    tab = pl.BlockSpec((tm, LANES), lambda q, i: (i, 0))
    return pl.pallas_call(
        body, name=name, grid=(3, s // tm),
        in_specs=[gspec(0), prev_spec, next_spec, gspec(1), gspec(2)]
        + [pl.BlockSpec((tm, w), lambda q, i: (i, q)), tab, tab, tab,
           pl.BlockSpec((None, 1, LANES), lambda q, i: (q, 0, 0))],
        out_specs=[pl.BlockSpec((tm, w), lambda q, i: (i, q)), pl.BlockSpec((None, 1, LANES), lambda q, i: (q, 0, 0))],
        out_shape=[jax.ShapeDtypeStruct((s, w3), BF16), jax.ShapeDtypeStruct((3, 1, LANES), F32)],
        compiler_params=_params("arbitrary", "arbitrary"),
    )(dq, dq_prev, dq_next, dk, dv, qkv, *tabs, gains)


ATT_TILE = 2048
ATT_HALO = BAND * max(DILATIONS)
ROWS_PER_COPY = 256


def _att_specs(s, t, col_fn, halo=True):
    hb, nhb = t // ATT_HALO, s // ATT_HALO
    specs = [pl.BlockSpec((t, LANES), lambda hp, i: (i, col_fn(hp)))]
    if halo:
        specs += [pl.BlockSpec((ATT_HALO, LANES), lambda hp, i: (jnp.maximum(i * hb - 1, 0), col_fn(hp))),
                  pl.BlockSpec((ATT_HALO, LANES), lambda hp, i: (jnp.minimum((i + 1) * hb, nhb - 1), col_fn(hp)))]
    return specs


def _gather_rows(dst_ref, dst_row, src_ref, start, count, stride, scale=None):
    for c in range(0, count, ROWS_PER_COPY):
        m = min(ROWS_PER_COPY, count - c)
        v = src_ref[pl.ds(start + c * stride, m, stride=stride), :]
        if scale is not None:
            v = v * scale
        dst_ref[dst_row + c:dst_row + c + m, :] = v.astype(dst_ref.dtype)


SPLIT = 4


def _split_rows(tmp_ref, base, src_ref, rows):
    part = rows // SPLIT
    for b in range(SPLIT):
        _gather_rows(tmp_ref, base + b * part, src_ref, b, part, SPLIT)


def _stage(dst_ref, cur_ref, d, t, scale=None, tmp_ref=None):
    n = t // d
    if tmp_ref is None or d != SPLIT * SPLIT:
        for r in range(d):
            _gather_rows(dst_ref, r * n, cur_ref, r, n, d, scale)
        return
    _split_rows(tmp_ref, 0, cur_ref, t)
    for r in range(d):
        _gather_rows(dst_ref, r * n, tmp_ref, (r % SPLIT) * (t // SPLIT) + r // SPLIT, n, SPLIT, scale)


def _stage_window(dst_ref, refs, d, t, scale=None, edges=None, tmp_ref=None):
    cur_ref, prev_ref, next_ref = refs
    n = t // d
    nw = n + 2 * BAND
    if tmp_ref is None or d != SPLIT * SPLIT:
        for r in range(d):
            _gather_rows(dst_ref, r * nw, prev_ref, ATT_HALO - BAND * d + r, BAND, d, scale)
            _gather_rows(dst_ref, r * nw + BAND, cur_ref, r, n, d, scale)
            _gather_rows(dst_ref, r * nw + BAND + n, next_ref, r, BAND, d, scale)
    else:
        assert ATT_HALO == BAND * d
        _split_rows(tmp_ref, 0, cur_ref, t)
        _split_rows(tmp_ref, t, prev_ref, ATT_HALO)
        _split_rows(tmp_ref, t + ATT_HALO, next_ref, ATT_HALO)
        for r in range(d):
            a, b = r // SPLIT, r % SPLIT
            _gather_rows(dst_ref, r * nw, tmp_ref, t + b * (ATT_HALO // SPLIT) + a, BAND, SPLIT, scale)
            _gather_rows(dst_ref, r * nw + BAND, tmp_ref, b * (t // SPLIT) + a, n, SPLIT, scale)
            _gather_rows(dst_ref, r * nw + BAND + n, tmp_ref, t + ATT_HALO + b * (ATT_HALO // SPLIT) + a, BAND, SPLIT, scale)
    if edges is not None:
        first, last, value = edges
        fill = jnp.full((BAND, LANES), value, dst_ref.dtype)

        @pl.when(first)
        def _():
            for r in range(d):
                dst_ref[r * nw:r * nw + BAND, :] = fill

        @pl.when(last)
        def _():
            for r in range(d):
                dst_ref[r * nw + BAND + n:(r + 1) * nw, :] = fill


def _band_bias(rows, cols, centre_axis):
    shape = (rows // 2, cols)
    ctr = lax.broadcasted_iota(jnp.int32, shape, centre_axis)
    win = lax.broadcasted_iota(jnp.int32, shape, 1 - centre_axis)
    bias = jnp.where(jnp.abs(win - BAND - ctr) <= BAND, 0.0, NEG).astype(F32)
    return jnp.concatenate([bias, bias], axis=0)


def _window_bias(first_row, length):
    row = first_row + lax.broadcasted_iota(jnp.int32, (1, 2 * CHUNK), 1)
    return jnp.where((row >= 0) & (row < length), 0.0, NEG).astype(F32)


def _scatter_rows(dst_ref, src_ref, d, t, combine):
    n = t // d
    for r in range(d):
        for c in range(0, n, ROWS_PER_COPY):
            m = min(ROWS_PER_COPY, n - c)
            idx = pl.ds(r + c * d, m, stride=d)
            combine(idx, slice(r * n + c, r * n + c + m))


def _unit_rows(u, n):
    upr = n // CHUNK
    r = u // upr
    b = u - r * upr
    if isinstance(u, int):
        return u * CHUNK, (u + r) * CHUNK, b * CHUNK - BAND
    return pl.multiple_of(u * CHUNK, CHUNK), pl.multiple_of((u + r) * CHUNK, CHUNK), b * CHUNK - BAND


def _two_heads(x):
    head0 = lax.broadcasted_iota(jnp.int32, x.shape, 1) < HEAD_DIM
    zero = jnp.zeros_like(x)
    return jnp.concatenate([jnp.where(head0, x, zero), jnp.where(head0, zero, x)], axis=0)


def _merge_heads(x2):
    rows = x2.shape[0] // 2
    head0 = lax.broadcasted_iota(jnp.int32, (rows, LANES), 1) < HEAD_DIM
    return jnp.where(head0, jnp.broadcast_to(x2[:rows], (rows, LANES)), jnp.broadcast_to(x2[rows:], (rows, LANES)))


def _col(part):
    return lambda hp: part * 8 + hp


def attention_forward(qk, qkv, name, comm=None):
    s, w2 = qk.shape
    w = w2 // 2
    t = _tile(s, ATT_TILE)
    scale = HEAD_DIM ** -0.5

    def body(q_ref, k_ref, kp_ref, kn_ref, v_ref, vp_ref, vn_ref, o_ref, lse_ref,
             qs_ref, ks_ref, vs_ref, os_ref, ls_ref, or_ref, tmp_ref):
        i = pl.program_id(1)
        band = _band_bias(2 * CHUNK, 2 * CHUNK, 0)
        for pi, d in enumerate(DILATIONS):
            n = t // d
            _stage(qs_ref, q_ref, d, t, scale, tmp_ref=tmp_ref)
            _stage_window(ks_ref, (k_ref, kp_ref, kn_ref), d, t, tmp_ref=tmp_ref)
            _stage_window(vs_ref, (v_ref, vp_ref, vn_ref), d, t, tmp_ref=tmp_ref)

            for u in range(t // CHUNK):
                qrow, wrow, first = _unit_rows(u, n)
                kb, vb = ks_ref[pl.ds(wrow, 2 * CHUNK), :], vs_ref[pl.ds(wrow, 2 * CHUNK), :]
                sc = _dot_nt(_two_heads(qs_ref[pl.ds(qrow, CHUNK), :]), kb) + band
                if first < 0 or first + 2 * CHUNK > n:
                    sc = sc + _window_bias(i * n + first, s // d)
                m = jnp.max(sc, axis=1, keepdims=True)
                p = jnp.exp(sc - m)
                den = jnp.sum(p, axis=1, keepdims=True)
                os_ref[pl.ds(qrow, CHUNK), :] = _merge_heads(_dot(p.astype(BF16), vb) / den)
                ls_ref[pl.ds(qrow, CHUNK), :] = _merge_heads(m + jnp.log(den))

            if pi == 0:
                def assign(idx, rows):
                    or_ref[idx, :] = os_ref[rows, :]
                    lse_ref[idx, :] = ls_ref[rows, :]
                _scatter_rows(None, None, d, t, assign)
            else:
                def merge(idx, rows):
                    la, lb = lse_ref[idx, :], ls_ref[rows, :]
                    mx = jnp.maximum(la, lb)
                    wa, wb = jnp.exp(la - mx), jnp.exp(lb - mx)
                    den = wa + wb
                    or_ref[idx, :] = (wa * or_ref[idx, :] + wb * os_ref[rows, :]) / den
                    lse_ref[idx, :] = mx + jnp.log(den)
                _scatter_rows(None, None, d, t, merge)
        o_ref[...] = or_ref[...].astype(BF16)

    ospec = pl.BlockSpec((t, LANES), lambda hp, i: (i, hp))
    win_rows = t + 2 * ATT_HALO
    return _pallas(
        body, name=name, grid=(w // LANES, s // t), args=[qk, qk, qk, qk, qkv, qkv, qkv], comm=comm,
        in_specs=_att_specs(s, t, _col(0), halo=False) + _att_specs(s, t, _col(1)) + _att_specs(s, t, _col(2)),
        out_specs=[ospec, ospec],
        out_shape=[jax.ShapeDtypeStruct((s, w), BF16), jax.ShapeDtypeStruct((s, w), F32)],
        scratch=[pltpu.VMEM((t, LANES), BF16), pltpu.VMEM((win_rows, LANES), BF16), pltpu.VMEM((win_rows, LANES), BF16),
                 pltpu.VMEM((t, LANES), F32), pltpu.VMEM((t, LANES), F32), pltpu.VMEM((t, LANES), F32),
                 pltpu.VMEM((win_rows, LANES), F32)],
        sem=("parallel", "parallel"))


def attention_delta(do, o, lse, name):
    s, w = do.shape
    tm = _tile(s, 512)

    def body(do_ref, o_ref, lse_ref, st_ref):
        bd = _block_diag(LANES)
        lane = lax.broadcasted_iota(jnp.int32, (CHUNK, LANES), 1)
        for rb in range(tm // CHUNK):
            rows = pl.ds(rb * CHUNK, CHUNK)
            for b in range(w // LANES):
                cols = pl.ds(b * LANES, LANES)
                dl = _seg_sum(do_ref[rows, cols] * o_ref[rows, cols].astype(F32), bd)
                ls = lse_ref[rows, cols]
                st_ref[rows, cols] = jnp.where(lane % HEAD_DIM < HEAD_DIM // 2, ls, dl)

    spec = pl.BlockSpec((tm, w), lambda i: (i, 0))
    return pl.pallas_call(
        body, name=name, grid=(s // tm,), in_specs=[spec, spec, spec], out_specs=spec,
        out_shape=jax.ShapeDtypeStruct((s, w), F32), compiler_params=_params("parallel"),
    )(do, o, lse)


def attention_backward(qk, qkv, do, stats, name, comm=None):
    s, w2 = qk.shape
    w = w2 // 2
    t = _tile(s, ATT_TILE)
    tiles = s // t
    scale = HEAD_DIM ** -0.5

    def body(k_ref, v_ref, q_ref, qp_ref, qn_ref, do_ref, dop_ref, don_ref, st_ref, stp_ref, stn_ref,
             dk_ref, dv_ref, dq_ref, dqp_ref, dqn_ref, ks_ref, vs_ref, qs_ref, dos_ref, sts_ref, dks_ref, dvs_ref, dqw_ref):
        i = pl.program_id(1)
        key = lax.broadcasted_iota(jnp.int32, (CHUNK, 2 * CHUNK), 0)
        win_ = lax.broadcasted_iota(jnp.int32, (CHUNK, 2 * CHUNK), 1)
        half = jnp.where(jnp.abs(win_ - BAND - key) <= BAND, 0.0, NEG).astype(F32)
        band = jnp.concatenate([half, half], axis=1)
        edges = (i == 0, i == tiles - 1, -NEG)
        dqp_ref[...] = jnp.zeros_like(dqp_ref)
        dqn_ref[...] = jnp.zeros_like(dqn_ref)
        for pi, d in enumerate(DILATIONS):
            n = t // d
            nw = n + 2 * BAND
            _stage(ks_ref, k_ref, d, t, tmp_ref=dqw_ref)
            _stage(vs_ref, v_ref, d, t, tmp_ref=dqw_ref)
            _stage_window(qs_ref, (q_ref, qp_ref, qn_ref), d, t, scale, tmp_ref=dqw_ref)
            _stage_window(dos_ref, (do_ref, dop_ref, don_ref), d, t, tmp_ref=dqw_ref)
            _stage_window(sts_ref, (st_ref, stp_ref, stn_ref), d, t, edges=edges, tmp_ref=dqw_ref)

            for u in range(t // CHUNK):
                krow, wrow, first = _unit_rows(u, n)
                rows, win = pl.ds(krow, CHUNK), pl.ds(wrow, 2 * CHUNK)
                kb = ks_ref[rows, :]
                q2, do2 = _two_heads(qs_ref[win, :]), _two_heads(dos_ref[win, :])
                st = jnp.transpose(sts_ref[win, :])
                mid = HEAD_DIM // 2
                lse2 = jnp.concatenate([st[0:1, :], st[HEAD_DIM:HEAD_DIM + 1, :]], axis=1)
                dl2 = jnp.concatenate([st[mid:mid + 1, :], st[HEAD_DIM + mid:HEAD_DIM + mid + 1, :]], axis=1)
                p = jnp.exp(_dot_nt(kb, q2) + band - lse2)
                ds = (p * (_dot_nt(vs_ref[rows, :], do2) - dl2)).astype(BF16)
                dvs_ref[rows, :] = _dot(p.astype(BF16), do2)
                dks_ref[rows, :] = _dot(ds, q2)
                k2 = _two_heads(kb)
                dqw = (_dot_tn(ds[:, :2 * CHUNK], k2[:CHUNK]) + _dot_tn(ds[:, 2 * CHUNK:], k2[CHUNK:])) * scale
                if first < 0:
                    dqw_ref[pl.ds(wrow, CHUNK), :] = dqw[:CHUNK]
                else:
                    dqw_ref[pl.ds(wrow, CHUNK), :] += dqw[:CHUNK]
                dqw_ref[pl.ds(wrow + CHUNK, CHUNK), :] = dqw[CHUNK:]

            def add(idx, rows, pi=pi):
                dk_ref[idx, :] = dks_ref[rows, :] if pi == 0 else dk_ref[idx, :] + dks_ref[rows, :]
                dv_ref[idx, :] = dvs_ref[rows, :] if pi == 0 else dv_ref[idx, :] + dvs_ref[rows, :]
            _scatter_rows(None, None, d, t, add)

            for r in range(d):
                before = pl.ds(ATT_HALO - BAND * d + r, BAND, stride=d)
                after = pl.ds(r, BAND, stride=d)
                dqp_ref[before, :] += dqw_ref[r * nw:r * nw + BAND, :]
                dqn_ref[after, :] += dqw_ref[r * nw + BAND + n:(r + 1) * nw, :]
                for c in range(0, n, ROWS_PER_COPY):
                    m = min(ROWS_PER_COPY, n - c)
                    idx = pl.ds(r + c * d, m, stride=d)
                    val = dqw_ref[r * nw + BAND + c:r * nw + BAND + c + m, :]
                    dq_ref[idx, :] = val if pi == 0 else dq_ref[idx, :] + val

    ident = lambda hp: hp
    ospec = pl.BlockSpec((t, LANES), lambda hp, i: (i, hp))
    hspec = pl.BlockSpec((ATT_HALO, LANES), lambda hp, i: (i, hp))
    win_rows = t + 2 * ATT_HALO
    halo = jax.ShapeDtypeStruct((tiles * ATT_HALO, w), F32)
    return _pallas(
        body, name=name, grid=(w // LANES, tiles), comm=comm,
        args=[qk, qkv, qk, qk, qk, do, do, do, stats, stats, stats],
        in_specs=_att_specs(s, t, _col(1), halo=False) + _att_specs(s, t, _col(2), halo=False)
        + _att_specs(s, t, _col(0)) + _att_specs(s, t, ident) * 2,
        out_specs=[ospec, ospec, ospec, hspec, hspec],
        out_shape=[jax.ShapeDtypeStruct((s, w), F32)] * 3 + [halo, halo],
        scratch=[pltpu.VMEM((t, LANES), BF16), pltpu.VMEM((t, LANES), BF16),
                 pltpu.VMEM((win_rows, LANES), BF16), pltpu.VMEM((win_rows, LANES), BF16),
                 pltpu.VMEM((win_rows, LANES), F32),
                 pltpu.VMEM((t, LANES), F32), pltpu.VMEM((t, LANES), F32), pltpu.VMEM((win_rows, LANES), F32)],
        sem=("parallel", "parallel"))


def adamw_update(recvs, w, m, v, name, comm=None):
    nl = len(recvs)
    r, c = recvs[0].shape[1:]
    tr = 256 if (r > 256 and r % 256 == 0) else r
    nt = r // tr
    c1 = 1.0 - ADAM_B1 ** ADAM_STEP
    c2 = 1.0 - ADAM_B2 ** ADAM_STEP

    def body(*refs):
        g_refs = refs[:nl]
        w_ref, m_ref, v_ref, go_ref, d_ref, mo_ref, vo_ref = refs[nl:]

        def update(g_ref):
            g = g_ref[0].astype(F32)
            for j in range(1, N_DEV):
                g = g + g_ref[j].astype(F32)
            mn = ADAM_B1 * m_ref[...] + (1.0 - ADAM_B1) * g
            vn = ADAM_B2 * v_ref[...] + (1.0 - ADAM_B2) * (g * g)
            go_ref[...] = g
            mo_ref[...] = mn
            vo_ref[...] = vn
            d_ref[...] = -ADAM_LR * ((mn / c1) / (jnp.sqrt(vn / c2) + ADAM_EPS) + ADAM_WD * w_ref[...])

        for layer in range(nl):
            pl.when(pl.program_id(0) == layer)(functools.partial(update, g_refs[layer]))

    def gspec(layer):
        return pl.BlockSpec((N_DEV, tr, c), lambda l, i: (0, jnp.where(l == layer, i, 0), 0))

    spec = pl.BlockSpec((tr, c), lambda l, i: (l * nt + i, 0))
    return _pallas(
        body, name=name, grid=(nl, nt), args=[*recvs, w, m, v], comm=comm,
        in_specs=[gspec(layer) for layer in range(nl)] + [spec, spec, spec],
        out_specs=[spec] * 4, out_shape=[jax.ShapeDtypeStruct((nl * r, c), F32)] * 4,
        sem=("arbitrary", "arbitrary"))


BIG = ("mlp_w1", "mlp_w2", "ab_w_in", "ab_w_out", "c_w_qkv", "c_w_out")
SMALL = ("mix_norm_g", "mlp_norm_g", "a_spatial_w", "a_spatial_b", "a_vnorm_g", "a_vnorm_b", "b_conv_b", "b_norm_g",
         "b_norm_b", "c_q_norm_g", "c_k_norm_g")
WEIGHTS = ("mix_norm_g", "mlp_norm_g", "mlp_w1", "mlp_w2", "ab_w_in", "a_spatial_w", "a_spatial_b", "a_vnorm_g",
           "a_vnorm_b", "b_conv_w", "b_conv_b", "b_norm_g", "b_norm_b", "ab_w_out", "c_w_qkv", "c_q_norm_g",
           "c_k_norm_g", "c_w_out")


def _mixer_params(p, conv_full, i):
    aw = p["a_vnorm_g"].shape[1]
    row = lambda t: t[i][None, :]
    return dict(
        w=p["a_spatial_w"][i].astype(BF16), wt=jnp.swapaxes(p["a_spatial_w"][i], 1, 2).astype(BF16),
        bias=jnp.repeat(p["a_spatial_b"][i].T, aw // p["a_spatial_b"].shape[1], axis=1),
        vg=row(p["a_vnorm_g"]), vb=row(p["a_vnorm_b"]), cw=jnp.pad(conv_full[i], ((0, 1), (0, 0))),
        cb=row(p["b_conv_b"]), cg=row(p["b_norm_g"]), cbn=row(p["b_norm_b"]))


def _head_gains(p, i):
    rep = LANES // HEAD_DIM
    return jnp.stack([jnp.tile(p["c_q_norm_g"][i], rep), jnp.tile(p["c_k_norm_g"][i], rep),
                      jnp.ones((LANES,), F32)])[:, None, :]


PACKED = tuple(n for n in SMALL if n != "a_spatial_w")


def _pack(d):
    flat = jnp.concatenate([d[n].reshape(-1) for n in PACKED])
    rows = -(-flat.shape[0] // (8 * LANES)) * 8
    return jnp.pad(flat, (0, rows * LANES - flat.shape[0])).reshape(rows, LANES)


def _unpack(packed, like):
    flat, out, pos = packed.reshape(-1), {}, 0
    for n in PACKED:
        size = math.prod(like[n].shape)
        out[n] = flat[pos:pos + size].reshape(like[n].shape)
        pos += size
    return out


class Traffic:
    def __init__(self, shards, full=()):
        self.shards, self.w, self.queue, self.parts = shards, dict(full), [], {}

    def run(self, fn, *args, gather=(), send=False, **kw):
        operands, flags, dest = [], [], []
        if self.shards is None:
            if send:
                self.parts.update(self.queue)
                self.queue = []
        else:
            for k in gather:
                if k not in self.w:
                    operands.append(self.shards[k])
                    flags.append(("gather", SHARD_AXIS.get(k[0])))
                    dest.append((self.w, k))
            if send:
                for k, t in self.queue:
                    operands.append(t)
                    flags.append(("scatter", SHARD_AXIS.get(k[0])))
                    dest.append((self.parts, k))
                self.queue = []
        outs, couts = fn(*args, comm=PeerCopies(operands, flags) if operands else None, **kw)
        for (table, k), t in zip(dest, couts):
            table[k] = t
        return outs

    def flush(self, name, extra=()):
        self.queue += list(extra)
        self.run(lambda comm: ([], run_copies(comm, name) if comm is not None else []), send=True)


SHARD_AXIS = {"mlp_w1": 1, "mlp_w2": 0, "ab_w_in": 1, "ab_w_out": 0, "c_w_qkv": 1, "c_w_out": 0}


def forward_backward(x, target, p, tr, conv_full):
    s, d = x.shape
    depth = p["mix_norm_g"].shape[0]
    tabs = rope_tables(s)
    saved = []
    for l in range(depth):
        i = l // 2
        mix_g, mlp_g = p["mix_norm_g"][l][None, :], p["mlp_norm_g"][l][None, :]
        st = dict(x_in=x)
        nxt = () if l + 1 == depth else ((("c_w_qkv", i), ("c_w_out", i)) if l % 2 == 0 else
                                        (("ab_w_in", i + 1), ("ab_w_out", i + 1)))
        if l % 2 == 0:
            sp = _mixer_params(p, conv_full, i)
            z, h = tr.run(norm_matmul, x, mix_g, tr.w["ab_w_in", i], f"ab_in_{l}", gather=[("mlp_w1", l)])
            ycat, gc = tr.run(ab_mid_forward, z, sp, f"ab_mid_{l}", gather=[("mlp_w2", l)])
            x = matmul_residual(x, ycat, tr.w["ab_w_out", i], f"ab_out_{l}")
            st.update(z=z, h=h, y=ycat, gc=gc, sp=sp)
        else:
            gains = _head_gains(p, i)
            qkv, h = norm_matmul(x, mix_g, tr.w["c_w_qkv", i], f"c_qkv_{l}")[0]
            qk = qk_prep_forward(qkv, tabs, gains, f"c_prep_{l}")[0]
            ahead = [("mlp_w1", l), ("mlp_w2", l)] + ([("mlp_w1", l + 1)] if l + 1 < depth else [])
            o, lse = tr.run(attention_forward, qk, qkv, f"c_attn_{l}", gather=ahead)
            x = matmul_residual(x, o, tr.w["c_w_out", i], f"c_out_{l}")
            st.update(qkv=qkv, h=h, qk=qk, y=o, lse=lse, gains=gains)
            nxt = nxt + ((("mlp_w2", l + 1),) if l + 1 < depth else ())
        st["x_mid"] = x
        x, a, h2 = tr.run(mlp_forward, x, mlp_g, tr.w["mlp_w1", l], tr.w["mlp_w2", l], f"mlp_{l}", gather=nxt)
        st.update(a=a, h2=h2)
        saved.append(st)

    dy, loss_part = loss_and_grad(x, target, "loss")

    def tn(*a, comm, **kw):
        out, couts = matmul_tn(*a, comm=comm, **kw)
        return [out], couts

    small = {n: [None] * p[n].shape[0] for n in SMALL}
    conv_grads = [None] * p["b_conv_b"].shape[0]
    for l in reversed(range(depth)):
        i, st = l // 2, saved[l]
        mix_g, mlp_g = p["mix_norm_g"][l][None, :], p["mlp_norm_g"][l][None, :]
        w1, w2 = tr.w["mlp_w1", l], tr.w["mlp_w2", l]
        dxm, da, dg, dyb = tr.run(mlp_backward, dy, st["a"], st["x_mid"], mlp_g, w1, w2, f"mlp_bwd_{l}", send=True)
        small["mlp_norm_g"][l] = dg[0]
        tr.queue.append((("mlp_w1", l), matmul_tn(st["h2"], da, f"mlp_dw1_{l}", n_split=2)[0]))
        tr.queue.append((("mlp_w2", l), matmul_tn(st["a"], dyb, f"mlp_dw2_{l}", m_split=4, relu2=True)[0]))
        if l % 2 == 0:
            sp = st["sp"]
            wout = tr.w["ab_w_out", i]
            dycat = matmul_nt(dxm, wout, f"ab_out_bwd_{l}")
            later = [tr.queue.pop(), (("ab_w_out", i), matmul_tn(st["y"], dxm, f"ab_dwout_{l}")[0])]
            dz, dgc, dw, dsb, dvg, dvb, dcg, dcbn, dcb = tr.run(
                ab_mid_backward, dycat, st["z"], st["gc"], sp, f"ab_mid_bwd_{l}", send=True)
            tr.queue += later
            dz, dcw = tr.run(conv_backward, dgc, st["z"], dz, sp, f"ab_conv_bwd_{l}", send=True)
            small["a_spatial_w"][i], small["a_spatial_b"][i] = dw, dsb
            small["a_vnorm_g"][i], small["a_vnorm_b"][i] = dvg[0], dvb[0]
            small["b_norm_g"][i], small["b_norm_b"][i], small["b_conv_b"][i] = dcg[0], dcbn[0], dcb[0]
            conv_grads[i] = dcw[:CONV_W]
            dy, dg = matmul_nt_norm_backward(dz, tr.w["ab_w_in", i], st["x_in"], mix_g, dxm, f"ab_in_bwd_{l}")
            small["mix_norm_g"][l] = dg[0]
            last = []
            if l == 0 and tr.shards is not None:
                tr.shards["small_grads", 0] = _pack({n: jnp.stack(small[n]) for n in PACKED})
                tr.shards["spatial_grads", 0] = jnp.stack(small["a_spatial_w"]).astype(BF16)
                last = [("small_grads", 0), ("spatial_grads", 0)]
            dwin = tr.run(tn, st["h"], dz, f"ab_dwin_{l}", send=True, gather=last)[0]
            tr.queue.append((("ab_w_in", i), dwin))
        else:
            wout = tr.w["c_w_out", i]
            do = matmul_nt(dxm, wout, f"c_out_bwd_{l}")
            tr.queue.append((("c_w_out", i), matmul_tn(st["y"], dxm, f"c_dwout_{l}")[0]))
            stats = attention_delta(do, st["y"], st["lse"], f"c_delta_{l}")
            dk, dv, dq, dqp, dqn = tr.run(attention_backward, st["qk"], st["qkv"], do, stats, f"c_attn_bwd_{l}", send=True)
            dqkv, dgn = qk_prep_backward(dq, dqp, dqn, dk, dv, st["qkv"], tabs, st["gains"], f"c_prep_bwd_{l}")
            small["c_q_norm_g"][i] = dgn[0, 0, :HEAD_DIM] + dgn[0, 0, HEAD_DIM:]
            small["c_k_norm_g"][i] = dgn[1, 0, :HEAD_DIM] + dgn[1, 0, HEAD_DIM:]
            dy, dg = matmul_nt_norm_backward(dqkv, tr.w["c_w_qkv", i], st["x_in"], mix_g, dxm, f"c_qkv_bwd_{l}")
            small["mix_norm_g"][l] = dg[0]
            tr.queue.append((("c_w_qkv", i), matmul_tn(st["h"], dqkv, f"c_dwqkv_{l}", n_split=2)[0]))
    small = {n: jnp.stack(v) for n, v in small.items()}
    return loss_part, dy, small, jnp.stack(conv_grads)


def kernel(x, mix_norm_g, mlp_norm_g, mlp_w1, mlp_w2, ab_w_in, a_spatial_w, a_spatial_b, a_vnorm_g, a_vnorm_b, b_conv_w, b_conv_b, b_norm_g, b_norm_b, ab_w_out, c_w_qkv, c_q_norm_g, c_k_norm_g, c_w_out, loss_target, m_mix_norm_g, m_mlp_norm_g, m_mlp_w1, m_mlp_w2, m_ab_w_in, m_a_spatial_w, m_a_spatial_b, m_a_vnorm_g, m_a_vnorm_b, m_b_conv_w, m_b_conv_b, m_b_norm_g, m_b_norm_b, m_ab_w_out, m_c_w_qkv, m_c_q_norm_g, m_c_k_norm_g, m_c_w_out, v_mix_norm_g, v_mlp_norm_g, v_mlp_w1, v_mlp_w2, v_ab_w_in, v_a_spatial_w, v_a_spatial_b, v_a_vnorm_g, v_a_vnorm_b, v_b_conv_w, v_b_conv_b, v_b_norm_g, v_b_norm_b, v_ab_w_out, v_c_w_qkv, v_c_q_norm_g, v_c_k_norm_g, v_c_w_out):
    args = dict(locals())
    w = {n: args[n] for n in WEIGHTS}
    m = {n: args["m_" + n] for n in WEIGHTS}
    v = {n: args["v_" + n] for n in WEIGHTS}

    shards = {(n, l): w[n][l].astype(BF16) for n in BIG for l in range(w[n].shape[0])}
    shards["b_conv_w", 0] = w["b_conv_w"]
    tr = Traffic(shards)
    first = [("ab_w_in", 0), ("ab_w_out", 0), ("b_conv_w", 0)]
    tr.run(lambda comm: ([], run_copies(comm, "gather_first")), gather=first)
    conv = tr.w["b_conv_w", 0]
    conv_full = jnp.transpose(conv, (1, 2, 0, 3)).reshape(conv.shape[1], conv.shape[2], -1)

    loss_part, dx, small, conv_grad = forward_backward(x[0], loss_target[0], w, tr, conv_full)
    loss = lax.psum(jnp.sum(loss_part), ("x", "y", "c"))
    nl, kw, cw = conv_grad.shape
    conv_parts = jnp.transpose(conv_grad.reshape(nl, kw, N_DEV, cw // N_DEV), (2, 0, 1, 3))
    tr.queue.append((("b_conv_w", 0), conv_parts))

    grads, deltas, new_m, new_v = {}, {}, {}, {}

    def update(n, recvs):
        shape = w[n].shape
        flat = lambda t: t.reshape(-1, shape[-1])
        recvs = [t.reshape(N_DEV, -1, shape[-1]) for t in recvs]
        outs = tr.run(adamw_update, recvs, flat(w[n]), flat(m[n]), flat(v[n]), "adamw_" + n, send=True)
        grads[n], deltas[n], new_m[n], new_v[n] = (t.reshape(shape) for t in outs)

    for n in BIG:
        update(n, [tr.parts[n, l] for l in range(w[n].shape[0])])
    update("b_conv_w", [tr.parts["b_conv_w", 0]])
    update("a_spatial_w", [tr.w["spatial_grads", 0]])
    outs = adamw_update([tr.w["small_grads", 0]], _pack(w), _pack(m), _pack(v), "adamw_small")[0]
    for dst, t in zip((grads, deltas, new_m, new_v), outs):
        dst.update(_unpack(t, w))

    return (loss, dx[None], *[grads[n] for n in WEIGHTS], *[deltas[n] for n in WEIGHTS],
            *[new_m[n] for n in WEIGHTS], *[new_v[n] for n in WEIGHTS])
```

```python
import functools
import math

import jax
import jax.numpy as jnp
from jax import lax
from jax.experimental import pallas as pl
from jax.experimental.pallas import tpu as pltpu

F32, BF16 = jnp.float32, jnp.bfloat16
N_DEV = 8
EPS = 1e-6
NEG = -1e30
LANES = 128
HEAD_DIM = 64
CHUNK = 128
CONV_W = 31
CONV_HALO = 16
BAND = 64
DILATIONS = (1, 4, 16)
ROT_DIM = 16
ROPE_THETA = 500000.0
VMEM_LIMIT = 56 * 1024 * 1024
MLP_CHUNK = 1024
MLP_BWD_CHUNK = 512
MLP_BWD_ROWS = 1024
ADAM_LR, ADAM_B1, ADAM_B2, ADAM_EPS, ADAM_WD, ADAM_STEP = 0.001, 0.9, 0.999, 1e-08, 0.01, 10
MESH = pl.DeviceIdType.MESH


def _params(*sem):
    return pltpu.CompilerParams(dimension_semantics=sem, vmem_limit_bytes=VMEM_LIMIT)


def _dot(a, b):
    return jnp.dot(a, b, preferred_element_type=F32)


def _dot_nt(a, b):
    return lax.dot_general(a, b, (((1,), (1,)), ((), ())), preferred_element_type=F32)


def _dot_tn(a, b):
    return lax.dot_general(a, b, (((0,), (0,)), ((), ())), preferred_element_type=F32)


def _rms_r(x):
    return lax.rsqrt(jnp.mean(x * x, axis=-1, keepdims=True) + EPS)


def _sigmoid(x):
    return 1.0 / (1.0 + jnp.exp(-x))


_GK = math.sqrt(2.0 / math.pi)


def _gelu(x):
    return 0.5 * x * (1.0 + jnp.tanh(_GK * (x + 0.044715 * x * x * x)))


def _gelu_grad(x):
    t = jnp.tanh(_GK * (x + 0.044715 * x * x * x))
    return 0.5 * (1.0 + t) + 0.5 * x * (1.0 - t * t) * (_GK * (1.0 + 3.0 * 0.044715 * x * x))


def _seg_sum(x, bd):
    hi = x.astype(BF16)
    lo = (x - hi.astype(F32)).astype(BF16)
    return _dot(hi, bd) + _dot(lo, bd)


def _block_diag(n):
    i = lax.broadcasted_iota(jnp.int32, (n, n), 0) // HEAD_DIM
    j = lax.broadcasted_iota(jnp.int32, (n, n), 1) // HEAD_DIM
    return jnp.where(i == j, 1.0, 0.0).astype(BF16)


def _tile(s, cap):
    t = min(s, cap)
    assert s % t == 0
    return t


def _my_index():
    return 4 * lax.axis_index("x") + 2 * lax.axis_index("y") + lax.axis_index("c")


def _device(i):
    return (i // 4, (i // 2) % 2, i % 2)


_HBM = pl.BlockSpec(memory_space=pl.ANY)


class PeerCopies:
    def __init__(self, operands, modes):
        self.inputs, self.modes = list(operands), list(modes)
        self.out_shape = []
        for t, (kind, axis) in zip(self.inputs, self.modes):
            shape = list(t.shape)
            if kind == "gather":
                shape = [N_DEV] + shape if axis is None else shape[:axis] + [N_DEV * shape[axis]] + shape[axis + 1:]
            elif axis is not None:
                shape = [N_DEV] + shape[:axis] + [shape[axis] // N_DEV] + shape[axis + 1:]
            self.out_shape.append(jax.ShapeDtypeStruct(tuple(shape), t.dtype))
        n = len(self.inputs)
        self.scratch = [pltpu.SemaphoreType.DMA((n, N_DEV - 1)), pltpu.SemaphoreType.DMA((n, N_DEV - 1)),
                        pltpu.SemaphoreType.DMA((n,))]

    @staticmethod
    def _block(ref, axis, size, j):
        if axis is None:
            return ref.at[j]
        return ref.at[tuple([slice(None)] * axis + [pl.ds(j * size, size)])]

    def _copies(self, in_refs, out_refs, sems, arrivals):
        send_sems, recv_sems, local_sems = sems
        me = _my_index()
        local, sends, recvs = [], [], []
        for t, (src, dst) in enumerate(zip(in_refs, out_refs)):
            kind, axis = self.modes[t]
            if kind == "gather":
                continue
            size = None if axis is None else src.shape[axis] // N_DEV
            source = lambda j, src=src, axis=axis, size=size: self._block(src, axis, size, j)
            place = lambda j, dst=dst: dst.at[j]
            local.append(pltpu.make_async_copy(source(me), place(me), local_sems.at[t]))
            for k in range(N_DEV - 1):
                to, frm = (me + k + 1) % N_DEV, (me + N_DEV - k - 1) % N_DEV
                sends.append(pltpu.make_async_remote_copy(
                    src_ref=source(to), dst_ref=place(me), send_sem=send_sems.at[t, k], recv_sem=recv_sems.at[t, k],
                    device_id=_device(to), device_id_type=MESH))
                if arrivals:
                    recvs.append(pltpu.make_async_remote_copy(
                        src_ref=source(me), dst_ref=place(frm), send_sem=send_sems.at[t, k], recv_sem=recv_sems.at[t, k],
                        device_id=_device(frm), device_id_type=MESH))
        return local, sends, recvs

    _FLIPS = (1, 4, 2, 6)

    def _gathers(self, in_refs, out_refs, sems):
        send_sems, recv_sems, local_sems = sems
        me = _my_index()
        jobs = []
        for t, (src, dst) in enumerate(zip(in_refs, out_refs)):
            kind, axis = self.modes[t]
            if kind != "gather":
                continue
            size = None if axis is None else src.shape[axis]
            place = lambda j, dst=dst, axis=axis, size=size: self._block(dst, axis, size, j)

            def copy(k, source, block, to, t=t, place=place):
                return pltpu.make_async_remote_copy(
                    src_ref=source, dst_ref=place(block), send_sem=send_sems.at[t, k], recv_sem=recv_sems.at[t, k],
                    device_id=_device(to), device_id_type=MESH)

            local = pltpu.make_async_copy(src, place(me), local_sems.at[t])
            first = [copy(k, src, me, jnp.bitwise_xor(me, f)) for k, f in enumerate(self._FLIPS)]
            jobs.append((local, first, place, copy, src))
        return me, jobs

    def start(self, in_refs, out_refs, sems):
        local, sends, _ = self._copies(in_refs, out_refs, sems, False)
        for cp in local + sends:
            cp.start()
        _, jobs = self._gathers(in_refs, out_refs, sems)
        for local, first, _, _, _ in jobs:
            local.start()
            for cp in first:
                cp.start()

    def finish(self, in_refs, out_refs, sems):
        me, jobs = self._gathers(in_refs, out_refs, sems)
        sibling = jnp.bitwise_xor(me, 1)
        passed = []
        for _, _, place, copy, src in jobs:
            for j, f in enumerate(self._FLIPS[1:]):
                origin = jnp.bitwise_xor(me, f)
                copy(1 + j, src, origin, me).wait_recv()
                cp = copy(4 + j, place(origin), origin, sibling)
                cp.start()
                passed.append(cp)
        local, sends, recvs = self._copies(in_refs, out_refs, sems, True)
        for cp in recvs:
            cp.wait_recv()
        for _, _, place, copy, src in jobs:
            copy(0, src, sibling, me).wait_recv()
            for j, f in enumerate(self._FLIPS[1:]):
                copy(4 + j, src, jnp.bitwise_xor(sibling, f), me).wait_recv()
        for cp in sends + passed:
            cp.wait_send()
        for local_, first, _, _, _ in jobs:
            for cp in first:
                cp.wait_send()
            local_.wait()
        for cp in local:
            cp.wait()


def _pallas(body, *, name, args, in_specs, out_specs, out_shape, grid=(), scratch=(), sem=(), comm=None, aliases=None):
    n_in, n_out, n_scr = len(args), len(out_shape), len(scratch)
    if comm is None:
        outs = pl.pallas_call(
            body, name=name, grid=grid, in_specs=in_specs, out_specs=out_specs, out_shape=out_shape,
            scratch_shapes=list(scratch), input_output_aliases=aliases or {}, compiler_params=_params(*sem))(*args)
        return list(outs), []
    ci, co = len(comm.inputs), len(comm.out_shape)

    def hosted(*refs):
        ins, cins = refs[:n_in], refs[n_in:n_in + ci]
        outs, couts = refs[n_in + ci:n_in + ci + n_out], refs[n_in + ci + n_out:n_in + ci + n_out + co]
        rest = refs[n_in + ci + n_out + co:]
        scr, sems = rest[:n_scr], rest[n_scr:]
        if not grid:
            comm.start(cins, couts, sems)
            comm.finish(cins, couts, sems)
            return
        first = last = None
        for axis, size in enumerate(grid):
            f, l = pl.program_id(axis) == 0, pl.program_id(axis) == size - 1
            first, last = (f, l) if first is None else (first & f, last & l)
        pl.when(first)(lambda: comm.start(cins, couts, sems))
        body(*ins, *outs, *scr)
        pl.when(last)(lambda: comm.finish(cins, couts, sems))

    outs = pl.pallas_call(
        hosted, name=name, grid=grid, in_specs=list(in_specs) + [_HBM] * ci, out_specs=list(out_specs) + [_HBM] * co,
        out_shape=list(out_shape) + comm.out_shape, scratch_shapes=list(scratch) + comm.scratch,
        input_output_aliases=aliases or {}, compiler_params=_params(*["arbitrary"] * len(grid)))(*args, *comm.inputs)
    return list(outs[:n_out]), list(outs[n_out:])


def run_copies(comm, name):
    return _pallas(None, name=name, args=[], in_specs=[], out_specs=[], out_shape=[], comm=comm)[1]


def norm_matmul(x, g, wg, name, comm=None):
    s, d = x.shape
    n = wg.shape[-1]
    ns = 1024 if n % 1024 == 0 else n // 4
    tm = _tile(s, 1024)

    def body(x_ref, g_ref, w_ref, z_ref, h_ref):
        @pl.when(pl.program_id(1) == 0)
        def _():
            xv = x_ref[...]
            h_ref[...] = (xv * _rms_r(xv) * g_ref[...]).astype(BF16)
        z_ref[...] = _dot(h_ref[...], w_ref[...])

    return _pallas(
        body, name=name, grid=(s // tm, n // ns), args=[x, g, wg], comm=comm,
        in_specs=[pl.BlockSpec((tm, d), lambda i, j: (i, 0)),
                  pl.BlockSpec((1, d), lambda i, j: (0, 0)),
                  pl.BlockSpec((d, ns), lambda i, j: (0, j))],
        out_specs=[pl.BlockSpec((tm, ns), lambda i, j: (i, j)),
                   pl.BlockSpec((tm, d), lambda i, j: (i, 0))],
        out_shape=[jax.ShapeDtypeStruct((s, n), F32), jax.ShapeDtypeStruct((s, d), BF16)],
        sem=("parallel", "arbitrary"))


def mlp_forward(x, g, w1g, w2g, name, comm=None):
    s, d = x.shape
    f = w1g.shape[-1]
    fs = MLP_CHUNK
    tm = _tile(s, 1024)

    def body(x_ref, g_ref, w1_ref, w2_ref, xo_ref, a_ref, h_ref):
        @pl.when(pl.program_id(1) == 0)
        def _():
            xv = x_ref[...]
            h_ref[...] = (xv * _rms_r(xv) * g_ref[...]).astype(BF16)
            xo_ref[...] = xv
        a = _dot(h_ref[...], w1_ref[...])
        a_ref[...] = a.astype(BF16)
        r = jnp.maximum(a, 0.0)
        xo_ref[...] += _dot((r * r).astype(BF16), w2_ref[...])

    return _pallas(
        body, name=name, grid=(s // tm, f // fs), args=[x, g, w1g, w2g], comm=comm,
        in_specs=[pl.BlockSpec((tm, d), lambda i, j: (i, 0)),
                  pl.BlockSpec((1, d), lambda i, j: (0, 0)),
                  pl.BlockSpec((d, fs), lambda i, j: (0, j)),
                  pl.BlockSpec((fs, d), lambda i, j: (j, 0))],
        out_specs=[pl.BlockSpec((tm, d), lambda i, j: (i, 0)),
                   pl.BlockSpec((tm, fs), lambda i, j: (i, j)),
                   pl.BlockSpec((tm, d), lambda i, j: (i, 0))],
        out_shape=[jax.ShapeDtypeStruct((s, d), F32), jax.ShapeDtypeStruct((s, f), BF16),
                   jax.ShapeDtypeStruct((s, d), BF16)],
        sem=("parallel", "arbitrary"))


def _norm_backward(dh, xv, g, dres):
    r = _rms_r(xv)
    xh = xv * r
    t = dh * g
    dx = dres + r * (t - xh * jnp.mean(t * xh, axis=-1, keepdims=True))
    return dx, jnp.sum(dh * xh, axis=0, keepdims=True)


def mlp_backward(dy, a, x, g, w1g, w2g, name, comm=None):
    s, d = x.shape
    f = w1g.shape[-1]
    fs = MLP_BWD_CHUNK
    tm = _tile(s, MLP_BWD_ROWS)

    def body(dy_ref, a_ref, x_ref, g_ref, w1_ref, w2_ref, dx_ref, da_ref, dg_ref, dyb_ref, dh_ref):
        i, j = pl.program_id(0), pl.program_id(1)

        @pl.when(j == 0)
        def _():
            dyb_ref[...] = dy_ref[...].astype(BF16)
            dh_ref[...] = jnp.zeros_like(dh_ref)

        dr = _dot_nt(dyb_ref[...], w2_ref[...])
        da = (dr * (2.0 * jnp.maximum(a_ref[...].astype(F32), 0.0))).astype(BF16)
        da_ref[...] = da
        dh_ref[...] += _dot_nt(da, w1_ref[...])

        @pl.when(j == f // fs - 1)
        def _():
            dx, dgp = _norm_backward(dh_ref[...], x_ref[...], g_ref[...], dy_ref[...])
            dx_ref[...] = dx

            @pl.when(i == 0)
            def _():
                dg_ref[...] = dgp

            @pl.when(i > 0)
            def _():
                dg_ref[...] += dgp

    return _pallas(
        body, name=name, grid=(s // tm, f // fs), args=[dy, a, x, g, w1g, w2g], comm=comm,
        in_specs=[pl.BlockSpec((tm, d), lambda i, j: (i, 0)),
                  pl.BlockSpec((tm, fs), lambda i, j: (i, j)),
                  pl.BlockSpec((tm, d), lambda i, j: (i, 0)),
                  pl.BlockSpec((1, d), lambda i, j: (0, 0)),
                  pl.BlockSpec((d, fs), lambda i, j: (0, j)),
                  pl.BlockSpec((fs, d), lambda i, j: (j, 0))],
        out_specs=[pl.BlockSpec((tm, d), lambda i, j: (i, 0)),
                   pl.BlockSpec((tm, fs), lambda i, j: (i, j)),
                   pl.BlockSpec((1, d), lambda i, j: (0, 0)),
                   pl.BlockSpec((tm, d), lambda i, j: (i, 0))],
        out_shape=[jax.ShapeDtypeStruct((s, d), F32), jax.ShapeDtypeStruct((s, f), BF16),
                   jax.ShapeDtypeStruct((1, d), F32), jax.ShapeDtypeStruct((s, d), BF16)],
        scratch=[pltpu.VMEM((tm, d), F32)],
        sem=("arbitrary", "arbitrary"))


def matmul_tn(a, b, name, m_split=1, n_split=1, relu2=False, comm=None):
    s, m = a.shape
    n = b.shape[1]
    ts = _tile(s, 2048)
    bm, bn = m // m_split, n // n_split
    a_map = lambda j, k: (k, j // n_split)
    b_map = lambda j, k: (k, j % n_split)

    def body(a_ref, b_ref, o_ref, acc_ref):
        k = pl.program_id(1)
        av = a_ref[...]
        if relu2:
            af = jnp.maximum(av.astype(F32), 0.0)
            av = af * af
        p = _dot_tn(av.astype(BF16), b_ref[...].astype(BF16))

        @pl.when(k == 0)
        def _():
            acc_ref[...] = p

        @pl.when(k > 0)
        def _():
            acc_ref[...] += p

        @pl.when(k == s // ts - 1)
        def _():
            o_ref[...] = acc_ref[...].astype(BF16)

    outs, couts = _pallas(
        body, name=name, grid=(m_split * n_split, s // ts), args=[a, b], comm=comm,
        in_specs=[pl.BlockSpec((ts, bm), a_map), pl.BlockSpec((ts, bn), b_map)],
        out_specs=[pl.BlockSpec((bm, bn), lambda j, k: (j // n_split, j % n_split))],
        out_shape=[jax.ShapeDtypeStruct((m, n), BF16)],
        scratch=[pltpu.VMEM((bm, bn), F32)],
        sem=("parallel", "arbitrary"))
    return outs[0], couts


def matmul_residual(x, y, w, name):
    s, n = x.shape
    k = y.shape[1]
    tm = _tile(s, 1024)

    def body(x_ref, y_ref, w_ref, o_ref):
        o_ref[...] = x_ref[...] + _dot(y_ref[...], w_ref[...])

    return pl.pallas_call(
        body, name=name, grid=(s // tm,),
        in_specs=[pl.BlockSpec((tm, n), lambda i: (i, 0)),
                  pl.BlockSpec((tm, k), lambda i: (i, 0)),
                  pl.BlockSpec((k, n), lambda i: (0, 0))],
        out_specs=pl.BlockSpec((tm, n), lambda i: (i, 0)),
        out_shape=jax.ShapeDtypeStruct((s, n), F32),
        compiler_params=_params("parallel"),
    )(x, y, w)


def matmul_nt(dy, wg, name):
    s, n = dy.shape
    k = wg.shape[0]
    tm = _tile(s, 1024)

    def body(dy_ref, w_ref, o_ref):
        o_ref[...] = _dot_nt(dy_ref[...].astype(BF16), w_ref[...])

    return pl.pallas_call(
        body, name=name, grid=(s // tm,),
        in_specs=[pl.BlockSpec((tm, n), lambda i: (i, 0)),
                  pl.BlockSpec((k, n), lambda i: (0, 0))],
        out_specs=pl.BlockSpec((tm, k), lambda i: (i, 0)),
        out_shape=jax.ShapeDtypeStruct((s, k), F32),
        compiler_params=_params("parallel"),
    )(dy, wg)


def matmul_nt_norm_backward(dz, wg, x, g, dres, name):
    s, d = x.shape
    n = wg.shape[-1]
    tm = _tile(s, 512)

    def body(dz_ref, w_ref, x_ref, g_ref, dres_ref, dx_ref, dg_ref):
        dx, dgp = _norm_backward(_dot_nt(dz_ref[...], w_ref[...]), x_ref[...], g_ref[...], dres_ref[...])
        dx_ref[...] = dx
        _accumulate(dg_ref, dgp, pl.program_id(0) == 0)

    return pl.pallas_call(
        body, name=name, grid=(s // tm,),
        in_specs=[pl.BlockSpec((tm, n), lambda i: (i, 0)),
                  pl.BlockSpec((d, n), lambda i: (0, 0)),
                  pl.BlockSpec((tm, d), lambda i: (i, 0)),
                  pl.BlockSpec((1, d), lambda i: (0, 0)),
                  pl.BlockSpec((tm, d), lambda i: (i, 0))],
        out_specs=[pl.BlockSpec((tm, d), lambda i: (i, 0)),
                   pl.BlockSpec((1, d), lambda i: (0, 0))],
        out_shape=[jax.ShapeDtypeStruct((s, d), F32), jax.ShapeDtypeStruct((1, d), F32)],
        compiler_params=_params("arbitrary"),
    )(dz, wg, x, g, dres)


def loss_and_grad(y, target, name):
    s, d = y.shape
    tm = _tile(s, 1024)

    def body(y_ref, t_ref, dy_ref, l_ref):
        e = y_ref[...] - t_ref[...]
        dy_ref[...] = e / d
        part = jnp.sum(e * e, axis=0, keepdims=True) * (0.5 / d)

        @pl.when(pl.program_id(0) == 0)
        def _():
            l_ref[...] = part

        @pl.when(pl.program_id(0) > 0)
        def _():
            l_ref[...] += part

    return pl.pallas_call(
        body, name=name, grid=(s // tm,),
        in_specs=[pl.BlockSpec((tm, d), lambda i: (i, 0)), pl.BlockSpec((tm, d), lambda i: (i, 0))],
        out_specs=[pl.BlockSpec((tm, d), lambda i: (i, 0)), pl.BlockSpec((1, d), lambda i: (0, 0))],
        out_shape=[jax.ShapeDtypeStruct((s, d), F32), jax.ShapeDtypeStruct((1, d), F32)],
        compiler_params=_params("arbitrary"),
    )(y, target)


def _layernorm(x, g, b):
    mu = jnp.mean(x, axis=-1, keepdims=True)
    xc = x - mu
    rstd = lax.rsqrt(jnp.mean(xc * xc, axis=-1, keepdims=True) + EPS)
    xn = xc * rstd
    return xn * g + b, xn, rstd


def _layernorm_backward(dy, xn, rstd, g):
    dxn = dy * g
    return rstd * (dxn - jnp.mean(dxn, axis=-1, keepdims=True) - xn * jnp.mean(dxn * xn, axis=-1, keepdims=True))


def _group_halves(x_ref, jp, nch):
    blk = jnp.concatenate([x_ref[c * CHUNK:(c + 1) * CHUNK, jp * LANES:(jp + 1) * LANES] for c in range(nch)], axis=1)
    low = (lax.broadcasted_iota(jnp.int32, blk.shape, 1) % LANES) < HEAD_DIM
    return jnp.where(low, blk, 0.0).astype(BF16), jnp.where(low, 0.0, blk).astype(BF16)


def _spatial_apply(src_ref, w_ref, dst_ref, nch, bias_ref=None):
    for jp in range(4):
        lo, hi = _group_halves(src_ref, jp, nch)
        r = _dot(w_ref[2 * jp], lo) + _dot(w_ref[2 * jp + 1], hi)
        for c in range(nch):
            v = r[:, c * LANES:(c + 1) * LANES]
            if bias_ref is not None:
                v = v + bias_ref[:, jp * LANES:(jp + 1) * LANES]
            dst_ref[c * CHUNK:(c + 1) * CHUNK, jp * LANES:(jp + 1) * LANES] = v


def _glu(zb):
    w = zb.shape[1] // 2
    return zb[:, :w] * _sigmoid(zb[:, w:])


def _fill_padded(pad_ref, prev, cur, nxt, i, nt, tm):
    pad_ref[0:CONV_HALO, :] = jnp.where(i > 0, prev, 0.0)
    pad_ref[CONV_HALO:CONV_HALO + tm, :] = cur
    pad_ref[CONV_HALO + tm:2 * CONV_HALO + tm, :] = jnp.where(i < nt - 1, nxt, 0.0)


def _halo_specs(tm, s, width, col):
    hb, nhb = tm // CONV_HALO, s // CONV_HALO
    return [pl.BlockSpec((tm, width), lambda i: (i, col)),
            pl.BlockSpec((CONV_HALO, width), lambda i: (jnp.maximum(i * hb - 1, 0), col)),
            pl.BlockSpec((CONV_HALO, width), lambda i: (jnp.minimum((i + 1) * hb, nhb - 1), col))]


def _const_spec(shape):
    nd = len(shape)
    return pl.BlockSpec(shape, lambda i: (0,) * nd)


SUBLANES = 8


def _shift_scratch(tm, width):
    return pltpu.VMEM((SUBLANES - 1, tm + 2 * CONV_HALO - SUBLANES, width), F32)


def _fill_shifts(sh_ref, pad_ref):
    rows = sh_ref.shape[1]
    for sft in range(1, SUBLANES):
        sh_ref[sft - 1] = pad_ref[pl.ds(sft, rows), :]


def _tap(pad_ref, sh_ref, offset, cols):
    sft = offset % SUBLANES
    rows = pl.ds(offset - sft, CHUNK)
    return pad_ref[rows, cols] if sft == 0 else sh_ref[sft - 1, rows, cols]


def ab_mid_forward(z, sp, name, comm=None):
    s = z.shape[0]
    aw = z.shape[1] // 4
    tm = _tile(s, 512)
    nch, nt = tm // CHUNK, s // tm

    def body(zu_ref, zv_ref, zb_ref, zp_ref, zn_ref, w_ref, bias_ref, vg_ref, vb_ref, cw_ref, cb_ref, cg_ref, cbn_ref,
             y_ref, gc_ref, vl_ref, sv_ref, pad_ref, sh_ref):
        i = pl.program_id(0)
        vl_ref[...] = _layernorm(_gelu(zv_ref[...]), vg_ref[...], vb_ref[...])[0]
        _spatial_apply(vl_ref, w_ref, sv_ref, nch, bias_ref)
        y_ref[:, :aw] = (_gelu(zu_ref[...]) * sv_ref[...]).astype(BF16)

        _fill_padded(pad_ref, _glu(zp_ref[...]), _glu(zb_ref[...]), _glu(zn_ref[...]), i, nt, tm)
        _fill_shifts(sh_ref, pad_ref)
        for rb in range(tm // CHUNK):
            for lb in range(aw // LANES):
                cols = pl.ds(lb * LANES, LANES)
                acc = jnp.broadcast_to(cb_ref[:, cols], (CHUNK, LANES))
                for k in range(CONV_W):
                    acc = acc + cw_ref[k:k + 1, cols] * _tap(pad_ref, sh_ref, rb * CHUNK + CONV_HALO - CONV_W // 2 + k, cols)
                gc_ref[rb * CHUNK:(rb + 1) * CHUNK, cols] = acc
        yl = _layernorm(gc_ref[...], cg_ref[...], cbn_ref[...])[0]
        y_ref[:, aw:] = (yl * _sigmoid(yl)).astype(BF16)

    return _pallas(
        body, name=name, grid=(nt,), comm=comm,
        args=[z, z, z, z, z, sp["w"], sp["bias"], sp["vg"], sp["vb"], sp["cw"], sp["cb"], sp["cg"], sp["cbn"]],
        in_specs=[pl.BlockSpec((tm, aw), lambda i: (i, 0)), pl.BlockSpec((tm, aw), lambda i: (i, 1))]
        + _halo_specs(tm, s, 2 * aw, 1)
        + [_const_spec(sp["w"].shape), _const_spec(sp["bias"].shape)]
        + [_const_spec((1, aw))] * 2 + [_const_spec(sp["cw"].shape)] + [_const_spec((1, aw))] * 3,
        out_specs=[pl.BlockSpec((tm, 2 * aw), lambda i: (i, 0)), pl.BlockSpec((tm, aw), lambda i: (i, 0))],
        out_shape=[jax.ShapeDtypeStruct((s, 2 * aw), BF16), jax.ShapeDtypeStruct((s, aw), F32)],
        scratch=[pltpu.VMEM((tm, aw), F32), pltpu.VMEM((tm, aw), F32), pltpu.VMEM((tm + 2 * CONV_HALO, aw), F32),
                 _shift_scratch(tm, aw)],
        sem=("parallel",))


def _accumulate(ref, val, first):
    @pl.when(first)
    def _():
        ref[...] = val

    @pl.when(jnp.logical_not(first))
    def _():
        ref[...] += val


def ab_mid_backward(dy, z, gc, sp, name, comm=None):
    s = z.shape[0]
    aw = z.shape[1] // 4
    tm = _tile(s, 512)
    nch, nt = tm // CHUNK, s // tm

    def body(dya_ref, dyb_ref, zu_ref, zv_ref, gc_ref, w_ref, wt_ref, bias_ref, vg_ref, vb_ref, cg_ref, cbn_ref,
             dz_ref, dgc_ref, dw_ref, dsb_ref, dvg_ref, dvb_ref, dcg_ref, dcbn_ref, dcb_ref,
             vl_ref, sv_ref, dsv_ref, dvl_ref):
        first = pl.program_id(0) == 0
        zu, zv = zu_ref[...], zv_ref[...]
        u = _gelu(zu)
        vl, vn, vrstd = _layernorm(_gelu(zv), vg_ref[...], vb_ref[...])
        vl_ref[...] = vl
        _spatial_apply(vl_ref, w_ref, sv_ref, nch, bias_ref)
        dya = dya_ref[...]
        dz_ref[:, :aw] = (dya * sv_ref[...] * _gelu_grad(zu)).astype(BF16)
        dsv = dya * u
        dsv_ref[...] = dsv
        _spatial_apply(dsv_ref, wt_ref, dvl_ref, nch)

        for jp in range(4):
            dlo, dhi = _group_halves(dsv_ref, jp, nch)
            vlo, vhi = _group_halves(vl_ref, jp, nch)
            vall = vlo + vhi
            _accumulate(dw_ref.at[2 * jp], _dot_nt(dlo, vall), first)
            _accumulate(dw_ref.at[2 * jp + 1], _dot_nt(dhi, vall), first)
        rows = dsv[0:CHUNK]
        for c in range(1, nch):
            rows = rows + dsv[c * CHUNK:(c + 1) * CHUNK]
        grp = lax.broadcasted_iota(jnp.int32, (8, aw), 0) == lax.broadcasted_iota(jnp.int32, (8, aw), 1) // HEAD_DIM
        e = jnp.where(grp, 1.0, 0.0).astype(BF16)
        hi = rows.astype(BF16)
        r1 = rows - hi.astype(F32)
        mid = r1.astype(BF16)
        lo = (r1 - mid.astype(F32)).astype(BF16)
        _accumulate(dsb_ref, _dot_nt(e, hi) + _dot_nt(e, mid) + _dot_nt(e, lo), first)

        dvl = dvl_ref[...]
        _accumulate(dvg_ref, jnp.sum(dvl * vn, axis=0, keepdims=True), first)
        _accumulate(dvb_ref, jnp.sum(dvl, axis=0, keepdims=True), first)
        dz_ref[:, aw:] = (_layernorm_backward(dvl, vn, vrstd, vg_ref[...]) * _gelu_grad(zv)).astype(BF16)

        yl, yn, yrstd = _layernorm(gc_ref[...], cg_ref[...], cbn_ref[...])
        sg = _sigmoid(yl)
        dyl = dyb_ref[...] * (sg + yl * sg * (1.0 - sg))
        _accumulate(dcg_ref, jnp.sum(dyl * yn, axis=0, keepdims=True), first)
        _accumulate(dcbn_ref, jnp.sum(dyl, axis=0, keepdims=True), first)
        dgc = _layernorm_backward(dyl, yn, yrstd, cg_ref[...])
        dgc_ref[...] = dgc
        _accumulate(dcb_ref, jnp.sum(dgc, axis=0, keepdims=True), first)

    vec = jax.ShapeDtypeStruct((1, aw), F32)
    return _pallas(
        body, name=name, grid=(nt,), comm=comm,
        args=[dy, dy, z, z, gc, sp["w"], sp["wt"], sp["bias"], sp["vg"], sp["vb"], sp["cg"], sp["cbn"]],
        in_specs=[pl.BlockSpec((tm, aw), lambda i: (i, 0)), pl.BlockSpec((tm, aw), lambda i: (i, 1)),
                  pl.BlockSpec((tm, aw), lambda i: (i, 0)), pl.BlockSpec((tm, aw), lambda i: (i, 1)),
                  pl.BlockSpec((tm, aw), lambda i: (i, 0)),
                  _const_spec(sp["w"].shape), _const_spec(sp["w"].shape), _const_spec(sp["bias"].shape)]
        + [_const_spec((1, aw))] * 4,
        out_specs=[pl.BlockSpec((tm, 2 * aw), lambda i: (i, 0)), pl.BlockSpec((tm, aw), lambda i: (i, 0)),
                   _const_spec(sp["w"].shape), _const_spec((8, CHUNK))] + [_const_spec((1, aw))] * 5,
        out_shape=[jax.ShapeDtypeStruct((s, 4 * aw), BF16), jax.ShapeDtypeStruct((s, aw), F32),
                   jax.ShapeDtypeStruct(sp["w"].shape, F32), jax.ShapeDtypeStruct((8, CHUNK), F32)] + [vec] * 5,
        scratch=[pltpu.VMEM((tm, aw), F32)] * 4,
        sem=("arbitrary",))


def conv_backward(dgc, z, dz_in, sp, name, comm=None):
    s = z.shape[0]
    aw = z.shape[1] // 4
    tm = _tile(s, 512)
    nt = s // tm

    def body(d_ref, dp_ref, dn_ref, zb_ref, cw_ref, dzin_ref, dz_ref, dcw_ref, padd_ref, g_ref, dgg_ref, shd_ref):
        i = pl.program_id(0)
        _fill_padded(padd_ref, dp_ref[...], d_ref[...], dn_ref[...], i, nt, tm)
        _fill_shifts(shd_ref, padd_ref)
        g_ref[...] = _glu(zb_ref[...])

        @pl.when(i == 0)
        def _():
            dcw_ref[...] = jnp.zeros_like(dcw_ref)

        def grad_input(cols, rb):
            acc = jnp.zeros((CHUNK, LANES), F32)
            for k in range(CONV_W):
                acc = acc + cw_ref[k:k + 1, cols] * _tap(padd_ref, shd_ref, rb * CHUNK + CONV_HALO + CONV_W // 2 - k, cols)
            dgg_ref[rb * CHUNK:(rb + 1) * CHUNK, cols] = acc

        def grad_taps(cols, rb):
            gblk = g_ref[rb * CHUNK:(rb + 1) * CHUNK, cols]
            for k in range(CONV_W):
                prod = gblk * _tap(padd_ref, shd_ref, rb * CHUNK + CONV_HALO + CONV_W // 2 - k, cols)
                dcw_ref[k:k + 1, cols] += jnp.sum(prod, axis=0, keepdims=True)

        for lb in range(aw // LANES):
            for rb in range(tm // CHUNK):
                pl.when(i >= 0)(functools.partial(grad_input, pl.ds(lb * LANES, LANES), rb))
                pl.when(i >= 0)(functools.partial(grad_taps, pl.ds(lb * LANES, LANES), rb))

        zb = zb_ref[...]
        val, sg = zb[:, :aw], _sigmoid(zb[:, aw:])
        dgg = dgg_ref[...]
        dz_ref[:, :aw] = (dgg * sg).astype(BF16)
        dz_ref[:, aw:] = (dgg * val * sg * (1.0 - sg)).astype(BF16)

    return _pallas(
        body, name=name, grid=(nt,), args=[dgc, dgc, dgc, z, sp["cw"], dz_in], comm=comm,
        in_specs=_halo_specs(tm, s, aw, 0) + [pl.BlockSpec((tm, 2 * aw), lambda i: (i, 1))]
        + [_const_spec(sp["cw"].shape), pl.BlockSpec(memory_space=pl.ANY)],
        out_specs=[pl.BlockSpec((tm, 2 * aw), lambda i: (i, 1)), _const_spec(sp["cw"].shape)],
        out_shape=[jax.ShapeDtypeStruct((s, 4 * aw), BF16), jax.ShapeDtypeStruct(sp["cw"].shape, F32)],
        scratch=[pltpu.VMEM((tm + 2 * CONV_HALO, aw), F32), pltpu.VMEM((tm, aw), F32), pltpu.VMEM((tm, aw), F32),
                 _shift_scratch(tm, aw)],
        aliases={5: 0}, sem=("arbitrary",))


def rope_tables(s):
    pos = jnp.arange(s, dtype=F32)
    inv_freq = ROPE_THETA ** (-jnp.arange(0, ROT_DIM, 2, dtype=F32) / ROT_DIM)
    ang = pos[:, None] * inv_freq[None, :]
    cos, sin = jnp.cos(ang), jnp.sin(ang)
    half = ROT_DIM // 2
    rest = HEAD_DIM - ROT_DIM
    one, zero, zrest = jnp.ones((s, rest), F32), jnp.zeros((s, half), F32), jnp.zeros((s, rest), F32)
    c = jnp.concatenate([cos, cos, one], axis=1)
    s1 = jnp.concatenate([-sin, zero, zrest], axis=1)
    s2 = jnp.concatenate([zero, sin, zrest], axis=1)
    return tuple(jnp.tile(t, (1, LANES // HEAD_DIM)) for t in (c, s1, s2))


def qk_prep_forward(qkv, tabs, gains, name, comm=None):
    s, w3 = qkv.shape
    w = w3 // 3
    tm = _tile(s, 512)

    def body(x_ref, c_ref, s1_ref, s2_ref, g_ref, o_ref):
        bd = _block_diag(LANES)
        for rb in range(tm // CHUNK):
            rows = pl.ds(rb * CHUNK, CHUNK)
            c, s1, s2 = c_ref[rows, :], s1_ref[rows, :], s2_ref[rows, :]
            for b in range(w // LANES):
                cols = pl.ds(b * LANES, LANES)
                t = x_ref[rows, cols]
                r = lax.rsqrt(_seg_sum(t * t, bd) * (1.0 / HEAD_DIM) + EPS)
                y = t * r * g_ref[...]
                o_ref[rows, cols] = (y * c + pltpu.roll(y, LANES - ROT_DIM // 2, 1) * s1
                                     + pltpu.roll(y, ROT_DIM // 2, 1) * s2)

    tab = pl.BlockSpec((tm, LANES), lambda i, p: (i, 0))
    outs, couts = _pallas(
        body, name=name, grid=(s // tm, 2), args=[qkv, *tabs, gains], comm=comm,
        in_specs=[pl.BlockSpec((tm, w), lambda i, p: (i, p)), tab, tab, tab,
                  pl.BlockSpec((None, 1, LANES), lambda i, p: (p, 0, 0))],
        out_specs=[pl.BlockSpec((tm, w), lambda i, p: (i, p))],
        out_shape=[jax.ShapeDtypeStruct((s, 2 * w), F32)],
        sem=("parallel", "arbitrary"))
    return outs[0], couts


def qk_prep_backward(dq, dq_prev, dq_next, dk, dv, qkv, tabs, gains, name):
    s, w3 = qkv.shape
    w = w3 // 3
    tm = _tile(s, 512)
    t = _tile(s, ATT_TILE)
    per_tile, per_halo, tiles = t // tm, ATT_HALO // tm, s // t

    def neighbours(i):
        tile, c = i // per_tile, i % per_tile
        from_before = (c < per_halo) & (tile >= 1)
        from_after = (c >= per_tile - per_halo) & (tile + 1 < tiles)
        return (from_before, from_after, jnp.where(from_before, (tile - 1) * per_halo + c, 0),
                jnp.where(from_after, (tile + 1) * per_halo + c - (per_tile - per_halo), 0))

    def body(*refs):
        grads = ((refs[0], refs[1], refs[2]), (refs[3],), (refs[4],))
        x_ref, c_ref, s1_ref, s2_ref, g_ref, o_ref, dg_ref = refs[5:]
        part, first = pl.program_id(0), pl.program_id(1) == 0
        from_before, from_after, _, _ = neighbours(pl.program_id(1))

        def normed(ds):
            bd = _block_diag(LANES)
            acc = jnp.zeros((1, LANES), F32)
            for rb in range(tm // CHUNK):
                rows = pl.ds(rb * CHUNK, CHUNK)
                c, s1, s2 = c_ref[rows, :], s1_ref[rows, :], s2_ref[rows, :]
                for b in range(w // LANES):
                    cols = pl.ds(b * LANES, LANES)
                    dout = ds[0][rows, cols]
                    if len(ds) == 3:
                        dout = (dout + jnp.where(from_after, ds[1][rows, cols], 0.0)
                                + jnp.where(from_before, ds[2][rows, cols], 0.0))
                    dy = (dout * c + pltpu.roll(dout * s1, ROT_DIM // 2, 1)
                          + pltpu.roll(dout * s2, LANES - ROT_DIM // 2, 1))
                    t = x_ref[rows, cols]
                    r = lax.rsqrt(_seg_sum(t * t, bd) * (1.0 / HEAD_DIM) + EPS)
                    xh = t * r
                    acc = acc + jnp.sum(dy * xh, axis=0, keepdims=True)
                    tt = dy * g_ref[...]
                    o_ref[rows, cols] = (r * (tt - xh * (_seg_sum(tt * xh, bd) * (1.0 / HEAD_DIM)))).astype(BF16)
            _accumulate(dg_ref, acc, first)

        for p in range(2):
            pl.when(part == p)(functools.partial(normed, grads[p]))

        @pl.when(part == 2)
        def _():
            o_ref[...] = grads[2][0][...].astype(BF16)
            _accumulate(dg_ref, jnp.zeros((1, LANES), F32), first)

    def gspec(p):
        return pl.BlockSpec((tm, w), lambda q, i: (jnp.where(q == p, i, 0), 0))

    prev_spec = pl.BlockSpec((tm, w), lambda q, i: (jnp.where(q == 0, neighbours(i)[3], 0), 0))
    next_spec = pl.BlockSpec((tm, w), lambda q, i: (jnp.where(q == 0, neighbours(i)[2], 0), 0))
    tab = pl.BlockSpec((tm, LANES), lambda q, i: (i, 0))
    return pl.pallas_call(
        body, name=name, grid=(3, s // tm),
        in_specs=[gspec(0), prev_spec, next_spec, gspec(1), gspec(2)]
        + [pl.BlockSpec((tm, w), lambda q, i: (i, q)), tab, tab, tab,
           pl.BlockSpec((None, 1, LANES), lambda q, i: (q, 0, 0))],
        out_specs=[pl.BlockSpec((tm, w), lambda q, i: (i, q)), pl.BlockSpec((None, 1, LANES), lambda q, i: (q, 0, 0))],
        out_shape=[jax.ShapeDtypeStruct((s, w3), BF16), jax.ShapeDtypeStruct((3, 1, LANES), F32)],
        compiler_params=_params("arbitrary", "arbitrary"),
    )(dq, dq_prev, dq_next, dk, dv, qkv, *tabs, gains)


ATT_TILE = 2048
ATT_HALO = BAND * max(DILATIONS)
ROWS_PER_COPY = 256


def _att_specs(s, t, col_fn, halo=True):
    hb, nhb = t // ATT_HALO, s // ATT_HALO
    specs = [pl.BlockSpec((t, LANES), lambda hp, i: (i, col_fn(hp)))]
    if halo:
        specs += [pl.BlockSpec((ATT_HALO, LANES), lambda hp, i: (jnp.maximum(i * hb - 1, 0), col_fn(hp))),
                  pl.BlockSpec((ATT_HALO, LANES), lambda hp, i: (jnp.minimum((i + 1) * hb, nhb - 1), col_fn(hp)))]
    return specs


def _gather_rows(dst_ref, dst_row, src_ref, start, count, stride, scale=None):
    for c in range(0, count, ROWS_PER_COPY):
        m = min(ROWS_PER_COPY, count - c)
        v = src_ref[pl.ds(start + c * stride, m, stride=stride), :]
        if scale is not None:
            v = v * scale
        dst_ref[dst_row + c:dst_row + c + m, :] = v.astype(dst_ref.dtype)


SPLIT = 4


def _split_rows(tmp_ref, base, src_ref, rows):
    part = rows // SPLIT
    for b in range(SPLIT):
        _gather_rows(tmp_ref, base + b * part, src_ref, b, part, SPLIT)


def _stage(dst_ref, cur_ref, d, t, scale=None, tmp_ref=None):
    n = t // d
    if tmp_ref is None or d != SPLIT * SPLIT:
        for r in range(d):
            _gather_rows(dst_ref, r * n, cur_ref, r, n, d, scale)
        return
    _split_rows(tmp_ref, 0, cur_ref, t)
    for r in range(d):
        _gather_rows(dst_ref, r * n, tmp_ref, (r % SPLIT) * (t // SPLIT) + r // SPLIT, n, SPLIT, scale)


def _stage_window(dst_ref, refs, d, t, scale=None, edges=None, tmp_ref=None):
    cur_ref, prev_ref, next_ref = refs
    n = t // d
    nw = n + 2 * BAND
    if tmp_ref is None or d != SPLIT * SPLIT:
        for r in range(d):
            _gather_rows(dst_ref, r * nw, prev_ref, ATT_HALO - BAND * d + r, BAND, d, scale)
            _gather_rows(dst_ref, r * nw + BAND, cur_ref, r, n, d, scale)
            _gather_rows(dst_ref, r * nw + BAND + n, next_ref, r, BAND, d, scale)
    else:
        assert ATT_HALO == BAND * d
        _split_rows(tmp_ref, 0, cur_ref, t)
        _split_rows(tmp_ref, t, prev_ref, ATT_HALO)
        _split_rows(tmp_ref, t + ATT_HALO, next_ref, ATT_HALO)
        for r in range(d):
            a, b = r // SPLIT, r % SPLIT
            _gather_rows(dst_ref, r * nw, tmp_ref, t + b * (ATT_HALO // SPLIT) + a, BAND, SPLIT, scale)
            _gather_rows(dst_ref, r * nw + BAND, tmp_ref, b * (t // SPLIT) + a, n, SPLIT, scale)
            _gather_rows(dst_ref, r * nw + BAND + n, tmp_ref, t + ATT_HALO + b * (ATT_HALO // SPLIT) + a, BAND, SPLIT, scale)
    if edges is not None:
        first, last, value = edges
        fill = jnp.full((BAND, LANES), value, dst_ref.dtype)

        @pl.when(first)
        def _():
            for r in range(d):
                dst_ref[r * nw:r * nw + BAND, :] = fill

        @pl.when(last)
        def _():
            for r in range(d):
                dst_ref[r * nw + BAND + n:(r + 1) * nw, :] = fill


def _band_bias(rows, cols, centre_axis):
    shape = (rows // 2, cols)
    ctr = lax.broadcasted_iota(jnp.int32, shape, centre_axis)
    win = lax.broadcasted_iota(jnp.int32, shape, 1 - centre_axis)
    bias = jnp.where(jnp.abs(win - BAND - ctr) <= BAND, 0.0, NEG).astype(F32)
    return jnp.concatenate([bias, bias], axis=0)


def _window_bias(first_row, length):
    row = first_row + lax.broadcasted_iota(jnp.int32, (1, 2 * CHUNK), 1)
    return jnp.where((row >= 0) & (row < length), 0.0, NEG).astype(F32)


def _scatter_rows(dst_ref, src_ref, d, t, combine):
    n = t // d
    for r in range(d):
        for c in range(0, n, ROWS_PER_COPY):
            m = min(ROWS_PER_COPY, n - c)
            idx = pl.ds(r + c * d, m, stride=d)
            combine(idx, slice(r * n + c, r * n + c + m))


def _unit_rows(u, n):
    upr = n // CHUNK
    r = u // upr
    b = u - r * upr
    if isinstance(u, int):
        return u * CHUNK, (u + r) * CHUNK, b * CHUNK - BAND
    return pl.multiple_of(u * CHUNK, CHUNK), pl.multiple_of((u + r) * CHUNK, CHUNK), b * CHUNK - BAND


def _two_heads(x):
    head0 = lax.broadcasted_iota(jnp.int32, x.shape, 1) < HEAD_DIM
    zero = jnp.zeros_like(x)
    return jnp.concatenate([jnp.where(head0, x, zero), jnp.where(head0, zero, x)], axis=0)


def _merge_heads(x2):
    rows = x2.shape[0] // 2
    head0 = lax.broadcasted_iota(jnp.int32, (rows, LANES), 1) < HEAD_DIM
    return jnp.where(head0, jnp.broadcast_to(x2[:rows], (rows, LANES)), jnp.broadcast_to(x2[rows:], (rows, LANES)))


def _col(part):
    return lambda hp: part * 8 + hp


def attention_forward(qk, qkv, name, comm=None):
    s, w2 = qk.shape
    w = w2 // 2
    t = _tile(s, ATT_TILE)
    scale = HEAD_DIM ** -0.5

    def body(q_ref, k_ref, kp_ref, kn_ref, v_ref, vp_ref, vn_ref, o_ref, lse_ref,
             qs_ref, ks_ref, vs_ref, os_ref, ls_ref, or_ref, tmp_ref):
        i = pl.program_id(1)
        band = _band_bias(2 * CHUNK, 2 * CHUNK, 0)
        for pi, d in enumerate(DILATIONS):
            n = t // d
            _stage(qs_ref, q_ref, d, t, scale, tmp_ref=tmp_ref)
            _stage_window(ks_ref, (k_ref, kp_ref, kn_ref), d, t, tmp_ref=tmp_ref)
            _stage_window(vs_ref, (v_ref, vp_ref, vn_ref), d, t, tmp_ref=tmp_ref)

            for u in range(t // CHUNK):
                qrow, wrow, first = _unit_rows(u, n)
                kb, vb = ks_ref[pl.ds(wrow, 2 * CHUNK), :], vs_ref[pl.ds(wrow, 2 * CHUNK), :]
                sc = _dot_nt(_two_heads(qs_ref[pl.ds(qrow, CHUNK), :]), kb) + band
                if first < 0 or first + 2 * CHUNK > n:
                    sc = sc + _window_bias(i * n + first, s // d)
                m = jnp.max(sc, axis=1, keepdims=True)
                p = jnp.exp(sc - m)
                den = jnp.sum(p, axis=1, keepdims=True)
                os_ref[pl.ds(qrow, CHUNK), :] = _merge_heads(_dot(p.astype(BF16), vb) / den)
                ls_ref[pl.ds(qrow, CHUNK), :] = _merge_heads(m + jnp.log(den))

            if pi == 0:
                def assign(idx, rows):
                    or_ref[idx, :] = os_ref[rows, :]
                    lse_ref[idx, :] = ls_ref[rows, :]
                _scatter_rows(None, None, d, t, assign)
            else:
                def merge(idx, rows):
                    la, lb = lse_ref[idx, :], ls_ref[rows, :]
                    mx = jnp.maximum(la, lb)
                    wa, wb = jnp.exp(la - mx), jnp.exp(lb - mx)
                    den = wa + wb
                    or_ref[idx, :] = (wa * or_ref[idx, :] + wb * os_ref[rows, :]) / den
                    lse_ref[idx, :] = mx + jnp.log(den)
                _scatter_rows(None, None, d, t, merge)
        o_ref[...] = or_ref[...].astype(BF16)

    ospec = pl.BlockSpec((t, LANES), lambda hp, i: (i, hp))
    win_rows = t + 2 * ATT_HALO
    return _pallas(
        body, name=name, grid=(w // LANES, s // t), args=[qk, qk, qk, qk, qkv, qkv, qkv], comm=comm,
        in_specs=_att_specs(s, t, _col(0), halo=False) + _att_specs(s, t, _col(1)) + _att_specs(s, t, _col(2)),
        out_specs=[ospec, ospec],
        out_shape=[jax.ShapeDtypeStruct((s, w), BF16), jax.ShapeDtypeStruct((s, w), F32)],
        scratch=[pltpu.VMEM((t, LANES), BF16), pltpu.VMEM((win_rows, LANES), BF16), pltpu.VMEM((win_rows, LANES), BF16),
                 pltpu.VMEM((t, LANES), F32), pltpu.VMEM((t, LANES), F32), pltpu.VMEM((t, LANES), F32),
                 pltpu.VMEM((win_rows, LANES), F32)],
        sem=("parallel", "parallel"))


def attention_delta(do, o, lse, name):
    s, w = do.shape
    tm = _tile(s, 512)

    def body(do_ref, o_ref, lse_ref, st_ref):
        bd = _block_diag(LANES)
        lane = lax.broadcasted_iota(jnp.int32, (CHUNK, LANES), 1)
        for rb in range(tm // CHUNK):
            rows = pl.ds(rb * CHUNK, CHUNK)
            for b in range(w // LANES):
                cols = pl.ds(b * LANES, LANES)
                dl = _seg_sum(do_ref[rows, cols] * o_ref[rows, cols].astype(F32), bd)
                ls = lse_ref[rows, cols]
                st_ref[rows, cols] = jnp.where(lane % HEAD_DIM < HEAD_DIM // 2, ls, dl)

    spec = pl.BlockSpec((tm, w), lambda i: (i, 0))
    return pl.pallas_call(
        body, name=name, grid=(s // tm,), in_specs=[spec, spec, spec], out_specs=spec,
        out_shape=jax.ShapeDtypeStruct((s, w), F32), compiler_params=_params("parallel"),
    )(do, o, lse)


def attention_backward(qk, qkv, do, stats, name, comm=None):
    s, w2 = qk.shape
    w = w2 // 2
    t = _tile(s, ATT_TILE)
    tiles = s // t
    scale = HEAD_DIM ** -0.5

    def body(k_ref, v_ref, q_ref, qp_ref, qn_ref, do_ref, dop_ref, don_ref, st_ref, stp_ref, stn_ref,
             dk_ref, dv_ref, dq_ref, dqp_ref, dqn_ref, ks_ref, vs_ref, qs_ref, dos_ref, sts_ref, dks_ref, dvs_ref, dqw_ref):
        i = pl.program_id(1)
        key = lax.broadcasted_iota(jnp.int32, (CHUNK, 2 * CHUNK), 0)
        win_ = lax.broadcasted_iota(jnp.int32, (CHUNK, 2 * CHUNK), 1)
        half = jnp.where(jnp.abs(win_ - BAND - key) <= BAND, 0.0, NEG).astype(F32)
        band = jnp.concatenate([half, half], axis=1)
        edges = (i == 0, i == tiles - 1, -NEG)
        dqp_ref[...] = jnp.zeros_like(dqp_ref)
        dqn_ref[...] = jnp.zeros_like(dqn_ref)
        for pi, d in enumerate(DILATIONS):
            n = t // d
            nw = n + 2 * BAND
            _stage(ks_ref, k_ref, d, t, tmp_ref=dqw_ref)
            _stage(vs_ref, v_ref, d, t, tmp_ref=dqw_ref)
            _stage_window(qs_ref, (q_ref, qp_ref, qn_ref), d, t, scale, tmp_ref=dqw_ref)
            _stage_window(dos_ref, (do_ref, dop_ref, don_ref), d, t, tmp_ref=dqw_ref)
            _stage_window(sts_ref, (st_ref, stp_ref, stn_ref), d, t, edges=edges, tmp_ref=dqw_ref)

            for u in range(t // CHUNK):
                krow, wrow, first = _unit_rows(u, n)
                rows, win = pl.ds(krow, CHUNK), pl.ds(wrow, 2 * CHUNK)
                kb = ks_ref[rows, :]
                q2, do2 = _two_heads(qs_ref[win, :]), _two_heads(dos_ref[win, :])
                st = jnp.transpose(sts_ref[win, :])
                mid = HEAD_DIM // 2
                lse2 = jnp.concatenate([st[0:1, :], st[HEAD_DIM:HEAD_DIM + 1, :]], axis=1)
                dl2 = jnp.concatenate([st[mid:mid + 1, :], st[HEAD_DIM + mid:HEAD_DIM + mid + 1, :]], axis=1)
                p = jnp.exp(_dot_nt(kb, q2) + band - lse2)
                ds = (p * (_dot_nt(vs_ref[rows, :], do2) - dl2)).astype(BF16)
                dvs_ref[rows, :] = _dot(p.astype(BF16), do2)
                dks_ref[rows, :] = _dot(ds, q2)
                k2 = _two_heads(kb)
                dqw = (_dot_tn(ds[:, :2 * CHUNK], k2[:CHUNK]) + _dot_tn(ds[:, 2 * CHUNK:], k2[CHUNK:])) * scale
                if first < 0:
                    dqw_ref[pl.ds(wrow, CHUNK), :] = dqw[:CHUNK]
                else:
                    dqw_ref[pl.ds(wrow, CHUNK), :] += dqw[:CHUNK]
                dqw_ref[pl.ds(wrow + CHUNK, CHUNK), :] = dqw[CHUNK:]

            def add(idx, rows, pi=pi):
                dk_ref[idx, :] = dks_ref[rows, :] if pi == 0 else dk_ref[idx, :] + dks_ref[rows, :]
                dv_ref[idx, :] = dvs_ref[rows, :] if pi == 0 else dv_ref[idx, :] + dvs_ref[rows, :]
            _scatter_rows(None, None, d, t, add)

            for r in range(d):
                before = pl.ds(ATT_HALO - BAND * d + r, BAND, stride=d)
                after = pl.ds(r, BAND, stride=d)
                dqp_ref[before, :] += dqw_ref[r * nw:r * nw + BAND, :]
                dqn_ref[after, :] += dqw_ref[r * nw + BAND + n:(r + 1) * nw, :]
                for c in range(0, n, ROWS_PER_COPY):
                    m = min(ROWS_PER_COPY, n - c)
                    idx = pl.ds(r + c * d, m, stride=d)
                    val = dqw_ref[r * nw + BAND + c:r * nw + BAND + c + m, :]
                    dq_ref[idx, :] = val if pi == 0 else dq_ref[idx, :] + val

    ident = lambda hp: hp
    ospec = pl.BlockSpec((t, LANES), lambda hp, i: (i, hp))
    hspec = pl.BlockSpec((ATT_HALO, LANES), lambda hp, i: (i, hp))
    win_rows = t + 2 * ATT_HALO
    halo = jax.ShapeDtypeStruct((tiles * ATT_HALO, w), F32)
    return _pallas(
        body, name=name, grid=(w // LANES, tiles), comm=comm,
        args=[qk, qkv, qk, qk, qk, do, do, do, stats, stats, stats],
        in_specs=_att_specs(s, t, _col(1), halo=False) + _att_specs(s, t, _col(2), halo=False)
        + _att_specs(s, t, _col(0)) + _att_specs(s, t, ident) * 2,
        out_specs=[ospec, ospec, ospec, hspec, hspec],
        out_shape=[jax.ShapeDtypeStruct((s, w), F32)] * 3 + [halo, halo],
        scratch=[pltpu.VMEM((t, LANES), BF16), pltpu.VMEM((t, LANES), BF16),
                 pltpu.VMEM((win_rows, LANES), BF16), pltpu.VMEM((win_rows, LANES), BF16),
                 pltpu.VMEM((win_rows, LANES), F32),
                 pltpu.VMEM((t, LANES), F32), pltpu.VMEM((t, LANES), F32), pltpu.VMEM((win_rows, LANES), F32)],
        sem=("parallel", "parallel"))


def adamw_update(recvs, w, m, v, name, comm=None):
    nl = len(recvs)
    r, c = recvs[0].shape[1:]
    tr = 256 if (r > 256 and r % 256 == 0) else r
    nt = r // tr
    c1 = 1.0 - ADAM_B1 ** ADAM_STEP
    c2 = 1.0 - ADAM_B2 ** ADAM_STEP

    def body(*refs):
        g_refs = refs[:nl]
        w_ref, m_ref, v_ref, go_ref, d_ref, mo_ref, vo_ref = refs[nl:]

        def update(g_ref):
            g = g_ref[0].astype(F32)
            for j in range(1, N_DEV):
                g = g + g_ref[j].astype(F32)
            mn = ADAM_B1 * m_ref[...] + (1.0 - ADAM_B1) * g
            vn = ADAM_B2 * v_ref[...] + (1.0 - ADAM_B2) * (g * g)
            go_ref[...] = g
            mo_ref[...] = mn
            vo_ref[...] = vn
            d_ref[...] = -ADAM_LR * ((mn / c1) / (jnp.sqrt(vn / c2) + ADAM_EPS) + ADAM_WD * w_ref[...])

        for layer in range(nl):
            pl.when(pl.program_id(0) == layer)(functools.partial(update, g_refs[layer]))

    def gspec(layer):
        return pl.BlockSpec((N_DEV, tr, c), lambda l, i: (0, jnp.where(l == layer, i, 0), 0))

    spec = pl.BlockSpec((tr, c), lambda l, i: (l * nt + i, 0))
    return _pallas(
        body, name=name, grid=(nl, nt), args=[*recvs, w, m, v], comm=comm,
        in_specs=[gspec(layer) for layer in range(nl)] + [spec, spec, spec],
        out_specs=[spec] * 4, out_shape=[jax.ShapeDtypeStruct((nl * r, c), F32)] * 4,
        sem=("arbitrary", "arbitrary"))


BIG = ("mlp_w1", "mlp_w2", "ab_w_in", "ab_w_out", "c_w_qkv", "c_w_out")
SMALL = ("mix_norm_g", "mlp_norm_g", "a_spatial_w", "a_spatial_b", "a_vnorm_g", "a_vnorm_b", "b_conv_b", "b_norm_g",
         "b_norm_b", "c_q_norm_g", "c_k_norm_g")
WEIGHTS = ("mix_norm_g", "mlp_norm_g", "mlp_w1", "mlp_w2", "ab_w_in", "a_spatial_w", "a_spatial_b", "a_vnorm_g",
           "a_vnorm_b", "b_conv_w", "b_conv_b", "b_norm_g", "b_norm_b", "ab_w_out", "c_w_qkv", "c_q_norm_g",
           "c_k_norm_g", "c_w_out")


def _mixer_params(p, conv_full, i):
    aw = p["a_vnorm_g"].shape[1]
    row = lambda t: t[i][None, :]
    return dict(
        w=p["a_spatial_w"][i].astype(BF16), wt=jnp.swapaxes(p["a_spatial_w"][i], 1, 2).astype(BF16),
        bias=jnp.repeat(p["a_spatial_b"][i].T, aw // p["a_spatial_b"].shape[1], axis=1),
        vg=row(p["a_vnorm_g"]), vb=row(p["a_vnorm_b"]), cw=jnp.pad(conv_full[i], ((0, 1), (0, 0))),
        cb=row(p["b_conv_b"]), cg=row(p["b_norm_g"]), cbn=row(p["b_norm_b"]))


def _head_gains(p, i):
    rep = LANES // HEAD_DIM
    return jnp.stack([jnp.tile(p["c_q_norm_g"][i], rep), jnp.tile(p["c_k_norm_g"][i], rep),
                      jnp.ones((LANES,), F32)])[:, None, :]


PACKED = tuple(n for n in SMALL if n != "a_spatial_w")


def _pack(d):
    flat = jnp.concatenate([d[n].reshape(-1) for n in PACKED])
    rows = -(-flat.shape[0] // (8 * LANES)) * 8
    return jnp.pad(flat, (0, rows * LANES - flat.shape[0])).reshape(rows, LANES)


def _unpack(packed, like):
    flat, out, pos = packed.reshape(-1), {}, 0
    for n in PACKED:
        size = math.prod(like[n].shape)
        out[n] = flat[pos:pos + size].reshape(like[n].shape)
        pos += size
    return out


class Traffic:
    def __init__(self, shards, full=()):
        self.shards, self.w, self.queue, self.parts = shards, dict(full), [], {}

    def run(self, fn, *args, gather=(), send=False, **kw):
        operands, flags, dest = [], [], []
        if self.shards is None:
            if send:
                self.parts.update(self.queue)
                self.queue = []
        else:
            for k in gather:
                if k not in self.w:
                    operands.append(self.shards[k])
                    flags.append(("gather", SHARD_AXIS.get(k[0])))
                    dest.append((self.w, k))
            if send:
                for k, t in self.queue:
                    operands.append(t)
                    flags.append(("scatter", SHARD_AXIS.get(k[0])))
                    dest.append((self.parts, k))
                self.queue = []
        outs, couts = fn(*args, comm=PeerCopies(operands, flags) if operands else None, **kw)
        for (table, k), t in zip(dest, couts):
            table[k] = t
        return outs

    def flush(self, name, extra=()):
        self.queue += list(extra)
        self.run(lambda comm: ([], run_copies(comm, name) if comm is not None else []), send=True)


SHARD_AXIS = {"mlp_w1": 1, "mlp_w2": 0, "ab_w_in": 1, "ab_w_out": 0, "c_w_qkv": 1, "c_w_out": 0}


def forward_backward(x, target, p, tr, conv_full):
    s, d = x.shape
    depth = p["mix_norm_g"].shape[0]
    tabs = rope_tables(s)
    saved = []
    for l in range(depth):
        i = l // 2
        mix_g, mlp_g = p["mix_norm_g"][l][None, :], p["mlp_norm_g"][l][None, :]
        st = dict(x_in=x)
        nxt = () if l + 1 == depth else ((("c_w_qkv", i), ("c_w_out", i)) if l % 2 == 0 else
                                        (("ab_w_in", i + 1), ("ab_w_out", i + 1)))
        if l % 2 == 0:
            sp = _mixer_params(p, conv_full, i)
            z, h = tr.run(norm_matmul, x, mix_g, tr.w["ab_w_in", i], f"ab_in_{l}", gather=[("mlp_w1", l)])
            ycat, gc = tr.run(ab_mid_forward, z, sp, f"ab_mid_{l}", gather=[("mlp_w2", l)])
            x = matmul_residual(x, ycat, tr.w["ab_w_out", i], f"ab_out_{l}")
            st.update(z=z, h=h, y=ycat, gc=gc, sp=sp)
        else:
            gains = _head_gains(p, i)
            qkv, h = norm_matmul(x, mix_g, tr.w["c_w_qkv", i], f"c_qkv_{l}")[0]
            qk = qk_prep_forward(qkv, tabs, gains, f"c_prep_{l}")[0]
            ahead = [("mlp_w1", l), ("mlp_w2", l)] + ([("mlp_w1", l + 1)] if l + 1 < depth else [])
            o, lse = tr.run(attention_forward, qk, qkv, f"c_attn_{l}", gather=ahead)
            x = matmul_residual(x, o, tr.w["c_w_out", i], f"c_out_{l}")
            st.update(qkv=qkv, h=h, qk=qk, y=o, lse=lse, gains=gains)
            nxt = nxt + ((("mlp_w2", l + 1),) if l + 1 < depth else ())
        st["x_mid"] = x
        x, a, h2 = tr.run(mlp_forward, x, mlp_g, tr.w["mlp_w1", l], tr.w["mlp_w2", l], f"mlp_{l}", gather=nxt)
        st.update(a=a, h2=h2)
        saved.append(st)

    dy, loss_part = loss_and_grad(x, target, "loss")

    def tn(*a, comm, **kw):
        out, couts = matmul_tn(*a, comm=comm, **kw)
        return [out], couts

    small = {n: [None] * p[n].shape[0] for n in SMALL}
    conv_grads = [None] * p["b_conv_b"].shape[0]
    for l in reversed(range(depth)):
        i, st = l // 2, saved[l]
        mix_g, mlp_g = p["mix_norm_g"][l][None, :], p["mlp_norm_g"][l][None, :]
        w1, w2 = tr.w["mlp_w1", l], tr.w["mlp_w2", l]
        dxm, da, dg, dyb = tr.run(mlp_backward, dy, st["a"], st["x_mid"], mlp_g, w1, w2, f"mlp_bwd_{l}", send=True)
        small["mlp_norm_g"][l] = dg[0]
        tr.queue.append((("mlp_w1", l), matmul_tn(st["h2"], da, f"mlp_dw1_{l}", n_split=2)[0]))
        tr.queue.append((("mlp_w2", l), matmul_tn(st["a"], dyb, f"mlp_dw2_{l}", m_split=4, relu2=True)[0]))
        if l % 2 == 0:
            sp = st["sp"]
            wout = tr.w["ab_w_out", i]
            dycat = matmul_nt(dxm, wout, f"ab_out_bwd_{l}")
            later = [tr.queue.pop(), (("ab_w_out", i), matmul_tn(st["y"], dxm, f"ab_dwout_{l}")[0])]
            dz, dgc, dw, dsb, dvg, dvb, dcg, dcbn, dcb = tr.run(
                ab_mid_backward, dycat, st["z"], st["gc"], sp, f"ab_mid_bwd_{l}", send=True)
            tr.queue += later
            dz, dcw = tr.run(conv_backward, dgc, st["z"], dz, sp, f"ab_conv_bwd_{l}", send=True)
            small["a_spatial_w"][i], small["a_spatial_b"][i] = dw, dsb
            small["a_vnorm_g"][i], small["a_vnorm_b"][i] = dvg[0], dvb[0]
            small["b_norm_g"][i], small["b_norm_b"][i], small["b_conv_b"][i] = dcg[0], dcbn[0], dcb[0]
            conv_grads[i] = dcw[:CONV_W]
            dy, dg = matmul_nt_norm_backward(dz, tr.w["ab_w_in", i], st["x_in"], mix_g, dxm, f"ab_in_bwd_{l}")
            small["mix_norm_g"][l] = dg[0]
            last = []
            if l == 0 and tr.shards is not None:
                tr.shards["small_grads", 0] = _pack({n: jnp.stack(small[n]) for n in PACKED})
                tr.shards["spatial_grads", 0] = jnp.stack(small["a_spatial_w"]).astype(BF16)
                last = [("small_grads", 0), ("spatial_grads", 0)]
            dwin = tr.run(tn, st["h"], dz, f"ab_dwin_{l}", send=True, gather=last)[0]
            tr.queue.append((("ab_w_in", i), dwin))
        else:
            wout = tr.w["c_w_out", i]
            do = matmul_nt(dxm, wout, f"c_out_bwd_{l}")
            tr.queue.append((("c_w_out", i), matmul_tn(st["y"], dxm, f"c_dwout_{l}")[0]))
            stats = attention_delta(do, st["y"], st["lse"], f"c_delta_{l}")
            dk, dv, dq, dqp, dqn = tr.run(attention_backward, st["qk"], st["qkv"], do, stats, f"c_attn_bwd_{l}", send=True)
            dqkv, dgn = qk_prep_backward(dq, dqp, dqn, dk, dv, st["qkv"], tabs, st["gains"], f"c_prep_bwd_{l}")
            small["c_q_norm_g"][i] = dgn[0, 0, :HEAD_DIM] + dgn[0, 0, HEAD_DIM:]
            small["c_k_norm_g"][i] = dgn[1, 0, :HEAD_DIM] + dgn[1, 0, HEAD_DIM:]
            dy, dg = matmul_nt_norm_backward(dqkv, tr.w["c_w_qkv", i], st["x_in"], mix_g, dxm, f"c_qkv_bwd_{l}")
            small["mix_norm_g"][l] = dg[0]
            tr.queue.append((("c_w_qkv", i), matmul_tn(st["h"], dqkv, f"c_dwqkv_{l}", n_split=2)[0]))
    small = {n: jnp.stack(v) for n, v in small.items()}
    return loss_part, dy, small, jnp.stack(conv_grads)


def kernel(x, mix_norm_g, mlp_norm_g, mlp_w1, mlp_w2, ab_w_in, a_spatial_w, a_spatial_b, a_vnorm_g, a_vnorm_b, b_conv_w, b_conv_b, b_norm_g, b_norm_b, ab_w_out, c_w_qkv, c_q_norm_g, c_k_norm_g, c_w_out, loss_target, m_mix_norm_g, m_mlp_norm_g, m_mlp_w1, m_mlp_w2, m_ab_w_in, m_a_spatial_w, m_a_spatial_b, m_a_vnorm_g, m_a_vnorm_b, m_b_conv_w, m_b_conv_b, m_b_norm_g, m_b_norm_b, m_ab_w_out, m_c_w_qkv, m_c_q_norm_g, m_c_k_norm_g, m_c_w_out, v_mix_norm_g, v_mlp_norm_g, v_mlp_w1, v_mlp_w2, v_ab_w_in, v_a_spatial_w, v_a_spatial_b, v_a_vnorm_g, v_a_vnorm_b, v_b_conv_w, v_b_conv_b, v_b_norm_g, v_b_norm_b, v_ab_w_out, v_c_w_qkv, v_c_q_norm_g, v_c_k_norm_g, v_c_w_out):
    args = dict(locals())
    w = {n: args[n] for n in WEIGHTS}
    m = {n: args["m_" + n] for n in WEIGHTS}
    v = {n: args["v_" + n] for n in WEIGHTS}

    shards = {(n, l): w[n][l].astype(BF16) for n in BIG for l in range(w[n].shape[0])}
    shards["b_conv_w", 0] = w["b_conv_w"]
    tr = Traffic(shards)
    first = [("ab_w_in", 0), ("ab_w_out", 0), ("b_conv_w", 0)]
    tr.run(lambda comm: ([], run_copies(comm, "gather_first")), gather=first)
    conv = tr.w["b_conv_w", 0]
    conv_full = jnp.transpose(conv, (1, 2, 0, 3)).reshape(conv.shape[1], conv.shape[2], -1)

    loss_part, dx, small, conv_grad = forward_backward(x[0], loss_target[0], w, tr, conv_full)
    loss = lax.psum(jnp.sum(loss_part), ("x", "y", "c"))
    nl, kw, cw = conv_grad.shape
    conv_parts = jnp.transpose(conv_grad.reshape(nl, kw, N_DEV, cw // N_DEV), (2, 0, 1, 3))
    tr.queue.append((("b_conv_w", 0), conv_parts))

    grads, deltas, new_m, new_v = {}, {}, {}, {}

    def update(n, recvs):
        shape = w[n].shape
        flat = lambda t: t.reshape(-1, shape[-1])
        recvs = [t.reshape(N_DEV, -1, shape[-1]) for t in recvs]
        outs = tr.run(adamw_update, recvs, flat(w[n]), flat(m[n]), flat(v[n]), "adamw_" + n, send=True)
        grads[n], deltas[n], new_m[n], new_v[n] = (t.reshape(shape) for t in outs)

    for n in BIG:
        update(n, [tr.parts[n, l] for l in range(w[n].shape[0])])
    update("b_conv_w", [tr.parts["b_conv_w", 0]])
    update("a_spatial_w", [tr.w["spatial_grads", 0]])
    outs = adamw_update([tr.w["small_grads", 0]], _pack(w), _pack(m), _pack(v), "adamw_small")[0]
    for dst, t in zip((grads, deltas, new_m, new_v), outs):
        dst.update(_unpack(t, w))

    return (loss, dx[None], *[grads[n] for n in WEIGHTS], *[deltas[n] for n in WEIGHTS],
            *[new_m[n] for n in WEIGHTS], *[new_v[n] for n in WEIGHTS])
```

```python
import functools
import math

import jax
import jax.numpy as jnp
from jax import lax
from jax.experimental import pallas as pl
from jax.experimental.pallas import tpu as pltpu

F32, BF16 = jnp.float32, jnp.bfloat16
N_DEV = 8
EPS = 1e-6
NEG = -1e30
LANES = 128
HEAD_DIM = 64
CHUNK = 128
CONV_W = 31
CONV_HALO = 16
BAND = 64
DILATIONS = (1, 4, 16)
ROT_DIM = 16
ROPE_THETA = 500000.0
VMEM_LIMIT = 56 * 1024 * 1024
MLP_CHUNK = 1024
MLP_BWD_CHUNK = 512
MLP_BWD_ROWS = 1024
ADAM_LR, ADAM_B1, ADAM_B2, ADAM_EPS, ADAM_WD, ADAM_STEP = 0.001, 0.9, 0.999, 1e-08, 0.01, 10
MESH = pl.DeviceIdType.MESH


def _params(*sem):
    return pltpu.CompilerParams(dimension_semantics=sem, vmem_limit_bytes=VMEM_LIMIT)


def _dot(a, b):
    return jnp.dot(a, b, preferred_element_type=F32)


def _dot_nt(a, b):
    return lax.dot_general(a, b, (((1,), (1,)), ((), ())), preferred_element_type=F32)


def _dot_tn(a, b):
    return lax.dot_general(a, b, (((0,), (0,)), ((), ())), preferred_element_type=F32)


def _rms_r(x):
    return lax.rsqrt(jnp.mean(x * x, axis=-1, keepdims=True) + EPS)


def _sigmoid(x):
    return 1.0 / (1.0 + jnp.exp(-x))


_GK = math.sqrt(2.0 / math.pi)


def _gelu(x):
    return 0.5 * x * (1.0 + jnp.tanh(_GK * (x + 0.044715 * x * x * x)))


def _gelu_grad(x):
    t = jnp.tanh(_GK * (x + 0.044715 * x * x * x))
    return 0.5 * (1.0 + t) + 0.5 * x * (1.0 - t * t) * (_GK * (1.0 + 3.0 * 0.044715 * x * x))


def _seg_sum(x, bd):
    hi = x.astype(BF16)
    lo = (x - hi.astype(F32)).astype(BF16)
    return _dot(hi, bd) + _dot(lo, bd)


def _block_diag(n):
    i = lax.broadcasted_iota(jnp.int32, (n, n), 0) // HEAD_DIM
    j = lax.broadcasted_iota(jnp.int32, (n, n), 1) // HEAD_DIM
    return jnp.where(i == j, 1.0, 0.0).astype(BF16)


def _tile(s, cap):
    t = min(s, cap)
    assert s % t == 0
    return t


def _my_index():
    return 4 * lax.axis_index("x") + 2 * lax.axis_index("y") + lax.axis_index("c")


def _device(i):
    return (i // 4, (i // 2) % 2, i % 2)


_HBM = pl.BlockSpec(memory_space=pl.ANY)


class PeerCopies:
    def __init__(self, operands, modes):
        self.inputs, self.modes = list(operands), list(modes)
        self.out_shape = []
        for t, (kind, axis) in zip(self.inputs, self.modes):
            shape = list(t.shape)
            if kind == "gather":
                shape = [N_DEV] + shape if axis is None else shape[:axis] + [N_DEV * shape[axis]] + shape[axis + 1:]
            elif axis is not None:
                shape = [N_DEV] + shape[:axis] + [shape[axis] // N_DEV] + shape[axis + 1:]
            self.out_shape.append(jax.ShapeDtypeStruct(tuple(shape), t.dtype))
        n = len(self.inputs)
        self.scratch = [pltpu.SemaphoreType.DMA((n, N_DEV - 1)), pltpu.SemaphoreType.DMA((n, N_DEV - 1)),
                        pltpu.SemaphoreType.DMA((n,))]

    @staticmethod
    def _block(ref, axis, size, j):
        if axis is None:
            return ref.at[j]
        return ref.at[tuple([slice(None)] * axis + [pl.ds(j * size, size)])]

    def _copies(self, in_refs, out_refs, sems, arrivals):
        send_sems, recv_sems, local_sems = sems
        me = _my_index()
        local, sends, recvs = [], [], []
        for t, (src, dst) in enumerate(zip(in_refs, out_refs)):
            kind, axis = self.modes[t]
            if kind == "gather":
                continue
            size = None if axis is None else src.shape[axis] // N_DEV
            source = lambda j, src=src, axis=axis, size=size: self._block(src, axis, size, j)
            place = lambda j, dst=dst: dst.at[j]
            local.append(pltpu.make_async_copy(source(me), place(me), local_sems.at[t]))
            for k in range(N_DEV - 1):
                to, frm = (me + k + 1) % N_DEV, (me + N_DEV - k - 1) % N_DEV
                sends.append(pltpu.make_async_remote_copy(
                    src_ref=source(to), dst_ref=place(me), send_sem=send_sems.at[t, k], recv_sem=recv_sems.at[t, k],
                    device_id=_device(to), device_id_type=MESH))
                if arrivals:
                    recvs.append(pltpu.make_async_remote_copy(
                        src_ref=source(me), dst_ref=place(frm), send_sem=send_sems.at[t, k], recv_sem=recv_sems.at[t, k],
                        device_id=_device(frm), device_id_type=MESH))
        return local, sends, recvs

    _FLIPS = (1, 4, 2, 6)

    def _gathers(self, in_refs, out_refs, sems):
        send_sems, recv_sems, local_sems = sems
        me = _my_index()
        jobs = []
        for t, (src, dst) in enumerate(zip(in_refs, out_refs)):
            kind, axis = self.modes[t]
            if kind != "gather":
                continue
            size = None if axis is None else src.shape[axis]
            place = lambda j, dst=dst, axis=axis, size=size: self._block(dst, axis, size, j)

            def copy(k, source, block, to, t=t, place=place):
                return pltpu.make_async_remote_copy(
                    src_ref=source, dst_ref=place(block), send_sem=send_sems.at[t, k], recv_sem=recv_sems.at[t, k],
                    device_id=_device(to), device_id_type=MESH)

            local = pltpu.make_async_copy(src, place(me), local_sems.at[t])
            first = [copy(k, src, me, jnp.bitwise_xor(me, f)) for k, f in enumerate(self._FLIPS)]
            jobs.append((local, first, place, copy, src))
        return me, jobs

    def start(self, in_refs, out_refs, sems):
        local, sends, _ = self._copies(in_refs, out_refs, sems, False)
        for cp in local + sends:
            cp.start()
        _, jobs = self._gathers(in_refs, out_refs, sems)
        for local, first, _, _, _ in jobs:
            local.start()
            for cp in first:
                cp.start()

    def relay(self, in_refs, out_refs, sems, issue=True):
        me, jobs = self._gathers(in_refs, out_refs, sems)
        sibling = jnp.bitwise_xor(me, 1)
        passed = []
        for _, _, place, copy, src in jobs:
            for j, f in enumerate(self._FLIPS[1:]):
                origin = jnp.bitwise_xor(me, f)
                if issue:
                    copy(1 + j, src, origin, me).wait_recv()
                cp = copy(4 + j, place(origin), origin, sibling)
                if issue:
                    cp.start()
                passed.append(cp)
        return passed

    def finish(self, in_refs, out_refs, sems, relayed=False):
        me, jobs = self._gathers(in_refs, out_refs, sems)
        sibling = jnp.bitwise_xor(me, 1)
        passed = self.relay(in_refs, out_refs, sems, issue=not relayed)
        local, sends, recvs = self._copies(in_refs, out_refs, sems, True)
        for cp in recvs:
            cp.wait_recv()
        for _, _, place, copy, src in jobs:
            copy(0, src, sibling, me).wait_recv()
            for j, f in enumerate(self._FLIPS[1:]):
                copy(4 + j, src, jnp.bitwise_xor(sibling, f), me).wait_recv()
        for cp in sends + passed:
            cp.wait_send()
        for local_, first, _, _, _ in jobs:
            for cp in first:
                cp.wait_send()
            local_.wait()
        for cp in local:
            cp.wait()


def _pallas(body, *, name, args, in_specs, out_specs, out_shape, grid=(), scratch=(), sem=(), comm=None, aliases=None):
    n_in, n_out, n_scr = len(args), len(out_shape), len(scratch)
    if comm is None:
        outs = pl.pallas_call(
            body, name=name, grid=grid, in_specs=in_specs, out_specs=out_specs, out_shape=out_shape,
            scratch_shapes=list(scratch), input_output_aliases=aliases or {}, compiler_params=_params(*sem))(*args)
        return list(outs), []
    ci, co = len(comm.inputs), len(comm.out_shape)

    def hosted(*refs):
        ins, cins = refs[:n_in], refs[n_in:n_in + ci]
        outs, couts = refs[n_in + ci:n_in + ci + n_out], refs[n_in + ci + n_out:n_in + ci + n_out + co]
        rest = refs[n_in + ci + n_out + co:]
        scr, sems = rest[:n_scr], rest[n_scr:]
        if not grid:
            comm.start(cins, couts, sems)
            comm.finish(cins, couts, sems)
            return
        first = last = None
        step, total = 0, 1
        for axis, size in enumerate(grid):
            f, l = pl.program_id(axis) == 0, pl.program_id(axis) == size - 1
            first, last = (f, l) if first is None else (first & f, last & l)
            step, total = step * size + pl.program_id(axis), total * size
        early = total >= 4
        pl.when(first)(lambda: comm.start(cins, couts, sems))
        body(*ins, *outs, *scr)
        if early:
            @pl.when(step == (3 * total) // 4)
            def _():
                comm.relay(cins, couts, sems)
        pl.when(last)(lambda: comm.finish(cins, couts, sems, relayed=early))

    outs = pl.pallas_call(
        hosted, name=name, grid=grid, in_specs=list(in_specs) + [_HBM] * ci, out_specs=list(out_specs) + [_HBM] * co,
        out_shape=list(out_shape) + comm.out_shape, scratch_shapes=list(scratch) + comm.scratch,
        input_output_aliases=aliases or {}, compiler_params=_params(*["arbitrary"] * len(grid)))(*args, *comm.inputs)
    return list(outs[:n_out]), list(outs[n_out:])


def run_copies(comm, name):
    return _pallas(None, name=name, args=[], in_specs=[], out_specs=[], out_shape=[], comm=comm)[1]


def norm_matmul(x, g, wg, name, comm=None):
    s, d = x.shape
    n = wg.shape[-1]
    ns = 1024 if n % 1024 == 0 else n // 4
    tm = _tile(s, 1024)

    def body(x_ref, g_ref, w_ref, z_ref, h_ref):
        @pl.when(pl.program_id(1) == 0)
        def _():
            xv = x_ref[...]
            h_ref[...] = (xv * _rms_r(xv) * g_ref[...]).astype(BF16)
        z_ref[...] = _dot(h_ref[...], w_ref[...])

    return _pallas(
        body, name=name, grid=(s // tm, n // ns), args=[x, g, wg], comm=comm,
        in_specs=[pl.BlockSpec((tm, d), lambda i, j: (i, 0)),
                  pl.BlockSpec((1, d), lambda i, j: (0, 0)),
                  pl.BlockSpec((d, ns), lambda i, j: (0, j))],
        out_specs=[pl.BlockSpec((tm, ns), lambda i, j: (i, j)),
                   pl.BlockSpec((tm, d), lambda i, j: (i, 0))],
        out_shape=[jax.ShapeDtypeStruct((s, n), F32), jax.ShapeDtypeStruct((s, d), BF16)],
        sem=("parallel", "arbitrary"))


def mlp_forward(x, g, w1g, w2g, name, comm=None):
    s, d = x.shape
    f = w1g.shape[-1]
    fs = MLP_CHUNK
    tm = _tile(s, 1024)

    def body(x_ref, g_ref, w1_ref, w2_ref, xo_ref, a_ref, h_ref):
        @pl.when(pl.program_id(1) == 0)
        def _():
            xv = x_ref[...]
            h_ref[...] = (xv * _rms_r(xv) * g_ref[...]).astype(BF16)
            xo_ref[...] = xv
        a = _dot(h_ref[...], w1_ref[...])
        a_ref[...] = a.astype(BF16)
        r = jnp.maximum(a, 0.0)
        xo_ref[...] += _dot((r * r).astype(BF16), w2_ref[...])

    return _pallas(
        body, name=name, grid=(s // tm, f // fs), args=[x, g, w1g, w2g], comm=comm,
        in_specs=[pl.BlockSpec((tm, d), lambda i, j: (i, 0)),
                  pl.BlockSpec((1, d), lambda i, j: (0, 0)),
                  pl.BlockSpec((d, fs), lambda i, j: (0, j)),
                  pl.BlockSpec((fs, d), lambda i, j: (j, 0))],
        out_specs=[pl.BlockSpec((tm, d), lambda i, j: (i, 0)),
                   pl.BlockSpec((tm, fs), lambda i, j: (i, j)),
                   pl.BlockSpec((tm, d), lambda i, j: (i, 0))],
        out_shape=[jax.ShapeDtypeStruct((s, d), F32), jax.ShapeDtypeStruct((s, f), BF16),
                   jax.ShapeDtypeStruct((s, d), BF16)],
        sem=("parallel", "arbitrary"))


def _norm_backward(dh, xv, g, dres):
    r = _rms_r(xv)
    xh = xv * r
    t = dh * g
    dx = dres + r * (t - xh * jnp.mean(t * xh, axis=-1, keepdims=True))
    return dx, jnp.sum(dh * xh, axis=0, keepdims=True)


def mlp_backward(dy, a, x, g, w1g, w2g, name, comm=None):
    s, d = x.shape
    f = w1g.shape[-1]
    fs = MLP_BWD_CHUNK
    tm = _tile(s, MLP_BWD_ROWS)

    def body(dy_ref, a_ref, x_ref, g_ref, w1_ref, w2_ref, dx_ref, da_ref, dg_ref, dyb_ref, dh_ref):
        i, j = pl.program_id(0), pl.program_id(1)

        @pl.when(j == 0)
        def _():
            dyb_ref[...] = dy_ref[...].astype(BF16)
            dh_ref[...] = jnp.zeros_like(dh_ref)

        dr = _dot_nt(dyb_ref[...], w2_ref[...])
        da = (dr * (2.0 * jnp.maximum(a_ref[...].astype(F32), 0.0))).astype(BF16)
        da_ref[...] = da
        dh_ref[...] += _dot_nt(da, w1_ref[...])

        @pl.when(j == f // fs - 1)
        def _():
            dx, dgp = _norm_backward(dh_ref[...], x_ref[...], g_ref[...], dy_ref[...])
            dx_ref[...] = dx

            @pl.when(i == 0)
            def _():
                dg_ref[...] = dgp

            @pl.when(i > 0)
            def _():
                dg_ref[...] += dgp

    return _pallas(
        body, name=name, grid=(s // tm, f // fs), args=[dy, a, x, g, w1g, w2g], comm=comm,
        in_specs=[pl.BlockSpec((tm, d), lambda i, j: (i, 0)),
                  pl.BlockSpec((tm, fs), lambda i, j: (i, j)),
                  pl.BlockSpec((tm, d), lambda i, j: (i, 0)),
                  pl.BlockSpec((1, d), lambda i, j: (0, 0)),
                  pl.BlockSpec((d, fs), lambda i, j: (0, j)),
                  pl.BlockSpec((fs, d), lambda i, j: (j, 0))],
        out_specs=[pl.BlockSpec((tm, d), lambda i, j: (i, 0)),
                   pl.BlockSpec((tm, fs), lambda i, j: (i, j)),
                   pl.BlockSpec((1, d), lambda i, j: (0, 0)),
                   pl.BlockSpec((tm, d), lambda i, j: (i, 0))],
        out_shape=[jax.ShapeDtypeStruct((s, d), F32), jax.ShapeDtypeStruct((s, f), BF16),
                   jax.ShapeDtypeStruct((1, d), F32), jax.ShapeDtypeStruct((s, d), BF16)],
        scratch=[pltpu.VMEM((tm, d), F32)],
        sem=("arbitrary", "arbitrary"))


def matmul_tn(a, b, name, m_split=1, n_split=1, relu2=False, comm=None):
    s, m = a.shape
    n = b.shape[1]
    ts = _tile(s, 2048)
    bm, bn = m // m_split, n // n_split
    a_map = lambda j, k: (k, j // n_split)
    b_map = lambda j, k: (k, j % n_split)

    def body(a_ref, b_ref, o_ref, acc_ref):
        k = pl.program_id(1)
        av = a_ref[...]
        if relu2:
            af = jnp.maximum(av.astype(F32), 0.0)
            av = af * af
        p = _dot_tn(av.astype(BF16), b_ref[...].astype(BF16))

        @pl.when(k == 0)
        def _():
            acc_ref[...] = p

        @pl.when(k > 0)
        def _():
            acc_ref[...] += p

        @pl.when(k == s // ts - 1)
        def _():
            o_ref[...] = acc_ref[...].astype(BF16)

    outs, couts = _pallas(
        body, name=name, grid=(m_split * n_split, s // ts), args=[a, b], comm=comm,
        in_specs=[pl.BlockSpec((ts, bm), a_map), pl.BlockSpec((ts, bn), b_map)],
        out_specs=[pl.BlockSpec((bm, bn), lambda j, k: (j // n_split, j % n_split))],
        out_shape=[jax.ShapeDtypeStruct((m, n), BF16)],
        scratch=[pltpu.VMEM((bm, bn), F32)],
        sem=("parallel", "arbitrary"))
    return outs[0], couts


def matmul_residual(x, y, w, name):
    s, n = x.shape
    k = y.shape[1]
    tm = _tile(s, 1024)

    def body(x_ref, y_ref, w_ref, o_ref):
        o_ref[...] = x_ref[...] + _dot(y_ref[...], w_ref[...])

    return pl.pallas_call(
        body, name=name, grid=(s // tm,),
        in_specs=[pl.BlockSpec((tm, n), lambda i: (i, 0)),
                  pl.BlockSpec((tm, k), lambda i: (i, 0)),
                  pl.BlockSpec((k, n), lambda i: (0, 0))],
        out_specs=pl.BlockSpec((tm, n), lambda i: (i, 0)),
        out_shape=jax.ShapeDtypeStruct((s, n), F32),
        compiler_params=_params("parallel"),
    )(x, y, w)


def matmul_nt(dy, wg, name):
    s, n = dy.shape
    k = wg.shape[0]
    tm = _tile(s, 1024)

    def body(dy_ref, w_ref, o_ref):
        o_ref[...] = _dot_nt(dy_ref[...].astype(BF16), w_ref[...])

    return pl.pallas_call(
        body, name=name, grid=(s // tm,),
        in_specs=[pl.BlockSpec((tm, n), lambda i: (i, 0)),
                  pl.BlockSpec((k, n), lambda i: (0, 0))],
        out_specs=pl.BlockSpec((tm, k), lambda i: (i, 0)),
        out_shape=jax.ShapeDtypeStruct((s, k), F32),
        compiler_params=_params("parallel"),
    )(dy, wg)


def matmul_nt_norm_backward(dz, wg, x, g, dres, name):
    s, d = x.shape
    n = wg.shape[-1]
    tm = _tile(s, 512)

    def body(dz_ref, w_ref, x_ref, g_ref, dres_ref, dx_ref, dg_ref):
        dx, dgp = _norm_backward(_dot_nt(dz_ref[...], w_ref[...]), x_ref[...], g_ref[...], dres_ref[...])
        dx_ref[...] = dx
        _accumulate(dg_ref, dgp, pl.program_id(0) == 0)

    return pl.pallas_call(
        body, name=name, grid=(s // tm,),
        in_specs=[pl.BlockSpec((tm, n), lambda i: (i, 0)),
                  pl.BlockSpec((d, n), lambda i: (0, 0)),
                  pl.BlockSpec((tm, d), lambda i: (i, 0)),
                  pl.BlockSpec((1, d), lambda i: (0, 0)),
                  pl.BlockSpec((tm, d), lambda i: (i, 0))],
        out_specs=[pl.BlockSpec((tm, d), lambda i: (i, 0)),
                   pl.BlockSpec((1, d), lambda i: (0, 0))],
        out_shape=[jax.ShapeDtypeStruct((s, d), F32), jax.ShapeDtypeStruct((1, d), F32)],
        compiler_params=_params("arbitrary"),
    )(dz, wg, x, g, dres)


def loss_and_grad(y, target, name):
    s, d = y.shape
    tm = _tile(s, 1024)

    def body(y_ref, t_ref, dy_ref, l_ref):
        e = y_ref[...] - t_ref[...]
        dy_ref[...] = e / d
        part = jnp.sum(e * e, axis=0, keepdims=True) * (0.5 / d)

        @pl.when(pl.program_id(0) == 0)
        def _():
            l_ref[...] = part

        @pl.when(pl.program_id(0) > 0)
        def _():
            l_ref[...] += part

    return pl.pallas_call(
        body, name=name, grid=(s // tm,),
        in_specs=[pl.BlockSpec((tm, d), lambda i: (i, 0)), pl.BlockSpec((tm, d), lambda i: (i, 0))],
        out_specs=[pl.BlockSpec((tm, d), lambda i: (i, 0)), pl.BlockSpec((1, d), lambda i: (0, 0))],
        out_shape=[jax.ShapeDtypeStruct((s, d), F32), jax.ShapeDtypeStruct((1, d), F32)],
        compiler_params=_params("arbitrary"),
    )(y, target)


def _layernorm(x, g, b):
    mu = jnp.mean(x, axis=-1, keepdims=True)
    xc = x - mu
    rstd = lax.rsqrt(jnp.mean(xc * xc, axis=-1, keepdims=True) + EPS)
    xn = xc * rstd
    return xn * g + b, xn, rstd


def _layernorm_backward(dy, xn, rstd, g):
    dxn = dy * g
    return rstd * (dxn - jnp.mean(dxn, axis=-1, keepdims=True) - xn * jnp.mean(dxn * xn, axis=-1, keepdims=True))


def _group_halves(x_ref, jp, nch):
    blk = jnp.concatenate([x_ref[c * CHUNK:(c + 1) * CHUNK, jp * LANES:(jp + 1) * LANES] for c in range(nch)], axis=1)
    low = (lax.broadcasted_iota(jnp.int32, blk.shape, 1) % LANES) < HEAD_DIM
    return jnp.where(low, blk, 0.0).astype(BF16), jnp.where(low, 0.0, blk).astype(BF16)


def _spatial_apply(src_ref, w_ref, dst_ref, nch, bias_ref=None):
    for jp in range(4):
        lo, hi = _group_halves(src_ref, jp, nch)
        r = _dot(w_ref[2 * jp], lo) + _dot(w_ref[2 * jp + 1], hi)
        for c in range(nch):
            v = r[:, c * LANES:(c + 1) * LANES]
            if bias_ref is not None:
                v = v + bias_ref[:, jp * LANES:(jp + 1) * LANES]
            dst_ref[c * CHUNK:(c + 1) * CHUNK, jp * LANES:(jp + 1) * LANES] = v


def _glu(zb):
    w = zb.shape[1] // 2
    return zb[:, :w] * _sigmoid(zb[:, w:])


def _fill_padded(pad_ref, prev, cur, nxt, i, nt, tm):
    pad_ref[0:CONV_HALO, :] = jnp.where(i > 0, prev, 0.0)
    pad_ref[CONV_HALO:CONV_HALO + tm, :] = cur
    pad_ref[CONV_HALO + tm:2 * CONV_HALO + tm, :] = jnp.where(i < nt - 1, nxt, 0.0)


def _halo_specs(tm, s, width, col):
    hb, nhb = tm // CONV_HALO, s // CONV_HALO
    return [pl.BlockSpec((tm, width), lambda i: (i, col)),
            pl.BlockSpec((CONV_HALO, width), lambda i: (jnp.maximum(i * hb - 1, 0), col)),
            pl.BlockSpec((CONV_HALO, width), lambda i: (jnp.minimum((i + 1) * hb, nhb - 1), col))]


def _const_spec(shape):
    nd = len(shape)
    return pl.BlockSpec(shape, lambda i: (0,) * nd)


SUBLANES = 8


def _shift_scratch(tm, width):
    return pltpu.VMEM((SUBLANES - 1, tm + 2 * CONV_HALO - SUBLANES, width), F32)


def _fill_shifts(sh_ref, pad_ref):
    rows = sh_ref.shape[1]
    for sft in range(1, SUBLANES):
        sh_ref[sft - 1] = pad_ref[pl.ds(sft, rows), :]


def _tap(pad_ref, sh_ref, offset, cols):
    sft = offset % SUBLANES
    rows = pl.ds(offset - sft, CHUNK)
    return pad_ref[rows, cols] if sft == 0 else sh_ref[sft - 1, rows, cols]


def ab_mid_forward(z, sp, name, comm=None):
    s = z.shape[0]
    aw = z.shape[1] // 4
    tm = _tile(s, 512)
    nch, nt = tm // CHUNK, s // tm

    def body(zu_ref, zv_ref, zb_ref, zp_ref, zn_ref, w_ref, bias_ref, vg_ref, vb_ref, cw_ref, cb_ref, cg_ref, cbn_ref,
             y_ref, gc_ref, vl_ref, sv_ref, pad_ref, sh_ref):
        i = pl.program_id(0)
        vl_ref[...] = _layernorm(_gelu(zv_ref[...]), vg_ref[...], vb_ref[...])[0]
        _spatial_apply(vl_ref, w_ref, sv_ref, nch, bias_ref)
        y_ref[:, :aw] = (_gelu(zu_ref[...]) * sv_ref[...]).astype(BF16)

        _fill_padded(pad_ref, _glu(zp_ref[...]), _glu(zb_ref[...]), _glu(zn_ref[...]), i, nt, tm)
        _fill_shifts(sh_ref, pad_ref)
        for rb in range(tm // CHUNK):
            for lb in range(aw // LANES):
                cols = pl.ds(lb * LANES, LANES)
                acc = jnp.broadcast_to(cb_ref[:, cols], (CHUNK, LANES))
                for k in range(CONV_W):
                    acc = acc + cw_ref[k:k + 1, cols] * _tap(pad_ref, sh_ref, rb * CHUNK + CONV_HALO - CONV_W // 2 + k, cols)
                gc_ref[rb * CHUNK:(rb + 1) * CHUNK, cols] = acc
        yl = _layernorm(gc_ref[...], cg_ref[...], cbn_ref[...])[0]
        y_ref[:, aw:] = (yl * _sigmoid(yl)).astype(BF16)

    return _pallas(
        body, name=name, grid=(nt,), comm=comm,
        args=[z, z, z, z, z, sp["w"], sp["bias"], sp["vg"], sp["vb"], sp["cw"], sp["cb"], sp["cg"], sp["cbn"]],
        in_specs=[pl.BlockSpec((tm, aw), lambda i: (i, 0)), pl.BlockSpec((tm, aw), lambda i: (i, 1))]
        + _halo_specs(tm, s, 2 * aw, 1)
        + [_const_spec(sp["w"].shape), _const_spec(sp["bias"].shape)]
        + [_const_spec((1, aw))] * 2 + [_const_spec(sp["cw"].shape)] + [_const_spec((1, aw))] * 3,
        out_specs=[pl.BlockSpec((tm, 2 * aw), lambda i: (i, 0)), pl.BlockSpec((tm, aw), lambda i: (i, 0))],
        out_shape=[jax.ShapeDtypeStruct((s, 2 * aw), BF16), jax.ShapeDtypeStruct((s, aw), F32)],
        scratch=[pltpu.VMEM((tm, aw), F32), pltpu.VMEM((tm, aw), F32), pltpu.VMEM((tm + 2 * CONV_HALO, aw), F32),
                 _shift_scratch(tm, aw)],
        sem=("parallel",))


def _accumulate(ref, val, first):
    @pl.when(first)
    def _():
        ref[...] = val

    @pl.when(jnp.logical_not(first))
    def _():
        ref[...] += val


def ab_mid_backward(dy, z, gc, sp, name, comm=None):
    s = z.shape[0]
    aw = z.shape[1] // 4
    tm = _tile(s, 512)
    nch, nt = tm // CHUNK, s // tm

    def body(dya_ref, dyb_ref, zu_ref, zv_ref, gc_ref, w_ref, wt_ref, bias_ref, vg_ref, vb_ref, cg_ref, cbn_ref,
             dz_ref, dgc_ref, dw_ref, dsb_ref, dvg_ref, dvb_ref, dcg_ref, dcbn_ref, dcb_ref,
             vl_ref, sv_ref, dsv_ref, dvl_ref):
        first = pl.program_id(0) == 0
        zu, zv = zu_ref[...], zv_ref[...]
        u = _gelu(zu)
        vl, vn, vrstd = _layernorm(_gelu(zv), vg_ref[...], vb_ref[...])
        vl_ref[...] = vl
        _spatial_apply(vl_ref, w_ref, sv_ref, nch, bias_ref)
        dya = dya_ref[...]
        dz_ref[:, :aw] = (dya * sv_ref[...] * _gelu_grad(zu)).astype(BF16)
        dsv = dya * u
        dsv_ref[...] = dsv
        _spatial_apply(dsv_ref, wt_ref, dvl_ref, nch)

        for jp in range(4):
            dlo, dhi = _group_halves(dsv_ref, jp, nch)
            vlo, vhi = _group_halves(vl_ref, jp, nch)
            vall = vlo + vhi
            _accumulate(dw_ref.at[2 * jp], _dot_nt(dlo, vall), first)
            _accumulate(dw_ref.at[2 * jp + 1], _dot_nt(dhi, vall), first)
        rows = dsv[0:CHUNK]
        for c in range(1, nch):
            rows = rows + dsv[c * CHUNK:(c + 1) * CHUNK]
        grp = lax.broadcasted_iota(jnp.int32, (8, aw), 0) == lax.broadcasted_iota(jnp.int32, (8, aw), 1) // HEAD_DIM
        e = jnp.where(grp, 1.0, 0.0).astype(BF16)
        hi = rows.astype(BF16)
        r1 = rows - hi.astype(F32)
        mid = r1.astype(BF16)
        lo = (r1 - mid.astype(F32)).astype(BF16)
        _accumulate(dsb_ref, _dot_nt(e, hi) + _dot_nt(e, mid) + _dot_nt(e, lo), first)

        dvl = dvl_ref[...]
        _accumulate(dvg_ref, jnp.sum(dvl * vn, axis=0, keepdims=True), first)
        _accumulate(dvb_ref, jnp.sum(dvl, axis=0, keepdims=True), first)
        dz_ref[:, aw:] = (_layernorm_backward(dvl, vn, vrstd, vg_ref[...]) * _gelu_grad(zv)).astype(BF16)

        yl, yn, yrstd = _layernorm(gc_ref[...], cg_ref[...], cbn_ref[...])
        sg = _sigmoid(yl)
        dyl = dyb_ref[...] * (sg + yl * sg * (1.0 - sg))
        _accumulate(dcg_ref, jnp.sum(dyl * yn, axis=0, keepdims=True), first)
        _accumulate(dcbn_ref, jnp.sum(dyl, axis=0, keepdims=True), first)
        dgc = _layernorm_backward(dyl, yn, yrstd, cg_ref[...])
        dgc_ref[...] = dgc
        _accumulate(dcb_ref, jnp.sum(dgc, axis=0, keepdims=True), first)

    vec = jax.ShapeDtypeStruct((1, aw), F32)
    return _pallas(
        body, name=name, grid=(nt,), comm=comm,
        args=[dy, dy, z, z, gc, sp["w"], sp["wt"], sp["bias"], sp["vg"], sp["vb"], sp["cg"], sp["cbn"]],
        in_specs=[pl.BlockSpec((tm, aw), lambda i: (i, 0)), pl.BlockSpec((tm, aw), lambda i: (i, 1)),
                  pl.BlockSpec((tm, aw), lambda i: (i, 0)), pl.BlockSpec((tm, aw), lambda i: (i, 1)),
                  pl.BlockSpec((tm, aw), lambda i: (i, 0)),
                  _const_spec(sp["w"].shape), _const_spec(sp["w"].shape), _const_spec(sp["bias"].shape)]
        + [_const_spec((1, aw))] * 4,
        out_specs=[pl.BlockSpec((tm, 2 * aw), lambda i: (i, 0)), pl.BlockSpec((tm, aw), lambda i: (i, 0)),
                   _const_spec(sp["w"].shape), _const_spec((8, CHUNK))] + [_const_spec((1, aw))] * 5,
        out_shape=[jax.ShapeDtypeStruct((s, 4 * aw), BF16), jax.ShapeDtypeStruct((s, aw), F32),
                   jax.ShapeDtypeStruct(sp["w"].shape, F32), jax.ShapeDtypeStruct((8, CHUNK), F32)] + [vec] * 5,
        scratch=[pltpu.VMEM((tm, aw), F32)] * 4,
        sem=("arbitrary",))


def conv_backward(dgc, z, dz_in, sp, name, comm=None):
    s = z.shape[0]
    aw = z.shape[1] // 4
    tm = _tile(s, 512)
    nt = s // tm

    def body(d_ref, dp_ref, dn_ref, zb_ref, cw_ref, dzin_ref, dz_ref, dcw_ref, padd_ref, g_ref, dgg_ref, shd_ref):
        i = pl.program_id(0)
        _fill_padded(padd_ref, dp_ref[...], d_ref[...], dn_ref[...], i, nt, tm)
        _fill_shifts(shd_ref, padd_ref)
        g_ref[...] = _glu(zb_ref[...])

        @pl.when(i == 0)
        def _():
            dcw_ref[...] = jnp.zeros_like(dcw_ref)

        def grad_input(cols, rb):
            acc = jnp.zeros((CHUNK, LANES), F32)
            for k in range(CONV_W):
                acc = acc + cw_ref[k:k + 1, cols] * _tap(padd_ref, shd_ref, rb * CHUNK + CONV_HALO + CONV_W // 2 - k, cols)
            dgg_ref[rb * CHUNK:(rb + 1) * CHUNK, cols] = acc

        def grad_taps(cols, rb):
            gblk = g_ref[rb * CHUNK:(rb + 1) * CHUNK, cols]
            for k in range(CONV_W):
                prod = gblk * _tap(padd_ref, shd_ref, rb * CHUNK + CONV_HALO + CONV_W // 2 - k, cols)
                dcw_ref[k:k + 1, cols] += jnp.sum(prod, axis=0, keepdims=True)

        for lb in range(aw // LANES):
            for rb in range(tm // CHUNK):
                pl.when(i >= 0)(functools.partial(grad_input, pl.ds(lb * LANES, LANES), rb))
                pl.when(i >= 0)(functools.partial(grad_taps, pl.ds(lb * LANES, LANES), rb))

        zb = zb_ref[...]
        val, sg = zb[:, :aw], _sigmoid(zb[:, aw:])
        dgg = dgg_ref[...]
        dz_ref[:, :aw] = (dgg * sg).astype(BF16)
        dz_ref[:, aw:] = (dgg * val * sg * (1.0 - sg)).astype(BF16)

    return _pallas(
        body, name=name, grid=(nt,), args=[dgc, dgc, dgc, z, sp["cw"], dz_in], comm=comm,
        in_specs=_halo_specs(tm, s, aw, 0) + [pl.BlockSpec((tm, 2 * aw), lambda i: (i, 1))]
        + [_const_spec(sp["cw"].shape), pl.BlockSpec(memory_space=pl.ANY)],
        out_specs=[pl.BlockSpec((tm, 2 * aw), lambda i: (i, 1)), _const_spec(sp["cw"].shape)],
        out_shape=[jax.ShapeDtypeStruct((s, 4 * aw), BF16), jax.ShapeDtypeStruct(sp["cw"].shape, F32)],
        scratch=[pltpu.VMEM((tm + 2 * CONV_HALO, aw), F32), pltpu.VMEM((tm, aw), F32), pltpu.VMEM((tm, aw), F32),
                 _shift_scratch(tm, aw)],
        aliases={5: 0}, sem=("arbitrary",))


def rope_tables(s):
    pos = jnp.arange(s, dtype=F32)
    inv_freq = ROPE_THETA ** (-jnp.arange(0, ROT_DIM, 2, dtype=F32) / ROT_DIM)
    ang = pos[:, None] * inv_freq[None, :]
    cos, sin = jnp.cos(ang), jnp.sin(ang)
    half = ROT_DIM // 2
    rest = HEAD_DIM - ROT_DIM
    one, zero, zrest = jnp.ones((s, rest), F32), jnp.zeros((s, half), F32), jnp.zeros((s, rest), F32)
    c = jnp.concatenate([cos, cos, one], axis=1)
    s1 = jnp.concatenate([-sin, zero, zrest], axis=1)
    s2 = jnp.concatenate([zero, sin, zrest], axis=1)
    return tuple(jnp.tile(t, (1, LANES // HEAD_DIM)) for t in (c, s1, s2))


def qk_prep_forward(qkv, tabs, gains, name, comm=None):
    s, w3 = qkv.shape
    w = w3 // 3
    tm = _tile(s, 512)

    def body(x_ref, c_ref, s1_ref, s2_ref, g_ref, o_ref):
        bd = _block_diag(LANES)
        for rb in range(tm // CHUNK):
            rows = pl.ds(rb * CHUNK, CHUNK)
            c, s1, s2 = c_ref[rows, :], s1_ref[rows, :], s2_ref[rows, :]
            for b in range(w // LANES):
                cols = pl.ds(b * LANES, LANES)
                t = x_ref[rows, cols]
                r = lax.rsqrt(_seg_sum(t * t, bd) * (1.0 / HEAD_DIM) + EPS)
                y = t * r * g_ref[...]
                o_ref[rows, cols] = (y * c + pltpu.roll(y, LANES - ROT_DIM // 2, 1) * s1
                                     + pltpu.roll(y, ROT_DIM // 2, 1) * s2)

    tab = pl.BlockSpec((tm, LANES), lambda i, p: (i, 0))
    outs, couts = _pallas(
        body, name=name, grid=(s // tm, 2), args=[qkv, *tabs, gains], comm=comm,
        in_specs=[pl.BlockSpec((tm, w), lambda i, p: (i, p)), tab, tab, tab,
                  pl.BlockSpec((None, 1, LANES), lambda i, p: (p, 0, 0))],
        out_specs=[pl.BlockSpec((tm, w), lambda i, p: (i, p))],
        out_shape=[jax.ShapeDtypeStruct((s, 2 * w), F32)],
        sem=("parallel", "arbitrary"))
    return outs[0], couts


def qk_prep_backward(dq, dq_prev, dq_next, dk, dv, qkv, tabs, gains, name):
    s, w3 = qkv.shape
    w = w3 // 3
    tm = _tile(s, 512)
    t = _tile(s, ATT_TILE)
    per_tile, per_halo, tiles = t // tm, ATT_HALO // tm, s // t

    def neighbours(i):
        tile, c = i // per_tile, i % per_tile
        from_before = (c < per_halo) & (tile >= 1)
        from_after = (c >= per_tile - per_halo) & (tile + 1 < tiles)
        return (from_before, from_after, jnp.where(from_before, (tile - 1) * per_halo + c, 0),
                jnp.where(from_after, (tile + 1) * per_halo + c - (per_tile - per_halo), 0))

    def body(*refs):
        grads = ((refs[0], refs[1], refs[2]), (refs[3],), (refs[4],))
        x_ref, c_ref, s1_ref, s2_ref, g_ref, o_ref, dg_ref = refs[5:]
        part, first = pl.program_id(0), pl.program_id(1) == 0
        from_before, from_after, _, _ = neighbours(pl.program_id(1))

        def normed(ds):
            bd = _block_diag(LANES)
            acc = jnp.zeros((1, LANES), F32)
            for rb in range(tm // CHUNK):
                rows = pl.ds(rb * CHUNK, CHUNK)
                c, s1, s2 = c_ref[rows, :], s1_ref[rows, :], s2_ref[rows, :]
                for b in range(w // LANES):
                    cols = pl.ds(b * LANES, LANES)
                    dout = ds[0][rows, cols]
                    if len(ds) == 3:
                        dout = (dout + jnp.where(from_after, ds[1][rows, cols], 0.0)
                                + jnp.where(from_before, ds[2][rows, cols], 0.0))
                    dy = (dout * c + pltpu.roll(dout * s1, ROT_DIM // 2, 1)
                          + pltpu.roll(dout * s2, LANES - ROT_DIM // 2, 1))
                    t = x_ref[rows, cols]
                    r = lax.rsqrt(_seg_sum(t * t, bd) * (1.0 / HEAD_DIM) + EPS)
                    xh = t * r
                    acc = acc + jnp.sum(dy * xh, axis=0, keepdims=True)
                    tt = dy * g_ref[...]
                    o_ref[rows, cols] = (r * (tt - xh * (_seg_sum(tt * xh, bd) * (1.0 / HEAD_DIM)))).astype(BF16)
            _accumulate(dg_ref, acc, first)

        for p in range(2):
            pl.when(part == p)(functools.partial(normed, grads[p]))

        @pl.when(part == 2)
        def _():
            o_ref[...] = grads[2][0][...].astype(BF16)
            _accumulate(dg_ref, jnp.zeros((1, LANES), F32), first)

    def gspec(p):
        return pl.BlockSpec((tm, w), lambda q, i: (jnp.where(q == p, i, 0), 0))

    prev_spec = pl.BlockSpec((tm, w), lambda q, i: (jnp.where(q == 0, neighbours(i)[3], 0), 0))
    next_spec = pl.BlockSpec((tm, w), lambda q, i: (jnp.where(q == 0, neighbours(i)[2], 0), 0))
    tab = pl.BlockSpec((tm, LANES), lambda q, i: (i, 0))
    return pl.pallas_call(
        body, name=name, grid=(3, s // tm),
        in_specs=[gspec(0), prev_spec, next_spec, gspec(1), gspec(2)]
        + [pl.BlockSpec((tm, w), lambda q, i: (i, q)), tab, tab, tab,
           pl.BlockSpec((None, 1, LANES), lambda q, i: (q, 0, 0))],
        out_specs=[pl.BlockSpec((tm, w), lambda q, i: (i, q)), pl.BlockSpec((None, 1, LANES), lambda q, i: (q, 0, 0))],
        out_shape=[jax.ShapeDtypeStruct((s, w3), BF16), jax.ShapeDtypeStruct((3, 1, LANES), F32)],
        compiler_params=_params("arbitrary", "arbitrary"),
    )(dq, dq_prev, dq_next, dk, dv, qkv, *tabs, gains)


ATT_TILE = 2048
ATT_HALO = BAND * max(DILATIONS)
ROWS_PER_COPY = 256


def _att_specs(s, t, col_fn, halo=True):
    hb, nhb = t // ATT_HALO, s // ATT_HALO
    specs = [pl.BlockSpec((t, LANES), lambda hp, i: (i, col_fn(hp)))]
    if halo:
        specs += [pl.BlockSpec((ATT_HALO, LANES), lambda hp, i: (jnp.maximum(i * hb - 1, 0), col_fn(hp))),
                  pl.BlockSpec((ATT_HALO, LANES), lambda hp, i: (jnp.minimum((i + 1) * hb, nhb - 1), col_fn(hp)))]
    return specs


def _gather_rows(dst_ref, dst_row, src_ref, start, count, stride, scale=None):
    for c in range(0, count, ROWS_PER_COPY):
        m = min(ROWS_PER_COPY, count - c)
        v = src_ref[pl.ds(start + c * stride, m, stride=stride), :]
        if scale is not None:
            v = v * scale
        dst_ref[dst_row + c:dst_row + c + m, :] = v.astype(dst_ref.dtype)


SPLIT = 4


def _split_rows(tmp_ref, base, src_ref, rows):
    part = rows // SPLIT
    for b in range(SPLIT):
        _gather_rows(tmp_ref, base + b * part, src_ref, b, part, SPLIT)


def _stage(dst_ref, cur_ref, d, t, scale=None, tmp_ref=None):
    n = t // d
    if tmp_ref is None or d != SPLIT * SPLIT:
        for r in range(d):
            _gather_rows(dst_ref, r * n, cur_ref, r, n, d, scale)
        return
    _split_rows(tmp_ref, 0, cur_ref, t)
    for r in range(d):
        _gather_rows(dst_ref, r * n, tmp_ref, (r % SPLIT) * (t // SPLIT) + r // SPLIT, n, SPLIT, scale)


def _stage_window(dst_ref, refs, d, t, scale=None, edges=None, tmp_ref=None):
    cur_ref, prev_ref, next_ref = refs
    n = t // d
    nw = n + 2 * BAND
    if tmp_ref is None or d != SPLIT * SPLIT:
        for r in range(d):
            _gather_rows(dst_ref, r * nw, prev_ref, ATT_HALO - BAND * d + r, BAND, d, scale)
            _gather_rows(dst_ref, r * nw + BAND, cur_ref, r, n, d, scale)
            _gather_rows(dst_ref, r * nw + BAND + n, next_ref, r, BAND, d, scale)
    else:
        assert ATT_HALO == BAND * d
        _split_rows(tmp_ref, 0, cur_ref, t)
        _split_rows(tmp_ref, t, prev_ref, ATT_HALO)
        _split_rows(tmp_ref, t + ATT_HALO, next_ref, ATT_HALO)
        for r in range(d):
            a, b = r // SPLIT, r % SPLIT
            _gather_rows(dst_ref, r * nw, tmp_ref, t + b * (ATT_HALO // SPLIT) + a, BAND, SPLIT, scale)
            _gather_rows(dst_ref, r * nw + BAND, tmp_ref, b * (t // SPLIT) + a, n, SPLIT, scale)
            _gather_rows(dst_ref, r * nw + BAND + n, tmp_ref, t + ATT_HALO + b * (ATT_HALO // SPLIT) + a, BAND, SPLIT, scale)
    if edges is not None:
        first, last, value = edges
        fill = jnp.full((BAND, LANES), value, dst_ref.dtype)

        @pl.when(first)
        def _():
            for r in range(d):
                dst_ref[r * nw:r * nw + BAND, :] = fill

        @pl.when(last)
        def _():
            for r in range(d):
                dst_ref[r * nw + BAND + n:(r + 1) * nw, :] = fill


def _band_bias(rows, cols, centre_axis):
    shape = (rows // 2, cols)
    ctr = lax.broadcasted_iota(jnp.int32, shape, centre_axis)
    win = lax.broadcasted_iota(jnp.int32, shape, 1 - centre_axis)
    bias = jnp.where(jnp.abs(win - BAND - ctr) <= BAND, 0.0, NEG).astype(F32)
    return jnp.concatenate([bias, bias], axis=0)


def _window_bias(first_row, length):
    row = first_row + lax.broadcasted_iota(jnp.int32, (1, 2 * CHUNK), 1)
    return jnp.where((row >= 0) & (row < length), 0.0, NEG).astype(F32)


def _scatter_rows(dst_ref, src_ref, d, t, combine):
    n = t // d
    for r in range(d):
        for c in range(0, n, ROWS_PER_COPY):
            m = min(ROWS_PER_COPY, n - c)
            idx = pl.ds(r + c * d, m, stride=d)
            combine(idx, slice(r * n + c, r * n + c + m))


def _unit_rows(u, n):
    upr = n // CHUNK
    r = u // upr
    b = u - r * upr
    if isinstance(u, int):
        return u * CHUNK, (u + r) * CHUNK, b * CHUNK - BAND
    return pl.multiple_of(u * CHUNK, CHUNK), pl.multiple_of((u + r) * CHUNK, CHUNK), b * CHUNK - BAND


def _two_heads(x):
    head0 = lax.broadcasted_iota(jnp.int32, x.shape, 1) < HEAD_DIM
    zero = jnp.zeros_like(x)
    return jnp.concatenate([jnp.where(head0, x, zero), jnp.where(head0, zero, x)], axis=0)


def _merge_heads(x2):
    rows = x2.shape[0] // 2
    head0 = lax.broadcasted_iota(jnp.int32, (rows, LANES), 1) < HEAD_DIM
    return jnp.where(head0, jnp.broadcast_to(x2[:rows], (rows, LANES)), jnp.broadcast_to(x2[rows:], (rows, LANES)))


def _col(part):
    return lambda hp: part * 8 + hp


def attention_forward(qk, qkv, name, comm=None):
    s, w2 = qk.shape
    w = w2 // 2
    t = _tile(s, ATT_TILE)
    scale = HEAD_DIM ** -0.5

    def body(q_ref, k_ref, kp_ref, kn_ref, v_ref, vp_ref, vn_ref, o_ref, lse_ref,
             qs_ref, ks_ref, vs_ref, os_ref, ls_ref, or_ref, tmp_ref):
        i = pl.program_id(1)
        band = _band_bias(2 * CHUNK, 2 * CHUNK, 0)
        for pi, d in enumerate(DILATIONS):
            n = t // d
            _stage(qs_ref, q_ref, d, t, scale, tmp_ref=tmp_ref)
            _stage_window(ks_ref, (k_ref, kp_ref, kn_ref), d, t, tmp_ref=tmp_ref)
            _stage_window(vs_ref, (v_ref, vp_ref, vn_ref), d, t, tmp_ref=tmp_ref)

            for u in range(t // CHUNK):
                qrow, wrow, first = _unit_rows(u, n)
                kb, vb = ks_ref[pl.ds(wrow, 2 * CHUNK), :], vs_ref[pl.ds(wrow, 2 * CHUNK), :]
                sc = _dot_nt(_two_heads(qs_ref[pl.ds(qrow, CHUNK), :]), kb) + band
                if first < 0 or first + 2 * CHUNK > n:
                    sc = sc + _window_bias(i * n + first, s // d)
                m = jnp.max(sc, axis=1, keepdims=True)
                p = jnp.exp(sc - m)
                den = jnp.sum(p, axis=1, keepdims=True)
                os_ref[pl.ds(qrow, CHUNK), :] = _merge_heads(_dot(p.astype(BF16), vb) / den)
                ls_ref[pl.ds(qrow, CHUNK), :] = _merge_heads(m + jnp.log(den))

            if pi == 0:
                def assign(idx, rows):
                    or_ref[idx, :] = os_ref[rows, :]
                    lse_ref[idx, :] = ls_ref[rows, :]
                _scatter_rows(None, None, d, t, assign)
            else:
                def merge(idx, rows):
                    la, lb = lse_ref[idx, :], ls_ref[rows, :]
                    mx = jnp.maximum(la, lb)
                    wa, wb = jnp.exp(la - mx), jnp.exp(lb - mx)
                    den = wa + wb
                    or_ref[idx, :] = (wa * or_ref[idx, :] + wb * os_ref[rows, :]) / den
                    lse_ref[idx, :] = mx + jnp.log(den)
                _scatter_rows(None, None, d, t, merge)
        o_ref[...] = or_ref[...].astype(BF16)

    ospec = pl.BlockSpec((t, LANES), lambda hp, i: (i, hp))
    win_rows = t + 2 * ATT_HALO
    return _pallas(
        body, name=name, grid=(w // LANES, s // t), args=[qk, qk, qk, qk, qkv, qkv, qkv], comm=comm,
        in_specs=_att_specs(s, t, _col(0), halo=False) + _att_specs(s, t, _col(1)) + _att_specs(s, t, _col(2)),
        out_specs=[ospec, ospec],
        out_shape=[jax.ShapeDtypeStruct((s, w), BF16), jax.ShapeDtypeStruct((s, w), F32)],
        scratch=[pltpu.VMEM((t, LANES), BF16), pltpu.VMEM((win_rows, LANES), BF16), pltpu.VMEM((win_rows, LANES), BF16),
                 pltpu.VMEM((t, LANES), F32), pltpu.VMEM((t, LANES), F32), pltpu.VMEM((t, LANES), F32),
                 pltpu.VMEM((win_rows, LANES), F32)],
        sem=("parallel", "parallel"))


def attention_delta(do, o, lse, name):
    s, w = do.shape
    tm = _tile(s, 512)

    def body(do_ref, o_ref, lse_ref, st_ref):
        bd = _block_diag(LANES)
        lane = lax.broadcasted_iota(jnp.int32, (CHUNK, LANES), 1)
        for rb in range(tm // CHUNK):
            rows = pl.ds(rb * CHUNK, CHUNK)
            for b in range(w // LANES):
                cols = pl.ds(b * LANES, LANES)
                dl = _seg_sum(do_ref[rows, cols] * o_ref[rows, cols].astype(F32), bd)
                ls = lse_ref[rows, cols]
                st_ref[rows, cols] = jnp.where(lane % HEAD_DIM < HEAD_DIM // 2, ls, dl)

    spec = pl.BlockSpec((tm, w), lambda i: (i, 0))
    return pl.pallas_call(
        body, name=name, grid=(s // tm,), in_specs=[spec, spec, spec], out_specs=spec,
        out_shape=jax.ShapeDtypeStruct((s, w), F32), compiler_params=_params("parallel"),
    )(do, o, lse)


def attention_backward(qk, qkv, do, stats, name, comm=None):
    s, w2 = qk.shape
    w = w2 // 2
    t = _tile(s, ATT_TILE)
    tiles = s // t
    scale = HEAD_DIM ** -0.5

    def body(k_ref, v_ref, q_ref, qp_ref, qn_ref, do_ref, dop_ref, don_ref, st_ref, stp_ref, stn_ref,
             dk_ref, dv_ref, dq_ref, dqp_ref, dqn_ref, ks_ref, vs_ref, qs_ref, dos_ref, sts_ref, dks_ref, dvs_ref, dqw_ref):
        i = pl.program_id(1)
        key = lax.broadcasted_iota(jnp.int32, (CHUNK, 2 * CHUNK), 0)
        win_ = lax.broadcasted_iota(jnp.int32, (CHUNK, 2 * CHUNK), 1)
        half = jnp.where(jnp.abs(win_ - BAND - key) <= BAND, 0.0, NEG).astype(F32)
        band = jnp.concatenate([half, half], axis=1)
        edges = (i == 0, i == tiles - 1, -NEG)
        dqp_ref[...] = jnp.zeros_like(dqp_ref)
        dqn_ref[...] = jnp.zeros_like(dqn_ref)
        for pi, d in enumerate(DILATIONS):
            n = t // d
            nw = n + 2 * BAND
            _stage(ks_ref, k_ref, d, t, tmp_ref=dqw_ref)
            _stage(vs_ref, v_ref, d, t, tmp_ref=dqw_ref)
            _stage_window(qs_ref, (q_ref, qp_ref, qn_ref), d, t, scale, tmp_ref=dqw_ref)
            _stage_window(dos_ref, (do_ref, dop_ref, don_ref), d, t, tmp_ref=dqw_ref)
            _stage_window(sts_ref, (st_ref, stp_ref, stn_ref), d, t, edges=edges, tmp_ref=dqw_ref)

            for u in range(t // CHUNK):
                krow, wrow, first = _unit_rows(u, n)
                rows, win = pl.ds(krow, CHUNK), pl.ds(wrow, 2 * CHUNK)
                kb = ks_ref[rows, :]
                q2, do2 = _two_heads(qs_ref[win, :]), _two_heads(dos_ref[win, :])
                st = jnp.transpose(sts_ref[win, :])
                mid = HEAD_DIM // 2
                lse2 = jnp.concatenate([st[0:1, :], st[HEAD_DIM:HEAD_DIM + 1, :]], axis=1)
                dl2 = jnp.concatenate([st[mid:mid + 1, :], st[HEAD_DIM + mid:HEAD_DIM + mid + 1, :]], axis=1)
                p = jnp.exp(_dot_nt(kb, q2) + band - lse2)
                ds = (p * (_dot_nt(vs_ref[rows, :], do2) - dl2)).astype(BF16)
                dvs_ref[rows, :] = _dot(p.astype(BF16), do2)
                dks_ref[rows, :] = _dot(ds, q2)
                k2 = _two_heads(kb)
                dqw = (_dot_tn(ds[:, :2 * CHUNK], k2[:CHUNK]) + _dot_tn(ds[:, 2 * CHUNK:], k2[CHUNK:])) * scale
                if first < 0:
                    dqw_ref[pl.ds(wrow, CHUNK), :] = dqw[:CHUNK]
                else:
                    dqw_ref[pl.ds(wrow, CHUNK), :] += dqw[:CHUNK]
                dqw_ref[pl.ds(wrow + CHUNK, CHUNK), :] = dqw[CHUNK:]

            def add(idx, rows, pi=pi):
                dk_ref[idx, :] = dks_ref[rows, :] if pi == 0 else dk_ref[idx, :] + dks_ref[rows, :]
                dv_ref[idx, :] = dvs_ref[rows, :] if pi == 0 else dv_ref[idx, :] + dvs_ref[rows, :]
            _scatter_rows(None, None, d, t, add)

            for r in range(d):
                before = pl.ds(ATT_HALO - BAND * d + r, BAND, stride=d)
                after = pl.ds(r, BAND, stride=d)
                dqp_ref[before, :] += dqw_ref[r * nw:r * nw + BAND, :]
                dqn_ref[after, :] += dqw_ref[r * nw + BAND + n:(r + 1) * nw, :]
                for c in range(0, n, ROWS_PER_COPY):
                    m = min(ROWS_PER_COPY, n - c)
                    idx = pl.ds(r + c * d, m, stride=d)
                    val = dqw_ref[r * nw + BAND + c:r * nw + BAND + c + m, :]
                    dq_ref[idx, :] = val if pi == 0 else dq_ref[idx, :] + val

    ident = lambda hp: hp
    ospec = pl.BlockSpec((t, LANES), lambda hp, i: (i, hp))
    hspec = pl.BlockSpec((ATT_HALO, LANES), lambda hp, i: (i, hp))
    win_rows = t + 2 * ATT_HALO
    halo = jax.ShapeDtypeStruct((tiles * ATT_HALO, w), F32)
    return _pallas(
        body, name=name, grid=(w // LANES, tiles), comm=comm,
        args=[qk, qkv, qk, qk, qk, do, do, do, stats, stats, stats],
        in_specs=_att_specs(s, t, _col(1), halo=False) + _att_specs(s, t, _col(2), halo=False)
        + _att_specs(s, t, _col(0)) + _att_specs(s, t, ident) * 2,
        out_specs=[ospec, ospec, ospec, hspec, hspec],
        out_shape=[jax.ShapeDtypeStruct((s, w), F32)] * 3 + [halo, halo],
        scratch=[pltpu.VMEM((t, LANES), BF16), pltpu.VMEM((t, LANES), BF16),
                 pltpu.VMEM((win_rows, LANES), BF16), pltpu.VMEM((win_rows, LANES), BF16),
                 pltpu.VMEM((win_rows, LANES), F32),
                 pltpu.VMEM((t, LANES), F32), pltpu.VMEM((t, LANES), F32), pltpu.VMEM((win_rows, LANES), F32)],
        sem=("parallel", "parallel"))


def adamw_update(recvs, w, m, v, name, comm=None):
    nl = len(recvs)
    r, c = recvs[0].shape[1:]
    tr = 256 if (r > 256 and r % 256 == 0) else r
    nt = r // tr
    c1 = 1.0 - ADAM_B1 ** ADAM_STEP
    c2 = 1.0 - ADAM_B2 ** ADAM_STEP

    def body(*refs):
        g_refs = refs[:nl]
        w_ref, m_ref, v_ref, go_ref, d_ref, mo_ref, vo_ref = refs[nl:]

        def update(g_ref):
            g = g_ref[0].astype(F32)
            for j in range(1, N_DEV):
                g = g + g_ref[j].astype(F32)
            mn = ADAM_B1 * m_ref[...] + (1.0 - ADAM_B1) * g
            vn = ADAM_B2 * v_ref[...] + (1.0 - ADAM_B2) * (g * g)
            go_ref[...] = g
            mo_ref[...] = mn
            vo_ref[...] = vn
            d_ref[...] = -ADAM_LR * ((mn / c1) / (jnp.sqrt(vn / c2) + ADAM_EPS) + ADAM_WD * w_ref[...])

        for layer in range(nl):
            pl.when(pl.program_id(0) == layer)(functools.partial(update, g_refs[layer]))

    def gspec(layer):
        return pl.BlockSpec((N_DEV, tr, c), lambda l, i: (0, jnp.where(l == layer, i, 0), 0))

    spec = pl.BlockSpec((tr, c), lambda l, i: (l * nt + i, 0))
    return _pallas(
        body, name=name, grid=(nl, nt), args=[*recvs, w, m, v], comm=comm,
        in_specs=[gspec(layer) for layer in range(nl)] + [spec, spec, spec],
        out_specs=[spec] * 4, out_shape=[jax.ShapeDtypeStruct((nl * r, c), F32)] * 4,
        sem=("arbitrary", "arbitrary"))


BIG = ("mlp_w1", "mlp_w2", "ab_w_in", "ab_w_out", "c_w_qkv", "c_w_out")
SMALL = ("mix_norm_g", "mlp_norm_g", "a_spatial_w", "a_spatial_b", "a_vnorm_g", "a_vnorm_b", "b_conv_b", "b_norm_g",
         "b_norm_b", "c_q_norm_g", "c_k_norm_g")
WEIGHTS = ("mix_norm_g", "mlp_norm_g", "mlp_w1", "mlp_w2", "ab_w_in", "a_spatial_w", "a_spatial_b", "a_vnorm_g",
           "a_vnorm_b", "b_conv_w", "b_conv_b", "b_norm_g", "b_norm_b", "ab_w_out", "c_w_qkv", "c_q_norm_g",
           "c_k_norm_g", "c_w_out")


def _mixer_params(p, conv_full, i):
    aw = p["a_vnorm_g"].shape[1]
    row = lambda t: t[i][None, :]
    return dict(
        w=p["a_spatial_w"][i].astype(BF16), wt=jnp.swapaxes(p["a_spatial_w"][i], 1, 2).astype(BF16),
        bias=jnp.repeat(p["a_spatial_b"][i].T, aw // p["a_spatial_b"].shape[1], axis=1),
        vg=row(p["a_vnorm_g"]), vb=row(p["a_vnorm_b"]), cw=jnp.pad(conv_full[i], ((0, 1), (0, 0))),
        cb=row(p["b_conv_b"]), cg=row(p["b_norm_g"]), cbn=row(p["b_norm_b"]))


def _head_gains(p, i):
    rep = LANES // HEAD_DIM
    return jnp.stack([jnp.tile(p["c_q_norm_g"][i], rep), jnp.tile(p["c_k_norm_g"][i], rep),
                      jnp.ones((LANES,), F32)])[:, None, :]


PACKED = tuple(n for n in SMALL if n != "a_spatial_w")


def _pack(d):
    flat = jnp.concatenate([d[n].reshape(-1) for n in PACKED])
    rows = -(-flat.shape[0] // (8 * LANES)) * 8
    return jnp.pad(flat, (0, rows * LANES - flat.shape[0])).reshape(rows, LANES)


def _unpack(packed, like):
    flat, out, pos = packed.reshape(-1), {}, 0
    for n in PACKED:
        size = math.prod(like[n].shape)
        out[n] = flat[pos:pos + size].reshape(like[n].shape)
        pos += size
    return out


class Traffic:
    def __init__(self, shards, full=()):
        self.shards, self.w, self.queue, self.parts = shards, dict(full), [], {}

    def run(self, fn, *args, gather=(), send=False, **kw):
        operands, flags, dest = [], [], []
        if self.shards is None:
            if send:
                self.parts.update(self.queue)
                self.queue = []
        else:
            for k in gather:
                if k not in self.w:
                    operands.append(self.shards[k])
                    flags.append(("gather", SHARD_AXIS.get(k[0])))
                    dest.append((self.w, k))
            if send:
                for k, t in self.queue:
                    operands.append(t)
                    flags.append(("scatter", SHARD_AXIS.get(k[0])))
                    dest.append((self.parts, k))
                self.queue = []
        outs, couts = fn(*args, comm=PeerCopies(operands, flags) if operands else None, **kw)
        for (table, k), t in zip(dest, couts):
            table[k] = t
        return outs

    def flush(self, name, extra=()):
        self.queue += list(extra)
        self.run(lambda comm: ([], run_copies(comm, name) if comm is not None else []), send=True)


SHARD_AXIS = {"mlp_w1": 1, "mlp_w2": 0, "ab_w_in": 1, "ab_w_out": 0, "c_w_qkv": 1, "c_w_out": 0}


def forward_backward(x, target, p, tr, conv_full):
    s, d = x.shape
    depth = p["mix_norm_g"].shape[0]
    tabs = rope_tables(s)
    saved = []
    for l in range(depth):
        i = l // 2
        mix_g, mlp_g = p["mix_norm_g"][l][None, :], p["mlp_norm_g"][l][None, :]
        st = dict(x_in=x)
        nxt = () if l + 1 == depth else ((("c_w_qkv", i), ("c_w_out", i)) if l % 2 == 0 else
                                        (("ab_w_in", i + 1), ("ab_w_out", i + 1)))
        if l % 2 == 0:
            sp = _mixer_params(p, conv_full, i)
            z, h = tr.run(norm_matmul, x, mix_g, tr.w["ab_w_in", i], f"ab_in_{l}", gather=[("mlp_w1", l)])
            ycat, gc = tr.run(ab_mid_forward, z, sp, f"ab_mid_{l}", gather=[("mlp_w2", l)])
            x = matmul_residual(x, ycat, tr.w["ab_w_out", i], f"ab_out_{l}")
            st.update(z=z, h=h, y=ycat, gc=gc, sp=sp)
        else:
            gains = _head_gains(p, i)
            qkv, h = norm_matmul(x, mix_g, tr.w["c_w_qkv", i], f"c_qkv_{l}")[0]
            qk = qk_prep_forward(qkv, tabs, gains, f"c_prep_{l}")[0]
            ahead = [("mlp_w1", l), ("mlp_w2", l)] + ([("mlp_w1", l + 1)] if l + 1 < depth else [])
            o, lse = tr.run(attention_forward, qk, qkv, f"c_attn_{l}", gather=ahead)
            x = matmul_residual(x, o, tr.w["c_w_out", i], f"c_out_{l}")
            st.update(qkv=qkv, h=h, qk=qk, y=o, lse=lse, gains=gains)
            nxt = nxt + ((("mlp_w2", l + 1),) if l + 1 < depth else ())
        st["x_mid"] = x
        x, a, h2 = tr.run(mlp_forward, x, mlp_g, tr.w["mlp_w1", l], tr.w["mlp_w2", l], f"mlp_{l}", gather=nxt)
        st.update(a=a, h2=h2)
        saved.append(st)

    dy, loss_part = loss_and_grad(x, target, "loss")

    def tn(*a, comm, **kw):
        out, couts = matmul_tn(*a, comm=comm, **kw)
        return [out], couts

    small = {n: [None] * p[n].shape[0] for n in SMALL}
    conv_grads = [None] * p["b_conv_b"].shape[0]
    for l in reversed(range(depth)):
        i, st = l // 2, saved[l]
        mix_g, mlp_g = p["mix_norm_g"][l][None, :], p["mlp_norm_g"][l][None, :]
        w1, w2 = tr.w["mlp_w1", l], tr.w["mlp_w2", l]
        dxm, da, dg, dyb = tr.run(mlp_backward, dy, st["a"], st["x_mid"], mlp_g, w1, w2, f"mlp_bwd_{l}", send=True)
        small["mlp_norm_g"][l] = dg[0]
        tr.queue.append((("mlp_w1", l), matmul_tn(st["h2"], da, f"mlp_dw1_{l}", n_split=2)[0]))
        tr.queue.append((("mlp_w2", l), matmul_tn(st["a"], dyb, f"mlp_dw2_{l}", m_split=4, relu2=True)[0]))
        if l % 2 == 0:
            sp = st["sp"]
            wout = tr.w["ab_w_out", i]
            dycat = matmul_nt(dxm, wout, f"ab_out_bwd_{l}")
            later = [tr.queue.pop(), (("ab_w_out", i), matmul_tn(st["y"], dxm, f"ab_dwout_{l}")[0])]
            dz, dgc, dw, dsb, dvg, dvb, dcg, dcbn, dcb = tr.run(
                ab_mid_backward, dycat, st["z"], st["gc"], sp, f"ab_mid_bwd_{l}", send=True)
            tr.queue += later
            dz, dcw = tr.run(conv_backward, dgc, st["z"], dz, sp, f"ab_conv_bwd_{l}", send=True)
            small["a_spatial_w"][i], small["a_spatial_b"][i] = dw, dsb
            small["a_vnorm_g"][i], small["a_vnorm_b"][i] = dvg[0], dvb[0]
            small["b_norm_g"][i], small["b_norm_b"][i], small["b_conv_b"][i] = dcg[0], dcbn[0], dcb[0]
            conv_grads[i] = dcw[:CONV_W]
            dy, dg = matmul_nt_norm_backward(dz, tr.w["ab_w_in", i], st["x_in"], mix_g, dxm, f"ab_in_bwd_{l}")
            small["mix_norm_g"][l] = dg[0]
            last = []
            if l == 0 and tr.shards is not None:
                tr.shards["small_grads", 0] = _pack({n: jnp.stack(small[n]) for n in PACKED})
                tr.shards["spatial_grads", 0] = jnp.stack(small["a_spatial_w"]).astype(BF16)
                last = [("small_grads", 0), ("spatial_grads", 0)]
            dwin = tr.run(tn, st["h"], dz, f"ab_dwin_{l}", send=True, gather=last)[0]
            tr.queue.append((("ab_w_in", i), dwin))
        else:
            wout = tr.w["c_w_out", i]
            do = matmul_nt(dxm, wout, f"c_out_bwd_{l}")
            tr.queue.append((("c_w_out", i), matmul_tn(st["y"], dxm, f"c_dwout_{l}")[0]))
            stats = attention_delta(do, st["y"], st["lse"], f"c_delta_{l}")
            dk, dv, dq, dqp, dqn = tr.run(attention_backward, st["qk"], st["qkv"], do, stats, f"c_attn_bwd_{l}", send=True)
            dqkv, dgn = qk_prep_backward(dq, dqp, dqn, dk, dv, st["qkv"], tabs, st["gains"], f"c_prep_bwd_{l}")
            small["c_q_norm_g"][i] = dgn[0, 0, :HEAD_DIM] + dgn[0, 0, HEAD_DIM:]
            small["c_k_norm_g"][i] = dgn[1, 0, :HEAD_DIM] + dgn[1, 0, HEAD_DIM:]
            dy, dg = matmul_nt_norm_backward(dqkv, tr.w["c_w_qkv", i], st["x_in"], mix_g, dxm, f"c_qkv_bwd_{l}")
            small["mix_norm_g"][l] = dg[0]
            tr.queue.append((("c_w_qkv", i), matmul_tn(st["h"], dqkv, f"c_dwqkv_{l}", n_split=2)[0]))
    small = {n: jnp.stack(v) for n, v in small.items()}
    return loss_part, dy, small, jnp.stack(conv_grads)


def kernel(x, mix_norm_g, mlp_norm_g, mlp_w1, mlp_w2, ab_w_in, a_spatial_w, a_spatial_b, a_vnorm_g, a_vnorm_b, b_conv_w, b_conv_b, b_norm_g, b_norm_b, ab_w_out, c_w_qkv, c_q_norm_g, c_k_norm_g, c_w_out, loss_target, m_mix_norm_g, m_mlp_norm_g, m_mlp_w1, m_mlp_w2, m_ab_w_in, m_a_spatial_w, m_a_spatial_b, m_a_vnorm_g, m_a_vnorm_b, m_b_conv_w, m_b_conv_b, m_b_norm_g, m_b_norm_b, m_ab_w_out, m_c_w_qkv, m_c_q_norm_g, m_c_k_norm_g, m_c_w_out, v_mix_norm_g, v_mlp_norm_g, v_mlp_w1, v_mlp_w2, v_ab_w_in, v_a_spatial_w, v_a_spatial_b, v_a_vnorm_g, v_a_vnorm_b, v_b_conv_w, v_b_conv_b, v_b_norm_g, v_b_norm_b, v_ab_w_out, v_c_w_qkv, v_c_q_norm_g, v_c_k_norm_g, v_c_w_out):
    args = dict(locals())
    w = {n: args[n] for n in WEIGHTS}
    m = {n: args["m_" + n] for n in WEIGHTS}
    v = {n: args["v_" + n] for n in WEIGHTS}

    shards = {(n, l): w[n][l].astype(BF16) for n in BIG for l in range(w[n].shape[0])}
    shards["b_conv_w", 0] = w["b_conv_w"]
    tr = Traffic(shards)
    first = [("ab_w_in", 0), ("ab_w_out", 0), ("b_conv_w", 0)]
    tr.run(lambda comm: ([], run_copies(comm, "gather_first")), gather=first)
    conv = tr.w["b_conv_w", 0]
    conv_full = jnp.transpose(conv, (1, 2, 0, 3)).reshape(conv.shape[1], conv.shape[2], -1)

    loss_part, dx, small, conv_grad = forward_backward(x[0], loss_target[0], w, tr, conv_full)
    loss = lax.psum(jnp.sum(loss_part), ("x", "y", "c"))
    nl, kw, cw = conv_grad.shape
    conv_parts = jnp.transpose(conv_grad.reshape(nl, kw, N_DEV, cw // N_DEV), (2, 0, 1, 3))
    tr.queue.append((("b_conv_w", 0), conv_parts))

    grads, deltas, new_m, new_v = {}, {}, {}, {}

    def update(n, recvs):
        shape = w[n].shape
        flat = lambda t: t.reshape(-1, shape[-1])
        recvs = [t.reshape(N_DEV, -1, shape[-1]) for t in recvs]
        outs = tr.run(adamw_update, recvs, flat(w[n]), flat(m[n]), flat(v[n]), "adamw_" + n, send=True)
        grads[n], deltas[n], new_m[n], new_v[n] = (t.reshape(shape) for t in outs)

    for n in BIG:
        update(n, [tr.parts[n, l] for l in range(w[n].shape[0])])
    update("b_conv_w", [tr.parts["b_conv_w", 0]])
    update("a_spatial_w", [tr.w["spatial_grads", 0]])
    outs = adamw_update([tr.w["small_grads", 0]], _pack(w), _pack(m), _pack(v), "adamw_small")[0]
    for dst, t in zip((grads, deltas, new_m, new_v), outs):
        dst.update(_unpack(t, w))

    return (loss, dx[None], *[grads[n] for n in WEIGHTS], *[deltas[n] for n in WEIGHTS],
            *[new_m[n] for n in WEIGHTS], *[new_v[n] for n in WEIGHTS])
```
